```python
import jax, jax.numpy as jnp
from jax import lax
import numpy as np

D_MODEL = 1024
BATCH = 8
SEQ = 8192
DEPTH = 2

POOL_WIDTH = D_MODEL // 2
POOL_WINDOWS = (2, 4, 8, 16)
N_POOL_GROUPS = len(POOL_WINDOWS)
POOL_GROUP = POOL_WIDTH // N_POOL_GROUPS
N_HEADS = 8
HEAD_DIM = 64
ATTN_WIDTH = N_HEADS * HEAD_DIM
Q_BLOCK = 128
IN_WIDTH = 2 * POOL_WIDTH + 4 * ATTN_WIDTH + 2 * D_MODEL
RMS_EPS = 1e-6

kernel_name = "hybrid_pool_stickbreak_gated"


def rms_norm(x, g):
    xf = x.astype(jnp.float32)
    y = xf * lax.rsqrt(jnp.mean(xf * xf, axis=-1, keepdims=True) + RMS_EPS)
    return (y * g.astype(jnp.float32)).astype(x.dtype)


def multiscale_pool(u, w_group, scale):
    B, S, _ = u.shape
    grp = u.astype(jnp.float32).reshape(B, S, N_POOL_GROUPS, POOL_GROUP)
    cs = jnp.cumsum(grp, axis=1)
    pos = jnp.arange(S)
    means = []
    for g, w in enumerate(POOL_WINDOWS):
        c = cs[:, :, g]
        prev = jnp.pad(c, ((0, 0), (w, 0), (0, 0)))[:, :S]
        cnt = jnp.minimum(pos + 1, w).astype(jnp.float32)[None, :, None]
        means.append((c - prev) / cnt)
    pooled = jnp.stack(means, axis=2) - grp
    mixed = jnp.einsum('bsgc,gcd->bsgd', pooled, w_group.astype(jnp.float32))
    return (mixed.reshape(B, S, POOL_WIDTH) * scale.astype(jnp.float32)).astype(u.dtype)


def stick_breaking_attention(q, k, v):
    B, S, H, Dh = q.shape
    n_blocks = S // Q_BLOCK
    qb = q.reshape(B, n_blocks, Q_BLOCK, H, Dh).transpose(1, 0, 2, 3, 4)
    kf = k.astype(jnp.float32)
    vf = v.astype(jnp.float32)
    key_pos = jnp.arange(S)
    inv_sqrt_d = 1.0 / float(np.sqrt(Dh))

    def one_block(args):
        q_blk, blk = args
        logits = jnp.einsum('bqhd,bkhd->bhqk', q_blk.astype(jnp.float32), kf) * inv_sqrt_d
        q_pos = blk * Q_BLOCK + jnp.arange(Q_BLOCK)
        mask = (key_pos[None, :] < q_pos[:, None])[None, None]
        log_beta = jax.nn.log_sigmoid(logits)
        log_1m_beta = jnp.where(mask, jax.nn.log_sigmoid(-logits), 0.0)
        later = lax.cumsum(log_1m_beta, axis=3, reverse=True) - log_1m_beta
        wts = jnp.where(mask, jnp.exp(log_beta + later), 0.0)
        return jnp.einsum('bhqk,bkhd->bqhd', wts, vf)

    out = lax.map(one_block, (qb, jnp.arange(n_blocks)))
    return out.transpose(1, 0, 2, 3, 4).reshape(B, S, H, Dh).astype(q.dtype)


def _fwd_setup_inputs(seed: int = 0) -> dict:
    key = jax.random.key(seed)
    ks = jax.random.split(key, 11)
    f32 = jnp.float32
    x = jax.random.normal(ks[0], (BATCH, SEQ, D_MODEL), f32)
    norm_g = 1.0 + 0.05 * jax.random.normal(ks[1], (DEPTH, D_MODEL), f32)
    w_in = jax.random.normal(ks[2], (DEPTH, D_MODEL, IN_WIDTH), f32) * D_MODEL ** -0.5
    b_gate = 0.01 * jax.random.normal(ks[3], (DEPTH, 2 * D_MODEL), f32)
    pool_w = jax.random.normal(ks[4], (DEPTH, N_POOL_GROUPS, POOL_GROUP, POOL_GROUP), f32) * POOL_GROUP ** -0.5
    pool_scale = 1.0 + 0.1 * jax.random.normal(ks[5], (DEPTH, POOL_WIDTH), f32)
    w_pool_up = jax.random.normal(ks[6], (DEPTH, POOL_WIDTH, D_MODEL), f32) * POOL_WIDTH ** -0.5
    w_attn_up = jax.random.normal(ks[7], (DEPTH, ATTN_WIDTH, D_MODEL), f32) * ATTN_WIDTH ** -0.5
    w_out = jax.random.normal(ks[8], (DEPTH, D_MODEL, D_MODEL), f32) * D_MODEL ** -0.5
    final_g = 1.0 + 0.05 * jax.random.normal(ks[9], (D_MODEL,), f32)
    return {"x": x, "norm_g": norm_g, "w_in": w_in, "b_gate": b_gate, "pool_w": pool_w,
            "pool_scale": pool_scale, "w_pool_up": w_pool_up, "w_attn_up": w_attn_up,
            "w_out": w_out, "final_g": final_g}


def _fwd_reference(x, norm_g, w_in, b_gate, pool_w, pool_scale, w_pool_up, w_attn_up, w_out, final_g):
    B, S, D = x.shape
    splits = np.cumsum([POOL_WIDTH, POOL_WIDTH, ATTN_WIDTH, ATTN_WIDTH, ATTN_WIDTH, ATTN_WIDTH]).tolist()
    for l in range(DEPTH):
        h = rms_norm(x, norm_g[l])
        proj = jnp.einsum('bsd,de->bse', h, w_in[l])
        u_pool, z_pool, q, k, v, z_attn, gate_logits = jnp.split(proj, splits, axis=-1)
        y_pool = multiscale_pool(u_pool, pool_w[l], pool_scale[l]) * jax.nn.silu(z_pool)
        attn = stick_breaking_attention(q.reshape(B, S, N_HEADS, HEAD_DIM),
                                        k.reshape(B, S, N_HEADS, HEAD_DIM),
                                        v.reshape(B, S, N_HEADS, HEAD_DIM))
        y_attn = attn.reshape(B, S, ATTN_WIDTH) * jax.nn.silu(z_attn)
        gates = jax.nn.sigmoid(gate_logits + b_gate[l]).reshape(B, S, 2, D)
        merged = (gates[:, :, 0] * jnp.einsum('bsp,pd->bsd', y_pool, w_pool_up[l])
                  + gates[:, :, 1] * jnp.einsum('bsa,ad->bsd', y_attn, w_attn_up[l]))
        x = x + jnp.einsum('bsd,de->bse', merged, w_out[l])
    return rms_norm(x, final_g)


import jax as _jax
import jax.numpy as _jnp

TWIN_FORMAT = 'train_step'
FWD_PARAMS = ['x', 'norm_g', 'w_in', 'b_gate', 'pool_w', 'pool_scale', 'w_pool_up', 'w_attn_up', 'w_out', 'final_g']
TWIN_WEIGHTS = ['norm_g', 'w_in', 'b_gate', 'pool_w', 'pool_scale', 'w_pool_up', 'w_attn_up', 'w_out', 'final_g']
TWIN_DIFF_INPUT = 'x'
TWIN_INPUTS = ['x', 'norm_g', 'w_in', 'b_gate', 'pool_w', 'pool_scale', 'w_pool_up', 'w_attn_up', 'w_out', 'final_g', 'loss_target', 'm_norm_g', 'm_w_in', 'm_b_gate', 'm_pool_w', 'm_pool_scale', 'm_w_pool_up', 'm_w_attn_up', 'm_w_out', 'm_final_g', 'v_norm_g', 'v_w_in', 'v_b_gate', 'v_pool_w', 'v_pool_scale', 'v_w_pool_up', 'v_w_attn_up', 'v_w_out', 'v_final_g']
TWIN_OUTPUTS = ['loss', 'grad_x', 'grad_norm_g', 'grad_w_in', 'grad_b_gate', 'grad_pool_w', 'grad_pool_scale', 'grad_w_pool_up', 'grad_w_attn_up', 'grad_w_out', 'grad_final_g', 'delta_norm_g', 'delta_w_in', 'delta_b_gate', 'delta_pool_w', 'delta_pool_scale', 'delta_w_pool_up', 'delta_w_attn_up', 'delta_w_out', 'delta_final_g', 'new_m_norm_g', 'new_m_w_in', 'new_m_b_gate', 'new_m_pool_w', 'new_m_pool_scale', 'new_m_w_pool_up', 'new_m_w_attn_up', 'new_m_w_out', 'new_m_final_g', 'new_v_norm_g', 'new_v_w_in', 'new_v_b_gate', 'new_v_pool_w', 'new_v_pool_scale', 'new_v_w_pool_up', 'new_v_w_attn_up', 'new_v_w_out', 'new_v_final_g']
TWIN_LEAF_KINDS = {'loss': 'loss', 'grad_x': 'grad_x', 'grad_norm_g': 'grad_w', 'grad_w_in': 'grad_w', 'grad_b_gate': 'grad_w', 'grad_pool_w': 'grad_w', 'grad_pool_scale': 'grad_w', 'grad_w_pool_up': 'grad_w', 'grad_w_attn_up': 'grad_w', 'grad_w_out': 'grad_w', 'grad_final_g': 'grad_w', 'delta_norm_g': 'delta_w', 'delta_w_in': 'delta_w', 'delta_b_gate': 'delta_w', 'delta_pool_w': 'delta_w', 'delta_pool_scale': 'delta_w', 'delta_w_pool_up': 'delta_w', 'delta_w_attn_up': 'delta_w', 'delta_w_out': 'delta_w', 'delta_final_g': 'delta_w', 'new_m_norm_g': 'new_m', 'new_m_w_in': 'new_m', 'new_m_b_gate': 'new_m', 'new_m_pool_w': 'new_m', 'new_m_pool_scale': 'new_m', 'new_m_w_pool_up': 'new_m', 'new_m_w_attn_up': 'new_m', 'new_m_w_out': 'new_m', 'new_m_final_g': 'new_m', 'new_v_norm_g': 'new_v', 'new_v_w_in': 'new_v', 'new_v_b_gate': 'new_v', 'new_v_pool_w': 'new_v', 'new_v_pool_scale': 'new_v', 'new_v_w_pool_up': 'new_v', 'new_v_w_attn_up': 'new_v', 'new_v_w_out': 'new_v', 'new_v_final_g': 'new_v'}


def _forward(args):
    return _fwd_reference(*[args[k] for k in FWD_PARAMS])


def _output_shape():
    def fwd():
        inp = _fwd_setup_inputs(0)
        return _fwd_reference(*[inp[k] for k in FWD_PARAMS])
    out = _jax.eval_shape(fwd)
    return out.shape, out.dtype

N_MICROBATCH = 1
ADAM_LR = 0.001
ADAM_B1 = 0.9
ADAM_B2 = 0.999
ADAM_EPS = 1e-08
ADAM_WD = 0.01
ADAM_STEP = 10
PER_EXAMPLE_BATCH_AXIS = {'x': 0, 'loss_target': 0}
SHARED_INPUTS = []
_WEIGHT_DTYPES = {'norm_g': _jnp.float32, 'w_in': _jnp.float32, 'b_gate': _jnp.float32, 'pool_w': _jnp.float32, 'pool_scale': _jnp.float32, 'w_pool_up': _jnp.float32, 'w_attn_up': _jnp.float32, 'w_out': _jnp.float32, 'final_g': _jnp.float32}
MOMENT_SCALE = {'norm_g': 1.346567e-01, 'w_in': 5.860207e-02, 'b_gate': 2.346504e-02, 'pool_w': 9.803097e-02, 'pool_scale': 9.988338e-02, 'w_pool_up': 6.939144e-02, 'w_attn_up': 5.035109e-02, 'w_out': 8.657807e-02, 'final_g': 6.407300e+01}


def _to_microbatches(a, axis):
    t = _jnp.moveaxis(a, axis, 0)
    t = t.reshape((N_MICROBATCH, t.shape[0] // N_MICROBATCH) + t.shape[1:])
    return _jnp.moveaxis(t, 1, axis + 1)


def setup_inputs(seed: int = 0) -> dict:
    inp = _fwd_setup_inputs(seed)
    key = _jax.random.fold_in(_jax.random.key(seed), 7919)
    shape, _ = _output_shape()
    out = dict(inp)
    out["loss_target"] = _jax.random.normal(_jax.random.fold_in(key, 0), shape, _jnp.float32)
    for i, name in enumerate(TWIN_WEIGHTS):
        w = inp[name].astype(_jnp.float32)
        if MOMENT_SCALE is None:
            s = _jnp.sqrt(_jnp.mean(_jnp.square(w)) + 1e-30)
        else:
            s = MOMENT_SCALE[name]
        km, kv = _jax.random.split(_jax.random.fold_in(key, i + 1))
        out[name] = w
        out["m_" + name] = s * _jax.random.normal(km, w.shape, _jnp.float32)
        out["v_" + name] = (s * s) * _jax.random.uniform(kv, w.shape, _jnp.float32, 0.5, 1.5)
    if N_MICROBATCH > 1:
        for name, axis in PER_EXAMPLE_BATCH_AXIS.items():
            out[name] = _to_microbatches(out[name], axis)
    return {'x': out['x'], 'norm_g': out['norm_g'], 'w_in': out['w_in'], 'b_gate': out['b_gate'], 'pool_w': out['pool_w'], 'pool_scale': out['pool_scale'], 'w_pool_up': out['w_pool_up'], 'w_attn_up': out['w_attn_up'], 'w_out': out['w_out'], 'final_g': out['final_g'], 'loss_target': out['loss_target'], 'm_norm_g': out['m_norm_g'], 'm_w_in': out['m_w_in'], 'm_b_gate': out['m_b_gate'], 'm_pool_w': out['m_pool_w'], 'm_pool_scale': out['m_pool_scale'], 'm_w_pool_up': out['m_w_pool_up'], 'm_w_attn_up': out['m_w_attn_up'], 'm_w_out': out['m_w_out'], 'm_final_g': out['m_final_g'], 'v_norm_g': out['v_norm_g'], 'v_w_in': out['v_w_in'], 'v_b_gate': out['v_b_gate'], 'v_pool_w': out['v_pool_w'], 'v_pool_scale': out['v_pool_scale'], 'v_w_pool_up': out['v_w_pool_up'], 'v_w_attn_up': out['v_w_attn_up'], 'v_w_out': out['v_w_out'], 'v_final_g': out['v_final_g']}


def _loss(weights, diff, rest, loss_target):
    with _jax.named_scope("forward"):
        args = {**rest, TWIN_DIFF_INPUT: diff, **{k: w.astype(_WEIGHT_DTYPES[k]) for k, w in weights.items()}}
        y = _forward(args)
    with _jax.named_scope("loss_head"):
        err = _jnp.square(y.astype(_jnp.float32) - loss_target)
        return 0.5 * _jnp.sum(_jnp.mean(err, axis=-1)) if err.ndim else 0.5 * err


def _adamw(w, g, m, v):
    m = ADAM_B1 * m + (1.0 - ADAM_B1) * g
    v = ADAM_B2 * v + (1.0 - ADAM_B2) * _jnp.square(g)
    m_hat = m / (1.0 - ADAM_B1 ** ADAM_STEP)
    v_hat = v / (1.0 - ADAM_B2 ** ADAM_STEP)
    delta = -ADAM_LR * (m_hat / (_jnp.sqrt(v_hat) + ADAM_EPS) + ADAM_WD * w)
    return delta, m, v


def reference(x, norm_g, w_in, b_gate, pool_w, pool_scale, w_pool_up, w_attn_up, w_out, final_g, loss_target, m_norm_g, m_w_in, m_b_gate, m_pool_w, m_pool_scale, m_w_pool_up, m_w_attn_up, m_w_out, m_final_g, v_norm_g, v_w_in, v_b_gate, v_pool_w, v_pool_scale, v_w_pool_up, v_w_attn_up, v_w_out, v_final_g):
    given = dict(x=x, norm_g=norm_g, w_in=w_in, b_gate=b_gate, pool_w=pool_w, pool_scale=pool_scale, w_pool_up=w_pool_up, w_attn_up=w_attn_up, w_out=w_out, final_g=final_g, loss_target=loss_target, m_norm_g=m_norm_g, m_w_in=m_w_in, m_b_gate=m_b_gate, m_pool_w=m_pool_w, m_pool_scale=m_pool_scale, m_w_pool_up=m_w_pool_up, m_w_attn_up=m_w_attn_up, m_w_out=m_w_out, m_final_g=m_final_g, v_norm_g=v_norm_g, v_w_in=v_w_in, v_b_gate=v_b_gate, v_pool_w=v_pool_w, v_pool_scale=v_pool_scale, v_w_pool_up=v_w_pool_up, v_w_attn_up=v_w_attn_up, v_w_out=v_w_out, v_final_g=v_final_g)
    weights = {n: given[n] for n in TWIN_WEIGHTS}
    shared = {n: given[n] for n in SHARED_INPUTS}
    per_example = {n: given[n] for n in ['x']}
    grad_fn = _jax.value_and_grad(_loss, argnums=(0, 1))

    def one_microbatch(ex, loss_target):
        ex = dict(ex)
        diff = ex.pop(TWIN_DIFF_INPUT)
        return grad_fn(weights, diff, {**shared, **ex}, loss_target)

    if N_MICROBATCH == 1:
        loss, (grad_w, grad_x) = one_microbatch(per_example, given["loss_target"])
    else:
        def body(carry, xs):
            loss_sum, grad_sum = carry
            l_k, (gw_k, gx_k) = one_microbatch(xs[0], xs[1])
            with _jax.named_scope("update"):
                return (loss_sum + l_k, _jax.tree.map(_jnp.add, grad_sum, gw_k)), gx_k

        init = (_jnp.zeros((), _jnp.float32), _jax.tree.map(_jnp.zeros_like, weights))
        (loss, grad_w), grad_x = _jax.lax.scan(body, init, (per_example, given["loss_target"]))
    with _jax.named_scope("update"):
        delta_w, new_m, new_v = {}, {}, {}
        for n in TWIN_WEIGHTS:
            delta_w[n], new_m[n], new_v[n] = _adamw(weights[n], grad_w[n], given["m_" + n], given["v_" + n])
    return (loss, grad_x, *[grad_w[n] for n in TWIN_WEIGHTS], *[delta_w[n] for n in TWIN_WEIGHTS],
            *[new_m[n] for n in TWIN_WEIGHTS], *[new_v[n] for n in TWIN_WEIGHTS])
```

```python
import functools

import jax
import jax.numpy as jnp
from jax import lax
from jax.experimental import pallas as pl
from jax.experimental.pallas import tpu as pltpu

F32 = jnp.float32
BF16 = jnp.bfloat16
MESH = pl.DeviceIdType.MESH

D_MODEL = 1024
POOL_WIDTH = 512
POOL_WINDOWS = (2, 4, 8, 16)
POOL_GROUP = 128
POOL_HALO = 16
ATTN_WIDTH = 512
HEAD_DIM = 64
HEAD_PAIRS = 4
IN_WIDTH = 5120
N_CHIPS = 4
RMS_EPS = 1e-6
C_U, C_ZP, C_Q, C_K, C_V, C_ZA, C_GL = 0, 512, 1024, 1536, 2048, 2560, 3072

ADAM_LR, ADAM_B1, ADAM_B2, ADAM_EPS, ADAM_WD, ADAM_STEP = 0.001, 0.9, 0.999, 1e-08, 0.01, 10

LANES = 128
ATTN_BLOCK = 256
ROW_TILE = 256
VMEM_LIMIT = 56 * 1024 * 1024


def _params(**kw):
    return pltpu.CompilerParams(vmem_limit_bytes=VMEM_LIMIT, **kw)


def _nt(a, b):
    return lax.dot_general(a, b, (((1,), (1,)), ((), ())), preferred_element_type=F32)


def _tn(a, b):
    return lax.dot_general(a, b, (((0,), (0,)), ((), ())), preferred_element_type=F32)


def _nn(a, b):
    return jnp.dot(a, b, preferred_element_type=F32)


def _sigmoid(z):
    return 1.0 / (1.0 + jnp.exp(-z))


def _rms_inproj(x, g, w_in, layer):
    S = x.shape[0]
    tm = min(ROW_TILE, S)

    def body(x_ref, g_ref, w_ref, u_ref, zp_ref, q_ref, k_ref, v_ref, za_ref, gl_ref, h_ref):
        xv = x_ref[...]
        r = lax.rsqrt(jnp.mean(xv * xv, axis=-1, keepdims=True) + RMS_EPS)
        h = ((xv * r) * g_ref[...]).astype(BF16)
        h_ref[...] = h

        def mm(c0, n):
            return _nn(h, w_ref[:, c0:c0 + n])

        u_ref[...] = mm(C_U, 512)
        zp_ref[...] = mm(C_ZP, 512)
        q_ref[...] = (mm(C_Q, 512) * 0.125).astype(BF16)
        k_ref[...] = mm(C_K, 512).astype(BF16)
        v_ref[...] = mm(C_V, 512).astype(BF16)
        za_ref[...] = mm(C_ZA, 512)
        for c in range(4):
            gl_ref[:, c * 512:(c + 1) * 512] = mm(C_GL + c * 512, 512)

    row = lambda n: pl.BlockSpec((tm, n), lambda i: (i, 0))
    sd = lambda n, dt: jax.ShapeDtypeStruct((S, n), dt)
    return pl.pallas_call(
        body, name=f"rms_inproj_l{layer}", grid=(S // tm,),
        in_specs=[row(D_MODEL), pl.BlockSpec((1, D_MODEL), lambda i: (0, 0)),
                  pl.BlockSpec((None, D_MODEL, IN_WIDTH), lambda i: (layer, 0, 0))],
        out_specs=[row(512), row(512), row(512), row(512), row(512), row(512), row(2048), row(D_MODEL)],
        out_shape=[sd(512, F32), sd(512, F32), sd(512, BF16), sd(512, BF16), sd(512, BF16), sd(512, F32),
                   sd(2048, F32), sd(D_MODEL, BF16)],
        compiler_params=_params(),
    )(x, g, w_in)


def _tri(n, strict_lower):
    r = lax.broadcasted_iota(jnp.int32, (n, n), 0)
    c = lax.broadcasted_iota(jnp.int32, (n, n), 1)
    return jnp.where(r > c if strict_lower else r < c, 1.0, 0.0).astype(BF16)


def _split_dot(x, m):
    hi = x.astype(BF16)
    lo = (x - hi.astype(F32)).astype(BF16)
    return _nn(hi, m) + _nn(lo, m)


def _log_terms(z):
    lg = jnp.log(1.0 + jnp.exp(-jnp.abs(z)))
    a = jnp.minimum(z, 0.0) - lg
    return a, a - z


def _attn_fwd(q, k, v, layer):
    S = q.shape[0]
    T = min(ATTN_BLOCK, S)
    nq = S // T
    assert nq <= LANES

    def body(q_ref, k_ref, v_ref, o_ref, c_ref):
        qi = pl.program_id(1)
        lane = lax.broadcasted_iota(jnp.int32, (T, LANES), 1)
        first = lane < HEAD_DIM
        q2 = q_ref[...]
        qh = (jnp.where(first, q2, 0).astype(BF16), jnp.where(first, 0, q2).astype(BF16))
        below = _tri(T, True)
        causal = lax.broadcasted_iota(jnp.int32, (T, T), 0) > lax.broadcasted_iota(jnp.int32, (T, T), 1)

        def tile(j, carry, masked):
            start = pl.multiple_of(j * T, T)
            kk = k_ref[pl.ds(start, T), :]
            vv = v_ref[pl.ds(start, T), :]
            new = []
            for h in range(2):
                acc, run, saved = carry[3 * h:3 * h + 3]
                z = _nt(qh[h], kk)
                a, l1m = _log_terms(z)
                if masked:
                    l1m = jnp.where(causal, l1m, 0.0)
                w = jnp.exp(a + _split_dot(l1m, below) + run)
                if masked:
                    w = jnp.where(causal, w, 0.0)
                saved = jnp.where(lane == j, run, saved)
                acc = acc + _nn(w.astype(BF16), vv)
                run = run + jnp.sum(l1m, axis=1, keepdims=True)
                new += [acc, run, saved]
            return tuple(new)

        zero = jnp.zeros((T, LANES), F32)
        col = jnp.zeros((T, 1), F32)
        carry = tile(qi, (zero, col, zero, zero, col, zero), True)
        carry = lax.fori_loop(0, qi, lambda n, c: tile(qi - 1 - n, c, False), carry)
        o_ref[...] = jnp.where(first, carry[0], carry[3])
        c_ref[:, :LANES] = carry[2]
        c_ref[:, LANES:] = carry[5]

    return pl.pallas_call(
        body, name=f"attn_fwd_l{layer}", grid=(HEAD_PAIRS, nq),
        in_specs=[pl.BlockSpec((T, LANES), lambda p, i: (i, p)),
                  pl.BlockSpec((S, LANES), lambda p, i: (0, p)),
                  pl.BlockSpec((S, LANES), lambda p, i: (0, p))],
        out_specs=[pl.BlockSpec((T, LANES), lambda p, i: (i, p)),
                   pl.BlockSpec((T, 2 * LANES), lambda p, i: (i, p))],
        out_shape=[jax.ShapeDtypeStruct((S, ATTN_WIDTH), F32), jax.ShapeDtypeStruct((S, 8 * LANES), F32)],
        compiler_params=_params(),
    )(q, k, v)


def _attn_bwd(q, k, v, saved, do, layer):
    S = q.shape[0]
    T = min(ATTN_BLOCK, S)
    nq = S // T

    def body(q_ref, k_ref, v_ref, c_ref, do_ref, dq_ref, dk_ref, dv_ref):
        qi = pl.program_id(1)

        @pl.when(qi == 0)
        def _():
            dk_ref[...] = jnp.zeros_like(dk_ref)
            dv_ref[...] = jnp.zeros_like(dv_ref)

        lane = lax.broadcasted_iota(jnp.int32, (T, LANES), 1)
        first = lane < HEAD_DIM
        q2 = q_ref[...]
        qh = (jnp.where(first, q2, 0).astype(BF16), jnp.where(first, 0, q2).astype(BF16))
        dob = do_ref[...].astype(BF16)
        doh = (jnp.where(first, dob, 0).astype(BF16), jnp.where(first, 0, dob).astype(BF16))
        saved_h = (c_ref[:, :LANES], c_ref[:, LANES:])
        below = _tri(T, True)
        before = _tri(T, False)
        causal = lax.broadcasted_iota(jnp.int32, (T, T), 0) > lax.broadcasted_iota(jnp.int32, (T, T), 1)

        def tile(j, carry, masked):
            start = pl.multiple_of(j * T, T)
            kk = k_ref[pl.ds(start, T), :]
            vv = v_ref[pl.ds(start, T), :]
            new, dks, dvs = [], [], []
            for h in range(2):
                dq, older = carry[2 * h:2 * h + 2]
                z = _nt(qh[h], kk)
                a, l1m = _log_terms(z)
                if masked:
                    l1m = jnp.where(causal, l1m, 0.0)
                run = jnp.sum(jnp.where(lane == j, saved_h[h], 0.0), axis=1, keepdims=True)
                w = jnp.exp(a + _split_dot(l1m, below) + run)
                if masked:
                    w = jnp.where(causal, w, 0.0)
                e = w * _nt(doh[h], vv)
                prefix = _split_dot(e, before) + older
                dz = e - jnp.exp(a) * (e + prefix)
                if masked:
                    dz = jnp.where(causal, dz, 0.0)
                dzb = dz.astype(BF16)
                dq = dq + _nn(dzb, kk)
                dks.append(_tn(dzb, q2))
                dvs.append(_tn(w.astype(BF16), dob))
                older = older + jnp.sum(e, axis=1, keepdims=True)
                new += [dq, older]
            dk_ref[pl.ds(start, T), :] += jnp.where(first, dks[0], dks[1])
            dv_ref[pl.ds(start, T), :] += jnp.where(first, dvs[0], dvs[1])
            return tuple(new)

        zero = jnp.zeros((T, LANES), F32)
        col = jnp.zeros((T, 1), F32)
        carry = lax.fori_loop(0, qi, lambda j, c: tile(j, c, False), (zero, col, zero, col))
        carry = tile(qi, carry, True)
        dq_ref[...] = jnp.where(first, carry[0], carry[2]) * 0.125

    blk = pl.BlockSpec((T, LANES), lambda p, i: (i, p))
    full = pl.BlockSpec((S, LANES), lambda p, i: (0, p))
    out = jax.ShapeDtypeStruct((S, ATTN_WIDTH), F32)
    return pl.pallas_call(
        body, name=f"attn_bwd_l{layer}", grid=(HEAD_PAIRS, nq),
        in_specs=[blk, full, full, pl.BlockSpec((T, 2 * LANES), lambda p, i: (i, p)), blk],
        out_specs=[blk, full, full],
        out_shape=[out, out, out],
        compiler_params=_params(),
    )(q, k, v, saved, do)


def _pool_counts(row0, tm):
    pos = row0 + lax.broadcasted_iota(jnp.int32, (tm, 1), 0)
    return [1.0 / jnp.minimum(pos + 1, w).astype(F32) for w in POOL_WINDOWS]


def _post_forward(ext_ref, inv_cnt, zp, o, za, gl, bg, pw_ref, scale, wpu_ref, wau_ref, tm):
    pooled, mixed = [], []
    for g, w in enumerate(POOL_WINDOWS):
        cols = slice(g * POOL_GROUP, (g + 1) * POOL_GROUP)
        tot = ext_ref[POOL_HALO:POOL_HALO + tm, cols]
        cur = tot
        for d in range(1, w):
            tot = tot + ext_ref[POOL_HALO - d:POOL_HALO - d + tm, cols]
        pg = (tot * inv_cnt[g] - cur).astype(BF16)
        pooled.append(pg)
        mixed.append(_nn(pg, pw_ref[g].astype(BF16)))
    pooled = jnp.concatenate(pooled, axis=1)
    mixed = jnp.concatenate(mixed, axis=1)
    sp = _sigmoid(zp)
    sa = _sigmoid(za)
    y_pool = (mixed * scale) * (zp * sp)
    y_attn = o * (za * sa)
    gate = _sigmoid(gl + bg)
    g0, g1 = gate[:, :D_MODEL], gate[:, D_MODEL:]
    up_p = _nn(y_pool.astype(BF16), wpu_ref[...])
    up_a = _nn(y_attn.astype(BF16), wau_ref[...])
    merged = g0 * up_p + g1 * up_a
    return pooled, mixed, sp, sa, y_pool, y_attn, g0, g1, up_p, up_a, merged


def _row_specs(tm, rev, n_tiles):
    tile_of = (lambda i: n_tiles - 1 - i) if rev else (lambda i: i)
    row = lambda n: pl.BlockSpec((tm, n), lambda i: (tile_of(i), 0))
    halo = pl.BlockSpec((POOL_HALO, POOL_WIDTH),
                        lambda i: (jnp.maximum(tile_of(i) * (tm // POOL_HALO) - 1, 0), 0))
    const = lambda shape: pl.BlockSpec(shape, lambda i: (0,) * len(shape))
    return tile_of, row, halo, const


def _layer_weight_spec(rows, cols, layer):
    return pl.BlockSpec((None, rows, cols), lambda i: (layer, 0, 0))


def _post_fwd(x, u, zp, o, za, gl, bg, pw, scale, wpu, wau, wout, layer):
    S = x.shape[0]
    tm = min(ROW_TILE, S)
    n_tiles = S // tm
    tile_of, row, halo, const = _row_specs(tm, False, n_tiles)

    def body(x_ref, u_ref, uh_ref, zp_ref, o_ref, za_ref, gl_ref, bg_ref, pw_ref, sc_ref, wpu_ref, wau_ref, wout_ref,
             out_ref, ext_ref):
        i = pl.program_id(0)
        ext_ref[:POOL_HALO, :] = jnp.where(i == 0, 0.0, uh_ref[...])
        ext_ref[POOL_HALO:, :] = u_ref[...]
        vals = _post_forward(ext_ref, _pool_counts(i * tm, tm), zp_ref[...], o_ref[...], za_ref[...], gl_ref[...],
                             bg_ref[...], pw_ref, sc_ref[...], wpu_ref, wau_ref, tm)
        out_ref[...] = x_ref[...] + _nn(vals[-1].astype(BF16), wout_ref[...])

    return pl.pallas_call(
        body, name=f"post_fwd_l{layer}", grid=(n_tiles,),
        in_specs=[row(D_MODEL), row(512), halo, row(512), row(512), row(512), row(2048), const((1, 2048)),
                  const((4, POOL_GROUP, POOL_GROUP)), const((1, POOL_WIDTH)),
                  _layer_weight_spec(POOL_WIDTH, D_MODEL, layer), _layer_weight_spec(ATTN_WIDTH, D_MODEL, layer),
                  _layer_weight_spec(D_MODEL, D_MODEL, layer)],
        out_specs=row(D_MODEL),
        out_shape=jax.ShapeDtypeStruct((S, D_MODEL), F32),
        scratch_shapes=[pltpu.VMEM((POOL_HALO + tm, POOL_WIDTH), F32)],
        compiler_params=_params(),
    )(x, u, u, zp, o, za, gl, bg, pw, scale, wpu, wau, wout)


def _post_bwd(dx, u, zp, o, za, gl, bg, pw, scale, wpu, wau, wout, layer):
    S = dx.shape[0]
    tm = min(ROW_TILE // 2, S)
    n_tiles = S // tm
    tile_of, row, halo, const = _row_specs(tm, True, n_tiles)

    def body(dx_ref, u_ref, uh_ref, zp_ref, o_ref, za_ref, gl_ref, bg_ref, pw_ref, sc_ref, wpu_ref, wau_ref, wout_ref,
             duz_ref, do_ref, dza_ref, dgl_ref, dsc_ref, dbg_ref,
             merged_ref, dup_ref, dua_ref, yp_ref, ya_ref, pooled_ref, dmixed_ref, ext_ref, nxt_ref):
        step = pl.program_id(0)
        i = tile_of(step)

        @pl.when(step == 0)
        def _():
            dsc_ref[...] = jnp.zeros_like(dsc_ref)
            dbg_ref[...] = jnp.zeros_like(dbg_ref)
            nxt_ref[tm:, :] = jnp.zeros((POOL_HALO, POOL_WIDTH), F32)

        ext_ref[:POOL_HALO, :] = jnp.where(i == 0, 0.0, uh_ref[...])
        ext_ref[POOL_HALO:, :] = u_ref[...]
        inv_cnt = _pool_counts(i * tm, tm)
        zp, za, o = zp_ref[...], za_ref[...], o_ref[...]
        pooled, mixed, sp, sa, y_pool, y_attn, g0, g1, up_p, up_a, merged = _post_forward(
            ext_ref, inv_cnt, zp, o, za, gl_ref[...], bg_ref[...], pw_ref, sc_ref[...], wpu_ref, wau_ref, tm)
        merged_ref[...] = merged.astype(BF16)
        yp_ref[...] = y_pool.astype(BF16)
        ya_ref[...] = y_attn.astype(BF16)
        pooled_ref[...] = pooled

        dmerged = _nt(dx_ref[...].astype(BF16), wout_ref[...])
        dup = (dmerged * g0).astype(BF16)
        dua = (dmerged * g1).astype(BF16)
        dup_ref[...] = dup
        dua_ref[...] = dua
        dgl0 = (dmerged * up_p) * (g0 * (1.0 - g0))
        dgl1 = (dmerged * up_a) * (g1 * (1.0 - g1))
        dgl_ref[:, :D_MODEL] = dgl0
        dgl_ref[:, D_MODEL:] = dgl1
        dbg_ref[:, :D_MODEL] += jnp.sum(dgl0, axis=0, keepdims=True)
        dbg_ref[:, D_MODEL:] += jnp.sum(dgl1, axis=0, keepdims=True)

        dy_attn = _nt(dua, wau_ref[...])
        do_ref[...] = dy_attn * (za * sa)
        dza_ref[...] = (dy_attn * o) * (sa * (1.0 + za * (1.0 - sa)))

        dy_pool = _nt(dup, wpu_ref[...])
        ms = mixed * sc_ref[...]
        dms = dy_pool * (zp * sp)
        duz_ref[:, POOL_WIDTH:] = (dy_pool * ms) * (sp * (1.0 + zp * (1.0 - sp)))
        dsc_ref[...] += jnp.sum(dms * mixed, axis=0, keepdims=True)
        dmixed = (dms * sc_ref[...]).astype(BF16)
        dmixed_ref[...] = dmixed
        for g, w in enumerate(POOL_WINDOWS):
            cols = slice(g * POOL_GROUP, (g + 1) * POOL_GROUP)
            dpg = _nt(dmixed[:, cols], pw_ref[g].astype(BF16))
            nxt_ref[:tm, cols] = dpg * inv_cnt[g]
            tot = -dpg
            for d in range(w):
                tot = tot + nxt_ref[d:d + tm, cols]
            duz_ref[:, cols] = tot
        nxt_ref[tm:, :] = nxt_ref[:POOL_HALO, :]

    sd = lambda n, dt: jax.ShapeDtypeStruct((S, n), dt)
    return pl.pallas_call(
        body, name=f"post_bwd_l{layer}", grid=(n_tiles,),
        in_specs=[row(D_MODEL), row(512), halo, row(512), row(512), row(512), row(2048), const((1, 2048)),
                  const((4, POOL_GROUP, POOL_GROUP)), const((1, POOL_WIDTH)),
                  _layer_weight_spec(POOL_WIDTH, D_MODEL, layer), _layer_weight_spec(ATTN_WIDTH, D_MODEL, layer),
                  _layer_weight_spec(D_MODEL, D_MODEL, layer)],
        out_specs=[row(1024), row(512), row(512), row(2048), const((1, POOL_WIDTH)), const((1, 2048)),
                   row(D_MODEL), row(D_MODEL), row(D_MODEL), row(512), row(512), row(512), row(512)],
        out_shape=[sd(1024, F32), sd(512, F32), sd(512, F32), sd(2048, F32),
                   jax.ShapeDtypeStruct((1, POOL_WIDTH), F32), jax.ShapeDtypeStruct((1, 2048), F32),
                   sd(D_MODEL, BF16), sd(D_MODEL, BF16), sd(D_MODEL, BF16), sd(512, BF16), sd(512, BF16),
                   sd(512, BF16), sd(512, BF16)],
        scratch_shapes=[pltpu.VMEM((POOL_HALO + tm, POOL_WIDTH), F32), pltpu.VMEM((tm + POOL_HALO, POOL_WIDTH), F32)],
        compiler_params=_params(),
    )(dx, u, u, zp, o, za, gl, bg, pw, scale, wpu, wau, wout)


def _rms_backward(dh, xv, r, g):
    xhat = xv * r
    dxhat = dh * g
    return r * (dxhat - xhat * jnp.mean(dxhat * xhat, axis=-1, keepdims=True)), dh * xhat


def _loss_head(x, g, target):
    S = x.shape[0]
    tm = min(ROW_TILE, S)

    def body(x_ref, g_ref, t_ref, loss_ref, dx_ref, dg_ref):
        @pl.when(pl.program_id(0) == 0)
        def _():
            loss_ref[...] = jnp.zeros_like(loss_ref)
            dg_ref[...] = jnp.zeros_like(dg_ref)

        xv = x_ref[...]
        r = lax.rsqrt(jnp.mean(xv * xv, axis=-1, keepdims=True) + RMS_EPS)
        diff = (xv * r) * g_ref[...] - t_ref[...]
        per_row = jnp.mean(diff * diff, axis=-1, keepdims=True)
        loss_ref[...] += 0.5 * jnp.sum(per_row, axis=0, keepdims=True)
        dx, dg_rows = _rms_backward(diff * (1.0 / D_MODEL), xv, r, g_ref[...])
        dx_ref[...] = dx
        dg_ref[...] += jnp.sum(dg_rows, axis=0, keepdims=True)

    row = pl.BlockSpec((tm, D_MODEL), lambda i: (i, 0))
    vec = pl.BlockSpec((1, D_MODEL), lambda i: (0, 0))
    return pl.pallas_call(
        body, name="loss_head", grid=(S // tm,),
        in_specs=[row, vec, row],
        out_specs=[pl.BlockSpec((1, LANES), lambda i: (0, 0)), row, vec],
        out_shape=[jax.ShapeDtypeStruct((1, LANES), F32), jax.ShapeDtypeStruct((S, D_MODEL), F32),
                   jax.ShapeDtypeStruct((1, D_MODEL), F32)],
        compiler_params=_params(),
    )(x, g, target)


def _inproj_bwd(pieces, w_in, x, g, dx_res, layer):
    S = x.shape[0]
    tm = min(ROW_TILE, S)
    cols = [(c0, p.shape[1]) for p, c0 in pieces]

    def body(*refs):
        piece_refs = refs[:len(cols)]
        w_ref, x_ref, g_ref, res_ref, dx_ref, dg_ref = refs[len(cols):]

        @pl.when(pl.program_id(0) == 0)
        def _():
            dg_ref[...] = jnp.zeros_like(dg_ref)

        dh = jnp.zeros((tm, D_MODEL), F32)
        for p_ref, (c0, n) in zip(piece_refs, cols):
            for c in range(0, n, 512):
                dh = dh + _nt(p_ref[:, c:c + 512].astype(BF16), w_ref[:, c0 + c:c0 + c + 512])
        xv = x_ref[...]
        r = lax.rsqrt(jnp.mean(xv * xv, axis=-1, keepdims=True) + RMS_EPS)
        dx, dg_rows = _rms_backward(dh, xv, r, g_ref[...])
        dx_ref[...] = res_ref[...] + dx
        dg_ref[...] += jnp.sum(dg_rows, axis=0, keepdims=True)

    row = lambda n: pl.BlockSpec((tm, n), lambda i: (i, 0))
    vec = pl.BlockSpec((1, D_MODEL), lambda i: (0, 0))
    return pl.pallas_call(
        body, name=f"inproj_bwd_l{layer}", grid=(S // tm,),
        in_specs=[row(n) for _, n in cols] + [_layer_weight_spec(D_MODEL, IN_WIDTH, layer), row(D_MODEL), vec,
                                              row(D_MODEL)],
        out_specs=[row(D_MODEL), vec],
        out_shape=[jax.ShapeDtypeStruct((S, D_MODEL), F32), jax.ShapeDtypeStruct((1, D_MODEL), F32)],
        compiler_params=_params(),
    )(*[p for p, _ in pieces], w_in, x, g, dx_res)


def _wgrad(a, b, name, layer, into=None, col0=0, n_total=None):
    S, M = a.shape
    N = b.shape[1]
    n_total = N if n_total is None else n_total
    tk = min(512, S)
    tn = min(512, N)
    nk = S // tk

    def body(*refs):
        a_ref, b_ref, out_ref = refs[0], refs[1], refs[-1]
        prod = _tn(a_ref[...].astype(BF16), b_ref[...].astype(BF16))

        @pl.when(pl.program_id(1) == 0)
        def _():
            out_ref[...] = prod

        @pl.when(pl.program_id(1) > 0)
        def _():
            out_ref[...] += prod

    in_specs = [pl.BlockSpec((tk, M), lambda j, k: (k, 0)), pl.BlockSpec((tk, tn), lambda j, k: (k, j))]
    args = [a, b]
    aliases = {}
    if into is not None:
        in_specs.append(pl.BlockSpec(memory_space=pl.ANY))
        args.append(into)
        aliases = {2: 0}
    return pl.pallas_call(
        body, name=name, grid=(N // tn, nk),
        in_specs=in_specs,
        out_specs=pl.BlockSpec((None, M, tn), lambda j, k: (layer, 0, col0 // tn + j)),
        out_shape=jax.ShapeDtypeStruct((2, M, n_total), F32),
        input_output_aliases=aliases,
        compiler_params=_params(),
    )(*args)


def _pool_wgrad(pooled, dmixed, layer):
    S = pooled.shape[0]
    tk = min(1024, S)

    def body(a_ref, b_ref, out_ref):
        prod = _tn(a_ref[...], b_ref[...])

        @pl.when(pl.program_id(1) == 0)
        def _():
            out_ref[...] = prod

        @pl.when(pl.program_id(1) > 0)
        def _():
            out_ref[...] += prod

    blk = pl.BlockSpec((tk, POOL_GROUP), lambda g, k: (k, g))
    return pl.pallas_call(
        body, name=f"pool_wgrad_l{layer}", grid=(4, S // tk),
        in_specs=[blk, blk],
        out_specs=pl.BlockSpec((None, POOL_GROUP, POOL_GROUP), lambda g, k: (g, 0, 0)),
        out_shape=jax.ShapeDtypeStruct((4, POOL_GROUP, POOL_GROUP), F32),
        compiler_params=_params(),
    )(pooled, dmixed)


def _local_step(x, target, norm_g, b_gate, pool_w, pool_scale, final_g, w_in, w_pu, w_au, w_out):
    n_layers = norm_g.shape[0]
    saved = []
    for l in range(n_layers):
        g = norm_g[l][None]
        bg = b_gate[l][None]
        sc = pool_scale[l][None]
        u, zp, q, k, v, za, gl, h = _rms_inproj(x, g, w_in, l)
        o, carry = _attn_fwd(q, k, v, l)
        saved.append((x, g, bg, sc, u, zp, q, k, v, za, gl, h, o, carry))
        x = _post_fwd(x, u, zp, o, za, gl, bg, pool_w[l], sc, w_pu, w_au, w_out, l)
    loss, dx, d_final_g = _loss_head(x, final_g[None], target)

    small = [None] * n_layers
    dw_in = dw_out = dw_pu = dw_au = None
    for l in reversed(range(n_layers)):
        x_in, g, bg, sc, u, zp, q, k, v, za, gl, h, o, carry = saved[l]
        (duz, do, dza, dgl, dsc, dbg, merged, dup, dua, y_pool, y_attn, pooled, dmixed) = _post_bwd(
            dx, u, zp, o, za, gl, bg, pool_w[l], sc, w_pu, w_au, w_out, l)
        dq, dk, dv = _attn_bwd(q, k, v, carry, do, l)
        pieces = [(duz, C_U), (dq, C_Q), (dk, C_K), (dv, C_V), (dza, C_ZA), (dgl, C_GL)]
        for p, c0 in pieces:
            dw_in = _wgrad(h, p, f"wgrad_in_l{l}_c{c0}", l, into=dw_in, col0=c0, n_total=IN_WIDTH)
        dw_out = _wgrad(merged, dx, f"wgrad_out_l{l}", l, into=dw_out)
        dw_pu = _wgrad(y_pool, dup, f"wgrad_pu_l{l}", l, into=dw_pu)
        dw_au = _wgrad(y_attn, dua, f"wgrad_au_l{l}", l, into=dw_au)
        dpw = _pool_wgrad(pooled, dmixed, l)
        dx, dg = _inproj_bwd(pieces, w_in, x_in, g, dx, l)
        small[l] = (dg[0], dbg[0], dpw, dsc[0])
    small = [jnp.stack([small[l][i] for l in range(n_layers)]) for i in range(4)]
    return loss[0, 0], dx, d_final_g[0], small, (dw_in, dw_pu, dw_au, dw_out)


SHARDED = ((2, 1280), (2, 256), (2, 256), (1, 256))
ANY = pl.BlockSpec(memory_space=pl.ANY)


def _part(ref, s, axis, width):
    sl = pl.ds(pl.multiple_of(s * width, width), width)
    return ref.at[:, sl] if axis == 2 else ref.at[sl, :]


def _place():
    x, y, c = lax.axis_index("x"), lax.axis_index("y"), lax.axis_index("c")
    return x, y, c, 2 * x + y


def _other_chip(x, y, m):
    px = 1 - x if m & 2 else x
    py = 1 - y if m & 1 else y
    return px, py, 2 * px + py


def _remote(src, dst, send, recv, k, to):
    return pltpu.make_async_remote_copy(src_ref=src, dst_ref=dst, send_sem=send.at[k], recv_sem=recv.at[k],
                                        device_id=to, device_id_type=MESH)


def _gather_weights(shards):
    n = len(SHARDED)

    def body(*refs):
        ins, outs = refs[:n], refs[n:2 * n]
        send, recv, local = refs[2 * n:]
        x, y, c, s = _place()
        me, sibling = (x, y, c), (x, y, 1 - c)
        own = []
        for a, (axis, width) in enumerate(SHARDED):
            for l in range(2):
                cp = pltpu.make_async_copy(ins[a].at[l], _part(outs[a].at[l], s, axis, width), local.at[2 * a + l])
                cp.start()
                own.append(cp)

        def copy(k, a, layer, shard, to, src=None):
            dst = _part(outs[a].at[layer], shard, *SHARDED[a])
            return _remote(dst if src is None else src, dst, send, recv, k, to)

        sent = []
        for a in range(n):
            for m in (1, 2, 3):
                px, py, _ = _other_chip(x, y, m)
                sent.append(copy(3 * a + m - 1, a, c, s, (px, py, c), src=ins[a].at[c]))
                sent[-1].start()
        for m in (1, 2, 3):
            _, _, sp = _other_chip(x, y, m)
            for a in range(n):
                copy(3 * a + m - 1, a, c, sp, me).wait_recv()
                sent.append(copy(3 * n + 3 * a + m - 1, a, c, sp, sibling))
                sent[-1].start()
        for m in (1, 2, 3):
            _, _, sp = _other_chip(x, y, m)
            for a in range(n):
                copy(3 * n + 3 * a + m - 1, a, 1 - c, sp, me).wait_recv()
        for cp in sent:
            cp.wait_send()
        for cp in own:
            cp.wait()

    def full(sh, axis):
        shape = list(sh.shape)
        shape[axis] *= N_CHIPS
        return jax.ShapeDtypeStruct(tuple(shape), sh.dtype)

    return pl.pallas_call(
        body, name="gather_weights",
        in_specs=[ANY] * n, out_specs=[ANY] * n,
        out_shape=[full(sh, axis) for sh, (axis, _) in zip(shards, SHARDED)],
        scratch_shapes=[pltpu.SemaphoreType.DMA((6 * n,)), pltpu.SemaphoreType.DMA((6 * n,)),
                        pltpu.SemaphoreType.DMA((2 * n,))],
    )(*shards)


def _sibling_exchange(dws):
    n = len(dws)

    def body(*refs):
        ins, outs = refs[:n], refs[n:2 * n]
        send, recv = refs[2 * n:]
        x, y, c, _ = _place()
        copies = [_remote(ins[a].at[1 - c], outs[a], send, recv, a, (x, y, 1 - c)) for a in range(n)]
        for cp in copies:
            cp.start()
        for cp in copies:
            cp.wait()

    return pl.pallas_call(
        body, name="grad_sibling_exchange",
        in_specs=[ANY] * n, out_specs=[ANY] * n,
        out_shape=[jax.ShapeDtypeStruct(d.shape[1:], d.dtype) for d in dws],
        scratch_shapes=[pltpu.SemaphoreType.DMA((n,)), pltpu.SemaphoreType.DMA((n,))],
    )(*dws)


def _pair_sum(dw, other, c_arr, name):
    _, R, C = dw.shape
    tr = 128 if C > 1024 else 256

    def body(c_ref, a_ref, b_ref, o_ref):
        o_ref[...] = a_ref[...] + b_ref[...]

    return pl.pallas_call(
        body, name=name,
        grid_spec=pltpu.PrefetchScalarGridSpec(
            num_scalar_prefetch=1, grid=(R // tr,),
            in_specs=[pl.BlockSpec((None, tr, C), lambda i, c_ref: (c_ref[0], i, 0)),
                      pl.BlockSpec((tr, C), lambda i, c_ref: (i, 0))],
            out_specs=pl.BlockSpec((tr, C), lambda i, c_ref: (i, 0))),
        out_shape=jax.ShapeDtypeStruct((R, C), dw.dtype),
        compiler_params=_params(),
    )(c_arr, dw, other)


def _chip_exchange(ps):
    n = len(ps)

    def body(*refs):
        ins, outs = refs[:n], refs[n:2 * n]
        send, recv = refs[2 * n:]
        x, y, c, _ = _place()
        copies = []
        for a, (axis, width) in enumerate(SHARDED):
            for m in (1, 2, 3):
                px, py, sp = _other_chip(x, y, m)
                copies.append(_remote(_part(ins[a], sp, axis, width), outs[a].at[m - 1], send, recv, 3 * a + m - 1,
                                      (px, py, c)))
        for cp in copies:
            cp.start()
        for cp in copies:
            cp.wait()

    def landing(p, axis, width):
        shape = [3] + list(p.shape)
        shape[axis] = width
        return jax.ShapeDtypeStruct(tuple(shape), p.dtype)

    return pl.pallas_call(
        body, name="grad_chip_exchange",
        in_specs=[ANY] * n, out_specs=[ANY] * n,
        out_shape=[landing(p, axis, width) for p, (axis, width) in zip(ps, SHARDED)],
        scratch_shapes=[pltpu.SemaphoreType.DMA((3 * n,)), pltpu.SemaphoreType.DMA((3 * n,))],
    )(*ps)


def _shard_sum(p, landed, s_arr, axis, width, name):
    R, C = p.shape
    _, Rs, Cs = landed.shape
    tr = min(256, Rs)
    if axis == 2:
        p_spec = pl.BlockSpec((tr, width), lambda i, s_ref: (i, s_ref[0]))
    else:
        p_spec = pl.BlockSpec((tr, C), lambda i, s_ref: (s_ref[0] * (width // tr) + i, 0))

    def body(s_ref, p_ref, l_ref, o_ref):
        o_ref[...] = ((p_ref[...] + l_ref[0]) + l_ref[1]) + l_ref[2]

    return pl.pallas_call(
        body, name=name,
        grid_spec=pltpu.PrefetchScalarGridSpec(
            num_scalar_prefetch=1, grid=(Rs // tr,),
            in_specs=[p_spec, pl.BlockSpec((3, tr, Cs), lambda i, s_ref: (0, i, 0))],
            out_specs=pl.BlockSpec((tr, Cs), lambda i, s_ref: (i, 0))),
        out_shape=jax.ShapeDtypeStruct((Rs, Cs), p.dtype),
        compiler_params=_params(),
    )(s_arr, p, landed)


def _sibling_share(fs):
    n = len(fs)

    def body(*refs):
        ins, outs = refs[:n], refs[n:2 * n]
        send, recv, local = refs[2 * n:]
        x, y, c, _ = _place()
        own = [pltpu.make_async_copy(ins[a], outs[a].at[c], local.at[a]) for a in range(n)]
        copies = [_remote(ins[a], outs[a].at[c], send, recv, a, (x, y, 1 - c)) for a in range(n)]
        for cp in own + copies:
            cp.start()
        for cp in copies + own:
            cp.wait()

    return pl.pallas_call(
        body, name="grad_sibling_share",
        in_specs=[ANY] * n, out_specs=[ANY] * n,
        out_shape=[jax.ShapeDtypeStruct((2,) + f.shape, f.dtype) for f in fs],
        scratch_shapes=[pltpu.SemaphoreType.DMA((n,)), pltpu.SemaphoreType.DMA((n,)), pltpu.SemaphoreType.DMA((n,))],
    )(*fs)


N_DEVICES = 8


def _all_reduce_small(packed):
    rows = packed.shape[0]

    def body(in_ref, out_ref, all_ref, send, recv):
        x, y, c, _ = _place()
        my_id = 4 * x + 2 * y + c
        all_ref[my_id] = in_ref[...]
        copies = []
        for m in range(1, N_DEVICES):
            px = 1 - x if m & 4 else x
            py = 1 - y if m & 2 else y
            pc = 1 - c if m & 1 else c
            cp = _remote(in_ref, all_ref.at[my_id], send, recv, m - 1, (px, py, pc))
            cp.start()
            copies.append((cp, 4 * px + 2 * py + pc))
        for m, (cp, peer_id) in enumerate(copies):
            _remote(in_ref, all_ref.at[peer_id], send, recv, m, (x, y, c)).wait_recv()
        for cp, _ in copies:
            cp.wait_send()
        total = all_ref[0]
        for d in range(1, N_DEVICES):
            total = total + all_ref[d]
        out_ref[...] = total

    vmem = pl.BlockSpec(memory_space=pltpu.VMEM)
    return pl.pallas_call(
        body, name="small_all_reduce",
        in_specs=[vmem], out_specs=vmem,
        out_shape=jax.ShapeDtypeStruct(packed.shape, packed.dtype),
        scratch_shapes=[pltpu.VMEM((N_DEVICES, rows, LANES), F32), pltpu.SemaphoreType.DMA((N_DEVICES - 1,)),
                        pltpu.SemaphoreType.DMA((N_DEVICES - 1,))],
        compiler_params=_params(),
    )(packed)


def _adamw(w, g, m, v, name):
    shape = w.shape
    C = shape[-1]
    flat = [t.reshape(-1, C) for t in (w, g, m, v)]
    R = flat[0].shape[0]
    tr = max(t for t in range(8, R + 1, 8) if R % t == 0 and t * C <= 384 * 1024)

    def body(w_ref, g_ref, m_ref, v_ref, d_ref, nm_ref, nv_ref):
        gv = g_ref[...]
        nm = ADAM_B1 * m_ref[...] + (1.0 - ADAM_B1) * gv
        nv = ADAM_B2 * v_ref[...] + (1.0 - ADAM_B2) * (gv * gv)
        m_hat = nm / (1.0 - ADAM_B1 ** ADAM_STEP)
        v_hat = nv / (1.0 - ADAM_B2 ** ADAM_STEP)
        d_ref[...] = -ADAM_LR * (m_hat / (jnp.sqrt(v_hat) + ADAM_EPS) + ADAM_WD * w_ref[...])
        nm_ref[...] = nm
        nv_ref[...] = nv

    blk = pl.BlockSpec((tr, C), lambda i: (i, 0))
    out = jax.ShapeDtypeStruct((R, C), F32)
    res = pl.pallas_call(
        body, name=name, grid=(R // tr,),
        in_specs=[blk] * 4, out_specs=[blk] * 3, out_shape=[out] * 3,
        compiler_params=_params(),
    )(*flat)
    return [t.reshape(shape) for t in res]


SMALL_SHAPES = ((2, 1024), (2, 2048), (2, 4, 128, 128), (2, 512), (1024,))


def _pack_small(parts):
    return jnp.concatenate([p.reshape(-1, LANES) for p in parts], axis=0)


def _unpack_small(packed):
    out, row = [], 0
    for shape in SMALL_SHAPES:
        n = 1
        for d in shape:
            n *= d
        out.append(packed[row:row + n // LANES].reshape(shape))
        row += n // LANES
    return out


def kernel(x, norm_g, w_in, b_gate, pool_w, pool_scale, w_pool_up, w_attn_up, w_out, final_g, loss_target, m_norm_g, m_w_in, m_b_gate, m_pool_w, m_pool_scale, m_w_pool_up, m_w_attn_up, m_w_out, m_final_g, v_norm_g, v_w_in, v_b_gate, v_pool_w, v_pool_scale, v_w_pool_up, v_w_attn_up, v_w_out, v_final_g):
    _, _, c, s = _place()
    c_arr = jnp.reshape(c, (1,)).astype(jnp.int32)
    s_arr = jnp.reshape(s, (1,)).astype(jnp.int32)

    full = _gather_weights([w.astype(BF16) for w in (w_in, w_pool_up, w_attn_up, w_out)])
    loss_part, dx, d_final_g, small, dws = _local_step(x[0], loss_target[0], norm_g, b_gate, pool_w, pool_scale,
                                                       final_g, *full)
    loss = lax.psum(loss_part, ("x", "y", "c"))

    names = ("w_in", "w_pool_up", "w_attn_up", "w_out")
    other = _sibling_exchange(dws)
    pair = [_pair_sum(d, o, c_arr, f"grad_pair_sum_{n}") for d, o, n in zip(dws, other, names)]
    landed = _chip_exchange(pair)
    mine = [_shard_sum(p, l, s_arr, axis, width, f"grad_shard_sum_{n}")
            for p, l, (axis, width), n in zip(pair, landed, SHARDED, names)]
    g_in, g_pu, g_au, g_out = _sibling_share(mine)

    g_small = _unpack_small(_all_reduce_small(_pack_small(small + [d_final_g])))
    upd_small = _adamw(_pack_small([norm_g, b_gate, pool_w, pool_scale, final_g]), _pack_small(g_small),
                       _pack_small([m_norm_g, m_b_gate, m_pool_w, m_pool_scale, m_final_g]),
                       _pack_small([v_norm_g, v_b_gate, v_pool_w, v_pool_scale, v_final_g]), "adamw_small")
    d_small, nm_small, nv_small = [_unpack_small(t) for t in upd_small]
    upd_in = _adamw(w_in, g_in, m_w_in, v_w_in, "adamw_w_in")
    upd_pu = _adamw(w_pool_up, g_pu, m_w_pool_up, v_w_pool_up, "adamw_w_pool_up")
    upd_au = _adamw(w_attn_up, g_au, m_w_attn_up, v_w_attn_up, "adamw_w_attn_up")
    upd_out = _adamw(w_out, g_out, m_w_out, v_w_out, "adamw_w_out")

    def ordered(sm, k):
        big = (upd_in[k], upd_pu[k], upd_au[k], upd_out[k]) if k is not None else (g_in, g_pu, g_au, g_out)
        return [sm[0], big[0], sm[1], sm[2], sm[3], big[1], big[2], big[3], sm[4]]

    return (loss, dx[None], *ordered(g_small, None), *ordered(d_small, 0), *ordered(nm_small, 1),
            *ordered(nv_small, 2))
```

```python
import functools

import jax
import jax.numpy as jnp
from jax import lax
from jax.experimental import pallas as pl
from jax.experimental.pallas import tpu as pltpu

F32 = jnp.float32
BF16 = jnp.bfloat16
MESH = pl.DeviceIdType.MESH

D_MODEL = 1024
POOL_WIDTH = 512
POOL_WINDOWS = (2, 4, 8, 16)
POOL_GROUP = 128
POOL_HALO = 16
ATTN_WIDTH = 512
HEAD_DIM = 64
HEAD_PAIRS = 4
IN_WIDTH = 5120
N_CHIPS = 4
RMS_EPS = 1e-6
C_U, C_ZP, C_Q, C_K, C_V, C_ZA, C_GL = 0, 512, 1024, 1536, 2048, 2560, 3072

ADAM_LR, ADAM_B1, ADAM_B2, ADAM_EPS, ADAM_WD, ADAM_STEP = 0.001, 0.9, 0.999, 1e-08, 0.01, 10

LANES = 128
ATTN_BLOCK = 256
ROW_TILE = 256
VMEM_LIMIT = 56 * 1024 * 1024


def _params(**kw):
    return pltpu.CompilerParams(vmem_limit_bytes=VMEM_LIMIT, **kw)


def _nt(a, b):
    return lax.dot_general(a, b, (((1,), (1,)), ((), ())), preferred_element_type=F32)


def _tn(a, b):
    return lax.dot_general(a, b, (((0,), (0,)), ((), ())), preferred_element_type=F32)


def _nn(a, b):
    return jnp.dot(a, b, preferred_element_type=F32)


def _sigmoid(z):
    return 1.0 / (1.0 + jnp.exp(-z))


def _rms_inproj(x, g, w_in, layer):
    S = x.shape[0]
    tm = min(ROW_TILE, S)

    def body(x_ref, g_ref, w_ref, u_ref, zp_ref, q_ref, k_ref, v_ref, za_ref, gl_ref, h_ref):
        xv = x_ref[...]
        r = lax.rsqrt(jnp.mean(xv * xv, axis=-1, keepdims=True) + RMS_EPS)
        h = ((xv * r) * g_ref[...]).astype(BF16)
        h_ref[...] = h

        def mm(c0, n):
            return _nn(h, w_ref[:, c0:c0 + n])

        u_ref[...] = mm(C_U, 512)
        zp_ref[...] = mm(C_ZP, 512)
        q_ref[...] = (mm(C_Q, 512) * 0.125).astype(BF16)
        k_ref[...] = mm(C_K, 512).astype(BF16)
        v_ref[...] = mm(C_V, 512).astype(BF16)
        za_ref[...] = mm(C_ZA, 512)
        for c in range(4):
            gl_ref[:, c * 512:(c + 1) * 512] = mm(C_GL + c * 512, 512)

    row = lambda n: pl.BlockSpec((tm, n), lambda i: (i, 0))
    sd = lambda n, dt: jax.ShapeDtypeStruct((S, n), dt)
    return pl.pallas_call(
        body, name=f"rms_inproj_l{layer}", grid=(S // tm,),
        in_specs=[row(D_MODEL), pl.BlockSpec((1, D_MODEL), lambda i: (0, 0)),
                  pl.BlockSpec((None, D_MODEL, IN_WIDTH), lambda i: (layer, 0, 0))],
        out_specs=[row(512), row(512), row(512), row(512), row(512), row(512), row(2048), row(D_MODEL)],
        out_shape=[sd(512, F32), sd(512, F32), sd(512, BF16), sd(512, BF16), sd(512, BF16), sd(512, F32),
                   sd(2048, F32), sd(D_MODEL, BF16)],
        compiler_params=_params(),
    )(x, g, w_in)


def _tri(n, strict_lower):
    r = lax.broadcasted_iota(jnp.int32, (n, n), 0)
    c = lax.broadcasted_iota(jnp.int32, (n, n), 1)
    return jnp.where(r > c if strict_lower else r < c, 1.0, 0.0).astype(BF16)


def _split_dot(x, m):
    hi = x.astype(BF16)
    lo = (x - hi.astype(F32)).astype(BF16)
    return _nn(hi, m) + _nn(lo, m)


def _log_terms(z):
    lg = jnp.log(1.0 + jnp.exp(-jnp.abs(z)))
    a = jnp.minimum(z, 0.0) - lg
    return a, a - z


EXHAUSTED = -104.0
UNREACHED = -1e30


def _attn_fwd(q, k, v, layer):
    S = q.shape[0]
    T = min(ATTN_BLOCK, S)
    nq = S // T
    assert nq <= LANES

    def body(q_ref, k_ref, v_ref, o_ref, c_ref):
        qi = pl.program_id(1)
        lane = lax.broadcasted_iota(jnp.int32, (T, LANES), 1)
        first = lane < HEAD_DIM
        q2 = q_ref[...]
        qh = (jnp.where(first, q2, 0).astype(BF16), jnp.where(first, 0, q2).astype(BF16))
        below = _tri(T, True)
        causal = lax.broadcasted_iota(jnp.int32, (T, T), 0) > lax.broadcasted_iota(jnp.int32, (T, T), 1)

        def tile(j, carry, masked):
            start = pl.multiple_of(j * T, T)
            kk = k_ref[pl.ds(start, T), :]
            vv = v_ref[pl.ds(start, T), :]
            new = []
            for h in range(2):
                acc, run, saved = carry[3 * h:3 * h + 3]
                z = _nt(qh[h], kk)
                a, l1m = _log_terms(z)
                if masked:
                    l1m = jnp.where(causal, l1m, 0.0)
                w = jnp.exp(a + _split_dot(l1m, below) + run)
                if masked:
                    w = jnp.where(causal, w, 0.0)
                saved = jnp.where(lane == j, run, saved)
                acc = acc + _nn(w.astype(BF16), vv)
                run = run + jnp.sum(l1m, axis=1, keepdims=True)
                new += [acc, run, saved]
            return tuple(new)

        def alive(carry):
            return (jnp.max(jnp.maximum(carry[1], carry[4])) > EXHAUSTED).astype(jnp.int32)

        def older_block(state):
            j, _, carry = state
            carry = tile(j, carry, False)
            return j - 1, alive(carry), carry

        zero = jnp.zeros((T, LANES), F32)
        col = jnp.zeros((T, 1), F32)
        never = jnp.full((T, LANES), UNREACHED, F32)
        carry = tile(qi, (zero, col, never, zero, col, never), True)
        carry = lax.while_loop(lambda st: jnp.logical_and(st[0] >= 0, st[1] > 0), older_block,
                               (qi - 1, alive(carry), carry))[2]
        o_ref[...] = jnp.where(first, carry[0], carry[3])
        c_ref[:, :LANES] = carry[2]
        c_ref[:, LANES:] = carry[5]

    return pl.pallas_call(
        body, name=f"attn_fwd_l{layer}", grid=(HEAD_PAIRS, nq),
        in_specs=[pl.BlockSpec((T, LANES), lambda p, i: (i, p)),
                  pl.BlockSpec((S, LANES), lambda p, i: (0, p)),
                  pl.BlockSpec((S, LANES), lambda p, i: (0, p))],
        out_specs=[pl.BlockSpec((T, LANES), lambda p, i: (i, p)),
                   pl.BlockSpec((T, 2 * LANES), lambda p, i: (i, p))],
        out_shape=[jax.ShapeDtypeStruct((S, ATTN_WIDTH), F32), jax.ShapeDtypeStruct((S, 8 * LANES), F32)],
        compiler_params=_params(),
    )(q, k, v)


def _attn_bwd(q, k, v, saved, do, layer):
    S = q.shape[0]
    T = min(ATTN_BLOCK, S)
    nq = S // T

    def body(q_ref, k_ref, v_ref, c_ref, do_ref, dq_ref, dk_ref, dv_ref):
        qi = pl.program_id(1)

        @pl.when(qi == 0)
        def _():
            dk_ref[...] = jnp.zeros_like(dk_ref)
            dv_ref[...] = jnp.zeros_like(dv_ref)

        lane = lax.broadcasted_iota(jnp.int32, (T, LANES), 1)
        first = lane < HEAD_DIM
        q2 = q_ref[...]
        qh = (jnp.where(first, q2, 0).astype(BF16), jnp.where(first, 0, q2).astype(BF16))
        dob = do_ref[...].astype(BF16)
        doh = (jnp.where(first, dob, 0).astype(BF16), jnp.where(first, 0, dob).astype(BF16))
        saved_h = (c_ref[:, :LANES], c_ref[:, LANES:])
        below = _tri(T, True)
        before = _tri(T, False)
        causal = lax.broadcasted_iota(jnp.int32, (T, T), 0) > lax.broadcasted_iota(jnp.int32, (T, T), 1)

        def tile(j, carry, masked):
            start = pl.multiple_of(j * T, T)
            kk = k_ref[pl.ds(start, T), :]
            vv = v_ref[pl.ds(start, T), :]
            new, dks, dvs = [], [], []
            for h in range(2):
                dq, older = carry[2 * h:2 * h + 2]
                z = _nt(qh[h], kk)
                a, l1m = _log_terms(z)
                if masked:
                    l1m = jnp.where(causal, l1m, 0.0)
                run = jnp.sum(jnp.where(lane == j, saved_h[h], 0.0), axis=1, keepdims=True)
                w = jnp.exp(a + _split_dot(l1m, below) + run)
                if masked:
                    w = jnp.where(causal, w, 0.0)
                e = w * _nt(doh[h], vv)
                prefix = _split_dot(e, before) + older
                dz = e - jnp.exp(a) * (e + prefix)
                if masked:
                    dz = jnp.where(causal, dz, 0.0)
                dzb = dz.astype(BF16)
                dq = dq + _nn(dzb, kk)
                dks.append(_tn(dzb, q2))
                dvs.append(_tn(w.astype(BF16), dob))
                older = older + jnp.sum(e, axis=1, keepdims=True)
                new += [dq, older]
            dk_ref[pl.ds(start, T), :] += jnp.where(first, dks[0], dks[1])
            dv_ref[pl.ds(start, T), :] += jnp.where(first, dvs[0], dvs[1])
            return tuple(new)

        col_max = jnp.max(jnp.maximum(saved_h[0], saved_h[1]), axis=0, keepdims=True)
        lane_row = lax.broadcasted_iota(jnp.int32, (1, LANES), 1)
        reached = jnp.sum(jnp.where(jnp.logical_and(col_max > EXHAUSTED, lane_row < qi), 1, 0))
        zero = jnp.zeros((T, LANES), F32)
        col = jnp.zeros((T, 1), F32)
        carry = lax.fori_loop(qi - reached, qi, lambda j, c: tile(j, c, False), (zero, col, zero, col))
        carry = tile(qi, carry, True)
        dq_ref[...] = jnp.where(first, carry[0], carry[2]) * 0.125

    blk = pl.BlockSpec((T, LANES), lambda p, i: (i, p))
    full = pl.BlockSpec((S, LANES), lambda p, i: (0, p))
    out = jax.ShapeDtypeStruct((S, ATTN_WIDTH), F32)
    return pl.pallas_call(
        body, name=f"attn_bwd_l{layer}", grid=(HEAD_PAIRS, nq),
        in_specs=[blk, full, full, pl.BlockSpec((T, 2 * LANES), lambda p, i: (i, p)), blk],
        out_specs=[blk, full, full],
        out_shape=[out, out, out],
        compiler_params=_params(),
    )(q, k, v, saved, do)


def _pool_counts(row0, tm):
    pos = row0 + lax.broadcasted_iota(jnp.int32, (tm, 1), 0)
    return [1.0 / jnp.minimum(pos + 1, w).astype(F32) for w in POOL_WINDOWS]


def _post_forward(ext_ref, inv_cnt, zp, o, za, gl, bg, pw_ref, scale, wpu_ref, wau_ref, tm):
    pooled, mixed = [], []
    for g, w in enumerate(POOL_WINDOWS):
        cols = slice(g * POOL_GROUP, (g + 1) * POOL_GROUP)
        tot = ext_ref[POOL_HALO:POOL_HALO + tm, cols]
        cur = tot
        for d in range(1, w):
            tot = tot + ext_ref[POOL_HALO - d:POOL_HALO - d + tm, cols]
        pg = (tot * inv_cnt[g] - cur).astype(BF16)
        pooled.append(pg)
        mixed.append(_nn(pg, pw_ref[g].astype(BF16)))
    pooled = jnp.concatenate(pooled, axis=1)
    mixed = jnp.concatenate(mixed, axis=1)
    sp = _sigmoid(zp)
    sa = _sigmoid(za)
    y_pool = (mixed * scale) * (zp * sp)
    y_attn = o * (za * sa)
    gate = _sigmoid(gl + bg)
    g0, g1 = gate[:, :D_MODEL], gate[:, D_MODEL:]
    up_p = _nn(y_pool.astype(BF16), wpu_ref[...])
    up_a = _nn(y_attn.astype(BF16), wau_ref[...])
    merged = g0 * up_p + g1 * up_a
    return pooled, mixed, sp, sa, y_pool, y_attn, g0, g1, up_p, up_a, merged


def _row_specs(tm, rev, n_tiles):
    tile_of = (lambda i: n_tiles - 1 - i) if rev else (lambda i: i)
    row = lambda n: pl.BlockSpec((tm, n), lambda i: (tile_of(i), 0))
    halo = pl.BlockSpec((POOL_HALO, POOL_WIDTH),
                        lambda i: (jnp.maximum(tile_of(i) * (tm // POOL_HALO) - 1, 0), 0))
    const = lambda shape: pl.BlockSpec(shape, lambda i: (0,) * len(shape))
    return tile_of, row, halo, const


def _layer_weight_spec(rows, cols, layer):
    return pl.BlockSpec((None, rows, cols), lambda i: (layer, 0, 0))


def _post_fwd(x, u, zp, o, za, gl, bg, pw, scale, wpu, wau, wout, layer):
    S = x.shape[0]
    tm = min(ROW_TILE, S)
    n_tiles = S // tm
    tile_of, row, halo, const = _row_specs(tm, False, n_tiles)

    def body(x_ref, u_ref, uh_ref, zp_ref, o_ref, za_ref, gl_ref, bg_ref, pw_ref, sc_ref, wpu_ref, wau_ref, wout_ref,
             out_ref, ext_ref):
        i = pl.program_id(0)
        ext_ref[:POOL_HALO, :] = jnp.where(i == 0, 0.0, uh_ref[...])
        ext_ref[POOL_HALO:, :] = u_ref[...]
        vals = _post_forward(ext_ref, _pool_counts(i * tm, tm), zp_ref[...], o_ref[...], za_ref[...], gl_ref[...],
                             bg_ref[...], pw_ref, sc_ref[...], wpu_ref, wau_ref, tm)
        out_ref[...] = x_ref[...] + _nn(vals[-1].astype(BF16), wout_ref[...])

    return pl.pallas_call(
        body, name=f"post_fwd_l{layer}", grid=(n_tiles,),
        in_specs=[row(D_MODEL), row(512), halo, row(512), row(512), row(512), row(2048), const((1, 2048)),
                  const((4, POOL_GROUP, POOL_GROUP)), const((1, POOL_WIDTH)),
                  _layer_weight_spec(POOL_WIDTH, D_MODEL, layer), _layer_weight_spec(ATTN_WIDTH, D_MODEL, layer),
                  _layer_weight_spec(D_MODEL, D_MODEL, layer)],
        out_specs=row(D_MODEL),
        out_shape=jax.ShapeDtypeStruct((S, D_MODEL), F32),
        scratch_shapes=[pltpu.VMEM((POOL_HALO + tm, POOL_WIDTH), F32)],
        compiler_params=_params(),
    )(x, u, u, zp, o, za, gl, bg, pw, scale, wpu, wau, wout)


def _post_bwd(dx, u, zp, o, za, gl, bg, pw, scale, wpu, wau, wout, layer):
    S = dx.shape[0]
    tm = min(ROW_TILE // 2, S)
    n_tiles = S // tm
    tile_of, row, halo, const = _row_specs(tm, True, n_tiles)

    def body(dx_ref, u_ref, uh_ref, zp_ref, o_ref, za_ref, gl_ref, bg_ref, pw_ref, sc_ref, wpu_ref, wau_ref, wout_ref,
             duz_ref, do_ref, dza_ref, dgl_ref, dsc_ref, dbg_ref,
             merged_ref, dup_ref, dua_ref, yp_ref, ya_ref, pooled_ref, dmixed_ref, ext_ref, nxt_ref):
        step = pl.program_id(0)
        i = tile_of(step)

        @pl.when(step == 0)
        def _():
            dsc_ref[...] = jnp.zeros_like(dsc_ref)
            dbg_ref[...] = jnp.zeros_like(dbg_ref)
            nxt_ref[tm:, :] = jnp.zeros((POOL_HALO, POOL_WIDTH), F32)

        ext_ref[:POOL_HALO, :] = jnp.where(i == 0, 0.0, uh_ref[...])
        ext_ref[POOL_HALO:, :] = u_ref[...]
        inv_cnt = _pool_counts(i * tm, tm)
        zp, za, o = zp_ref[...], za_ref[...], o_ref[...]
        pooled, mixed, sp, sa, y_pool, y_attn, g0, g1, up_p, up_a, merged = _post_forward(
            ext_ref, inv_cnt, zp, o, za, gl_ref[...], bg_ref[...], pw_ref, sc_ref[...], wpu_ref, wau_ref, tm)
        merged_ref[...] = merged.astype(BF16)
        yp_ref[...] = y_pool.astype(BF16)
        ya_ref[...] = y_attn.astype(BF16)
        pooled_ref[...] = pooled

        dmerged = _nt(dx_ref[...].astype(BF16), wout_ref[...])
        dup = (dmerged * g0).astype(BF16)
        dua = (dmerged * g1).astype(BF16)
        dup_ref[...] = dup
        dua_ref[...] = dua
        dgl0 = (dmerged * up_p) * (g0 * (1.0 - g0))
        dgl1 = (dmerged * up_a) * (g1 * (1.0 - g1))
        dgl_ref[:, :D_MODEL] = dgl0
        dgl_ref[:, D_MODEL:] = dgl1
        dbg_ref[:, :D_MODEL] += jnp.sum(dgl0, axis=0, keepdims=True)
        dbg_ref[:, D_MODEL:] += jnp.sum(dgl1, axis=0, keepdims=True)

        dy_attn = _nt(dua, wau_ref[...])
        do_ref[...] = dy_attn * (za * sa)
        dza_ref[...] = (dy_attn * o) * (sa * (1.0 + za * (1.0 - sa)))

        dy_pool = _nt(dup, wpu_ref[...])
        ms = mixed * sc_ref[...]
        dms = dy_pool * (zp * sp)
        duz_ref[:, POOL_WIDTH:] = (dy_pool * ms) * (sp * (1.0 + zp * (1.0 - sp)))
        dsc_ref[...] += jnp.sum(dms * mixed, axis=0, keepdims=True)
        dmixed = (dms * sc_ref[...]).astype(BF16)
        dmixed_ref[...] = dmixed
        for g, w in enumerate(POOL_WINDOWS):
            cols = slice(g * POOL_GROUP, (g + 1) * POOL_GROUP)
            dpg = _nt(dmixed[:, cols], pw_ref[g].astype(BF16))
            nxt_ref[:tm, cols] = dpg * inv_cnt[g]
            tot = -dpg
            for d in range(w):
                tot = tot + nxt_ref[d:d + tm, cols]
            duz_ref[:, cols] = tot
        nxt_ref[tm:, :] = nxt_ref[:POOL_HALO, :]

    sd = lambda n, dt: jax.ShapeDtypeStruct((S, n), dt)
    return pl.pallas_call(
        body, name=f"post_bwd_l{layer}", grid=(n_tiles,),
        in_specs=[row(D_MODEL), row(512), halo, row(512), row(512), row(512), row(2048), const((1, 2048)),
                  const((4, POOL_GROUP, POOL_GROUP)), const((1, POOL_WIDTH)),
                  _layer_weight_spec(POOL_WIDTH, D_MODEL, layer), _layer_weight_spec(ATTN_WIDTH, D_MODEL, layer),
                  _layer_weight_spec(D_MODEL, D_MODEL, layer)],
        out_specs=[row(1024), row(512), row(512), row(2048), const((1, POOL_WIDTH)), const((1, 2048)),
                   row(D_MODEL), row(D_MODEL), row(D_MODEL), row(512), row(512), row(512), row(512)],
        out_shape=[sd(1024, F32), sd(512, F32), sd(512, F32), sd(2048, F32),
                   jax.ShapeDtypeStruct((1, POOL_WIDTH), F32), jax.ShapeDtypeStruct((1, 2048), F32),
                   sd(D_MODEL, BF16), sd(D_MODEL, BF16), sd(D_MODEL, BF16), sd(512, BF16), sd(512, BF16),
                   sd(512, BF16), sd(512, BF16)],
        scratch_shapes=[pltpu.VMEM((POOL_HALO + tm, POOL_WIDTH), F32), pltpu.VMEM((tm + POOL_HALO, POOL_WIDTH), F32)],
        compiler_params=_params(),
    )(dx, u, u, zp, o, za, gl, bg, pw, scale, wpu, wau, wout)


def _rms_backward(dh, xv, r, g):
    xhat = xv * r
    dxhat = dh * g
    return r * (dxhat - xhat * jnp.mean(dxhat * xhat, axis=-1, keepdims=True)), dh * xhat


def _loss_head(x, g, target):
    S = x.shape[0]
    tm = min(ROW_TILE, S)

    def body(x_ref, g_ref, t_ref, loss_ref, dx_ref, dg_ref):
        @pl.when(pl.program_id(0) == 0)
        def _():
            loss_ref[...] = jnp.zeros_like(loss_ref)
            dg_ref[...] = jnp.zeros_like(dg_ref)

        xv = x_ref[...]
        r = lax.rsqrt(jnp.mean(xv * xv, axis=-1, keepdims=True) + RMS_EPS)
        diff = (xv * r) * g_ref[...] - t_ref[...]
        per_row = jnp.mean(diff * diff, axis=-1, keepdims=True)
        loss_ref[...] += 0.5 * jnp.sum(per_row, axis=0, keepdims=True)
        dx, dg_rows = _rms_backward(diff * (1.0 / D_MODEL), xv, r, g_ref[...])
        dx_ref[...] = dx
        dg_ref[...] += jnp.sum(dg_rows, axis=0, keepdims=True)

    row = pl.BlockSpec((tm, D_MODEL), lambda i: (i, 0))
    vec = pl.BlockSpec((1, D_MODEL), lambda i: (0, 0))
    return pl.pallas_call(
        body, name="loss_head", grid=(S // tm,),
        in_specs=[row, vec, row],
        out_specs=[pl.BlockSpec((1, LANES), lambda i: (0, 0)), row, vec],
        out_shape=[jax.ShapeDtypeStruct((1, LANES), F32), jax.ShapeDtypeStruct((S, D_MODEL), F32),
                   jax.ShapeDtypeStruct((1, D_MODEL), F32)],
        compiler_params=_params(),
    )(x, g, target)


def _inproj_bwd(pieces, w_in, x, g, dx_res, layer):
    S = x.shape[0]
    tm = min(ROW_TILE, S)
    cols = [(c0, p.shape[1]) for p, c0 in pieces]

    def body(*refs):
        piece_refs = refs[:len(cols)]
        w_ref, x_ref, g_ref, res_ref, dx_ref, dg_ref = refs[len(cols):]

        @pl.when(pl.program_id(0) == 0)
        def _():
            dg_ref[...] = jnp.zeros_like(dg_ref)

        dh = jnp.zeros((tm, D_MODEL), F32)
        for p_ref, (c0, n) in zip(piece_refs, cols):
            for c in range(0, n, 512):
                dh = dh + _nt(p_ref[:, c:c + 512].astype(BF16), w_ref[:, c0 + c:c0 + c + 512])
        xv = x_ref[...]
        r = lax.rsqrt(jnp.mean(xv * xv, axis=-1, keepdims=True) + RMS_EPS)
        dx, dg_rows = _rms_backward(dh, xv, r, g_ref[...])
        dx_ref[...] = res_ref[...] + dx
        dg_ref[...] += jnp.sum(dg_rows, axis=0, keepdims=True)

    row = lambda n: pl.BlockSpec((tm, n), lambda i: (i, 0))
    vec = pl.BlockSpec((1, D_MODEL), lambda i: (0, 0))
    return pl.pallas_call(
        body, name=f"inproj_bwd_l{layer}", grid=(S // tm,),
        in_specs=[row(n) for _, n in cols] + [_layer_weight_spec(D_MODEL, IN_WIDTH, layer), row(D_MODEL), vec,
                                              row(D_MODEL)],
        out_specs=[row(D_MODEL), vec],
        out_shape=[jax.ShapeDtypeStruct((S, D_MODEL), F32), jax.ShapeDtypeStruct((1, D_MODEL), F32)],
        compiler_params=_params(),
    )(*[p for p, _ in pieces], w_in, x, g, dx_res)


def _wgrad(a, b, name, layer, into=None, col0=0, n_total=None):
    S, M = a.shape
    N = b.shape[1]
    n_total = N if n_total is None else n_total
    tk = min(512, S)
    tn = min(512, N)
    nk = S // tk

    def body(*refs):
        a_ref, b_ref, out_ref = refs[0], refs[1], refs[-1]
        prod = _tn(a_ref[...].astype(BF16), b_ref[...].astype(BF16))

        @pl.when(pl.program_id(1) == 0)
        def _():
            out_ref[...] = prod

        @pl.when(pl.program_id(1) > 0)
        def _():
            out_ref[...] += prod

    in_specs = [pl.BlockSpec((tk, M), lambda j, k: (k, 0)), pl.BlockSpec((tk, tn), lambda j, k: (k, j))]
    args = [a, b]
    aliases = {}
    if into is not None:
        in_specs.append(pl.BlockSpec(memory_space=pl.ANY))
        args.append(into)
        aliases = {2: 0}
    return pl.pallas_call(
        body, name=name, grid=(N // tn, nk),
        in_specs=in_specs,
        out_specs=pl.BlockSpec((None, M, tn), lambda j, k: (layer, 0, col0 // tn + j)),
        out_shape=jax.ShapeDtypeStruct((2, M, n_total), F32),
        input_output_aliases=aliases,
        compiler_params=_params(),
    )(*args)


def _pool_wgrad(pooled, dmixed, layer):
    S = pooled.shape[0]
    tk = min(1024, S)

    def body(a_ref, b_ref, out_ref):
        prod = _tn(a_ref[...], b_ref[...])

        @pl.when(pl.program_id(1) == 0)
        def _():
            out_ref[...] = prod

        @pl.when(pl.program_id(1) > 0)
        def _():
            out_ref[...] += prod

    blk = pl.BlockSpec((tk, POOL_GROUP), lambda g, k: (k, g))
    return pl.pallas_call(
        body, name=f"pool_wgrad_l{layer}", grid=(4, S // tk),
        in_specs=[blk, blk],
        out_specs=pl.BlockSpec((None, POOL_GROUP, POOL_GROUP), lambda g, k: (g, 0, 0)),
        out_shape=jax.ShapeDtypeStruct((4, POOL_GROUP, POOL_GROUP), F32),
        compiler_params=_params(),
    )(pooled, dmixed)


def _local_step(x, target, norm_g, b_gate, pool_w, pool_scale, final_g, w_in, w_pu, w_au, w_out):
    n_layers = norm_g.shape[0]
    saved = []
    for l in range(n_layers):
        g = norm_g[l][None]
        bg = b_gate[l][None]
        sc = pool_scale[l][None]
        u, zp, q, k, v, za, gl, h = _rms_inproj(x, g, w_in, l)
        o, carry = _attn_fwd(q, k, v, l)
        saved.append((x, g, bg, sc, u, zp, q, k, v, za, gl, h, o, carry))
        x = _post_fwd(x, u, zp, o, za, gl, bg, pool_w[l], sc, w_pu, w_au, w_out, l)
    loss, dx, d_final_g = _loss_head(x, final_g[None], target)

    small = [None] * n_layers
    dw_in = dw_out = dw_pu = dw_au = None
    for l in reversed(range(n_layers)):
        x_in, g, bg, sc, u, zp, q, k, v, za, gl, h, o, carry = saved[l]
        (duz, do, dza, dgl, dsc, dbg, merged, dup, dua, y_pool, y_attn, pooled, dmixed) = _post_bwd(
            dx, u, zp, o, za, gl, bg, pool_w[l], sc, w_pu, w_au, w_out, l)
        dq, dk, dv = _attn_bwd(q, k, v, carry, do, l)
        pieces = [(duz, C_U), (dq, C_Q), (dk, C_K), (dv, C_V), (dza, C_ZA), (dgl, C_GL)]
        for p, c0 in pieces:
            dw_in = _wgrad(h, p, f"wgrad_in_l{l}_c{c0}", l, into=dw_in, col0=c0, n_total=IN_WIDTH)
        dw_out = _wgrad(merged, dx, f"wgrad_out_l{l}", l, into=dw_out)
        dw_pu = _wgrad(y_pool, dup, f"wgrad_pu_l{l}", l, into=dw_pu)
        dw_au = _wgrad(y_attn, dua, f"wgrad_au_l{l}", l, into=dw_au)
        dpw = _pool_wgrad(pooled, dmixed, l)
        dx, dg = _inproj_bwd(pieces, w_in, x_in, g, dx, l)
        small[l] = (dg[0], dbg[0], dpw, dsc[0])
    small = [jnp.stack([small[l][i] for l in range(n_layers)]) for i in range(4)]
    return loss[0, 0], dx, d_final_g[0], small, (dw_in, dw_pu, dw_au, dw_out)


SHARDED = ((2, 1280), (2, 256), (2, 256), (1, 256))
ANY = pl.BlockSpec(memory_space=pl.ANY)


def _part(ref, s, axis, width):
    sl = pl.ds(pl.multiple_of(s * width, width), width)
    return ref.at[:, sl] if axis == 2 else ref.at[sl, :]


def _place():
    x, y, c = lax.axis_index("x"), lax.axis_index("y"), lax.axis_index("c")
    return x, y, c, 2 * x + y


def _other_chip(x, y, m):
    px = 1 - x if m & 2 else x
    py = 1 - y if m & 1 else y
    return px, py, 2 * px + py


def _remote(src, dst, send, recv, k, to):
    return pltpu.make_async_remote_copy(src_ref=src, dst_ref=dst, send_sem=send.at[k], recv_sem=recv.at[k],
                                        device_id=to, device_id_type=MESH)


def _gather_weights(shards):
    n = len(SHARDED)

    def body(*refs):
        ins, outs = refs[:n], refs[n:2 * n]
        send, recv, local = refs[2 * n:]
        x, y, c, s = _place()
        me, sibling = (x, y, c), (x, y, 1 - c)
        own = []
        for a, (axis, width) in enumerate(SHARDED):
            for l in range(2):
                cp = pltpu.make_async_copy(ins[a].at[l], _part(outs[a].at[l], s, axis, width), local.at[2 * a + l])
                cp.start()
                own.append(cp)

        def copy(k, a, layer, shard, to, src=None):
            dst = _part(outs[a].at[layer], shard, *SHARDED[a])
            return _remote(dst if src is None else src, dst, send, recv, k, to)

        sent = []
        for a in range(n):
            for m in (1, 2, 3):
                px, py, _ = _other_chip(x, y, m)
                sent.append(copy(3 * a + m - 1, a, c, s, (px, py, c), src=ins[a].at[c]))
                sent[-1].start()
        for m in (1, 2, 3):
            _, _, sp = _other_chip(x, y, m)
            for a in range(n):
                copy(3 * a + m - 1, a, c, sp, me).wait_recv()
                sent.append(copy(3 * n + 3 * a + m - 1, a, c, sp, sibling))
                sent[-1].start()
        for m in (1, 2, 3):
            _, _, sp = _other_chip(x, y, m)
            for a in range(n):
                copy(3 * n + 3 * a + m - 1, a, 1 - c, sp, me).wait_recv()
        for cp in sent:
            cp.wait_send()
        for cp in own:
            cp.wait()

    def full(sh, axis):
        shape = list(sh.shape)
        shape[axis] *= N_CHIPS
        return jax.ShapeDtypeStruct(tuple(shape), sh.dtype)

    return pl.pallas_call(
        body, name="gather_weights",
        in_specs=[ANY] * n, out_specs=[ANY] * n,
        out_shape=[full(sh, axis) for sh, (axis, _) in zip(shards, SHARDED)],
        scratch_shapes=[pltpu.SemaphoreType.DMA((6 * n,)), pltpu.SemaphoreType.DMA((6 * n,)),
                        pltpu.SemaphoreType.DMA((2 * n,))],
    )(*shards)


def _sibling_exchange(dws):
    n = len(dws)

    def body(*refs):
        ins, outs = refs[:n], refs[n:2 * n]
        send, recv = refs[2 * n:]
        x, y, c, _ = _place()
        copies = [_remote(ins[a].at[1 - c], outs[a], send, recv, a, (x, y, 1 - c)) for a in range(n)]
        for cp in copies:
            cp.start()
        for cp in copies:
            cp.wait()

    return pl.pallas_call(
        body, name="grad_sibling_exchange",
        in_specs=[ANY] * n, out_specs=[ANY] * n,
        out_shape=[jax.ShapeDtypeStruct(d.shape[1:], d.dtype) for d in dws],
        scratch_shapes=[pltpu.SemaphoreType.DMA((n,)), pltpu.SemaphoreType.DMA((n,))],
    )(*dws)


def _pair_sum(dw, other, c_arr, name):
    _, R, C = dw.shape
    tr = 128 if C > 1024 else 256

    def body(c_ref, a_ref, b_ref, o_ref):
        o_ref[...] = a_ref[...] + b_ref[...]

    return pl.pallas_call(
        body, name=name,
        grid_spec=pltpu.PrefetchScalarGridSpec(
            num_scalar_prefetch=1, grid=(R // tr,),
            in_specs=[pl.BlockSpec((None, tr, C), lambda i, c_ref: (c_ref[0], i, 0)),
                      pl.BlockSpec((tr, C), lambda i, c_ref: (i, 0))],
            out_specs=pl.BlockSpec((tr, C), lambda i, c_ref: (i, 0))),
        out_shape=jax.ShapeDtypeStruct((R, C), dw.dtype),
        compiler_params=_params(),
    )(c_arr, dw, other)


def _chip_exchange(ps):
    n = len(ps)

    def body(*refs):
        ins, outs = refs[:n], refs[n:2 * n]
        send, recv = refs[2 * n:]
        x, y, c, _ = _place()
        copies = []
        for a, (axis, width) in enumerate(SHARDED):
            for m in (1, 2, 3):
                px, py, sp = _other_chip(x, y, m)
                copies.append(_remote(_part(ins[a], sp, axis, width), outs[a].at[m - 1], send, recv, 3 * a + m - 1,
                                      (px, py, c)))
        for cp in copies:
            cp.start()
        for cp in copies:
            cp.wait()

    def landing(p, axis, width):
        shape = [3] + list(p.shape)
        shape[axis] = width
        return jax.ShapeDtypeStruct(tuple(shape), p.dtype)

    return pl.pallas_call(
        body, name="grad_chip_exchange",
        in_specs=[ANY] * n, out_specs=[ANY] * n,
        out_shape=[landing(p, axis, width) for p, (axis, width) in zip(ps, SHARDED)],
        scratch_shapes=[pltpu.SemaphoreType.DMA((3 * n,)), pltpu.SemaphoreType.DMA((3 * n,))],
    )(*ps)


def _shard_sum(p, landed, s_arr, axis, width, name):
    R, C = p.shape
    _, Rs, Cs = landed.shape
    tr = min(256, Rs)
    if axis == 2:
        p_spec = pl.BlockSpec((tr, width), lambda i, s_ref: (i, s_ref[0]))
    else:
        p_spec = pl.BlockSpec((tr, C), lambda i, s_ref: (s_ref[0] * (width // tr) + i, 0))

    def body(s_ref, p_ref, l_ref, o_ref):
        o_ref[...] = ((p_ref[...] + l_ref[0]) + l_ref[1]) + l_ref[2]

    return pl.pallas_call(
        body, name=name,
        grid_spec=pltpu.PrefetchScalarGridSpec(
            num_scalar_prefetch=1, grid=(Rs // tr,),
            in_specs=[p_spec, pl.BlockSpec((3, tr, Cs), lambda i, s_ref: (0, i, 0))],
            out_specs=pl.BlockSpec((tr, Cs), lambda i, s_ref: (i, 0))),
        out_shape=jax.ShapeDtypeStruct((Rs, Cs), p.dtype),
        compiler_params=_params(),
    )(s_arr, p, landed)


def _sibling_share(fs):
    n = len(fs)

    def body(*refs):
        ins, outs = refs[:n], refs[n:2 * n]
        send, recv, local = refs[2 * n:]
        x, y, c, _ = _place()
        own = [pltpu.make_async_copy(ins[a], outs[a].at[c], local.at[a]) for a in range(n)]
        copies = [_remote(ins[a], outs[a].at[c], send, recv, a, (x, y, 1 - c)) for a in range(n)]
        for cp in own + copies:
            cp.start()
        for cp in copies + own:
            cp.wait()

    return pl.pallas_call(
        body, name="grad_sibling_share",
        in_specs=[ANY] * n, out_specs=[ANY] * n,
        out_shape=[jax.ShapeDtypeStruct((2,) + f.shape, f.dtype) for f in fs],
        scratch_shapes=[pltpu.SemaphoreType.DMA((n,)), pltpu.SemaphoreType.DMA((n,)), pltpu.SemaphoreType.DMA((n,))],
    )(*fs)


N_DEVICES = 8


def _all_reduce_small(packed):
    rows = packed.shape[0]

    def body(in_ref, out_ref, all_ref, send, recv):
        x, y, c, _ = _place()
        my_id = 4 * x + 2 * y + c
        all_ref[my_id] = in_ref[...]
        copies = []
        for m in range(1, N_DEVICES):
            px = 1 - x if m & 4 else x
            py = 1 - y if m & 2 else y
            pc = 1 - c if m & 1 else c
            cp = _remote(in_ref, all_ref.at[my_id], send, recv, m - 1, (px, py, pc))
            cp.start()
            copies.append((cp, 4 * px + 2 * py + pc))
        for m, (cp, peer_id) in enumerate(copies):
            _remote(in_ref, all_ref.at[peer_id], send, recv, m, (x, y, c)).wait_recv()
        for cp, _ in copies:
            cp.wait_send()
        total = all_ref[0]
        for d in range(1, N_DEVICES):
            total = total + all_ref[d]
        out_ref[...] = total

    vmem = pl.BlockSpec(memory_space=pltpu.VMEM)
    return pl.pallas_call(
        body, name="small_all_reduce",
        in_specs=[vmem], out_specs=vmem,
        out_shape=jax.ShapeDtypeStruct(packed.shape, packed.dtype),
        scratch_shapes=[pltpu.VMEM((N_DEVICES, rows, LANES), F32), pltpu.SemaphoreType.DMA((N_DEVICES - 1,)),
                        pltpu.SemaphoreType.DMA((N_DEVICES - 1,))],
        compiler_params=_params(),
    )(packed)


def _adamw(w, g, m, v, name):
    shape = w.shape
    C = shape[-1]
    flat = [t.reshape(-1, C) for t in (w, g, m, v)]
    R = flat[0].shape[0]
    tr = max(t for t in range(8, R + 1, 8) if R % t == 0 and t * C <= 384 * 1024)

    def body(w_ref, g_ref, m_ref, v_ref, d_ref, nm_ref, nv_ref):
        gv = g_ref[...]
        nm = ADAM_B1 * m_ref[...] + (1.0 - ADAM_B1) * gv
        nv = ADAM_B2 * v_ref[...] + (1.0 - ADAM_B2) * (gv * gv)
        m_hat = nm / (1.0 - ADAM_B1 ** ADAM_STEP)
        v_hat = nv / (1.0 - ADAM_B2 ** ADAM_STEP)
        d_ref[...] = -ADAM_LR * (m_hat / (jnp.sqrt(v_hat) + ADAM_EPS) + ADAM_WD * w_ref[...])
        nm_ref[...] = nm
        nv_ref[...] = nv

    blk = pl.BlockSpec((tr, C), lambda i: (i, 0))
    out = jax.ShapeDtypeStruct((R, C), F32)
    res = pl.pallas_call(
        body, name=name, grid=(R // tr,),
        in_specs=[blk] * 4, out_specs=[blk] * 3, out_shape=[out] * 3,
        compiler_params=_params(),
    )(*flat)
    return [t.reshape(shape) for t in res]


SMALL_SHAPES = ((2, 1024), (2, 2048), (2, 4, 128, 128), (2, 512), (1024,))


def _pack_small(parts):
    return jnp.concatenate([p.reshape(-1, LANES) for p in parts], axis=0)


def _unpack_small(packed):
    out, row = [], 0
    for shape in SMALL_SHAPES:
        n = 1
        for d in shape:
            n *= d
        out.append(packed[row:row + n // LANES].reshape(shape))
        row += n // LANES
    return out


def kernel(x, norm_g, w_in, b_gate, pool_w, pool_scale, w_pool_up, w_attn_up, w_out, final_g, loss_target, m_norm_g, m_w_in, m_b_gate, m_pool_w, m_pool_scale, m_w_pool_up, m_w_attn_up, m_w_out, m_final_g, v_norm_g, v_w_in, v_b_gate, v_pool_w, v_pool_scale, v_w_pool_up, v_w_attn_up, v_w_out, v_final_g):
    _, _, c, s = _place()
    c_arr = jnp.reshape(c, (1,)).astype(jnp.int32)
    s_arr = jnp.reshape(s, (1,)).astype(jnp.int32)

    full = _gather_weights([w.astype(BF16) for w in (w_in, w_pool_up, w_attn_up, w_out)])
    loss_part, dx, d_final_g, small, dws = _local_step(x[0], loss_target[0], norm_g, b_gate, pool_w, pool_scale,
                                                       final_g, *full)
    loss = lax.psum(loss_part, ("x", "y", "c"))

    names = ("w_in", "w_pool_up", "w_attn_up", "w_out")
    other = _sibling_exchange(dws)
    pair = [_pair_sum(d, o, c_arr, f"grad_pair_sum_{n}") for d, o, n in zip(dws, other, names)]
    landed = _chip_exchange(pair)
    mine = [_shard_sum(p, l, s_arr, axis, width, f"grad_shard_sum_{n}")
            for p, l, (axis, width), n in zip(pair, landed, SHARDED, names)]
    g_in, g_pu, g_au, g_out = _sibling_share(mine)

    g_small = _unpack_small(_all_reduce_small(_pack_small(small + [d_final_g])))
    upd_small = _adamw(_pack_small([norm_g, b_gate, pool_w, pool_scale, final_g]), _pack_small(g_small),
                       _pack_small([m_norm_g, m_b_gate, m_pool_w, m_pool_scale, m_final_g]),
                       _pack_small([v_norm_g, v_b_gate, v_pool_w, v_pool_scale, v_final_g]), "adamw_small")
    d_small, nm_small, nv_small = [_unpack_small(t) for t in upd_small]
    upd_in = _adamw(w_in, g_in, m_w_in, v_w_in, "adamw_w_in")
    upd_pu = _adamw(w_pool_up, g_pu, m_w_pool_up, v_w_pool_up, "adamw_w_pool_up")
    upd_au = _adamw(w_attn_up, g_au, m_w_attn_up, v_w_attn_up, "adamw_w_attn_up")
    upd_out = _adamw(w_out, g_out, m_w_out, v_w_out, "adamw_w_out")

    def ordered(sm, k):
        big = (upd_in[k], upd_pu[k], upd_au[k], upd_out[k]) if k is not None else (g_in, g_pu, g_au, g_out)
        return [sm[0], big[0], sm[1], sm[2], sm[3], big[1], big[2], big[3], sm[4]]

    return (loss, dx[None], *ordered(g_small, None), *ordered(d_small, 0), *ordered(nm_small, 1),
            *ordered(nv_small, 2))
```

```python
import functools

import jax
import jax.numpy as jnp
from jax import lax
from jax.experimental import pallas as pl
from jax.experimental.pallas import tpu as pltpu

F32 = jnp.float32
BF16 = jnp.bfloat16
MESH = pl.DeviceIdType.MESH

D_MODEL = 1024
POOL_WIDTH = 512
POOL_WINDOWS = (2, 4, 8, 16)
POOL_GROUP = 128
POOL_HALO = 16
ATTN_WIDTH = 512
HEAD_DIM = 64
HEAD_PAIRS = 4
IN_WIDTH = 5120
N_CHIPS = 4
RMS_EPS = 1e-6
C_U, C_ZP, C_Q, C_K, C_V, C_ZA, C_GL = 0, 512, 1024, 1536, 2048, 2560, 3072

ADAM_LR, ADAM_B1, ADAM_B2, ADAM_EPS, ADAM_WD, ADAM_STEP = 0.001, 0.9, 0.999, 1e-08, 0.01, 10

LANES = 128
ATTN_BLOCK = 128
ROW_TILE = 256
VMEM_LIMIT = 56 * 1024 * 1024


def _params(**kw):
    return pltpu.CompilerParams(vmem_limit_bytes=VMEM_LIMIT, **kw)


def _nt(a, b):
    return lax.dot_general(a, b, (((1,), (1,)), ((), ())), preferred_element_type=F32)


def _tn(a, b):
    return lax.dot_general(a, b, (((0,), (0,)), ((), ())), preferred_element_type=F32)


def _nn(a, b):
    return jnp.dot(a, b, preferred_element_type=F32)


def _sigmoid(z):
    return 1.0 / (1.0 + jnp.exp(-z))


def _rms_inproj(x, g, w_in, layer):
    S = x.shape[0]
    tm = min(ROW_TILE, S)

    def body(x_ref, g_ref, w_ref, u_ref, zp_ref, q_ref, k_ref, v_ref, za_ref, gl_ref, h_ref):
        xv = x_ref[...]
        r = lax.rsqrt(jnp.mean(xv * xv, axis=-1, keepdims=True) + RMS_EPS)
        h = ((xv * r) * g_ref[...]).astype(BF16)
        h_ref[...] = h

        def mm(c0, n):
            return _nn(h, w_ref[:, c0:c0 + n])

        u_ref[...] = mm(C_U, 512)
        zp_ref[...] = mm(C_ZP, 512)
        q_ref[...] = (mm(C_Q, 512) * 0.125).astype(BF16)
        k_ref[...] = mm(C_K, 512).astype(BF16)
        v_ref[...] = mm(C_V, 512).astype(BF16)
        za_ref[...] = mm(C_ZA, 512)
        for c in range(4):
            gl_ref[:, c * 512:(c + 1) * 512] = mm(C_GL + c * 512, 512)

    row = lambda n: pl.BlockSpec((tm, n), lambda i: (i, 0))
    sd = lambda n, dt: jax.ShapeDtypeStruct((S, n), dt)
    return pl.pallas_call(
        body, name=f"rms_inproj_l{layer}", grid=(S // tm,),
        in_specs=[row(D_MODEL), pl.BlockSpec((1, D_MODEL), lambda i: (0, 0)),
                  pl.BlockSpec((None, D_MODEL, IN_WIDTH), lambda i: (layer, 0, 0))],
        out_specs=[row(512), row(512), row(512), row(512), row(512), row(512), row(2048), row(D_MODEL)],
        out_shape=[sd(512, F32), sd(512, F32), sd(512, BF16), sd(512, BF16), sd(512, BF16), sd(512, F32),
                   sd(2048, F32), sd(D_MODEL, BF16)],
        compiler_params=_params(),
    )(x, g, w_in)


def _tri(n, strict_lower):
    r = lax.broadcasted_iota(jnp.int32, (n, n), 0)
    c = lax.broadcasted_iota(jnp.int32, (n, n), 1)
    return jnp.where(r > c if strict_lower else r < c, 1.0, 0.0).astype(BF16)


def _split_dot(x, m):
    hi = x.astype(BF16)
    lo = (x - hi.astype(F32)).astype(BF16)
    return _nn(hi, m) + _nn(lo, m)


def _log_terms(z):
    lg = jnp.log(1.0 + jnp.exp(-jnp.abs(z)))
    a = jnp.minimum(z, 0.0) - lg
    return a, a - z


EXHAUSTED = -104.0
UNREACHED = -1e30


def _attn_fwd(q, k, v, layer):
    S = q.shape[0]
    T = min(ATTN_BLOCK, S)
    nq = S // T
    assert nq <= LANES

    def body(q_ref, k_ref, v_ref, o_ref, c_ref):
        qi = pl.program_id(1)
        lane = lax.broadcasted_iota(jnp.int32, (T, LANES), 1)
        first = lane < HEAD_DIM
        q2 = q_ref[...]
        qh = (jnp.where(first, q2, 0).astype(BF16), jnp.where(first, 0, q2).astype(BF16))
        below = _tri(T, True)
        causal = lax.broadcasted_iota(jnp.int32, (T, T), 0) > lax.broadcasted_iota(jnp.int32, (T, T), 1)

        def tile(j, carry, masked):
            start = pl.multiple_of(j * T, T)
            kk = k_ref[pl.ds(start, T), :]
            vv = v_ref[pl.ds(start, T), :]
            new = []
            for h in range(2):
                acc, run, saved = carry[3 * h:3 * h + 3]
                z = _nt(qh[h], kk)
                a, l1m = _log_terms(z)
                if masked:
                    l1m = jnp.where(causal, l1m, 0.0)
                w = jnp.exp(a + _split_dot(l1m, below) + run)
                if masked:
                    w = jnp.where(causal, w, 0.0)
                saved = jnp.where(lane == j, run, saved)
                acc = acc + _nn(w.astype(BF16), vv)
                run = run + jnp.sum(l1m, axis=1, keepdims=True)
                new += [acc, run, saved]
            return tuple(new)

        def alive(carry):
            return (jnp.max(jnp.maximum(carry[1], carry[4])) > EXHAUSTED).astype(jnp.int32)

        def older_block(state):
            j, _, carry = state
            carry = tile(j, carry, False)
            return j - 1, alive(carry), carry

        zero = jnp.zeros((T, LANES), F32)
        col = jnp.zeros((T, 1), F32)
        never = jnp.full((T, LANES), UNREACHED, F32)
        carry = tile(qi, (zero, col, never, zero, col, never), True)
        carry = lax.while_loop(lambda st: jnp.logical_and(st[0] >= 0, st[1] > 0), older_block,
                               (qi - 1, alive(carry), carry))[2]
        o_ref[...] = jnp.where(first, carry[0], carry[3])
        c_ref[:, :LANES] = carry[2]
        c_ref[:, LANES:] = carry[5]

    return pl.pallas_call(
        body, name=f"attn_fwd_l{layer}", grid=(HEAD_PAIRS, nq),
        in_specs=[pl.BlockSpec((T, LANES), lambda p, i: (i, p)),
                  pl.BlockSpec((S, LANES), lambda p, i: (0, p)),
                  pl.BlockSpec((S, LANES), lambda p, i: (0, p))],
        out_specs=[pl.BlockSpec((T, LANES), lambda p, i: (i, p)),
                   pl.BlockSpec((T, 2 * LANES), lambda p, i: (i, p))],
        out_shape=[jax.ShapeDtypeStruct((S, ATTN_WIDTH), F32), jax.ShapeDtypeStruct((S, 8 * LANES), F32)],
        compiler_params=_params(),
    )(q, k, v)


def _attn_bwd(q, k, v, saved, do, layer):
    S = q.shape[0]
    T = min(ATTN_BLOCK, S)
    nq = S // T

    def body(q_ref, k_ref, v_ref, c_ref, do_ref, dq_ref, dk_ref, dv_ref):
        qi = pl.program_id(1)

        @pl.when(qi == 0)
        def _():
            dk_ref[...] = jnp.zeros_like(dk_ref)
            dv_ref[...] = jnp.zeros_like(dv_ref)

        lane = lax.broadcasted_iota(jnp.int32, (T, LANES), 1)
        first = lane < HEAD_DIM
        q2 = q_ref[...]
        qh = (jnp.where(first, q2, 0).astype(BF16), jnp.where(first, 0, q2).astype(BF16))
        dob = do_ref[...].astype(BF16)
        doh = (jnp.where(first, dob, 0).astype(BF16), jnp.where(first, 0, dob).astype(BF16))
        saved_h = (c_ref[:, :LANES], c_ref[:, LANES:])
        below = _tri(T, True)
        before = _tri(T, False)
        causal = lax.broadcasted_iota(jnp.int32, (T, T), 0) > lax.broadcasted_iota(jnp.int32, (T, T), 1)

        def tile(j, carry, masked):
            start = pl.multiple_of(j * T, T)
            kk = k_ref[pl.ds(start, T), :]
            vv = v_ref[pl.ds(start, T), :]
            new, dks, dvs = [], [], []
            for h in range(2):
                dq, older = carry[2 * h:2 * h + 2]
                z = _nt(qh[h], kk)
                a, l1m = _log_terms(z)
                if masked:
                    l1m = jnp.where(causal, l1m, 0.0)
                run = jnp.sum(jnp.where(lane == j, saved_h[h], 0.0), axis=1, keepdims=True)
                w = jnp.exp(a + _split_dot(l1m, below) + run)
                if masked:
                    w = jnp.where(causal, w, 0.0)
                e = w * _nt(doh[h], vv)
                prefix = _split_dot(e, before) + older
                dz = e - jnp.exp(a) * (e + prefix)
                if masked:
                    dz = jnp.where(causal, dz, 0.0)
                dzb = dz.astype(BF16)
                dq = dq + _nn(dzb, kk)
                dks.append(_tn(dzb, q2))
                dvs.append(_tn(w.astype(BF16), dob))
                older = older + jnp.sum(e, axis=1, keepdims=True)
                new += [dq, older]
            dk_ref[pl.ds(start, T), :] += jnp.where(first, dks[0], dks[1])
            dv_ref[pl.ds(start, T), :] += jnp.where(first, dvs[0], dvs[1])
            return tuple(new)

        col_max = jnp.max(jnp.maximum(saved_h[0], saved_h[1]), axis=0, keepdims=True)
        lane_row = lax.broadcasted_iota(jnp.int32, (1, LANES), 1)
        reached = jnp.sum(jnp.where(jnp.logical_and(col_max > EXHAUSTED, lane_row < qi), 1, 0))
        zero = jnp.zeros((T, LANES), F32)
        col = jnp.zeros((T, 1), F32)
        carry = lax.fori_loop(qi - reached, qi, lambda j, c: tile(j, c, False), (zero, col, zero, col))
        carry = tile(qi, carry, True)
        dq_ref[...] = jnp.where(first, carry[0], carry[2]) * 0.125

    blk = pl.BlockSpec((T, LANES), lambda p, i: (i, p))
    full = pl.BlockSpec((S, LANES), lambda p, i: (0, p))
    out = jax.ShapeDtypeStruct((S, ATTN_WIDTH), F32)
    return pl.pallas_call(
        body, name=f"attn_bwd_l{layer}", grid=(HEAD_PAIRS, nq),
        in_specs=[blk, full, full, pl.BlockSpec((T, 2 * LANES), lambda p, i: (i, p)), blk],
        out_specs=[blk, full, full],
        out_shape=[out, out, out],
        compiler_params=_params(),
    )(q, k, v, saved, do)


def _pool_counts(row0, tm):
    pos = row0 + lax.broadcasted_iota(jnp.int32, (tm, 1), 0)
    return [1.0 / jnp.minimum(pos + 1, w).astype(F32) for w in POOL_WINDOWS]


def _post_forward(ext_ref, inv_cnt, zp, o, za, gl, bg, pw_ref, scale, wpu_ref, wau_ref, tm):
    pooled, mixed = [], []
    for g, w in enumerate(POOL_WINDOWS):
        cols = slice(g * POOL_GROUP, (g + 1) * POOL_GROUP)
        tot = ext_ref[POOL_HALO:POOL_HALO + tm, cols]
        cur = tot
        for d in range(1, w):
            tot = tot + ext_ref[POOL_HALO - d:POOL_HALO - d + tm, cols]
        pg = (tot * inv_cnt[g] - cur).astype(BF16)
        pooled.append(pg)
        mixed.append(_nn(pg, pw_ref[g].astype(BF16)))
    pooled = jnp.concatenate(pooled, axis=1)
    mixed = jnp.concatenate(mixed, axis=1)
    sp = _sigmoid(zp)
    sa = _sigmoid(za)
    y_pool = (mixed * scale) * (zp * sp)
    y_attn = o * (za * sa)
    gate = _sigmoid(gl + bg)
    g0, g1 = gate[:, :D_MODEL], gate[:, D_MODEL:]
    up_p = _nn(y_pool.astype(BF16), wpu_ref[...])
    up_a = _nn(y_attn.astype(BF16), wau_ref[...])
    merged = g0 * up_p + g1 * up_a
    return pooled, mixed, sp, sa, y_pool, y_attn, g0, g1, up_p, up_a, merged


def _row_specs(tm, rev, n_tiles):
    tile_of = (lambda i: n_tiles - 1 - i) if rev else (lambda i: i)
    row = lambda n: pl.BlockSpec((tm, n), lambda i: (tile_of(i), 0))
    halo = pl.BlockSpec((POOL_HALO, POOL_WIDTH),
                        lambda i: (jnp.maximum(tile_of(i) * (tm // POOL_HALO) - 1, 0), 0))
    const = lambda shape: pl.BlockSpec(shape, lambda i: (0,) * len(shape))
    return tile_of, row, halo, const


def _layer_weight_spec(rows, cols, layer):
    return pl.BlockSpec((None, rows, cols), lambda i: (layer, 0, 0))


def _post_fwd(x, u, zp, o, za, gl, bg, pw, scale, wpu, wau, wout, layer):
    S = x.shape[0]
    tm = min(ROW_TILE, S)
    n_tiles = S // tm
    tile_of, row, halo, const = _row_specs(tm, False, n_tiles)

    def body(x_ref, u_ref, uh_ref, zp_ref, o_ref, za_ref, gl_ref, bg_ref, pw_ref, sc_ref, wpu_ref, wau_ref, wout_ref,
             out_ref, ext_ref):
        i = pl.program_id(0)
        ext_ref[:POOL_HALO, :] = jnp.where(i == 0, 0.0, uh_ref[...])
        ext_ref[POOL_HALO:, :] = u_ref[...]
        vals = _post_forward(ext_ref, _pool_counts(i * tm, tm), zp_ref[...], o_ref[...], za_ref[...], gl_ref[...],
                             bg_ref[...], pw_ref, sc_ref[...], wpu_ref, wau_ref, tm)
        out_ref[...] = x_ref[...] + _nn(vals[-1].astype(BF16), wout_ref[...])

    return pl.pallas_call(
        body, name=f"post_fwd_l{layer}", grid=(n_tiles,),
        in_specs=[row(D_MODEL), row(512), halo, row(512), row(512), row(512), row(2048), const((1, 2048)),
                  const((4, POOL_GROUP, POOL_GROUP)), const((1, POOL_WIDTH)),
                  _layer_weight_spec(POOL_WIDTH, D_MODEL, layer), _layer_weight_spec(ATTN_WIDTH, D_MODEL, layer),
                  _layer_weight_spec(D_MODEL, D_MODEL, layer)],
        out_specs=row(D_MODEL),
        out_shape=jax.ShapeDtypeStruct((S, D_MODEL), F32),
        scratch_shapes=[pltpu.VMEM((POOL_HALO + tm, POOL_WIDTH), F32)],
        compiler_params=_params(),
    )(x, u, u, zp, o, za, gl, bg, pw, scale, wpu, wau, wout)


def _post_bwd(dx, u, zp, o, za, gl, bg, pw, scale, wpu, wau, wout, layer):
    S = dx.shape[0]
    tm = min(ROW_TILE // 2, S)
    n_tiles = S // tm
    tile_of, row, halo, const = _row_specs(tm, True, n_tiles)

    def body(dx_ref, u_ref, uh_ref, zp_ref, o_ref, za_ref, gl_ref, bg_ref, pw_ref, sc_ref, wpu_ref, wau_ref, wout_ref,
             duz_ref, do_ref, dza_ref, dgl_ref, dsc_ref, dbg_ref,
             merged_ref, dup_ref, dua_ref, yp_ref, ya_ref, pooled_ref, dmixed_ref, ext_ref, nxt_ref):
        step = pl.program_id(0)
        i = tile_of(step)

        @pl.when(step == 0)
        def _():
            dsc_ref[...] = jnp.zeros_like(dsc_ref)
            dbg_ref[...] = jnp.zeros_like(dbg_ref)
            nxt_ref[tm:, :] = jnp.zeros((POOL_HALO, POOL_WIDTH), F32)

        ext_ref[:POOL_HALO, :] = jnp.where(i == 0, 0.0, uh_ref[...])
        ext_ref[POOL_HALO:, :] = u_ref[...]
        inv_cnt = _pool_counts(i * tm, tm)
        zp, za, o = zp_ref[...], za_ref[...], o_ref[...]
        pooled, mixed, sp, sa, y_pool, y_attn, g0, g1, up_p, up_a, merged = _post_forward(
            ext_ref, inv_cnt, zp, o, za, gl_ref[...], bg_ref[...], pw_ref, sc_ref[...], wpu_ref, wau_ref, tm)
        merged_ref[...] = merged.astype(BF16)
        yp_ref[...] = y_pool.astype(BF16)
        ya_ref[...] = y_attn.astype(BF16)
        pooled_ref[...] = pooled

        dmerged = _nt(dx_ref[...].astype(BF16), wout_ref[...])
        dup = (dmerged * g0).astype(BF16)
        dua = (dmerged * g1).astype(BF16)
        dup_ref[...] = dup
        dua_ref[...] = dua
        dgl0 = (dmerged * up_p) * (g0 * (1.0 - g0))
        dgl1 = (dmerged * up_a) * (g1 * (1.0 - g1))
        dgl_ref[:, :D_MODEL] = dgl0
        dgl_ref[:, D_MODEL:] = dgl1
        dbg_ref[:, :D_MODEL] += jnp.sum(dgl0, axis=0, keepdims=True)
        dbg_ref[:, D_MODEL:] += jnp.sum(dgl1, axis=0, keepdims=True)

        dy_attn = _nt(dua, wau_ref[...])
        do_ref[...] = dy_attn * (za * sa)
        dza_ref[...] = (dy_attn * o) * (sa * (1.0 + za * (1.0 - sa)))

        dy_pool = _nt(dup, wpu_ref[...])
        ms = mixed * sc_ref[...]
        dms = dy_pool * (zp * sp)
        duz_ref[:, POOL_WIDTH:] = (dy_pool * ms) * (sp * (1.0 + zp * (1.0 - sp)))
        dsc_ref[...] += jnp.sum(dms * mixed, axis=0, keepdims=True)
        dmixed = (dms * sc_ref[...]).astype(BF16)
        dmixed_ref[...] = dmixed
        for g, w in enumerate(POOL_WINDOWS):
            cols = slice(g * POOL_GROUP, (g + 1) * POOL_GROUP)
            dpg = _nt(dmixed[:, cols], pw_ref[g].astype(BF16))
            nxt_ref[:tm, cols] = dpg * inv_cnt[g]
            tot = -dpg
            for d in range(w):
                tot = tot + nxt_ref[d:d + tm, cols]
            duz_ref[:, cols] = tot
        nxt_ref[tm:, :] = nxt_ref[:POOL_HALO, :]

    sd = lambda n, dt: jax.ShapeDtypeStruct((S, n), dt)
    return pl.pallas_call(
        body, name=f"post_bwd_l{layer}", grid=(n_tiles,),
        in_specs=[row(D_MODEL), row(512), halo, row(512), row(512), row(512), row(2048), const((1, 2048)),
                  const((4, POOL_GROUP, POOL_GROUP)), const((1, POOL_WIDTH)),
                  _layer_weight_spec(POOL_WIDTH, D_MODEL, layer), _layer_weight_spec(ATTN_WIDTH, D_MODEL, layer),
                  _layer_weight_spec(D_MODEL, D_MODEL, layer)],
        out_specs=[row(1024), row(512), row(512), row(2048), const((1, POOL_WIDTH)), const((1, 2048)),
                   row(D_MODEL), row(D_MODEL), row(D_MODEL), row(512), row(512), row(512), row(512)],
        out_shape=[sd(1024, F32), sd(512, F32), sd(512, F32), sd(2048, F32),
                   jax.ShapeDtypeStruct((1, POOL_WIDTH), F32), jax.ShapeDtypeStruct((1, 2048), F32),
                   sd(D_MODEL, BF16), sd(D_MODEL, BF16), sd(D_MODEL, BF16), sd(512, BF16), sd(512, BF16),
                   sd(512, BF16), sd(512, BF16)],
        scratch_shapes=[pltpu.VMEM((POOL_HALO + tm, POOL_WIDTH), F32), pltpu.VMEM((tm + POOL_HALO, POOL_WIDTH), F32)],
        compiler_params=_params(),
    )(dx, u, u, zp, o, za, gl, bg, pw, scale, wpu, wau, wout)


def _rms_backward(dh, xv, r, g):
    xhat = xv * r
    dxhat = dh * g
    return r * (dxhat - xhat * jnp.mean(dxhat * xhat, axis=-1, keepdims=True)), dh * xhat


def _loss_head(x, g, target):
    S = x.shape[0]
    tm = min(ROW_TILE, S)

    def body(x_ref, g_ref, t_ref, loss_ref, dx_ref, dg_ref):
        @pl.when(pl.program_id(0) == 0)
        def _():
            loss_ref[...] = jnp.zeros_like(loss_ref)
            dg_ref[...] = jnp.zeros_like(dg_ref)

        xv = x_ref[...]
        r = lax.rsqrt(jnp.mean(xv * xv, axis=-1, keepdims=True) + RMS_EPS)
        diff = (xv * r) * g_ref[...] - t_ref[...]
        per_row = jnp.mean(diff * diff, axis=-1, keepdims=True)
        loss_ref[...] += 0.5 * jnp.sum(per_row, axis=0, keepdims=True)
        dx, dg_rows = _rms_backward(diff * (1.0 / D_MODEL), xv, r, g_ref[...])
        dx_ref[...] = dx
        dg_ref[...] += jnp.sum(dg_rows, axis=0, keepdims=True)

    row = pl.BlockSpec((tm, D_MODEL), lambda i: (i, 0))
    vec = pl.BlockSpec((1, D_MODEL), lambda i: (0, 0))
    return pl.pallas_call(
        body, name="loss_head", grid=(S // tm,),
        in_specs=[row, vec, row],
        out_specs=[pl.BlockSpec((1, LANES), lambda i: (0, 0)), row, vec],
        out_shape=[jax.ShapeDtypeStruct((1, LANES), F32), jax.ShapeDtypeStruct((S, D_MODEL), F32),
                   jax.ShapeDtypeStruct((1, D_MODEL), F32)],
        compiler_params=_params(),
    )(x, g, target)


def _inproj_bwd(pieces, w_in, x, g, dx_res, layer):
    S = x.shape[0]
    tm = min(ROW_TILE, S)
    cols = [(c0, p.shape[1]) for p, c0 in pieces]

    def body(*refs):
        piece_refs = refs[:len(cols)]
        w_ref, x_ref, g_ref, res_ref, dx_ref, dg_ref = refs[len(cols):]

        @pl.when(pl.program_id(0) == 0)
        def _():
            dg_ref[...] = jnp.zeros_like(dg_ref)

        dh = jnp.zeros((tm, D_MODEL), F32)
        for p_ref, (c0, n) in zip(piece_refs, cols):
            for c in range(0, n, 512):
                dh = dh + _nt(p_ref[:, c:c + 512].astype(BF16), w_ref[:, c0 + c:c0 + c + 512])
        xv = x_ref[...]
        r = lax.rsqrt(jnp.mean(xv * xv, axis=-1, keepdims=True) + RMS_EPS)
        dx, dg_rows = _rms_backward(dh, xv, r, g_ref[...])
        dx_ref[...] = res_ref[...] + dx
        dg_ref[...] += jnp.sum(dg_rows, axis=0, keepdims=True)

    row = lambda n: pl.BlockSpec((tm, n), lambda i: (i, 0))
    vec = pl.BlockSpec((1, D_MODEL), lambda i: (0, 0))
    return pl.pallas_call(
        body, name=f"inproj_bwd_l{layer}", grid=(S // tm,),
        in_specs=[row(n) for _, n in cols] + [_layer_weight_spec(D_MODEL, IN_WIDTH, layer), row(D_MODEL), vec,
                                              row(D_MODEL)],
        out_specs=[row(D_MODEL), vec],
        out_shape=[jax.ShapeDtypeStruct((S, D_MODEL), F32), jax.ShapeDtypeStruct((1, D_MODEL), F32)],
        compiler_params=_params(),
    )(*[p for p, _ in pieces], w_in, x, g, dx_res)


def _wgrad(a, b, name, layer, into=None, col0=0, n_total=None):
    S, M = a.shape
    N = b.shape[1]
    n_total = N if n_total is None else n_total
    tk = min(512, S)
    tn = min(512, N)
    nk = S // tk

    def body(*refs):
        a_ref, b_ref, out_ref = refs[0], refs[1], refs[-1]
        prod = _tn(a_ref[...].astype(BF16), b_ref[...].astype(BF16))

        @pl.when(pl.program_id(1) == 0)
        def _():
            out_ref[...] = prod

        @pl.when(pl.program_id(1) > 0)
        def _():
            out_ref[...] += prod

    in_specs = [pl.BlockSpec((tk, M), lambda j, k: (k, 0)), pl.BlockSpec((tk, tn), lambda j, k: (k, j))]
    args = [a, b]
    aliases = {}
    if into is not None:
        in_specs.append(pl.BlockSpec(memory_space=pl.ANY))
        args.append(into)
        aliases = {2: 0}
    return pl.pallas_call(
        body, name=name, grid=(N // tn, nk),
        in_specs=in_specs,
        out_specs=pl.BlockSpec((None, M, tn), lambda j, k: (layer, 0, col0 // tn + j)),
        out_shape=jax.ShapeDtypeStruct((2, M, n_total), F32),
        input_output_aliases=aliases,
        compiler_params=_params(),
    )(*args)


def _pool_wgrad(pooled, dmixed, layer):
    S = pooled.shape[0]
    tk = min(1024, S)

    def body(a_ref, b_ref, out_ref):
        prod = _tn(a_ref[...], b_ref[...])

        @pl.when(pl.program_id(1) == 0)
        def _():
            out_ref[...] = prod

        @pl.when(pl.program_id(1) > 0)
        def _():
            out_ref[...] += prod

    blk = pl.BlockSpec((tk, POOL_GROUP), lambda g, k: (k, g))
    return pl.pallas_call(
        body, name=f"pool_wgrad_l{layer}", grid=(4, S // tk),
        in_specs=[blk, blk],
        out_specs=pl.BlockSpec((None, POOL_GROUP, POOL_GROUP), lambda g, k: (g, 0, 0)),
        out_shape=jax.ShapeDtypeStruct((4, POOL_GROUP, POOL_GROUP), F32),
        compiler_params=_params(),
    )(pooled, dmixed)


def _local_step(x, target, norm_g, b_gate, pool_w, pool_scale, final_g, w_in, w_pu, w_au, w_out):
    n_layers = norm_g.shape[0]
    saved = []
    for l in range(n_layers):
        g = norm_g[l][None]
        bg = b_gate[l][None]
        sc = pool_scale[l][None]
        u, zp, q, k, v, za, gl, h = _rms_inproj(x, g, w_in, l)
        o, carry = _attn_fwd(q, k, v, l)
        saved.append((x, g, bg, sc, u, zp, q, k, v, za, gl, h, o, carry))
        x = _post_fwd(x, u, zp, o, za, gl, bg, pool_w[l], sc, w_pu, w_au, w_out, l)
    loss, dx, d_final_g = _loss_head(x, final_g[None], target)

    small = [None] * n_layers
    dw_in = dw_out = dw_pu = dw_au = None
    for l in reversed(range(n_layers)):
        x_in, g, bg, sc, u, zp, q, k, v, za, gl, h, o, carry = saved[l]
        (duz, do, dza, dgl, dsc, dbg, merged, dup, dua, y_pool, y_attn, pooled, dmixed) = _post_bwd(
            dx, u, zp, o, za, gl, bg, pool_w[l], sc, w_pu, w_au, w_out, l)
        dq, dk, dv = _attn_bwd(q, k, v, carry, do, l)
        pieces = [(duz, C_U), (dq, C_Q), (dk, C_K), (dv, C_V), (dza, C_ZA), (dgl, C_GL)]
        for p, c0 in pieces:
            dw_in = _wgrad(h, p, f"wgrad_in_l{l}_c{c0}", l, into=dw_in, col0=c0, n_total=IN_WIDTH)
        dw_out = _wgrad(merged, dx, f"wgrad_out_l{l}", l, into=dw_out)
        dw_pu = _wgrad(y_pool, dup, f"wgrad_pu_l{l}", l, into=dw_pu)
        dw_au = _wgrad(y_attn, dua, f"wgrad_au_l{l}", l, into=dw_au)
        dpw = _pool_wgrad(pooled, dmixed, l)
        dx, dg = _inproj_bwd(pieces, w_in, x_in, g, dx, l)
        small[l] = (dg[0], dbg[0], dpw, dsc[0])
    small = [jnp.stack([small[l][i] for l in range(n_layers)]) for i in range(4)]
    return loss[0, 0], dx, d_final_g[0], small, (dw_in, dw_pu, dw_au, dw_out)


SHARDED = ((2, 1280), (2, 256), (2, 256), (1, 256))
ANY = pl.BlockSpec(memory_space=pl.ANY)


def _part(ref, s, axis, width):
    sl = pl.ds(pl.multiple_of(s * width, width), width)
    return ref.at[:, sl] if axis == 2 else ref.at[sl, :]


def _place():
    x, y, c = lax.axis_index("x"), lax.axis_index("y"), lax.axis_index("c")
    return x, y, c, 2 * x + y


def _other_chip(x, y, m):
    px = 1 - x if m & 2 else x
    py = 1 - y if m & 1 else y
    return px, py, 2 * px + py


def _remote(src, dst, send, recv, k, to):
    return pltpu.make_async_remote_copy(src_ref=src, dst_ref=dst, send_sem=send.at[k], recv_sem=recv.at[k],
                                        device_id=to, device_id_type=MESH)


def _part_spec(tr, rows_s, cols_s, axis, width, lead):
    if axis == 2:
        return pl.BlockSpec((None, tr, width), lambda *a: (lead(a), a[-2], a[-1][1]))
    return pl.BlockSpec((None, tr, cols_s), lambda *a: (lead(a), a[-1][1] * (rows_s // tr) + a[-2], 0))


def _cast_into_place(w, pos, axis, width, name):
    L, Rs, Cs = w.shape
    tr = min(256, Rs)
    shape = [L, Rs, Cs]
    shape[axis] *= N_CHIPS

    def body(pos_ref, w_ref, o_ref):
        o_ref[...] = w_ref[...].astype(BF16)

    return pl.pallas_call(
        body, name=name,
        grid_spec=pltpu.PrefetchScalarGridSpec(
            num_scalar_prefetch=1, grid=(L, Rs // tr),
            in_specs=[pl.BlockSpec((None, tr, Cs), lambda l, i, pos: (l, i, 0))],
            out_specs=_part_spec(tr, Rs, Cs, axis, width, lambda a: a[0])),
        out_shape=jax.ShapeDtypeStruct(tuple(shape), BF16),
        compiler_params=_params(),
    )(pos, w)


def _gather_weights(fulls):
    n = len(SHARDED)

    def body(*refs):
        outs = refs[n:2 * n]
        send, recv = refs[2 * n:]
        x, y, c, s = _place()
        me, sibling = (x, y, c), (x, y, 1 - c)

        def copy(k, a, layer, shard, to):
            part = _part(outs[a].at[layer], shard, *SHARDED[a])
            return _remote(part, part, send, recv, k, to)

        sent = []
        for a in range(n):
            for m in (1, 2, 3):
                px, py, _ = _other_chip(x, y, m)
                sent.append(copy(3 * a + m - 1, a, c, s, (px, py, c)))
                sent[-1].start()
        for m in (1, 2, 3):
            _, _, sp = _other_chip(x, y, m)
            for a in range(n):
                copy(3 * a + m - 1, a, c, sp, me).wait_recv()
                sent.append(copy(3 * n + 3 * a + m - 1, a, c, sp, sibling))
                sent[-1].start()
        for m in (1, 2, 3):
            _, _, sp = _other_chip(x, y, m)
            for a in range(n):
                copy(3 * n + 3 * a + m - 1, a, 1 - c, sp, me).wait_recv()
        for cp in sent:
            cp.wait_send()

    return pl.pallas_call(
        body, name="gather_weights",
        in_specs=[ANY] * n, out_specs=[ANY] * n,
        out_shape=[jax.ShapeDtypeStruct(f.shape, f.dtype) for f in fulls],
        input_output_aliases={a: a for a in range(n)},
        scratch_shapes=[pltpu.SemaphoreType.DMA((6 * n,)), pltpu.SemaphoreType.DMA((6 * n,))],
    )(*fulls)


def _sibling_exchange(dws):
    n = len(dws)

    def body(*refs):
        ins, outs = refs[:n], refs[n:2 * n]
        send, recv = refs[2 * n:]
        x, y, c, _ = _place()
        copies = [_remote(ins[a].at[1 - c], outs[a], send, recv, a, (x, y, 1 - c)) for a in range(n)]
        for cp in copies:
            cp.start()
        for cp in copies:
            cp.wait()

    return pl.pallas_call(
        body, name="grad_sibling_exchange",
        in_specs=[ANY] * n, out_specs=[ANY] * n,
        out_shape=[jax.ShapeDtypeStruct(d.shape[1:], d.dtype) for d in dws],
        scratch_shapes=[pltpu.SemaphoreType.DMA((n,)), pltpu.SemaphoreType.DMA((n,))],
    )(*dws)


def _pair_sum(dw, other, pos, name):
    _, R, C = dw.shape
    tr = 128 if C > 1024 else 256

    def body(pos_ref, a_ref, b_ref, o_ref, ob_ref):
        tot = a_ref[...] + b_ref[...]
        o_ref[...] = tot
        ob_ref[...] = tot.astype(BF16)

    blk = pl.BlockSpec((tr, C), lambda i, pos: (i, 0))
    return pl.pallas_call(
        body, name=name,
        grid_spec=pltpu.PrefetchScalarGridSpec(
            num_scalar_prefetch=1, grid=(R // tr,),
            in_specs=[pl.BlockSpec((None, tr, C), lambda i, pos: (pos[0], i, 0)), blk],
            out_specs=[blk, blk]),
        out_shape=[jax.ShapeDtypeStruct((R, C), F32), jax.ShapeDtypeStruct((R, C), BF16)],
        compiler_params=_params(),
    )(pos, dw, other)


def _chip_exchange(ps):
    n = len(ps)

    def body(*refs):
        ins, outs = refs[:n], refs[n:2 * n]
        send, recv = refs[2 * n:]
        x, y, c, _ = _place()
        copies = []
        for a, (axis, width) in enumerate(SHARDED):
            for m in (1, 2, 3):
                px, py, sp = _other_chip(x, y, m)
                copies.append(_remote(_part(ins[a], sp, axis, width), outs[a].at[m - 1], send, recv, 3 * a + m - 1,
                                      (px, py, c)))
        for cp in copies:
            cp.start()
        for cp in copies:
            cp.wait()

    def landing(p, axis, width):
        shape = [3] + list(p.shape)
        shape[axis] = width
        return jax.ShapeDtypeStruct(tuple(shape), p.dtype)

    return pl.pallas_call(
        body, name="grad_chip_exchange",
        in_specs=[ANY] * n, out_specs=[ANY] * n,
        out_shape=[landing(p, axis, width) for p, (axis, width) in zip(ps, SHARDED)],
        scratch_shapes=[pltpu.SemaphoreType.DMA((3 * n,)), pltpu.SemaphoreType.DMA((3 * n,))],
    )(*ps)


def _shard_sum(p, landed, pos, axis, width, name):
    _, Rs, Cs = landed.shape
    tr = min(256, Rs)
    p_spec = _part_spec(tr, Rs, Cs, axis, width, lambda a: 0)

    def body(pos_ref, p_ref, l_ref, o_ref):
        o_ref[...] = ((p_ref[...] + l_ref[0].astype(F32)) + l_ref[1].astype(F32)) + l_ref[2].astype(F32)

    return pl.pallas_call(
        body, name=name,
        grid_spec=pltpu.PrefetchScalarGridSpec(
            num_scalar_prefetch=1, grid=(Rs // tr,),
            in_specs=[p_spec, pl.BlockSpec((3, tr, Cs), lambda i, pos: (0, i, 0))],
            out_specs=pl.BlockSpec((None, tr, Cs), lambda i, pos: (pos[0], i, 0))),
        out_shape=jax.ShapeDtypeStruct((2, Rs, Cs), F32),
        compiler_params=_params(),
    )(pos, p[None], landed)


def _sibling_share(gs):
    n = len(gs)

    def body(*refs):
        outs = refs[n:2 * n]
        send, recv = refs[2 * n:]
        x, y, c, _ = _place()
        copies = [_remote(outs[a].at[c], outs[a].at[c], send, recv, a, (x, y, 1 - c)) for a in range(n)]
        for cp in copies:
            cp.start()
        for a, cp in enumerate(copies):
            cp.wait_send()
            _remote(outs[a].at[1 - c], outs[a].at[1 - c], send, recv, a, (x, y, c)).wait_recv()

    return pl.pallas_call(
        body, name="grad_sibling_share",
        in_specs=[ANY] * n, out_specs=[ANY] * n,
        out_shape=[jax.ShapeDtypeStruct(g.shape, g.dtype) for g in gs],
        input_output_aliases={a: a for a in range(n)},
        scratch_shapes=[pltpu.SemaphoreType.DMA((n,)), pltpu.SemaphoreType.DMA((n,))],
    )(*gs)


N_DEVICES = 8


def _all_reduce_small(packed):
    rows = packed.shape[0]

    def body(in_ref, out_ref, all_ref, send, recv):
        x, y, c, _ = _place()
        my_id = 4 * x + 2 * y + c
        all_ref[my_id] = in_ref[...]
        copies = []
        for m in range(1, N_DEVICES):
            px = 1 - x if m & 4 else x
            py = 1 - y if m & 2 else y
            pc = 1 - c if m & 1 else c
            cp = _remote(in_ref, all_ref.at[my_id], send, recv, m - 1, (px, py, pc))
            cp.start()
            copies.append((cp, 4 * px + 2 * py + pc))
        for m, (cp, peer_id) in enumerate(copies):
            _remote(in_ref, all_ref.at[peer_id], send, recv, m, (x, y, c)).wait_recv()
        for cp, _ in copies:
            cp.wait_send()
        total = all_ref[0]
        for d in range(1, N_DEVICES):
            total = total + all_ref[d]
        out_ref[...] = total

    vmem = pl.BlockSpec(memory_space=pltpu.VMEM)
    return pl.pallas_call(
        body, name="small_all_reduce",
        in_specs=[vmem], out_specs=vmem,
        out_shape=jax.ShapeDtypeStruct(packed.shape, packed.dtype),
        scratch_shapes=[pltpu.VMEM((N_DEVICES, rows, LANES), F32), pltpu.SemaphoreType.DMA((N_DEVICES - 1,)),
                        pltpu.SemaphoreType.DMA((N_DEVICES - 1,))],
        compiler_params=_params(),
    )(packed)


def _adamw(w, g, m, v, name):
    shape = w.shape
    C = shape[-1]
    flat = [t.reshape(-1, C) for t in (w, g, m, v)]
    R = flat[0].shape[0]
    tr = max(t for t in range(8, R + 1, 8) if R % t == 0 and t * C <= 384 * 1024)

    def body(w_ref, g_ref, m_ref, v_ref, d_ref, nm_ref, nv_ref):
        gv = g_ref[...]
        nm = ADAM_B1 * m_ref[...] + (1.0 - ADAM_B1) * gv
        nv = ADAM_B2 * v_ref[...] + (1.0 - ADAM_B2) * (gv * gv)
        m_hat = nm / (1.0 - ADAM_B1 ** ADAM_STEP)
        v_hat = nv / (1.0 - ADAM_B2 ** ADAM_STEP)
        d_ref[...] = -ADAM_LR * (m_hat / (jnp.sqrt(v_hat) + ADAM_EPS) + ADAM_WD * w_ref[...])
        nm_ref[...] = nm
        nv_ref[...] = nv

    blk = pl.BlockSpec((tr, C), lambda i: (i, 0))
    out = jax.ShapeDtypeStruct((R, C), F32)
    res = pl.pallas_call(
        body, name=name, grid=(R // tr,),
        in_specs=[blk] * 4, out_specs=[blk] * 3, out_shape=[out] * 3,
        compiler_params=_params(),
    )(*flat)
    return [t.reshape(shape) for t in res]


SMALL_SHAPES = ((2, 1024), (2, 2048), (2, 4, 128, 128), (2, 512), (1024,))


def _pack_small(parts):
    return jnp.concatenate([p.reshape(-1, LANES) for p in parts], axis=0)


def _unpack_small(packed):
    out, row = [], 0
    for shape in SMALL_SHAPES:
        n = 1
        for d in shape:
            n *= d
        out.append(packed[row:row + n // LANES].reshape(shape))
        row += n // LANES
    return out


def kernel(x, norm_g, w_in, b_gate, pool_w, pool_scale, w_pool_up, w_attn_up, w_out, final_g, loss_target, m_norm_g, m_w_in, m_b_gate, m_pool_w, m_pool_scale, m_w_pool_up, m_w_attn_up, m_w_out, m_final_g, v_norm_g, v_w_in, v_b_gate, v_pool_w, v_pool_scale, v_w_pool_up, v_w_attn_up, v_w_out, v_final_g):
    _, _, c, s = _place()
    pos = jnp.stack([c, s]).astype(jnp.int32)
    names = ("w_in", "w_pool_up", "w_attn_up", "w_out")

    full = _gather_weights([_cast_into_place(w, pos, axis, width, f"cast_{n}")
                            for w, (axis, width), n in zip((w_in, w_pool_up, w_attn_up, w_out), SHARDED, names)])
    loss_part, dx, d_final_g, small, dws = _local_step(x[0], loss_target[0], norm_g, b_gate, pool_w, pool_scale,
                                                       final_g, *full)
    loss = lax.psum(loss_part, ("x", "y", "c"))

    other = _sibling_exchange(dws)
    pair = [_pair_sum(d, o, pos, f"grad_pair_sum_{n}") for d, o, n in zip(dws, other, names)]
    landed = _chip_exchange([pb for _, pb in pair])
    mine = [_shard_sum(p, l, pos, axis, width, f"grad_shard_sum_{n}")
            for (p, _), l, (axis, width), n in zip(pair, landed, SHARDED, names)]
    g_in, g_pu, g_au, g_out = _sibling_share(mine)

    g_small = _unpack_small(_all_reduce_small(_pack_small(small + [d_final_g])))
    upd_small = _adamw(_pack_small([norm_g, b_gate, pool_w, pool_scale, final_g]), _pack_small(g_small),
                       _pack_small([m_norm_g, m_b_gate, m_pool_w, m_pool_scale, m_final_g]),
                       _pack_small([v_norm_g, v_b_gate, v_pool_w, v_pool_scale, v_final_g]), "adamw_small")
    d_small, nm_small, nv_small = [_unpack_small(t) for t in upd_small]
    upd_in = _adamw(w_in, g_in, m_w_in, v_w_in, "adamw_w_in")
    upd_pu = _adamw(w_pool_up, g_pu, m_w_pool_up, v_w_pool_up, "adamw_w_pool_up")
    upd_au = _adamw(w_attn_up, g_au, m_w_attn_up, v_w_attn_up, "adamw_w_attn_up")
    upd_out = _adamw(w_out, g_out, m_w_out, v_w_out, "adamw_w_out")

    def ordered(sm, k):
        big = (upd_in[k], upd_pu[k], upd_au[k], upd_out[k]) if k is not None else (g_in, g_pu, g_au, g_out)
        return [sm[0], big[0], sm[1], sm[2], sm[3], big[1], big[2], big[3], sm[4]]

    return (loss, dx[None], *ordered(g_small, None), *ordered(d_small, 0), *ordered(nm_small, 1),
            *ordered(nv_small, 2))
```

```python
import jax
import jax.numpy as jnp
import numpy as np
from jax import lax
from jax.experimental import pallas as pl
from jax.experimental.pallas import tpu as pltpu

F32 = jnp.float32
BF16 = jnp.bfloat16
MESH = pl.DeviceIdType.MESH

D_MODEL = 1024
POOL_WIDTH = 512
POOL_WINDOWS = (2, 4, 8, 16)
POOL_GROUP = 128
POOL_HALO = 16
ATTN_WIDTH = 512
HEAD_DIM = 64
HEAD_PAIRS = 4
IN_WIDTH = 5120
N_CHIPS = 4
RMS_EPS = 1e-6
C_U, C_ZP, C_Q, C_K, C_V, C_ZA, C_GL = 0, 512, 1024, 1536, 2048, 2560, 3072

ADAM_LR, ADAM_B1, ADAM_B2, ADAM_EPS, ADAM_WD, ADAM_STEP = 0.001, 0.9, 0.999, 1e-08, 0.01, 10

LANES = 128
ATTN_BLOCK = 256
ROW_TILE = 256
VMEM_LIMIT = 56 * 1024 * 1024


def _params(**kw):
    return pltpu.CompilerParams(vmem_limit_bytes=VMEM_LIMIT, **kw)


def _nt(a, b):
    return lax.dot_general(a, b, (((1,), (1,)), ((), ())), preferred_element_type=F32)


def _tn(a, b):
    return lax.dot_general(a, b, (((0,), (0,)), ((), ())), preferred_element_type=F32)


def _nn(a, b):
    return jnp.dot(a, b, preferred_element_type=F32)


def _sigmoid(z):
    return 1.0 / (1.0 + jnp.exp(-z))


def _rms_inproj(x, g, w_in, layer):
    S = x.shape[0]
    tm = min(ROW_TILE, S)

    def body(x_ref, g_ref, w_ref, u_ref, zp_ref, q_ref, k_ref, v_ref, za_ref, gl_ref, h_ref):
        xv = x_ref[...]
        r = lax.rsqrt(jnp.mean(xv * xv, axis=-1, keepdims=True) + RMS_EPS)
        h = ((xv * r) * g_ref[...]).astype(BF16)
        h_ref[...] = h

        def mm(c0, n):
            return _nn(h, w_ref[:, c0:c0 + n])

        u_ref[...] = mm(C_U, 512)
        zp_ref[...] = mm(C_ZP, 512)
        q_ref[...] = (mm(C_Q, 512) * 0.125).astype(BF16)
        k_ref[...] = mm(C_K, 512).astype(BF16)
        v_ref[...] = mm(C_V, 512).astype(BF16)
        za_ref[...] = mm(C_ZA, 512)
        for c in range(4):
            gl_ref[:, c * 512:(c + 1) * 512] = mm(C_GL + c * 512, 512)

    row = lambda n: pl.BlockSpec((tm, n), lambda i: (i, 0))
    sd = lambda n, dt: jax.ShapeDtypeStruct((S, n), dt)
    return pl.pallas_call(
        body, name=f"rms_inproj_l{layer}", grid=(S // tm,),
        in_specs=[row(D_MODEL), pl.BlockSpec((1, D_MODEL), lambda i: (0, 0)),
                  pl.BlockSpec((None, D_MODEL, IN_WIDTH), lambda i: (layer, 0, 0))],
        out_specs=[row(512), row(512), row(512), row(512), row(512), row(512), row(2048), row(D_MODEL)],
        out_shape=[sd(512, F32), sd(512, F32), sd(512, BF16), sd(512, BF16), sd(512, BF16), sd(512, F32),
                   sd(2048, F32), sd(D_MODEL, BF16)],
        compiler_params=_params(),
    )(x, g, w_in)


def _tri(n, strict_lower):
    r = lax.broadcasted_iota(jnp.int32, (n, n), 0)
    c = lax.broadcasted_iota(jnp.int32, (n, n), 1)
    return jnp.where(r > c if strict_lower else r < c, 1.0, 0.0).astype(BF16)


def _split_dot(x, m):
    hi = x.astype(BF16)
    lo = (x - hi.astype(F32)).astype(BF16)
    return _nn(hi, m) + _nn(lo, m)


def _log_terms(z):
    lg = jnp.log(1.0 + jnp.exp(-jnp.abs(z)))
    a = jnp.minimum(z, 0.0) - lg
    return a, a - z


EXHAUSTED = -104.0
UNREACHED = -1e30


class _HeadPair:
    def __init__(self, T, q2):
        self.T = T
        self.first = lax.broadcasted_iota(jnp.int32, (T, LANES), 1) < HEAD_DIM
        self.lane = lax.broadcasted_iota(jnp.int32, (2 * T, LANES), 1)
        row = lax.broadcasted_iota(jnp.int32, (2 * T, T), 0)
        row = jnp.where(row >= T, row - T, row)
        self.causal = row > lax.broadcasted_iota(jnp.int32, (2 * T, T), 1)
        self.below = _tri(T, True)
        self.q = self.stack(q2)

    def stack(self, x2):
        return jnp.concatenate([jnp.where(self.first, x2, 0), jnp.where(self.first, 0, x2)], axis=0).astype(BF16)

    def keys(self, ref, blocks):
        T = self.T
        return jnp.concatenate([ref[pl.ds(pl.multiple_of(j * T, T), T), :] for j, _ in blocks], axis=0)

    def log_terms(self, kcat, blocks):
        T = self.T
        a_all, l_all = _log_terms(_nt(self.q, kcat))
        a = [a_all[:, b * T:(b + 1) * T] for b in range(len(blocks))]
        l1m = [l_all[:, b * T:(b + 1) * T] for b in range(len(blocks))]
        l1m = [jnp.where(self.causal, l, 0.0) if diagonal else l for l, (_, diagonal) in zip(l1m, blocks)]
        later = _split_dot(jnp.concatenate(l1m, axis=0), self.below)
        return a, l1m, [later[2 * T * b:2 * T * (b + 1)] for b in range(len(blocks))]


def _attn_fwd(q, k, v, layer):
    S = q.shape[0]
    T = min(ATTN_BLOCK, S)
    nq = S // T
    assert nq <= LANES

    def body(q_ref, k_ref, v_ref, o_ref, c_ref):
        qi = pl.program_id(1)
        pair = _HeadPair(T, q_ref[...])

        def sweep(blocks, carry):
            acc, run, saved = carry
            kcat, vcat = pair.keys(k_ref, blocks), pair.keys(v_ref, blocks)
            a, l1m, later = pair.log_terms(kcat, blocks)
            ws = []
            for b, (j, diagonal) in enumerate(blocks):
                saved = jnp.where(pair.lane == j, run, saved)
                w = jnp.exp(a[b] + later[b] + run)
                ws.append(jnp.where(pair.causal, w, 0.0) if diagonal else w)
                run = run + jnp.sum(l1m[b], axis=1, keepdims=True)
            acc = acc + _nn(jnp.concatenate(ws, axis=1).astype(BF16), vcat)
            return acc, run, saved

        def finish(carry):
            acc, _, saved = carry
            o_ref[...] = jnp.where(pair.first, acc[:T], acc[T:])
            c_ref[:, :LANES] = saved[:T]
            c_ref[:, LANES:] = saved[T:]

        def alive(carry):
            return (jnp.max(carry[1]) > EXHAUSTED).astype(jnp.int32)

        def older_block(state):
            j, _, carry = state
            carry = sweep([(j, False)], carry)
            return j - 1, alive(carry), carry

        init = (jnp.zeros((2 * T, LANES), F32), jnp.zeros((2 * T, 1), F32), jnp.full((2 * T, LANES), UNREACHED, F32))

        @pl.when(qi == 0)
        def _():
            finish(sweep([(qi, True)], init))

        @pl.when(qi > 0)
        def _():
            carry = sweep([(qi, True), (qi - 1, False)], init)
            finish(lax.while_loop(lambda st: jnp.logical_and(st[0] >= 0, st[1] > 0), older_block,
                                  (qi - 2, alive(carry), carry))[2])

    return pl.pallas_call(
        body, name=f"attn_fwd_l{layer}", grid=(HEAD_PAIRS, nq),
        in_specs=[pl.BlockSpec((T, LANES), lambda p, i: (i, p)),
                  pl.BlockSpec((S, LANES), lambda p, i: (0, p)),
                  pl.BlockSpec((S, LANES), lambda p, i: (0, p))],
        out_specs=[pl.BlockSpec((T, LANES), lambda p, i: (i, p)),
                   pl.BlockSpec((T, 2 * LANES), lambda p, i: (i, p))],
        out_shape=[jax.ShapeDtypeStruct((S, ATTN_WIDTH), F32), jax.ShapeDtypeStruct((S, 8 * LANES), F32)],
        compiler_params=_params(),
    )(q, k, v)


def _attn_bwd(q, k, v, saved, do, layer):
    S = q.shape[0]
    T = min(ATTN_BLOCK, S)
    nq = S // T

    def body(q_ref, k_ref, v_ref, c_ref, do_ref, dq_ref, dk_ref, dv_ref):
        qi = pl.program_id(1)

        @pl.when(qi == 0)
        def _():
            dk_ref[...] = jnp.zeros_like(dk_ref)
            dv_ref[...] = jnp.zeros_like(dv_ref)

        pair = _HeadPair(T, q_ref[...])
        do = pair.stack(do_ref[...].astype(BF16))
        saved = jnp.concatenate([c_ref[:, :LANES], c_ref[:, LANES:]], axis=0)
        before = _tri(T, False)

        def sweep(blocks, carry):
            dq, older = carry
            kcat, vcat = pair.keys(k_ref, blocks), pair.keys(v_ref, blocks)
            a, l1m, later = pair.log_terms(kcat, blocks)
            g = _nt(do, vcat)
            ws, es = [], []
            for b, (j, diagonal) in enumerate(blocks):
                run = jnp.sum(jnp.where(pair.lane == j, saved, 0.0), axis=1, keepdims=True)
                w = jnp.exp(a[b] + later[b] + run)
                ws.append(jnp.where(pair.causal, w, 0.0) if diagonal else w)
                es.append(ws[b] * g[:, b * T:(b + 1) * T])
            prefix = _split_dot(jnp.concatenate(es, axis=0), before)
            dzs = []
            for b, (j, diagonal) in enumerate(blocks):
                dz = es[b] - jnp.exp(a[b]) * (es[b] + (prefix[2 * T * b:2 * T * (b + 1)] + older))
                dzs.append(jnp.where(pair.causal, dz, 0.0) if diagonal else dz)
                older = older + jnp.sum(es[b], axis=1, keepdims=True)
            dz = jnp.concatenate(dzs, axis=1).astype(BF16)
            dk = _tn(dz, pair.q)
            dv = _tn(jnp.concatenate(ws, axis=1).astype(BF16), do)
            for b, (j, _) in enumerate(blocks):
                rows = pl.ds(pl.multiple_of(j * T, T), T)
                dk_ref[rows, :] += dk[b * T:(b + 1) * T]
                dv_ref[rows, :] += dv[b * T:(b + 1) * T]
            return dq + _nn(dz, kcat), older

        def finish(carry):
            dq_ref[...] = jnp.where(pair.first, carry[0][:T], carry[0][T:]) * 0.125

        init = (jnp.zeros((2 * T, LANES), F32), jnp.zeros((2 * T, 1), F32))

        @pl.when(qi == 0)
        def _():
            finish(sweep([(qi, True)], init))

        @pl.when(qi > 0)
        def _():
            col_max = jnp.max(saved, axis=0, keepdims=True)
            lane_row = lax.broadcasted_iota(jnp.int32, (1, LANES), 1)
            reached = jnp.sum(jnp.where(jnp.logical_and(col_max > EXHAUSTED, lane_row < qi), 1, 0))
            carry = lax.fori_loop(qi - reached, qi - 1, lambda j, c: sweep([(j, False)], c), init)
            finish(sweep([(qi - 1, False), (qi, True)], carry))

    blk = pl.BlockSpec((T, LANES), lambda p, i: (i, p))
    full = pl.BlockSpec((S, LANES), lambda p, i: (0, p))
    out = jax.ShapeDtypeStruct((S, ATTN_WIDTH), F32)
    return pl.pallas_call(
        body, name=f"attn_bwd_l{layer}", grid=(HEAD_PAIRS, nq),
        in_specs=[blk, full, full, pl.BlockSpec((T, 2 * LANES), lambda p, i: (i, p)), blk],
        out_specs=[blk, full, full],
        out_shape=[out, out, out],
        compiler_params=_params(),
    )(q, k, v, saved, do)


def _pool_counts(row0, tm):
    pos = row0 + lax.broadcasted_iota(jnp.int32, (tm, 1), 0)
    return [1.0 / jnp.minimum(pos + 1, w).astype(F32) for w in POOL_WINDOWS]


def _window_bands(tm, backward):
    t = np.arange(tm)[:, None]
    c = np.arange(tm + POOL_HALO)[None, :]
    off = c - t if backward else t + POOL_HALO - c
    return jnp.asarray(np.stack([(off >= 0) & (off < w) for w in POOL_WINDOWS]), BF16)


def _window_sums(ext, band_ref):
    hi = ext.astype(BF16)
    lo = (ext - hi.astype(F32)).astype(BF16)
    sums = []
    for g in range(len(POOL_WINDOWS)):
        cols = slice(g * POOL_GROUP, (g + 1) * POOL_GROUP)
        sums.append(_nn(band_ref[g], hi[:, cols]) + _nn(band_ref[g], lo[:, cols]))
    return sums


def _post_forward(ext, band_ref, inv_cnt, zp, o, za, gl, bg, pw_ref, scale, wpu_ref, wau_ref):
    pooled, mixed = [], []
    for g, tot in enumerate(_window_sums(ext, band_ref)):
        pg = (tot * inv_cnt[g] - ext[POOL_HALO:, g * POOL_GROUP:(g + 1) * POOL_GROUP]).astype(BF16)
        pooled.append(pg)
        mixed.append(_nn(pg, pw_ref[g].astype(BF16)))
    pooled = jnp.concatenate(pooled, axis=1)
    mixed = jnp.concatenate(mixed, axis=1)
    sp = _sigmoid(zp)
    sa = _sigmoid(za)
    y_pool = (mixed * scale) * (zp * sp)
    y_attn = o * (za * sa)
    gate = _sigmoid(gl + bg)
    g0, g1 = gate[:, :D_MODEL], gate[:, D_MODEL:]
    up_p = _nn(y_pool.astype(BF16), wpu_ref[...])
    up_a = _nn(y_attn.astype(BF16), wau_ref[...])
    merged = g0 * up_p + g1 * up_a
    return pooled, mixed, sp, sa, y_pool, y_attn, g0, g1, up_p, up_a, merged


def _row_specs(tm, rev, n_tiles):
    tile_of = (lambda i: n_tiles - 1 - i) if rev else (lambda i: i)
    row = lambda n: pl.BlockSpec((tm, n), lambda i: (tile_of(i), 0))
    halo = pl.BlockSpec((POOL_HALO, POOL_WIDTH),
                        lambda i: (jnp.maximum(tile_of(i) * (tm // POOL_HALO) - 1, 0), 0))
    const = lambda shape: pl.BlockSpec(shape, lambda i: (0,) * len(shape))
    return tile_of, row, halo, const


def _layer_weight_spec(rows, cols, layer):
    return pl.BlockSpec((None, rows, cols), lambda i: (layer, 0, 0))


def _post_fwd(x, u, zp, o, za, gl, bg, pw, scale, wpu, wau, wout, layer):
    S = x.shape[0]
    tm = min(ROW_TILE, S)
    n_tiles = S // tm
    tile_of, row, halo, const = _row_specs(tm, False, n_tiles)

    def body(x_ref, u_ref, uh_ref, band_ref, zp_ref, o_ref, za_ref, gl_ref, bg_ref, pw_ref, sc_ref, wpu_ref, wau_ref,
             wout_ref, out_ref):
        i = pl.program_id(0)
        ext = jnp.concatenate([jnp.where(i == 0, 0.0, uh_ref[...]), u_ref[...]], axis=0)
        vals = _post_forward(ext, band_ref, _pool_counts(i * tm, tm), zp_ref[...], o_ref[...], za_ref[...],
                             gl_ref[...], bg_ref[...], pw_ref, sc_ref[...], wpu_ref, wau_ref)
        out_ref[...] = x_ref[...] + _nn(vals[-1].astype(BF16), wout_ref[...])

    return pl.pallas_call(
        body, name=f"post_fwd_l{layer}", grid=(n_tiles,),
        in_specs=[row(D_MODEL), row(512), halo, const((4, tm, tm + POOL_HALO)), row(512), row(512), row(512),
                  row(2048), const((1, 2048)), const((4, POOL_GROUP, POOL_GROUP)), const((1, POOL_WIDTH)),
                  _layer_weight_spec(POOL_WIDTH, D_MODEL, layer), _layer_weight_spec(ATTN_WIDTH, D_MODEL, layer),
                  _layer_weight_spec(D_MODEL, D_MODEL, layer)],
        out_specs=row(D_MODEL),
        out_shape=jax.ShapeDtypeStruct((S, D_MODEL), F32),
        compiler_params=_params(),
    )(x, u, u, _window_bands(tm, False), zp, o, za, gl, bg, pw, scale, wpu, wau, wout)


def _post_bwd(dx, u, zp, o, za, gl, bg, pw, scale, wpu, wau, wout, layer):
    S = dx.shape[0]
    tm = min(ROW_TILE // 2, S)
    n_tiles = S // tm
    tile_of, row, halo, const = _row_specs(tm, True, n_tiles)

    def body(dx_ref, u_ref, uh_ref, band_ref, back_ref, zp_ref, o_ref, za_ref, gl_ref, bg_ref, pw_ref, sc_ref, wpu_ref,
             wau_ref, wout_ref,
             duz_ref, do_ref, dza_ref, dgl_ref, dsc_ref, dbg_ref,
             merged_ref, dup_ref, dua_ref, yp_ref, ya_ref, pooled_ref, dmixed_ref, nxt_ref):
        step = pl.program_id(0)
        i = tile_of(step)

        @pl.when(step == 0)
        def _():
            dsc_ref[...] = jnp.zeros_like(dsc_ref)
            dbg_ref[...] = jnp.zeros_like(dbg_ref)
            nxt_ref[...] = jnp.zeros_like(nxt_ref)

        ext = jnp.concatenate([jnp.where(i == 0, 0.0, uh_ref[...]), u_ref[...]], axis=0)
        inv_cnt = _pool_counts(i * tm, tm)
        zp, za, o = zp_ref[...], za_ref[...], o_ref[...]
        pooled, mixed, sp, sa, y_pool, y_attn, g0, g1, up_p, up_a, merged = _post_forward(
            ext, band_ref, inv_cnt, zp, o, za, gl_ref[...], bg_ref[...], pw_ref, sc_ref[...], wpu_ref, wau_ref)
        merged_ref[...] = merged.astype(BF16)
        yp_ref[...] = y_pool.astype(BF16)
        ya_ref[...] = y_attn.astype(BF16)
        pooled_ref[...] = pooled

        dmerged = _nt(dx_ref[...].astype(BF16), wout_ref[...])
        dup = (dmerged * g0).astype(BF16)
        dua = (dmerged * g1).astype(BF16)
        dup_ref[...] = dup
        dua_ref[...] = dua
        dgl0 = (dmerged * up_p) * (g0 * (1.0 - g0))
        dgl1 = (dmerged * up_a) * (g1 * (1.0 - g1))
        dgl_ref[:, :D_MODEL] = dgl0
        dgl_ref[:, D_MODEL:] = dgl1
        dbg_ref[:, :D_MODEL] += jnp.sum(dgl0, axis=0, keepdims=True)
        dbg_ref[:, D_MODEL:] += jnp.sum(dgl1, axis=0, keepdims=True)

        dy_attn = _nt(dua, wau_ref[...])
        do_ref[...] = dy_attn * (za * sa)
        dza_ref[...] = (dy_attn * o) * (sa * (1.0 + za * (1.0 - sa)))

        dy_pool = _nt(dup, wpu_ref[...])
        ms = mixed * sc_ref[...]
        dms = dy_pool * (zp * sp)
        duz_ref[:, POOL_WIDTH:] = (dy_pool * ms) * (sp * (1.0 + zp * (1.0 - sp)))
        dsc_ref[...] += jnp.sum(dms * mixed, axis=0, keepdims=True)
        dmixed = (dms * sc_ref[...]).astype(BF16)
        dmixed_ref[...] = dmixed
        dpooled = [_nt(dmixed[:, g * POOL_GROUP:(g + 1) * POOL_GROUP], pw_ref[g].astype(BF16)) for g in range(4)]
        scaled = jnp.concatenate([d * inv for d, inv in zip(dpooled, inv_cnt)], axis=1)
        for g, tot in enumerate(_window_sums(jnp.concatenate([scaled, nxt_ref[...]], axis=0), back_ref)):
            duz_ref[:, g * POOL_GROUP:(g + 1) * POOL_GROUP] = tot - dpooled[g]
        nxt_ref[...] = scaled[:POOL_HALO]

    sd = lambda n, dt: jax.ShapeDtypeStruct((S, n), dt)
    bands = const((4, tm, tm + POOL_HALO))
    return pl.pallas_call(
        body, name=f"post_bwd_l{layer}", grid=(n_tiles,),
        in_specs=[row(D_MODEL), row(512), halo, bands, bands, row(512), row(512), row(512), row(2048),
                  const((1, 2048)), const((4, POOL_GROUP, POOL_GROUP)), const((1, POOL_WIDTH)),
                  _layer_weight_spec(POOL_WIDTH, D_MODEL, layer), _layer_weight_spec(ATTN_WIDTH, D_MODEL, layer),
                  _layer_weight_spec(D_MODEL, D_MODEL, layer)],
        out_specs=[row(1024), row(512), row(512), row(2048), const((1, POOL_WIDTH)), const((1, 2048)),
                   row(D_MODEL), row(D_MODEL), row(D_MODEL), row(512), row(512), row(512), row(512)],
        out_shape=[sd(1024, F32), sd(512, F32), sd(512, F32), sd(2048, F32),
                   jax.ShapeDtypeStruct((1, POOL_WIDTH), F32), jax.ShapeDtypeStruct((1, 2048), F32),
                   sd(D_MODEL, BF16), sd(D_MODEL, BF16), sd(D_MODEL, BF16), sd(512, BF16), sd(512, BF16),
                   sd(512, BF16), sd(512, BF16)],
        scratch_shapes=[pltpu.VMEM((POOL_HALO, POOL_WIDTH), F32)],
        compiler_params=_params(),
    )(dx, u, u, _window_bands(tm, False), _window_bands(tm, True), zp, o, za, gl, bg, pw, scale, wpu, wau, wout)


def _rms_backward(dh, xv, r, g):
    xhat = xv * r
    dxhat = dh * g
    return r * (dxhat - xhat * jnp.mean(dxhat * xhat, axis=-1, keepdims=True)), dh * xhat


def _loss_head(x, g, target):
    S = x.shape[0]
    tm = min(ROW_TILE, S)

    def body(x_ref, g_ref, t_ref, loss_ref, dx_ref, dg_ref):
        @pl.when(pl.program_id(0) == 0)
        def _():
            loss_ref[...] = jnp.zeros_like(loss_ref)
            dg_ref[...] = jnp.zeros_like(dg_ref)

        xv = x_ref[...]
        r = lax.rsqrt(jnp.mean(xv * xv, axis=-1, keepdims=True) + RMS_EPS)
        diff = (xv * r) * g_ref[...] - t_ref[...]
        per_row = jnp.mean(diff * diff, axis=-1, keepdims=True)
        loss_ref[...] += 0.5 * jnp.sum(per_row, axis=0, keepdims=True)
        dx, dg_rows = _rms_backward(diff * (1.0 / D_MODEL), xv, r, g_ref[...])
        dx_ref[...] = dx
        dg_ref[...] += jnp.sum(dg_rows, axis=0, keepdims=True)

    row = pl.BlockSpec((tm, D_MODEL), lambda i: (i, 0))
    vec = pl.BlockSpec((1, D_MODEL), lambda i: (0, 0))
    return pl.pallas_call(
        body, name="loss_head", grid=(S // tm,),
        in_specs=[row, vec, row],
        out_specs=[pl.BlockSpec((1, LANES), lambda i: (0, 0)), row, vec],
        out_shape=[jax.ShapeDtypeStruct((1, LANES), F32), jax.ShapeDtypeStruct((S, D_MODEL), F32),
                   jax.ShapeDtypeStruct((1, D_MODEL), F32)],
        compiler_params=_params(),
    )(x, g, target)


def _inproj_bwd(pieces, w_in, x, g, dx_res, layer):
    S = x.shape[0]
    tm = min(ROW_TILE, S)
    cols = [(c0, p.shape[1]) for p, c0 in pieces]

    def body(*refs):
        piece_refs = refs[:len(cols)]
        w_ref, x_ref, g_ref, res_ref, dx_ref, dg_ref = refs[len(cols):]

        @pl.when(pl.program_id(0) == 0)
        def _():
            dg_ref[...] = jnp.zeros_like(dg_ref)

        dh = jnp.zeros((tm, D_MODEL), F32)
        for p_ref, (c0, n) in zip(piece_refs, cols):
            for c in range(0, n, 512):
                dh = dh + _nt(p_ref[:, c:c + 512].astype(BF16), w_ref[:, c0 + c:c0 + c + 512])
        xv = x_ref[...]
        r = lax.rsqrt(jnp.mean(xv * xv, axis=-1, keepdims=True) + RMS_EPS)
        dx, dg_rows = _rms_backward(dh, xv, r, g_ref[...])
        dx_ref[...] = res_ref[...] + dx
        dg_ref[...] += jnp.sum(dg_rows, axis=0, keepdims=True)

    row = lambda n: pl.BlockSpec((tm, n), lambda i: (i, 0))
    vec = pl.BlockSpec((1, D_MODEL), lambda i: (0, 0))
    return pl.pallas_call(
        body, name=f"inproj_bwd_l{layer}", grid=(S // tm,),
        in_specs=[row(n) for _, n in cols] + [_layer_weight_spec(D_MODEL, IN_WIDTH, layer), row(D_MODEL), vec,
                                              row(D_MODEL)],
        out_specs=[row(D_MODEL), vec],
        out_shape=[jax.ShapeDtypeStruct((S, D_MODEL), F32), jax.ShapeDtypeStruct((1, D_MODEL), F32)],
        compiler_params=_params(),
    )(*[p for p, _ in pieces], w_in, x, g, dx_res)


def _wgrad(a, b, name, layer, into=None, col0=0, n_total=None):
    S, M = a.shape
    N = b.shape[1]
    n_total = N if n_total is None else n_total
    tk = min(2048, S)
    tn = min(512, N)
    nk = S // tk

    def body(*refs):
        a_ref, b_ref, out_ref = refs[0], refs[1], refs[-1]
        prod = _tn(a_ref[...].astype(BF16), b_ref[...].astype(BF16))

        @pl.when(pl.program_id(1) == 0)
        def _():
            out_ref[...] = prod

        @pl.when(pl.program_id(1) > 0)
        def _():
            out_ref[...] += prod

    in_specs = [pl.BlockSpec((tk, M), lambda j, k: (k, 0)), pl.BlockSpec((tk, tn), lambda j, k: (k, j))]
    args = [a, b]
    aliases = {}
    if into is not None:
        in_specs.append(pl.BlockSpec(memory_space=pl.ANY))
        args.append(into)
        aliases = {2: 0}
    return pl.pallas_call(
        body, name=name, grid=(N // tn, nk),
        in_specs=in_specs,
        out_specs=pl.BlockSpec((None, M, tn), lambda j, k: (layer, 0, col0 // tn + j)),
        out_shape=jax.ShapeDtypeStruct((2, M, n_total), F32),
        input_output_aliases=aliases,
        compiler_params=_params(),
    )(*args)


def _pool_wgrad(pooled, dmixed, layer):
    S = pooled.shape[0]
    tk = min(1024, S)

    def body(a_ref, b_ref, out_ref):
        prod = _tn(a_ref[...], b_ref[...])

        @pl.when(pl.program_id(1) == 0)
        def _():
            out_ref[...] = prod

        @pl.when(pl.program_id(1) > 0)
        def _():
            out_ref[...] += prod

    blk = pl.BlockSpec((tk, POOL_GROUP), lambda g, k: (k, g))
    return pl.pallas_call(
        body, name=f"pool_wgrad_l{layer}", grid=(4, S // tk),
        in_specs=[blk, blk],
        out_specs=pl.BlockSpec((None, POOL_GROUP, POOL_GROUP), lambda g, k: (g, 0, 0)),
        out_shape=jax.ShapeDtypeStruct((4, POOL_GROUP, POOL_GROUP), F32),
        compiler_params=_params(),
    )(pooled, dmixed)


def _local_step(x, target, norm_g, b_gate, pool_w, pool_scale, final_g, w_in, w_pu, w_au, w_out):
    n_layers = norm_g.shape[0]
    saved = []
    for l in range(n_layers):
        g = norm_g[l][None]
        bg = b_gate[l][None]
        sc = pool_scale[l][None]
        u, zp, q, k, v, za, gl, h = _rms_inproj(x, g, w_in, l)
        o, carry = _attn_fwd(q, k, v, l)
        saved.append((x, g, bg, sc, u, zp, q, k, v, za, gl, h, o, carry))
        x = _post_fwd(x, u, zp, o, za, gl, bg, pool_w[l], sc, w_pu, w_au, w_out, l)
    loss, dx, d_final_g = _loss_head(x, final_g[None], target)

    small = [None] * n_layers
    dw_in = dw_out = dw_pu = dw_au = None
    for l in reversed(range(n_layers)):
        x_in, g, bg, sc, u, zp, q, k, v, za, gl, h, o, carry = saved[l]
        (duz, do, dza, dgl, dsc, dbg, merged, dup, dua, y_pool, y_attn, pooled, dmixed) = _post_bwd(
            dx, u, zp, o, za, gl, bg, pool_w[l], sc, w_pu, w_au, w_out, l)
        dq, dk, dv = _attn_bwd(q, k, v, carry, do, l)
        pieces = [(duz, C_U), (dq, C_Q), (dk, C_K), (dv, C_V), (dza, C_ZA), (dgl, C_GL)]
        for p, c0 in pieces:
            dw_in = _wgrad(h, p, f"wgrad_in_l{l}_c{c0}", l, into=dw_in, col0=c0, n_total=IN_WIDTH)
        dw_out = _wgrad(merged, dx, f"wgrad_out_l{l}", l, into=dw_out)
        dw_pu = _wgrad(y_pool, dup, f"wgrad_pu_l{l}", l, into=dw_pu)
        dw_au = _wgrad(y_attn, dua, f"wgrad_au_l{l}", l, into=dw_au)
        dpw = _pool_wgrad(pooled, dmixed, l)
        dx, dg = _inproj_bwd(pieces, w_in, x_in, g, dx, l)
        small[l] = (dg[0], dbg[0], dpw, dsc[0])
    small = [jnp.stack([small[l][i] for l in range(n_layers)]) for i in range(4)]
    return loss[0, 0], dx, d_final_g[0], small, (dw_in, dw_pu, dw_au, dw_out)


SHARDED = ((2, 1280), (2, 256), (2, 256), (1, 256))
ANY = pl.BlockSpec(memory_space=pl.ANY)


def _part(ref, s, axis, width):
    sl = pl.ds(pl.multiple_of(s * width, width), width)
    return ref.at[:, sl] if axis == 2 else ref.at[sl, :]


def _place():
    x, y, c = lax.axis_index("x"), lax.axis_index("y"), lax.axis_index("c")
    return x, y, c, 2 * x + y


def _other_chip(x, y, m):
    px = 1 - x if m & 2 else x
    py = 1 - y if m & 1 else y
    return px, py, 2 * px + py


def _remote(src, dst, send, recv, k, to):
    return pltpu.make_async_remote_copy(src_ref=src, dst_ref=dst, send_sem=send.at[k], recv_sem=recv.at[k],
                                        device_id=to, device_id_type=MESH)


def _part_spec(tr, rows_s, cols_s, axis, width, lead):
    if axis == 2:
        return pl.BlockSpec((None, tr, width), lambda *a: (lead(a), a[-2], a[-1][1]))
    return pl.BlockSpec((None, tr, cols_s), lambda *a: (lead(a), a[-1][1] * (rows_s // tr) + a[-2], 0))


def _cast_into_place(w, pos, axis, width, name):
    L, Rs, Cs = w.shape
    tr = min(256, Rs)
    shape = [L, Rs, Cs]
    shape[axis] *= N_CHIPS

    def body(pos_ref, w_ref, o_ref):
        o_ref[...] = w_ref[...].astype(BF16)

    return pl.pallas_call(
        body, name=name,
        grid_spec=pltpu.PrefetchScalarGridSpec(
            num_scalar_prefetch=1, grid=(L, Rs // tr),
            in_specs=[pl.BlockSpec((None, tr, Cs), lambda l, i, pos: (l, i, 0))],
            out_specs=_part_spec(tr, Rs, Cs, axis, width, lambda a: a[0])),
        out_shape=jax.ShapeDtypeStruct(tuple(shape), BF16),
        compiler_params=_params(),
    )(pos, w)


def _gather_weights(fulls):
    n = len(SHARDED)

    def body(*refs):
        outs = refs[n:2 * n]
        send, recv = refs[2 * n:]
        x, y, c, s = _place()
        me, sibling = (x, y, c), (x, y, 1 - c)

        def copy(k, a, layer, shard, to):
            part = _part(outs[a].at[layer], shard, *SHARDED[a])
            return _remote(part, part, send, recv, k, to)

        sent = []
        for a in range(n):
            for m in (1, 2, 3):
                px, py, _ = _other_chip(x, y, m)
                sent.append(copy(3 * a + m - 1, a, c, s, (px, py, c)))
                sent[-1].start()
        for m in (1, 2, 3):
            _, _, sp = _other_chip(x, y, m)
            for a in range(n):
                copy(3 * a + m - 1, a, c, sp, me).wait_recv()
                sent.append(copy(3 * n + 3 * a + m - 1, a, c, sp, sibling))
                sent[-1].start()
        for m in (1, 2, 3):
            _, _, sp = _other_chip(x, y, m)
            for a in range(n):
                copy(3 * n + 3 * a + m - 1, a, 1 - c, sp, me).wait_recv()
        for cp in sent:
            cp.wait_send()

    return pl.pallas_call(
        body, name="gather_weights",
        in_specs=[ANY] * n, out_specs=[ANY] * n,
        out_shape=[jax.ShapeDtypeStruct(f.shape, f.dtype) for f in fulls],
        input_output_aliases={a: a for a in range(n)},
        scratch_shapes=[pltpu.SemaphoreType.DMA((6 * n,)), pltpu.SemaphoreType.DMA((6 * n,))],
    )(*fulls)


def _sibling_exchange(dws):
    n = len(dws)

    def body(*refs):
        ins, outs = refs[:n], refs[n:2 * n]
        send, recv = refs[2 * n:]
        x, y, c, _ = _place()
        copies = [_remote(ins[a].at[1 - c], outs[a], send, recv, a, (x, y, 1 - c)) for a in range(n)]
        for cp in copies:
            cp.start()
        for cp in copies:
            cp.wait()

    return pl.pallas_call(
        body, name="grad_sibling_exchange",
        in_specs=[ANY] * n, out_specs=[ANY] * n,
        out_shape=[jax.ShapeDtypeStruct(d.shape[1:], d.dtype) for d in dws],
        scratch_shapes=[pltpu.SemaphoreType.DMA((n,)), pltpu.SemaphoreType.DMA((n,))],
    )(*dws)


def _pair_sum(dw, other, pos, name):
    _, R, C = dw.shape
    tr = 128 if C > 1024 else 256

    def body(pos_ref, a_ref, b_ref, o_ref, ob_ref):
        tot = a_ref[...] + b_ref[...]
        o_ref[...] = tot
        ob_ref[...] = tot.astype(BF16)

    blk = pl.BlockSpec((tr, C), lambda i, pos: (i, 0))
    return pl.pallas_call(
        body, name=name,
        grid_spec=pltpu.PrefetchScalarGridSpec(
            num_scalar_prefetch=1, grid=(R // tr,),
            in_specs=[pl.BlockSpec((None, tr, C), lambda i, pos: (pos[0], i, 0)), blk],
            out_specs=[blk, blk]),
        out_shape=[jax.ShapeDtypeStruct((R, C), F32), jax.ShapeDtypeStruct((R, C), BF16)],
        compiler_params=_params(),
    )(pos, dw, other)


def _chip_exchange(ps):
    n = len(ps)

    def body(*refs):
        ins, outs = refs[:n], refs[n:2 * n]
        send, recv = refs[2 * n:]
        x, y, c, _ = _place()
        copies = []
        for a, (axis, width) in enumerate(SHARDED):
            for m in (1, 2, 3):
                px, py, sp = _other_chip(x, y, m)
                copies.append(_remote(_part(ins[a], sp, axis, width), outs[a].at[m - 1], send, recv, 3 * a + m - 1,
                                      (px, py, c)))
        for cp in copies:
            cp.start()
        for cp in copies:
            cp.wait()

    def landing(p, axis, width):
        shape = [3] + list(p.shape)
        shape[axis] = width
        return jax.ShapeDtypeStruct(tuple(shape), p.dtype)

    return pl.pallas_call(
        body, name="grad_chip_exchange",
        in_specs=[ANY] * n, out_specs=[ANY] * n,
        out_shape=[landing(p, axis, width) for p, (axis, width) in zip(ps, SHARDED)],
        scratch_shapes=[pltpu.SemaphoreType.DMA((3 * n,)), pltpu.SemaphoreType.DMA((3 * n,))],
    )(*ps)


def _shard_sum(p, landed, pos, axis, width, name):
    _, Rs, Cs = landed.shape
    tr = min(256, Rs)
    p_spec = _part_spec(tr, Rs, Cs, axis, width, lambda a: 0)

    def body(pos_ref, p_ref, l_ref, o_ref):
        o_ref[...] = ((p_ref[...] + l_ref[0].astype(F32)) + l_ref[1].astype(F32)) + l_ref[2].astype(F32)

    return pl.pallas_call(
        body, name=name,
        grid_spec=pltpu.PrefetchScalarGridSpec(
            num_scalar_prefetch=1, grid=(Rs // tr,),
            in_specs=[p_spec, pl.BlockSpec((3, tr, Cs), lambda i, pos: (0, i, 0))],
            out_specs=pl.BlockSpec((None, tr, Cs), lambda i, pos: (pos[0], i, 0))),
        out_shape=jax.ShapeDtypeStruct((2, Rs, Cs), F32),
        compiler_params=_params(),
    )(pos, p[None], landed)


def _sibling_share(gs):
    n = len(gs)

    def body(*refs):
        outs = refs[n:2 * n]
        send, recv = refs[2 * n:]
        x, y, c, _ = _place()
        copies = [_remote(outs[a].at[c], outs[a].at[c], send, recv, a, (x, y, 1 - c)) for a in range(n)]
        for cp in copies:
            cp.start()
        for a, cp in enumerate(copies):
            cp.wait_send()
            _remote(outs[a].at[1 - c], outs[a].at[1 - c], send, recv, a, (x, y, c)).wait_recv()

    return pl.pallas_call(
        body, name="grad_sibling_share",
        in_specs=[ANY] * n, out_specs=[ANY] * n,
        out_shape=[jax.ShapeDtypeStruct(g.shape, g.dtype) for g in gs],
        input_output_aliases={a: a for a in range(n)},
        scratch_shapes=[pltpu.SemaphoreType.DMA((n,)), pltpu.SemaphoreType.DMA((n,))],
    )(*gs)


N_DEVICES = 8


def _all_reduce_small(packed):
    rows = packed.shape[0]

    def body(in_ref, out_ref, all_ref, send, recv):
        x, y, c, _ = _place()
        my_id = 4 * x + 2 * y + c
        all_ref[my_id] = in_ref[...]
        copies = []
        for m in range(1, N_DEVICES):
            px = 1 - x if m & 4 else x
            py = 1 - y if m & 2 else y
            pc = 1 - c if m & 1 else c
            cp = _remote(in_ref, all_ref.at[my_id], send, recv, m - 1, (px, py, pc))
            cp.start()
            copies.append((cp, 4 * px + 2 * py + pc))
        for m, (cp, peer_id) in enumerate(copies):
            _remote(in_ref, all_ref.at[peer_id], send, recv, m, (x, y, c)).wait_recv()
        for cp, _ in copies:
            cp.wait_send()
        total = all_ref[0]
        for d in range(1, N_DEVICES):
            total = total + all_ref[d]
        out_ref[...] = total

    vmem = pl.BlockSpec(memory_space=pltpu.VMEM)
    return pl.pallas_call(
        body, name="small_all_reduce",
        in_specs=[vmem], out_specs=vmem,
        out_shape=jax.ShapeDtypeStruct(packed.shape, packed.dtype),
        scratch_shapes=[pltpu.VMEM((N_DEVICES, rows, LANES), F32), pltpu.SemaphoreType.DMA((N_DEVICES - 1,)),
                        pltpu.SemaphoreType.DMA((N_DEVICES - 1,))],
        compiler_params=_params(),
    )(packed)


def _adamw(w, g, m, v, name):
    shape = w.shape
    C = shape[-1]
    flat = [t.reshape(-1, C) for t in (w, g, m, v)]
    R = flat[0].shape[0]
    tr = max(t for t in range(8, R + 1, 8) if R % t == 0 and t * C <= 384 * 1024)

    def body(w_ref, g_ref, m_ref, v_ref, d_ref, nm_ref, nv_ref):
        gv = g_ref[...]
        nm = ADAM_B1 * m_ref[...] + (1.0 - ADAM_B1) * gv
        nv = ADAM_B2 * v_ref[...] + (1.0 - ADAM_B2) * (gv * gv)
        m_hat = nm / (1.0 - ADAM_B1 ** ADAM_STEP)
        v_hat = nv / (1.0 - ADAM_B2 ** ADAM_STEP)
        d_ref[...] = -ADAM_LR * (m_hat / (jnp.sqrt(v_hat) + ADAM_EPS) + ADAM_WD * w_ref[...])
        nm_ref[...] = nm
        nv_ref[...] = nv

    blk = pl.BlockSpec((tr, C), lambda i: (i, 0))
    out = jax.ShapeDtypeStruct((R, C), F32)
    res = pl.pallas_call(
        body, name=name, grid=(R // tr,),
        in_specs=[blk] * 4, out_specs=[blk] * 3, out_shape=[out] * 3,
        compiler_params=_params(),
    )(*flat)
    return [t.reshape(shape) for t in res]


SMALL_SHAPES = ((2, 1024), (2, 2048), (2, 4, 128, 128), (2, 512), (1024,))


def _pack_small(parts):
    return jnp.concatenate([p.reshape(-1, LANES) for p in parts], axis=0)


def _unpack_small(packed):
    out, row = [], 0
    for shape in SMALL_SHAPES:
        n = 1
        for d in shape:
            n *= d
        out.append(packed[row:row + n // LANES].reshape(shape))
        row += n // LANES
    return out


def kernel(x, norm_g, w_in, b_gate, pool_w, pool_scale, w_pool_up, w_attn_up, w_out, final_g, loss_target, m_norm_g, m_w_in, m_b_gate, m_pool_w, m_pool_scale, m_w_pool_up, m_w_attn_up, m_w_out, m_final_g, v_norm_g, v_w_in, v_b_gate, v_pool_w, v_pool_scale, v_w_pool_up, v_w_attn_up, v_w_out, v_final_g):
    _, _, c, s = _place()
    pos = jnp.stack([c, s]).astype(jnp.int32)
    names = ("w_in", "w_pool_up", "w_attn_up", "w_out")

    full = _gather_weights([_cast_into_place(w, pos, axis, width, f"cast_{n}")
                            for w, (axis, width), n in zip((w_in, w_pool_up, w_attn_up, w_out), SHARDED, names)])
    loss_part, dx, d_final_g, small, dws = _local_step(x[0], loss_target[0], norm_g, b_gate, pool_w, pool_scale,
                                                       final_g, *full)

    other = _sibling_exchange(dws)
    pair = [_pair_sum(d, o, pos, f"grad_pair_sum_{n}") for d, o, n in zip(dws, other, names)]
    landed = _chip_exchange([pb for _, pb in pair])
    mine = [_shard_sum(p, l, pos, axis, width, f"grad_shard_sum_{n}")
            for (p, _), l, (axis, width), n in zip(pair, landed, SHARDED, names)]
    g_in, g_pu, g_au, g_out = _sibling_share(mine)

    summed = _all_reduce_small(_pack_small(small + [d_final_g, jnp.broadcast_to(loss_part, (8, LANES))]))
    g_small = _unpack_small(summed)
    loss = summed[-8, 0]
    upd_small = _adamw(_pack_small([norm_g, b_gate, pool_w, pool_scale, final_g]), _pack_small(g_small),
                       _pack_small([m_norm_g, m_b_gate, m_pool_w, m_pool_scale, m_final_g]),
                       _pack_small([v_norm_g, v_b_gate, v_pool_w, v_pool_scale, v_final_g]), "adamw_small")
    d_small, nm_small, nv_small = [_unpack_small(t) for t in upd_small]
    upd_in = _adamw(w_in, g_in, m_w_in, v_w_in, "adamw_w_in")
    upd_pu = _adamw(w_pool_up, g_pu, m_w_pool_up, v_w_pool_up, "adamw_w_pool_up")
    upd_au = _adamw(w_attn_up, g_au, m_w_attn_up, v_w_attn_up, "adamw_w_attn_up")
    upd_out = _adamw(w_out, g_out, m_w_out, v_w_out, "adamw_w_out")

    def ordered(sm, k):
        big = (upd_in[k], upd_pu[k], upd_au[k], upd_out[k]) if k is not None else (g_in, g_pu, g_au, g_out)
        return [sm[0], big[0], sm[1], sm[2], sm[3], big[1], big[2], big[3], sm[4]]

    return (loss, dx[None], *ordered(g_small, None), *ordered(d_small, 0), *ordered(nm_small, 1),
            *ordered(nv_small, 2))
```

```python
import jax
import jax.numpy as jnp
import numpy as np
from jax import lax
from jax.experimental import pallas as pl
from jax.experimental.pallas import tpu as pltpu

F32 = jnp.float32
BF16 = jnp.bfloat16
MESH = pl.DeviceIdType.MESH

D_MODEL = 1024
POOL_WIDTH = 512
POOL_WINDOWS = (2, 4, 8, 16)
POOL_GROUP = 128
POOL_HALO = 16
ATTN_WIDTH = 512
HEAD_DIM = 64
HEAD_PAIRS = 4
IN_WIDTH = 5120
N_CHIPS = 4
RMS_EPS = 1e-6
C_U, C_ZP, C_Q, C_K, C_V, C_ZA, C_GL = 0, 512, 1024, 1536, 2048, 2560, 3072

ADAM_LR, ADAM_B1, ADAM_B2, ADAM_EPS, ADAM_WD, ADAM_STEP = 0.001, 0.9, 0.999, 1e-08, 0.01, 10

LANES = 128
ATTN_BLOCK = 256
ROW_TILE = 256
VMEM_LIMIT = 56 * 1024 * 1024


def _params(**kw):
    return pltpu.CompilerParams(vmem_limit_bytes=VMEM_LIMIT, **kw)


def _nt(a, b):
    return lax.dot_general(a, b, (((1,), (1,)), ((), ())), preferred_element_type=F32)


def _tn(a, b):
    return lax.dot_general(a, b, (((0,), (0,)), ((), ())), preferred_element_type=F32)


def _nn(a, b):
    return jnp.dot(a, b, preferred_element_type=F32)


def _sigmoid(z):
    return 1.0 / (1.0 + jnp.exp(-z))


def _rms_inproj(x, g, w_in, layer):
    S = x.shape[0]
    tm = min(ROW_TILE, S)

    def body(x_ref, g_ref, w_ref, u_ref, zp_ref, q_ref, k_ref, v_ref, za_ref, gl_ref, h_ref):
        xv = x_ref[...]
        r = lax.rsqrt(jnp.mean(xv * xv, axis=-1, keepdims=True) + RMS_EPS)
        h = ((xv * r) * g_ref[...]).astype(BF16)
        h_ref[...] = h

        def mm(c0, n):
            return _nn(h, w_ref[:, c0:c0 + n])

        u_ref[...] = mm(C_U, 512)
        zp_ref[...] = mm(C_ZP, 512)
        q_ref[...] = (mm(C_Q, 512) * 0.125).astype(BF16)
        k_ref[...] = mm(C_K, 512).astype(BF16)
        v_ref[...] = mm(C_V, 512).astype(BF16)
        za_ref[...] = mm(C_ZA, 512)
        for c in range(4):
            gl_ref[:, c * 512:(c + 1) * 512] = mm(C_GL + c * 512, 512)

    row = lambda n: pl.BlockSpec((tm, n), lambda i: (i, 0))
    sd = lambda n, dt: jax.ShapeDtypeStruct((S, n), dt)
    return pl.pallas_call(
        body, name=f"rms_inproj_l{layer}", grid=(S // tm,),
        in_specs=[row(D_MODEL), pl.BlockSpec((1, D_MODEL), lambda i: (0, 0)),
                  _layer_weight_spec(D_MODEL, IN_WIDTH, layer)],
        out_specs=[row(512), row(512), row(512), row(512), row(512), row(512), row(2048), row(D_MODEL)],
        out_shape=[sd(512, F32), sd(512, F32), sd(512, BF16), sd(512, BF16), sd(512, BF16), sd(512, F32),
                   sd(2048, F32), sd(D_MODEL, BF16)],
        compiler_params=_params(),
    )(x, g, w_in)


def _tri(n, strict_lower):
    r = lax.broadcasted_iota(jnp.int32, (n, n), 0)
    c = lax.broadcasted_iota(jnp.int32, (n, n), 1)
    return jnp.where(r > c if strict_lower else r < c, 1.0, 0.0).astype(BF16)


def _split_dot(x, m):
    hi = x.astype(BF16)
    lo = (x - hi.astype(F32)).astype(BF16)
    return _nn(hi, m) + _nn(lo, m)


def _log_terms(z):
    lg = jnp.log(1.0 + jnp.exp(-jnp.abs(z)))
    a = jnp.minimum(z, 0.0) - lg
    return a, a - z


EXHAUSTED = -104.0
UNREACHED = -1e30


class _HeadPair:
    def __init__(self, T):
        self.T = T
        self.first = lax.broadcasted_iota(jnp.int32, (T, LANES), 1) < HEAD_DIM
        self.lane = lax.broadcasted_iota(jnp.int32, (2 * T, LANES), 1)
        row = lax.broadcasted_iota(jnp.int32, (2 * T, T), 0)
        row = jnp.where(row >= T, row - T, row)
        self.causal = row > lax.broadcasted_iota(jnp.int32, (2 * T, T), 1)
        self.below = _tri(T, True)

    def stack(self, x2):
        return jnp.concatenate([jnp.where(self.first, x2, 0), jnp.where(self.first, 0, x2)], axis=0).astype(BF16)

    def unstack(self, x):
        return jnp.where(self.first, x[:self.T], x[self.T:])

    def keys(self, ref, blocks):
        T = self.T
        return jnp.concatenate([ref[pl.ds(pl.multiple_of(j * T, T), T), :] for j, _ in blocks], axis=0)

    def log_terms(self, z, blocks):
        T = self.T
        a_all, l_all = _log_terms(z)
        a = [a_all[:, b * T:(b + 1) * T] for b in range(len(blocks))]
        l1m = [l_all[:, b * T:(b + 1) * T] for b in range(len(blocks))]
        return a, [jnp.where(self.causal, l, 0.0) if diagonal else l for l, (_, diagonal) in zip(l1m, blocks)]

    def later_sums(self, l1m):
        later = _split_dot(jnp.concatenate(l1m, axis=0), self.below)
        return [later[2 * self.T * b:2 * self.T * (b + 1)] for b in range(len(l1m))]


def _halves(x):
    return x.reshape(2, x.shape[0] // 2, x.shape[1])


def _attn_fwd(q, k, v, layer):
    S = q.shape[0]
    T = min(ATTN_BLOCK, S)
    nq = S // T
    assert nq <= LANES and nq % 2 == 0
    half = nq // 2

    def body(q_ref, k_ref, v_ref, o_ref, c_ref):
        i = pl.program_id(1)
        pair = _HeadPair(T)
        qs = [pair.stack(q_ref[0]), pair.stack(q_ref[1])]
        diag = [i, i + half]

        def sweep(jobs):
            kv = [(pair.keys(k_ref, bl), pair.keys(v_ref, bl)) for _, bl, _ in jobs]
            zs = [_nt(qs[n], kcat) for (n, _, _), (kcat, _) in zip(jobs, kv)]
            terms = [pair.log_terms(z, bl) for (_, bl, _), z in zip(jobs, zs)]
            laters = [pair.later_sums(l1m) for _, l1m in terms]
            weights = []
            for (_, bl, (acc, run, saved)), (a, l1m), later in zip(jobs, terms, laters):
                ws = []
                for b, (j, diagonal) in enumerate(bl):
                    saved = jnp.where(pair.lane == j, run, saved)
                    w = jnp.exp(a[b] + later[b] + run)
                    ws.append(jnp.where(pair.causal, w, 0.0) if diagonal else w)
                    run = run + jnp.sum(l1m[b], axis=1, keepdims=True)
                weights.append((jnp.concatenate(ws, axis=1).astype(BF16), acc, run, saved))
            return [(acc + _nn(w, vcat), run, saved) for (w, acc, run, saved), (_, vcat) in zip(weights, kv)]

        def alive(carry):
            return (jnp.max(carry[1]) > EXHAUSTED).astype(jnp.int32)

        def older_blocks(n, carry):
            def older_block(state):
                j, _, c = state
                c = sweep([(n, [(j, False)], c)])[0]
                return j - 1, alive(c), c

            return lax.while_loop(lambda st: jnp.logical_and(st[0] >= 0, st[1] > 0), older_block,
                                  (diag[n] - 2, alive(carry), carry))[2]

        def run(first_blocks):
            init = (jnp.zeros((2 * T, LANES), F32), jnp.zeros((2 * T, 1), F32),
                    jnp.full((2 * T, LANES), UNREACHED, F32))
            carries = sweep([(n, first_blocks[n], init) for n in range(2)])
            for n in range(2):
                acc, _, saved = older_blocks(n, carries[n])
                o_ref[n] = pair.unstack(acc)
                c_ref[n, :, :LANES] = saved[:T]
                c_ref[n, :, LANES:] = saved[T:]

        with_previous = lambda d: [(d, True), (d - 1, False)]

        @pl.when(i == 0)
        def _():
            run([[(diag[0], True)], with_previous(diag[1])])

        @pl.when(i > 0)
        def _():
            run([with_previous(diag[0]), with_previous(diag[1])])

    blk = lambda n: pl.BlockSpec((2, T, n), lambda p, i: (0, i, p))
    full = pl.BlockSpec((S, LANES), lambda p, i: (0, p))
    o, carry = pl.pallas_call(
        body, name=f"attn_fwd_l{layer}", grid=(HEAD_PAIRS, half),
        in_specs=[blk(LANES), full, full],
        out_specs=[blk(LANES), blk(2 * LANES)],
        out_shape=[jax.ShapeDtypeStruct((2, S // 2, ATTN_WIDTH), F32),
                   jax.ShapeDtypeStruct((2, S // 2, 8 * LANES), F32)],
        compiler_params=_params(),
    )(_halves(q), k, v)
    return o.reshape(S, ATTN_WIDTH), carry.reshape(S, 8 * LANES)


def _attn_bwd(q, k, v, saved, do, layer):
    S = q.shape[0]
    T = min(ATTN_BLOCK, S)
    nq = S // T

    half = nq // 2

    def body(q_ref, k_ref, v_ref, c_ref, do_ref, dq_ref, dk_ref, dv_ref):
        i = pl.program_id(1)

        @pl.when(i == 0)
        def _():
            dk_ref[...] = jnp.zeros_like(dk_ref)
            dv_ref[...] = jnp.zeros_like(dv_ref)

        pair = _HeadPair(T)
        diag = [i, i + half]
        qs = [pair.stack(q_ref[n]) for n in range(2)]
        dos = [pair.stack(do_ref[n].astype(BF16)) for n in range(2)]
        saved = [jnp.concatenate([c_ref[n, :, :LANES], c_ref[n, :, LANES:]], axis=0) for n in range(2)]
        before = _tri(T, False)

        def sweep(jobs):
            kv = [(pair.keys(k_ref, bl), pair.keys(v_ref, bl)) for _, bl, _ in jobs]
            zs = [_nt(qs[n], kcat) for (n, _, _), (kcat, _) in zip(jobs, kv)]
            gs = [_nt(dos[n], vcat) for (n, _, _), (_, vcat) in zip(jobs, kv)]
            terms = [pair.log_terms(z, bl) for (_, bl, _), z in zip(jobs, zs)]
            laters = [pair.later_sums(l1m) for _, l1m in terms]
            ws, es = [], []
            for (n, bl, _), (a, _), later, g in zip(jobs, terms, laters, gs):
                w_job, e_job = [], []
                for b, (j, diagonal) in enumerate(bl):
                    run = jnp.sum(jnp.where(pair.lane == j, saved[n], 0.0), axis=1, keepdims=True)
                    w = jnp.exp(a[b] + later[b] + run)
                    w_job.append(jnp.where(pair.causal, w, 0.0) if diagonal else w)
                    e_job.append(w_job[b] * g[:, b * T:(b + 1) * T])
                ws.append(w_job)
                es.append(e_job)
            prefixes = [_split_dot(jnp.concatenate(e_job, axis=0), before) for e_job in es]
            dzs, olders = [], []
            for (_, bl, (_, older)), (a, _), e_job, prefix in zip(jobs, terms, es, prefixes):
                dz_job = []
                for b, (j, diagonal) in enumerate(bl):
                    dz = e_job[b] - jnp.exp(a[b]) * (e_job[b] + (prefix[2 * T * b:2 * T * (b + 1)] + older))
                    dz_job.append(jnp.where(pair.causal, dz, 0.0) if diagonal else dz)
                    older = older + jnp.sum(e_job[b], axis=1, keepdims=True)
                dzs.append(jnp.concatenate(dz_job, axis=1).astype(BF16))
                olders.append(older)
            out = []
            for (n, bl, (dq, _)), dz, w_job, older, (kcat, _) in zip(jobs, dzs, ws, olders, kv):
                dk = _tn(dz, qs[n])
                dv = _tn(jnp.concatenate(w_job, axis=1).astype(BF16), dos[n])
                for b, (j, _) in enumerate(bl):
                    rows = pl.ds(pl.multiple_of(j * T, T), T)
                    dk_ref[rows, :] += dk[b * T:(b + 1) * T]
                    dv_ref[rows, :] += dv[b * T:(b + 1) * T]
                out.append((dq + _nn(dz, kcat), older))
            return out

        def older_blocks(n):
            col_max = jnp.max(saved[n], axis=0, keepdims=True)
            lane_row = lax.broadcasted_iota(jnp.int32, (1, LANES), 1)
            reached = jnp.sum(jnp.where(jnp.logical_and(col_max > EXHAUSTED, lane_row < diag[n]), 1, 0))
            init = (jnp.zeros((2 * T, LANES), F32), jnp.zeros((2 * T, 1), F32))
            return lax.fori_loop(diag[n] - reached, diag[n] - 1, lambda j, c: sweep([(n, [(j, False)], c)])[0], init)

        def run(last_blocks):
            carries = sweep([(n, last_blocks[n], older_blocks(n)) for n in range(2)])
            for n in range(2):
                dq_ref[n] = pair.unstack(carries[n][0]) * 0.125

        with_previous = lambda d: [(d - 1, False), (d, True)]

        @pl.when(i == 0)
        def _():
            run([[(diag[0], True)], with_previous(diag[1])])

        @pl.when(i > 0)
        def _():
            run([with_previous(diag[0]), with_previous(diag[1])])

    blk = lambda n: pl.BlockSpec((2, T, n), lambda p, i: (0, i, p))
    full = pl.BlockSpec((S, LANES), lambda p, i: (0, p))
    out = jax.ShapeDtypeStruct((S, ATTN_WIDTH), F32)
    dq, dk, dv = pl.pallas_call(
        body, name=f"attn_bwd_l{layer}", grid=(HEAD_PAIRS, half),
        in_specs=[blk(LANES), full, full, blk(2 * LANES), blk(LANES)],
        out_specs=[blk(LANES), full, full],
        out_shape=[jax.ShapeDtypeStruct((2, S // 2, ATTN_WIDTH), F32), out, out],
        compiler_params=_params(),
    )(_halves(q), k, v, _halves(saved), _halves(do))
    return dq.reshape(S, ATTN_WIDTH), dk, dv


def _pool_counts(row0, tm):
    pos = row0 + lax.broadcasted_iota(jnp.int32, (tm, 1), 0)
    return [1.0 / jnp.minimum(pos + 1, w).astype(F32) for w in POOL_WINDOWS]


def _window_bands(tm, backward):
    t = np.arange(tm)[:, None]
    c = np.arange(tm + POOL_HALO)[None, :]
    off = c - t if backward else t + POOL_HALO - c
    return jnp.asarray(np.stack([(off >= 0) & (off < w) for w in POOL_WINDOWS]), BF16)


def _window_sums(ext, band_ref):
    hi = ext.astype(BF16)
    lo = (ext - hi.astype(F32)).astype(BF16)
    sums = []
    for g in range(len(POOL_WINDOWS)):
        cols = slice(g * POOL_GROUP, (g + 1) * POOL_GROUP)
        sums.append(_nn(band_ref[g], hi[:, cols]) + _nn(band_ref[g], lo[:, cols]))
    return sums


def _post_forward(ext, band_ref, inv_cnt, zp, o, za, gl, bg, pw_ref, scale, wpu_ref, wau_ref):
    pooled, mixed = [], []
    for g, tot in enumerate(_window_sums(ext, band_ref)):
        pg = (tot * inv_cnt[g] - ext[POOL_HALO:, g * POOL_GROUP:(g + 1) * POOL_GROUP]).astype(BF16)
        pooled.append(pg)
        mixed.append(_nn(pg, pw_ref[g].astype(BF16)))
    pooled = jnp.concatenate(pooled, axis=1)
    mixed = jnp.concatenate(mixed, axis=1)
    sp = _sigmoid(zp)
    sa = _sigmoid(za)
    y_pool = (mixed * scale) * (zp * sp)
    y_attn = o * (za * sa)
    gate = _sigmoid(gl + bg)
    g0, g1 = gate[:, :D_MODEL], gate[:, D_MODEL:]
    up_p = _nn(y_pool.astype(BF16), wpu_ref[...])
    up_a = _nn(y_attn.astype(BF16), wau_ref[...])
    merged = g0 * up_p + g1 * up_a
    return pooled, mixed, sp, sa, y_pool, y_attn, g0, g1, up_p, up_a, merged


def _row_specs(tm, rev, n_tiles):
    tile_of = (lambda i: n_tiles - 1 - i) if rev else (lambda i: i)
    row = lambda n: pl.BlockSpec((tm, n), lambda i: (tile_of(i), 0))
    halo = pl.BlockSpec((POOL_HALO, POOL_WIDTH),
                        lambda i: (jnp.maximum(tile_of(i) * (tm // POOL_HALO) - 1, 0), 0))
    const = lambda shape: pl.BlockSpec(shape, lambda i: (0,) * len(shape))
    return tile_of, row, halo, const


def _layer_weight_spec(rows, cols, layer):
    return pl.BlockSpec((None, rows, cols), lambda i: (layer, 0, 0), pipeline_mode=pl.Buffered(1))


def _post_fwd(x, u, zp, o, za, gl, bg, pw, scale, wpu, wau, wout, layer):
    S = x.shape[0]
    tm = min(ROW_TILE, S)
    n_tiles = S // tm
    tile_of, row, halo, const = _row_specs(tm, False, n_tiles)

    def body(x_ref, u_ref, uh_ref, band_ref, zp_ref, o_ref, za_ref, gl_ref, bg_ref, pw_ref, sc_ref, wpu_ref, wau_ref,
             wout_ref, out_ref):
        i = pl.program_id(0)
        ext = jnp.concatenate([jnp.where(i == 0, 0.0, uh_ref[...]), u_ref[...]], axis=0)
        vals = _post_forward(ext, band_ref, _pool_counts(i * tm, tm), zp_ref[...], o_ref[...], za_ref[...],
                             gl_ref[...], bg_ref[...], pw_ref, sc_ref[...], wpu_ref, wau_ref)
        out_ref[...] = x_ref[...] + _nn(vals[-1].astype(BF16), wout_ref[...])

    return pl.pallas_call(
        body, name=f"post_fwd_l{layer}", grid=(n_tiles,),
        in_specs=[row(D_MODEL), row(512), halo, const((4, tm, tm + POOL_HALO)), row(512), row(512), row(512),
                  row(2048), const((1, 2048)), const((4, POOL_GROUP, POOL_GROUP)), const((1, POOL_WIDTH)),
                  _layer_weight_spec(POOL_WIDTH, D_MODEL, layer), _layer_weight_spec(ATTN_WIDTH, D_MODEL, layer),
                  _layer_weight_spec(D_MODEL, D_MODEL, layer)],
        out_specs=row(D_MODEL),
        out_shape=jax.ShapeDtypeStruct((S, D_MODEL), F32),
        compiler_params=_params(),
    )(x, u, u, _window_bands(tm, False), zp, o, za, gl, bg, pw, scale, wpu, wau, wout)


def _post_bwd(dx, u, zp, o, za, gl, bg, pw, scale, wpu, wau, wout, layer):
    S = dx.shape[0]
    tm = min(ROW_TILE, S)
    n_tiles = S // tm
    tile_of, row, halo, const = _row_specs(tm, True, n_tiles)

    def body(dx_ref, u_ref, uh_ref, band_ref, back_ref, zp_ref, o_ref, za_ref, gl_ref, bg_ref, pw_ref, sc_ref, wpu_ref,
             wau_ref, wout_ref,
             duz_ref, do_ref, dza_ref, dgl_ref, dsc_ref, dbg_ref,
             merged_ref, dup_ref, dua_ref, yp_ref, ya_ref, pooled_ref, dmixed_ref, nxt_ref):
        step = pl.program_id(0)
        i = tile_of(step)

        @pl.when(step == 0)
        def _():
            dsc_ref[...] = jnp.zeros_like(dsc_ref)
            dbg_ref[...] = jnp.zeros_like(dbg_ref)
            nxt_ref[...] = jnp.zeros_like(nxt_ref)

        ext = jnp.concatenate([jnp.where(i == 0, 0.0, uh_ref[...]), u_ref[...]], axis=0)
        inv_cnt = _pool_counts(i * tm, tm)
        zp, za, o = zp_ref[...], za_ref[...], o_ref[...]
        pooled, mixed, sp, sa, y_pool, y_attn, g0, g1, up_p, up_a, merged = _post_forward(
            ext, band_ref, inv_cnt, zp, o, za, gl_ref[...], bg_ref[...], pw_ref, sc_ref[...], wpu_ref, wau_ref)
        merged_ref[...] = merged.astype(BF16)
        yp_ref[...] = y_pool.astype(BF16)
        ya_ref[...] = y_attn.astype(BF16)
        pooled_ref[...] = pooled

        dmerged = _nt(dx_ref[...].astype(BF16), wout_ref[...])
        dup = (dmerged * g0).astype(BF16)
        dua = (dmerged * g1).astype(BF16)
        dup_ref[...] = dup
        dua_ref[...] = dua
        dgl0 = (dmerged * up_p) * (g0 * (1.0 - g0))
        dgl1 = (dmerged * up_a) * (g1 * (1.0 - g1))
        dgl_ref[:, :D_MODEL] = dgl0
        dgl_ref[:, D_MODEL:] = dgl1
        dbg_ref[:, :D_MODEL] += jnp.sum(dgl0, axis=0, keepdims=True)
        dbg_ref[:, D_MODEL:] += jnp.sum(dgl1, axis=0, keepdims=True)

        dy_attn = _nt(dua, wau_ref[...])
        do_ref[...] = dy_attn * (za * sa)
        dza_ref[...] = (dy_attn * o) * (sa * (1.0 + za * (1.0 - sa)))

        dy_pool = _nt(dup, wpu_ref[...])
        ms = mixed * sc_ref[...]
        dms = dy_pool * (zp * sp)
        duz_ref[:, POOL_WIDTH:] = (dy_pool * ms) * (sp * (1.0 + zp * (1.0 - sp)))
        dsc_ref[...] += jnp.sum(dms * mixed, axis=0, keepdims=True)
        dmixed = (dms * sc_ref[...]).astype(BF16)
        dmixed_ref[...] = dmixed
        dpooled = [_nt(dmixed[:, g * POOL_GROUP:(g + 1) * POOL_GROUP], pw_ref[g].astype(BF16)) for g in range(4)]
        scaled = jnp.concatenate([d * inv for d, inv in zip(dpooled, inv_cnt)], axis=1)
        for g, tot in enumerate(_window_sums(jnp.concatenate([scaled, nxt_ref[...]], axis=0), back_ref)):
            duz_ref[:, g * POOL_GROUP:(g + 1) * POOL_GROUP] = tot - dpooled[g]
        nxt_ref[...] = scaled[:POOL_HALO]

    sd = lambda n, dt: jax.ShapeDtypeStruct((S, n), dt)
    bands = const((4, tm, tm + POOL_HALO))
    return pl.pallas_call(
        body, name=f"post_bwd_l{layer}", grid=(n_tiles,),
        in_specs=[row(D_MODEL), row(512), halo, bands, bands, row(512), row(512), row(512), row(2048),
                  const((1, 2048)), const((4, POOL_GROUP, POOL_GROUP)), const((1, POOL_WIDTH)),
                  _layer_weight_spec(POOL_WIDTH, D_MODEL, layer), _layer_weight_spec(ATTN_WIDTH, D_MODEL, layer),
                  _layer_weight_spec(D_MODEL, D_MODEL, layer)],
        out_specs=[row(1024), row(512), row(512), row(2048), const((1, POOL_WIDTH)), const((1, 2048)),
                   row(D_MODEL), row(D_MODEL), row(D_MODEL), row(512), row(512), row(512), row(512)],
        out_shape=[sd(1024, F32), sd(512, F32), sd(512, F32), sd(2048, F32),
                   jax.ShapeDtypeStruct((1, POOL_WIDTH), F32), jax.ShapeDtypeStruct((1, 2048), F32),
                   sd(D_MODEL, BF16), sd(D_MODEL, BF16), sd(D_MODEL, BF16), sd(512, BF16), sd(512, BF16),
                   sd(512, BF16), sd(512, BF16)],
        scratch_shapes=[pltpu.VMEM((POOL_HALO, POOL_WIDTH), F32)],
        compiler_params=_params(),
    )(dx, u, u, _window_bands(tm, False), _window_bands(tm, True), zp, o, za, gl, bg, pw, scale, wpu, wau, wout)


def _rms_backward(dh, xv, r, g):
    xhat = xv * r
    dxhat = dh * g
    return r * (dxhat - xhat * jnp.mean(dxhat * xhat, axis=-1, keepdims=True)), dh * xhat


def _loss_head(x, g, target):
    S = x.shape[0]
    tm = min(ROW_TILE, S)

    def body(x_ref, g_ref, t_ref, loss_ref, dx_ref, dg_ref):
        @pl.when(pl.program_id(0) == 0)
        def _():
            loss_ref[...] = jnp.zeros_like(loss_ref)
            dg_ref[...] = jnp.zeros_like(dg_ref)

        xv = x_ref[...]
        r = lax.rsqrt(jnp.mean(xv * xv, axis=-1, keepdims=True) + RMS_EPS)
        diff = (xv * r) * g_ref[...] - t_ref[...]
        per_row = jnp.mean(diff * diff, axis=-1, keepdims=True)
        loss_ref[...] += 0.5 * jnp.sum(per_row, axis=0, keepdims=True)
        dx, dg_rows = _rms_backward(diff * (1.0 / D_MODEL), xv, r, g_ref[...])
        dx_ref[...] = dx
        dg_ref[...] += jnp.sum(dg_rows, axis=0, keepdims=True)

    row = pl.BlockSpec((tm, D_MODEL), lambda i: (i, 0))
    vec = pl.BlockSpec((1, D_MODEL), lambda i: (0, 0))
    return pl.pallas_call(
        body, name="loss_head", grid=(S // tm,),
        in_specs=[row, vec, row],
        out_specs=[pl.BlockSpec((1, LANES), lambda i: (0, 0)), row, vec],
        out_shape=[jax.ShapeDtypeStruct((1, LANES), F32), jax.ShapeDtypeStruct((S, D_MODEL), F32),
                   jax.ShapeDtypeStruct((1, D_MODEL), F32)],
        compiler_params=_params(),
    )(x, g, target)


def _inproj_bwd(pieces, w_in, x, g, dx_res, layer):
    S = x.shape[0]
    tm = min(ROW_TILE, S)
    cols = [(c0, p.shape[1]) for p, c0 in pieces]

    def body(*refs):
        piece_refs = refs[:len(cols)]
        w_ref, x_ref, g_ref, res_ref, dx_ref, dg_ref = refs[len(cols):]

        @pl.when(pl.program_id(0) == 0)
        def _():
            dg_ref[...] = jnp.zeros_like(dg_ref)

        dh = jnp.zeros((tm, D_MODEL), F32)
        for p_ref, (c0, n) in zip(piece_refs, cols):
            for c in range(0, n, 512):
                dh = dh + _nt(p_ref[:, c:c + 512].astype(BF16), w_ref[:, c0 + c:c0 + c + 512])
        xv = x_ref[...]
        r = lax.rsqrt(jnp.mean(xv * xv, axis=-1, keepdims=True) + RMS_EPS)
        dx, dg_rows = _rms_backward(dh, xv, r, g_ref[...])
        dx_ref[...] = res_ref[...] + dx
        dg_ref[...] += jnp.sum(dg_rows, axis=0, keepdims=True)

    row = lambda n: pl.BlockSpec((tm, n), lambda i: (i, 0))
    vec = pl.BlockSpec((1, D_MODEL), lambda i: (0, 0))
    return pl.pallas_call(
        body, name=f"inproj_bwd_l{layer}", grid=(S // tm,),
        in_specs=[row(n) for _, n in cols] + [_layer_weight_spec(D_MODEL, IN_WIDTH, layer), row(D_MODEL), vec,
                                              row(D_MODEL)],
        out_specs=[row(D_MODEL), vec],
        out_shape=[jax.ShapeDtypeStruct((S, D_MODEL), F32), jax.ShapeDtypeStruct((1, D_MODEL), F32)],
        compiler_params=_params(),
    )(*[p for p, _ in pieces], w_in, x, g, dx_res)


def _wgrad(a, b, name, layer, into=None, col0=0, n_total=None):
    S, M = a.shape
    N = b.shape[1]
    n_total = N if n_total is None else n_total
    tk = min(2048, S)
    tn = min(512, N)
    nk = S // tk

    def body(*refs):
        a_ref, b_ref, out_ref = refs[0], refs[1], refs[-1]
        prod = _tn(a_ref[...].astype(BF16), b_ref[...].astype(BF16))

        @pl.when(pl.program_id(1) == 0)
        def _():
            out_ref[...] = prod

        @pl.when(pl.program_id(1) > 0)
        def _():
            out_ref[...] += prod

    in_specs = [pl.BlockSpec((tk, M), lambda j, k: (k, 0)), pl.BlockSpec((tk, tn), lambda j, k: (k, j))]
    args = [a, b]
    aliases = {}
    if into is not None:
        in_specs.append(pl.BlockSpec(memory_space=pl.ANY))
        args.append(into)
        aliases = {2: 0}
    return pl.pallas_call(
        body, name=name, grid=(N // tn, nk),
        in_specs=in_specs,
        out_specs=pl.BlockSpec((None, M, tn), lambda j, k: (layer, 0, col0 // tn + j)),
        out_shape=jax.ShapeDtypeStruct((2, M, n_total), F32),
        input_output_aliases=aliases,
        compiler_params=_params(),
    )(*args)


def _pool_wgrad(pooled, dmixed, layer):
    S = pooled.shape[0]
    tk = min(1024, S)

    def body(a_ref, b_ref, out_ref):
        prod = _tn(a_ref[...], b_ref[...])

        @pl.when(pl.program_id(1) == 0)
        def _():
            out_ref[...] = prod

        @pl.when(pl.program_id(1) > 0)
        def _():
            out_ref[...] += prod

    blk = pl.BlockSpec((tk, POOL_GROUP), lambda g, k: (k, g))
    return pl.pallas_call(
        body, name=f"pool_wgrad_l{layer}", grid=(4, S // tk),
        in_specs=[blk, blk],
        out_specs=pl.BlockSpec((None, POOL_GROUP, POOL_GROUP), lambda g, k: (g, 0, 0)),
        out_shape=jax.ShapeDtypeStruct((4, POOL_GROUP, POOL_GROUP), F32),
        compiler_params=_params(),
    )(pooled, dmixed)


def _local_step(x, target, norm_g, b_gate, pool_w, pool_scale, final_g, w_in, w_pu, w_au, w_out):
    n_layers = norm_g.shape[0]
    saved = []
    for l in range(n_layers):
        g = norm_g[l][None]
        bg = b_gate[l][None]
        sc = pool_scale[l][None]
        u, zp, q, k, v, za, gl, h = _rms_inproj(x, g, w_in, l)
        o, carry = _attn_fwd(q, k, v, l)
        saved.append((x, g, bg, sc, u, zp, q, k, v, za, gl, h, o, carry))
        x = _post_fwd(x, u, zp, o, za, gl, bg, pool_w[l], sc, w_pu, w_au, w_out, l)
    loss, dx, d_final_g = _loss_head(x, final_g[None], target)

    small = [None] * n_layers
    dw_in = dw_out = dw_pu = dw_au = None
    for l in reversed(range(n_layers)):
        x_in, g, bg, sc, u, zp, q, k, v, za, gl, h, o, carry = saved[l]
        (duz, do, dza, dgl, dsc, dbg, merged, dup, dua, y_pool, y_attn, pooled, dmixed) = _post_bwd(
            dx, u, zp, o, za, gl, bg, pool_w[l], sc, w_pu, w_au, w_out, l)
        dq, dk, dv = _attn_bwd(q, k, v, carry, do, l)
        pieces = [(duz, C_U), (dq, C_Q), (dk, C_K), (dv, C_V), (dza, C_ZA), (dgl, C_GL)]
        for p, c0 in pieces:
            dw_in = _wgrad(h, p, f"wgrad_in_l{l}_c{c0}", l, into=dw_in, col0=c0, n_total=IN_WIDTH)
        dw_out = _wgrad(merged, dx, f"wgrad_out_l{l}", l, into=dw_out)
        dw_pu = _wgrad(y_pool, dup, f"wgrad_pu_l{l}", l, into=dw_pu)
        dw_au = _wgrad(y_attn, dua, f"wgrad_au_l{l}", l, into=dw_au)
        dpw = _pool_wgrad(pooled, dmixed, l)
        dx, dg = _inproj_bwd(pieces, w_in, x_in, g, dx, l)
        small[l] = (dg[0], dbg[0], dpw, dsc[0])
    small = [jnp.stack([small[l][i] for l in range(n_layers)]) for i in range(4)]
    return loss[0, 0], dx, d_final_g[0], small, (dw_in, dw_pu, dw_au, dw_out)


SHARDED = ((2, 1280), (2, 256), (2, 256), (1, 256))
ANY = pl.BlockSpec(memory_space=pl.ANY)


def _part(ref, s, axis, width):
    sl = pl.ds(pl.multiple_of(s * width, width), width)
    return ref.at[:, sl] if axis == 2 else ref.at[sl, :]


def _place():
    x, y, c = lax.axis_index("x"), lax.axis_index("y"), lax.axis_index("c")
    return x, y, c, 2 * x + y


def _other_chip(x, y, m):
    px = 1 - x if m & 2 else x
    py = 1 - y if m & 1 else y
    return px, py, 2 * px + py


def _remote(src, dst, send, recv, k, to):
    return pltpu.make_async_remote_copy(src_ref=src, dst_ref=dst, send_sem=send.at[k], recv_sem=recv.at[k],
                                        device_id=to, device_id_type=MESH)


def _part_spec(tr, rows_s, cols_s, axis, width, lead):
    if axis == 2:
        return pl.BlockSpec((None, tr, width), lambda *a: (lead(a), a[-2], a[-1][1]))
    return pl.BlockSpec((None, tr, cols_s), lambda *a: (lead(a), a[-1][1] * (rows_s // tr) + a[-2], 0))


def _cast_into_place(w, pos, axis, width, name):
    L, Rs, Cs = w.shape
    tr = min(256, Rs)
    shape = [L, Rs, Cs]
    shape[axis] *= N_CHIPS

    def body(pos_ref, w_ref, o_ref):
        o_ref[...] = w_ref[...].astype(BF16)

    return pl.pallas_call(
        body, name=name,
        grid_spec=pltpu.PrefetchScalarGridSpec(
            num_scalar_prefetch=1, grid=(L, Rs // tr),
            in_specs=[pl.BlockSpec((None, tr, Cs), lambda l, i, pos: (l, i, 0))],
            out_specs=_part_spec(tr, Rs, Cs, axis, width, lambda a: a[0])),
        out_shape=jax.ShapeDtypeStruct(tuple(shape), BF16),
        compiler_params=_params(),
    )(pos, w)


def _gather_weights(fulls):
    n = len(SHARDED)

    def body(*refs):
        outs = refs[n:2 * n]
        send, recv = refs[2 * n:]
        x, y, c, s = _place()
        me, sibling = (x, y, c), (x, y, 1 - c)

        def copy(k, a, layer, shard, to):
            part = _part(outs[a].at[layer], shard, *SHARDED[a])
            return _remote(part, part, send, recv, k, to)

        sent = []
        for a in range(n):
            for m in (1, 2, 3):
                px, py, _ = _other_chip(x, y, m)
                sent.append(copy(3 * a + m - 1, a, c, s, (px, py, c)))
                sent[-1].start()
        for m in (1, 2, 3):
            _, _, sp = _other_chip(x, y, m)
            for a in range(n):
                copy(3 * a + m - 1, a, c, sp, me).wait_recv()
                sent.append(copy(3 * n + 3 * a + m - 1, a, c, sp, sibling))
                sent[-1].start()
        for m in (1, 2, 3):
            _, _, sp = _other_chip(x, y, m)
            for a in range(n):
                copy(3 * n + 3 * a + m - 1, a, 1 - c, sp, me).wait_recv()
        for cp in sent:
            cp.wait_send()

    return pl.pallas_call(
        body, name="gather_weights",
        in_specs=[ANY] * n, out_specs=[ANY] * n,
        out_shape=[jax.ShapeDtypeStruct(f.shape, f.dtype) for f in fulls],
        input_output_aliases={a: a for a in range(n)},
        scratch_shapes=[pltpu.SemaphoreType.DMA((6 * n,)), pltpu.SemaphoreType.DMA((6 * n,))],
    )(*fulls)


def _sibling_exchange(dws):
    n = len(dws)

    def body(*refs):
        ins, outs = refs[:n], refs[n:2 * n]
        send, recv = refs[2 * n:]
        x, y, c, _ = _place()
        copies = [_remote(ins[a].at[1 - c], outs[a], send, recv, a, (x, y, 1 - c)) for a in range(n)]
        for cp in copies:
            cp.start()
        for cp in copies:
            cp.wait()

    return pl.pallas_call(
        body, name="grad_sibling_exchange",
        in_specs=[ANY] * n, out_specs=[ANY] * n,
        out_shape=[jax.ShapeDtypeStruct(d.shape[1:], d.dtype) for d in dws],
        scratch_shapes=[pltpu.SemaphoreType.DMA((n,)), pltpu.SemaphoreType.DMA((n,))],
    )(*dws)


def _pair_sum(dw, other, pos, name):
    _, R, C = dw.shape
    tr = 128 if C > 1024 else 256

    def body(pos_ref, a_ref, b_ref, o_ref, ob_ref):
        tot = a_ref[...] + b_ref[...]
        o_ref[...] = tot
        ob_ref[...] = tot.astype(BF16)

    blk = pl.BlockSpec((tr, C), lambda i, pos: (i, 0))
    return pl.pallas_call(
        body, name=name,
        grid_spec=pltpu.PrefetchScalarGridSpec(
            num_scalar_prefetch=1, grid=(R // tr,),
            in_specs=[pl.BlockSpec((None, tr, C), lambda i, pos: (pos[0], i, 0)), blk],
            out_specs=[blk, blk]),
        out_shape=[jax.ShapeDtypeStruct((R, C), F32), jax.ShapeDtypeStruct((R, C), BF16)],
        compiler_params=_params(),
    )(pos, dw, other)


def _chip_exchange(ps):
    n = len(ps)

    def body(*refs):
        ins, outs = refs[:n], refs[n:2 * n]
        send, recv = refs[2 * n:]
        x, y, c, _ = _place()
        copies = []
        for a, (axis, width) in enumerate(SHARDED):
            for m in (1, 2, 3):
                px, py, sp = _other_chip(x, y, m)
                copies.append(_remote(_part(ins[a], sp, axis, width), outs[a].at[m - 1], send, recv, 3 * a + m - 1,
                                      (px, py, c)))
        for cp in copies:
            cp.start()
        for cp in copies:
            cp.wait()

    def landing(p, axis, width):
        shape = [3] + list(p.shape)
        shape[axis] = width
        return jax.ShapeDtypeStruct(tuple(shape), p.dtype)

    return pl.pallas_call(
        body, name="grad_chip_exchange",
        in_specs=[ANY] * n, out_specs=[ANY] * n,
        out_shape=[landing(p, axis, width) for p, (axis, width) in zip(ps, SHARDED)],
        scratch_shapes=[pltpu.SemaphoreType.DMA((3 * n,)), pltpu.SemaphoreType.DMA((3 * n,))],
    )(*ps)


def _shard_sum(p, landed, pos, axis, width, name):
    _, Rs, Cs = landed.shape
    tr = min(256, Rs)
    p_spec = _part_spec(tr, Rs, Cs, axis, width, lambda a: 0)

    def body(pos_ref, p_ref, l_ref, o_ref):
        o_ref[...] = ((p_ref[...] + l_ref[0].astype(F32)) + l_ref[1].astype(F32)) + l_ref[2].astype(F32)

    return pl.pallas_call(
        body, name=name,
        grid_spec=pltpu.PrefetchScalarGridSpec(
            num_scalar_prefetch=1, grid=(Rs // tr,),
            in_specs=[p_spec, pl.BlockSpec((3, tr, Cs), lambda i, pos: (0, i, 0))],
            out_specs=pl.BlockSpec((None, tr, Cs), lambda i, pos: (pos[0], i, 0))),
        out_shape=jax.ShapeDtypeStruct((2, Rs, Cs), F32),
        compiler_params=_params(),
    )(pos, p[None], landed)


def _sibling_share(gs):
    n = len(gs)

    def body(*refs):
        outs = refs[n:2 * n]
        send, recv = refs[2 * n:]
        x, y, c, _ = _place()
        copies = [_remote(outs[a].at[c], outs[a].at[c], send, recv, a, (x, y, 1 - c)) for a in range(n)]
        for cp in copies:
            cp.start()
        for a, cp in enumerate(copies):
            cp.wait_send()
            _remote(outs[a].at[1 - c], outs[a].at[1 - c], send, recv, a, (x, y, c)).wait_recv()

    return pl.pallas_call(
        body, name="grad_sibling_share",
        in_specs=[ANY] * n, out_specs=[ANY] * n,
        out_shape=[jax.ShapeDtypeStruct(g.shape, g.dtype) for g in gs],
        input_output_aliases={a: a for a in range(n)},
        scratch_shapes=[pltpu.SemaphoreType.DMA((n,)), pltpu.SemaphoreType.DMA((n,))],
    )(*gs)


N_DEVICES = 8


def _all_reduce_small(packed):
    rows = packed.shape[0]

    def body(in_ref, out_ref, all_ref, send, recv):
        x, y, c, _ = _place()
        my_id = 4 * x + 2 * y + c
        all_ref[my_id] = in_ref[...]
        copies = []
        for m in range(1, N_DEVICES):
            px = 1 - x if m & 4 else x
            py = 1 - y if m & 2 else y
            pc = 1 - c if m & 1 else c
            cp = _remote(in_ref, all_ref.at[my_id], send, recv, m - 1, (px, py, pc))
            cp.start()
            copies.append((cp, 4 * px + 2 * py + pc))
        for m, (cp, peer_id) in enumerate(copies):
            _remote(in_ref, all_ref.at[peer_id], send, recv, m, (x, y, c)).wait_recv()
        for cp, _ in copies:
            cp.wait_send()
        total = all_ref[0]
        for d in range(1, N_DEVICES):
            total = total + all_ref[d]
        out_ref[...] = total

    vmem = pl.BlockSpec(memory_space=pltpu.VMEM)
    return pl.pallas_call(
        body, name="small_all_reduce",
        in_specs=[vmem], out_specs=vmem,
        out_shape=jax.ShapeDtypeStruct(packed.shape, packed.dtype),
        scratch_shapes=[pltpu.VMEM((N_DEVICES, rows, LANES), F32), pltpu.SemaphoreType.DMA((N_DEVICES - 1,)),
                        pltpu.SemaphoreType.DMA((N_DEVICES - 1,))],
        compiler_params=_params(),
    )(packed)


def _adamw(w, g, m, v, name):
    shape = w.shape
    C = shape[-1]
    flat = [t.reshape(-1, C) for t in (w, g, m, v)]
    R = flat[0].shape[0]
    tr = max(t for t in range(8, R + 1, 8) if R % t == 0 and t * C <= 384 * 1024)

    def body(w_ref, g_ref, m_ref, v_ref, d_ref, nm_ref, nv_ref):
        gv = g_ref[...]
        nm = ADAM_B1 * m_ref[...] + (1.0 - ADAM_B1) * gv
        nv = ADAM_B2 * v_ref[...] + (1.0 - ADAM_B2) * (gv * gv)
        m_hat = nm / (1.0 - ADAM_B1 ** ADAM_STEP)
        v_hat = nv / (1.0 - ADAM_B2 ** ADAM_STEP)
        d_ref[...] = -ADAM_LR * (m_hat / (jnp.sqrt(v_hat) + ADAM_EPS) + ADAM_WD * w_ref[...])
        nm_ref[...] = nm
        nv_ref[...] = nv

    blk = pl.BlockSpec((tr, C), lambda i: (i, 0))
    out = jax.ShapeDtypeStruct((R, C), F32)
    res = pl.pallas_call(
        body, name=name, grid=(R // tr,),
        in_specs=[blk] * 4, out_specs=[blk] * 3, out_shape=[out] * 3,
        compiler_params=_params(),
    )(*flat)
    return [t.reshape(shape) for t in res]


SMALL_SHAPES = ((2, 1024), (2, 2048), (2, 4, 128, 128), (2, 512), (1024,))


def _pack_small(parts):
    return jnp.concatenate([p.reshape(-1, LANES) for p in parts], axis=0)


def _unpack_small(packed):
    out, row = [], 0
    for shape in SMALL_SHAPES:
        n = 1
        for d in shape:
            n *= d
        out.append(packed[row:row + n // LANES].reshape(shape))
        row += n // LANES
    return out


def kernel(x, norm_g, w_in, b_gate, pool_w, pool_scale, w_pool_up, w_attn_up, w_out, final_g, loss_target, m_norm_g, m_w_in, m_b_gate, m_pool_w, m_pool_scale, m_w_pool_up, m_w_attn_up, m_w_out, m_final_g, v_norm_g, v_w_in, v_b_gate, v_pool_w, v_pool_scale, v_w_pool_up, v_w_attn_up, v_w_out, v_final_g):
    _, _, c, s = _place()
    pos = jnp.stack([c, s]).astype(jnp.int32)
    names = ("w_in", "w_pool_up", "w_attn_up", "w_out")

    full = _gather_weights([_cast_into_place(w, pos, axis, width, f"cast_{n}")
                            for w, (axis, width), n in zip((w_in, w_pool_up, w_attn_up, w_out), SHARDED, names)])
    loss_part, dx, d_final_g, small, dws = _local_step(x[0], loss_target[0], norm_g, b_gate, pool_w, pool_scale,
                                                       final_g, *full)

    other = _sibling_exchange(dws)
    pair = [_pair_sum(d, o, pos, f"grad_pair_sum_{n}") for d, o, n in zip(dws, other, names)]
    landed = _chip_exchange([pb for _, pb in pair])
    mine = [_shard_sum(p, l, pos, axis, width, f"grad_shard_sum_{n}")
            for (p, _), l, (axis, width), n in zip(pair, landed, SHARDED, names)]
    g_in, g_pu, g_au, g_out = _sibling_share(mine)

    summed = _all_reduce_small(_pack_small(small + [d_final_g, jnp.broadcast_to(loss_part, (8, LANES))]))
    g_small = _unpack_small(summed)
    loss = summed[-8, 0]
    upd_small = _adamw(_pack_small([norm_g, b_gate, pool_w, pool_scale, final_g]), _pack_small(g_small),
                       _pack_small([m_norm_g, m_b_gate, m_pool_w, m_pool_scale, m_final_g]),
                       _pack_small([v_norm_g, v_b_gate, v_pool_w, v_pool_scale, v_final_g]), "adamw_small")
    d_small, nm_small, nv_small = [_unpack_small(t) for t in upd_small]
    upd_in = _adamw(w_in, g_in, m_w_in, v_w_in, "adamw_w_in")
    upd_pu = _adamw(w_pool_up, g_pu, m_w_pool_up, v_w_pool_up, "adamw_w_pool_up")
    upd_au = _adamw(w_attn_up, g_au, m_w_attn_up, v_w_attn_up, "adamw_w_attn_up")
    upd_out = _adamw(w_out, g_out, m_w_out, v_w_out, "adamw_w_out")

    def ordered(sm, k):
        big = (upd_in[k], upd_pu[k], upd_au[k], upd_out[k]) if k is not None else (g_in, g_pu, g_au, g_out)
        return [sm[0], big[0], sm[1], sm[2], sm[3], big[1], big[2], big[3], sm[4]]

    return (loss, dx[None], *ordered(g_small, None), *ordered(d_small, 0), *ordered(nm_small, 1),
            *ordered(nv_small, 2))
```

```python
import jax
import jax.numpy as jnp
import numpy as np
from jax import lax
from jax.experimental import pallas as pl
from jax.experimental.pallas import tpu as pltpu

F32 = jnp.float32
BF16 = jnp.bfloat16
MESH = pl.DeviceIdType.MESH

D_MODEL = 1024
POOL_WIDTH = 512
POOL_WINDOWS = (2, 4, 8, 16)
POOL_GROUP = 128
POOL_HALO = 16
ATTN_WIDTH = 512
HEAD_DIM = 64
HEAD_PAIRS = 4
IN_WIDTH = 5120
N_CHIPS = 4
N_DEVICES = 8
RMS_EPS = 1e-6
C_U, C_ZP, C_Q, C_K, C_V, C_ZA, C_GL = 0, 512, 1024, 1536, 2048, 2560, 3072

ADAM_LR, ADAM_B1, ADAM_B2, ADAM_EPS, ADAM_WD, ADAM_STEP = 0.001, 0.9, 0.999, 1e-08, 0.01, 10

LANES = 128
ATTN_BLOCK = 256
ROW_TILE = 256
PROJ_ROW_TILE = 512
VMEM_LIMIT = 56 * 1024 * 1024


def _params(**kw):
    return pltpu.CompilerParams(vmem_limit_bytes=VMEM_LIMIT, **kw)


def _nt(a, b):
    return lax.dot_general(a, b, (((1,), (1,)), ((), ())), preferred_element_type=F32)


def _tn(a, b):
    return lax.dot_general(a, b, (((0,), (0,)), ((), ())), preferred_element_type=F32)


def _nn(a, b):
    return jnp.dot(a, b, preferred_element_type=F32)


def _sigmoid(z):
    return 1.0 / (1.0 + jnp.exp(-z))


def _rms_inproj(x, g, w_in, layer):
    S = x.shape[0]
    tm = min(PROJ_ROW_TILE, S)

    def body(x_ref, g_ref, w_ref, u_ref, zp_ref, q_ref, k_ref, v_ref, za_ref, gl_ref, h_ref):
        xv = x_ref[...]
        r = lax.rsqrt(jnp.mean(xv * xv, axis=-1, keepdims=True) + RMS_EPS)
        h = ((xv * r) * g_ref[...]).astype(BF16)
        h_ref[...] = h

        def mm(c0, n):
            return _nn(h, w_ref[:, c0:c0 + n])

        u_ref[...] = mm(C_U, 512)
        zp_ref[...] = mm(C_ZP, 512)
        q_ref[...] = (mm(C_Q, 512) * 0.125).astype(BF16)
        k_ref[...] = mm(C_K, 512).astype(BF16)
        v_ref[...] = mm(C_V, 512).astype(BF16)
        za_ref[...] = mm(C_ZA, 512)
        for c in range(4):
            gl_ref[:, c * 512:(c + 1) * 512] = mm(C_GL + c * 512, 512)

    row = lambda n: pl.BlockSpec((tm, n), lambda i: (i, 0))
    sd = lambda n, dt: jax.ShapeDtypeStruct((S, n), dt)
    return pl.pallas_call(
        body, name=f"rms_inproj_l{layer}", grid=(S // tm,),
        in_specs=[row(D_MODEL), pl.BlockSpec((1, D_MODEL), lambda i: (0, 0)),
                  _layer_weight_spec(D_MODEL, IN_WIDTH, layer)],
        out_specs=[row(512), row(512), row(512), row(512), row(512), row(512), row(2048), row(D_MODEL)],
        out_shape=[sd(512, F32), sd(512, F32), sd(512, BF16), sd(512, BF16), sd(512, BF16), sd(512, F32),
                   sd(2048, F32), sd(D_MODEL, BF16)],
        compiler_params=_params(),
    )(x, g, w_in)


def _tri(n, strict_lower):
    r = lax.broadcasted_iota(jnp.int32, (n, n), 0)
    c = lax.broadcasted_iota(jnp.int32, (n, n), 1)
    return jnp.where(r > c if strict_lower else r < c, 1.0, 0.0).astype(BF16)


def _split_dot(x, m):
    hi = x.astype(BF16)
    lo = (x - hi.astype(F32)).astype(BF16)
    return _nn(hi, m) + _nn(lo, m)


def _log_terms(z):
    lg = jnp.log(1.0 + jnp.exp(-jnp.abs(z)))
    a = jnp.minimum(z, 0.0) - lg
    return a, a - z


EXHAUSTED = -104.0
UNREACHED = -1e30


class _HeadPair:
    def __init__(self, T):
        self.T = T
        self.first = lax.broadcasted_iota(jnp.int32, (T, LANES), 1) < HEAD_DIM
        self.lane = lax.broadcasted_iota(jnp.int32, (2 * T, LANES), 1)
        row = lax.broadcasted_iota(jnp.int32, (2 * T, T), 0)
        row = jnp.where(row >= T, row - T, row)
        self.causal = row > lax.broadcasted_iota(jnp.int32, (2 * T, T), 1)
        self.below = _tri(T, True)

    def stack(self, x2):
        return jnp.concatenate([jnp.where(self.first, x2, 0), jnp.where(self.first, 0, x2)], axis=0).astype(BF16)

    def unstack(self, x):
        return jnp.where(self.first, x[:self.T], x[self.T:])

    def keys(self, ref, blocks):
        T = self.T
        return jnp.concatenate([ref[pl.ds(pl.multiple_of(j * T, T), T), :] for j, _ in blocks], axis=0)

    def log_terms(self, z, blocks):
        T = self.T
        a_all, l_all = _log_terms(z)
        a = [a_all[:, b * T:(b + 1) * T] for b in range(len(blocks))]
        l1m = [l_all[:, b * T:(b + 1) * T] for b in range(len(blocks))]
        return a, [jnp.where(self.causal, l, 0.0) if diagonal else l for l, (_, diagonal) in zip(l1m, blocks)]

    def later_sums(self, l1m):
        later = _split_dot(jnp.concatenate(l1m, axis=0), self.below)
        return [later[2 * self.T * b:2 * self.T * (b + 1)] for b in range(len(l1m))]


def _halves(x):
    return x.reshape(2, x.shape[0] // 2, x.shape[1])


def _attn_fwd(q, k, v, layer):
    S = q.shape[0]
    T = min(ATTN_BLOCK, S)
    nq = S // T
    assert nq <= LANES and nq % 2 == 0
    half = nq // 2

    def body(q_ref, k_ref, v_ref, o_ref, c_ref):
        i = pl.program_id(1)
        pair = _HeadPair(T)
        qs = [pair.stack(q_ref[0]), pair.stack(q_ref[1])]
        diag = [i, i + half]

        def sweep(jobs):
            kv = [(pair.keys(k_ref, bl), pair.keys(v_ref, bl)) for _, bl, _ in jobs]
            zs = [_nt(qs[n], kcat) for (n, _, _), (kcat, _) in zip(jobs, kv)]
            terms = [pair.log_terms(z, bl) for (_, bl, _), z in zip(jobs, zs)]
            laters = [pair.later_sums(l1m) for _, l1m in terms]
            weights = []
            for (_, bl, (acc, run, saved)), (a, l1m), later in zip(jobs, terms, laters):
                ws = []
                for b, (j, diagonal) in enumerate(bl):
                    saved = jnp.where(pair.lane == j, run, saved)
                    w = jnp.exp(a[b] + later[b] + run)
                    ws.append(jnp.where(pair.causal, w, 0.0) if diagonal else w)
                    run = run + jnp.sum(l1m[b], axis=1, keepdims=True)
                weights.append((jnp.concatenate(ws, axis=1).astype(BF16), acc, run, saved))
            return [(acc + _nn(w, vcat), run, saved) for (w, acc, run, saved), (_, vcat) in zip(weights, kv)]

        def alive(carry):
            return (jnp.max(carry[1]) > EXHAUSTED).astype(jnp.int32)

        def older_blocks(n, carry):
            def older_block(state):
                j, _, c = state
                c = sweep([(n, [(j, False)], c)])[0]
                return j - 1, alive(c), c

            return lax.while_loop(lambda st: jnp.logical_and(st[0] >= 0, st[1] > 0), older_block,
                                  (diag[n] - 2, alive(carry), carry))[2]

        def run(first_blocks):
            init = (jnp.zeros((2 * T, LANES), F32), jnp.zeros((2 * T, 1), F32),
                    jnp.full((2 * T, LANES), UNREACHED, F32))
            carries = sweep([(n, first_blocks[n], init) for n in range(2)])
            for n in range(2):
                acc, _, saved = older_blocks(n, carries[n])
                o_ref[n] = pair.unstack(acc)
                c_ref[n, :, :LANES] = saved[:T]
                c_ref[n, :, LANES:] = saved[T:]

        with_previous = lambda d: [(d, True), (d - 1, False)]

        @pl.when(i == 0)
        def _():
            run([[(diag[0], True)], with_previous(diag[1])])

        @pl.when(i > 0)
        def _():
            run([with_previous(diag[0]), with_previous(diag[1])])

    blk = lambda n: pl.BlockSpec((2, T, n), lambda p, i: (0, i, p))
    full = pl.BlockSpec((S, LANES), lambda p, i: (0, p))
    o, carry = pl.pallas_call(
        body, name=f"attn_fwd_l{layer}", grid=(HEAD_PAIRS, half),
        in_specs=[blk(LANES), full, full],
        out_specs=[blk(LANES), blk(2 * LANES)],
        out_shape=[jax.ShapeDtypeStruct((2, S // 2, ATTN_WIDTH), F32),
                   jax.ShapeDtypeStruct((2, S // 2, 8 * LANES), F32)],
        compiler_params=_params(),
    )(_halves(q), k, v)
    return o.reshape(S, ATTN_WIDTH), carry.reshape(S, 8 * LANES)


def _attn_bwd(q, k, v, saved, do, layer):
    S = q.shape[0]
    T = min(ATTN_BLOCK, S)
    nq = S // T

    half = nq // 2

    def body(q_ref, k_ref, v_ref, c_ref, do_ref, dq_ref, dk_ref, dv_ref):
        i = pl.program_id(1)

        @pl.when(i == 0)
        def _():
            dk_ref[...] = jnp.zeros_like(dk_ref)
            dv_ref[...] = jnp.zeros_like(dv_ref)

        pair = _HeadPair(T)
        diag = [i, i + half]
        qs = [pair.stack(q_ref[n]) for n in range(2)]
        dos = [pair.stack(do_ref[n].astype(BF16)) for n in range(2)]
        saved = [jnp.concatenate([c_ref[n, :, :LANES], c_ref[n, :, LANES:]], axis=0) for n in range(2)]
        before = _tri(T, False)

        def sweep(jobs):
            kv = [(pair.keys(k_ref, bl), pair.keys(v_ref, bl)) for _, bl, _ in jobs]
            zs = [_nt(qs[n], kcat) for (n, _, _), (kcat, _) in zip(jobs, kv)]
            gs = [_nt(dos[n], vcat) for (n, _, _), (_, vcat) in zip(jobs, kv)]
            terms = [pair.log_terms(z, bl) for (_, bl, _), z in zip(jobs, zs)]
            laters = [pair.later_sums(l1m) for _, l1m in terms]
            ws, es = [], []
            for (n, bl, _), (a, _), later, g in zip(jobs, terms, laters, gs):
                w_job, e_job = [], []
                for b, (j, diagonal) in enumerate(bl):
                    run = jnp.sum(jnp.where(pair.lane == j, saved[n], 0.0), axis=1, keepdims=True)
                    w = jnp.exp(a[b] + later[b] + run)
                    w_job.append(jnp.where(pair.causal, w, 0.0) if diagonal else w)
                    e_job.append(w_job[b] * g[:, b * T:(b + 1) * T])
                ws.append(w_job)
                es.append(e_job)
            prefixes = [_nn(jnp.concatenate(e_job, axis=0).astype(BF16), before) for e_job in es]
            dzs, olders = [], []
            for (_, bl, (_, older)), (a, _), e_job, prefix in zip(jobs, terms, es, prefixes):
                dz_job = []
                for b, (j, diagonal) in enumerate(bl):
                    dz = e_job[b] - jnp.exp(a[b]) * (e_job[b] + (prefix[2 * T * b:2 * T * (b + 1)] + older))
                    dz_job.append(jnp.where(pair.causal, dz, 0.0) if diagonal else dz)
                    older = older + jnp.sum(e_job[b], axis=1, keepdims=True)
                dzs.append(jnp.concatenate(dz_job, axis=1).astype(BF16))
                olders.append(older)
            out = []
            for (n, bl, (dq, _)), dz, w_job, older, (kcat, _) in zip(jobs, dzs, ws, olders, kv):
                dk = _tn(dz, qs[n])
                dv = _tn(jnp.concatenate(w_job, axis=1).astype(BF16), dos[n])
                for b, (j, _) in enumerate(bl):
                    rows = pl.ds(pl.multiple_of(j * T, T), T)
                    dk_ref[rows, :] += dk[b * T:(b + 1) * T]
                    dv_ref[rows, :] += dv[b * T:(b + 1) * T]
                out.append((dq + _nn(dz, kcat), older))
            return out

        def older_blocks(n):
            col_max = jnp.max(saved[n], axis=0, keepdims=True)
            lane_row = lax.broadcasted_iota(jnp.int32, (1, LANES), 1)
            reached = jnp.sum(jnp.where(jnp.logical_and(col_max > EXHAUSTED, lane_row < diag[n]), 1, 0))
            init = (jnp.zeros((2 * T, LANES), F32), jnp.zeros((2 * T, 1), F32))
            return lax.fori_loop(diag[n] - reached, diag[n] - 1, lambda j, c: sweep([(n, [(j, False)], c)])[0], init)

        def run(last_blocks):
            carries = sweep([(n, last_blocks[n], older_blocks(n)) for n in range(2)])
            for n in range(2):
                dq_ref[n] = pair.unstack(carries[n][0]) * 0.125

        with_previous = lambda d: [(d - 1, False), (d, True)]

        @pl.when(i == 0)
        def _():
            run([[(diag[0], True)], with_previous(diag[1])])

        @pl.when(i > 0)
        def _():
            run([with_previous(diag[0]), with_previous(diag[1])])

    blk = lambda n: pl.BlockSpec((2, T, n), lambda p, i: (0, i, p))
    full = pl.BlockSpec((S, LANES), lambda p, i: (0, p))
    out = jax.ShapeDtypeStruct((S, ATTN_WIDTH), F32)
    dq, dk, dv = pl.pallas_call(
        body, name=f"attn_bwd_l{layer}", grid=(HEAD_PAIRS, half),
        in_specs=[blk(LANES), full, full, blk(2 * LANES), blk(LANES)],
        out_specs=[blk(LANES), full, full],
        out_shape=[jax.ShapeDtypeStruct((2, S // 2, ATTN_WIDTH), F32), out, out],
        compiler_params=_params(),
    )(_halves(q), k, v, _halves(saved), _halves(do))
    return dq.reshape(S, ATTN_WIDTH), dk, dv


def _pool_counts(row0, tm):
    pos = row0 + lax.broadcasted_iota(jnp.int32, (tm, 1), 0)
    return [1.0 / jnp.minimum(pos + 1, w).astype(F32) for w in POOL_WINDOWS]


def _window_bands(tm, backward):
    t = np.arange(tm)[:, None]
    c = np.arange(tm)[None, :]
    off = c - t if backward else t - c
    main = np.stack([(off >= 0) & (off < w) for w in POOL_WINDOWS])
    r = np.arange(POOL_HALO)[:, None]
    h = np.arange(POOL_HALO)[None, :]
    off = h - r + POOL_HALO if backward else r - h + POOL_HALO
    edge = np.concatenate([(off < w) for w in POOL_WINDOWS])
    return jnp.asarray(main, BF16), jnp.asarray(edge, BF16)


def _window_sums(tile, beside, main_ref, edge_ref, backward):
    tm = tile.shape[0]
    tb = tile.astype(BF16)
    edge = _nn(edge_ref[...], beside.astype(BF16))
    sums = []
    for g in range(len(POOL_WINDOWS)):
        cols = slice(g * POOL_GROUP, (g + 1) * POOL_GROUP)
        tot = _nn(main_ref[g], tb[:, cols])
        extra = edge[g * POOL_HALO:(g + 1) * POOL_HALO, cols]
        if backward:
            sums.append(jnp.concatenate([tot[:tm - POOL_HALO], tot[tm - POOL_HALO:] + extra], axis=0))
        else:
            sums.append(jnp.concatenate([tot[:POOL_HALO] + extra, tot[POOL_HALO:]], axis=0))
    return sums


def _post_forward(u, history, bands, inv_cnt, zp, o, za, gl, bg, pw_ref, scale, wpu_ref, wau_ref):
    pooled, mixed = [], []
    for g, tot in enumerate(_window_sums(u, history, *bands, False)):
        pg = (tot * inv_cnt[g] - u[:, g * POOL_GROUP:(g + 1) * POOL_GROUP]).astype(BF16)
        pooled.append(pg)
        mixed.append(_nn(pg, pw_ref[g].astype(BF16)))
    pooled = jnp.concatenate(pooled, axis=1)
    mixed = jnp.concatenate(mixed, axis=1)
    sp = _sigmoid(zp)
    sa = _sigmoid(za)
    y_pool = (mixed * scale) * (zp * sp)
    y_attn = o * (za * sa)
    gate = _sigmoid(gl + bg)
    g0, g1 = gate[:, :D_MODEL], gate[:, D_MODEL:]
    up_p = _nn(y_pool.astype(BF16), wpu_ref[...])
    up_a = _nn(y_attn.astype(BF16), wau_ref[...])
    merged = g0 * up_p + g1 * up_a
    return pooled, mixed, sp, sa, y_pool, y_attn, g0, g1, up_p, up_a, merged


def _row_specs(tm, rev, n_tiles):
    tile_of = (lambda i: n_tiles - 1 - i) if rev else (lambda i: i)
    row = lambda n: pl.BlockSpec((tm, n), lambda i: (tile_of(i), 0))
    halo = pl.BlockSpec((POOL_HALO, POOL_WIDTH),
                        lambda i: (jnp.maximum(tile_of(i) * (tm // POOL_HALO) - 1, 0), 0))
    const = lambda shape: pl.BlockSpec(shape, lambda i: (0,) * len(shape))
    return tile_of, row, halo, const


def _layer_weight_spec(rows, cols, layer):
    return pl.BlockSpec((None, rows, cols), lambda i: (layer, 0, 0), pipeline_mode=pl.Buffered(1))


def _post_fwd(x, u, zp, o, za, gl, bg, pw, scale, wpu, wau, wout, layer):
    S = x.shape[0]
    tm = min(ROW_TILE, S)
    n_tiles = S // tm
    tile_of, row, halo, const = _row_specs(tm, False, n_tiles)

    def body(x_ref, u_ref, uh_ref, main_ref, edge_ref, zp_ref, o_ref, za_ref, gl_ref, bg_ref, pw_ref, sc_ref, wpu_ref,
             wau_ref, wout_ref, out_ref):
        i = pl.program_id(0)
        vals = _post_forward(u_ref[...], jnp.where(i == 0, 0.0, uh_ref[...]), (main_ref, edge_ref),
                             _pool_counts(i * tm, tm), zp_ref[...], o_ref[...], za_ref[...], gl_ref[...], bg_ref[...],
                             pw_ref, sc_ref[...], wpu_ref, wau_ref)
        out_ref[...] = x_ref[...] + _nn(vals[-1].astype(BF16), wout_ref[...])

    return pl.pallas_call(
        body, name=f"post_fwd_l{layer}", grid=(n_tiles,),
        in_specs=[row(D_MODEL), row(512), halo, const((4, tm, tm)), const((4 * POOL_HALO, POOL_HALO)), row(512),
                  row(512), row(512), row(2048), const((1, 2048)), const((4, POOL_GROUP, POOL_GROUP)),
                  const((1, POOL_WIDTH)),
                  _layer_weight_spec(POOL_WIDTH, D_MODEL, layer), _layer_weight_spec(ATTN_WIDTH, D_MODEL, layer),
                  _layer_weight_spec(D_MODEL, D_MODEL, layer)],
        out_specs=row(D_MODEL),
        out_shape=jax.ShapeDtypeStruct((S, D_MODEL), F32),
        compiler_params=_params(),
    )(x, u, u, *_window_bands(tm, False), zp, o, za, gl, bg, pw, scale, wpu, wau, wout)


def _post_bwd(dx, u, zp, o, za, gl, bg, pw, scale, wpu, wau, wout, layer):
    S = dx.shape[0]
    tm = min(ROW_TILE, S)
    n_tiles = S // tm
    tile_of, row, halo, const = _row_specs(tm, True, n_tiles)

    def body(dx_ref, u_ref, uh_ref, main_ref, edge_ref, back_main_ref, back_edge_ref, zp_ref, o_ref, za_ref, gl_ref,
             bg_ref, pw_ref, sc_ref, wpu_ref, wau_ref, wout_ref,
             duz_ref, do_ref, dza_ref, dgl_ref, dsc_ref, dbg_ref,
             merged_ref, dup_ref, dua_ref, yp_ref, ya_ref, pooled_ref, dmixed_ref, nxt_ref):
        step = pl.program_id(0)
        i = tile_of(step)

        @pl.when(step == 0)
        def _():
            dsc_ref[...] = jnp.zeros_like(dsc_ref)
            dbg_ref[...] = jnp.zeros_like(dbg_ref)
            nxt_ref[...] = jnp.zeros_like(nxt_ref)

        inv_cnt = _pool_counts(i * tm, tm)
        zp, za, o = zp_ref[...], za_ref[...], o_ref[...]
        pooled, mixed, sp, sa, y_pool, y_attn, g0, g1, up_p, up_a, merged = _post_forward(
            u_ref[...], jnp.where(i == 0, 0.0, uh_ref[...]), (main_ref, edge_ref), inv_cnt, zp, o, za, gl_ref[...],
            bg_ref[...], pw_ref, sc_ref[...], wpu_ref, wau_ref)
        merged_ref[...] = merged.astype(BF16)
        yp_ref[...] = y_pool.astype(BF16)
        ya_ref[...] = y_attn.astype(BF16)
        pooled_ref[...] = pooled

        dmerged = _nt(dx_ref[...].astype(BF16), wout_ref[...])
        dup = (dmerged * g0).astype(BF16)
        dua = (dmerged * g1).astype(BF16)
        dup_ref[...] = dup
        dua_ref[...] = dua
        dgl0 = (dmerged * up_p) * (g0 * (1.0 - g0))
        dgl1 = (dmerged * up_a) * (g1 * (1.0 - g1))
        dgl_ref[:, :D_MODEL] = dgl0
        dgl_ref[:, D_MODEL:] = dgl1
        dbg_ref[:, :D_MODEL] += jnp.sum(dgl0, axis=0, keepdims=True)
        dbg_ref[:, D_MODEL:] += jnp.sum(dgl1, axis=0, keepdims=True)

        dy_attn = _nt(dua, wau_ref[...])
        do_ref[...] = dy_attn * (za * sa)
        dza_ref[...] = (dy_attn * o) * (sa * (1.0 + za * (1.0 - sa)))

        dy_pool = _nt(dup, wpu_ref[...])
        ms = mixed * sc_ref[...]
        dms = dy_pool * (zp * sp)
        duz_ref[:, POOL_WIDTH:] = (dy_pool * ms) * (sp * (1.0 + zp * (1.0 - sp)))
        dsc_ref[...] += jnp.sum(dms * mixed, axis=0, keepdims=True)
        dmixed = (dms * sc_ref[...]).astype(BF16)
        dmixed_ref[...] = dmixed
        dpooled = [_nt(dmixed[:, g * POOL_GROUP:(g + 1) * POOL_GROUP], pw_ref[g].astype(BF16)) for g in range(4)]
        scaled = jnp.concatenate([d * inv for d, inv in zip(dpooled, inv_cnt)], axis=1)
        for g, tot in enumerate(_window_sums(scaled, nxt_ref[...], back_main_ref, back_edge_ref, True)):
            duz_ref[:, g * POOL_GROUP:(g + 1) * POOL_GROUP] = tot - dpooled[g]
        nxt_ref[...] = scaled[:POOL_HALO]

    sd = lambda n, dt: jax.ShapeDtypeStruct((S, n), dt)
    bands = [const((4, tm, tm)), const((4 * POOL_HALO, POOL_HALO))]
    return pl.pallas_call(
        body, name=f"post_bwd_l{layer}", grid=(n_tiles,),
        in_specs=[row(D_MODEL), row(512), halo, *bands, *bands, row(512), row(512), row(512), row(2048),
                  const((1, 2048)), const((4, POOL_GROUP, POOL_GROUP)), const((1, POOL_WIDTH)),
                  _layer_weight_spec(POOL_WIDTH, D_MODEL, layer), _layer_weight_spec(ATTN_WIDTH, D_MODEL, layer),
                  _layer_weight_spec(D_MODEL, D_MODEL, layer)],
        out_specs=[row(1024), row(512), row(512), row(2048), const((1, POOL_WIDTH)), const((1, 2048)),
                   row(D_MODEL), row(D_MODEL), row(D_MODEL), row(512), row(512), row(512), row(512)],
        out_shape=[sd(1024, F32), sd(512, F32), sd(512, F32), sd(2048, F32),
                   jax.ShapeDtypeStruct((1, POOL_WIDTH), F32), jax.ShapeDtypeStruct((1, 2048), F32),
                   sd(D_MODEL, BF16), sd(D_MODEL, BF16), sd(D_MODEL, BF16), sd(512, BF16), sd(512, BF16),
                   sd(512, BF16), sd(512, BF16)],
        scratch_shapes=[pltpu.VMEM((POOL_HALO, POOL_WIDTH), F32)],
        compiler_params=_params(),
    )(dx, u, u, *_window_bands(tm, False), *_window_bands(tm, True), zp, o, za, gl, bg, pw, scale, wpu, wau, wout)


def _rms_backward(dh, xv, r, g):
    xhat = xv * r
    dxhat = dh * g
    return r * (dxhat - xhat * jnp.mean(dxhat * xhat, axis=-1, keepdims=True)), dh * xhat


def _loss_head(x, g, target):
    S = x.shape[0]
    tm = min(ROW_TILE, S)

    def body(x_ref, g_ref, t_ref, loss_ref, dx_ref, dg_ref):
        @pl.when(pl.program_id(0) == 0)
        def _():
            loss_ref[...] = jnp.zeros_like(loss_ref)
            dg_ref[...] = jnp.zeros_like(dg_ref)

        xv = x_ref[...]
        r = lax.rsqrt(jnp.mean(xv * xv, axis=-1, keepdims=True) + RMS_EPS)
        diff = (xv * r) * g_ref[...] - t_ref[...]
        per_row = jnp.mean(diff * diff, axis=-1, keepdims=True)
        loss_ref[...] += 0.5 * jnp.sum(per_row, axis=0, keepdims=True)
        dx, dg_rows = _rms_backward(diff * (1.0 / D_MODEL), xv, r, g_ref[...])
        dx_ref[...] = dx
        dg_ref[...] += jnp.sum(dg_rows, axis=0, keepdims=True)

    row = pl.BlockSpec((tm, D_MODEL), lambda i: (i, 0))
    vec = pl.BlockSpec((1, D_MODEL), lambda i: (0, 0))
    return pl.pallas_call(
        body, name="loss_head", grid=(S // tm,),
        in_specs=[row, vec, row],
        out_specs=[pl.BlockSpec((1, LANES), lambda i: (0, 0)), row, vec],
        out_shape=[jax.ShapeDtypeStruct((1, LANES), F32), jax.ShapeDtypeStruct((S, D_MODEL), F32),
                   jax.ShapeDtypeStruct((1, D_MODEL), F32)],
        compiler_params=_params(),
    )(x, g, target)


def _inproj_bwd(pieces, w_in, x, g, dx_res, layer):
    S = x.shape[0]
    tm = min(PROJ_ROW_TILE, S)
    cols = [(c0, p.shape[1]) for p, c0 in pieces]

    def body(*refs):
        piece_refs = refs[:len(cols)]
        w_ref, x_ref, g_ref, res_ref, dx_ref, dg_ref = refs[len(cols):]

        @pl.when(pl.program_id(0) == 0)
        def _():
            dg_ref[...] = jnp.zeros_like(dg_ref)

        dh = jnp.zeros((tm, D_MODEL), F32)
        for p_ref, (c0, n) in zip(piece_refs, cols):
            for c in range(0, n, 512):
                dh = dh + _nt(p_ref[:, c:c + 512].astype(BF16), w_ref[:, c0 + c:c0 + c + 512])
        xv = x_ref[...]
        r = lax.rsqrt(jnp.mean(xv * xv, axis=-1, keepdims=True) + RMS_EPS)
        dx, dg_rows = _rms_backward(dh, xv, r, g_ref[...])
        dx_ref[...] = res_ref[...] + dx
        dg_ref[...] += jnp.sum(dg_rows, axis=0, keepdims=True)

    row = lambda n: pl.BlockSpec((tm, n), lambda i: (i, 0))
    vec = pl.BlockSpec((1, D_MODEL), lambda i: (0, 0))
    return pl.pallas_call(
        body, name=f"inproj_bwd_l{layer}", grid=(S // tm,),
        in_specs=[row(n) for _, n in cols] + [_layer_weight_spec(D_MODEL, IN_WIDTH, layer), row(D_MODEL), vec,
                                              row(D_MODEL)],
        out_specs=[row(D_MODEL), vec],
        out_shape=[jax.ShapeDtypeStruct((S, D_MODEL), F32), jax.ShapeDtypeStruct((1, D_MODEL), F32)],
        compiler_params=_params(),
    )(*[p for p, _ in pieces], w_in, x, g, dx_res)


def _wgrad(a, b, name, layer, into=None, col0=0, n_total=None):
    S, M = a.shape
    N = b.shape[1]
    n_total = N if n_total is None else n_total
    tk = min(2048, S)
    tn = min(512, N)
    nk = S // tk

    def body(*refs):
        a_ref, b_ref, out_ref = refs[0], refs[1], refs[-1]
        prod = _tn(a_ref[...].astype(BF16), b_ref[...].astype(BF16))

        @pl.when(pl.program_id(1) == 0)
        def _():
            out_ref[...] = prod

        @pl.when(pl.program_id(1) > 0)
        def _():
            out_ref[...] += prod

    in_specs = [pl.BlockSpec((tk, M), lambda j, k: (k, 0)), pl.BlockSpec((tk, tn), lambda j, k: (k, j))]
    args = [a, b]
    aliases = {}
    if into is not None:
        in_specs.append(pl.BlockSpec(memory_space=pl.ANY))
        args.append(into)
        aliases = {2: 0}
    return pl.pallas_call(
        body, name=name, grid=(N // tn, nk),
        in_specs=in_specs,
        out_specs=pl.BlockSpec((None, M, tn), lambda j, k: (layer, 0, col0 // tn + j)),
        out_shape=jax.ShapeDtypeStruct((2, M, n_total), F32),
        input_output_aliases=aliases,
        compiler_params=_params(),
    )(*args)


def _pool_wgrad(pooled, dmixed, layer):
    S = pooled.shape[0]
    tk = min(1024, S)

    def body(a_ref, b_ref, out_ref):
        prod = _tn(a_ref[...], b_ref[...])

        @pl.when(pl.program_id(1) == 0)
        def _():
            out_ref[...] = prod

        @pl.when(pl.program_id(1) > 0)
        def _():
            out_ref[...] += prod

    blk = pl.BlockSpec((tk, POOL_GROUP), lambda g, k: (k, g))
    return pl.pallas_call(
        body, name=f"pool_wgrad_l{layer}", grid=(4, S // tk),
        in_specs=[blk, blk],
        out_specs=pl.BlockSpec((None, POOL_GROUP, POOL_GROUP), lambda g, k: (g, 0, 0)),
        out_shape=jax.ShapeDtypeStruct((4, POOL_GROUP, POOL_GROUP), F32),
        compiler_params=_params(),
    )(pooled, dmixed)


def _local_step(x, target, norm_g, b_gate, pool_w, pool_scale, final_g, w_in, w_pu, w_au, w_out):
    n_layers = norm_g.shape[0]
    saved = []
    for l in range(n_layers):
        g = norm_g[l][None]
        bg = b_gate[l][None]
        sc = pool_scale[l][None]
        u, zp, q, k, v, za, gl, h = _rms_inproj(x, g, w_in, l)
        o, carry = _attn_fwd(q, k, v, l)
        saved.append((x, g, bg, sc, u, zp, q, k, v, za, gl, h, o, carry))
        x = _post_fwd(x, u, zp, o, za, gl, bg, pool_w[l], sc, w_pu, w_au, w_out, l)
    loss, dx, d_final_g = _loss_head(x, final_g[None], target)

    small = [None] * n_layers
    dw_in = dw_out = dw_pu = dw_au = None
    for l in reversed(range(n_layers)):
        x_in, g, bg, sc, u, zp, q, k, v, za, gl, h, o, carry = saved[l]
        (duz, do, dza, dgl, dsc, dbg, merged, dup, dua, y_pool, y_attn, pooled, dmixed) = _post_bwd(
            dx, u, zp, o, za, gl, bg, pool_w[l], sc, w_pu, w_au, w_out, l)
        dq, dk, dv = _attn_bwd(q, k, v, carry, do, l)
        pieces = [(duz, C_U), (dq, C_Q), (dk, C_K), (dv, C_V), (dza, C_ZA), (dgl, C_GL)]
        for p, c0 in pieces:
            dw_in = _wgrad(h, p, f"wgrad_in_l{l}_c{c0}", l, into=dw_in, col0=c0, n_total=IN_WIDTH)
        dw_out = _wgrad(merged, dx, f"wgrad_out_l{l}", l, into=dw_out)
        dw_pu = _wgrad(y_pool, dup, f"wgrad_pu_l{l}", l, into=dw_pu)
        dw_au = _wgrad(y_attn, dua, f"wgrad_au_l{l}", l, into=dw_au)
        dpw = _pool_wgrad(pooled, dmixed, l)
        dx, dg = _inproj_bwd(pieces, w_in, x_in, g, dx, l)
        small[l] = (dg[0], dbg[0], dpw, dsc[0])
    small = [jnp.stack([small[l][i] for l in range(n_layers)]) for i in range(4)]
    return loss[0, 0], dx, d_final_g[0], small, (dw_in, dw_pu, dw_au, dw_out)


SHARDED = ((2, 1280), (2, 256), (2, 256), (1, 256))
ANY = pl.BlockSpec(memory_space=pl.ANY)


def _part(ref, s, axis, width):
    sl = pl.ds(pl.multiple_of(s * width, width), width)
    return ref.at[:, sl] if axis == 2 else ref.at[sl, :]


def _place():
    x, y, c = lax.axis_index("x"), lax.axis_index("y"), lax.axis_index("c")
    return x, y, c, 2 * x + y


def _other_chip(x, y, m):
    px = 1 - x if m & 2 else x
    py = 1 - y if m & 1 else y
    return px, py, 2 * px + py


def _remote(src, dst, send, recv, k, to):
    return pltpu.make_async_remote_copy(src_ref=src, dst_ref=dst, send_sem=send.at[k], recv_sem=recv.at[k],
                                        device_id=to, device_id_type=MESH)


def _part_spec(tr, rows_s, cols_s, axis, width, lead):
    if axis == 2:
        return pl.BlockSpec((None, tr, width), lambda *a: (lead(a), a[-2], a[-1][1]))
    return pl.BlockSpec((None, tr, cols_s), lambda *a: (lead(a), a[-1][1] * (rows_s // tr) + a[-2], 0))


def _cast_into_place(w, pos, axis, width, name):
    L, Rs, Cs = w.shape
    tr = min(256, Rs)
    shape = [L, Rs, Cs]
    shape[axis] *= N_CHIPS

    def body(pos_ref, w_ref, o_ref):
        o_ref[...] = w_ref[...].astype(BF16)

    return pl.pallas_call(
        body, name=name,
        grid_spec=pltpu.PrefetchScalarGridSpec(
            num_scalar_prefetch=1, grid=(L, Rs // tr),
            in_specs=[pl.BlockSpec((None, tr, Cs), lambda l, i, pos: (l, i, 0))],
            out_specs=_part_spec(tr, Rs, Cs, axis, width, lambda a: a[0])),
        out_shape=jax.ShapeDtypeStruct(tuple(shape), BF16),
        compiler_params=_params(),
    )(pos, w)


def _gather_weights(fulls):
    n = len(SHARDED)

    def body(*refs):
        outs = refs[n:2 * n]
        send, recv = refs[2 * n:]
        x, y, c, s = _place()
        me, sibling = (x, y, c), (x, y, 1 - c)

        def copy(k, a, layer, shard, to):
            part = _part(outs[a].at[layer], shard, *SHARDED[a])
            return _remote(part, part, send, recv, k, to)

        sent = []
        for a in range(n):
            for m in (1, 2, 3):
                px, py, _ = _other_chip(x, y, m)
                sent.append(copy(3 * a + m - 1, a, c, s, (px, py, c)))
                sent[-1].start()
        for m in (1, 2, 3):
            _, _, sp = _other_chip(x, y, m)
            for a in range(n):
                copy(3 * a + m - 1, a, c, sp, me).wait_recv()
                sent.append(copy(3 * n + 3 * a + m - 1, a, c, sp, sibling))
                sent[-1].start()
        for m in (1, 2, 3):
            _, _, sp = _other_chip(x, y, m)
            for a in range(n):
                copy(3 * n + 3 * a + m - 1, a, 1 - c, sp, me).wait_recv()
        for cp in sent:
            cp.wait_send()

    return pl.pallas_call(
        body, name="gather_weights",
        in_specs=[ANY] * n, out_specs=[ANY] * n,
        out_shape=[jax.ShapeDtypeStruct(f.shape, f.dtype) for f in fulls],
        input_output_aliases={a: a for a in range(n)},
        scratch_shapes=[pltpu.SemaphoreType.DMA((6 * n,)), pltpu.SemaphoreType.DMA((6 * n,))],
    )(*fulls)


def _sibling_exchange(dws):
    n = len(dws)

    def body(*refs):
        ins, outs = refs[:n], refs[n:2 * n]
        send, recv = refs[2 * n:]
        x, y, c, _ = _place()
        copies = [_remote(ins[a].at[1 - c], outs[a], send, recv, a, (x, y, 1 - c)) for a in range(n)]
        for cp in copies:
            cp.start()
        for cp in copies:
            cp.wait()

    return pl.pallas_call(
        body, name="grad_sibling_exchange",
        in_specs=[ANY] * n, out_specs=[ANY] * n,
        out_shape=[jax.ShapeDtypeStruct(d.shape[1:], d.dtype) for d in dws],
        scratch_shapes=[pltpu.SemaphoreType.DMA((n,)), pltpu.SemaphoreType.DMA((n,))],
    )(*dws)


def _pair_sum(dw, other, pos, name):
    _, R, C = dw.shape
    tr = 128 if C > 1024 else 256

    def body(pos_ref, a_ref, b_ref, o_ref, ob_ref):
        tot = a_ref[...] + b_ref[...]
        o_ref[...] = tot
        ob_ref[...] = tot.astype(BF16)

    blk = pl.BlockSpec((tr, C), lambda i, pos: (i, 0))
    return pl.pallas_call(
        body, name=name,
        grid_spec=pltpu.PrefetchScalarGridSpec(
            num_scalar_prefetch=1, grid=(R // tr,),
            in_specs=[pl.BlockSpec((None, tr, C), lambda i, pos: (pos[0], i, 0)), blk],
            out_specs=[blk, blk]),
        out_shape=[jax.ShapeDtypeStruct((R, C), F32), jax.ShapeDtypeStruct((R, C), BF16)],
        compiler_params=_params(),
    )(pos, dw, other)


def _chip_exchange(ps, packed):
    n = len(ps)

    def body(*refs):
        ins, small_ref = refs[:n], refs[n]
        outs, total_ref = refs[n + 1:2 * n + 1], refs[2 * n + 1]
        all_ref, send, recv, small_send, small_recv = refs[2 * n + 2:]
        x, y, c, _ = _place()
        my_id = 4 * x + 2 * y + c
        all_ref[my_id] = small_ref[...]
        small = []
        for m in range(1, N_DEVICES):
            px = 1 - x if m & 4 else x
            py = 1 - y if m & 2 else y
            pc = 1 - c if m & 1 else c
            cp = _remote(small_ref, all_ref.at[my_id], small_send, small_recv, m - 1, (px, py, pc))
            cp.start()
            small.append((cp, 4 * px + 2 * py + pc))
        copies = []
        for a, (axis, width) in enumerate(SHARDED):
            for m in (1, 2, 3):
                px, py, sp = _other_chip(x, y, m)
                copies.append(_remote(_part(ins[a], sp, axis, width), outs[a].at[m - 1], send, recv, 3 * a + m - 1,
                                      (px, py, c)))
        for cp in copies:
            cp.start()
        for m, (cp, peer_id) in enumerate(small):
            _remote(small_ref, all_ref.at[peer_id], small_send, small_recv, m, (x, y, c)).wait_recv()
        total = all_ref[0]
        for d in range(1, N_DEVICES):
            total = total + all_ref[d]
        total_ref[...] = total
        for cp, _ in small:
            cp.wait_send()
        for cp in copies:
            cp.wait()

    def landing(p, axis, width):
        shape = [3] + list(p.shape)
        shape[axis] = width
        return jax.ShapeDtypeStruct(tuple(shape), p.dtype)

    vmem = pl.BlockSpec(memory_space=pltpu.VMEM)
    res = pl.pallas_call(
        body, name="grad_chip_exchange",
        in_specs=[ANY] * n + [vmem], out_specs=[ANY] * n + [vmem],
        out_shape=[landing(p, axis, width) for p, (axis, width) in zip(ps, SHARDED)]
        + [jax.ShapeDtypeStruct(packed.shape, packed.dtype)],
        scratch_shapes=[pltpu.VMEM((N_DEVICES,) + packed.shape, F32),
                        pltpu.SemaphoreType.DMA((3 * n,)), pltpu.SemaphoreType.DMA((3 * n,)),
                        pltpu.SemaphoreType.DMA((N_DEVICES - 1,)), pltpu.SemaphoreType.DMA((N_DEVICES - 1,))],
        compiler_params=_params(),
    )(*ps, packed)
    return res[:n], res[n]


def _shard_sum(p, landed, pos, axis, width, name):
    _, Rs, Cs = landed.shape
    tr = min(256, Rs)
    p_spec = _part_spec(tr, Rs, Cs, axis, width, lambda a: 0)

    def body(pos_ref, p_ref, l_ref, o_ref):
        o_ref[...] = ((p_ref[...] + l_ref[0].astype(F32)) + l_ref[1].astype(F32)) + l_ref[2].astype(F32)

    return pl.pallas_call(
        body, name=name,
        grid_spec=pltpu.PrefetchScalarGridSpec(
            num_scalar_prefetch=1, grid=(Rs // tr,),
            in_specs=[p_spec, pl.BlockSpec((3, tr, Cs), lambda i, pos: (0, i, 0))],
            out_specs=pl.BlockSpec((None, tr, Cs), lambda i, pos: (pos[0], i, 0))),
        out_shape=jax.ShapeDtypeStruct((2, Rs, Cs), F32),
        compiler_params=_params(),
    )(pos, p[None], landed)


def _sibling_share(gs):
    n = len(gs)

    def body(*refs):
        outs = refs[n:2 * n]
        send, recv = refs[2 * n:]
        x, y, c, _ = _place()
        copies = [_remote(outs[a].at[c], outs[a].at[c], send, recv, a, (x, y, 1 - c)) for a in range(n)]
        for cp in copies:
            cp.start()
        for a, cp in enumerate(copies):
            cp.wait_send()
            _remote(outs[a].at[1 - c], outs[a].at[1 - c], send, recv, a, (x, y, c)).wait_recv()

    return pl.pallas_call(
        body, name="grad_sibling_share",
        in_specs=[ANY] * n, out_specs=[ANY] * n,
        out_shape=[jax.ShapeDtypeStruct(g.shape, g.dtype) for g in gs],
        input_output_aliases={a: a for a in range(n)},
        scratch_shapes=[pltpu.SemaphoreType.DMA((n,)), pltpu.SemaphoreType.DMA((n,))],
    )(*gs)


def _adamw(w, g, m, v, name):
    shape = w.shape
    C = shape[-1]
    flat = [t.reshape(-1, C) for t in (w, g, m, v)]
    R = flat[0].shape[0]
    tr = max(t for t in range(8, R + 1, 8) if R % t == 0 and t * C <= 384 * 1024)

    def body(w_ref, g_ref, m_ref, v_ref, d_ref, nm_ref, nv_ref):
        gv = g_ref[...]
        nm = ADAM_B1 * m_ref[...] + (1.0 - ADAM_B1) * gv
        nv = ADAM_B2 * v_ref[...] + (1.0 - ADAM_B2) * (gv * gv)
        m_hat = nm / (1.0 - ADAM_B1 ** ADAM_STEP)
        v_hat = nv / (1.0 - ADAM_B2 ** ADAM_STEP)
        d_ref[...] = -ADAM_LR * (m_hat / (jnp.sqrt(v_hat) + ADAM_EPS) + ADAM_WD * w_ref[...])
        nm_ref[...] = nm
        nv_ref[...] = nv

    blk = pl.BlockSpec((tr, C), lambda i: (i, 0))
    out = jax.ShapeDtypeStruct((R, C), F32)
    res = pl.pallas_call(
        body, name=name, grid=(R // tr,),
        in_specs=[blk] * 4, out_specs=[blk] * 3, out_shape=[out] * 3,
        compiler_params=_params(),
    )(*flat)
    return [t.reshape(shape) for t in res]


SMALL_SHAPES = ((2, 1024), (2, 2048), (2, 4, 128, 128), (2, 512), (1024,))


def _pack_small(parts):
    return jnp.concatenate([p.reshape(-1, LANES) for p in parts], axis=0)


def _unpack_small(packed):
    out, row = [], 0
    for shape in SMALL_SHAPES:
        n = 1
        for d in shape:
            n *= d
        out.append(packed[row:row + n // LANES].reshape(shape))
        row += n // LANES
    return out


def kernel(x, norm_g, w_in, b_gate, pool_w, pool_scale, w_pool_up, w_attn_up, w_out, final_g, loss_target, m_norm_g, m_w_in, m_b_gate, m_pool_w, m_pool_scale, m_w_pool_up, m_w_attn_up, m_w_out, m_final_g, v_norm_g, v_w_in, v_b_gate, v_pool_w, v_pool_scale, v_w_pool_up, v_w_attn_up, v_w_out, v_final_g):
    _, _, c, s = _place()
    pos = jnp.stack([c, s]).astype(jnp.int32)
    names = ("w_in", "w_pool_up", "w_attn_up", "w_out")

    full = _gather_weights([_cast_into_place(w, pos, axis, width, f"cast_{n}")
                            for w, (axis, width), n in zip((w_in, w_pool_up, w_attn_up, w_out), SHARDED, names)])
    loss_part, dx, d_final_g, small, dws = _local_step(x[0], loss_target[0], norm_g, b_gate, pool_w, pool_scale,
                                                       final_g, *full)

    other = _sibling_exchange(dws)
    pair = [_pair_sum(d, o, pos, f"grad_pair_sum_{n}") for d, o, n in zip(dws, other, names)]
    landed, summed = _chip_exchange([pb for _, pb in pair],
                                    _pack_small(small + [d_final_g, jnp.broadcast_to(loss_part, (8, LANES))]))
    mine = [_shard_sum(p, l, pos, axis, width, f"grad_shard_sum_{n}")
            for (p, _), l, (axis, width), n in zip(pair, landed, SHARDED, names)]
    g_in, g_pu, g_au, g_out = _sibling_share(mine)
    g_small = _unpack_small(summed)
    loss = summed[-8, 0]
    upd_small = _adamw(_pack_small([norm_g, b_gate, pool_w, pool_scale, final_g]), _pack_small(g_small),
                       _pack_small([m_norm_g, m_b_gate, m_pool_w, m_pool_scale, m_final_g]),
                       _pack_small([v_norm_g, v_b_gate, v_pool_w, v_pool_scale, v_final_g]), "adamw_small")
    d_small, nm_small, nv_small = [_unpack_small(t) for t in upd_small]
    upd_in = _adamw(w_in, g_in, m_w_in, v_w_in, "adamw_w_in")
    upd_pu = _adamw(w_pool_up, g_pu, m_w_pool_up, v_w_pool_up, "adamw_w_pool_up")
    upd_au = _adamw(w_attn_up, g_au, m_w_attn_up, v_w_attn_up, "adamw_w_attn_up")
    upd_out = _adamw(w_out, g_out, m_w_out, v_w_out, "adamw_w_out")

    def ordered(sm, k):
        big = (upd_in[k], upd_pu[k], upd_au[k], upd_out[k]) if k is not None else (g_in, g_pu, g_au, g_out)
        return [sm[0], big[0], sm[1], sm[2], sm[3], big[1], big[2], big[3], sm[4]]

    return (loss, dx[None], *ordered(g_small, None), *ordered(d_small, 0), *ordered(nm_small, 1),
            *ordered(nv_small, 2))
```

```python
import jax
import jax.numpy as jnp
import numpy as np
from jax import lax
from jax.experimental import pallas as pl
from jax.experimental.pallas import tpu as pltpu

F32 = jnp.float32
BF16 = jnp.bfloat16
MESH = pl.DeviceIdType.MESH

D_MODEL = 1024
POOL_WIDTH = 512
POOL_WINDOWS = (2, 4, 8, 16)
POOL_GROUP = 128
POOL_HALO = 16
ATTN_WIDTH = 512
HEAD_DIM = 64
HEAD_PAIRS = 4
IN_WIDTH = 5120
N_CHIPS = 4
N_DEVICES = 8
RMS_EPS = 1e-6
C_U, C_ZP, C_Q, C_K, C_V, C_ZA, C_GL = 0, 512, 1024, 1536, 2048, 2560, 3072

ADAM_LR, ADAM_B1, ADAM_B2, ADAM_EPS, ADAM_WD, ADAM_STEP = 0.001, 0.9, 0.999, 1e-08, 0.01, 10

LANES = 128
ATTN_BLOCK = 256
ROW_TILE = 256
PROJ_ROW_TILE = 512
VMEM_LIMIT = 56 * 1024 * 1024


def _params(**kw):
    return pltpu.CompilerParams(vmem_limit_bytes=VMEM_LIMIT, **kw)


def _nt(a, b):
    return lax.dot_general(a, b, (((1,), (1,)), ((), ())), preferred_element_type=F32)


def _tn(a, b):
    return lax.dot_general(a, b, (((0,), (0,)), ((), ())), preferred_element_type=F32)


def _nn(a, b):
    return jnp.dot(a, b, preferred_element_type=F32)


def _sigmoid(z):
    return 1.0 / (1.0 + jnp.exp(-z))


def _rms_inproj(x, g, w_in, layer):
    S = x.shape[0]
    tm = min(PROJ_ROW_TILE, S)

    def body(x_ref, g_ref, w_ref, u_ref, zp_ref, q_ref, k_ref, v_ref, za_ref, gl_ref, h_ref):
        xv = x_ref[...]
        r = lax.rsqrt(jnp.mean(xv * xv, axis=-1, keepdims=True) + RMS_EPS)
        h = ((xv * r) * g_ref[...]).astype(BF16)
        h_ref[...] = h

        def mm(c0, n):
            return _nn(h, w_ref[:, c0:c0 + n])

        u_ref[...] = mm(C_U, 512)
        zp_ref[...] = mm(C_ZP, 512)
        q_ref[...] = (mm(C_Q, 512) * 0.125).astype(BF16)
        k_ref[...] = mm(C_K, 512).astype(BF16)
        v_ref[...] = mm(C_V, 512).astype(BF16)
        za_ref[...] = mm(C_ZA, 512)
        for c in range(4):
            gl_ref[:, c * 512:(c + 1) * 512] = mm(C_GL + c * 512, 512).astype(BF16)

    row = lambda n: pl.BlockSpec((tm, n), lambda i: (i, 0))
    sd = lambda n, dt: jax.ShapeDtypeStruct((S, n), dt)
    return pl.pallas_call(
        body, name=f"rms_inproj_l{layer}", grid=(S // tm,),
        in_specs=[row(D_MODEL), pl.BlockSpec((1, D_MODEL), lambda i: (0, 0)),
                  _layer_weight_spec(D_MODEL, IN_WIDTH, layer)],
        out_specs=[row(512), row(512), row(512), row(512), row(512), row(512), row(2048), row(D_MODEL)],
        out_shape=[sd(512, F32), sd(512, F32), sd(512, BF16), sd(512, BF16), sd(512, BF16), sd(512, F32),
                   sd(2048, BF16), sd(D_MODEL, BF16)],
        compiler_params=_params(),
    )(x, g, w_in)


def _tri(n, strict_lower):
    r = lax.broadcasted_iota(jnp.int32, (n, n), 0)
    c = lax.broadcasted_iota(jnp.int32, (n, n), 1)
    return jnp.where(r > c if strict_lower else r < c, 1.0, 0.0).astype(BF16)


def _split_dot(x, m):
    hi = x.astype(BF16)
    lo = (x - hi.astype(F32)).astype(BF16)
    return _nn(hi, m) + _nn(lo, m)


def _log_terms(z):
    lg = jnp.log(1.0 + jnp.exp(-jnp.abs(z)))
    a = jnp.minimum(z, 0.0) - lg
    return a, a - z


EXHAUSTED = -104.0
UNREACHED = -1e30


class _HeadPair:
    def __init__(self, T):
        self.T = T
        self.first = lax.broadcasted_iota(jnp.int32, (T, LANES), 1) < HEAD_DIM
        self.lane = lax.broadcasted_iota(jnp.int32, (2 * T, LANES), 1)
        row = lax.broadcasted_iota(jnp.int32, (2 * T, T), 0)
        row = jnp.where(row >= T, row - T, row)
        self.causal = row > lax.broadcasted_iota(jnp.int32, (2 * T, T), 1)
        self.below = _tri(T, True)

    def stack(self, x2):
        return jnp.concatenate([jnp.where(self.first, x2, 0), jnp.where(self.first, 0, x2)], axis=0).astype(BF16)

    def unstack(self, x):
        return jnp.where(self.first, x[:self.T], x[self.T:])

    def keys(self, ref, blocks):
        T = self.T
        return jnp.concatenate([ref[pl.ds(pl.multiple_of(j * T, T), T), :] for j, _ in blocks], axis=0)

    def log_terms(self, z, blocks):
        T = self.T
        a_all, l_all = _log_terms(z)
        a = [a_all[:, b * T:(b + 1) * T] for b in range(len(blocks))]
        l1m = [l_all[:, b * T:(b + 1) * T] for b in range(len(blocks))]
        return a, [jnp.where(self.causal, l, 0.0) if diagonal else l for l, (_, diagonal) in zip(l1m, blocks)]

    def later_sums(self, l1m):
        later = _split_dot(jnp.concatenate(l1m, axis=0), self.below)
        return [later[2 * self.T * b:2 * self.T * (b + 1)] for b in range(len(l1m))]


def _halves(x):
    return x.reshape(2, x.shape[0] // 2, x.shape[1])


def _attn_fwd(q, k, v, layer):
    S = q.shape[0]
    T = min(ATTN_BLOCK, S)
    nq = S // T
    assert nq <= LANES and nq % 2 == 0
    half = nq // 2

    def body(q_ref, k_ref, v_ref, o_ref, c_ref):
        i = pl.program_id(1)
        pair = _HeadPair(T)
        qs = [pair.stack(q_ref[0]), pair.stack(q_ref[1])]
        diag = [i, i + half]

        def sweep(jobs):
            kv = [(pair.keys(k_ref, bl), pair.keys(v_ref, bl)) for _, bl, _ in jobs]
            zs = [_nt(qs[n], kcat) for (n, _, _), (kcat, _) in zip(jobs, kv)]
            terms = [pair.log_terms(z, bl) for (_, bl, _), z in zip(jobs, zs)]
            laters = [pair.later_sums(l1m) for _, l1m in terms]
            weights = []
            for (_, bl, (acc, run, saved)), (a, l1m), later in zip(jobs, terms, laters):
                ws = []
                for b, (j, diagonal) in enumerate(bl):
                    saved = jnp.where(pair.lane == j, run, saved)
                    w = jnp.exp(a[b] + later[b] + run)
                    ws.append(jnp.where(pair.causal, w, 0.0) if diagonal else w)
                    run = run + jnp.sum(l1m[b], axis=1, keepdims=True)
                weights.append((jnp.concatenate(ws, axis=1).astype(BF16), acc, run, saved))
            return [(acc + _nn(w, vcat), run, saved) for (w, acc, run, saved), (_, vcat) in zip(weights, kv)]

        def alive(carry):
            return (jnp.max(carry[1]) > EXHAUSTED).astype(jnp.int32)

        def older_blocks(n, carry):
            def older_block(state):
                j, _, c = state
                c = sweep([(n, [(j, False)], c)])[0]
                return j - 1, alive(c), c

            return lax.while_loop(lambda st: jnp.logical_and(st[0] >= 0, st[1] > 0), older_block,
                                  (diag[n] - 2, alive(carry), carry))[2]

        def run(first_blocks):
            init = (jnp.zeros((2 * T, LANES), F32), jnp.zeros((2 * T, 1), F32),
                    jnp.full((2 * T, LANES), UNREACHED, F32))
            carries = sweep([(n, first_blocks[n], init) for n in range(2)])
            for n in range(2):
                acc, _, saved = older_blocks(n, carries[n])
                o_ref[n] = pair.unstack(acc)
                c_ref[n, :, :LANES] = saved[:T]
                c_ref[n, :, LANES:] = saved[T:]

        with_previous = lambda d: [(d, True), (d - 1, False)]

        @pl.when(i == 0)
        def _():
            run([[(diag[0], True)], with_previous(diag[1])])

        @pl.when(i > 0)
        def _():
            run([with_previous(diag[0]), with_previous(diag[1])])

    blk = lambda n: pl.BlockSpec((2, T, n), lambda p, i: (0, i, p))
    full = pl.BlockSpec((S, LANES), lambda p, i: (0, p))
    o, carry = pl.pallas_call(
        body, name=f"attn_fwd_l{layer}", grid=(HEAD_PAIRS, half),
        in_specs=[blk(LANES), full, full],
        out_specs=[blk(LANES), blk(2 * LANES)],
        out_shape=[jax.ShapeDtypeStruct((2, S // 2, ATTN_WIDTH), F32),
                   jax.ShapeDtypeStruct((2, S // 2, 8 * LANES), F32)],
        compiler_params=_params(),
    )(_halves(q), k, v)
    return o.reshape(S, ATTN_WIDTH), carry.reshape(S, 8 * LANES)


def _attn_bwd(q, k, v, saved, do, layer):
    S = q.shape[0]
    T = min(ATTN_BLOCK, S)
    nq = S // T

    half = nq // 2

    def body(q_ref, k_ref, v_ref, c_ref, do_ref, dq_ref, dk_ref, dv_ref):
        i = pl.program_id(1)

        @pl.when(i == 0)
        def _():
            dk_ref[...] = jnp.zeros_like(dk_ref)
            dv_ref[...] = jnp.zeros_like(dv_ref)

        pair = _HeadPair(T)
        diag = [i, i + half]
        qs = [pair.stack(q_ref[n]) for n in range(2)]
        dos = [pair.stack(do_ref[n].astype(BF16)) for n in range(2)]
        saved = [jnp.concatenate([c_ref[n, :, :LANES], c_ref[n, :, LANES:]], axis=0) for n in range(2)]
        before = _tri(T, False)

        def sweep(jobs):
            kv = [(pair.keys(k_ref, bl), pair.keys(v_ref, bl)) for _, bl, _ in jobs]
            zs = [_nt(qs[n], kcat) for (n, _, _), (kcat, _) in zip(jobs, kv)]
            gs = [_nt(dos[n], vcat) for (n, _, _), (_, vcat) in zip(jobs, kv)]
            terms = [pair.log_terms(z, bl) for (_, bl, _), z in zip(jobs, zs)]
            laters = [pair.later_sums(l1m) for _, l1m in terms]
            ws, es = [], []
            for (n, bl, _), (a, _), later, g in zip(jobs, terms, laters, gs):
                w_job, e_job = [], []
                for b, (j, diagonal) in enumerate(bl):
                    run = jnp.sum(jnp.where(pair.lane == j, saved[n], 0.0), axis=1, keepdims=True)
                    w = jnp.exp(a[b] + later[b] + run)
                    w_job.append(jnp.where(pair.causal, w, 0.0) if diagonal else w)
                    e_job.append(w_job[b] * g[:, b * T:(b + 1) * T])
                ws.append(w_job)
                es.append(e_job)
            prefixes = [_nn(jnp.concatenate(e_job, axis=0).astype(BF16), before) for e_job in es]
            dzs, olders = [], []
            for (_, bl, (_, older)), (a, _), e_job, prefix in zip(jobs, terms, es, prefixes):
                dz_job = []
                for b, (j, diagonal) in enumerate(bl):
                    dz = e_job[b] - jnp.exp(a[b]) * (e_job[b] + (prefix[2 * T * b:2 * T * (b + 1)] + older))
                    dz_job.append(jnp.where(pair.causal, dz, 0.0) if diagonal else dz)
                    older = older + jnp.sum(e_job[b], axis=1, keepdims=True)
                dzs.append(jnp.concatenate(dz_job, axis=1).astype(BF16))
                olders.append(older)
            out = []
            for (n, bl, (dq, _)), dz, w_job, older, (kcat, _) in zip(jobs, dzs, ws, olders, kv):
                dk = _tn(dz, qs[n])
                dv = _tn(jnp.concatenate(w_job, axis=1).astype(BF16), dos[n])
                for b, (j, _) in enumerate(bl):
                    rows = pl.ds(pl.multiple_of(j * T, T), T)
                    dk_ref[rows, :] += dk[b * T:(b + 1) * T]
                    dv_ref[rows, :] += dv[b * T:(b + 1) * T]
                out.append((dq + _nn(dz, kcat), older))
            return out

        def older_blocks(n):
            col_max = jnp.max(saved[n], axis=0, keepdims=True)
            lane_row = lax.broadcasted_iota(jnp.int32, (1, LANES), 1)
            reached = jnp.sum(jnp.where(jnp.logical_and(col_max > EXHAUSTED, lane_row < diag[n]), 1, 0))
            init = (jnp.zeros((2 * T, LANES), F32), jnp.zeros((2 * T, 1), F32))
            return lax.fori_loop(diag[n] - reached, diag[n] - 1, lambda j, c: sweep([(n, [(j, False)], c)])[0], init)

        def run(last_blocks):
            carries = sweep([(n, last_blocks[n], older_blocks(n)) for n in range(2)])
            for n in range(2):
                dq_ref[n] = (pair.unstack(carries[n][0]) * 0.125).astype(BF16)

        with_previous = lambda d: [(d - 1, False), (d, True)]

        @pl.when(i == 0)
        def _():
            run([[(diag[0], True)], with_previous(diag[1])])

        @pl.when(i > 0)
        def _():
            run([with_previous(diag[0]), with_previous(diag[1])])

    blk = lambda n: pl.BlockSpec((2, T, n), lambda p, i: (0, i, p))
    full = pl.BlockSpec((S, LANES), lambda p, i: (0, p))
    out = jax.ShapeDtypeStruct((S, ATTN_WIDTH), F32)
    dq, dk, dv = pl.pallas_call(
        body, name=f"attn_bwd_l{layer}", grid=(HEAD_PAIRS, half),
        in_specs=[blk(LANES), full, full, blk(2 * LANES), blk(LANES)],
        out_specs=[blk(LANES), full, full],
        out_shape=[jax.ShapeDtypeStruct((2, S // 2, ATTN_WIDTH), BF16), out, out],
        compiler_params=_params(),
    )(_halves(q), k, v, _halves(saved), _halves(do))
    return dq.reshape(S, ATTN_WIDTH), dk, dv


def _pool_counts(row0, tm):
    pos = row0 + lax.broadcasted_iota(jnp.int32, (tm, 1), 0)
    return [1.0 / jnp.minimum(pos + 1, w).astype(F32) for w in POOL_WINDOWS]


def _window_bands(tm, backward):
    t = np.arange(tm)[:, None]
    c = np.arange(tm)[None, :]
    off = c - t if backward else t - c
    main = np.stack([(off >= 0) & (off < w) for w in POOL_WINDOWS])
    r = np.arange(POOL_HALO)[:, None]
    h = np.arange(POOL_HALO)[None, :]
    off = h - r + POOL_HALO if backward else r - h + POOL_HALO
    edge = np.concatenate([(off < w) for w in POOL_WINDOWS])
    return jnp.asarray(main, BF16), jnp.asarray(edge, BF16)


def _window_sums(tile, beside, main_ref, edge_ref, backward):
    tm = tile.shape[0]
    tb = tile.astype(BF16)
    edge = _nn(edge_ref[...], beside.astype(BF16))
    sums = []
    for g in range(len(POOL_WINDOWS)):
        cols = slice(g * POOL_GROUP, (g + 1) * POOL_GROUP)
        tot = _nn(main_ref[g], tb[:, cols])
        extra = edge[g * POOL_HALO:(g + 1) * POOL_HALO, cols]
        if backward:
            sums.append(jnp.concatenate([tot[:tm - POOL_HALO], tot[tm - POOL_HALO:] + extra], axis=0))
        else:
            sums.append(jnp.concatenate([tot[:POOL_HALO] + extra, tot[POOL_HALO:]], axis=0))
    return sums


def _post_forward(u, history, bands, inv_cnt, zp, o, za, gl, bg, pw_ref, scale, wpu_ref, wau_ref):
    pooled, mixed = [], []
    for g, tot in enumerate(_window_sums(u, history, *bands, False)):
        pg = (tot * inv_cnt[g] - u[:, g * POOL_GROUP:(g + 1) * POOL_GROUP]).astype(BF16)
        pooled.append(pg)
        mixed.append(_nn(pg, pw_ref[g].astype(BF16)))
    pooled = jnp.concatenate(pooled, axis=1)
    mixed = jnp.concatenate(mixed, axis=1)
    sp = _sigmoid(zp)
    sa = _sigmoid(za)
    y_pool = (mixed * scale) * (zp * sp)
    y_attn = o * (za * sa)
    gate = _sigmoid(gl + bg)
    g0, g1 = gate[:, :D_MODEL], gate[:, D_MODEL:]
    up_p = _nn(y_pool.astype(BF16), wpu_ref[...])
    up_a = _nn(y_attn.astype(BF16), wau_ref[...])
    merged = g0 * up_p + g1 * up_a
    return pooled, mixed, sp, sa, y_pool, y_attn, g0, g1, up_p, up_a, merged


def _row_specs(tm, rev, n_tiles):
    tile_of = (lambda i: n_tiles - 1 - i) if rev else (lambda i: i)
    row = lambda n: pl.BlockSpec((tm, n), lambda i: (tile_of(i), 0))
    halo = pl.BlockSpec((POOL_HALO, POOL_WIDTH),
                        lambda i: (jnp.maximum(tile_of(i) * (tm // POOL_HALO) - 1, 0), 0))
    const = lambda shape: pl.BlockSpec(shape, lambda i: (0,) * len(shape))
    return tile_of, row, halo, const


def _layer_weight_spec(rows, cols, layer):
    return pl.BlockSpec((None, rows, cols), lambda i: (layer, 0, 0), pipeline_mode=pl.Buffered(1))


def _post_fwd(x, u, zp, o, za, gl, bg, pw, scale, wpu, wau, wout, layer):
    S = x.shape[0]
    tm = min(ROW_TILE, S)
    n_tiles = S // tm
    tile_of, row, halo, const = _row_specs(tm, False, n_tiles)

    def body(x_ref, u_ref, uh_ref, main_ref, edge_ref, zp_ref, o_ref, za_ref, gl_ref, bg_ref, pw_ref, sc_ref, wpu_ref,
             wau_ref, wout_ref, out_ref):
        i = pl.program_id(0)
        vals = _post_forward(u_ref[...], jnp.where(i == 0, 0.0, uh_ref[...]), (main_ref, edge_ref),
                             _pool_counts(i * tm, tm), zp_ref[...], o_ref[...], za_ref[...], gl_ref[...], bg_ref[...],
                             pw_ref, sc_ref[...], wpu_ref, wau_ref)
        out_ref[...] = x_ref[...] + _nn(vals[-1].astype(BF16), wout_ref[...])

    return pl.pallas_call(
        body, name=f"post_fwd_l{layer}", grid=(n_tiles,),
        in_specs=[row(D_MODEL), row(512), halo, const((4, tm, tm)), const((4 * POOL_HALO, POOL_HALO)), row(512),
                  row(512), row(512), row(2048), const((1, 2048)), const((4, POOL_GROUP, POOL_GROUP)),
                  const((1, POOL_WIDTH)),
                  _layer_weight_spec(POOL_WIDTH, D_MODEL, layer), _layer_weight_spec(ATTN_WIDTH, D_MODEL, layer),
                  _layer_weight_spec(D_MODEL, D_MODEL, layer)],
        out_specs=row(D_MODEL),
        out_shape=jax.ShapeDtypeStruct((S, D_MODEL), F32),
        compiler_params=_params(),
    )(x, u, u, *_window_bands(tm, False), zp, o, za, gl, bg, pw, scale, wpu, wau, wout)


def _post_bwd(dx, u, zp, o, za, gl, bg, pw, scale, wpu, wau, wout, layer):
    S = dx.shape[0]
    tm = min(ROW_TILE, S)
    n_tiles = S // tm
    tile_of, row, halo, const = _row_specs(tm, True, n_tiles)

    def body(dx_ref, u_ref, uh_ref, main_ref, edge_ref, back_main_ref, back_edge_ref, zp_ref, o_ref, za_ref, gl_ref,
             bg_ref, pw_ref, sc_ref, wpu_ref, wau_ref, wout_ref,
             duz_ref, do_ref, dza_ref, dgl_ref, dsc_ref, dbg_ref,
             merged_ref, dup_ref, dua_ref, yp_ref, ya_ref, pooled_ref, dmixed_ref, nxt_ref):
        step = pl.program_id(0)
        i = tile_of(step)

        @pl.when(step == 0)
        def _():
            dsc_ref[...] = jnp.zeros_like(dsc_ref)
            dbg_ref[...] = jnp.zeros_like(dbg_ref)
            nxt_ref[...] = jnp.zeros_like(nxt_ref)

        inv_cnt = _pool_counts(i * tm, tm)
        zp, za, o = zp_ref[...], za_ref[...], o_ref[...]
        pooled, mixed, sp, sa, y_pool, y_attn, g0, g1, up_p, up_a, merged = _post_forward(
            u_ref[...], jnp.where(i == 0, 0.0, uh_ref[...]), (main_ref, edge_ref), inv_cnt, zp, o, za, gl_ref[...],
            bg_ref[...], pw_ref, sc_ref[...], wpu_ref, wau_ref)
        merged_ref[...] = merged.astype(BF16)
        yp_ref[...] = y_pool.astype(BF16)
        ya_ref[...] = y_attn.astype(BF16)
        pooled_ref[...] = pooled

        dmerged = _nt(dx_ref[...].astype(BF16), wout_ref[...])
        dup = (dmerged * g0).astype(BF16)
        dua = (dmerged * g1).astype(BF16)
        dup_ref[...] = dup
        dua_ref[...] = dua
        dgl0 = (dmerged * up_p) * (g0 * (1.0 - g0))
        dgl1 = (dmerged * up_a) * (g1 * (1.0 - g1))
        dgl_ref[:, :D_MODEL] = dgl0.astype(BF16)
        dgl_ref[:, D_MODEL:] = dgl1.astype(BF16)
        dbg_ref[:, :D_MODEL] += jnp.sum(dgl0, axis=0, keepdims=True)
        dbg_ref[:, D_MODEL:] += jnp.sum(dgl1, axis=0, keepdims=True)

        dy_attn = _nt(dua, wau_ref[...])
        do_ref[...] = (dy_attn * (za * sa)).astype(BF16)
        dza_ref[...] = ((dy_attn * o) * (sa * (1.0 + za * (1.0 - sa)))).astype(BF16)

        dy_pool = _nt(dup, wpu_ref[...])
        ms = mixed * sc_ref[...]
        dms = dy_pool * (zp * sp)
        duz_ref[:, POOL_WIDTH:] = ((dy_pool * ms) * (sp * (1.0 + zp * (1.0 - sp)))).astype(BF16)
        dsc_ref[...] += jnp.sum(dms * mixed, axis=0, keepdims=True)
        dmixed = (dms * sc_ref[...]).astype(BF16)
        dmixed_ref[...] = dmixed
        dpooled = [_nt(dmixed[:, g * POOL_GROUP:(g + 1) * POOL_GROUP], pw_ref[g].astype(BF16)) for g in range(4)]
        scaled = jnp.concatenate([d * inv for d, inv in zip(dpooled, inv_cnt)], axis=1)
        for g, tot in enumerate(_window_sums(scaled, nxt_ref[...], back_main_ref, back_edge_ref, True)):
            duz_ref[:, g * POOL_GROUP:(g + 1) * POOL_GROUP] = (tot - dpooled[g]).astype(BF16)
        nxt_ref[...] = scaled[:POOL_HALO]

    sd = lambda n, dt: jax.ShapeDtypeStruct((S, n), dt)
    bands = [const((4, tm, tm)), const((4 * POOL_HALO, POOL_HALO))]
    return pl.pallas_call(
        body, name=f"post_bwd_l{layer}", grid=(n_tiles,),
        in_specs=[row(D_MODEL), row(512), halo, *bands, *bands, row(512), row(512), row(512), row(2048),
                  const((1, 2048)), const((4, POOL_GROUP, POOL_GROUP)), const((1, POOL_WIDTH)),
                  _layer_weight_spec(POOL_WIDTH, D_MODEL, layer), _layer_weight_spec(ATTN_WIDTH, D_MODEL, layer),
                  _layer_weight_spec(D_MODEL, D_MODEL, layer)],
        out_specs=[row(1024), row(512), row(512), row(2048), const((1, POOL_WIDTH)), const((1, 2048)),
                   row(D_MODEL), row(D_MODEL), row(D_MODEL), row(512), row(512), row(512), row(512)],
        out_shape=[sd(1024, BF16), sd(512, BF16), sd(512, BF16), sd(2048, BF16),
                   jax.ShapeDtypeStruct((1, POOL_WIDTH), F32), jax.ShapeDtypeStruct((1, 2048), F32),
                   sd(D_MODEL, BF16), sd(D_MODEL, BF16), sd(D_MODEL, BF16), sd(512, BF16), sd(512, BF16),
                   sd(512, BF16), sd(512, BF16)],
        scratch_shapes=[pltpu.VMEM((POOL_HALO, POOL_WIDTH), F32)],
        compiler_params=_params(),
    )(dx, u, u, *_window_bands(tm, False), *_window_bands(tm, True), zp, o, za, gl, bg, pw, scale, wpu, wau, wout)


def _rms_backward(dh, xv, r, g):
    xhat = xv * r
    dxhat = dh * g
    return r * (dxhat - xhat * jnp.mean(dxhat * xhat, axis=-1, keepdims=True)), dh * xhat


def _loss_head(x, g, target):
    S = x.shape[0]
    tm = min(ROW_TILE, S)

    def body(x_ref, g_ref, t_ref, loss_ref, dx_ref, dg_ref):
        @pl.when(pl.program_id(0) == 0)
        def _():
            loss_ref[...] = jnp.zeros_like(loss_ref)
            dg_ref[...] = jnp.zeros_like(dg_ref)

        xv = x_ref[...]
        r = lax.rsqrt(jnp.mean(xv * xv, axis=-1, keepdims=True) + RMS_EPS)
        diff = (xv * r) * g_ref[...] - t_ref[...]
        per_row = jnp.mean(diff * diff, axis=-1, keepdims=True)
        loss_ref[...] += 0.5 * jnp.sum(per_row, axis=0, keepdims=True)
        dx, dg_rows = _rms_backward(diff * (1.0 / D_MODEL), xv, r, g_ref[...])
        dx_ref[...] = dx
        dg_ref[...] += jnp.sum(dg_rows, axis=0, keepdims=True)

    row = pl.BlockSpec((tm, D_MODEL), lambda i: (i, 0))
    vec = pl.BlockSpec((1, D_MODEL), lambda i: (0, 0))
    return pl.pallas_call(
        body, name="loss_head", grid=(S // tm,),
        in_specs=[row, vec, row],
        out_specs=[pl.BlockSpec((1, LANES), lambda i: (0, 0)), row, vec],
        out_shape=[jax.ShapeDtypeStruct((1, LANES), F32), jax.ShapeDtypeStruct((S, D_MODEL), F32),
                   jax.ShapeDtypeStruct((1, D_MODEL), F32)],
        compiler_params=_params(),
    )(x, g, target)


def _inproj_bwd(pieces, w_in, x, g, dx_res, layer):
    S = x.shape[0]
    tm = min(PROJ_ROW_TILE, S)
    cols = [(c0, p.shape[1]) for p, c0 in pieces]

    def body(*refs):
        piece_refs = refs[:len(cols)]
        w_ref, x_ref, g_ref, res_ref, dx_ref, dg_ref = refs[len(cols):]

        @pl.when(pl.program_id(0) == 0)
        def _():
            dg_ref[...] = jnp.zeros_like(dg_ref)

        dh = jnp.zeros((tm, D_MODEL), F32)
        for p_ref, (c0, n) in zip(piece_refs, cols):
            for c in range(0, n, 512):
                dh = dh + _nt(p_ref[:, c:c + 512].astype(BF16), w_ref[:, c0 + c:c0 + c + 512])
        xv = x_ref[...]
        r = lax.rsqrt(jnp.mean(xv * xv, axis=-1, keepdims=True) + RMS_EPS)
        dx, dg_rows = _rms_backward(dh, xv, r, g_ref[...])
        dx_ref[...] = res_ref[...] + dx
        dg_ref[...] += jnp.sum(dg_rows, axis=0, keepdims=True)

    row = lambda n: pl.BlockSpec((tm, n), lambda i: (i, 0))
    vec = pl.BlockSpec((1, D_MODEL), lambda i: (0, 0))
    return pl.pallas_call(
        body, name=f"inproj_bwd_l{layer}", grid=(S // tm,),
        in_specs=[row(n) for _, n in cols] + [_layer_weight_spec(D_MODEL, IN_WIDTH, layer), row(D_MODEL), vec,
                                              row(D_MODEL)],
        out_specs=[row(D_MODEL), vec],
        out_shape=[jax.ShapeDtypeStruct((S, D_MODEL), F32), jax.ShapeDtypeStruct((1, D_MODEL), F32)],
        compiler_params=_params(),
    )(*[p for p, _ in pieces], w_in, x, g, dx_res)


def _wgrad(a, b, name, layer, into=None, col0=0, n_total=None):
    S, M = a.shape
    N = b.shape[1]
    n_total = N if n_total is None else n_total
    tk = min(2048, S)
    tn = min(512, N)
    nk = S // tk

    def body(*refs):
        a_ref, b_ref, out_ref = refs[0], refs[1], refs[-1]
        prod = _tn(a_ref[...].astype(BF16), b_ref[...].astype(BF16))

        @pl.when(pl.program_id(1) == 0)
        def _():
            out_ref[...] = prod

        @pl.when(pl.program_id(1) > 0)
        def _():
            out_ref[...] += prod

    in_specs = [pl.BlockSpec((tk, M), lambda j, k: (k, 0)), pl.BlockSpec((tk, tn), lambda j, k: (k, j))]
    args = [a, b]
    aliases = {}
    if into is not None:
        in_specs.append(pl.BlockSpec(memory_space=pl.ANY))
        args.append(into)
        aliases = {2: 0}
    return pl.pallas_call(
        body, name=name, grid=(N // tn, nk),
        in_specs=in_specs,
        out_specs=pl.BlockSpec((None, M, tn), lambda j, k: (layer, 0, col0 // tn + j)),
        out_shape=jax.ShapeDtypeStruct((2, M, n_total), F32),
        input_output_aliases=aliases,
        compiler_params=_params(),
    )(*args)


def _pool_wgrad(pooled, dmixed, layer):
    S = pooled.shape[0]
    tk = min(1024, S)

    def body(a_ref, b_ref, out_ref):
        prod = _tn(a_ref[...], b_ref[...])

        @pl.when(pl.program_id(1) == 0)
        def _():
            out_ref[...] = prod

        @pl.when(pl.program_id(1) > 0)
        def _():
            out_ref[...] += prod

    blk = pl.BlockSpec((tk, POOL_GROUP), lambda g, k: (k, g))
    return pl.pallas_call(
        body, name=f"pool_wgrad_l{layer}", grid=(4, S // tk),
        in_specs=[blk, blk],
        out_specs=pl.BlockSpec((None, POOL_GROUP, POOL_GROUP), lambda g, k: (g, 0, 0)),
        out_shape=jax.ShapeDtypeStruct((4, POOL_GROUP, POOL_GROUP), F32),
        compiler_params=_params(),
    )(pooled, dmixed)


def _local_step(x, target, norm_g, b_gate, pool_w, pool_scale, final_g, w_in, w_pu, w_au, w_out):
    n_layers = norm_g.shape[0]
    saved = []
    for l in range(n_layers):
        g = norm_g[l][None]
        bg = b_gate[l][None]
        sc = pool_scale[l][None]
        u, zp, q, k, v, za, gl, h = _rms_inproj(x, g, w_in, l)
        o, carry = _attn_fwd(q, k, v, l)
        saved.append((x, g, bg, sc, u, zp, q, k, v, za, gl, h, o, carry))
        x = _post_fwd(x, u, zp, o, za, gl, bg, pool_w[l], sc, w_pu, w_au, w_out, l)
    loss, dx, d_final_g = _loss_head(x, final_g[None], target)

    small = [None] * n_layers
    dw_in = dw_out = dw_pu = dw_au = None
    for l in reversed(range(n_layers)):
        x_in, g, bg, sc, u, zp, q, k, v, za, gl, h, o, carry = saved[l]
        (duz, do, dza, dgl, dsc, dbg, merged, dup, dua, y_pool, y_attn, pooled, dmixed) = _post_bwd(
            dx, u, zp, o, za, gl, bg, pool_w[l], sc, w_pu, w_au, w_out, l)
        dq, dk, dv = _attn_bwd(q, k, v, carry, do, l)
        pieces = [(duz, C_U), (dq, C_Q), (dk, C_K), (dv, C_V), (dza, C_ZA), (dgl, C_GL)]
        for p, c0 in pieces:
            dw_in = _wgrad(h, p, f"wgrad_in_l{l}_c{c0}", l, into=dw_in, col0=c0, n_total=IN_WIDTH)
        dw_out = _wgrad(merged, dx, f"wgrad_out_l{l}", l, into=dw_out)
        dw_pu = _wgrad(y_pool, dup, f"wgrad_pu_l{l}", l, into=dw_pu)
        dw_au = _wgrad(y_attn, dua, f"wgrad_au_l{l}", l, into=dw_au)
        dpw = _pool_wgrad(pooled, dmixed, l)
        dx, dg = _inproj_bwd(pieces, w_in, x_in, g, dx, l)
        small[l] = (dg[0], dbg[0], dpw, dsc[0])
    small = [jnp.stack([small[l][i] for l in range(n_layers)]) for i in range(4)]
    return loss[0, 0], dx, d_final_g[0], small, (dw_in, dw_pu, dw_au, dw_out)


SHARDED = ((2, 1280), (2, 256), (2, 256), (1, 256))
ANY = pl.BlockSpec(memory_space=pl.ANY)


def _part(ref, s, axis, width):
    sl = pl.ds(pl.multiple_of(s * width, width), width)
    return ref.at[:, sl] if axis == 2 else ref.at[sl, :]


def _place():
    x, y, c = lax.axis_index("x"), lax.axis_index("y"), lax.axis_index("c")
    return x, y, c, 2 * x + y


def _other_chip(x, y, m):
    px = 1 - x if m & 2 else x
    py = 1 - y if m & 1 else y
    return px, py, 2 * px + py


def _remote(src, dst, send, recv, k, to):
    return pltpu.make_async_remote_copy(src_ref=src, dst_ref=dst, send_sem=send.at[k], recv_sem=recv.at[k],
                                        device_id=to, device_id_type=MESH)


def _part_spec(tr, rows_s, cols_s, axis, width, lead):
    if axis == 2:
        return pl.BlockSpec((None, tr, width), lambda *a: (lead(a), a[-2], a[-1][1]))
    return pl.BlockSpec((None, tr, cols_s), lambda *a: (lead(a), a[-1][1] * (rows_s // tr) + a[-2], 0))


def _cast_into_place(w, pos, axis, width, name):
    L, Rs, Cs = w.shape
    tr = min(256, Rs)
    shape = [L, Rs, Cs]
    shape[axis] *= N_CHIPS

    def body(pos_ref, w_ref, o_ref):
        o_ref[...] = w_ref[...].astype(BF16)

    return pl.pallas_call(
        body, name=name,
        grid_spec=pltpu.PrefetchScalarGridSpec(
            num_scalar_prefetch=1, grid=(L, Rs // tr),
            in_specs=[pl.BlockSpec((None, tr, Cs), lambda l, i, pos: (l, i, 0))],
            out_specs=_part_spec(tr, Rs, Cs, axis, width, lambda a: a[0])),
        out_shape=jax.ShapeDtypeStruct(tuple(shape), BF16),
        compiler_params=_params(),
    )(pos, w)


def _gather_weights(fulls):
    n = len(SHARDED)

    def body(*refs):
        outs = refs[n:2 * n]
        send, recv = refs[2 * n:]
        x, y, c, s = _place()
        me, sibling = (x, y, c), (x, y, 1 - c)

        def copy(k, a, layer, shard, to):
            part = _part(outs[a].at[layer], shard, *SHARDED[a])
            return _remote(part, part, send, recv, k, to)

        sent = []
        for a in range(n):
            for m in (1, 2, 3):
                px, py, _ = _other_chip(x, y, m)
                sent.append(copy(3 * a + m - 1, a, c, s, (px, py, c)))
                sent[-1].start()
        for m in (1, 2, 3):
            _, _, sp = _other_chip(x, y, m)
            for a in range(n):
                copy(3 * a + m - 1, a, c, sp, me).wait_recv()
                sent.append(copy(3 * n + 3 * a + m - 1, a, c, sp, sibling))
                sent[-1].start()
        for m in (1, 2, 3):
            _, _, sp = _other_chip(x, y, m)
            for a in range(n):
                copy(3 * n + 3 * a + m - 1, a, 1 - c, sp, me).wait_recv()
        for cp in sent:
            cp.wait_send()

    return pl.pallas_call(
        body, name="gather_weights",
        in_specs=[ANY] * n, out_specs=[ANY] * n,
        out_shape=[jax.ShapeDtypeStruct(f.shape, f.dtype) for f in fulls],
        input_output_aliases={a: a for a in range(n)},
        scratch_shapes=[pltpu.SemaphoreType.DMA((6 * n,)), pltpu.SemaphoreType.DMA((6 * n,))],
    )(*fulls)


def _sibling_exchange(dws):
    n = len(dws)

    def body(*refs):
        ins, outs = refs[:n], refs[n:2 * n]
        send, recv = refs[2 * n:]
        x, y, c, _ = _place()
        copies = [_remote(ins[a].at[1 - c], outs[a], send, recv, a, (x, y, 1 - c)) for a in range(n)]
        for cp in copies:
            cp.start()
        for cp in copies:
            cp.wait()

    return pl.pallas_call(
        body, name="grad_sibling_exchange",
        in_specs=[ANY] * n, out_specs=[ANY] * n,
        out_shape=[jax.ShapeDtypeStruct(d.shape[1:], d.dtype) for d in dws],
        scratch_shapes=[pltpu.SemaphoreType.DMA((n,)), pltpu.SemaphoreType.DMA((n,))],
    )(*dws)


def _pair_sum(dw, other, pos, name):
    _, R, C = dw.shape
    tr = 128 if C > 1024 else 256

    def body(pos_ref, a_ref, b_ref, o_ref, ob_ref):
        tot = a_ref[...] + b_ref[...]
        o_ref[...] = tot
        ob_ref[...] = tot.astype(BF16)

    blk = pl.BlockSpec((tr, C), lambda i, pos: (i, 0))
    return pl.pallas_call(
        body, name=name,
        grid_spec=pltpu.PrefetchScalarGridSpec(
            num_scalar_prefetch=1, grid=(R // tr,),
            in_specs=[pl.BlockSpec((None, tr, C), lambda i, pos: (pos[0], i, 0)), blk],
            out_specs=[blk, blk]),
        out_shape=[jax.ShapeDtypeStruct((R, C), F32), jax.ShapeDtypeStruct((R, C), BF16)],
        compiler_params=_params(),
    )(pos, dw, other)


def _chip_exchange(ps, packed):
    n = len(ps)

    def body(*refs):
        ins, small_ref = refs[:n], refs[n]
        outs, total_ref = refs[n + 1:2 * n + 1], refs[2 * n + 1]
        all_ref, send, recv, small_send, small_recv = refs[2 * n + 2:]
        x, y, c, _ = _place()
        my_id = 4 * x + 2 * y + c
        all_ref[my_id] = small_ref[...]
        small = []
        for m in range(1, N_DEVICES):
            px = 1 - x if m & 4 else x
            py = 1 - y if m & 2 else y
            pc = 1 - c if m & 1 else c
            cp = _remote(small_ref, all_ref.at[my_id], small_send, small_recv, m - 1, (px, py, pc))
            cp.start()
            small.append((cp, 4 * px + 2 * py + pc))
        copies = []
        for a, (axis, width) in enumerate(SHARDED):
            for m in (1, 2, 3):
                px, py, sp = _other_chip(x, y, m)
                copies.append(_remote(_part(ins[a], sp, axis, width), outs[a].at[m - 1], send, recv, 3 * a + m - 1,
                                      (px, py, c)))
        for cp in copies:
            cp.start()
        for m, (cp, peer_id) in enumerate(small):
            _remote(small_ref, all_ref.at[peer_id], small_send, small_recv, m, (x, y, c)).wait_recv()
        total = all_ref[0]
        for d in range(1, N_DEVICES):
            total = total + all_ref[d]
        total_ref[...] = total
        for cp, _ in small:
            cp.wait_send()
        for cp in copies:
            cp.wait()

    def landing(p, axis, width):
        shape = [3] + list(p.shape)
        shape[axis] = width
        return jax.ShapeDtypeStruct(tuple(shape), p.dtype)

    vmem = pl.BlockSpec(memory_space=pltpu.VMEM)
    res = pl.pallas_call(
        body, name="grad_chip_exchange",
        in_specs=[ANY] * n + [vmem], out_specs=[ANY] * n + [vmem],
        out_shape=[landing(p, axis, width) for p, (axis, width) in zip(ps, SHARDED)]
        + [jax.ShapeDtypeStruct(packed.shape, packed.dtype)],
        scratch_shapes=[pltpu.VMEM((N_DEVICES,) + packed.shape, F32),
                        pltpu.SemaphoreType.DMA((3 * n,)), pltpu.SemaphoreType.DMA((3 * n,)),
                        pltpu.SemaphoreType.DMA((N_DEVICES - 1,)), pltpu.SemaphoreType.DMA((N_DEVICES - 1,))],
        compiler_params=_params(),
    )(*ps, packed)
    return res[:n], res[n]


def _shard_sum(p, landed, pos, axis, width, name):
    _, Rs, Cs = landed.shape
    tr = min(256, Rs)
    p_spec = _part_spec(tr, Rs, Cs, axis, width, lambda a: 0)

    def body(pos_ref, p_ref, l_ref, o_ref):
        o_ref[...] = ((p_ref[...] + l_ref[0].astype(F32)) + l_ref[1].astype(F32)) + l_ref[2].astype(F32)

    return pl.pallas_call(
        body, name=name,
        grid_spec=pltpu.PrefetchScalarGridSpec(
            num_scalar_prefetch=1, grid=(Rs // tr,),
            in_specs=[p_spec, pl.BlockSpec((3, tr, Cs), lambda i, pos: (0, i, 0))],
            out_specs=pl.BlockSpec((None, tr, Cs), lambda i, pos: (pos[0], i, 0))),
        out_shape=jax.ShapeDtypeStruct((2, Rs, Cs), F32),
        compiler_params=_params(),
    )(pos, p[None], landed)


def _sibling_share(gs):
    n = len(gs)

    def body(*refs):
        outs = refs[n:2 * n]
        send, recv = refs[2 * n:]
        x, y, c, _ = _place()
        copies = [_remote(outs[a].at[c], outs[a].at[c], send, recv, a, (x, y, 1 - c)) for a in range(n)]
        for cp in copies:
            cp.start()
        for a, cp in enumerate(copies):
            cp.wait_send()
            _remote(outs[a].at[1 - c], outs[a].at[1 - c], send, recv, a, (x, y, c)).wait_recv()

    return pl.pallas_call(
        body, name="grad_sibling_share",
        in_specs=[ANY] * n, out_specs=[ANY] * n,
        out_shape=[jax.ShapeDtypeStruct(g.shape, g.dtype) for g in gs],
        input_output_aliases={a: a for a in range(n)},
        scratch_shapes=[pltpu.SemaphoreType.DMA((n,)), pltpu.SemaphoreType.DMA((n,))],
    )(*gs)


def _adamw(w, g, m, v, name):
    shape = w.shape
    C = shape[-1]
    flat = [t.reshape(-1, C) for t in (w, g, m, v)]
    R = flat[0].shape[0]
    tr = max(t for t in range(8, R + 1, 8) if R % t == 0 and t * C <= 384 * 1024)

    def body(w_ref, g_ref, m_ref, v_ref, d_ref, nm_ref, nv_ref):
        gv = g_ref[...]
        nm = ADAM_B1 * m_ref[...] + (1.0 - ADAM_B1) * gv
        nv = ADAM_B2 * v_ref[...] + (1.0 - ADAM_B2) * (gv * gv)
        m_hat = nm / (1.0 - ADAM_B1 ** ADAM_STEP)
        v_hat = nv / (1.0 - ADAM_B2 ** ADAM_STEP)
        d_ref[...] = -ADAM_LR * (m_hat / (jnp.sqrt(v_hat) + ADAM_EPS) + ADAM_WD * w_ref[...])
        nm_ref[...] = nm
        nv_ref[...] = nv

    blk = pl.BlockSpec((tr, C), lambda i: (i, 0))
    out = jax.ShapeDtypeStruct((R, C), F32)
    res = pl.pallas_call(
        body, name=name, grid=(R // tr,),
        in_specs=[blk] * 4, out_specs=[blk] * 3, out_shape=[out] * 3,
        compiler_params=_params(),
    )(*flat)
    return [t.reshape(shape) for t in res]


SMALL_SHAPES = ((2, 1024), (2, 2048), (2, 4, 128, 128), (2, 512), (1024,))


def _pack_small(parts):
    return jnp.concatenate([p.reshape(-1, LANES) for p in parts], axis=0)


def _unpack_small(packed):
    out, row = [], 0
    for shape in SMALL_SHAPES:
        n = 1
        for d in shape:
            n *= d
        out.append(packed[row:row + n // LANES].reshape(shape))
        row += n // LANES
    return out


def kernel(x, norm_g, w_in, b_gate, pool_w, pool_scale, w_pool_up, w_attn_up, w_out, final_g, loss_target, m_norm_g, m_w_in, m_b_gate, m_pool_w, m_pool_scale, m_w_pool_up, m_w_attn_up, m_w_out, m_final_g, v_norm_g, v_w_in, v_b_gate, v_pool_w, v_pool_scale, v_w_pool_up, v_w_attn_up, v_w_out, v_final_g):
    _, _, c, s = _place()
    pos = jnp.stack([c, s]).astype(jnp.int32)
    names = ("w_in", "w_pool_up", "w_attn_up", "w_out")

    full = _gather_weights([_cast_into_place(w, pos, axis, width, f"cast_{n}")
                            for w, (axis, width), n in zip((w_in, w_pool_up, w_attn_up, w_out), SHARDED, names)])
    loss_part, dx, d_final_g, small, dws = _local_step(x[0], loss_target[0], norm_g, b_gate, pool_w, pool_scale,
                                                       final_g, *full)

    other = _sibling_exchange(dws)
    pair = [_pair_sum(d, o, pos, f"grad_pair_sum_{n}") for d, o, n in zip(dws, other, names)]
    landed, summed = _chip_exchange([pb for _, pb in pair],
                                    _pack_small(small + [d_final_g, jnp.broadcast_to(loss_part, (8, LANES))]))
    mine = [_shard_sum(p, l, pos, axis, width, f"grad_shard_sum_{n}")
            for (p, _), l, (axis, width), n in zip(pair, landed, SHARDED, names)]
    g_in, g_pu, g_au, g_out = _sibling_share(mine)
    g_small = _unpack_small(summed)
    loss = summed[-8, 0]
    upd_small = _adamw(_pack_small([norm_g, b_gate, pool_w, pool_scale, final_g]), _pack_small(g_small),
                       _pack_small([m_norm_g, m_b_gate, m_pool_w, m_pool_scale, m_final_g]),
                       _pack_small([v_norm_g, v_b_gate, v_pool_w, v_pool_scale, v_final_g]), "adamw_small")
    d_small, nm_small, nv_small = [_unpack_small(t) for t in upd_small]
    upd_in = _adamw(w_in, g_in, m_w_in, v_w_in, "adamw_w_in")
    upd_pu = _adamw(w_pool_up, g_pu, m_w_pool_up, v_w_pool_up, "adamw_w_pool_up")
    upd_au = _adamw(w_attn_up, g_au, m_w_attn_up, v_w_attn_up, "adamw_w_attn_up")
    upd_out = _adamw(w_out, g_out, m_w_out, v_w_out, "adamw_w_out")

    def ordered(sm, k):
        big = (upd_in[k], upd_pu[k], upd_au[k], upd_out[k]) if k is not None else (g_in, g_pu, g_au, g_out)
        return [sm[0], big[0], sm[1], sm[2], sm[3], big[1], big[2], big[3], sm[4]]

    return (loss, dx[None], *ordered(g_small, None), *ordered(d_small, 0), *ordered(nm_small, 1),
            *ordered(nv_small, 2))
```

```python
import jax
import jax.numpy as jnp
import numpy as np
from jax import lax
from jax.experimental import pallas as pl
from jax.experimental.pallas import tpu as pltpu

F32 = jnp.float32
BF16 = jnp.bfloat16
MESH = pl.DeviceIdType.MESH

D_MODEL = 1024
POOL_WIDTH = 512
POOL_WINDOWS = (2, 4, 8, 16)
POOL_GROUP = 128
POOL_HALO = 16
ATTN_WIDTH = 512
HEAD_DIM = 64
HEAD_PAIRS = 4
IN_WIDTH = 5120
N_CHIPS = 4
N_DEVICES = 8
RMS_EPS = 1e-6
C_U, C_ZP, C_Q, C_K, C_V, C_ZA, C_GL = 0, 512, 1024, 1536, 2048, 2560, 3072

ADAM_LR, ADAM_B1, ADAM_B2, ADAM_EPS, ADAM_WD, ADAM_STEP = 0.001, 0.9, 0.999, 1e-08, 0.01, 10

LANES = 128
ATTN_BLOCK = 256
ROW_TILE = 256
PROJ_ROW_TILE = 512
VMEM_LIMIT = 56 * 1024 * 1024


def _params(**kw):
    return pltpu.CompilerParams(vmem_limit_bytes=VMEM_LIMIT, **kw)


def _nt(a, b):
    return lax.dot_general(a, b, (((1,), (1,)), ((), ())), preferred_element_type=F32)


def _tn(a, b):
    return lax.dot_general(a, b, (((0,), (0,)), ((), ())), preferred_element_type=F32)


def _nn(a, b):
    return jnp.dot(a, b, preferred_element_type=F32)


def _sigmoid(z):
    return 1.0 / (1.0 + jnp.exp(-z))


def _rms_inproj(x, g, w_in, layer, gather=()):
    S = x.shape[0]
    tm = min(PROJ_ROW_TILE, S)
    n_tiles = S // tm
    n_g = len(gather)

    def body(*refs):
        x_ref, g_ref = refs[:2]
        if n_g:
            (u_ref, zp_ref, q_ref, k_ref, v_ref, za_ref, gl_ref, h_ref) = refs[2 + n_g:10 + n_g]
            fulls = refs[10 + n_g:10 + 2 * n_g]
            w_ref, load_sem, send, recv = refs[10 + 2 * n_g:]
            later = _Gather(LATER_PIECES, fulls, send, recv)

            @pl.when(pl.program_id(0) == 0)
            def _():
                load = pltpu.make_async_copy(fulls[0].at[layer], w_ref, load_sem)
                load.start()
                later.start()
                load.wait()

            @pl.when(pl.program_id(0) == n_tiles - 1)
            def _():
                later.finish()
        else:
            w_ref, u_ref, zp_ref, q_ref, k_ref, v_ref, za_ref, gl_ref, h_ref = refs[2:]
        xv = x_ref[...]
        r = lax.rsqrt(jnp.mean(xv * xv, axis=-1, keepdims=True) + RMS_EPS)
        h = ((xv * r) * g_ref[...]).astype(BF16)
        h_ref[...] = h

        def mm(c0, n):
            return _nn(h, w_ref[:, c0:c0 + n])

        u_ref[...] = mm(C_U, 512)
        zp_ref[...] = mm(C_ZP, 512)
        q_ref[...] = (mm(C_Q, 512) * 0.125).astype(BF16)
        k_ref[...] = mm(C_K, 512).astype(BF16)
        v_ref[...] = mm(C_V, 512).astype(BF16)
        za_ref[...] = mm(C_ZA, 512)
        for c in range(4):
            gl_ref[:, c * 512:(c + 1) * 512] = mm(C_GL + c * 512, 512).astype(BF16)

    row = lambda n: pl.BlockSpec((tm, n), lambda i: (i, 0))
    sd = lambda n, dt: jax.ShapeDtypeStruct((S, n), dt)
    any_space = pl.BlockSpec(memory_space=pl.ANY)
    weights = [any_space] * n_g if n_g else [_layer_weight_spec(D_MODEL, IN_WIDTH, layer)]
    res = pl.pallas_call(
        body, name=f"rms_inproj_l{layer}", grid=(n_tiles,),
        in_specs=[row(D_MODEL), pl.BlockSpec((1, D_MODEL), lambda i: (0, 0))] + weights,
        out_specs=[row(512), row(512), row(512), row(512), row(512), row(512), row(2048), row(D_MODEL)]
        + [any_space] * n_g,
        out_shape=[sd(512, F32), sd(512, F32), sd(512, BF16), sd(512, BF16), sd(512, BF16), sd(512, F32),
                   sd(2048, BF16), sd(D_MODEL, BF16)] + [jax.ShapeDtypeStruct(f.shape, f.dtype) for f in gather],
        input_output_aliases={2 + a: 8 + a for a in range(n_g)},
        scratch_shapes=([pltpu.VMEM((D_MODEL, IN_WIDTH), BF16), pltpu.SemaphoreType.DMA(())]
                        + _Gather.semaphores(LATER_PIECES)) if n_g else [],
        compiler_params=_params(),
    )(x, g, *(gather if n_g else (w_in,)))
    return res[:8], res[8:]


def _tri(n, strict_lower):
    r = lax.broadcasted_iota(jnp.int32, (n, n), 0)
    c = lax.broadcasted_iota(jnp.int32, (n, n), 1)
    return jnp.where(r > c if strict_lower else r < c, 1.0, 0.0).astype(BF16)


def _split_dot(x, m):
    hi = x.astype(BF16)
    lo = (x - hi.astype(F32)).astype(BF16)
    return _nn(hi, m) + _nn(lo, m)


def _log_terms(z):
    lg = jnp.log(1.0 + jnp.exp(-jnp.abs(z)))
    a = jnp.minimum(z, 0.0) - lg
    return a, a - z


EXHAUSTED = -104.0
UNREACHED = -1e30


class _HeadPair:
    def __init__(self, T):
        self.T = T
        self.first = lax.broadcasted_iota(jnp.int32, (T, LANES), 1) < HEAD_DIM
        self.lane = lax.broadcasted_iota(jnp.int32, (2 * T, LANES), 1)
        row = lax.broadcasted_iota(jnp.int32, (2 * T, T), 0)
        row = jnp.where(row >= T, row - T, row)
        self.causal = row > lax.broadcasted_iota(jnp.int32, (2 * T, T), 1)
        self.below = _tri(T, True)

    def stack(self, x2):
        return jnp.concatenate([jnp.where(self.first, x2, 0), jnp.where(self.first, 0, x2)], axis=0).astype(BF16)

    def unstack(self, x):
        return jnp.where(self.first, x[:self.T], x[self.T:])

    def keys(self, ref, blocks):
        T = self.T
        return jnp.concatenate([ref[pl.ds(pl.multiple_of(j * T, T), T), :] for j, _ in blocks], axis=0)

    def log_terms(self, z, blocks):
        T = self.T
        a_all, l_all = _log_terms(z)
        a = [a_all[:, b * T:(b + 1) * T] for b in range(len(blocks))]
        l1m = [l_all[:, b * T:(b + 1) * T] for b in range(len(blocks))]
        return a, [jnp.where(self.causal, l, 0.0) if diagonal else l for l, (_, diagonal) in zip(l1m, blocks)]

    def later_sums(self, l1m):
        later = _split_dot(jnp.concatenate(l1m, axis=0), self.below)
        return [later[2 * self.T * b:2 * self.T * (b + 1)] for b in range(len(l1m))]


def _halves(x):
    return x.reshape(2, x.shape[0] // 2, x.shape[1])


def _attn_fwd(q, k, v, layer):
    S = q.shape[0]
    T = min(ATTN_BLOCK, S)
    nq = S // T
    assert nq <= LANES and nq % 2 == 0
    half = nq // 2

    def body(q_ref, k_ref, v_ref, o_ref, c_ref):
        i = pl.program_id(1)
        pair = _HeadPair(T)
        qs = [pair.stack(q_ref[0]), pair.stack(q_ref[1])]
        diag = [i, i + half]

        def sweep(jobs):
            kv = [(pair.keys(k_ref, bl), pair.keys(v_ref, bl)) for _, bl, _ in jobs]
            zs = [_nt(qs[n], kcat) for (n, _, _), (kcat, _) in zip(jobs, kv)]
            terms = [pair.log_terms(z, bl) for (_, bl, _), z in zip(jobs, zs)]
            laters = [pair.later_sums(l1m) for _, l1m in terms]
            weights = []
            for (_, bl, (acc, run, saved)), (a, l1m), later in zip(jobs, terms, laters):
                ws = []
                for b, (j, diagonal) in enumerate(bl):
                    saved = jnp.where(pair.lane == j, run, saved)
                    w = jnp.exp(a[b] + later[b] + run)
                    ws.append(jnp.where(pair.causal, w, 0.0) if diagonal else w)
                    run = run + jnp.sum(l1m[b], axis=1, keepdims=True)
                weights.append((jnp.concatenate(ws, axis=1).astype(BF16), acc, run, saved))
            return [(acc + _nn(w, vcat), run, saved) for (w, acc, run, saved), (_, vcat) in zip(weights, kv)]

        def alive(carry):
            return (jnp.max(carry[1]) > EXHAUSTED).astype(jnp.int32)

        def older_blocks(n, carry):
            def older_block(state):
                j, _, c = state
                c = sweep([(n, [(j, False)], c)])[0]
                return j - 1, alive(c), c

            return lax.while_loop(lambda st: jnp.logical_and(st[0] >= 0, st[1] > 0), older_block,
                                  (diag[n] - 2, alive(carry), carry))[2]

        def run(first_blocks):
            init = (jnp.zeros((2 * T, LANES), F32), jnp.zeros((2 * T, 1), F32),
                    jnp.full((2 * T, LANES), UNREACHED, F32))
            carries = sweep([(n, first_blocks[n], init) for n in range(2)])
            for n in range(2):
                acc, _, saved = older_blocks(n, carries[n])
                o_ref[n] = pair.unstack(acc)
                c_ref[n, :, :LANES] = saved[:T]
                c_ref[n, :, LANES:] = saved[T:]

        with_previous = lambda d: [(d, True), (d - 1, False)]

        @pl.when(i == 0)
        def _():
            run([[(diag[0], True)], with_previous(diag[1])])

        @pl.when(i > 0)
        def _():
            run([with_previous(diag[0]), with_previous(diag[1])])

    blk = lambda n: pl.BlockSpec((2, T, n), lambda p, i: (0, i, p))
    full = pl.BlockSpec((S, LANES), lambda p, i: (0, p))
    o, carry = pl.pallas_call(
        body, name=f"attn_fwd_l{layer}", grid=(HEAD_PAIRS, half),
        in_specs=[blk(LANES), full, full],
        out_specs=[blk(LANES), blk(2 * LANES)],
        out_shape=[jax.ShapeDtypeStruct((2, S // 2, ATTN_WIDTH), F32),
                   jax.ShapeDtypeStruct((2, S // 2, 8 * LANES), F32)],
        compiler_params=_params(),
    )(_halves(q), k, v)
    return o.reshape(S, ATTN_WIDTH), carry.reshape(S, 8 * LANES)


def _attn_bwd(q, k, v, saved, do, layer):
    S = q.shape[0]
    T = min(ATTN_BLOCK, S)
    nq = S // T

    half = nq // 2

    def body(q_ref, k_ref, v_ref, c_ref, do_ref, dq_ref, dk_ref, dv_ref):
        i = pl.program_id(1)

        @pl.when(i == 0)
        def _():
            dk_ref[...] = jnp.zeros_like(dk_ref)
            dv_ref[...] = jnp.zeros_like(dv_ref)

        pair = _HeadPair(T)
        diag = [i, i + half]
        qs = [pair.stack(q_ref[n]) for n in range(2)]
        dos = [pair.stack(do_ref[n].astype(BF16)) for n in range(2)]
        saved = [jnp.concatenate([c_ref[n, :, :LANES], c_ref[n, :, LANES:]], axis=0) for n in range(2)]
        before = _tri(T, False)

        def sweep(jobs):
            kv = [(pair.keys(k_ref, bl), pair.keys(v_ref, bl)) for _, bl, _ in jobs]
            zs = [_nt(qs[n], kcat) for (n, _, _), (kcat, _) in zip(jobs, kv)]
            gs = [_nt(dos[n], vcat) for (n, _, _), (_, vcat) in zip(jobs, kv)]
            terms = [pair.log_terms(z, bl) for (_, bl, _), z in zip(jobs, zs)]
            laters = [pair.later_sums(l1m) for _, l1m in terms]
            ws, es = [], []
            for (n, bl, _), (a, _), later, g in zip(jobs, terms, laters, gs):
                w_job, e_job = [], []
                for b, (j, diagonal) in enumerate(bl):
                    run = jnp.sum(jnp.where(pair.lane == j, saved[n], 0.0), axis=1, keepdims=True)
                    w = jnp.exp(a[b] + later[b] + run)
                    w_job.append(jnp.where(pair.causal, w, 0.0) if diagonal else w)
                    e_job.append(w_job[b] * g[:, b * T:(b + 1) * T])
                ws.append(w_job)
                es.append(e_job)
            prefixes = [_nn(jnp.concatenate(e_job, axis=0).astype(BF16), before) for e_job in es]
            dzs, olders = [], []
            for (_, bl, (_, older)), (a, _), e_job, prefix in zip(jobs, terms, es, prefixes):
                dz_job = []
                for b, (j, diagonal) in enumerate(bl):
                    dz = e_job[b] - jnp.exp(a[b]) * (e_job[b] + (prefix[2 * T * b:2 * T * (b + 1)] + older))
                    dz_job.append(jnp.where(pair.causal, dz, 0.0) if diagonal else dz)
                    older = older + jnp.sum(e_job[b], axis=1, keepdims=True)
                dzs.append(jnp.concatenate(dz_job, axis=1).astype(BF16))
                olders.append(older)
            out = []
            for (n, bl, (dq, _)), dz, w_job, older, (kcat, _) in zip(jobs, dzs, ws, olders, kv):
                dk = _tn(dz, qs[n])
                dv = _tn(jnp.concatenate(w_job, axis=1).astype(BF16), dos[n])
                for b, (j, _) in enumerate(bl):
                    rows = pl.ds(pl.multiple_of(j * T, T), T)
                    dk_ref[rows, :] += dk[b * T:(b + 1) * T]
                    dv_ref[rows, :] += dv[b * T:(b + 1) * T]
                out.append((dq + _nn(dz, kcat), older))
            return out

        def older_blocks(n):
            col_max = jnp.max(saved[n], axis=0, keepdims=True)
            lane_row = lax.broadcasted_iota(jnp.int32, (1, LANES), 1)
            reached = jnp.sum(jnp.where(jnp.logical_and(col_max > EXHAUSTED, lane_row < diag[n]), 1, 0))
            init = (jnp.zeros((2 * T, LANES), F32), jnp.zeros((2 * T, 1), F32))
            return lax.fori_loop(diag[n] - reached, diag[n] - 1, lambda j, c: sweep([(n, [(j, False)], c)])[0], init)

        def run(last_blocks):
            carries = sweep([(n, last_blocks[n], older_blocks(n)) for n in range(2)])
            for n in range(2):
                dq_ref[n] = (pair.unstack(carries[n][0]) * 0.125).astype(BF16)

        with_previous = lambda d: [(d - 1, False), (d, True)]

        @pl.when(i == 0)
        def _():
            run([[(diag[0], True)], with_previous(diag[1])])

        @pl.when(i > 0)
        def _():
            run([with_previous(diag[0]), with_previous(diag[1])])

    blk = lambda n: pl.BlockSpec((2, T, n), lambda p, i: (0, i, p))
    full = pl.BlockSpec((S, LANES), lambda p, i: (0, p))
    out = jax.ShapeDtypeStruct((S, ATTN_WIDTH), F32)
    dq, dk, dv = pl.pallas_call(
        body, name=f"attn_bwd_l{layer}", grid=(HEAD_PAIRS, half),
        in_specs=[blk(LANES), full, full, blk(2 * LANES), blk(LANES)],
        out_specs=[blk(LANES), full, full],
        out_shape=[jax.ShapeDtypeStruct((2, S // 2, ATTN_WIDTH), BF16), out, out],
        compiler_params=_params(),
    )(_halves(q), k, v, _halves(saved), _halves(do))
    return dq.reshape(S, ATTN_WIDTH), dk, dv


def _pool_counts(row0, tm):
    pos = row0 + lax.broadcasted_iota(jnp.int32, (tm, 1), 0)
    return [1.0 / jnp.minimum(pos + 1, w).astype(F32) for w in POOL_WINDOWS]


def _window_bands(tm, backward):
    t = np.arange(tm)[:, None]
    c = np.arange(tm)[None, :]
    off = c - t if backward else t - c
    main = np.stack([(off >= 0) & (off < w) for w in POOL_WINDOWS])
    r = np.arange(POOL_HALO)[:, None]
    h = np.arange(POOL_HALO)[None, :]
    off = h - r + POOL_HALO if backward else r - h + POOL_HALO
    edge = np.concatenate([(off < w) for w in POOL_WINDOWS])
    return jnp.asarray(main, BF16), jnp.asarray(edge, BF16)


def _window_sums(tile, beside, main_ref, edge_ref, backward):
    tm = tile.shape[0]
    tb = tile.astype(BF16)
    edge = _nn(edge_ref[...], beside.astype(BF16))
    sums = []
    for g in range(len(POOL_WINDOWS)):
        cols = slice(g * POOL_GROUP, (g + 1) * POOL_GROUP)
        tot = _nn(main_ref[g], tb[:, cols])
        extra = edge[g * POOL_HALO:(g + 1) * POOL_HALO, cols]
        if backward:
            sums.append(jnp.concatenate([tot[:tm - POOL_HALO], tot[tm - POOL_HALO:] + extra], axis=0))
        else:
            sums.append(jnp.concatenate([tot[:POOL_HALO] + extra, tot[POOL_HALO:]], axis=0))
    return sums


def _post_forward(u, history, bands, inv_cnt, zp, o, za, gl, bg, pw_ref, scale, wpu_ref, wau_ref):
    pooled, mixed = [], []
    for g, tot in enumerate(_window_sums(u, history, *bands, False)):
        pg = (tot * inv_cnt[g] - u[:, g * POOL_GROUP:(g + 1) * POOL_GROUP]).astype(BF16)
        pooled.append(pg)
        mixed.append(_nn(pg, pw_ref[g].astype(BF16)))
    pooled = jnp.concatenate(pooled, axis=1)
    mixed = jnp.concatenate(mixed, axis=1)
    sp = _sigmoid(zp)
    sa = _sigmoid(za)
    y_pool = (mixed * scale) * (zp * sp)
    y_attn = o * (za * sa)
    gate = _sigmoid(gl + bg)
    g0, g1 = gate[:, :D_MODEL], gate[:, D_MODEL:]
    up_p = _nn(y_pool.astype(BF16), wpu_ref[...])
    up_a = _nn(y_attn.astype(BF16), wau_ref[...])
    merged = g0 * up_p + g1 * up_a
    return pooled, mixed, sp, sa, y_pool, y_attn, g0, g1, up_p, up_a, merged


def _row_specs(tm, rev, n_tiles):
    tile_of = (lambda i: n_tiles - 1 - i) if rev else (lambda i: i)
    row = lambda n: pl.BlockSpec((tm, n), lambda i: (tile_of(i), 0))
    halo = pl.BlockSpec((POOL_HALO, POOL_WIDTH),
                        lambda i: (jnp.maximum(tile_of(i) * (tm // POOL_HALO) - 1, 0), 0))
    const = lambda shape: pl.BlockSpec(shape, lambda i: (0,) * len(shape))
    return tile_of, row, halo, const


def _layer_weight_spec(rows, cols, layer):
    return pl.BlockSpec((None, rows, cols), lambda i: (layer, 0, 0), pipeline_mode=pl.Buffered(1))


def _post_fwd(x, u, zp, o, za, gl, bg, pw, scale, wpu, wau, wout, layer):
    S = x.shape[0]
    tm = min(ROW_TILE, S)
    n_tiles = S // tm
    tile_of, row, halo, const = _row_specs(tm, False, n_tiles)

    def body(x_ref, u_ref, uh_ref, main_ref, edge_ref, zp_ref, o_ref, za_ref, gl_ref, bg_ref, pw_ref, sc_ref, wpu_ref,
             wau_ref, wout_ref, out_ref):
        i = pl.program_id(0)
        vals = _post_forward(u_ref[...], jnp.where(i == 0, 0.0, uh_ref[...]), (main_ref, edge_ref),
                             _pool_counts(i * tm, tm), zp_ref[...], o_ref[...], za_ref[...], gl_ref[...], bg_ref[...],
                             pw_ref, sc_ref[...], wpu_ref, wau_ref)
        out_ref[...] = x_ref[...] + _nn(vals[-1].astype(BF16), wout_ref[...])

    return pl.pallas_call(
        body, name=f"post_fwd_l{layer}", grid=(n_tiles,),
        in_specs=[row(D_MODEL), row(512), halo, const((4, tm, tm)), const((4 * POOL_HALO, POOL_HALO)), row(512),
                  row(512), row(512), row(2048), const((1, 2048)), const((4, POOL_GROUP, POOL_GROUP)),
                  const((1, POOL_WIDTH)),
                  _layer_weight_spec(POOL_WIDTH, D_MODEL, layer), _layer_weight_spec(ATTN_WIDTH, D_MODEL, layer),
                  _layer_weight_spec(D_MODEL, D_MODEL, layer)],
        out_specs=row(D_MODEL),
        out_shape=jax.ShapeDtypeStruct((S, D_MODEL), F32),
        compiler_params=_params(),
    )(x, u, u, *_window_bands(tm, False), zp, o, za, gl, bg, pw, scale, wpu, wau, wout)


def _post_bwd(dx, u, zp, o, za, gl, bg, pw, scale, wpu, wau, wout, layer):
    S = dx.shape[0]
    tm = min(ROW_TILE, S)
    n_tiles = S // tm
    tile_of, row, halo, const = _row_specs(tm, True, n_tiles)

    def body(dx_ref, u_ref, uh_ref, main_ref, edge_ref, back_main_ref, back_edge_ref, zp_ref, o_ref, za_ref, gl_ref,
             bg_ref, pw_ref, sc_ref, wpu_ref, wau_ref, wout_ref,
             duz_ref, do_ref, dza_ref, dgl_ref, dsc_ref, dbg_ref,
             merged_ref, dup_ref, dua_ref, yp_ref, ya_ref, pooled_ref, dmixed_ref, nxt_ref):
        step = pl.program_id(0)
        i = tile_of(step)

        @pl.when(step == 0)
        def _():
            dsc_ref[...] = jnp.zeros_like(dsc_ref)
            dbg_ref[...] = jnp.zeros_like(dbg_ref)
            nxt_ref[...] = jnp.zeros_like(nxt_ref)

        inv_cnt = _pool_counts(i * tm, tm)
        zp, za, o = zp_ref[...], za_ref[...], o_ref[...]
        pooled, mixed, sp, sa, y_pool, y_attn, g0, g1, up_p, up_a, merged = _post_forward(
            u_ref[...], jnp.where(i == 0, 0.0, uh_ref[...]), (main_ref, edge_ref), inv_cnt, zp, o, za, gl_ref[...],
            bg_ref[...], pw_ref, sc_ref[...], wpu_ref, wau_ref)
        merged_ref[...] = merged.astype(BF16)
        yp_ref[...] = y_pool.astype(BF16)
        ya_ref[...] = y_attn.astype(BF16)
        pooled_ref[...] = pooled

        dmerged = _nt(dx_ref[...].astype(BF16), wout_ref[...])
        dup = (dmerged * g0).astype(BF16)
        dua = (dmerged * g1).astype(BF16)
        dup_ref[...] = dup
        dua_ref[...] = dua
        dgl0 = (dmerged * up_p) * (g0 * (1.0 - g0))
        dgl1 = (dmerged * up_a) * (g1 * (1.0 - g1))
        dgl_ref[:, :D_MODEL] = dgl0.astype(BF16)
        dgl_ref[:, D_MODEL:] = dgl1.astype(BF16)
        dbg_ref[:, :D_MODEL] += jnp.sum(dgl0, axis=0, keepdims=True)
        dbg_ref[:, D_MODEL:] += jnp.sum(dgl1, axis=0, keepdims=True)

        dy_attn = _nt(dua, wau_ref[...])
        do_ref[...] = (dy_attn * (za * sa)).astype(BF16)
        dza_ref[...] = ((dy_attn * o) * (sa * (1.0 + za * (1.0 - sa)))).astype(BF16)

        dy_pool = _nt(dup, wpu_ref[...])
        ms = mixed * sc_ref[...]
        dms = dy_pool * (zp * sp)
        duz_ref[:, POOL_WIDTH:] = ((dy_pool * ms) * (sp * (1.0 + zp * (1.0 - sp)))).astype(BF16)
        dsc_ref[...] += jnp.sum(dms * mixed, axis=0, keepdims=True)
        dmixed = (dms * sc_ref[...]).astype(BF16)
        dmixed_ref[...] = dmixed
        dpooled = [_nt(dmixed[:, g * POOL_GROUP:(g + 1) * POOL_GROUP], pw_ref[g].astype(BF16)) for g in range(4)]
        scaled = jnp.concatenate([d * inv for d, inv in zip(dpooled, inv_cnt)], axis=1)
        for g, tot in enumerate(_window_sums(scaled, nxt_ref[...], back_main_ref, back_edge_ref, True)):
            duz_ref[:, g * POOL_GROUP:(g + 1) * POOL_GROUP] = (tot - dpooled[g]).astype(BF16)
        nxt_ref[...] = scaled[:POOL_HALO]

    sd = lambda n, dt: jax.ShapeDtypeStruct((S, n), dt)
    bands = [const((4, tm, tm)), const((4 * POOL_HALO, POOL_HALO))]
    return pl.pallas_call(
        body, name=f"post_bwd_l{layer}", grid=(n_tiles,),
        in_specs=[row(D_MODEL), row(512), halo, *bands, *bands, row(512), row(512), row(512), row(2048),
                  const((1, 2048)), const((4, POOL_GROUP, POOL_GROUP)), const((1, POOL_WIDTH)),
                  _layer_weight_spec(POOL_WIDTH, D_MODEL, layer), _layer_weight_spec(ATTN_WIDTH, D_MODEL, layer),
                  _layer_weight_spec(D_MODEL, D_MODEL, layer)],
        out_specs=[row(1024), row(512), row(512), row(2048), const((1, POOL_WIDTH)), const((1, 2048)),
                   row(D_MODEL), row(D_MODEL), row(D_MODEL), row(512), row(512), row(512), row(512)],
        out_shape=[sd(1024, BF16), sd(512, BF16), sd(512, BF16), sd(2048, BF16),
                   jax.ShapeDtypeStruct((1, POOL_WIDTH), F32), jax.ShapeDtypeStruct((1, 2048), F32),
                   sd(D_MODEL, BF16), sd(D_MODEL, BF16), sd(D_MODEL, BF16), sd(512, BF16), sd(512, BF16),
                   sd(512, BF16), sd(512, BF16)],
        scratch_shapes=[pltpu.VMEM((POOL_HALO, POOL_WIDTH), F32)],
        compiler_params=_params(),
    )(dx, u, u, *_window_bands(tm, False), *_window_bands(tm, True), zp, o, za, gl, bg, pw, scale, wpu, wau, wout)


def _rms_backward(dh, xv, r, g):
    xhat = xv * r
    dxhat = dh * g
    return r * (dxhat - xhat * jnp.mean(dxhat * xhat, axis=-1, keepdims=True)), dh * xhat


def _loss_head(x, g, target):
    S = x.shape[0]
    tm = min(ROW_TILE, S)

    def body(x_ref, g_ref, t_ref, loss_ref, dx_ref, dg_ref):
        @pl.when(pl.program_id(0) == 0)
        def _():
            loss_ref[...] = jnp.zeros_like(loss_ref)
            dg_ref[...] = jnp.zeros_like(dg_ref)

        xv = x_ref[...]
        r = lax.rsqrt(jnp.mean(xv * xv, axis=-1, keepdims=True) + RMS_EPS)
        diff = (xv * r) * g_ref[...] - t_ref[...]
        per_row = jnp.mean(diff * diff, axis=-1, keepdims=True)
        loss_ref[...] += 0.5 * jnp.sum(per_row, axis=0, keepdims=True)
        dx, dg_rows = _rms_backward(diff * (1.0 / D_MODEL), xv, r, g_ref[...])
        dx_ref[...] = dx
        dg_ref[...] += jnp.sum(dg_rows, axis=0, keepdims=True)

    row = pl.BlockSpec((tm, D_MODEL), lambda i: (i, 0))
    vec = pl.BlockSpec((1, D_MODEL), lambda i: (0, 0))
    return pl.pallas_call(
        body, name="loss_head", grid=(S // tm,),
        in_specs=[row, vec, row],
        out_specs=[pl.BlockSpec((1, LANES), lambda i: (0, 0)), row, vec],
        out_shape=[jax.ShapeDtypeStruct((1, LANES), F32), jax.ShapeDtypeStruct((S, D_MODEL), F32),
                   jax.ShapeDtypeStruct((1, D_MODEL), F32)],
        compiler_params=_params(),
    )(x, g, target)


def _inproj_bwd(pieces, w_in, x, g, dx_res, layer, swap=()):
    S = x.shape[0]
    tm = min(PROJ_ROW_TILE, S)
    n_tiles = S // tm
    cols = [(c0, p.shape[1]) for p, c0 in pieces]
    n_swap = len(swap)

    def body(*refs):
        piece_refs = refs[:len(cols)]
        w_ref, x_ref, g_ref, res_ref = refs[len(cols):len(cols) + 4]
        rest = refs[len(cols) + 4:]
        mine, (dx_ref, dg_ref), theirs = rest[:n_swap], rest[n_swap:n_swap + 2], rest[n_swap + 2:2 * n_swap + 2]
        if n_swap:
            send, recv = rest[2 * n_swap + 2:]
            mx, my, mc, _ = _place()
            copies = [_remote(mine[a].at[1 - mc], theirs[a], send, recv, a, (mx, my, 1 - mc)) for a in range(n_swap)]

        @pl.when(pl.program_id(0) == 0)
        def _():
            dg_ref[...] = jnp.zeros_like(dg_ref)
            for cp in (copies if n_swap else ()):
                cp.start()

        @pl.when(pl.program_id(0) == n_tiles - 1)
        def _():
            for cp in (copies if n_swap else ()):
                cp.wait()

        dh = jnp.zeros((tm, D_MODEL), F32)
        for p_ref, (c0, n) in zip(piece_refs, cols):
            for c in range(0, n, 512):
                dh = dh + _nt(p_ref[:, c:c + 512].astype(BF16), w_ref[:, c0 + c:c0 + c + 512])
        xv = x_ref[...]
        r = lax.rsqrt(jnp.mean(xv * xv, axis=-1, keepdims=True) + RMS_EPS)
        dx, dg_rows = _rms_backward(dh, xv, r, g_ref[...])
        dx_ref[...] = res_ref[...] + dx
        dg_ref[...] += jnp.sum(dg_rows, axis=0, keepdims=True)

    row = lambda n: pl.BlockSpec((tm, n), lambda i: (i, 0))
    vec = pl.BlockSpec((1, D_MODEL), lambda i: (0, 0))
    any_space = pl.BlockSpec(memory_space=pl.ANY)
    res = pl.pallas_call(
        body, name=f"inproj_bwd_l{layer}", grid=(n_tiles,),
        in_specs=[row(n) for _, n in cols] + [_layer_weight_spec(D_MODEL, IN_WIDTH, layer), row(D_MODEL), vec,
                                              row(D_MODEL)] + [any_space] * n_swap,
        out_specs=[row(D_MODEL), vec] + [any_space] * n_swap,
        out_shape=[jax.ShapeDtypeStruct((S, D_MODEL), F32), jax.ShapeDtypeStruct((1, D_MODEL), F32)]
        + [jax.ShapeDtypeStruct(d.shape[1:], d.dtype) for d in swap],
        scratch_shapes=[pltpu.SemaphoreType.DMA((n_swap,)), pltpu.SemaphoreType.DMA((n_swap,))] if n_swap else [],
        compiler_params=_params(),
    )(*[p for p, _ in pieces], w_in, x, g, dx_res, *swap)
    return res[0], res[1], res[2:]


def _wgrad(a, b, name, layer, into=None, col0=0, n_total=None):
    S, M = a.shape
    N = b.shape[1]
    n_total = N if n_total is None else n_total
    tk = min(2048, S)
    tn = min(512, N)
    nk = S // tk

    def body(*refs):
        a_ref, b_ref, out_ref = refs[0], refs[1], refs[-1]
        prod = _tn(a_ref[...].astype(BF16), b_ref[...].astype(BF16))

        @pl.when(pl.program_id(1) == 0)
        def _():
            out_ref[...] = prod

        @pl.when(pl.program_id(1) > 0)
        def _():
            out_ref[...] += prod

    in_specs = [pl.BlockSpec((tk, M), lambda j, k: (k, 0)), pl.BlockSpec((tk, tn), lambda j, k: (k, j))]
    args = [a, b]
    aliases = {}
    if into is not None:
        in_specs.append(pl.BlockSpec(memory_space=pl.ANY))
        args.append(into)
        aliases = {2: 0}
    return pl.pallas_call(
        body, name=name, grid=(N // tn, nk),
        in_specs=in_specs,
        out_specs=pl.BlockSpec((None, M, tn), lambda j, k: (layer, 0, col0 // tn + j)),
        out_shape=jax.ShapeDtypeStruct((2, M, n_total), F32),
        input_output_aliases=aliases,
        compiler_params=_params(),
    )(*args)


def _pool_wgrad(pooled, dmixed, layer):
    S = pooled.shape[0]
    tk = min(1024, S)

    def body(a_ref, b_ref, out_ref):
        prod = _tn(a_ref[...], b_ref[...])

        @pl.when(pl.program_id(1) == 0)
        def _():
            out_ref[...] = prod

        @pl.when(pl.program_id(1) > 0)
        def _():
            out_ref[...] += prod

    blk = pl.BlockSpec((tk, POOL_GROUP), lambda g, k: (k, g))
    return pl.pallas_call(
        body, name=f"pool_wgrad_l{layer}", grid=(4, S // tk),
        in_specs=[blk, blk],
        out_specs=pl.BlockSpec((None, POOL_GROUP, POOL_GROUP), lambda g, k: (g, 0, 0)),
        out_shape=jax.ShapeDtypeStruct((4, POOL_GROUP, POOL_GROUP), F32),
        compiler_params=_params(),
    )(pooled, dmixed)


def _local_step(x, target, norm_g, b_gate, pool_w, pool_scale, final_g, weights):
    n_layers = norm_g.shape[0]
    saved = []
    for l in range(n_layers):
        g = norm_g[l][None]
        bg = b_gate[l][None]
        sc = pool_scale[l][None]
        if l == 0:
            (u, zp, q, k, v, za, gl, h), (w_in, w_pu, w_au, w_out) = _rms_inproj(x, g, None, l, gather=weights)
        else:
            (u, zp, q, k, v, za, gl, h), _ = _rms_inproj(x, g, w_in, l)
        o, carry = _attn_fwd(q, k, v, l)
        saved.append((x, g, bg, sc, u, zp, q, k, v, za, gl, h, o, carry))
        x = _post_fwd(x, u, zp, o, za, gl, bg, pool_w[l], sc, w_pu, w_au, w_out, l)
    loss, dx, d_final_g = _loss_head(x, final_g[None], target)

    small = [None] * n_layers
    dw_in = dw_out = dw_pu = dw_au = None
    for l in reversed(range(n_layers)):
        x_in, g, bg, sc, u, zp, q, k, v, za, gl, h, o, carry = saved[l]
        (duz, do, dza, dgl, dsc, dbg, merged, dup, dua, y_pool, y_attn, pooled, dmixed) = _post_bwd(
            dx, u, zp, o, za, gl, bg, pool_w[l], sc, w_pu, w_au, w_out, l)
        dq, dk, dv = _attn_bwd(q, k, v, carry, do, l)
        pieces = [(duz, C_U), (dq, C_Q), (dk, C_K), (dv, C_V), (dza, C_ZA), (dgl, C_GL)]
        for p, c0 in pieces:
            dw_in = _wgrad(h, p, f"wgrad_in_l{l}_c{c0}", l, into=dw_in, col0=c0, n_total=IN_WIDTH)
        dw_out = _wgrad(merged, dx, f"wgrad_out_l{l}", l, into=dw_out)
        dw_pu = _wgrad(y_pool, dup, f"wgrad_pu_l{l}", l, into=dw_pu)
        dw_au = _wgrad(y_attn, dua, f"wgrad_au_l{l}", l, into=dw_au)
        dpw = _pool_wgrad(pooled, dmixed, l)
        dx, dg, other = _inproj_bwd(pieces, w_in, x_in, g, dx, l, swap=(dw_in, dw_pu, dw_au, dw_out) if l == 0 else ())
        small[l] = (dg[0], dbg[0], dpw, dsc[0])
    small = [jnp.stack([small[l][i] for l in range(n_layers)]) for i in range(4)]
    return loss[0, 0], dx, d_final_g[0], small, (dw_in, dw_pu, dw_au, dw_out), other


SHARDED = ((2, 1280), (2, 256), (2, 256), (1, 256))
ANY = pl.BlockSpec(memory_space=pl.ANY)


def _part(ref, s, axis, width):
    sl = pl.ds(pl.multiple_of(s * width, width), width)
    return ref.at[:, sl] if axis == 2 else ref.at[sl, :]


def _place():
    x, y, c = lax.axis_index("x"), lax.axis_index("y"), lax.axis_index("c")
    return x, y, c, 2 * x + y


def _other_chip(x, y, m):
    px = 1 - x if m & 2 else x
    py = 1 - y if m & 1 else y
    return px, py, 2 * px + py


def _remote(src, dst, send, recv, k, to):
    return pltpu.make_async_remote_copy(src_ref=src, dst_ref=dst, send_sem=send.at[k], recv_sem=recv.at[k],
                                        device_id=to, device_id_type=MESH)


def _part_spec(tr, rows_s, cols_s, axis, width, lead):
    if axis == 2:
        return pl.BlockSpec((None, tr, width), lambda *a: (lead(a), a[-2], a[-1][1]))
    return pl.BlockSpec((None, tr, cols_s), lambda *a: (lead(a), a[-1][1] * (rows_s // tr) + a[-2], 0))


def _cast_into_place(w, pos, axis, width, name):
    L, Rs, Cs = w.shape
    tr = min(256, Rs)
    shape = [L, Rs, Cs]
    shape[axis] *= N_CHIPS

    def body(pos_ref, w_ref, o_ref):
        o_ref[...] = w_ref[...].astype(BF16)

    return pl.pallas_call(
        body, name=name,
        grid_spec=pltpu.PrefetchScalarGridSpec(
            num_scalar_prefetch=1, grid=(L, Rs // tr),
            in_specs=[pl.BlockSpec((None, tr, Cs), lambda l, i, pos: (l, i, 0))],
            out_specs=_part_spec(tr, Rs, Cs, axis, width, lambda a: a[0])),
        out_shape=jax.ShapeDtypeStruct(tuple(shape), BF16),
        compiler_params=_params(),
    )(pos, w)


def _w_in_half(layer):
    def piece(refs, who, shard):
        rows = pl.ds(pl.multiple_of(who * (D_MODEL // 2), D_MODEL // 2), D_MODEL // 2)
        return refs[0].at[layer, rows, pl.ds(pl.multiple_of(shard * SHARDED[0][1], SHARDED[0][1]), SHARDED[0][1])]
    return piece


def _whole_layer(a):
    def piece(refs, who, shard):
        return _part(refs[a].at[who], shard, *SHARDED[a])
    return piece


FIRST_PIECES = (_w_in_half(0),)
LATER_PIECES = (_w_in_half(1), _whole_layer(1), _whole_layer(2), _whole_layer(3))


class _Gather:
    def __init__(self, pieces, refs, send, recv):
        self.pieces, self.refs, self.send, self.recv = pieces, refs, send, recv
        self.x, self.y, self.c, s = _place()
        self.first = []
        for u, piece in enumerate(pieces):
            for m in (1, 2, 3):
                px, py, _ = _other_chip(self.x, self.y, m)
                own = piece(refs, self.c, s)
                self.first.append(_remote(own, own, send, recv, 3 * u + m - 1, (px, py, self.c)))

    def start(self):
        for cp in self.first:
            cp.start()

    def finish(self):
        x, y, c, n = self.x, self.y, self.c, len(self.pieces)

        def landed(u, m, who):
            _, _, sp = _other_chip(x, y, m)
            return self.pieces[u](self.refs, who, sp)

        passed = []
        for m in (1, 2, 3):
            for u in range(n):
                got = landed(u, m, c)
                _remote(got, got, self.send, self.recv, 3 * u + m - 1, (x, y, c)).wait_recv()
                passed.append(_remote(got, got, self.send, self.recv, 3 * n + 3 * u + m - 1, (x, y, 1 - c)))
                passed[-1].start()
        for m in (1, 2, 3):
            for u in range(n):
                got = landed(u, m, 1 - c)
                _remote(got, got, self.send, self.recv, 3 * n + 3 * u + m - 1, (x, y, c)).wait_recv()
        for cp in self.first + passed:
            cp.wait_send()

    @staticmethod
    def semaphores(pieces):
        return [pltpu.SemaphoreType.DMA((6 * len(pieces),)), pltpu.SemaphoreType.DMA((6 * len(pieces),))]


def _gather_first(fulls):
    n = len(fulls)

    def body(*refs):
        gather = _Gather(FIRST_PIECES, refs[n:2 * n], *refs[2 * n:])
        gather.start()
        gather.finish()

    return pl.pallas_call(
        body, name="gather_first",
        in_specs=[ANY] * n, out_specs=[ANY] * n,
        out_shape=[jax.ShapeDtypeStruct(f.shape, f.dtype) for f in fulls],
        input_output_aliases={a: a for a in range(n)},
        scratch_shapes=_Gather.semaphores(FIRST_PIECES),
    )(*fulls)


def _pair_sum(dw, other, pos, name):
    _, R, C = dw.shape
    tr = 128 if C > 1024 else 256

    def body(pos_ref, a_ref, b_ref, o_ref, ob_ref):
        tot = a_ref[...] + b_ref[...]
        o_ref[...] = tot
        ob_ref[...] = tot.astype(BF16)

    blk = pl.BlockSpec((tr, C), lambda i, pos: (i, 0))
    return pl.pallas_call(
        body, name=name,
        grid_spec=pltpu.PrefetchScalarGridSpec(
            num_scalar_prefetch=1, grid=(R // tr,),
            in_specs=[pl.BlockSpec((None, tr, C), lambda i, pos: (pos[0], i, 0)), blk],
            out_specs=[blk, blk]),
        out_shape=[jax.ShapeDtypeStruct((R, C), F32), jax.ShapeDtypeStruct((R, C), BF16)],
        compiler_params=_params(),
    )(pos, dw, other)


def _chip_exchange(ps, packed):
    n = len(ps)

    def body(*refs):
        ins, small_ref = refs[:n], refs[n]
        outs, total_ref = refs[n + 1:2 * n + 1], refs[2 * n + 1]
        all_ref, send, recv, small_send, small_recv = refs[2 * n + 2:]
        x, y, c, _ = _place()
        my_id = 4 * x + 2 * y + c
        all_ref[my_id] = small_ref[...]
        small = []
        for m in range(1, N_DEVICES):
            px = 1 - x if m & 4 else x
            py = 1 - y if m & 2 else y
            pc = 1 - c if m & 1 else c
            cp = _remote(small_ref, all_ref.at[my_id], small_send, small_recv, m - 1, (px, py, pc))
            cp.start()
            small.append((cp, 4 * px + 2 * py + pc))
        copies = []
        for a, (axis, width) in enumerate(SHARDED):
            for m in (1, 2, 3):
                px, py, sp = _other_chip(x, y, m)
                copies.append(_remote(_part(ins[a], sp, axis, width), outs[a].at[m - 1], send, recv, 3 * a + m - 1,
                                      (px, py, c)))
        for cp in copies:
            cp.start()
        for m, (cp, peer_id) in enumerate(small):
            _remote(small_ref, all_ref.at[peer_id], small_send, small_recv, m, (x, y, c)).wait_recv()
        total = all_ref[0]
        for d in range(1, N_DEVICES):
            total = total + all_ref[d]
        total_ref[...] = total
        for cp, _ in small:
            cp.wait_send()
        for cp in copies:
            cp.wait()

    def landing(p, axis, width):
        shape = [3] + list(p.shape)
        shape[axis] = width
        return jax.ShapeDtypeStruct(tuple(shape), p.dtype)

    vmem = pl.BlockSpec(memory_space=pltpu.VMEM)
    res = pl.pallas_call(
        body, name="grad_chip_exchange",
        in_specs=[ANY] * n + [vmem], out_specs=[ANY] * n + [vmem],
        out_shape=[landing(p, axis, width) for p, (axis, width) in zip(ps, SHARDED)]
        + [jax.ShapeDtypeStruct(packed.shape, packed.dtype)],
        scratch_shapes=[pltpu.VMEM((N_DEVICES,) + packed.shape, F32),
                        pltpu.SemaphoreType.DMA((3 * n,)), pltpu.SemaphoreType.DMA((3 * n,)),
                        pltpu.SemaphoreType.DMA((N_DEVICES - 1,)), pltpu.SemaphoreType.DMA((N_DEVICES - 1,))],
        compiler_params=_params(),
    )(*ps, packed)
    return res[:n], res[n]


def _shard_sum(p, landed, pos, axis, width, name):
    _, Rs, Cs = landed.shape
    tr = min(256, Rs)
    p_spec = _part_spec(tr, Rs, Cs, axis, width, lambda a: 0)

    def body(pos_ref, p_ref, l_ref, o_ref):
        o_ref[...] = ((p_ref[...] + l_ref[0].astype(F32)) + l_ref[1].astype(F32)) + l_ref[2].astype(F32)

    return pl.pallas_call(
        body, name=name,
        grid_spec=pltpu.PrefetchScalarGridSpec(
            num_scalar_prefetch=1, grid=(Rs // tr,),
            in_specs=[p_spec, pl.BlockSpec((3, tr, Cs), lambda i, pos: (0, i, 0))],
            out_specs=pl.BlockSpec((None, tr, Cs), lambda i, pos: (pos[0], i, 0))),
        out_shape=jax.ShapeDtypeStruct((2, Rs, Cs), F32),
        compiler_params=_params(),
    )(pos, p[None], landed)


def _sibling_share(gs):
    n = len(gs)

    def body(*refs):
        outs = refs[n:2 * n]
        send, recv = refs[2 * n:]
        x, y, c, _ = _place()
        copies = [_remote(outs[a].at[c], outs[a].at[c], send, recv, a, (x, y, 1 - c)) for a in range(n)]
        for cp in copies:
            cp.start()
        for a, cp in enumerate(copies):
            cp.wait_send()
            _remote(outs[a].at[1 - c], outs[a].at[1 - c], send, recv, a, (x, y, c)).wait_recv()

    return pl.pallas_call(
        body, name="grad_sibling_share",
        in_specs=[ANY] * n, out_specs=[ANY] * n,
        out_shape=[jax.ShapeDtypeStruct(g.shape, g.dtype) for g in gs],
        input_output_aliases={a: a for a in range(n)},
        scratch_shapes=[pltpu.SemaphoreType.DMA((n,)), pltpu.SemaphoreType.DMA((n,))],
    )(*gs)


def _adamw(w, g, m, v, name):
    shape = w.shape
    C = shape[-1]
    flat = [t.reshape(-1, C) for t in (w, g, m, v)]
    R = flat[0].shape[0]
    tr = max(t for t in range(8, R + 1, 8) if R % t == 0 and t * C <= 384 * 1024)

    def body(w_ref, g_ref, m_ref, v_ref, d_ref, nm_ref, nv_ref):
        gv = g_ref[...]
        nm = ADAM_B1 * m_ref[...] + (1.0 - ADAM_B1) * gv
        nv = ADAM_B2 * v_ref[...] + (1.0 - ADAM_B2) * (gv * gv)
        m_hat = nm / (1.0 - ADAM_B1 ** ADAM_STEP)
        v_hat = nv / (1.0 - ADAM_B2 ** ADAM_STEP)
        d_ref[...] = -ADAM_LR * (m_hat / (jnp.sqrt(v_hat) + ADAM_EPS) + ADAM_WD * w_ref[...])
        nm_ref[...] = nm
        nv_ref[...] = nv

    blk = pl.BlockSpec((tr, C), lambda i: (i, 0))
    out = jax.ShapeDtypeStruct((R, C), F32)
    res = pl.pallas_call(
        body, name=name, grid=(R // tr,),
        in_specs=[blk] * 4, out_specs=[blk] * 3, out_shape=[out] * 3,
        compiler_params=_params(),
    )(*flat)
    return [t.reshape(shape) for t in res]


SMALL_SHAPES = ((2, 1024), (2, 2048), (2, 4, 128, 128), (2, 512), (1024,))


def _pack_small(parts):
    return jnp.concatenate([p.reshape(-1, LANES) for p in parts], axis=0)


def _unpack_small(packed):
    out, row = [], 0
    for shape in SMALL_SHAPES:
        n = 1
        for d in shape:
            n *= d
        out.append(packed[row:row + n // LANES].reshape(shape))
        row += n // LANES
    return out


def kernel(x, norm_g, w_in, b_gate, pool_w, pool_scale, w_pool_up, w_attn_up, w_out, final_g, loss_target, m_norm_g, m_w_in, m_b_gate, m_pool_w, m_pool_scale, m_w_pool_up, m_w_attn_up, m_w_out, m_final_g, v_norm_g, v_w_in, v_b_gate, v_pool_w, v_pool_scale, v_w_pool_up, v_w_attn_up, v_w_out, v_final_g):
    _, _, c, s = _place()
    pos = jnp.stack([c, s]).astype(jnp.int32)
    names = ("w_in", "w_pool_up", "w_attn_up", "w_out")

    weights = _gather_first([_cast_into_place(w, pos, axis, width, f"cast_{n}")
                             for w, (axis, width), n in zip((w_in, w_pool_up, w_attn_up, w_out), SHARDED, names)])
    loss_part, dx, d_final_g, small, dws, other = _local_step(x[0], loss_target[0], norm_g, b_gate, pool_w,
                                                              pool_scale, final_g, weights)
    pair =[_pair_sum(d, o, pos, f"grad_pair_sum_{n}") for d, o, n in zip(dws, other, names)]
    landed, summed = _chip_exchange([pb for _, pb in pair],
                                    _pack_small(small + [d_final_g, jnp.broadcast_to(loss_part, (8, LANES))]))
    mine = [_shard_sum(p, l, pos, axis, width, f"grad_shard_sum_{n}")
            for (p, _), l, (axis, width), n in zip(pair, landed, SHARDED, names)]
    g_in, g_pu, g_au, g_out = _sibling_share(mine)
    g_small = _unpack_small(summed)
    loss = summed[-8, 0]
    upd_small = _adamw(_pack_small([norm_g, b_gate, pool_w, pool_scale, final_g]), _pack_small(g_small),
                       _pack_small([m_norm_g, m_b_gate, m_pool_w, m_pool_scale, m_final_g]),
                       _pack_small([v_norm_g, v_b_gate, v_pool_w, v_pool_scale, v_final_g]), "adamw_small")
    d_small, nm_small, nv_small = [_unpack_small(t) for t in upd_small]
    upd_in = _adamw(w_in, g_in, m_w_in, v_w_in, "adamw_w_in")
    upd_pu = _adamw(w_pool_up, g_pu, m_w_pool_up, v_w_pool_up, "adamw_w_pool_up")
    upd_au = _adamw(w_attn_up, g_au, m_w_attn_up, v_w_attn_up, "adamw_w_attn_up")
    upd_out = _adamw(w_out, g_out, m_w_out, v_w_out, "adamw_w_out")

    def ordered(sm, k):
        big = (upd_in[k], upd_pu[k], upd_au[k], upd_out[k]) if k is not None else (g_in, g_pu, g_au, g_out)
        return [sm[0], big[0], sm[1], sm[2], sm[3], big[1], big[2], big[3], sm[4]]

    return (loss, dx[None], *ordered(g_small, None), *ordered(d_small, 0), *ordered(nm_small, 1),
            *ordered(nv_small, 2))
```

```python
import jax
import jax.numpy as jnp
import numpy as np
from jax import lax
from jax.experimental import pallas as pl
from jax.experimental.pallas import tpu as pltpu

F32 = jnp.float32
BF16 = jnp.bfloat16
MESH = pl.DeviceIdType.MESH

D_MODEL = 1024
POOL_WIDTH = 512
POOL_WINDOWS = (2, 4, 8, 16)
POOL_GROUP = 128
POOL_HALO = 16
ATTN_WIDTH = 512
HEAD_DIM = 64
HEAD_PAIRS = 4
IN_WIDTH = 5120
N_CHIPS = 4
N_DEVICES = 8
RMS_EPS = 1e-6
C_U, C_ZP, C_Q, C_K, C_V, C_ZA, C_GL = 0, 512, 1024, 1536, 2048, 2560, 3072

ADAM_LR, ADAM_B1, ADAM_B2, ADAM_EPS, ADAM_WD, ADAM_STEP = 0.001, 0.9, 0.999, 1e-08, 0.01, 10

LANES = 128
ATTN_BLOCK = 256
ROW_TILE = 256
PROJ_ROW_TILE = 512
VMEM_LIMIT = 56 * 1024 * 1024


def _params(**kw):
    return pltpu.CompilerParams(vmem_limit_bytes=VMEM_LIMIT, **kw)


def _nt(a, b):
    return lax.dot_general(a, b, (((1,), (1,)), ((), ())), preferred_element_type=F32)


def _tn(a, b):
    return lax.dot_general(a, b, (((0,), (0,)), ((), ())), preferred_element_type=F32)


def _nn(a, b):
    return jnp.dot(a, b, preferred_element_type=F32)


def _sigmoid(z):
    return 1.0 / (1.0 + jnp.exp(-z))


def _rms_inproj(x, g, w_in, layer, gather=()):
    S = x.shape[0]
    tm = min(PROJ_ROW_TILE, S)
    n_tiles = S // tm
    n_g = len(gather)

    def body(*refs):
        x_ref, g_ref = refs[:2]
        if n_g:
            (u_ref, zp_ref, q_ref, k_ref, v_ref, za_ref, gl_ref, h_ref) = refs[2 + n_g:10 + n_g]
            fulls = refs[10 + n_g:10 + 2 * n_g]
            w_ref, load_sem, send, recv = refs[10 + 2 * n_g:]
            later = _Gather(LATER_PIECES, fulls, send, recv)

            @pl.when(pl.program_id(0) == 0)
            def _():
                load = pltpu.make_async_copy(fulls[0].at[layer], w_ref, load_sem)
                load.start()
                later.start()
                load.wait()

            @pl.when(pl.program_id(0) == n_tiles - 1)
            def _():
                later.finish()
        else:
            w_ref, u_ref, zp_ref, q_ref, k_ref, v_ref, za_ref, gl_ref, h_ref = refs[2:]
        xv = x_ref[...]
        r = lax.rsqrt(jnp.mean(xv * xv, axis=-1, keepdims=True) + RMS_EPS)
        h = ((xv * r) * g_ref[...]).astype(BF16)
        h_ref[...] = h

        def mm(c0, n):
            return _nn(h, w_ref[:, c0:c0 + n])

        u_ref[...] = mm(C_U, 512)
        zp_ref[...] = mm(C_ZP, 512).astype(BF16)
        q_ref[...] = (mm(C_Q, 512) * 0.125).astype(BF16)
        k_ref[...] = mm(C_K, 512).astype(BF16)
        v_ref[...] = mm(C_V, 512).astype(BF16)
        za_ref[...] = mm(C_ZA, 512).astype(BF16)
        for c in range(4):
            gl_ref[:, c * 512:(c + 1) * 512] = mm(C_GL + c * 512, 512).astype(BF16)

    row = lambda n: pl.BlockSpec((tm, n), lambda i: (i, 0))
    sd = lambda n, dt: jax.ShapeDtypeStruct((S, n), dt)
    any_space = pl.BlockSpec(memory_space=pl.ANY)
    weights = [any_space] * n_g if n_g else [_layer_weight_spec(D_MODEL, IN_WIDTH, layer)]
    res = pl.pallas_call(
        body, name=f"rms_inproj_l{layer}", grid=(n_tiles,),
        in_specs=[row(D_MODEL), pl.BlockSpec((1, D_MODEL), lambda i: (0, 0))] + weights,
        out_specs=[row(512), row(512), row(512), row(512), row(512), row(512), row(2048), row(D_MODEL)]
        + [any_space] * n_g,
        out_shape=[sd(512, F32), sd(512, BF16), sd(512, BF16), sd(512, BF16), sd(512, BF16), sd(512, BF16),
                   sd(2048, BF16), sd(D_MODEL, BF16)] + [jax.ShapeDtypeStruct(f.shape, f.dtype) for f in gather],
        input_output_aliases={2 + a: 8 + a for a in range(n_g)},
        scratch_shapes=([pltpu.VMEM((D_MODEL, IN_WIDTH), BF16), pltpu.SemaphoreType.DMA(())]
                        + _Gather.semaphores(LATER_PIECES)) if n_g else [],
        compiler_params=_params(),
    )(x, g, *(gather if n_g else (w_in,)))
    return res[:8], res[8:]


def _tri(n, strict_lower):
    r = lax.broadcasted_iota(jnp.int32, (n, n), 0)
    c = lax.broadcasted_iota(jnp.int32, (n, n), 1)
    return jnp.where(r > c if strict_lower else r < c, 1.0, 0.0).astype(BF16)


def _split_dot(x, m):
    hi = x.astype(BF16)
    lo = (x - hi.astype(F32)).astype(BF16)
    return _nn(hi, m) + _nn(lo, m)


def _log_terms(z):
    lg = jnp.log(1.0 + jnp.exp(-jnp.abs(z)))
    a = jnp.minimum(z, 0.0) - lg
    return a, a - z


EXHAUSTED = -104.0
UNREACHED = -1e30


class _HeadPair:
    def __init__(self, T):
        self.T = T
        self.first = lax.broadcasted_iota(jnp.int32, (T, LANES), 1) < HEAD_DIM
        self.lane = lax.broadcasted_iota(jnp.int32, (2 * T, LANES), 1)
        row = lax.broadcasted_iota(jnp.int32, (2 * T, T), 0)
        row = jnp.where(row >= T, row - T, row)
        self.causal = row > lax.broadcasted_iota(jnp.int32, (2 * T, T), 1)
        self.below = _tri(T, True)

    def stack(self, x2):
        return jnp.concatenate([jnp.where(self.first, x2, 0), jnp.where(self.first, 0, x2)], axis=0).astype(BF16)

    def unstack(self, x):
        return jnp.where(self.first, x[:self.T], x[self.T:])

    def keys(self, ref, blocks):
        T = self.T
        return jnp.concatenate([ref[pl.ds(pl.multiple_of(j * T, T), T), :] for j, _ in blocks], axis=0)

    def log_terms(self, z, blocks):
        T = self.T
        a_all, l_all = _log_terms(z)
        a = [a_all[:, b * T:(b + 1) * T] for b in range(len(blocks))]
        l1m = [l_all[:, b * T:(b + 1) * T] for b in range(len(blocks))]
        return a, [jnp.where(self.causal, l, 0.0) if diagonal else l for l, (_, diagonal) in zip(l1m, blocks)]

    def later_sums(self, l1m):
        later = _split_dot(jnp.concatenate(l1m, axis=0), self.below)
        return [later[2 * self.T * b:2 * self.T * (b + 1)] for b in range(len(l1m))]


def _halves(x):
    return x.reshape(2, x.shape[0] // 2, x.shape[1])


def _attn_fwd(q, k, v, layer):
    S = q.shape[0]
    T = min(ATTN_BLOCK, S)
    nq = S // T
    assert nq <= LANES and nq % 2 == 0
    half = nq // 2

    def body(q_ref, k_ref, v_ref, o_ref, c_ref):
        i = pl.program_id(1)
        pair = _HeadPair(T)
        qs = [pair.stack(q_ref[0]), pair.stack(q_ref[1])]
        diag = [i, i + half]

        def sweep(jobs):
            kv = [(pair.keys(k_ref, bl), pair.keys(v_ref, bl)) for _, bl, _ in jobs]
            zs = [_nt(qs[n], kcat) for (n, _, _), (kcat, _) in zip(jobs, kv)]
            terms = [pair.log_terms(z, bl) for (_, bl, _), z in zip(jobs, zs)]
            laters = [pair.later_sums(l1m) for _, l1m in terms]
            weights = []
            for (_, bl, (acc, run, saved)), (a, l1m), later in zip(jobs, terms, laters):
                ws = []
                for b, (j, diagonal) in enumerate(bl):
                    saved = jnp.where(pair.lane == j, run, saved)
                    w = jnp.exp(a[b] + later[b] + run)
                    ws.append(jnp.where(pair.causal, w, 0.0) if diagonal else w)
                    run = run + jnp.sum(l1m[b], axis=1, keepdims=True)
                weights.append((jnp.concatenate(ws, axis=1).astype(BF16), acc, run, saved))
            return [(acc + _nn(w, vcat), run, saved) for (w, acc, run, saved), (_, vcat) in zip(weights, kv)]

        def alive(carry):
            return (jnp.max(carry[1]) > EXHAUSTED).astype(jnp.int32)

        def older_blocks(n, carry):
            def older_block(state):
                j, _, c = state
                c = sweep([(n, [(j, False)], c)])[0]
                return j - 1, alive(c), c

            return lax.while_loop(lambda st: jnp.logical_and(st[0] >= 0, st[1] > 0), older_block,
                                  (diag[n] - 2, alive(carry), carry))[2]

        def run(first_blocks):
            init = (jnp.zeros((2 * T, LANES), F32), jnp.zeros((2 * T, 1), F32),
                    jnp.full((2 * T, LANES), UNREACHED, F32))
            carries = sweep([(n, first_blocks[n], init) for n in range(2)])
            for n in range(2):
                acc, _, saved = older_blocks(n, carries[n])
                o_ref[n] = pair.unstack(acc).astype(BF16)
                c_ref[n, :, :LANES] = saved[:T]
                c_ref[n, :, LANES:] = saved[T:]

        with_previous = lambda d: [(d, True), (d - 1, False)]

        @pl.when(i == 0)
        def _():
            run([[(diag[0], True)], with_previous(diag[1])])

        @pl.when(i > 0)
        def _():
            run([with_previous(diag[0]), with_previous(diag[1])])

    blk = lambda n: pl.BlockSpec((2, T, n), lambda p, i: (0, i, p))
    full = pl.BlockSpec((S, LANES), lambda p, i: (0, p))
    o, carry = pl.pallas_call(
        body, name=f"attn_fwd_l{layer}", grid=(HEAD_PAIRS, half),
        in_specs=[blk(LANES), full, full],
        out_specs=[blk(LANES), blk(2 * LANES)],
        out_shape=[jax.ShapeDtypeStruct((2, S // 2, ATTN_WIDTH), BF16),
                   jax.ShapeDtypeStruct((2, S // 2, 8 * LANES), F32)],
        compiler_params=_params(),
    )(_halves(q), k, v)
    return o.reshape(S, ATTN_WIDTH), carry.reshape(S, 8 * LANES)


def _attn_bwd(q, k, v, saved, do, layer):
    S = q.shape[0]
    T = min(ATTN_BLOCK, S)
    nq = S // T

    half = nq // 2

    def body(q_ref, k_ref, v_ref, c_ref, do_ref, dq_ref, dk_ref, dv_ref):
        i = pl.program_id(1)

        @pl.when(i == 0)
        def _():
            dk_ref[...] = jnp.zeros_like(dk_ref)
            dv_ref[...] = jnp.zeros_like(dv_ref)

        pair = _HeadPair(T)
        diag = [i, i + half]
        qs = [pair.stack(q_ref[n]) for n in range(2)]
        dos = [pair.stack(do_ref[n].astype(BF16)) for n in range(2)]
        saved = [jnp.concatenate([c_ref[n, :, :LANES], c_ref[n, :, LANES:]], axis=0) for n in range(2)]
        before = _tri(T, False)

        def sweep(jobs):
            kv = [(pair.keys(k_ref, bl), pair.keys(v_ref, bl)) for _, bl, _ in jobs]
            zs = [_nt(qs[n], kcat) for (n, _, _), (kcat, _) in zip(jobs, kv)]
            gs = [_nt(dos[n], vcat) for (n, _, _), (_, vcat) in zip(jobs, kv)]
            terms = [pair.log_terms(z, bl) for (_, bl, _), z in zip(jobs, zs)]
            laters = [pair.later_sums(l1m) for _, l1m in terms]
            ws, es = [], []
            for (n, bl, _), (a, _), later, g in zip(jobs, terms, laters, gs):
                w_job, e_job = [], []
                for b, (j, diagonal) in enumerate(bl):
                    run = jnp.sum(jnp.where(pair.lane == j, saved[n], 0.0), axis=1, keepdims=True)
                    w = jnp.exp(a[b] + later[b] + run)
                    w_job.append(jnp.where(pair.causal, w, 0.0) if diagonal else w)
                    e_job.append(w_job[b] * g[:, b * T:(b + 1) * T])
                ws.append(w_job)
                es.append(e_job)
            prefixes = [_nn(jnp.concatenate(e_job, axis=0).astype(BF16), before) for e_job in es]
            dzs, olders = [], []
            for (_, bl, (_, older)), (a, _), e_job, prefix in zip(jobs, terms, es, prefixes):
                dz_job = []
                for b, (j, diagonal) in enumerate(bl):
                    dz = e_job[b] - jnp.exp(a[b]) * (e_job[b] + (prefix[2 * T * b:2 * T * (b + 1)] + older))
                    dz_job.append(jnp.where(pair.causal, dz, 0.0) if diagonal else dz)
                    older = older + jnp.sum(e_job[b], axis=1, keepdims=True)
                dzs.append(jnp.concatenate(dz_job, axis=1).astype(BF16))
                olders.append(older)
            out = []
            for (n, bl, (dq, _)), dz, w_job, older, (kcat, _) in zip(jobs, dzs, ws, olders, kv):
                dk = _tn(dz, qs[n])
                dv = _tn(jnp.concatenate(w_job, axis=1).astype(BF16), dos[n])
                for b, (j, _) in enumerate(bl):
                    rows = pl.ds(pl.multiple_of(j * T, T), T)
                    dk_ref[rows, :] += dk[b * T:(b + 1) * T]
                    dv_ref[rows, :] += dv[b * T:(b + 1) * T]
                out.append((dq + _nn(dz, kcat), older))
            return out

        def older_blocks(n):
            col_max = jnp.max(saved[n], axis=0, keepdims=True)
            lane_row = lax.broadcasted_iota(jnp.int32, (1, LANES), 1)
            reached = jnp.sum(jnp.where(jnp.logical_and(col_max > EXHAUSTED, lane_row < diag[n]), 1, 0))
            init = (jnp.zeros((2 * T, LANES), F32), jnp.zeros((2 * T, 1), F32))
            return lax.fori_loop(diag[n] - reached, diag[n] - 1, lambda j, c: sweep([(n, [(j, False)], c)])[0], init)

        def run(last_blocks):
            carries = sweep([(n, last_blocks[n], older_blocks(n)) for n in range(2)])
            for n in range(2):
                dq_ref[n] = (pair.unstack(carries[n][0]) * 0.125).astype(BF16)

        with_previous = lambda d: [(d - 1, False), (d, True)]

        @pl.when(i == 0)
        def _():
            run([[(diag[0], True)], with_previous(diag[1])])

        @pl.when(i > 0)
        def _():
            run([with_previous(diag[0]), with_previous(diag[1])])

    blk = lambda n: pl.BlockSpec((2, T, n), lambda p, i: (0, i, p))
    full = pl.BlockSpec((S, LANES), lambda p, i: (0, p))
    out = jax.ShapeDtypeStruct((S, ATTN_WIDTH), F32)
    dq, dk, dv = pl.pallas_call(
        body, name=f"attn_bwd_l{layer}", grid=(HEAD_PAIRS, half),
        in_specs=[blk(LANES), full, full, blk(2 * LANES), blk(LANES)],
        out_specs=[blk(LANES), full, full],
        out_shape=[jax.ShapeDtypeStruct((2, S // 2, ATTN_WIDTH), BF16), out, out],
        compiler_params=_params(),
    )(_halves(q), k, v, _halves(saved), _halves(do))
    return dq.reshape(S, ATTN_WIDTH), dk, dv


def _pool_counts(row0, tm):
    pos = row0 + lax.broadcasted_iota(jnp.int32, (tm, 1), 0)
    return [1.0 / jnp.minimum(pos + 1, w).astype(F32) for w in POOL_WINDOWS]


def _window_bands(tm, backward):
    t = np.arange(tm)[:, None]
    c = np.arange(tm)[None, :]
    off = c - t if backward else t - c
    main = np.stack([(off >= 0) & (off < w) for w in POOL_WINDOWS])
    r = np.arange(POOL_HALO)[:, None]
    h = np.arange(POOL_HALO)[None, :]
    off = h - r + POOL_HALO if backward else r - h + POOL_HALO
    edge = np.concatenate([(off < w) for w in POOL_WINDOWS])
    return jnp.asarray(main, BF16), jnp.asarray(edge, BF16)


def _window_sums(tile, beside, main_ref, edge_ref, backward):
    tm = tile.shape[0]
    tb = tile.astype(BF16)
    edge = _nn(edge_ref[...], beside.astype(BF16))
    sums = []
    for g in range(len(POOL_WINDOWS)):
        cols = slice(g * POOL_GROUP, (g + 1) * POOL_GROUP)
        tot = _nn(main_ref[g], tb[:, cols])
        extra = edge[g * POOL_HALO:(g + 1) * POOL_HALO, cols]
        if backward:
            sums.append(jnp.concatenate([tot[:tm - POOL_HALO], tot[tm - POOL_HALO:] + extra], axis=0))
        else:
            sums.append(jnp.concatenate([tot[:POOL_HALO] + extra, tot[POOL_HALO:]], axis=0))
    return sums


def _post_forward(u, history, bands, inv_cnt, zp, o, za, gl, bg, pw_ref, scale, wpu_ref, wau_ref):
    pooled, mixed = [], []
    for g, tot in enumerate(_window_sums(u, history, *bands, False)):
        pg = (tot * inv_cnt[g] - u[:, g * POOL_GROUP:(g + 1) * POOL_GROUP]).astype(BF16)
        pooled.append(pg)
        mixed.append(_nn(pg, pw_ref[g].astype(BF16)))
    pooled = jnp.concatenate(pooled, axis=1)
    mixed = jnp.concatenate(mixed, axis=1)
    zp, za, o = zp.astype(F32), za.astype(F32), o.astype(F32)
    sp = _sigmoid(zp)
    sa = _sigmoid(za)
    y_pool = (mixed * scale) * (zp * sp)
    y_attn = o * (za * sa)
    gate = _sigmoid(gl + bg)
    g0, g1 = gate[:, :D_MODEL], gate[:, D_MODEL:]
    up_p = _nn(y_pool.astype(BF16), wpu_ref[...])
    up_a = _nn(y_attn.astype(BF16), wau_ref[...])
    merged = g0 * up_p + g1 * up_a
    return pooled, mixed, sp, sa, y_pool, y_attn, g0, g1, up_p, up_a, merged


def _row_specs(tm, rev, n_tiles):
    tile_of = (lambda i: n_tiles - 1 - i) if rev else (lambda i: i)
    row = lambda n: pl.BlockSpec((tm, n), lambda i: (tile_of(i), 0))
    halo = pl.BlockSpec((POOL_HALO, POOL_WIDTH),
                        lambda i: (jnp.maximum(tile_of(i) * (tm // POOL_HALO) - 1, 0), 0))
    const = lambda shape: pl.BlockSpec(shape, lambda i: (0,) * len(shape))
    return tile_of, row, halo, const


def _layer_weight_spec(rows, cols, layer):
    return pl.BlockSpec((None, rows, cols), lambda i: (layer, 0, 0), pipeline_mode=pl.Buffered(1))


def _post_fwd(x, u, zp, o, za, gl, bg, pw, scale, wpu, wau, wout, layer):
    S = x.shape[0]
    tm = min(ROW_TILE, S)
    n_tiles = S // tm
    tile_of, row, halo, const = _row_specs(tm, False, n_tiles)

    def body(x_ref, u_ref, uh_ref, main_ref, edge_ref, zp_ref, o_ref, za_ref, gl_ref, bg_ref, pw_ref, sc_ref, wpu_ref,
             wau_ref, wout_ref, out_ref):
        i = pl.program_id(0)
        vals = _post_forward(u_ref[...], jnp.where(i == 0, 0.0, uh_ref[...]), (main_ref, edge_ref),
                             _pool_counts(i * tm, tm), zp_ref[...], o_ref[...], za_ref[...], gl_ref[...], bg_ref[...],
                             pw_ref, sc_ref[...], wpu_ref, wau_ref)
        out_ref[...] = x_ref[...] + _nn(vals[-1].astype(BF16), wout_ref[...])

    return pl.pallas_call(
        body, name=f"post_fwd_l{layer}", grid=(n_tiles,),
        in_specs=[row(D_MODEL), row(512), halo, const((4, tm, tm)), const((4 * POOL_HALO, POOL_HALO)), row(512),
                  row(512), row(512), row(2048), const((1, 2048)), const((4, POOL_GROUP, POOL_GROUP)),
                  const((1, POOL_WIDTH)),
                  _layer_weight_spec(POOL_WIDTH, D_MODEL, layer), _layer_weight_spec(ATTN_WIDTH, D_MODEL, layer),
                  _layer_weight_spec(D_MODEL, D_MODEL, layer)],
        out_specs=row(D_MODEL),
        out_shape=jax.ShapeDtypeStruct((S, D_MODEL), F32),
        compiler_params=_params(),
    )(x, u, u, *_window_bands(tm, False), zp, o, za, gl, bg, pw, scale, wpu, wau, wout)


def _post_bwd(dx, u, zp, o, za, gl, bg, pw, scale, wpu, wau, wout, layer):
    S = dx.shape[0]
    tm = min(ROW_TILE, S)
    n_tiles = S // tm
    tile_of, row, halo, const = _row_specs(tm, True, n_tiles)

    def body(dx_ref, u_ref, uh_ref, main_ref, edge_ref, back_main_ref, back_edge_ref, zp_ref, o_ref, za_ref, gl_ref,
             bg_ref, pw_ref, sc_ref, wpu_ref, wau_ref, wout_ref,
             duz_ref, do_ref, dza_ref, dgl_ref, dsc_ref, dbg_ref,
             merged_ref, dup_ref, dua_ref, yp_ref, ya_ref, pooled_ref, dmixed_ref, nxt_ref):
        step = pl.program_id(0)
        i = tile_of(step)

        @pl.when(step == 0)
        def _():
            dsc_ref[...] = jnp.zeros_like(dsc_ref)
            dbg_ref[...] = jnp.zeros_like(dbg_ref)
            nxt_ref[...] = jnp.zeros_like(nxt_ref)

        inv_cnt = _pool_counts(i * tm, tm)
        zp, za, o = zp_ref[...].astype(F32), za_ref[...].astype(F32), o_ref[...].astype(F32)
        pooled, mixed, sp, sa, y_pool, y_attn, g0, g1, up_p, up_a, merged = _post_forward(
            u_ref[...], jnp.where(i == 0, 0.0, uh_ref[...]), (main_ref, edge_ref), inv_cnt, zp, o, za, gl_ref[...],
            bg_ref[...], pw_ref, sc_ref[...], wpu_ref, wau_ref)
        merged_ref[...] = merged.astype(BF16)
        yp_ref[...] = y_pool.astype(BF16)
        ya_ref[...] = y_attn.astype(BF16)
        pooled_ref[...] = pooled

        dmerged = _nt(dx_ref[...].astype(BF16), wout_ref[...])
        dup = (dmerged * g0).astype(BF16)
        dua = (dmerged * g1).astype(BF16)
        dup_ref[...] = dup
        dua_ref[...] = dua
        dgl0 = (dmerged * up_p) * (g0 * (1.0 - g0))
        dgl1 = (dmerged * up_a) * (g1 * (1.0 - g1))
        dgl_ref[:, :D_MODEL] = dgl0.astype(BF16)
        dgl_ref[:, D_MODEL:] = dgl1.astype(BF16)
        dbg_ref[:, :D_MODEL] += jnp.sum(dgl0, axis=0, keepdims=True)
        dbg_ref[:, D_MODEL:] += jnp.sum(dgl1, axis=0, keepdims=True)

        dy_attn = _nt(dua, wau_ref[...])
        do_ref[...] = (dy_attn * (za * sa)).astype(BF16)
        dza_ref[...] = ((dy_attn * o) * (sa * (1.0 + za * (1.0 - sa)))).astype(BF16)

        dy_pool = _nt(dup, wpu_ref[...])
        ms = mixed * sc_ref[...]
        dms = dy_pool * (zp * sp)
        duz_ref[:, POOL_WIDTH:] = ((dy_pool * ms) * (sp * (1.0 + zp * (1.0 - sp)))).astype(BF16)
        dsc_ref[...] += jnp.sum(dms * mixed, axis=0, keepdims=True)
        dmixed = (dms * sc_ref[...]).astype(BF16)
        dmixed_ref[...] = dmixed
        dpooled = [_nt(dmixed[:, g * POOL_GROUP:(g + 1) * POOL_GROUP], pw_ref[g].astype(BF16)) for g in range(4)]
        scaled = jnp.concatenate([d * inv for d, inv in zip(dpooled, inv_cnt)], axis=1)
        for g, tot in enumerate(_window_sums(scaled, nxt_ref[...], back_main_ref, back_edge_ref, True)):
            duz_ref[:, g * POOL_GROUP:(g + 1) * POOL_GROUP] = (tot - dpooled[g]).astype(BF16)
        nxt_ref[...] = scaled[:POOL_HALO]

    sd = lambda n, dt: jax.ShapeDtypeStruct((S, n), dt)
    bands = [const((4, tm, tm)), const((4 * POOL_HALO, POOL_HALO))]
    return pl.pallas_call(
        body, name=f"post_bwd_l{layer}", grid=(n_tiles,),
        in_specs=[row(D_MODEL), row(512), halo, *bands, *bands, row(512), row(512), row(512), row(2048),
                  const((1, 2048)), const((4, POOL_GROUP, POOL_GROUP)), const((1, POOL_WIDTH)),
                  _layer_weight_spec(POOL_WIDTH, D_MODEL, layer), _layer_weight_spec(ATTN_WIDTH, D_MODEL, layer),
                  _layer_weight_spec(D_MODEL, D_MODEL, layer)],
        out_specs=[row(1024), row(512), row(512), row(2048), const((1, POOL_WIDTH)), const((1, 2048)),
                   row(D_MODEL), row(D_MODEL), row(D_MODEL), row(512), row(512), row(512), row(512)],
        out_shape=[sd(1024, BF16), sd(512, BF16), sd(512, BF16), sd(2048, BF16),
                   jax.ShapeDtypeStruct((1, POOL_WIDTH), F32), jax.ShapeDtypeStruct((1, 2048), F32),
                   sd(D_MODEL, BF16), sd(D_MODEL, BF16), sd(D_MODEL, BF16), sd(512, BF16), sd(512, BF16),
                   sd(512, BF16), sd(512, BF16)],
        scratch_shapes=[pltpu.VMEM((POOL_HALO, POOL_WIDTH), F32)],
        compiler_params=_params(),
    )(dx, u, u, *_window_bands(tm, False), *_window_bands(tm, True), zp, o, za, gl, bg, pw, scale, wpu, wau, wout)


def _rms_backward(dh, xv, r, g):
    xhat = xv * r
    dxhat = dh * g
    return r * (dxhat - xhat * jnp.mean(dxhat * xhat, axis=-1, keepdims=True)), dh * xhat


def _loss_head(x, g, target):
    S = x.shape[0]
    tm = min(ROW_TILE, S)

    def body(x_ref, g_ref, t_ref, loss_ref, dx_ref, dg_ref):
        @pl.when(pl.program_id(0) == 0)
        def _():
            loss_ref[...] = jnp.zeros_like(loss_ref)
            dg_ref[...] = jnp.zeros_like(dg_ref)

        xv = x_ref[...]
        r = lax.rsqrt(jnp.mean(xv * xv, axis=-1, keepdims=True) + RMS_EPS)
        diff = (xv * r) * g_ref[...] - t_ref[...]
        per_row = jnp.mean(diff * diff, axis=-1, keepdims=True)
        loss_ref[...] += 0.5 * jnp.sum(per_row, axis=0, keepdims=True)
        dx, dg_rows = _rms_backward(diff * (1.0 / D_MODEL), xv, r, g_ref[...])
        dx_ref[...] = dx
        dg_ref[...] += jnp.sum(dg_rows, axis=0, keepdims=True)

    row = pl.BlockSpec((tm, D_MODEL), lambda i: (i, 0))
    vec = pl.BlockSpec((1, D_MODEL), lambda i: (0, 0))
    return pl.pallas_call(
        body, name="loss_head", grid=(S // tm,),
        in_specs=[row, vec, row],
        out_specs=[pl.BlockSpec((1, LANES), lambda i: (0, 0)), row, vec],
        out_shape=[jax.ShapeDtypeStruct((1, LANES), F32), jax.ShapeDtypeStruct((S, D_MODEL), F32),
                   jax.ShapeDtypeStruct((1, D_MODEL), F32)],
        compiler_params=_params(),
    )(x, g, target)


def _inproj_bwd(pieces, w_in, x, g, dx_res, layer):
    S = x.shape[0]
    tm = min(PROJ_ROW_TILE, S)
    cols = [(c0, p.shape[1]) for p, c0 in pieces]

    def body(*refs):
        piece_refs = refs[:len(cols)]
        w_ref, x_ref, g_ref, res_ref, dx_ref, dg_ref = refs[len(cols):]

        @pl.when(pl.program_id(0) == 0)
        def _():
            dg_ref[...] = jnp.zeros_like(dg_ref)

        dh = jnp.zeros((tm, D_MODEL), F32)
        for p_ref, (c0, n) in zip(piece_refs, cols):
            for c in range(0, n, 512):
                dh = dh + _nt(p_ref[:, c:c + 512].astype(BF16), w_ref[:, c0 + c:c0 + c + 512])
        xv = x_ref[...]
        r = lax.rsqrt(jnp.mean(xv * xv, axis=-1, keepdims=True) + RMS_EPS)
        dx, dg_rows = _rms_backward(dh, xv, r, g_ref[...])
        dx_ref[...] = res_ref[...] + dx
        dg_ref[...] += jnp.sum(dg_rows, axis=0, keepdims=True)

    row = lambda n: pl.BlockSpec((tm, n), lambda i: (i, 0))
    vec = pl.BlockSpec((1, D_MODEL), lambda i: (0, 0))
    return pl.pallas_call(
        body, name=f"inproj_bwd_l{layer}", grid=(S // tm,),
        in_specs=[row(n) for _, n in cols] + [_layer_weight_spec(D_MODEL, IN_WIDTH, layer), row(D_MODEL), vec,
                                              row(D_MODEL)],
        out_specs=[row(D_MODEL), vec],
        out_shape=[jax.ShapeDtypeStruct((S, D_MODEL), F32), jax.ShapeDtypeStruct((1, D_MODEL), F32)],
        compiler_params=_params(),
    )(*[p for p, _ in pieces], w_in, x, g, dx_res)


class _SiblingSwap:
    def __init__(self, mine, theirs, send, recv):
        x, y, c, _ = _place()
        self.copies = [_remote(m.at[1 - c], t, send, recv, a, (x, y, 1 - c))
                       for a, (m, t) in enumerate(zip(mine, theirs))]

    def start(self):
        for cp in self.copies:
            cp.start()

    def wait(self):
        for cp in self.copies:
            cp.wait()


def _with_swap(body, grid, n_in, n_out, n_swap):
    if not n_swap:
        return lambda *refs: body(refs[:n_in], refs[n_in:])

    def riding(*refs):
        ins, mine = refs[:n_in], refs[n_in:n_in + n_swap]
        outs, theirs = refs[n_in + n_swap:n_in + n_swap + n_out], refs[n_in + n_swap + n_out:n_in + 2 * n_swap + n_out]
        swap = _SiblingSwap(mine, theirs, *refs[n_in + 2 * n_swap + n_out:])
        step = [pl.program_id(d) for d in range(len(grid))]
        first, last = step[0] == 0, step[0] == grid[0] - 1
        for d in range(1, len(grid)):
            first, last = jnp.logical_and(first, step[d] == 0), jnp.logical_and(last, step[d] == grid[d] - 1)
        pl.when(first)(swap.start)
        body(ins, outs)
        pl.when(last)(swap.wait)

    return riding


def _swap_specs(swap):
    any_space = pl.BlockSpec(memory_space=pl.ANY)
    shapes = [jax.ShapeDtypeStruct(d.shape[1:], d.dtype) for d in swap]
    sems = [pltpu.SemaphoreType.DMA((len(swap),)), pltpu.SemaphoreType.DMA((len(swap),))] if swap else []
    return [any_space] * len(swap), shapes, sems


def _wgrad(a, b, name, layer, into=None, col0=0, n_total=None, swap=()):
    S, M = a.shape
    N = b.shape[1]
    n_total = N if n_total is None else n_total
    tk = min(2048, S)
    tn = min(512, N)
    grid = (N // tn, S // tk)

    def body(ins, outs):
        prod = _tn(ins[0][...].astype(BF16), ins[1][...].astype(BF16))

        @pl.when(pl.program_id(1) == 0)
        def _():
            outs[0][...] = prod

        @pl.when(pl.program_id(1) > 0)
        def _():
            outs[0][...] += prod

    in_specs = [pl.BlockSpec((tk, M), lambda j, k: (k, 0)), pl.BlockSpec((tk, tn), lambda j, k: (k, j))]
    args = [a, b]
    aliases = {}
    if into is not None:
        in_specs.append(pl.BlockSpec(memory_space=pl.ANY))
        args.append(into)
        aliases = {2: 0}
    swap_specs, swap_shapes, swap_sems = _swap_specs(swap)
    res = pl.pallas_call(
        _with_swap(body, grid, len(args), 1, len(swap)), name=name, grid=grid,
        in_specs=in_specs + swap_specs,
        out_specs=[pl.BlockSpec((None, M, tn), lambda j, k: (layer, 0, col0 // tn + j))] + swap_specs,
        out_shape=[jax.ShapeDtypeStruct((2, M, n_total), F32)] + swap_shapes,
        input_output_aliases=aliases,
        scratch_shapes=swap_sems,
        compiler_params=_params(),
    )(*args, *swap)
    return (res[0], res[1:]) if swap else res[0]


def _pool_wgrad(pooled, dmixed, layer, swap=()):
    S = pooled.shape[0]
    tk = min(1024, S)
    grid = (4, S // tk)

    def body(ins, outs):
        prod = _tn(ins[0][...], ins[1][...])

        @pl.when(pl.program_id(1) == 0)
        def _():
            outs[0][...] = prod

        @pl.when(pl.program_id(1) > 0)
        def _():
            outs[0][...] += prod

    blk = pl.BlockSpec((tk, POOL_GROUP), lambda g, k: (k, g))
    swap_specs, swap_shapes, swap_sems = _swap_specs(swap)
    res = pl.pallas_call(
        _with_swap(body, grid, 2, 1, len(swap)), name=f"pool_wgrad_l{layer}", grid=grid,
        in_specs=[blk, blk] + swap_specs,
        out_specs=[pl.BlockSpec((None, POOL_GROUP, POOL_GROUP), lambda g, k: (g, 0, 0))] + swap_specs,
        out_shape=[jax.ShapeDtypeStruct((4, POOL_GROUP, POOL_GROUP), F32)] + swap_shapes,
        scratch_shapes=swap_sems,
        compiler_params=_params(),
    )(pooled, dmixed, *swap)
    return (res[0], res[1:]) if swap else res[0]


def _local_step(x, target, norm_g, b_gate, pool_w, pool_scale, final_g, weights):
    n_layers = norm_g.shape[0]
    saved = []
    for l in range(n_layers):
        g = norm_g[l][None]
        bg = b_gate[l][None]
        sc = pool_scale[l][None]
        if l == 0:
            (u, zp, q, k, v, za, gl, h), (w_in, w_pu, w_au, w_out) = _rms_inproj(x, g, None, l, gather=weights)
        else:
            (u, zp, q, k, v, za, gl, h), _ = _rms_inproj(x, g, w_in, l)
        o, carry = _attn_fwd(q, k, v, l)
        saved.append((x, g, bg, sc, u, zp, q, k, v, za, gl, h, o, carry))
        x = _post_fwd(x, u, zp, o, za, gl, bg, pool_w[l], sc, w_pu, w_au, w_out, l)
    loss, dx, d_final_g = _loss_head(x, final_g[None], target)

    small = [None] * n_layers
    dw_in = dw_out = dw_pu = dw_au = None
    for l in reversed(range(n_layers)):
        x_in, g, bg, sc, u, zp, q, k, v, za, gl, h, o, carry = saved[l]
        (duz, do, dza, dgl, dsc, dbg, merged, dup, dua, y_pool, y_attn, pooled, dmixed) = _post_bwd(
            dx, u, zp, o, za, gl, bg, pool_w[l], sc, w_pu, w_au, w_out, l)
        dq, dk, dv = _attn_bwd(q, k, v, carry, do, l)
        pieces = [(duz, C_U), (dq, C_Q), (dk, C_K), (dv, C_V), (dza, C_ZA), (dgl, C_GL)]
        for p, c0 in pieces:
            dw_in = _wgrad(h, p, f"wgrad_in_l{l}_c{c0}", l, into=dw_in, col0=c0, n_total=IN_WIDTH)
        if l > 0:
            dw_out = _wgrad(merged, dx, f"wgrad_out_l{l}", l, into=dw_out)
        else:
            dw_out, (other_in,) = _wgrad(merged, dx, f"wgrad_out_l{l}", l, into=dw_out, swap=(dw_in,))
        dw_pu = _wgrad(y_pool, dup, f"wgrad_pu_l{l}", l, into=dw_pu)
        dw_au = _wgrad(y_attn, dua, f"wgrad_au_l{l}", l, into=dw_au)
        if l > 0:
            dpw = _pool_wgrad(pooled, dmixed, l)
        else:
            dpw, (other_pu, other_au, other_out) = _pool_wgrad(pooled, dmixed, l, swap=(dw_pu, dw_au, dw_out))
        dx, dg = _inproj_bwd(pieces, w_in, x_in, g, dx, l)
        small[l] = (dg[0], dbg[0], dpw, dsc[0])
    small = [jnp.stack([small[l][i] for l in range(n_layers)]) for i in range(4)]
    return loss[0, 0], dx, d_final_g[0], small, (dw_in, dw_pu, dw_au, dw_out), (other_in, other_pu, other_au, other_out)


SHARDED = ((2, 1280), (2, 256), (2, 256), (1, 256))
ANY = pl.BlockSpec(memory_space=pl.ANY)


def _part(ref, s, axis, width):
    sl = pl.ds(pl.multiple_of(s * width, width), width)
    return ref.at[:, sl] if axis == 2 else ref.at[sl, :]


def _place():
    x, y, c = lax.axis_index("x"), lax.axis_index("y"), lax.axis_index("c")
    return x, y, c, 2 * x + y


def _other_chip(x, y, m):
    px = 1 - x if m & 2 else x
    py = 1 - y if m & 1 else y
    return px, py, 2 * px + py


def _remote(src, dst, send, recv, k, to):
    return pltpu.make_async_remote_copy(src_ref=src, dst_ref=dst, send_sem=send.at[k], recv_sem=recv.at[k],
                                        device_id=to, device_id_type=MESH)


def _part_spec(tr, rows_s, cols_s, axis, width, lead):
    if axis == 2:
        return pl.BlockSpec((None, tr, width), lambda *a: (lead(a), a[-2], a[-1][1]))
    return pl.BlockSpec((None, tr, cols_s), lambda *a: (lead(a), a[-1][1] * (rows_s // tr) + a[-2], 0))


def _cast_into_place(w, pos, axis, width, name):
    L, Rs, Cs = w.shape
    tr = min(256, Rs)
    shape = [L, Rs, Cs]
    shape[axis] *= N_CHIPS

    def body(pos_ref, w_ref, o_ref):
        o_ref[...] = w_ref[...].astype(BF16)

    return pl.pallas_call(
        body, name=name,
        grid_spec=pltpu.PrefetchScalarGridSpec(
            num_scalar_prefetch=1, grid=(L, Rs // tr),
            in_specs=[pl.BlockSpec((None, tr, Cs), lambda l, i, pos: (l, i, 0))],
            out_specs=_part_spec(tr, Rs, Cs, axis, width, lambda a: a[0])),
        out_shape=jax.ShapeDtypeStruct(tuple(shape), BF16),
        compiler_params=_params(),
    )(pos, w)


def _w_in_half(layer):
    def piece(refs, who, shard):
        rows = pl.ds(pl.multiple_of(who * (D_MODEL // 2), D_MODEL // 2), D_MODEL // 2)
        return refs[0].at[layer, rows, pl.ds(pl.multiple_of(shard * SHARDED[0][1], SHARDED[0][1]), SHARDED[0][1])]
    return piece


def _whole_layer(a):
    def piece(refs, who, shard):
        return _part(refs[a].at[who], shard, *SHARDED[a])
    return piece


FIRST_PIECES = (_w_in_half(0),)
LATER_PIECES = (_w_in_half(1), _whole_layer(1), _whole_layer(2), _whole_layer(3))


class _Gather:
    def __init__(self, pieces, refs, send, recv):
        self.pieces, self.refs, self.send, self.recv = pieces, refs, send, recv
        self.x, self.y, self.c, s = _place()
        self.first = []
        for u, piece in enumerate(pieces):
            for m in (1, 2, 3):
                px, py, _ = _other_chip(self.x, self.y, m)
                own = piece(refs, self.c, s)
                self.first.append(_remote(own, own, send, recv, 3 * u + m - 1, (px, py, self.c)))

    def start(self):
        for cp in self.first:
            cp.start()

    def finish(self):
        x, y, c, n = self.x, self.y, self.c, len(self.pieces)

        def landed(u, m, who):
            _, _, sp = _other_chip(x, y, m)
            return self.pieces[u](self.refs, who, sp)

        passed = []
        for m in (1, 2, 3):
            for u in range(n):
                got = landed(u, m, c)
                _remote(got, got, self.send, self.recv, 3 * u + m - 1, (x, y, c)).wait_recv()
                passed.append(_remote(got, got, self.send, self.recv, 3 * n + 3 * u + m - 1, (x, y, 1 - c)))
                passed[-1].start()
        for m in (1, 2, 3):
            for u in range(n):
                got = landed(u, m, 1 - c)
                _remote(got, got, self.send, self.recv, 3 * n + 3 * u + m - 1, (x, y, c)).wait_recv()
        for cp in self.first + passed:
            cp.wait_send()

    @staticmethod
    def semaphores(pieces):
        return [pltpu.SemaphoreType.DMA((6 * len(pieces),)), pltpu.SemaphoreType.DMA((6 * len(pieces),))]


def _gather_first(fulls):
    n = len(fulls)

    def body(*refs):
        gather = _Gather(FIRST_PIECES, refs[n:2 * n], *refs[2 * n:])
        gather.start()
        gather.finish()

    return pl.pallas_call(
        body, name="gather_first",
        in_specs=[ANY] * n, out_specs=[ANY] * n,
        out_shape=[jax.ShapeDtypeStruct(f.shape, f.dtype) for f in fulls],
        input_output_aliases={a: a for a in range(n)},
        scratch_shapes=_Gather.semaphores(FIRST_PIECES),
    )(*fulls)


def _pair_sum(dw, other, pos, name):
    _, R, C = dw.shape
    tr = 128 if C > 1024 else 256

    def body(pos_ref, a_ref, b_ref, o_ref, ob_ref):
        tot = a_ref[...] + b_ref[...]
        o_ref[...] = tot
        ob_ref[...] = tot.astype(BF16)

    blk = pl.BlockSpec((tr, C), lambda i, pos: (i, 0))
    return pl.pallas_call(
        body, name=name,
        grid_spec=pltpu.PrefetchScalarGridSpec(
            num_scalar_prefetch=1, grid=(R // tr,),
            in_specs=[pl.BlockSpec((None, tr, C), lambda i, pos: (pos[0], i, 0)), blk],
            out_specs=[blk, blk]),
        out_shape=[jax.ShapeDtypeStruct((R, C), F32), jax.ShapeDtypeStruct((R, C), BF16)],
        compiler_params=_params(),
    )(pos, dw, other)


def _chip_exchange(ps, packed):
    n = len(ps)

    def body(*refs):
        ins, small_ref = refs[:n], refs[n]
        outs, total_ref = refs[n + 1:2 * n + 1], refs[2 * n + 1]
        all_ref, send, recv, small_send, small_recv = refs[2 * n + 2:]
        x, y, c, _ = _place()
        my_id = 4 * x + 2 * y + c
        all_ref[my_id] = small_ref[...]
        small = []
        for m in range(1, N_DEVICES):
            px = 1 - x if m & 4 else x
            py = 1 - y if m & 2 else y
            pc = 1 - c if m & 1 else c
            cp = _remote(small_ref, all_ref.at[my_id], small_send, small_recv, m - 1, (px, py, pc))
            cp.start()
            small.append((cp, 4 * px + 2 * py + pc))
        copies = []
        for a, (axis, width) in enumerate(SHARDED):
            for m in (1, 2, 3):
                px, py, sp = _other_chip(x, y, m)
                copies.append(_remote(_part(ins[a], sp, axis, width), outs[a].at[m - 1], send, recv, 3 * a + m - 1,
                                      (px, py, c)))
        for cp in copies:
            cp.start()
        for m, (cp, peer_id) in enumerate(small):
            _remote(small_ref, all_ref.at[peer_id], small_send, small_recv, m, (x, y, c)).wait_recv()
        total = all_ref[0]
        for d in range(1, N_DEVICES):
            total = total + all_ref[d]
        total_ref[...] = total
        for cp, _ in small:
            cp.wait_send()
        for cp in copies:
            cp.wait()

    def landing(p, axis, width):
        shape = [3] + list(p.shape)
        shape[axis] = width
        return jax.ShapeDtypeStruct(tuple(shape), p.dtype)

    vmem = pl.BlockSpec(memory_space=pltpu.VMEM)
    res = pl.pallas_call(
        body, name="grad_chip_exchange",
        in_specs=[ANY] * n + [vmem], out_specs=[ANY] * n + [vmem],
        out_shape=[landing(p, axis, width) for p, (axis, width) in zip(ps, SHARDED)]
        + [jax.ShapeDtypeStruct(packed.shape, packed.dtype)],
        scratch_shapes=[pltpu.VMEM((N_DEVICES,) + packed.shape, F32),
                        pltpu.SemaphoreType.DMA((3 * n,)), pltpu.SemaphoreType.DMA((3 * n,)),
                        pltpu.SemaphoreType.DMA((N_DEVICES - 1,)), pltpu.SemaphoreType.DMA((N_DEVICES - 1,))],
        compiler_params=_params(),
    )(*ps, packed)
    return res[:n], res[n]


def _shard_sum(p, landed, pos, axis, width, name):
    _, Rs, Cs = landed.shape
    tr = min(256, Rs)
    p_spec = _part_spec(tr, Rs, Cs, axis, width, lambda a: 0)

    def body(pos_ref, p_ref, l_ref, o_ref):
        o_ref[...] = ((p_ref[...] + l_ref[0].astype(F32)) + l_ref[1].astype(F32)) + l_ref[2].astype(F32)

    return pl.pallas_call(
        body, name=name,
        grid_spec=pltpu.PrefetchScalarGridSpec(
            num_scalar_prefetch=1, grid=(Rs // tr,),
            in_specs=[p_spec, pl.BlockSpec((3, tr, Cs), lambda i, pos: (0, i, 0))],
            out_specs=pl.BlockSpec((None, tr, Cs), lambda i, pos: (pos[0], i, 0))),
        out_shape=jax.ShapeDtypeStruct((2, Rs, Cs), F32),
        compiler_params=_params(),
    )(pos, p[None], landed)


def _sibling_share(gs):
    n = len(gs)

    def body(*refs):
        outs = refs[n:2 * n]
        send, recv = refs[2 * n:]
        x, y, c, _ = _place()
        copies = [_remote(outs[a].at[c], outs[a].at[c], send, recv, a, (x, y, 1 - c)) for a in range(n)]
        for cp in copies:
            cp.start()
        for a, cp in enumerate(copies):
            cp.wait_send()
            _remote(outs[a].at[1 - c], outs[a].at[1 - c], send, recv, a, (x, y, c)).wait_recv()

    return pl.pallas_call(
        body, name="grad_sibling_share",
        in_specs=[ANY] * n, out_specs=[ANY] * n,
        out_shape=[jax.ShapeDtypeStruct(g.shape, g.dtype) for g in gs],
        input_output_aliases={a: a for a in range(n)},
        scratch_shapes=[pltpu.SemaphoreType.DMA((n,)), pltpu.SemaphoreType.DMA((n,))],
    )(*gs)


def _adamw(w, g, m, v, name):
    shape = w.shape
    C = shape[-1]
    flat = [t.reshape(-1, C) for t in (w, g, m, v)]
    R = flat[0].shape[0]
    tr = max(t for t in range(8, R + 1, 8) if R % t == 0 and t * C <= 384 * 1024)

    def body(w_ref, g_ref, m_ref, v_ref, d_ref, nm_ref, nv_ref):
        gv = g_ref[...]
        nm = ADAM_B1 * m_ref[...] + (1.0 - ADAM_B1) * gv
        nv = ADAM_B2 * v_ref[...] + (1.0 - ADAM_B2) * (gv * gv)
        m_hat = nm / (1.0 - ADAM_B1 ** ADAM_STEP)
        v_hat = nv / (1.0 - ADAM_B2 ** ADAM_STEP)
        d_ref[...] = -ADAM_LR * (m_hat / (jnp.sqrt(v_hat) + ADAM_EPS) + ADAM_WD * w_ref[...])
        nm_ref[...] = nm
        nv_ref[...] = nv

    blk = pl.BlockSpec((tr, C), lambda i: (i, 0))
    out = jax.ShapeDtypeStruct((R, C), F32)
    res = pl.pallas_call(
        body, name=name, grid=(R // tr,),
        in_specs=[blk] * 4, out_specs=[blk] * 3, out_shape=[out] * 3,
        compiler_params=_params(),
    )(*flat)
    return [t.reshape(shape) for t in res]


SMALL_SHAPES = ((2, 1024), (2, 2048), (2, 4, 128, 128), (2, 512), (1024,))


def _pack_small(parts):
    return jnp.concatenate([p.reshape(-1, LANES) for p in parts], axis=0)


def _unpack_small(packed):
    out, row = [], 0
    for shape in SMALL_SHAPES:
        n = 1
        for d in shape:
            n *= d
        out.append(packed[row:row + n // LANES].reshape(shape))
        row += n // LANES
    return out


def kernel(x, norm_g, w_in, b_gate, pool_w, pool_scale, w_pool_up, w_attn_up, w_out, final_g, loss_target, m_norm_g, m_w_in, m_b_gate, m_pool_w, m_pool_scale, m_w_pool_up, m_w_attn_up, m_w_out, m_final_g, v_norm_g, v_w_in, v_b_gate, v_pool_w, v_pool_scale, v_w_pool_up, v_w_attn_up, v_w_out, v_final_g):
    _, _, c, s = _place()
    pos = jnp.stack([c, s]).astype(jnp.int32)
    names = ("w_in", "w_pool_up", "w_attn_up", "w_out")

    weights = _gather_first([_cast_into_place(w, pos, axis, width, f"cast_{n}")
                             for w, (axis, width), n in zip((w_in, w_pool_up, w_attn_up, w_out), SHARDED, names)])
    loss_part, dx, d_final_g, small, dws, other = _local_step(x[0], loss_target[0], norm_g, b_gate, pool_w,
                                                              pool_scale, final_g, weights)
    pair =[_pair_sum(d, o, pos, f"grad_pair_sum_{n}") for d, o, n in zip(dws, other, names)]
    landed, summed = _chip_exchange([pb for _, pb in pair],
                                    _pack_small(small + [d_final_g, jnp.broadcast_to(loss_part, (8, LANES))]))
    mine = [_shard_sum(p, l, pos, axis, width, f"grad_shard_sum_{n}")
            for (p, _), l, (axis, width), n in zip(pair, landed, SHARDED, names)]
    g_in, g_pu, g_au, g_out = _sibling_share(mine)
    g_small = _unpack_small(summed)
    loss = summed[-8, 0]
    upd_small = _adamw(_pack_small([norm_g, b_gate, pool_w, pool_scale, final_g]), _pack_small(g_small),
                       _pack_small([m_norm_g, m_b_gate, m_pool_w, m_pool_scale, m_final_g]),
                       _pack_small([v_norm_g, v_b_gate, v_pool_w, v_pool_scale, v_final_g]), "adamw_small")
    d_small, nm_small, nv_small = [_unpack_small(t) for t in upd_small]
    upd_in = _adamw(w_in, g_in, m_w_in, v_w_in, "adamw_w_in")
    upd_pu = _adamw(w_pool_up, g_pu, m_w_pool_up, v_w_pool_up, "adamw_w_pool_up")
    upd_au = _adamw(w_attn_up, g_au, m_w_attn_up, v_w_attn_up, "adamw_w_attn_up")
    upd_out = _adamw(w_out, g_out, m_w_out, v_w_out, "adamw_w_out")

    def ordered(sm, k):
        big = (upd_in[k], upd_pu[k], upd_au[k], upd_out[k]) if k is not None else (g_in, g_pu, g_au, g_out)
        return [sm[0], big[0], sm[1], sm[2], sm[3], big[1], big[2], big[3], sm[4]]

    return (loss, dx[None], *ordered(g_small, None), *ordered(d_small, 0), *ordered(nm_small, 1),
            *ordered(nv_small, 2))
```

```python
import jax
import jax.numpy as jnp
import numpy as np
from jax import lax
from jax.experimental import pallas as pl
from jax.experimental.pallas import tpu as pltpu

F32 = jnp.float32
BF16 = jnp.bfloat16
MESH = pl.DeviceIdType.MESH

D_MODEL = 1024
POOL_WIDTH = 512
POOL_WINDOWS = (2, 4, 8, 16)
POOL_GROUP = 128
POOL_HALO = 16
ATTN_WIDTH = 512
HEAD_DIM = 64
HEAD_PAIRS = 4
IN_WIDTH = 5120
N_CHIPS = 4
N_DEVICES = 8
RMS_EPS = 1e-6
C_U, C_ZP, C_Q, C_K, C_V, C_ZA, C_GL = 0, 512, 1024, 1536, 2048, 2560, 3072

ADAM_LR, ADAM_B1, ADAM_B2, ADAM_EPS, ADAM_WD, ADAM_STEP = 0.001, 0.9, 0.999, 1e-08, 0.01, 10

LANES = 128
ATTN_BLOCK = 256
QUERY_BLOCKS = 4
ROW_TILE = 256
PROJ_ROW_TILE = 512
VMEM_LIMIT = 56 * 1024 * 1024


def _params(**kw):
    return pltpu.CompilerParams(vmem_limit_bytes=VMEM_LIMIT, **kw)


def _nt(a, b):
    return lax.dot_general(a, b, (((1,), (1,)), ((), ())), preferred_element_type=F32)


def _tn(a, b):
    return lax.dot_general(a, b, (((0,), (0,)), ((), ())), preferred_element_type=F32)


def _nn(a, b):
    return jnp.dot(a, b, preferred_element_type=F32)


def _sigmoid(z):
    return 1.0 / (1.0 + jnp.exp(-z))


def _rms_inproj(x, g, w_in, layer, gather=()):
    S = x.shape[0]
    tm = min(PROJ_ROW_TILE, S)
    n_tiles = S // tm
    n_g = len(gather)

    def body(*refs):
        x_ref, g_ref = refs[:2]
        if n_g:
            (u_ref, zp_ref, q_ref, k_ref, v_ref, za_ref, gl_ref, h_ref) = refs[2 + n_g:10 + n_g]
            fulls = refs[10 + n_g:10 + 2 * n_g]
            w_ref, load_sem, send, recv = refs[10 + 2 * n_g:]
            later = _Gather(LATER_PIECES, fulls, send, recv)

            @pl.when(pl.program_id(0) == 0)
            def _():
                load = pltpu.make_async_copy(fulls[0].at[layer], w_ref, load_sem)
                load.start()
                later.start()
                load.wait()

            @pl.when(pl.program_id(0) == n_tiles - 1)
            def _():
                later.finish()
        else:
            w_ref, u_ref, zp_ref, q_ref, k_ref, v_ref, za_ref, gl_ref, h_ref = refs[2:]
        xv = x_ref[...]
        r = lax.rsqrt(jnp.mean(xv * xv, axis=-1, keepdims=True) + RMS_EPS)
        h = ((xv * r) * g_ref[...]).astype(BF16)
        h_ref[...] = h

        def mm(c0, n):
            return _nn(h, w_ref[:, c0:c0 + n])

        u_ref[...] = mm(C_U, 512)
        zp_ref[...] = mm(C_ZP, 512).astype(BF16)
        q_ref[...] = (mm(C_Q, 512) * 0.125).astype(BF16)
        k_ref[...] = mm(C_K, 512).astype(BF16)
        v_ref[...] = mm(C_V, 512).astype(BF16)
        za_ref[...] = mm(C_ZA, 512).astype(BF16)
        for c in range(4):
            gl_ref[:, c * 512:(c + 1) * 512] = mm(C_GL + c * 512, 512).astype(BF16)

    row = lambda n: pl.BlockSpec((tm, n), lambda i: (i, 0))
    sd = lambda n, dt: jax.ShapeDtypeStruct((S, n), dt)
    any_space = pl.BlockSpec(memory_space=pl.ANY)
    weights = [any_space] * n_g if n_g else [_layer_weight_spec(D_MODEL, IN_WIDTH, layer)]
    res = pl.pallas_call(
        body, name=f"rms_inproj_l{layer}", grid=(n_tiles,),
        in_specs=[row(D_MODEL), pl.BlockSpec((1, D_MODEL), lambda i: (0, 0))] + weights,
        out_specs=[row(512), row(512), row(512), row(512), row(512), row(512), row(2048), row(D_MODEL)]
        + [any_space] * n_g,
        out_shape=[sd(512, F32), sd(512, BF16), sd(512, BF16), sd(512, BF16), sd(512, BF16), sd(512, BF16),
                   sd(2048, BF16), sd(D_MODEL, BF16)] + [jax.ShapeDtypeStruct(f.shape, f.dtype) for f in gather],
        input_output_aliases={2 + a: 8 + a for a in range(n_g)},
        scratch_shapes=([pltpu.VMEM((D_MODEL, IN_WIDTH), BF16), pltpu.SemaphoreType.DMA(())]
                        + _Gather.semaphores(LATER_PIECES)) if n_g else [],
        compiler_params=_params(),
    )(x, g, *(gather if n_g else (w_in,)))
    return res[:8], res[8:]


def _tri(n, strict_lower):
    r = lax.broadcasted_iota(jnp.int32, (n, n), 0)
    c = lax.broadcasted_iota(jnp.int32, (n, n), 1)
    return jnp.where(r > c if strict_lower else r < c, 1.0, 0.0).astype(BF16)


def _split_dot(x, m):
    hi = x.astype(BF16)
    lo = (x - hi.astype(F32)).astype(BF16)
    return _nn(hi, m) + _nn(lo, m)


def _log_terms(z):
    lg = jnp.log(1.0 + jnp.exp(-jnp.abs(z)))
    a = jnp.minimum(z, 0.0) - lg
    return a, a - z


EXHAUSTED = -104.0
UNREACHED = -1e30


class _HeadPair:
    def __init__(self, T):
        self.T = T
        self.first = lax.broadcasted_iota(jnp.int32, (T, LANES), 1) < HEAD_DIM
        self.lane = lax.broadcasted_iota(jnp.int32, (2 * T, LANES), 1)
        row = lax.broadcasted_iota(jnp.int32, (2 * T, T), 0)
        row = jnp.where(row >= T, row - T, row)
        self.causal = row > lax.broadcasted_iota(jnp.int32, (2 * T, T), 1)
        self.below = _tri(T, True)

    def stack(self, x2):
        return jnp.concatenate([jnp.where(self.first, x2, 0), jnp.where(self.first, 0, x2)], axis=0).astype(BF16)

    def unstack(self, x):
        return jnp.where(self.first, x[:self.T], x[self.T:])

    def keys(self, ref, blocks):
        T = self.T
        return jnp.concatenate([ref[pl.ds(pl.multiple_of(j * T, T), T), :] for j, _ in blocks], axis=0)

    def log_terms(self, z, blocks):
        T = self.T
        a_all, l_all = _log_terms(z)
        a = [a_all[:, b * T:(b + 1) * T] for b in range(len(blocks))]
        l1m = [l_all[:, b * T:(b + 1) * T] for b in range(len(blocks))]
        return a, [jnp.where(self.causal, l, 0.0) if diagonal else l for l, (_, diagonal) in zip(l1m, blocks)]

    def later_sums(self, l1m):
        later = _split_dot(jnp.concatenate(l1m, axis=0), self.below)
        return [later[2 * self.T * b:2 * self.T * (b + 1)] for b in range(len(l1m))]


def _sections(x, n):
    return x.reshape(n, x.shape[0] // n, x.shape[1])


def _attn_fwd(q, k, v, layer):
    S = q.shape[0]
    T = min(ATTN_BLOCK, S)
    nq = S // T
    jobs = min(QUERY_BLOCKS, nq)
    assert nq <= LANES and nq % jobs == 0
    per_job = nq // jobs

    def body(q_ref, k_ref, v_ref, o_ref, c_ref):
        i = pl.program_id(1)
        pair = _HeadPair(T)
        qs = [pair.stack(q_ref[n]) for n in range(jobs)]
        diag = [i + n * per_job for n in range(jobs)]

        def sweep(jobs):
            kv = [(pair.keys(k_ref, bl), pair.keys(v_ref, bl)) for _, bl, _ in jobs]
            zs = [_nt(qs[n], kcat) for (n, _, _), (kcat, _) in zip(jobs, kv)]
            terms = [pair.log_terms(z, bl) for (_, bl, _), z in zip(jobs, zs)]
            laters = [pair.later_sums(l1m) for _, l1m in terms]
            weights = []
            for (_, bl, (acc, run, saved)), (a, l1m), later in zip(jobs, terms, laters):
                ws = []
                for b, (j, diagonal) in enumerate(bl):
                    saved = jnp.where(pair.lane == j, run, saved)
                    w = jnp.exp(a[b] + later[b] + run)
                    ws.append(jnp.where(pair.causal, w, 0.0) if diagonal else w)
                    run = run + jnp.sum(l1m[b], axis=1, keepdims=True)
                weights.append((jnp.concatenate(ws, axis=1).astype(BF16), acc, run, saved))
            return [(acc + _nn(w, vcat), run, saved) for (w, acc, run, saved), (_, vcat) in zip(weights, kv)]

        def alive(carry):
            return (jnp.max(carry[1]) > EXHAUSTED).astype(jnp.int32)

        def older_blocks(n, carry):
            def older_block(state):
                j, _, c = state
                c = sweep([(n, [(j, False)], c)])[0]
                return j - 1, alive(c), c

            return lax.while_loop(lambda st: jnp.logical_and(st[0] >= 0, st[1] > 0), older_block,
                                  (diag[n] - 2, alive(carry), carry))[2]

        def run(first_blocks):
            init = (jnp.zeros((2 * T, LANES), F32), jnp.zeros((2 * T, 1), F32),
                    jnp.full((2 * T, LANES), UNREACHED, F32))
            carries = sweep([(n, first_blocks[n], init) for n in range(jobs)])
            for n in range(jobs):
                acc, _, saved = older_blocks(n, carries[n])
                o_ref[n] = pair.unstack(acc).astype(BF16)
                c_ref[n, :, :LANES] = saved[:T]
                c_ref[n, :, LANES:] = saved[T:]

        with_previous = lambda d: [(d, True), (d - 1, False)]

        @pl.when(i == 0)
        def _():
            run([[(diag[0], True)]] + [with_previous(d) for d in diag[1:]])

        @pl.when(i > 0)
        def _():
            run([with_previous(d) for d in diag])

    blk = lambda n: pl.BlockSpec((jobs, T, n), lambda p, i: (0, i, p))
    full = pl.BlockSpec((S, LANES), lambda p, i: (0, p))
    o, carry = pl.pallas_call(
        body, name=f"attn_fwd_l{layer}", grid=(HEAD_PAIRS, per_job),
        in_specs=[blk(LANES), full, full],
        out_specs=[blk(LANES), blk(2 * LANES)],
        out_shape=[jax.ShapeDtypeStruct((jobs, S // jobs, ATTN_WIDTH), BF16),
                   jax.ShapeDtypeStruct((jobs, S // jobs, 8 * LANES), F32)],
        compiler_params=_params(),
    )(_sections(q, jobs), k, v)
    return o.reshape(S, ATTN_WIDTH), carry.reshape(S, 8 * LANES)


def _attn_bwd(q, k, v, saved, do, layer):
    S = q.shape[0]
    T = min(ATTN_BLOCK, S)
    nq = S // T
    jobs = min(QUERY_BLOCKS, nq)
    per_job = nq // jobs

    def body(q_ref, k_ref, v_ref, c_ref, do_ref, dq_ref, dk_ref, dv_ref):
        i = pl.program_id(1)

        @pl.when(i == 0)
        def _():
            dk_ref[...] = jnp.zeros_like(dk_ref)
            dv_ref[...] = jnp.zeros_like(dv_ref)

        pair = _HeadPair(T)
        diag = [i + n * per_job for n in range(jobs)]
        qs = [pair.stack(q_ref[n]) for n in range(jobs)]
        dos = [pair.stack(do_ref[n].astype(BF16)) for n in range(jobs)]
        saved = [jnp.concatenate([c_ref[n, :, :LANES], c_ref[n, :, LANES:]], axis=0) for n in range(jobs)]
        before = _tri(T, False)

        def sweep(jobs):
            kv = [(pair.keys(k_ref, bl), pair.keys(v_ref, bl)) for _, bl, _ in jobs]
            zs = [_nt(qs[n], kcat) for (n, _, _), (kcat, _) in zip(jobs, kv)]
            gs = [_nt(dos[n], vcat) for (n, _, _), (_, vcat) in zip(jobs, kv)]
            terms = [pair.log_terms(z, bl) for (_, bl, _), z in zip(jobs, zs)]
            laters = [pair.later_sums(l1m) for _, l1m in terms]
            ws, es = [], []
            for (n, bl, _), (a, _), later, g in zip(jobs, terms, laters, gs):
                w_job, e_job = [], []
                for b, (j, diagonal) in enumerate(bl):
                    run = jnp.sum(jnp.where(pair.lane == j, saved[n], 0.0), axis=1, keepdims=True)
                    w = jnp.exp(a[b] + later[b] + run)
                    w_job.append(jnp.where(pair.causal, w, 0.0) if diagonal else w)
                    e_job.append(w_job[b] * g[:, b * T:(b + 1) * T])
                ws.append(w_job)
                es.append(e_job)
            prefixes = [_nn(jnp.concatenate(e_job, axis=0).astype(BF16), before) for e_job in es]
            dzs, olders = [], []
            for (_, bl, (_, older)), (a, _), e_job, prefix in zip(jobs, terms, es, prefixes):
                dz_job = []
                for b, (j, diagonal) in enumerate(bl):
                    dz = e_job[b] - jnp.exp(a[b]) * (e_job[b] + (prefix[2 * T * b:2 * T * (b + 1)] + older))
                    dz_job.append(jnp.where(pair.causal, dz, 0.0) if diagonal else dz)
                    older = older + jnp.sum(e_job[b], axis=1, keepdims=True)
                dzs.append(jnp.concatenate(dz_job, axis=1).astype(BF16))
                olders.append(older)
            out = []
            for (n, bl, (dq, _)), dz, w_job, older, (kcat, _) in zip(jobs, dzs, ws, olders, kv):
                dk = _tn(dz, qs[n])
                dv = _tn(jnp.concatenate(w_job, axis=1).astype(BF16), dos[n])
                for b, (j, _) in enumerate(bl):
                    rows = pl.ds(pl.multiple_of(j * T, T), T)
                    dk_ref[rows, :] += dk[b * T:(b + 1) * T]
                    dv_ref[rows, :] += dv[b * T:(b + 1) * T]
                out.append((dq + _nn(dz, kcat), older))
            return out

        def older_blocks(n):
            col_max = jnp.max(saved[n], axis=0, keepdims=True)
            lane_row = lax.broadcasted_iota(jnp.int32, (1, LANES), 1)
            reached = jnp.sum(jnp.where(jnp.logical_and(col_max > EXHAUSTED, lane_row < diag[n]), 1, 0))
            init = (jnp.zeros((2 * T, LANES), F32), jnp.zeros((2 * T, 1), F32))
            return lax.fori_loop(diag[n] - reached, diag[n] - 1, lambda j, c: sweep([(n, [(j, False)], c)])[0], init)

        def run(last_blocks):
            carries = sweep([(n, last_blocks[n], older_blocks(n)) for n in range(jobs)])
            for n in range(jobs):
                dq_ref[n] = (pair.unstack(carries[n][0]) * 0.125).astype(BF16)

        with_previous = lambda d: [(d - 1, False), (d, True)]

        @pl.when(i == 0)
        def _():
            run([[(diag[0], True)]] + [with_previous(d) for d in diag[1:]])

        @pl.when(i > 0)
        def _():
            run([with_previous(d) for d in diag])

    blk = lambda n: pl.BlockSpec((jobs, T, n), lambda p, i: (0, i, p))
    full = pl.BlockSpec((S, LANES), lambda p, i: (0, p))
    out = jax.ShapeDtypeStruct((S, ATTN_WIDTH), F32)
    dq, dk, dv = pl.pallas_call(
        body, name=f"attn_bwd_l{layer}", grid=(HEAD_PAIRS, per_job),
        in_specs=[blk(LANES), full, full, blk(2 * LANES), blk(LANES)],
        out_specs=[blk(LANES), full, full],
        out_shape=[jax.ShapeDtypeStruct((jobs, S // jobs, ATTN_WIDTH), BF16), out, out],
        compiler_params=_params(),
    )(_sections(q, jobs), k, v, _sections(saved, jobs), _sections(do, jobs))
    return dq.reshape(S, ATTN_WIDTH), dk, dv


def _pool_counts(row0, tm):
    pos = row0 + lax.broadcasted_iota(jnp.int32, (tm, 1), 0)
    return [1.0 / jnp.minimum(pos + 1, w).astype(F32) for w in POOL_WINDOWS]


def _window_bands(tm, backward):
    t = np.arange(tm)[:, None]
    c = np.arange(tm)[None, :]
    off = c - t if backward else t - c
    main = np.stack([(off >= 0) & (off < w) for w in POOL_WINDOWS])
    r = np.arange(POOL_HALO)[:, None]
    h = np.arange(POOL_HALO)[None, :]
    off = h - r + POOL_HALO if backward else r - h + POOL_HALO
    edge = np.concatenate([(off < w) for w in POOL_WINDOWS])
    return jnp.asarray(main, BF16), jnp.asarray(edge, BF16)


def _window_sums(tile, beside, main_ref, edge_ref, backward):
    tm = tile.shape[0]
    tb = tile.astype(BF16)
    edge = _nn(edge_ref[...], beside.astype(BF16))
    sums = []
    for g in range(len(POOL_WINDOWS)):
        cols = slice(g * POOL_GROUP, (g + 1) * POOL_GROUP)
        tot = _nn(main_ref[g], tb[:, cols])
        extra = edge[g * POOL_HALO:(g + 1) * POOL_HALO, cols]
        if backward:
            sums.append(jnp.concatenate([tot[:tm - POOL_HALO], tot[tm - POOL_HALO:] + extra], axis=0))
        else:
            sums.append(jnp.concatenate([tot[:POOL_HALO] + extra, tot[POOL_HALO:]], axis=0))
    return sums


def _post_forward(u, history, bands, inv_cnt, zp, o, za, gl, bg, pw_ref, scale, wpu_ref, wau_ref):
    pooled, mixed = [], []
    for g, tot in enumerate(_window_sums(u, history, *bands, False)):
        pg = (tot * inv_cnt[g] - u[:, g * POOL_GROUP:(g + 1) * POOL_GROUP]).astype(BF16)
        pooled.append(pg)
        mixed.append(_nn(pg, pw_ref[g].astype(BF16)))
    pooled = jnp.concatenate(pooled, axis=1)
    mixed = jnp.concatenate(mixed, axis=1)
    zp, za, o = zp.astype(F32), za.astype(F32), o.astype(F32)
    sp = _sigmoid(zp)
    sa = _sigmoid(za)
    y_pool = (mixed * scale) * (zp * sp)
    y_attn = o * (za * sa)
    gate = _sigmoid(gl + bg)
    g0, g1 = gate[:, :D_MODEL], gate[:, D_MODEL:]
    up_p = _nn(y_pool.astype(BF16), wpu_ref[...])
    up_a = _nn(y_attn.astype(BF16), wau_ref[...])
    merged = g0 * up_p + g1 * up_a
    return pooled, mixed, sp, sa, y_pool, y_attn, g0, g1, up_p, up_a, merged


def _row_specs(tm, rev, n_tiles):
    tile_of = (lambda i: n_tiles - 1 - i) if rev else (lambda i: i)
    row = lambda n: pl.BlockSpec((tm, n), lambda i: (tile_of(i), 0))
    halo = pl.BlockSpec((POOL_HALO, POOL_WIDTH),
                        lambda i: (jnp.maximum(tile_of(i) * (tm // POOL_HALO) - 1, 0), 0))
    const = lambda shape: pl.BlockSpec(shape, lambda i: (0,) * len(shape))
    return tile_of, row, halo, const


def _layer_weight_spec(rows, cols, layer):
    return pl.BlockSpec((None, rows, cols), lambda i: (layer, 0, 0), pipeline_mode=pl.Buffered(1))


def _post_fwd(x, u, zp, o, za, gl, bg, pw, scale, wpu, wau, wout, layer, head=()):
    S = x.shape[0]
    tm = min(ROW_TILE, S)
    n_tiles = S // tm
    tile_of, row, halo, const = _row_specs(tm, False, n_tiles)

    def body(x_ref, u_ref, uh_ref, main_ref, edge_ref, zp_ref, o_ref, za_ref, gl_ref, bg_ref, pw_ref, sc_ref, wpu_ref,
             wau_ref, wout_ref, *rest):
        i = pl.program_id(0)
        vals = _post_forward(u_ref[...], jnp.where(i == 0, 0.0, uh_ref[...]), (main_ref, edge_ref),
                             _pool_counts(i * tm, tm), zp_ref[...], o_ref[...], za_ref[...], gl_ref[...], bg_ref[...],
                             pw_ref, sc_ref[...], wpu_ref, wau_ref)
        xv = x_ref[...] + _nn(vals[-1].astype(BF16), wout_ref[...])
        if not head:
            rest[0][...] = xv
            return
        gf_ref, t_ref, loss_ref, dx_ref, dg_ref = rest

        @pl.when(i == 0)
        def _():
            loss_ref[...] = jnp.zeros_like(loss_ref)
            dg_ref[...] = jnp.zeros_like(dg_ref)

        r = lax.rsqrt(jnp.mean(xv * xv, axis=-1, keepdims=True) + RMS_EPS)
        diff = (xv * r) * gf_ref[...] - t_ref[...]
        per_row = jnp.mean(diff * diff, axis=-1, keepdims=True)
        loss_ref[...] += 0.5 * jnp.sum(per_row, axis=0, keepdims=True)
        dx, dg_rows = _rms_backward(diff * (1.0 / D_MODEL), xv, r, gf_ref[...])
        dx_ref[...] = dx
        dg_ref[...] += jnp.sum(dg_rows, axis=0, keepdims=True)

    out = jax.ShapeDtypeStruct((S, D_MODEL), F32)
    return pl.pallas_call(
        body, name=f"post_fwd_l{layer}", grid=(n_tiles,),
        in_specs=[row(D_MODEL), row(512), halo, const((4, tm, tm)), const((4 * POOL_HALO, POOL_HALO)), row(512),
                  row(512), row(512), row(2048), const((1, 2048)), const((4, POOL_GROUP, POOL_GROUP)),
                  const((1, POOL_WIDTH)),
                  _layer_weight_spec(POOL_WIDTH, D_MODEL, layer), _layer_weight_spec(ATTN_WIDTH, D_MODEL, layer),
                  _layer_weight_spec(D_MODEL, D_MODEL, layer)] + ([const((1, D_MODEL)), row(D_MODEL)] if head else []),
        out_specs=[const((1, LANES)), row(D_MODEL), const((1, D_MODEL))] if head else row(D_MODEL),
        out_shape=[jax.ShapeDtypeStruct((1, LANES), F32), out, jax.ShapeDtypeStruct((1, D_MODEL), F32)] if head else out,
        compiler_params=_params(),
    )(x, u, u, *_window_bands(tm, False), zp, o, za, gl, bg, pw, scale, wpu, wau, wout, *head)


def _post_bwd(dx, u, zp, o, za, gl, bg, pw, scale, wpu, wau, wout, layer):
    S = dx.shape[0]
    tm = min(ROW_TILE, S)
    n_tiles = S // tm
    tile_of, row, halo, const = _row_specs(tm, True, n_tiles)

    def body(dx_ref, u_ref, uh_ref, main_ref, edge_ref, back_main_ref, back_edge_ref, zp_ref, o_ref, za_ref, gl_ref,
             bg_ref, pw_ref, sc_ref, wpu_ref, wau_ref, wout_ref,
             duz_ref, do_ref, dza_ref, dgl_ref, dsc_ref, dbg_ref,
             merged_ref, dup_ref, dua_ref, yp_ref, ya_ref, pooled_ref, dmixed_ref, nxt_ref):
        step = pl.program_id(0)
        i = tile_of(step)

        @pl.when(step == 0)
        def _():
            dsc_ref[...] = jnp.zeros_like(dsc_ref)
            dbg_ref[...] = jnp.zeros_like(dbg_ref)
            nxt_ref[...] = jnp.zeros_like(nxt_ref)

        inv_cnt = _pool_counts(i * tm, tm)
        zp, za, o = zp_ref[...].astype(F32), za_ref[...].astype(F32), o_ref[...].astype(F32)
        pooled, mixed, sp, sa, y_pool, y_attn, g0, g1, up_p, up_a, merged = _post_forward(
            u_ref[...], jnp.where(i == 0, 0.0, uh_ref[...]), (main_ref, edge_ref), inv_cnt, zp, o, za, gl_ref[...],
            bg_ref[...], pw_ref, sc_ref[...], wpu_ref, wau_ref)
        merged_ref[...] = merged.astype(BF16)
        yp_ref[...] = y_pool.astype(BF16)
        ya_ref[...] = y_attn.astype(BF16)
        pooled_ref[...] = pooled

        dmerged = _nt(dx_ref[...].astype(BF16), wout_ref[...])
        dup = (dmerged * g0).astype(BF16)
        dua = (dmerged * g1).astype(BF16)
        dup_ref[...] = dup
        dua_ref[...] = dua
        dgl0 = (dmerged * up_p) * (g0 * (1.0 - g0))
        dgl1 = (dmerged * up_a) * (g1 * (1.0 - g1))
        dgl_ref[:, :D_MODEL] = dgl0.astype(BF16)
        dgl_ref[:, D_MODEL:] = dgl1.astype(BF16)
        dbg_ref[:, :D_MODEL] += jnp.sum(dgl0, axis=0, keepdims=True)
        dbg_ref[:, D_MODEL:] += jnp.sum(dgl1, axis=0, keepdims=True)

        dy_attn = _nt(dua, wau_ref[...])
        do_ref[...] = (dy_attn * (za * sa)).astype(BF16)
        dza_ref[...] = ((dy_attn * o) * (sa * (1.0 + za * (1.0 - sa)))).astype(BF16)

        dy_pool = _nt(dup, wpu_ref[...])
        ms = mixed * sc_ref[...]
        dms = dy_pool * (zp * sp)
        duz_ref[:, POOL_WIDTH:] = ((dy_pool * ms) * (sp * (1.0 + zp * (1.0 - sp)))).astype(BF16)
        dsc_ref[...] += jnp.sum(dms * mixed, axis=0, keepdims=True)
        dmixed = (dms * sc_ref[...]).astype(BF16)
        dmixed_ref[...] = dmixed
        dpooled = [_nt(dmixed[:, g * POOL_GROUP:(g + 1) * POOL_GROUP], pw_ref[g].astype(BF16)) for g in range(4)]
        scaled = jnp.concatenate([d * inv for d, inv in zip(dpooled, inv_cnt)], axis=1)
        for g, tot in enumerate(_window_sums(scaled, nxt_ref[...], back_main_ref, back_edge_ref, True)):
            duz_ref[:, g * POOL_GROUP:(g + 1) * POOL_GROUP] = (tot - dpooled[g]).astype(BF16)
        nxt_ref[...] = scaled[:POOL_HALO]

    sd = lambda n, dt: jax.ShapeDtypeStruct((S, n), dt)
    bands = [const((4, tm, tm)), const((4 * POOL_HALO, POOL_HALO))]
    return pl.pallas_call(
        body, name=f"post_bwd_l{layer}", grid=(n_tiles,),
        in_specs=[row(D_MODEL), row(512), halo, *bands, *bands, row(512), row(512), row(512), row(2048),
                  const((1, 2048)), const((4, POOL_GROUP, POOL_GROUP)), const((1, POOL_WIDTH)),
                  _layer_weight_spec(POOL_WIDTH, D_MODEL, layer), _layer_weight_spec(ATTN_WIDTH, D_MODEL, layer),
                  _layer_weight_spec(D_MODEL, D_MODEL, layer)],
        out_specs=[row(1024), row(512), row(512), row(2048), const((1, POOL_WIDTH)), const((1, 2048)),
                   row(D_MODEL), row(D_MODEL), row(D_MODEL), row(512), row(512), row(512), row(512)],
        out_shape=[sd(1024, BF16), sd(512, BF16), sd(512, BF16), sd(2048, BF16),
                   jax.ShapeDtypeStruct((1, POOL_WIDTH), F32), jax.ShapeDtypeStruct((1, 2048), F32),
                   sd(D_MODEL, BF16), sd(D_MODEL, BF16), sd(D_MODEL, BF16), sd(512, BF16), sd(512, BF16),
                   sd(512, BF16), sd(512, BF16)],
        scratch_shapes=[pltpu.VMEM((POOL_HALO, POOL_WIDTH), F32)],
        compiler_params=_params(),
    )(dx, u, u, *_window_bands(tm, False), *_window_bands(tm, True), zp, o, za, gl, bg, pw, scale, wpu, wau, wout)


def _rms_backward(dh, xv, r, g):
    xhat = xv * r
    dxhat = dh * g
    return r * (dxhat - xhat * jnp.mean(dxhat * xhat, axis=-1, keepdims=True)), dh * xhat


def _inproj_bwd(pieces, w_in, x, g, dx_res, layer):
    S = x.shape[0]
    tm = min(PROJ_ROW_TILE, S)
    cols = [(c0, p.shape[1]) for p, c0 in pieces]

    def body(*refs):
        piece_refs = refs[:len(cols)]
        w_ref, x_ref, g_ref, res_ref, dx_ref, dg_ref = refs[len(cols):]

        @pl.when(pl.program_id(0) == 0)
        def _():
            dg_ref[...] = jnp.zeros_like(dg_ref)

        dh = jnp.zeros((tm, D_MODEL), F32)
        for p_ref, (c0, n) in zip(piece_refs, cols):
            for c in range(0, n, 512):
                dh = dh + _nt(p_ref[:, c:c + 512].astype(BF16), w_ref[:, c0 + c:c0 + c + 512])
        xv = x_ref[...]
        r = lax.rsqrt(jnp.mean(xv * xv, axis=-1, keepdims=True) + RMS_EPS)
        dx, dg_rows = _rms_backward(dh, xv, r, g_ref[...])
        dx_ref[...] = res_ref[...] + dx
        dg_ref[...] += jnp.sum(dg_rows, axis=0, keepdims=True)

    row = lambda n: pl.BlockSpec((tm, n), lambda i: (i, 0))
    vec = pl.BlockSpec((1, D_MODEL), lambda i: (0, 0))
    return pl.pallas_call(
        body, name=f"inproj_bwd_l{layer}", grid=(S // tm,),
        in_specs=[row(n) for _, n in cols] + [_layer_weight_spec(D_MODEL, IN_WIDTH, layer), row(D_MODEL), vec,
                                              row(D_MODEL)],
        out_specs=[row(D_MODEL), vec],
        out_shape=[jax.ShapeDtypeStruct((S, D_MODEL), F32), jax.ShapeDtypeStruct((1, D_MODEL), F32)],
        compiler_params=_params(),
    )(*[p for p, _ in pieces], w_in, x, g, dx_res)


class _SiblingSwap:
    def __init__(self, mine, theirs, send, recv):
        x, y, c, _ = _place()
        self.copies = [_remote(m.at[1 - c], t, send, recv, a, (x, y, 1 - c))
                       for a, (m, t) in enumerate(zip(mine, theirs))]

    def start(self):
        for cp in self.copies:
            cp.start()

    def wait(self):
        for cp in self.copies:
            cp.wait()


def _with_swap(body, grid, n_in, n_out, n_swap):
    if not n_swap:
        return lambda *refs: body(refs[:n_in], refs[n_in:])

    def riding(*refs):
        ins, mine = refs[:n_in], refs[n_in:n_in + n_swap]
        outs, theirs = refs[n_in + n_swap:n_in + n_swap + n_out], refs[n_in + n_swap + n_out:n_in + 2 * n_swap + n_out]
        swap = _SiblingSwap(mine, theirs, *refs[n_in + 2 * n_swap + n_out:])
        step = [pl.program_id(d) for d in range(len(grid))]
        first, last = step[0] == 0, step[0] == grid[0] - 1
        for d in range(1, len(grid)):
            first, last = jnp.logical_and(first, step[d] == 0), jnp.logical_and(last, step[d] == grid[d] - 1)
        pl.when(first)(swap.start)
        body(ins, outs)
        pl.when(last)(swap.wait)

    return riding


def _swap_specs(swap):
    any_space = pl.BlockSpec(memory_space=pl.ANY)
    shapes = [jax.ShapeDtypeStruct(d.shape[1:], d.dtype) for d in swap]
    sems = [pltpu.SemaphoreType.DMA((len(swap),)), pltpu.SemaphoreType.DMA((len(swap),))] if swap else []
    return [any_space] * len(swap), shapes, sems


def _wgrad(a, b, name, layer, into=None, col0=0, n_total=None, swap=()):
    S, M = a.shape
    N = b.shape[1]
    n_total = N if n_total is None else n_total
    tk = min(2048, S)
    tn = min(512, N)
    grid = (N // tn, S // tk)

    def body(ins, outs):
        prod = _tn(ins[0][...].astype(BF16), ins[1][...].astype(BF16))

        @pl.when(pl.program_id(1) == 0)
        def _():
            outs[0][...] = prod

        @pl.when(pl.program_id(1) > 0)
        def _():
            outs[0][...] += prod

    in_specs = [pl.BlockSpec((tk, M), lambda j, k: (k, 0)), pl.BlockSpec((tk, tn), lambda j, k: (k, j))]
    args = [a, b]
    aliases = {}
    if into is not None:
        in_specs.append(pl.BlockSpec(memory_space=pl.ANY))
        args.append(into)
        aliases = {2: 0}
    swap_specs, swap_shapes, swap_sems = _swap_specs(swap)
    res = pl.pallas_call(
        _with_swap(body, grid, len(args), 1, len(swap)), name=name, grid=grid,
        in_specs=in_specs + swap_specs,
        out_specs=[pl.BlockSpec((None, M, tn), lambda j, k: (layer, 0, col0 // tn + j))] + swap_specs,
        out_shape=[jax.ShapeDtypeStruct((2, M, n_total), F32)] + swap_shapes,
        input_output_aliases=aliases,
        scratch_shapes=swap_sems,
        compiler_params=_params(),
    )(*args, *swap)
    return (res[0], res[1:]) if swap else res[0]


def _pool_wgrad(pooled, dmixed, layer, swap=()):
    S = pooled.shape[0]
    tk = min(1024, S)
    grid = (4, S // tk)

    def body(ins, outs):
        prod = _tn(ins[0][...], ins[1][...])

        @pl.when(pl.program_id(1) == 0)
        def _():
            outs[0][...] = prod

        @pl.when(pl.program_id(1) > 0)
        def _():
            outs[0][...] += prod

    blk = pl.BlockSpec((tk, POOL_GROUP), lambda g, k: (k, g))
    swap_specs, swap_shapes, swap_sems = _swap_specs(swap)
    res = pl.pallas_call(
        _with_swap(body, grid, 2, 1, len(swap)), name=f"pool_wgrad_l{layer}", grid=grid,
        in_specs=[blk, blk] + swap_specs,
        out_specs=[pl.BlockSpec((None, POOL_GROUP, POOL_GROUP), lambda g, k: (g, 0, 0))] + swap_specs,
        out_shape=[jax.ShapeDtypeStruct((4, POOL_GROUP, POOL_GROUP), F32)] + swap_shapes,
        scratch_shapes=swap_sems,
        compiler_params=_params(),
    )(pooled, dmixed, *swap)
    return (res[0], res[1:]) if swap else res[0]


def _local_step(x, target, norm_g, b_gate, pool_w, pool_scale, final_g, weights):
    n_layers = norm_g.shape[0]
    saved = []
    for l in range(n_layers):
        g = norm_g[l][None]
        bg = b_gate[l][None]
        sc = pool_scale[l][None]
        if l == 0:
            (u, zp, q, k, v, za, gl, h), (w_in, w_pu, w_au, w_out) = _rms_inproj(x, g, None, l, gather=weights)
        else:
            (u, zp, q, k, v, za, gl, h), _ = _rms_inproj(x, g, w_in, l)
        o, carry = _attn_fwd(q, k, v, l)
        saved.append((x, g, bg, sc, u, zp, q, k, v, za, gl, h, o, carry))
        if l < n_layers - 1:
            x = _post_fwd(x, u, zp, o, za, gl, bg, pool_w[l], sc, w_pu, w_au, w_out, l)
        else:
            loss, dx, d_final_g = _post_fwd(x, u, zp, o, za, gl, bg, pool_w[l], sc, w_pu, w_au, w_out, l,
                                            head=(final_g[None], target))

    small = [None] * n_layers
    dw_in = dw_out = dw_pu = dw_au = None
    for l in reversed(range(n_layers)):
        x_in, g, bg, sc, u, zp, q, k, v, za, gl, h, o, carry = saved[l]
        (duz, do, dza, dgl, dsc, dbg, merged, dup, dua, y_pool, y_attn, pooled, dmixed) = _post_bwd(
            dx, u, zp, o, za, gl, bg, pool_w[l], sc, w_pu, w_au, w_out, l)
        dq, dk, dv = _attn_bwd(q, k, v, carry, do, l)
        pieces = [(duz, C_U), (dq, C_Q), (dk, C_K), (dv, C_V), (dza, C_ZA), (dgl, C_GL)]
        for p, c0 in pieces:
            dw_in = _wgrad(h, p, f"wgrad_in_l{l}_c{c0}", l, into=dw_in, col0=c0, n_total=IN_WIDTH)
        if l > 0:
            dw_out = _wgrad(merged, dx, f"wgrad_out_l{l}", l, into=dw_out)
        else:
            dw_out, (other_in,) = _wgrad(merged, dx, f"wgrad_out_l{l}", l, into=dw_out, swap=(dw_in,))
        dw_pu = _wgrad(y_pool, dup, f"wgrad_pu_l{l}", l, into=dw_pu)
        dw_au = _wgrad(y_attn, dua, f"wgrad_au_l{l}", l, into=dw_au)
        if l > 0:
            dpw = _pool_wgrad(pooled, dmixed, l)
        else:
            dpw, (other_pu, other_au, other_out) = _pool_wgrad(pooled, dmixed, l, swap=(dw_pu, dw_au, dw_out))
        dx, dg = _inproj_bwd(pieces, w_in, x_in, g, dx, l)
        small[l] = (dg[0], dbg[0], dpw, dsc[0])
    small = [jnp.stack([small[l][i] for l in range(n_layers)]) for i in range(4)]
    return loss[0, 0], dx, d_final_g[0], small, (dw_in, dw_pu, dw_au, dw_out), (other_in, other_pu, other_au, other_out)


SHARDED = ((2, 1280), (2, 256), (2, 256), (1, 256))
ANY = pl.BlockSpec(memory_space=pl.ANY)


def _part(ref, s, axis, width):
    sl = pl.ds(pl.multiple_of(s * width, width), width)
    return ref.at[:, sl] if axis == 2 else ref.at[sl, :]


def _place():
    x, y, c = lax.axis_index("x"), lax.axis_index("y"), lax.axis_index("c")
    return x, y, c, 2 * x + y


def _other_chip(x, y, m):
    px = 1 - x if m & 2 else x
    py = 1 - y if m & 1 else y
    return px, py, 2 * px + py


def _remote(src, dst, send, recv, k, to):
    return pltpu.make_async_remote_copy(src_ref=src, dst_ref=dst, send_sem=send.at[k], recv_sem=recv.at[k],
                                        device_id=to, device_id_type=MESH)


def _part_spec(tr, rows_s, cols_s, axis, width, lead):
    if axis == 2:
        return pl.BlockSpec((None, tr, width), lambda *a: (lead(a), a[-2], a[-1][1]))
    return pl.BlockSpec((None, tr, cols_s), lambda *a: (lead(a), a[-1][1] * (rows_s // tr) + a[-2], 0))


def _cast_into_place(w, pos, axis, width, name):
    L, Rs, Cs = w.shape
    tr = min(256, Rs)
    shape = [L, Rs, Cs]
    shape[axis] *= N_CHIPS

    def body(pos_ref, w_ref, o_ref):
        o_ref[...] = w_ref[...].astype(BF16)

    return pl.pallas_call(
        body, name=name,
        grid_spec=pltpu.PrefetchScalarGridSpec(
            num_scalar_prefetch=1, grid=(L, Rs // tr),
            in_specs=[pl.BlockSpec((None, tr, Cs), lambda l, i, pos: (l, i, 0))],
            out_specs=_part_spec(tr, Rs, Cs, axis, width, lambda a: a[0])),
        out_shape=jax.ShapeDtypeStruct(tuple(shape), BF16),
        compiler_params=_params(),
    )(pos, w)


def _w_in_half(layer):
    def piece(refs, who, shard):
        rows = pl.ds(pl.multiple_of(who * (D_MODEL // 2), D_MODEL // 2), D_MODEL // 2)
        return refs[0].at[layer, rows, pl.ds(pl.multiple_of(shard * SHARDED[0][1], SHARDED[0][1]), SHARDED[0][1])]
    return piece


def _whole_layer(a):
    def piece(refs, who, shard):
        return _part(refs[a].at[who], shard, *SHARDED[a])
    return piece


FIRST_PIECES = (_w_in_half(0),)
LATER_PIECES = (_w_in_half(1), _whole_layer(1), _whole_layer(2), _whole_layer(3))


class _Gather:
    def __init__(self, pieces, refs, send, recv):
        self.pieces, self.refs, self.send, self.recv = pieces, refs, send, recv
        self.x, self.y, self.c, s = _place()
        self.first = []
        for u, piece in enumerate(pieces):
            for m in (1, 2, 3):
                px, py, _ = _other_chip(self.x, self.y, m)
                own = piece(refs, self.c, s)
                self.first.append(_remote(own, own, send, recv, 3 * u + m - 1, (px, py, self.c)))

    def start(self):
        for cp in self.first:
            cp.start()

    def finish(self):
        x, y, c, n = self.x, self.y, self.c, len(self.pieces)

        def landed(u, m, who):
            _, _, sp = _other_chip(x, y, m)
            return self.pieces[u](self.refs, who, sp)

        passed = []
        for m in (1, 2, 3):
            for u in range(n):
                got = landed(u, m, c)
                _remote(got, got, self.send, self.recv, 3 * u + m - 1, (x, y, c)).wait_recv()
                passed.append(_remote(got, got, self.send, self.recv, 3 * n + 3 * u + m - 1, (x, y, 1 - c)))
                passed[-1].start()
        for m in (1, 2, 3):
            for u in range(n):
                got = landed(u, m, 1 - c)
                _remote(got, got, self.send, self.recv, 3 * n + 3 * u + m - 1, (x, y, c)).wait_recv()
        for cp in self.first + passed:
            cp.wait_send()

    @staticmethod
    def semaphores(pieces):
        return [pltpu.SemaphoreType.DMA((6 * len(pieces),)), pltpu.SemaphoreType.DMA((6 * len(pieces),))]


def _gather_first(fulls):
    n = len(fulls)

    def body(*refs):
        gather = _Gather(FIRST_PIECES, refs[n:2 * n], *refs[2 * n:])
        gather.start()
        gather.finish()

    return pl.pallas_call(
        body, name="gather_first",
        in_specs=[ANY] * n, out_specs=[ANY] * n,
        out_shape=[jax.ShapeDtypeStruct(f.shape, f.dtype) for f in fulls],
        input_output_aliases={a: a for a in range(n)},
        scratch_shapes=_Gather.semaphores(FIRST_PIECES),
    )(*fulls)


def _pair_sum(dw, other, pos, name):
    _, R, C = dw.shape
    tr = 128 if C > 1024 else 256

    def body(pos_ref, a_ref, b_ref, o_ref, ob_ref):
        tot = a_ref[...] + b_ref[...]
        o_ref[...] = tot
        ob_ref[...] = tot.astype(BF16)

    blk = pl.BlockSpec((tr, C), lambda i, pos: (i, 0))
    return pl.pallas_call(
        body, name=name,
        grid_spec=pltpu.PrefetchScalarGridSpec(
            num_scalar_prefetch=1, grid=(R // tr,),
            in_specs=[pl.BlockSpec((None, tr, C), lambda i, pos: (pos[0], i, 0)), blk],
            out_specs=[blk, blk]),
        out_shape=[jax.ShapeDtypeStruct((R, C), F32), jax.ShapeDtypeStruct((R, C), BF16)],
        compiler_params=_params(),
    )(pos, dw, other)


def _chip_exchange(ps, packed):
    n = len(ps)

    def body(*refs):
        ins, small_ref = refs[:n], refs[n]
        outs, total_ref = refs[n + 1:2 * n + 1], refs[2 * n + 1]
        all_ref, send, recv, small_send, small_recv = refs[2 * n + 2:]
        x, y, c, _ = _place()
        my_id = 4 * x + 2 * y + c
        all_ref[my_id] = small_ref[...]
        small = []
        for m in range(1, N_DEVICES):
            px = 1 - x if m & 4 else x
            py = 1 - y if m & 2 else y
            pc = 1 - c if m & 1 else c
            cp = _remote(small_ref, all_ref.at[my_id], small_send, small_recv, m - 1, (px, py, pc))
            cp.start()
            small.append((cp, 4 * px + 2 * py + pc))
        copies = []
        for a, (axis, width) in enumerate(SHARDED):
            for m in (1, 2, 3):
                px, py, sp = _other_chip(x, y, m)
                copies.append(_remote(_part(ins[a], sp, axis, width), outs[a].at[m - 1], send, recv, 3 * a + m - 1,
                                      (px, py, c)))
        for cp in copies:
            cp.start()
        for m, (cp, peer_id) in enumerate(small):
            _remote(small_ref, all_ref.at[peer_id], small_send, small_recv, m, (x, y, c)).wait_recv()
        total = all_ref[0]
        for d in range(1, N_DEVICES):
            total = total + all_ref[d]
        total_ref[...] = total
        for cp, _ in small:
            cp.wait_send()
        for cp in copies:
            cp.wait()

    def landing(p, axis, width):
        shape = [3] + list(p.shape)
        shape[axis] = width
        return jax.ShapeDtypeStruct(tuple(shape), p.dtype)

    vmem = pl.BlockSpec(memory_space=pltpu.VMEM)
    res = pl.pallas_call(
        body, name="grad_chip_exchange",
        in_specs=[ANY] * n + [vmem], out_specs=[ANY] * n + [vmem],
        out_shape=[landing(p, axis, width) for p, (axis, width) in zip(ps, SHARDED)]
        + [jax.ShapeDtypeStruct(packed.shape, packed.dtype)],
        scratch_shapes=[pltpu.VMEM((N_DEVICES,) + packed.shape, F32),
                        pltpu.SemaphoreType.DMA((3 * n,)), pltpu.SemaphoreType.DMA((3 * n,)),
                        pltpu.SemaphoreType.DMA((N_DEVICES - 1,)), pltpu.SemaphoreType.DMA((N_DEVICES - 1,))],
        compiler_params=_params(),
    )(*ps, packed)
    return res[:n], res[n]


def _shard_sum(p, landed, pos, axis, width, name):
    _, Rs, Cs = landed.shape
    tr = min(256, Rs)
    p_spec = _part_spec(tr, Rs, Cs, axis, width, lambda a: 0)

    def body(pos_ref, p_ref, l_ref, o_ref):
        o_ref[...] = ((p_ref[...] + l_ref[0].astype(F32)) + l_ref[1].astype(F32)) + l_ref[2].astype(F32)

    return pl.pallas_call(
        body, name=name,
        grid_spec=pltpu.PrefetchScalarGridSpec(
            num_scalar_prefetch=1, grid=(Rs // tr,),
            in_specs=[p_spec, pl.BlockSpec((3, tr, Cs), lambda i, pos: (0, i, 0))],
            out_specs=pl.BlockSpec((None, tr, Cs), lambda i, pos: (pos[0], i, 0))),
        out_shape=jax.ShapeDtypeStruct((2, Rs, Cs), F32),
        compiler_params=_params(),
    )(pos, p[None], landed)


def _sibling_share(gs):
    n = len(gs)

    def body(*refs):
        outs = refs[n:2 * n]
        send, recv = refs[2 * n:]
        x, y, c, _ = _place()
        copies = [_remote(outs[a].at[c], outs[a].at[c], send, recv, a, (x, y, 1 - c)) for a in range(n)]
        for cp in copies:
            cp.start()
        for a, cp in enumerate(copies):
            cp.wait_send()
            _remote(outs[a].at[1 - c], outs[a].at[1 - c], send, recv, a, (x, y, c)).wait_recv()

    return pl.pallas_call(
        body, name="grad_sibling_share",
        in_specs=[ANY] * n, out_specs=[ANY] * n,
        out_shape=[jax.ShapeDtypeStruct(g.shape, g.dtype) for g in gs],
        input_output_aliases={a: a for a in range(n)},
        scratch_shapes=[pltpu.SemaphoreType.DMA((n,)), pltpu.SemaphoreType.DMA((n,))],
    )(*gs)


def _adamw(w, g, m, v, name):
    shape = w.shape
    C = shape[-1]
    flat = [t.reshape(-1, C) for t in (w, g, m, v)]
    R = flat[0].shape[0]
    tr = max(t for t in range(8, R + 1, 8) if R % t == 0 and t * C <= 384 * 1024)

    def body(w_ref, g_ref, m_ref, v_ref, d_ref, nm_ref, nv_ref):
        gv = g_ref[...]
        nm = ADAM_B1 * m_ref[...] + (1.0 - ADAM_B1) * gv
        nv = ADAM_B2 * v_ref[...] + (1.0 - ADAM_B2) * (gv * gv)
        m_hat = nm / (1.0 - ADAM_B1 ** ADAM_STEP)
        v_hat = nv / (1.0 - ADAM_B2 ** ADAM_STEP)
        d_ref[...] = -ADAM_LR * (m_hat / (jnp.sqrt(v_hat) + ADAM_EPS) + ADAM_WD * w_ref[...])
        nm_ref[...] = nm
        nv_ref[...] = nv

    blk = pl.BlockSpec((tr, C), lambda i: (i, 0))
    out = jax.ShapeDtypeStruct((R, C), F32)
    res = pl.pallas_call(
        body, name=name, grid=(R // tr,),
        in_specs=[blk] * 4, out_specs=[blk] * 3, out_shape=[out] * 3,
        compiler_params=_params(),
    )(*flat)
    return [t.reshape(shape) for t in res]


SMALL_SHAPES = ((2, 1024), (2, 2048), (2, 4, 128, 128), (2, 512), (1024,))


def _pack_small(parts):
    return jnp.concatenate([p.reshape(-1, LANES) for p in parts], axis=0)


def _unpack_small(packed):
    out, row = [], 0
    for shape in SMALL_SHAPES:
        n = 1
        for d in shape:
            n *= d
        out.append(packed[row:row + n // LANES].reshape(shape))
        row += n // LANES
    return out


def kernel(x, norm_g, w_in, b_gate, pool_w, pool_scale, w_pool_up, w_attn_up, w_out, final_g, loss_target, m_norm_g, m_w_in, m_b_gate, m_pool_w, m_pool_scale, m_w_pool_up, m_w_attn_up, m_w_out, m_final_g, v_norm_g, v_w_in, v_b_gate, v_pool_w, v_pool_scale, v_w_pool_up, v_w_attn_up, v_w_out, v_final_g):
    _, _, c, s = _place()
    pos = jnp.stack([c, s]).astype(jnp.int32)
    names = ("w_in", "w_pool_up", "w_attn_up", "w_out")

    weights = _gather_first([_cast_into_place(w, pos, axis, width, f"cast_{n}")
                             for w, (axis, width), n in zip((w_in, w_pool_up, w_attn_up, w_out), SHARDED, names)])
    loss_part, dx, d_final_g, small, dws, other = _local_step(x[0], loss_target[0], norm_g, b_gate, pool_w,
                                                              pool_scale, final_g, weights)
    pair =[_pair_sum(d, o, pos, f"grad_pair_sum_{n}") for d, o, n in zip(dws, other, names)]
    landed, summed = _chip_exchange([pb for _, pb in pair],
                                    _pack_small(small + [d_final_g, jnp.broadcast_to(loss_part, (8, LANES))]))
    mine = [_shard_sum(p, l, pos, axis, width, f"grad_shard_sum_{n}")
            for (p, _), l, (axis, width), n in zip(pair, landed, SHARDED, names)]
    g_in, g_pu, g_au, g_out = _sibling_share(mine)
    g_small = _unpack_small(summed)
    loss = summed[-8, 0]
    upd_small = _adamw(_pack_small([norm_g, b_gate, pool_w, pool_scale, final_g]), _pack_small(g_small),
                       _pack_small([m_norm_g, m_b_gate, m_pool_w, m_pool_scale, m_final_g]),
                       _pack_small([v_norm_g, v_b_gate, v_pool_w, v_pool_scale, v_final_g]), "adamw_small")
    d_small, nm_small, nv_small = [_unpack_small(t) for t in upd_small]
    upd_in = _adamw(w_in, g_in, m_w_in, v_w_in, "adamw_w_in")
    upd_pu = _adamw(w_pool_up, g_pu, m_w_pool_up, v_w_pool_up, "adamw_w_pool_up")
    upd_au = _adamw(w_attn_up, g_au, m_w_attn_up, v_w_attn_up, "adamw_w_attn_up")
    upd_out = _adamw(w_out, g_out, m_w_out, v_w_out, "adamw_w_out")

    def ordered(sm, k):
        big = (upd_in[k], upd_pu[k], upd_au[k], upd_out[k]) if k is not None else (g_in, g_pu, g_au, g_out)
        return [sm[0], big[0], sm[1], sm[2], sm[3], big[1], big[2], big[3], sm[4]]

    return (loss, dx[None], *ordered(g_small, None), *ordered(d_small, 0), *ordered(nm_small, 1),
            *ordered(nv_small, 2))
```

```python
import jax
import jax.numpy as jnp
import numpy as np
from jax import lax
from jax.experimental import pallas as pl
from jax.experimental.pallas import tpu as pltpu

F32 = jnp.float32
BF16 = jnp.bfloat16
MESH = pl.DeviceIdType.MESH

D_MODEL = 1024
POOL_WIDTH = 512
POOL_WINDOWS = (2, 4, 8, 16)
POOL_GROUP = 128
POOL_HALO = 16
ATTN_WIDTH = 512
HEAD_DIM = 64
HEAD_PAIRS = 4
IN_WIDTH = 5120
N_CHIPS = 4
N_DEVICES = 8
RMS_EPS = 1e-6
C_U, C_ZP, C_Q, C_K, C_V, C_ZA, C_GL = 0, 512, 1024, 1536, 2048, 2560, 3072

ADAM_LR, ADAM_B1, ADAM_B2, ADAM_EPS, ADAM_WD, ADAM_STEP = 0.001, 0.9, 0.999, 1e-08, 0.01, 10

LANES = 128
ATTN_BLOCK = 256
QUERY_BLOCKS = 4
ROW_TILE = 256
PROJ_ROW_TILE = 512
VMEM_LIMIT = 56 * 1024 * 1024


def _params(**kw):
    return pltpu.CompilerParams(vmem_limit_bytes=VMEM_LIMIT, **kw)


def _nt(a, b):
    return lax.dot_general(a, b, (((1,), (1,)), ((), ())), preferred_element_type=F32)


def _tn(a, b):
    return lax.dot_general(a, b, (((0,), (0,)), ((), ())), preferred_element_type=F32)


def _nn(a, b):
    return jnp.dot(a, b, preferred_element_type=F32)


def _sigmoid(z):
    return 1.0 / (1.0 + jnp.exp(-z))


def _rms_inproj(x, g, w_in, layer, gather=()):
    S = x.shape[0]
    tm = min(PROJ_ROW_TILE, S)
    n_tiles = S // tm
    n_g = len(gather)

    def body(*refs):
        x_ref, g_ref = refs[:2]
        if n_g:
            (u_ref, zp_ref, q_ref, k_ref, v_ref, za_ref, gl_ref, h_ref) = refs[2 + n_g:10 + n_g]
            fulls = refs[10 + n_g:10 + 2 * n_g]
            w_ref, load_sem, send, recv = refs[10 + 2 * n_g:]
            later = _Gather(LATER_PIECES, fulls, send, recv)

            @pl.when(pl.program_id(0) == 0)
            def _():
                load = pltpu.make_async_copy(fulls[0].at[layer], w_ref, load_sem)
                load.start()
                later.start()
                load.wait()

            @pl.when(pl.program_id(0) == n_tiles - 1)
            def _():
                later.finish()
        else:
            w_ref, u_ref, zp_ref, q_ref, k_ref, v_ref, za_ref, gl_ref, h_ref = refs[2:]
        xv = x_ref[...]
        r = lax.rsqrt(jnp.mean(xv * xv, axis=-1, keepdims=True) + RMS_EPS)
        h = ((xv * r) * g_ref[...]).astype(BF16)
        h_ref[...] = h

        def mm(c0, n):
            return _nn(h, w_ref[:, c0:c0 + n])

        u_ref[...] = mm(C_U, 512)
        zp_ref[...] = mm(C_ZP, 512).astype(BF16)
        q_ref[...] = (mm(C_Q, 512) * 0.125).astype(BF16)
        k_ref[...] = mm(C_K, 512).astype(BF16)
        v_ref[...] = mm(C_V, 512).astype(BF16)
        za_ref[...] = mm(C_ZA, 512).astype(BF16)
        for c in range(4):
            gl_ref[:, c * 512:(c + 1) * 512] = mm(C_GL + c * 512, 512).astype(BF16)

    row = lambda n: pl.BlockSpec((tm, n), lambda i: (i, 0))
    sd = lambda n, dt: jax.ShapeDtypeStruct((S, n), dt)
    any_space = pl.BlockSpec(memory_space=pl.ANY)
    weights = [any_space] * n_g if n_g else [_layer_weight_spec(D_MODEL, IN_WIDTH, layer)]
    res = pl.pallas_call(
        body, name=f"rms_inproj_l{layer}", grid=(n_tiles,),
        in_specs=[row(D_MODEL), pl.BlockSpec((1, D_MODEL), lambda i: (0, 0))] + weights,
        out_specs=[row(512), row(512), row(512), row(512), row(512), row(512), row(2048), row(D_MODEL)]
        + [any_space] * n_g,
        out_shape=[sd(512, F32), sd(512, BF16), sd(512, BF16), sd(512, BF16), sd(512, BF16), sd(512, BF16),
                   sd(2048, BF16), sd(D_MODEL, BF16)] + [jax.ShapeDtypeStruct(f.shape, f.dtype) for f in gather],
        input_output_aliases={2 + a: 8 + a for a in range(n_g)},
        scratch_shapes=([pltpu.VMEM((D_MODEL, IN_WIDTH), BF16), pltpu.SemaphoreType.DMA(())]
                        + _Gather.semaphores(LATER_PIECES)) if n_g else [],
        compiler_params=_params(),
    )(x, g, *(gather if n_g else (w_in,)))
    return res[:8], res[8:]


def _tri(n, strict_lower):
    r = lax.broadcasted_iota(jnp.int32, (n, n), 0)
    c = lax.broadcasted_iota(jnp.int32, (n, n), 1)
    return jnp.where(r > c if strict_lower else r < c, 1.0, 0.0).astype(BF16)


def _split_dot(x, m):
    hi = x.astype(BF16)
    lo = (x - hi.astype(F32)).astype(BF16)
    return _nn(hi, m) + _nn(lo, m)


def _log_terms(z):
    lg = jnp.log(1.0 + jnp.exp(-jnp.abs(z)))
    a = jnp.minimum(z, 0.0) - lg
    return a, a - z


EXHAUSTED = -104.0
UNREACHED = -1e30


class _HeadPair:
    def __init__(self, T):
        self.T = T
        self.first = lax.broadcasted_iota(jnp.int32, (T, LANES), 1) < HEAD_DIM
        self.lane = lax.broadcasted_iota(jnp.int32, (2 * T, LANES), 1)
        row = lax.broadcasted_iota(jnp.int32, (2 * T, T), 0)
        row = jnp.where(row >= T, row - T, row)
        self.causal = row > lax.broadcasted_iota(jnp.int32, (2 * T, T), 1)
        self.below = _tri(T, True)

    def stack(self, x2):
        return jnp.concatenate([jnp.where(self.first, x2, 0), jnp.where(self.first, 0, x2)], axis=0).astype(BF16)

    def unstack(self, x):
        return jnp.where(self.first, x[:self.T], x[self.T:])

    def keys(self, ref, blocks):
        T = self.T
        return jnp.concatenate([ref[pl.ds(pl.multiple_of(j * T, T), T), :] for j, _ in blocks], axis=0)

    def log_terms(self, z, blocks):
        T = self.T
        a_all, l_all = _log_terms(z)
        a = [a_all[:, b * T:(b + 1) * T] for b in range(len(blocks))]
        l1m = [l_all[:, b * T:(b + 1) * T] for b in range(len(blocks))]
        return a, [jnp.where(self.causal, l, 0.0) if diagonal else l for l, (_, diagonal) in zip(l1m, blocks)]

    def later_sums(self, l1m):
        later = _split_dot(jnp.concatenate(l1m, axis=0), self.below)
        return [later[2 * self.T * b:2 * self.T * (b + 1)] for b in range(len(l1m))]


def _sections(x, n):
    return x.reshape(n, x.shape[0] // n, x.shape[1])


def _attn_fwd(q, k, v, layer):
    S = q.shape[0]
    T = min(ATTN_BLOCK, S)
    nq = S // T
    jobs = min(QUERY_BLOCKS, nq)
    assert nq <= LANES and nq % jobs == 0
    per_job = nq // jobs

    def body(q_ref, k_ref, v_ref, o_ref, c_ref):
        i = pl.program_id(1)
        pair = _HeadPair(T)
        qs = [pair.stack(q_ref[n]) for n in range(jobs)]
        diag = [i + n * per_job for n in range(jobs)]

        def sweep(jobs):
            kv = [(pair.keys(k_ref, bl), pair.keys(v_ref, bl)) for _, bl, _ in jobs]
            zs = [_nt(qs[n], kcat) for (n, _, _), (kcat, _) in zip(jobs, kv)]
            terms = [pair.log_terms(z, bl) for (_, bl, _), z in zip(jobs, zs)]
            laters = [pair.later_sums(l1m) for _, l1m in terms]
            weights = []
            for (_, bl, (acc, run, saved)), (a, l1m), later in zip(jobs, terms, laters):
                ws = []
                for b, (j, diagonal) in enumerate(bl):
                    saved = jnp.where(pair.lane == j, run, saved)
                    w = jnp.exp(a[b] + later[b] + run)
                    ws.append(jnp.where(pair.causal, w, 0.0) if diagonal else w)
                    run = run + jnp.sum(l1m[b], axis=1, keepdims=True)
                weights.append((jnp.concatenate(ws, axis=1).astype(BF16), acc, run, saved))
            return [(acc + _nn(w, vcat), run, saved) for (w, acc, run, saved), (_, vcat) in zip(weights, kv)]

        def alive(carry):
            return (jnp.max(carry[1]) > EXHAUSTED).astype(jnp.int32)

        def older_blocks(n, carry):
            def older_block(state):
                j, _, c = state
                c = sweep([(n, [(j, False)], c)])[0]
                return j - 1, alive(c), c

            return lax.while_loop(lambda st: jnp.logical_and(st[0] >= 0, st[1] > 0), older_block,
                                  (diag[n] - 2, alive(carry), carry))[2]

        def run(first_blocks):
            init = (jnp.zeros((2 * T, LANES), F32), jnp.zeros((2 * T, 1), F32),
                    jnp.full((2 * T, LANES), UNREACHED, F32))
            carries = sweep([(n, first_blocks[n], init) for n in range(jobs)])
            for n in range(jobs):
                acc, _, saved = older_blocks(n, carries[n])
                o_ref[n] = pair.unstack(acc).astype(BF16)
                c_ref[n, :, :LANES] = saved[:T]
                c_ref[n, :, LANES:] = saved[T:]

        with_previous = lambda d: [(d, True), (d - 1, False)]

        @pl.when(i == 0)
        def _():
            run([[(diag[0], True)]] + [with_previous(d) for d in diag[1:]])

        @pl.when(i > 0)
        def _():
            run([with_previous(d) for d in diag])

    blk = lambda n: pl.BlockSpec((jobs, T, n), lambda p, i: (0, i, p))
    full = pl.BlockSpec((S, LANES), lambda p, i: (0, p))
    o, carry = pl.pallas_call(
        body, name=f"attn_fwd_l{layer}", grid=(HEAD_PAIRS, per_job),
        in_specs=[blk(LANES), full, full],
        out_specs=[blk(LANES), blk(2 * LANES)],
        out_shape=[jax.ShapeDtypeStruct((jobs, S // jobs, ATTN_WIDTH), BF16),
                   jax.ShapeDtypeStruct((jobs, S // jobs, 8 * LANES), F32)],
        compiler_params=_params(),
    )(_sections(q, jobs), k, v)
    return o.reshape(S, ATTN_WIDTH), carry.reshape(S, 8 * LANES)


def _attn_bwd(q, k, v, saved, do, layer):
    S = q.shape[0]
    T = min(ATTN_BLOCK, S)
    nq = S // T
    jobs = min(QUERY_BLOCKS, nq)
    per_job = nq // jobs

    def body(q_ref, k_ref, v_ref, c_ref, do_ref, dq_ref, dk_ref, dv_ref):
        i = pl.program_id(1)

        @pl.when(i == 0)
        def _():
            dk_ref[...] = jnp.zeros_like(dk_ref)
            dv_ref[...] = jnp.zeros_like(dv_ref)

        pair = _HeadPair(T)
        diag = [i + n * per_job for n in range(jobs)]
        qs = [pair.stack(q_ref[n]) for n in range(jobs)]
        dos = [pair.stack(do_ref[n].astype(BF16)) for n in range(jobs)]
        saved = [jnp.concatenate([c_ref[n, :, :LANES], c_ref[n, :, LANES:]], axis=0) for n in range(jobs)]
        before = _tri(T, False)

        def sweep(jobs):
            kv = [(pair.keys(k_ref, bl), pair.keys(v_ref, bl)) for _, bl, _ in jobs]
            zs = [_nt(qs[n], kcat) for (n, _, _), (kcat, _) in zip(jobs, kv)]
            gs = [_nt(dos[n], vcat) for (n, _, _), (_, vcat) in zip(jobs, kv)]
            terms = [pair.log_terms(z, bl) for (_, bl, _), z in zip(jobs, zs)]
            laters = [pair.later_sums(l1m) for _, l1m in terms]
            ws, es = [], []
            for (n, bl, _), (a, _), later, g in zip(jobs, terms, laters, gs):
                w_job, e_job = [], []
                for b, (j, diagonal) in enumerate(bl):
                    run = jnp.sum(jnp.where(pair.lane == j, saved[n], 0.0), axis=1, keepdims=True)
                    w = jnp.exp(a[b] + later[b] + run)
                    w_job.append(jnp.where(pair.causal, w, 0.0) if diagonal else w)
                    e_job.append(w_job[b] * g[:, b * T:(b + 1) * T])
                ws.append(w_job)
                es.append(e_job)
            prefixes = [_nn(jnp.concatenate(e_job, axis=0).astype(BF16), before) for e_job in es]
            dzs, olders = [], []
            for (_, bl, (_, older)), (a, _), e_job, prefix in zip(jobs, terms, es, prefixes):
                dz_job = []
                for b, (j, diagonal) in enumerate(bl):
                    dz = e_job[b] - jnp.exp(a[b]) * (e_job[b] + (prefix[2 * T * b:2 * T * (b + 1)] + older))
                    dz_job.append(jnp.where(pair.causal, dz, 0.0) if diagonal else dz)
                    older = older + jnp.sum(e_job[b], axis=1, keepdims=True)
                dzs.append(jnp.concatenate(dz_job, axis=1).astype(BF16))
                olders.append(older)
            out = []
            for (n, bl, (dq, _)), dz, w_job, older, (kcat, _) in zip(jobs, dzs, ws, olders, kv):
                dk = _tn(dz, qs[n])
                dv = _tn(jnp.concatenate(w_job, axis=1).astype(BF16), dos[n])
                for b, (j, _) in enumerate(bl):
                    rows = pl.ds(pl.multiple_of(j * T, T), T)
                    dk_ref[rows, :] += dk[b * T:(b + 1) * T]
                    dv_ref[rows, :] += dv[b * T:(b + 1) * T]
                out.append((dq + _nn(dz, kcat), older))
            return out

        def older_blocks(n):
            col_max = jnp.max(saved[n], axis=0, keepdims=True)
            lane_row = lax.broadcasted_iota(jnp.int32, (1, LANES), 1)
            reached = jnp.sum(jnp.where(jnp.logical_and(col_max > EXHAUSTED, lane_row < diag[n]), 1, 0))
            init = (jnp.zeros((2 * T, LANES), F32), jnp.zeros((2 * T, 1), F32))
            return lax.fori_loop(diag[n] - reached, diag[n] - 1, lambda j, c: sweep([(n, [(j, False)], c)])[0], init)

        def run(last_blocks):
            carries = sweep([(n, last_blocks[n], older_blocks(n)) for n in range(jobs)])
            for n in range(jobs):
                dq_ref[n] = (pair.unstack(carries[n][0]) * 0.125).astype(BF16)

        with_previous = lambda d: [(d - 1, False), (d, True)]

        @pl.when(i == 0)
        def _():
            run([[(diag[0], True)]] + [with_previous(d) for d in diag[1:]])

        @pl.when(i > 0)
        def _():
            run([with_previous(d) for d in diag])

    blk = lambda n: pl.BlockSpec((jobs, T, n), lambda p, i: (0, i, p))
    full = pl.BlockSpec((S, LANES), lambda p, i: (0, p))
    out = jax.ShapeDtypeStruct((S, ATTN_WIDTH), F32)
    dq, dk, dv = pl.pallas_call(
        body, name=f"attn_bwd_l{layer}", grid=(HEAD_PAIRS, per_job),
        in_specs=[blk(LANES), full, full, blk(2 * LANES), blk(LANES)],
        out_specs=[blk(LANES), full, full],
        out_shape=[jax.ShapeDtypeStruct((jobs, S // jobs, ATTN_WIDTH), BF16), out, out],
        compiler_params=_params(),
    )(_sections(q, jobs), k, v, _sections(saved, jobs), _sections(do, jobs))
    return dq.reshape(S, ATTN_WIDTH), dk, dv


def _pool_counts(row0, tm):
    pos = row0 + lax.broadcasted_iota(jnp.int32, (tm, 1), 0)
    return [1.0 / jnp.minimum(pos + 1, w).astype(F32) for w in POOL_WINDOWS]


def _window_bands(tm, backward):
    t = np.arange(tm)[:, None]
    c = np.arange(tm)[None, :]
    off = c - t if backward else t - c
    main = np.stack([(off >= 0) & (off < w) for w in POOL_WINDOWS])
    r = np.arange(POOL_HALO)[:, None]
    h = np.arange(POOL_HALO)[None, :]
    off = h - r + POOL_HALO if backward else r - h + POOL_HALO
    edge = np.concatenate([(off < w) for w in POOL_WINDOWS])
    return jnp.asarray(main, BF16), jnp.asarray(edge, BF16)


def _window_sums(tile, beside, main_ref, edge_ref, backward):
    tm = tile.shape[0]
    tb = tile.astype(BF16)
    edge = _nn(edge_ref[...], beside.astype(BF16))
    sums = []
    for g in range(len(POOL_WINDOWS)):
        cols = slice(g * POOL_GROUP, (g + 1) * POOL_GROUP)
        tot = _nn(main_ref[g], tb[:, cols])
        extra = edge[g * POOL_HALO:(g + 1) * POOL_HALO, cols]
        if backward:
            sums.append(jnp.concatenate([tot[:tm - POOL_HALO], tot[tm - POOL_HALO:] + extra], axis=0))
        else:
            sums.append(jnp.concatenate([tot[:POOL_HALO] + extra, tot[POOL_HALO:]], axis=0))
    return sums


def _post_forward(u, history, bands, inv_cnt, zp, o, za, gl, bg, pw_ref, scale, wpu_ref, wau_ref):
    pooled, mixed = [], []
    for g, tot in enumerate(_window_sums(u, history, *bands, False)):
        pg = (tot * inv_cnt[g] - u[:, g * POOL_GROUP:(g + 1) * POOL_GROUP]).astype(BF16)
        pooled.append(pg)
        mixed.append(_nn(pg, pw_ref[g].astype(BF16)))
    pooled = jnp.concatenate(pooled, axis=1)
    mixed = jnp.concatenate(mixed, axis=1)
    zp, za, o = zp.astype(F32), za.astype(F32), o.astype(F32)
    sp = _sigmoid(zp)
    sa = _sigmoid(za)
    y_pool = (mixed * scale) * (zp * sp)
    y_attn = o * (za * sa)
    gate = _sigmoid(gl + bg)
    g0, g1 = gate[:, :D_MODEL], gate[:, D_MODEL:]
    up_p = _nn(y_pool.astype(BF16), wpu_ref[...])
    up_a = _nn(y_attn.astype(BF16), wau_ref[...])
    merged = g0 * up_p + g1 * up_a
    return pooled, mixed, sp, sa, y_pool, y_attn, g0, g1, up_p, up_a, merged


def _row_specs(tm, rev, n_tiles):
    tile_of = (lambda i: n_tiles - 1 - i) if rev else (lambda i: i)
    row = lambda n: pl.BlockSpec((tm, n), lambda i: (tile_of(i), 0))
    halo = pl.BlockSpec((POOL_HALO, POOL_WIDTH),
                        lambda i: (jnp.maximum(tile_of(i) * (tm // POOL_HALO) - 1, 0), 0))
    const = lambda shape: pl.BlockSpec(shape, lambda i: (0,) * len(shape))
    return tile_of, row, halo, const


def _layer_weight_spec(rows, cols, layer):
    return pl.BlockSpec((None, rows, cols), lambda i: (layer, 0, 0), pipeline_mode=pl.Buffered(1))


def _post_fwd(x, u, zp, o, za, gl, bg, pw, scale, wpu, wau, wout, layer, head=()):
    S = x.shape[0]
    tm = min(ROW_TILE, S)
    n_tiles = S // tm
    tile_of, row, halo, const = _row_specs(tm, False, n_tiles)

    def body(x_ref, u_ref, uh_ref, main_ref, edge_ref, zp_ref, o_ref, za_ref, gl_ref, bg_ref, pw_ref, sc_ref, wpu_ref,
             wau_ref, wout_ref, *rest):
        i = pl.program_id(0)
        vals = _post_forward(u_ref[...], jnp.where(i == 0, 0.0, uh_ref[...]), (main_ref, edge_ref),
                             _pool_counts(i * tm, tm), zp_ref[...], o_ref[...], za_ref[...], gl_ref[...], bg_ref[...],
                             pw_ref, sc_ref[...], wpu_ref, wau_ref)
        xv = x_ref[...] + _nn(vals[-1].astype(BF16), wout_ref[...])
        if not head:
            rest[0][...] = xv
            return
        gf_ref, t_ref, loss_ref, dx_ref, dg_ref = rest

        @pl.when(i == 0)
        def _():
            loss_ref[...] = jnp.zeros_like(loss_ref)
            dg_ref[...] = jnp.zeros_like(dg_ref)

        r = lax.rsqrt(jnp.mean(xv * xv, axis=-1, keepdims=True) + RMS_EPS)
        diff = (xv * r) * gf_ref[...] - t_ref[...]
        per_row = jnp.mean(diff * diff, axis=-1, keepdims=True)
        loss_ref[...] += 0.5 * jnp.sum(per_row, axis=0, keepdims=True)
        dx, dg_rows = _rms_backward(diff * (1.0 / D_MODEL), xv, r, gf_ref[...])
        dx_ref[...] = dx
        dg_ref[...] += jnp.sum(dg_rows, axis=0, keepdims=True)

    out = jax.ShapeDtypeStruct((S, D_MODEL), F32)
    return pl.pallas_call(
        body, name=f"post_fwd_l{layer}", grid=(n_tiles,),
        in_specs=[row(D_MODEL), row(512), halo, const((4, tm, tm)), const((4 * POOL_HALO, POOL_HALO)), row(512),
                  row(512), row(512), row(2048), const((1, 2048)), const((4, POOL_GROUP, POOL_GROUP)),
                  const((1, POOL_WIDTH)),
                  _layer_weight_spec(POOL_WIDTH, D_MODEL, layer), _layer_weight_spec(ATTN_WIDTH, D_MODEL, layer),
                  _layer_weight_spec(D_MODEL, D_MODEL, layer)] + ([const((1, D_MODEL)), row(D_MODEL)] if head else []),
        out_specs=[const((1, LANES)), row(D_MODEL), const((1, D_MODEL))] if head else row(D_MODEL),
        out_shape=[jax.ShapeDtypeStruct((1, LANES), F32), out, jax.ShapeDtypeStruct((1, D_MODEL), F32)] if head else out,
        compiler_params=_params(),
    )(x, u, u, *_window_bands(tm, False), zp, o, za, gl, bg, pw, scale, wpu, wau, wout, *head)


def _post_bwd(dx, u, zp, o, za, gl, bg, pw, scale, wpu, wau, wout, layer):
    S = dx.shape[0]
    tm = min(ROW_TILE, S)
    n_tiles = S // tm
    tile_of, row, halo, const = _row_specs(tm, True, n_tiles)

    def body(dx_ref, u_ref, uh_ref, main_ref, edge_ref, back_main_ref, back_edge_ref, zp_ref, o_ref, za_ref, gl_ref,
             bg_ref, pw_ref, sc_ref, wpu_ref, wau_ref, wout_ref,
             duz_ref, do_ref, dza_ref, dgl_ref, dsc_ref, dbg_ref,
             merged_ref, dup_ref, dua_ref, yp_ref, ya_ref, pooled_ref, dmixed_ref, nxt_ref):
        step = pl.program_id(0)
        i = tile_of(step)

        @pl.when(step == 0)
        def _():
            dsc_ref[...] = jnp.zeros_like(dsc_ref)
            dbg_ref[...] = jnp.zeros_like(dbg_ref)
            nxt_ref[...] = jnp.zeros_like(nxt_ref)

        inv_cnt = _pool_counts(i * tm, tm)
        zp, za, o = zp_ref[...].astype(F32), za_ref[...].astype(F32), o_ref[...].astype(F32)
        pooled, mixed, sp, sa, y_pool, y_attn, g0, g1, up_p, up_a, merged = _post_forward(
            u_ref[...], jnp.where(i == 0, 0.0, uh_ref[...]), (main_ref, edge_ref), inv_cnt, zp, o, za, gl_ref[...],
            bg_ref[...], pw_ref, sc_ref[...], wpu_ref, wau_ref)
        merged_ref[...] = merged.astype(BF16)
        yp_ref[...] = y_pool.astype(BF16)
        ya_ref[...] = y_attn.astype(BF16)
        pooled_ref[...] = pooled

        dmerged = _nt(dx_ref[...].astype(BF16), wout_ref[...])
        dup = (dmerged * g0).astype(BF16)
        dua = (dmerged * g1).astype(BF16)
        dup_ref[...] = dup
        dua_ref[...] = dua
        dgl0 = (dmerged * up_p) * (g0 * (1.0 - g0))
        dgl1 = (dmerged * up_a) * (g1 * (1.0 - g1))
        dgl_ref[:, :D_MODEL] = dgl0.astype(BF16)
        dgl_ref[:, D_MODEL:] = dgl1.astype(BF16)
        dbg_ref[:, :D_MODEL] += jnp.sum(dgl0, axis=0, keepdims=True)
        dbg_ref[:, D_MODEL:] += jnp.sum(dgl1, axis=0, keepdims=True)

        dy_attn = _nt(dua, wau_ref[...])
        do_ref[...] = (dy_attn * (za * sa)).astype(BF16)
        dza_ref[...] = ((dy_attn * o) * (sa * (1.0 + za * (1.0 - sa)))).astype(BF16)

        dy_pool = _nt(dup, wpu_ref[...])
        ms = mixed * sc_ref[...]
        dms = dy_pool * (zp * sp)
        duz_ref[:, POOL_WIDTH:] = ((dy_pool * ms) * (sp * (1.0 + zp * (1.0 - sp)))).astype(BF16)
        dsc_ref[...] += jnp.sum(dms * mixed, axis=0, keepdims=True)
        dmixed = (dms * sc_ref[...]).astype(BF16)
        dmixed_ref[...] = dmixed
        dpooled = [_nt(dmixed[:, g * POOL_GROUP:(g + 1) * POOL_GROUP], pw_ref[g].astype(BF16)) for g in range(4)]
        scaled = jnp.concatenate([d * inv for d, inv in zip(dpooled, inv_cnt)], axis=1)
        for g, tot in enumerate(_window_sums(scaled, nxt_ref[...], back_main_ref, back_edge_ref, True)):
            duz_ref[:, g * POOL_GROUP:(g + 1) * POOL_GROUP] = (tot - dpooled[g]).astype(BF16)
        nxt_ref[...] = scaled[:POOL_HALO]

    sd = lambda n, dt: jax.ShapeDtypeStruct((S, n), dt)
    bands = [const((4, tm, tm)), const((4 * POOL_HALO, POOL_HALO))]
    return pl.pallas_call(
        body, name=f"post_bwd_l{layer}", grid=(n_tiles,),
        in_specs=[row(D_MODEL), row(512), halo, *bands, *bands, row(512), row(512), row(512), row(2048),
                  const((1, 2048)), const((4, POOL_GROUP, POOL_GROUP)), const((1, POOL_WIDTH)),
                  _layer_weight_spec(POOL_WIDTH, D_MODEL, layer), _layer_weight_spec(ATTN_WIDTH, D_MODEL, layer),
                  _layer_weight_spec(D_MODEL, D_MODEL, layer)],
        out_specs=[row(1024), row(512), row(512), row(2048), const((1, POOL_WIDTH)), const((1, 2048)),
                   row(D_MODEL), row(D_MODEL), row(D_MODEL), row(512), row(512), row(512), row(512)],
        out_shape=[sd(1024, BF16), sd(512, BF16), sd(512, BF16), sd(2048, BF16),
                   jax.ShapeDtypeStruct((1, POOL_WIDTH), F32), jax.ShapeDtypeStruct((1, 2048), F32),
                   sd(D_MODEL, BF16), sd(D_MODEL, BF16), sd(D_MODEL, BF16), sd(512, BF16), sd(512, BF16),
                   sd(512, BF16), sd(512, BF16)],
        scratch_shapes=[pltpu.VMEM((POOL_HALO, POOL_WIDTH), F32)],
        compiler_params=_params(),
    )(dx, u, u, *_window_bands(tm, False), *_window_bands(tm, True), zp, o, za, gl, bg, pw, scale, wpu, wau, wout)


def _rms_backward(dh, xv, r, g):
    xhat = xv * r
    dxhat = dh * g
    return r * (dxhat - xhat * jnp.mean(dxhat * xhat, axis=-1, keepdims=True)), dh * xhat


def _inproj_bwd(pieces, w_in, x, g, dx_res, layer, exchange=()):
    S = x.shape[0]
    tm = min(PROJ_ROW_TILE, S)
    cols = [(c0, p.shape[1]) for p, c0 in pieces]

    def body(ins, outs):
        piece_refs = ins[:len(cols)]
        w_ref, x_ref, g_ref, res_ref = ins[len(cols):]
        dx_ref, dg_ref = outs

        @pl.when(pl.program_id(0) == 0)
        def _():
            dg_ref[...] = jnp.zeros_like(dg_ref)

        dh = jnp.zeros((tm, D_MODEL), F32)
        for p_ref, (c0, n) in zip(piece_refs, cols):
            for c in range(0, n, 512):
                dh = dh + _nt(p_ref[:, c:c + 512].astype(BF16), w_ref[:, c0 + c:c0 + c + 512])
        xv = x_ref[...]
        r = lax.rsqrt(jnp.mean(xv * xv, axis=-1, keepdims=True) + RMS_EPS)
        dx, dg_rows = _rms_backward(dh, xv, r, g_ref[...])
        dx_ref[...] = res_ref[...] + dx
        dg_ref[...] += jnp.sum(dg_rows, axis=0, keepdims=True)

    row = lambda n: pl.BlockSpec((tm, n), lambda i: (i, 0))
    vec = pl.BlockSpec((1, D_MODEL), lambda i: (0, 0))
    any_space = pl.BlockSpec(memory_space=pl.ANY)
    n_x = len(exchange)
    grid = (S // tm,)
    res = pl.pallas_call(
        _with_swap(body, grid, len(cols) + 4, 2, n_x, rider=_ChipExchange), name=f"inproj_bwd_l{layer}", grid=grid,
        in_specs=[row(n) for _, n in cols] + [_layer_weight_spec(D_MODEL, IN_WIDTH, layer), row(D_MODEL), vec,
                                              row(D_MODEL)] + [any_space] * n_x,
        out_specs=[row(D_MODEL), vec] + [any_space] * n_x,
        out_shape=[jax.ShapeDtypeStruct((S, D_MODEL), F32), jax.ShapeDtypeStruct((1, D_MODEL), F32)]
        + _ChipExchange.landing(exchange),
        scratch_shapes=_ChipExchange.semaphores(n_x) if n_x else [],
        compiler_params=_params(),
    )(*[p for p, _ in pieces], w_in, x, g, dx_res, *exchange)
    return res[0], res[1], res[2:]


class _SiblingSwap:
    def __init__(self, mine, theirs, send, recv):
        x, y, c, _ = _place()
        self.copies = [_remote(m.at[1 - c], t, send, recv, a, (x, y, 1 - c))
                       for a, (m, t) in enumerate(zip(mine, theirs))]

    def start(self):
        for cp in self.copies:
            cp.start()

    def wait(self):
        for cp in self.copies:
            cp.wait()


def _with_swap(body, grid, n_in, n_out, n_swap, rider=_SiblingSwap):
    if not n_swap:
        return lambda *refs: body(refs[:n_in], refs[n_in:])

    def riding(*refs):
        ins, mine = refs[:n_in], refs[n_in:n_in + n_swap]
        outs, theirs = refs[n_in + n_swap:n_in + n_swap + n_out], refs[n_in + n_swap + n_out:n_in + 2 * n_swap + n_out]
        swap = rider(mine, theirs, *refs[n_in + 2 * n_swap + n_out:])
        step = [pl.program_id(d) for d in range(len(grid))]
        first, last = step[0] == 0, step[0] == grid[0] - 1
        for d in range(1, len(grid)):
            first, last = jnp.logical_and(first, step[d] == 0), jnp.logical_and(last, step[d] == grid[d] - 1)
        pl.when(first)(swap.start)
        body(ins, outs)
        pl.when(last)(swap.wait)

    return riding


def _swap_specs(swap):
    any_space = pl.BlockSpec(memory_space=pl.ANY)
    shapes = [jax.ShapeDtypeStruct(d.shape[1:], d.dtype) for d in swap]
    sems = [pltpu.SemaphoreType.DMA((len(swap),)), pltpu.SemaphoreType.DMA((len(swap),))] if swap else []
    return [any_space] * len(swap), shapes, sems


def _wgrad(a, b, name, layer, into=None, col0=0, n_total=None, swap=()):
    S, M = a.shape
    N = b.shape[1]
    n_total = N if n_total is None else n_total
    tk = min(2048, S)
    tn = min(512, N)
    grid = (N // tn, S // tk)

    def body(ins, outs):
        prod = _tn(ins[0][...].astype(BF16), ins[1][...].astype(BF16))

        @pl.when(pl.program_id(1) == 0)
        def _():
            outs[0][...] = prod

        @pl.when(pl.program_id(1) > 0)
        def _():
            outs[0][...] += prod

    in_specs = [pl.BlockSpec((tk, M), lambda j, k: (k, 0)), pl.BlockSpec((tk, tn), lambda j, k: (k, j))]
    args = [a, b]
    aliases = {}
    if into is not None:
        in_specs.append(pl.BlockSpec(memory_space=pl.ANY))
        args.append(into)
        aliases = {2: 0}
    swap_specs, swap_shapes, swap_sems = _swap_specs(swap)
    res = pl.pallas_call(
        _with_swap(body, grid, len(args), 1, len(swap)), name=name, grid=grid,
        in_specs=in_specs + swap_specs,
        out_specs=[pl.BlockSpec((None, M, tn), lambda j, k: (layer, 0, col0 // tn + j))] + swap_specs,
        out_shape=[jax.ShapeDtypeStruct((2, M, n_total), F32)] + swap_shapes,
        input_output_aliases=aliases,
        scratch_shapes=swap_sems,
        compiler_params=_params(),
    )(*args, *swap)
    return (res[0], res[1:]) if swap else res[0]


def _pool_wgrad(pooled, dmixed, layer, swap=()):
    S = pooled.shape[0]
    tk = min(1024, S)
    grid = (4, S // tk)

    def body(ins, outs):
        prod = _tn(ins[0][...], ins[1][...])

        @pl.when(pl.program_id(1) == 0)
        def _():
            outs[0][...] = prod

        @pl.when(pl.program_id(1) > 0)
        def _():
            outs[0][...] += prod

    blk = pl.BlockSpec((tk, POOL_GROUP), lambda g, k: (k, g))
    swap_specs, swap_shapes, swap_sems = _swap_specs(swap)
    res = pl.pallas_call(
        _with_swap(body, grid, 2, 1, len(swap)), name=f"pool_wgrad_l{layer}", grid=grid,
        in_specs=[blk, blk] + swap_specs,
        out_specs=[pl.BlockSpec((None, POOL_GROUP, POOL_GROUP), lambda g, k: (g, 0, 0))] + swap_specs,
        out_shape=[jax.ShapeDtypeStruct((4, POOL_GROUP, POOL_GROUP), F32)] + swap_shapes,
        scratch_shapes=swap_sems,
        compiler_params=_params(),
    )(pooled, dmixed, *swap)
    return (res[0], res[1:]) if swap else res[0]


def _local_step(x, target, norm_g, b_gate, pool_w, pool_scale, final_g, weights, pos):
    n_layers = norm_g.shape[0]
    saved = []
    for l in range(n_layers):
        g = norm_g[l][None]
        bg = b_gate[l][None]
        sc = pool_scale[l][None]
        if l == 0:
            (u, zp, q, k, v, za, gl, h), (w_in, w_pu, w_au, w_out) = _rms_inproj(x, g, None, l, gather=weights)
        else:
            (u, zp, q, k, v, za, gl, h), _ = _rms_inproj(x, g, w_in, l)
        o, carry = _attn_fwd(q, k, v, l)
        saved.append((x, g, bg, sc, u, zp, q, k, v, za, gl, h, o, carry))
        if l < n_layers - 1:
            x = _post_fwd(x, u, zp, o, za, gl, bg, pool_w[l], sc, w_pu, w_au, w_out, l)
        else:
            loss, dx, d_final_g = _post_fwd(x, u, zp, o, za, gl, bg, pool_w[l], sc, w_pu, w_au, w_out, l,
                                            head=(final_g[None], target))

    small = [None] * n_layers
    dw_in = dw_out = dw_pu = dw_au = None
    for l in reversed(range(n_layers)):
        x_in, g, bg, sc, u, zp, q, k, v, za, gl, h, o, carry = saved[l]
        (duz, do, dza, dgl, dsc, dbg, merged, dup, dua, y_pool, y_attn, pooled, dmixed) = _post_bwd(
            dx, u, zp, o, za, gl, bg, pool_w[l], sc, w_pu, w_au, w_out, l)
        dq, dk, dv = _attn_bwd(q, k, v, carry, do, l)
        pieces = [(duz, C_U), (dq, C_Q), (dk, C_K), (dv, C_V), (dza, C_ZA), (dgl, C_GL)]
        for p, c0 in pieces:
            dw_in = _wgrad(h, p, f"wgrad_in_l{l}_c{c0}", l, into=dw_in, col0=c0, n_total=IN_WIDTH)
        if l > 0:
            dw_out = _wgrad(merged, dx, f"wgrad_out_l{l}", l, into=dw_out)
        else:
            dw_out, (other_in,) = _wgrad(merged, dx, f"wgrad_out_l{l}", l, into=dw_out, swap=(dw_in,))
        dw_pu = _wgrad(y_pool, dup, f"wgrad_pu_l{l}", l, into=dw_pu)
        dw_au = _wgrad(y_attn, dua, f"wgrad_au_l{l}", l, into=dw_au)
        if l > 0:
            dpw = _pool_wgrad(pooled, dmixed, l)
        else:
            dpw, (other_pu, other_au, other_out) = _pool_wgrad(pooled, dmixed, l, swap=(dw_pu, dw_au, dw_out))
        if l > 0:
            dx, dg, _ = _inproj_bwd(pieces, w_in, x_in, g, dx, l)
        else:
            pair = [_pair_sum(d, t, pos, f"grad_pair_sum_{n}") for d, t, n in
                    zip((dw_in, dw_pu, dw_au, dw_out), (other_in, other_pu, other_au, other_out), SHARDED_NAMES)]
            dx, dg, landed = _inproj_bwd(pieces, w_in, x_in, g, dx, l, exchange=[pb for _, pb in pair])
        small[l] = (dg[0], dbg[0], dpw, dsc[0])
    small = [jnp.stack([small[l][i] for l in range(n_layers)]) for i in range(4)]
    return loss[0, 0], dx, d_final_g[0], small, [p for p, _ in pair], landed


SHARDED = ((2, 1280), (2, 256), (2, 256), (1, 256))
SHARDED_NAMES = ("w_in", "w_pool_up", "w_attn_up", "w_out")
ANY = pl.BlockSpec(memory_space=pl.ANY)


def _part(ref, s, axis, width):
    sl = pl.ds(pl.multiple_of(s * width, width), width)
    return ref.at[:, sl] if axis == 2 else ref.at[sl, :]


def _place():
    x, y, c = lax.axis_index("x"), lax.axis_index("y"), lax.axis_index("c")
    return x, y, c, 2 * x + y


def _other_chip(x, y, m):
    px = 1 - x if m & 2 else x
    py = 1 - y if m & 1 else y
    return px, py, 2 * px + py


def _remote(src, dst, send, recv, k, to):
    return pltpu.make_async_remote_copy(src_ref=src, dst_ref=dst, send_sem=send.at[k], recv_sem=recv.at[k],
                                        device_id=to, device_id_type=MESH)


def _part_spec(tr, rows_s, cols_s, axis, width, lead):
    if axis == 2:
        return pl.BlockSpec((None, tr, width), lambda *a: (lead(a), a[-2], a[-1][1]))
    return pl.BlockSpec((None, tr, cols_s), lambda *a: (lead(a), a[-1][1] * (rows_s // tr) + a[-2], 0))


def _cast_into_place(w, pos, axis, width, name):
    L, Rs, Cs = w.shape
    tr = min(256, Rs)
    shape = [L, Rs, Cs]
    shape[axis] *= N_CHIPS

    def body(pos_ref, w_ref, o_ref):
        o_ref[...] = w_ref[...].astype(BF16)

    return pl.pallas_call(
        body, name=name,
        grid_spec=pltpu.PrefetchScalarGridSpec(
            num_scalar_prefetch=1, grid=(L, Rs // tr),
            in_specs=[pl.BlockSpec((None, tr, Cs), lambda l, i, pos: (l, i, 0))],
            out_specs=_part_spec(tr, Rs, Cs, axis, width, lambda a: a[0])),
        out_shape=jax.ShapeDtypeStruct(tuple(shape), BF16),
        compiler_params=_params(),
    )(pos, w)


def _w_in_half(layer):
    def piece(refs, who, shard):
        rows = pl.ds(pl.multiple_of(who * (D_MODEL // 2), D_MODEL // 2), D_MODEL // 2)
        return refs[0].at[layer, rows, pl.ds(pl.multiple_of(shard * SHARDED[0][1], SHARDED[0][1]), SHARDED[0][1])]
    return piece


def _whole_layer(a):
    def piece(refs, who, shard):
        return _part(refs[a].at[who], shard, *SHARDED[a])
    return piece


FIRST_PIECES = (_w_in_half(0),)
LATER_PIECES = (_w_in_half(1), _whole_layer(1), _whole_layer(2), _whole_layer(3))


class _Gather:
    def __init__(self, pieces, refs, send, recv):
        self.pieces, self.refs, self.send, self.recv = pieces, refs, send, recv
        self.x, self.y, self.c, s = _place()
        self.first = []
        for u, piece in enumerate(pieces):
            for m in (1, 2, 3):
                px, py, _ = _other_chip(self.x, self.y, m)
                own = piece(refs, self.c, s)
                self.first.append(_remote(own, own, send, recv, 3 * u + m - 1, (px, py, self.c)))

    def start(self):
        for cp in self.first:
            cp.start()

    def finish(self):
        x, y, c, n = self.x, self.y, self.c, len(self.pieces)

        def landed(u, m, who):
            _, _, sp = _other_chip(x, y, m)
            return self.pieces[u](self.refs, who, sp)

        passed = []
        for m in (1, 2, 3):
            for u in range(n):
                got = landed(u, m, c)
                _remote(got, got, self.send, self.recv, 3 * u + m - 1, (x, y, c)).wait_recv()
                passed.append(_remote(got, got, self.send, self.recv, 3 * n + 3 * u + m - 1, (x, y, 1 - c)))
                passed[-1].start()
        for m in (1, 2, 3):
            for u in range(n):
                got = landed(u, m, 1 - c)
                _remote(got, got, self.send, self.recv, 3 * n + 3 * u + m - 1, (x, y, c)).wait_recv()
        for cp in self.first + passed:
            cp.wait_send()

    @staticmethod
    def semaphores(pieces):
        return [pltpu.SemaphoreType.DMA((6 * len(pieces),)), pltpu.SemaphoreType.DMA((6 * len(pieces),))]


def _gather_first(fulls):
    n = len(fulls)

    def body(*refs):
        gather = _Gather(FIRST_PIECES, refs[n:2 * n], *refs[2 * n:])
        gather.start()
        gather.finish()

    return pl.pallas_call(
        body, name="gather_first",
        in_specs=[ANY] * n, out_specs=[ANY] * n,
        out_shape=[jax.ShapeDtypeStruct(f.shape, f.dtype) for f in fulls],
        input_output_aliases={a: a for a in range(n)},
        scratch_shapes=_Gather.semaphores(FIRST_PIECES),
    )(*fulls)


def _pair_sum(dw, other, pos, name):
    _, R, C = dw.shape
    tr = 128 if C > 1024 else 256

    def body(pos_ref, a_ref, b_ref, o_ref, ob_ref):
        tot = a_ref[...] + b_ref[...]
        o_ref[...] = tot
        ob_ref[...] = tot.astype(BF16)

    blk = pl.BlockSpec((tr, C), lambda i, pos: (i, 0))
    return pl.pallas_call(
        body, name=name,
        grid_spec=pltpu.PrefetchScalarGridSpec(
            num_scalar_prefetch=1, grid=(R // tr,),
            in_specs=[pl.BlockSpec((None, tr, C), lambda i, pos: (pos[0], i, 0)), blk],
            out_specs=[blk, blk]),
        out_shape=[jax.ShapeDtypeStruct((R, C), F32), jax.ShapeDtypeStruct((R, C), BF16)],
        compiler_params=_params(),
    )(pos, dw, other)


class _ChipExchange:
    def __init__(self, mine, theirs, send, recv):
        x, y, c, _ = _place()
        self.copies = []
        for a, (axis, width) in enumerate(SHARDED):
            for m in (1, 2, 3):
                px, py, sp = _other_chip(x, y, m)
                self.copies.append(_remote(_part(mine[a], sp, axis, width), theirs[a].at[m - 1], send, recv,
                                           3 * a + m - 1, (px, py, c)))

    def start(self):
        for cp in self.copies:
            cp.start()

    def wait(self):
        for cp in self.copies:
            cp.wait()

    @staticmethod
    def landing(ps):
        shapes = []
        for p, (axis, width) in zip(ps, SHARDED):
            shape = [3] + list(p.shape)
            shape[axis] = width
            shapes.append(jax.ShapeDtypeStruct(tuple(shape), p.dtype))
        return shapes

    @staticmethod
    def semaphores(n):
        return [pltpu.SemaphoreType.DMA((3 * n,)), pltpu.SemaphoreType.DMA((3 * n,))]


def _shard_sum(p, landed, pos, axis, width, name):
    _, Rs, Cs = landed.shape
    tr = min(256, Rs)
    p_spec = _part_spec(tr, Rs, Cs, axis, width, lambda a: 0)

    def body(pos_ref, p_ref, l_ref, o_ref):
        o_ref[...] = ((p_ref[...] + l_ref[0].astype(F32)) + l_ref[1].astype(F32)) + l_ref[2].astype(F32)

    return pl.pallas_call(
        body, name=name,
        grid_spec=pltpu.PrefetchScalarGridSpec(
            num_scalar_prefetch=1, grid=(Rs // tr,),
            in_specs=[p_spec, pl.BlockSpec((3, tr, Cs), lambda i, pos: (0, i, 0))],
            out_specs=pl.BlockSpec((None, tr, Cs), lambda i, pos: (pos[0], i, 0))),
        out_shape=jax.ShapeDtypeStruct((2, Rs, Cs), F32),
        compiler_params=_params(),
    )(pos, p[None], landed)


def _final_exchange(gs, packed):
    n = len(gs)

    def body(*refs):
        small_ref = refs[n]
        outs, total_ref = refs[n + 1:2 * n + 1], refs[2 * n + 1]
        all_ref, send, recv, small_send, small_recv = refs[2 * n + 2:]
        x, y, c, _ = _place()
        my_id = 4 * x + 2 * y + c
        all_ref[my_id] = small_ref[...]
        small = []
        for m in range(1, N_DEVICES):
            px = 1 - x if m & 4 else x
            py = 1 - y if m & 2 else y
            pc = 1 - c if m & 1 else c
            cp = _remote(small_ref, all_ref.at[my_id], small_send, small_recv, m - 1, (px, py, pc))
            cp.start()
            small.append((cp, 4 * px + 2 * py + pc))
        copies = [_remote(outs[a].at[c], outs[a].at[c], send, recv, a, (x, y, 1 - c)) for a in range(n)]
        for cp in copies:
            cp.start()
        for m, (cp, peer_id) in enumerate(small):
            _remote(small_ref, all_ref.at[peer_id], small_send, small_recv, m, (x, y, c)).wait_recv()
        total = all_ref[0]
        for d in range(1, N_DEVICES):
            total = total + all_ref[d]
        total_ref[...] = total
        for cp, _ in small:
            cp.wait_send()
        for a, cp in enumerate(copies):
            cp.wait_send()
            _remote(outs[a].at[1 - c], outs[a].at[1 - c], send, recv, a, (x, y, c)).wait_recv()

    vmem = pl.BlockSpec(memory_space=pltpu.VMEM)
    res = pl.pallas_call(
        body, name="final_exchange",
        in_specs=[ANY] * n + [vmem], out_specs=[ANY] * n + [vmem],
        out_shape=[jax.ShapeDtypeStruct(g.shape, g.dtype) for g in gs]
        + [jax.ShapeDtypeStruct(packed.shape, packed.dtype)],
        input_output_aliases={a: a for a in range(n)},
        scratch_shapes=[pltpu.VMEM((N_DEVICES,) + packed.shape, F32),
                        pltpu.SemaphoreType.DMA((n,)), pltpu.SemaphoreType.DMA((n,)),
                        pltpu.SemaphoreType.DMA((N_DEVICES - 1,)), pltpu.SemaphoreType.DMA((N_DEVICES - 1,))],
        compiler_params=_params(),
    )(*gs, packed)
    return res[:n], res[n]


def _adamw(w, g, m, v, name):
    shape = w.shape
    C = shape[-1]
    flat = [t.reshape(-1, C) for t in (w, g, m, v)]
    R = flat[0].shape[0]
    tr = max(t for t in range(8, R + 1, 8) if R % t == 0 and t * C <= 384 * 1024)

    def body(w_ref, g_ref, m_ref, v_ref, d_ref, nm_ref, nv_ref):
        gv = g_ref[...]
        nm = ADAM_B1 * m_ref[...] + (1.0 - ADAM_B1) * gv
        nv = ADAM_B2 * v_ref[...] + (1.0 - ADAM_B2) * (gv * gv)
        m_hat = nm / (1.0 - ADAM_B1 ** ADAM_STEP)
        v_hat = nv / (1.0 - ADAM_B2 ** ADAM_STEP)
        d_ref[...] = -ADAM_LR * (m_hat / (jnp.sqrt(v_hat) + ADAM_EPS) + ADAM_WD * w_ref[...])
        nm_ref[...] = nm
        nv_ref[...] = nv

    blk = pl.BlockSpec((tr, C), lambda i: (i, 0))
    out = jax.ShapeDtypeStruct((R, C), F32)
    res = pl.pallas_call(
        body, name=name, grid=(R // tr,),
        in_specs=[blk] * 4, out_specs=[blk] * 3, out_shape=[out] * 3,
        compiler_params=_params(),
    )(*flat)
    return [t.reshape(shape) for t in res]


SMALL_SHAPES = ((2, 1024), (2, 2048), (2, 4, 128, 128), (2, 512), (1024,))


def _pack_small(parts):
    return jnp.concatenate([p.reshape(-1, LANES) for p in parts], axis=0)


def _unpack_small(packed):
    out, row = [], 0
    for shape in SMALL_SHAPES:
        n = 1
        for d in shape:
            n *= d
        out.append(packed[row:row + n // LANES].reshape(shape))
        row += n // LANES
    return out


def kernel(x, norm_g, w_in, b_gate, pool_w, pool_scale, w_pool_up, w_attn_up, w_out, final_g, loss_target, m_norm_g, m_w_in, m_b_gate, m_pool_w, m_pool_scale, m_w_pool_up, m_w_attn_up, m_w_out, m_final_g, v_norm_g, v_w_in, v_b_gate, v_pool_w, v_pool_scale, v_w_pool_up, v_w_attn_up, v_w_out, v_final_g):
    _, _, c, s = _place()
    pos = jnp.stack([c, s]).astype(jnp.int32)
    names = SHARDED_NAMES

    weights = _gather_first([_cast_into_place(w, pos, axis, width, f"cast_{n}")
                             for w, (axis, width), n in zip((w_in, w_pool_up, w_attn_up, w_out), SHARDED, names)])
    loss_part, dx, d_final_g, small, pair, landed = _local_step(x[0], loss_target[0], norm_g, b_gate, pool_w,
                                                                pool_scale, final_g, weights, pos)
    mine = [_shard_sum(p, l, pos, axis, width, f"grad_shard_sum_{n}")
            for p, l, (axis, width), n in zip(pair, landed, SHARDED, names)]
    (g_in, g_pu, g_au, g_out), summed = _final_exchange(
        mine, _pack_small(small + [d_final_g, jnp.broadcast_to(loss_part, (8, LANES))]))
    g_small = _unpack_small(summed)
    loss = summed[-8, 0]
    upd_small = _adamw(_pack_small([norm_g, b_gate, pool_w, pool_scale, final_g]), _pack_small(g_small),
                       _pack_small([m_norm_g, m_b_gate, m_pool_w, m_pool_scale, m_final_g]),
                       _pack_small([v_norm_g, v_b_gate, v_pool_w, v_pool_scale, v_final_g]), "adamw_small")
    d_small, nm_small, nv_small = [_unpack_small(t) for t in upd_small]
    upd_in = _adamw(w_in, g_in, m_w_in, v_w_in, "adamw_w_in")
    upd_pu = _adamw(w_pool_up, g_pu, m_w_pool_up, v_w_pool_up, "adamw_w_pool_up")
    upd_au = _adamw(w_attn_up, g_au, m_w_attn_up, v_w_attn_up, "adamw_w_attn_up")
    upd_out = _adamw(w_out, g_out, m_w_out, v_w_out, "adamw_w_out")

    def ordered(sm, k):
        big = (upd_in[k], upd_pu[k], upd_au[k], upd_out[k]) if k is not None else (g_in, g_pu, g_au, g_out)
        return [sm[0], big[0], sm[1], sm[2], sm[3], big[1], big[2], big[3], sm[4]]

    return (loss, dx[None], *ordered(g_small, None), *ordered(d_small, 0), *ordered(nm_small, 1),
            *ordered(nv_small, 2))
```

```python
import jax
import jax.numpy as jnp
import numpy as np
from jax import lax
from jax.experimental import pallas as pl
from jax.experimental.pallas import tpu as pltpu

F32 = jnp.float32
BF16 = jnp.bfloat16
MESH = pl.DeviceIdType.MESH

D_MODEL = 1024
POOL_WIDTH = 512
POOL_WINDOWS = (2, 4, 8, 16)
POOL_GROUP = 128
POOL_HALO = 16
ATTN_WIDTH = 512
HEAD_DIM = 64
HEAD_PAIRS = 4
IN_WIDTH = 5120
N_CHIPS = 4
N_DEVICES = 8
RMS_EPS = 1e-6
C_U, C_ZP, C_Q, C_K, C_V, C_ZA, C_GL = 0, 512, 1024, 1536, 2048, 2560, 3072

ADAM_LR, ADAM_B1, ADAM_B2, ADAM_EPS, ADAM_WD, ADAM_STEP = 0.001, 0.9, 0.999, 1e-08, 0.01, 10

LANES = 128
ATTN_BLOCK = 256
QUERY_BLOCKS = 4
ROW_TILE = 256
PROJ_ROW_TILE = 512
VMEM_LIMIT = 56 * 1024 * 1024


def _params(**kw):
    return pltpu.CompilerParams(vmem_limit_bytes=VMEM_LIMIT, **kw)


def _nt(a, b):
    return lax.dot_general(a, b, (((1,), (1,)), ((), ())), preferred_element_type=F32)


def _tn(a, b):
    return lax.dot_general(a, b, (((0,), (0,)), ((), ())), preferred_element_type=F32)


def _nn(a, b):
    return jnp.dot(a, b, preferred_element_type=F32)


def _sigmoid(z):
    return 1.0 / (1.0 + jnp.exp(-z))


def _rms_inproj(x, g, w_in, layer, gather=()):
    S = x.shape[0]
    tm = min(PROJ_ROW_TILE, S)
    n_tiles = S // tm
    n_g = len(gather)

    def body(*refs):
        x_ref, g_ref = refs[:2]
        if n_g:
            (u_ref, zp_ref, q_ref, k_ref, v_ref, za_ref, gl_ref, h_ref) = refs[2 + n_g:10 + n_g]
            fulls = refs[10 + n_g:10 + 2 * n_g]
            w_ref, load_sem, send, recv = refs[10 + 2 * n_g:]
            later = _Gather(LATER_PIECES, fulls, send, recv)

            @pl.when(pl.program_id(0) == 0)
            def _():
                load = pltpu.make_async_copy(fulls[0].at[layer], w_ref, load_sem)
                load.start()
                later.start()
                load.wait()

            @pl.when(pl.program_id(0) == n_tiles - 1)
            def _():
                later.finish()
        else:
            w_ref, u_ref, zp_ref, q_ref, k_ref, v_ref, za_ref, gl_ref, h_ref = refs[2:]
        xv = x_ref[...]
        r = lax.rsqrt(jnp.mean(xv * xv, axis=-1, keepdims=True) + RMS_EPS)
        h = ((xv * r) * g_ref[...]).astype(BF16)
        h_ref[...] = h

        def mm(c0, n):
            return _nn(h, w_ref[:, c0:c0 + n])

        u_ref[...] = mm(C_U, 512)
        zp_ref[...] = mm(C_ZP, 512).astype(BF16)
        q_ref[...] = (mm(C_Q, 512) * 0.125).astype(BF16)
        k_ref[...] = mm(C_K, 512).astype(BF16)
        v_ref[...] = mm(C_V, 512).astype(BF16)
        za_ref[...] = mm(C_ZA, 512).astype(BF16)
        for c in range(4):
            gl_ref[:, c * 512:(c + 1) * 512] = mm(C_GL + c * 512, 512).astype(BF16)

    row = lambda n: pl.BlockSpec((tm, n), lambda i: (i, 0))
    sd = lambda n, dt: jax.ShapeDtypeStruct((S, n), dt)
    any_space = pl.BlockSpec(memory_space=pl.ANY)
    weights = [any_space] * n_g if n_g else [_layer_weight_spec(D_MODEL, IN_WIDTH, layer)]
    res = pl.pallas_call(
        body, name=f"rms_inproj_l{layer}", grid=(n_tiles,),
        in_specs=[row(D_MODEL), pl.BlockSpec((1, D_MODEL), lambda i: (0, 0))] + weights,
        out_specs=[row(512), row(512), row(512), row(512), row(512), row(512), row(2048), row(D_MODEL)]
        + [any_space] * n_g,
        out_shape=[sd(512, F32), sd(512, BF16), sd(512, BF16), sd(512, BF16), sd(512, BF16), sd(512, BF16),
                   sd(2048, BF16), sd(D_MODEL, BF16)] + [jax.ShapeDtypeStruct(f.shape, f.dtype) for f in gather],
        input_output_aliases={2 + a: 8 + a for a in range(n_g)},
        scratch_shapes=([pltpu.VMEM((D_MODEL, IN_WIDTH), BF16), pltpu.SemaphoreType.DMA(())]
                        + _Gather.semaphores(LATER_PIECES)) if n_g else [],
        compiler_params=_params(),
    )(x, g, *(gather if n_g else (w_in,)))
    return res[:8], res[8:]


def _tri(n, strict_lower):
    r = lax.broadcasted_iota(jnp.int32, (n, n), 0)
    c = lax.broadcasted_iota(jnp.int32, (n, n), 1)
    return jnp.where(r > c if strict_lower else r < c, 1.0, 0.0).astype(BF16)


def _split_dot(x, m):
    hi = x.astype(BF16)
    lo = (x - hi.astype(F32)).astype(BF16)
    return _nn(hi, m) + _nn(lo, m)


def _log_terms(z):
    lg = jnp.log(1.0 + jnp.exp(-jnp.abs(z)))
    a = jnp.minimum(z, 0.0) - lg
    return a, a - z


EXHAUSTED = -104.0
UNREACHED = -1e30


class _HeadPair:
    def __init__(self, T):
        self.T = T
        self.first = lax.broadcasted_iota(jnp.int32, (T, LANES), 1) < HEAD_DIM
        self.lane = lax.broadcasted_iota(jnp.int32, (2 * T, LANES), 1)
        row = lax.broadcasted_iota(jnp.int32, (2 * T, T), 0)
        row = jnp.where(row >= T, row - T, row)
        self.causal = row > lax.broadcasted_iota(jnp.int32, (2 * T, T), 1)
        self.below = _tri(T, True)

    def stack(self, x2):
        return jnp.concatenate([jnp.where(self.first, x2, 0), jnp.where(self.first, 0, x2)], axis=0).astype(BF16)

    def unstack(self, x):
        return jnp.where(self.first, x[:self.T], x[self.T:])

    def keys(self, ref, blocks):
        T = self.T
        return jnp.concatenate([ref[pl.ds(pl.multiple_of(j * T, T), T), :] for j, _ in blocks], axis=0)

    def log_terms(self, z, blocks):
        T = self.T
        a_all, l_all = _log_terms(z)
        a = [a_all[:, b * T:(b + 1) * T] for b in range(len(blocks))]
        l1m = [l_all[:, b * T:(b + 1) * T] for b in range(len(blocks))]
        return a, [jnp.where(self.causal, l, 0.0) if diagonal else l for l, (_, diagonal) in zip(l1m, blocks)]

    def later_sums(self, l1m):
        later = _split_dot(jnp.concatenate(l1m, axis=0), self.below)
        return [later[2 * self.T * b:2 * self.T * (b + 1)] for b in range(len(l1m))]


def _sections(x, n):
    return x.reshape(n, x.shape[0] // n, x.shape[1])


def _attn_fwd(q, k, v, layer):
    S = q.shape[0]
    T = min(ATTN_BLOCK, S)
    nq = S // T
    jobs = min(QUERY_BLOCKS, nq)
    assert nq <= LANES and nq % jobs == 0
    per_job = nq // jobs

    def body(q_ref, k_ref, v_ref, o_ref, c_ref):
        i = pl.program_id(1)
        pair = _HeadPair(T)
        qs = [pair.stack(q_ref[n]) for n in range(jobs)]
        diag = [i + n * per_job for n in range(jobs)]

        def sweep(jobs):
            kv = [(pair.keys(k_ref, bl), pair.keys(v_ref, bl)) for _, bl, _ in jobs]
            zs = [_nt(qs[n], kcat) for (n, _, _), (kcat, _) in zip(jobs, kv)]
            terms = [pair.log_terms(z, bl) for (_, bl, _), z in zip(jobs, zs)]
            laters = [pair.later_sums(l1m) for _, l1m in terms]
            weights = []
            for (_, bl, (acc, run, saved)), (a, l1m), later in zip(jobs, terms, laters):
                ws = []
                for b, (j, diagonal) in enumerate(bl):
                    saved = jnp.where(pair.lane == j, run, saved)
                    w = jnp.exp(a[b] + later[b] + run)
                    ws.append(jnp.where(pair.causal, w, 0.0) if diagonal else w)
                    run = run + jnp.sum(l1m[b], axis=1, keepdims=True)
                weights.append((jnp.concatenate(ws, axis=1).astype(BF16), acc, run, saved))
            return [(acc + _nn(w, vcat), run, saved) for (w, acc, run, saved), (_, vcat) in zip(weights, kv)]

        def alive(carry):
            return (jnp.max(carry[1]) > EXHAUSTED).astype(jnp.int32)

        def older_blocks(n, carry):
            def older_block(state):
                j, _, c = state
                c = sweep([(n, [(j, False)], c)])[0]
                return j - 1, alive(c), c

            return lax.while_loop(lambda st: jnp.logical_and(st[0] >= 0, st[1] > 0), older_block,
                                  (diag[n] - 2, alive(carry), carry))[2]

        def run(first_blocks):
            init = (jnp.zeros((2 * T, LANES), F32), jnp.zeros((2 * T, 1), F32),
                    jnp.full((2 * T, LANES), UNREACHED, F32))
            carries = sweep([(n, first_blocks[n], init) for n in range(jobs)])
            for n in range(jobs):
                acc, _, saved = older_blocks(n, carries[n])
                o_ref[n] = pair.unstack(acc).astype(BF16)
                c_ref[n, :, :LANES] = saved[:T]
                c_ref[n, :, LANES:] = saved[T:]

        with_previous = lambda d: [(d, True), (d - 1, False)]

        @pl.when(i == 0)
        def _():
            run([[(diag[0], True)]] + [with_previous(d) for d in diag[1:]])

        @pl.when(i > 0)
        def _():
            run([with_previous(d) for d in diag])

    blk = lambda n: pl.BlockSpec((jobs, T, n), lambda p, i: (0, i, p))
    full = pl.BlockSpec((S, LANES), lambda p, i: (0, p))
    o, carry = pl.pallas_call(
        body, name=f"attn_fwd_l{layer}", grid=(HEAD_PAIRS, per_job),
        in_specs=[blk(LANES), full, full],
        out_specs=[blk(LANES), blk(2 * LANES)],
        out_shape=[jax.ShapeDtypeStruct((jobs, S // jobs, ATTN_WIDTH), BF16),
                   jax.ShapeDtypeStruct((jobs, S // jobs, 8 * LANES), F32)],
        compiler_params=_params(),
    )(_sections(q, jobs), k, v)
    return o.reshape(S, ATTN_WIDTH), carry.reshape(S, 8 * LANES)


def _attn_bwd(q, k, v, saved, do, layer):
    S = q.shape[0]
    T = min(ATTN_BLOCK, S)
    nq = S // T
    jobs = min(QUERY_BLOCKS, nq)
    per_job = nq // jobs

    def body(q_ref, k_ref, v_ref, c_ref, do_ref, dq_ref, dk_ref, dv_ref):
        i = pl.program_id(1)

        @pl.when(i == 0)
        def _():
            dk_ref[...] = jnp.zeros_like(dk_ref)
            dv_ref[...] = jnp.zeros_like(dv_ref)

        pair = _HeadPair(T)
        diag = [i + n * per_job for n in range(jobs)]
        qs = [pair.stack(q_ref[n]) for n in range(jobs)]
        dos = [pair.stack(do_ref[n].astype(BF16)) for n in range(jobs)]
        saved = [jnp.concatenate([c_ref[n, :, :LANES], c_ref[n, :, LANES:]], axis=0) for n in range(jobs)]
        before = _tri(T, False)

        def sweep(jobs):
            kv = [(pair.keys(k_ref, bl), pair.keys(v_ref, bl)) for _, bl, _ in jobs]
            zs = [_nt(qs[n], kcat) for (n, _, _), (kcat, _) in zip(jobs, kv)]
            gs = [_nt(dos[n], vcat) for (n, _, _), (_, vcat) in zip(jobs, kv)]
            terms = [pair.log_terms(z, bl) for (_, bl, _), z in zip(jobs, zs)]
            laters = [pair.later_sums(l1m) for _, l1m in terms]
            ws, es = [], []
            for (n, bl, _), (a, _), later, g in zip(jobs, terms, laters, gs):
                w_job, e_job = [], []
                for b, (j, diagonal) in enumerate(bl):
                    run = jnp.sum(jnp.where(pair.lane == j, saved[n], 0.0), axis=1, keepdims=True)
                    w = jnp.exp(a[b] + later[b] + run)
                    w_job.append(jnp.where(pair.causal, w, 0.0) if diagonal else w)
                    e_job.append(w_job[b] * g[:, b * T:(b + 1) * T])
                ws.append(w_job)
                es.append(e_job)
            prefixes = [_nn(jnp.concatenate(e_job, axis=0).astype(BF16), before) for e_job in es]
            dzs, olders = [], []
            for (_, bl, (_, older)), (a, _), e_job, prefix in zip(jobs, terms, es, prefixes):
                dz_job = []
                for b, (j, diagonal) in enumerate(bl):
                    dz = e_job[b] - jnp.exp(a[b]) * (e_job[b] + (prefix[2 * T * b:2 * T * (b + 1)] + older))
                    dz_job.append(jnp.where(pair.causal, dz, 0.0) if diagonal else dz)
                    older = older + jnp.sum(e_job[b], axis=1, keepdims=True)
                dzs.append(jnp.concatenate(dz_job, axis=1).astype(BF16))
                olders.append(older)
            out = []
            for (n, bl, (dq, _)), dz, w_job, older, (kcat, _) in zip(jobs, dzs, ws, olders, kv):
                dk = _tn(dz, qs[n])
                dv = _tn(jnp.concatenate(w_job, axis=1).astype(BF16), dos[n])
                for b, (j, _) in enumerate(bl):
                    rows = pl.ds(pl.multiple_of(j * T, T), T)
                    dk_ref[rows, :] += dk[b * T:(b + 1) * T]
                    dv_ref[rows, :] += dv[b * T:(b + 1) * T]
                out.append((dq + _nn(dz, kcat), older))
            return out

        def older_blocks(n):
            col_max = jnp.max(saved[n], axis=0, keepdims=True)
            lane_row = lax.broadcasted_iota(jnp.int32, (1, LANES), 1)
            reached = jnp.sum(jnp.where(jnp.logical_and(col_max > EXHAUSTED, lane_row < diag[n]), 1, 0))
            init = (jnp.zeros((2 * T, LANES), F32), jnp.zeros((2 * T, 1), F32))
            return lax.fori_loop(diag[n] - reached, diag[n] - 1, lambda j, c: sweep([(n, [(j, False)], c)])[0], init)

        def run(last_blocks):
            carries = sweep([(n, last_blocks[n], older_blocks(n)) for n in range(jobs)])
            for n in range(jobs):
                dq_ref[n] = (pair.unstack(carries[n][0]) * 0.125).astype(BF16)

        with_previous = lambda d: [(d - 1, False), (d, True)]

        @pl.when(i == 0)
        def _():
            run([[(diag[0], True)]] + [with_previous(d) for d in diag[1:]])

        @pl.when(i > 0)
        def _():
            run([with_previous(d) for d in diag])

    blk = lambda n: pl.BlockSpec((jobs, T, n), lambda p, i: (0, i, p))
    full = pl.BlockSpec((S, LANES), lambda p, i: (0, p))
    out = jax.ShapeDtypeStruct((S, ATTN_WIDTH), F32)
    dq, dk, dv = pl.pallas_call(
        body, name=f"attn_bwd_l{layer}", grid=(HEAD_PAIRS, per_job),
        in_specs=[blk(LANES), full, full, blk(2 * LANES), blk(LANES)],
        out_specs=[blk(LANES), full, full],
        out_shape=[jax.ShapeDtypeStruct((jobs, S // jobs, ATTN_WIDTH), BF16), out, out],
        compiler_params=_params(),
    )(_sections(q, jobs), k, v, _sections(saved, jobs), _sections(do, jobs))
    return dq.reshape(S, ATTN_WIDTH), dk, dv


def _pool_counts(row0, tm):
    pos = row0 + lax.broadcasted_iota(jnp.int32, (tm, 1), 0)
    return [1.0 / jnp.minimum(pos + 1, w).astype(F32) for w in POOL_WINDOWS]


def _window_bands(tm, backward):
    t = np.arange(tm)[:, None]
    c = np.arange(tm)[None, :]
    off = c - t if backward else t - c
    main = np.stack([(off >= 0) & (off < w) for w in POOL_WINDOWS])
    r = np.arange(POOL_HALO)[:, None]
    h = np.arange(POOL_HALO)[None, :]
    off = h - r + POOL_HALO if backward else r - h + POOL_HALO
    edge = np.concatenate([(off < w) for w in POOL_WINDOWS])
    return jnp.asarray(main, BF16), jnp.asarray(edge, BF16)


def _window_sums(tile, beside, main_ref, edge_ref, backward):
    tm = tile.shape[0]
    tb = tile.astype(BF16)
    edge = _nn(edge_ref[...], beside.astype(BF16))
    sums = []
    for g in range(len(POOL_WINDOWS)):
        cols = slice(g * POOL_GROUP, (g + 1) * POOL_GROUP)
        tot = _nn(main_ref[g], tb[:, cols])
        extra = edge[g * POOL_HALO:(g + 1) * POOL_HALO, cols]
        if backward:
            sums.append(jnp.concatenate([tot[:tm - POOL_HALO], tot[tm - POOL_HALO:] + extra], axis=0))
        else:
            sums.append(jnp.concatenate([tot[:POOL_HALO] + extra, tot[POOL_HALO:]], axis=0))
    return sums


def _post_forward(u, history, bands, inv_cnt, zp, o, za, gl, bg, pw_ref, scale, wpu_ref, wau_ref):
    pooled, mixed = [], []
    for g, tot in enumerate(_window_sums(u, history, *bands, False)):
        pg = (tot * inv_cnt[g] - u[:, g * POOL_GROUP:(g + 1) * POOL_GROUP]).astype(BF16)
        pooled.append(pg)
        mixed.append(_nn(pg, pw_ref[g].astype(BF16)))
    pooled = jnp.concatenate(pooled, axis=1)
    mixed = jnp.concatenate(mixed, axis=1)
    zp, za, o = zp.astype(F32), za.astype(F32), o.astype(F32)
    sp = _sigmoid(zp)
    sa = _sigmoid(za)
    y_pool = (mixed * scale) * (zp * sp)
    y_attn = o * (za * sa)
    gate = _sigmoid(gl + bg)
    g0, g1 = gate[:, :D_MODEL], gate[:, D_MODEL:]
    up_p = _nn(y_pool.astype(BF16), wpu_ref[...])
    up_a = _nn(y_attn.astype(BF16), wau_ref[...])
    merged = g0 * up_p + g1 * up_a
    return pooled, mixed, sp, sa, y_pool, y_attn, g0, g1, up_p, up_a, merged


def _row_specs(tm, rev, n_tiles):
    tile_of = (lambda i: n_tiles - 1 - i) if rev else (lambda i: i)
    row = lambda n: pl.BlockSpec((tm, n), lambda i: (tile_of(i), 0))
    halo = pl.BlockSpec((POOL_HALO, POOL_WIDTH),
                        lambda i: (jnp.maximum(tile_of(i) * (tm // POOL_HALO) - 1, 0), 0))
    const = lambda shape: pl.BlockSpec(shape, lambda i: (0,) * len(shape))
    return tile_of, row, halo, const


def _layer_weight_spec(rows, cols, layer):
    return pl.BlockSpec((None, rows, cols), lambda i: (layer, 0, 0), pipeline_mode=pl.Buffered(1))


def _post_fwd(x, u, zp, o, za, gl, bg, pw, scale, wpu, wau, wout, layer, head=()):
    S = x.shape[0]
    tm = min(ROW_TILE, S)
    n_tiles = S // tm
    tile_of, row, halo, const = _row_specs(tm, False, n_tiles)

    def body(x_ref, u_ref, uh_ref, main_ref, edge_ref, zp_ref, o_ref, za_ref, gl_ref, bg_ref, pw_ref, sc_ref, wpu_ref,
             wau_ref, wout_ref, *rest):
        i = pl.program_id(0)
        vals = _post_forward(u_ref[...], jnp.where(i == 0, 0.0, uh_ref[...]), (main_ref, edge_ref),
                             _pool_counts(i * tm, tm), zp_ref[...], o_ref[...], za_ref[...], gl_ref[...], bg_ref[...],
                             pw_ref, sc_ref[...], wpu_ref, wau_ref)
        xv = x_ref[...] + _nn(vals[-1].astype(BF16), wout_ref[...])
        if not head:
            rest[0][...] = xv
            return
        gf_ref, t_ref, loss_ref, dx_ref, dg_ref = rest

        @pl.when(i == 0)
        def _():
            loss_ref[...] = jnp.zeros_like(loss_ref)
            dg_ref[...] = jnp.zeros_like(dg_ref)

        r = lax.rsqrt(jnp.mean(xv * xv, axis=-1, keepdims=True) + RMS_EPS)
        diff = (xv * r) * gf_ref[...] - t_ref[...]
        per_row = jnp.mean(diff * diff, axis=-1, keepdims=True)
        loss_ref[...] += 0.5 * jnp.sum(per_row, axis=0, keepdims=True)
        dx, dg_rows = _rms_backward(diff * (1.0 / D_MODEL), xv, r, gf_ref[...])
        dx_ref[...] = dx
        dg_ref[...] += jnp.sum(dg_rows, axis=0, keepdims=True)

    out = jax.ShapeDtypeStruct((S, D_MODEL), F32)
    return pl.pallas_call(
        body, name=f"post_fwd_l{layer}", grid=(n_tiles,),
        in_specs=[row(D_MODEL), row(512), halo, const((4, tm, tm)), const((4 * POOL_HALO, POOL_HALO)), row(512),
                  row(512), row(512), row(2048), const((1, 2048)), const((4, POOL_GROUP, POOL_GROUP)),
                  const((1, POOL_WIDTH)),
                  _layer_weight_spec(POOL_WIDTH, D_MODEL, layer), _layer_weight_spec(ATTN_WIDTH, D_MODEL, layer),
                  _layer_weight_spec(D_MODEL, D_MODEL, layer)] + ([const((1, D_MODEL)), row(D_MODEL)] if head else []),
        out_specs=[const((1, LANES)), row(D_MODEL), const((1, D_MODEL))] if head else row(D_MODEL),
        out_shape=[jax.ShapeDtypeStruct((1, LANES), F32), out, jax.ShapeDtypeStruct((1, D_MODEL), F32)] if head else out,
        compiler_params=_params(),
    )(x, u, u, *_window_bands(tm, False), zp, o, za, gl, bg, pw, scale, wpu, wau, wout, *head)


def _post_bwd(dx, u, zp, o, za, gl, bg, pw, scale, wpu, wau, wout, layer):
    S = dx.shape[0]
    tm = min(ROW_TILE, S)
    n_tiles = S // tm
    tile_of, row, halo, const = _row_specs(tm, True, n_tiles)

    def body(dx_ref, u_ref, uh_ref, main_ref, edge_ref, back_main_ref, back_edge_ref, zp_ref, o_ref, za_ref, gl_ref,
             bg_ref, pw_ref, sc_ref, wpu_ref, wau_ref, wout_ref,
             duz_ref, do_ref, dzg_ref, dsc_ref, dbg_ref,
             merged_ref, dup_ref, dua_ref, yp_ref, ya_ref, pooled_ref, dmixed_ref, nxt_ref):
        step = pl.program_id(0)
        i = tile_of(step)

        @pl.when(step == 0)
        def _():
            dsc_ref[...] = jnp.zeros_like(dsc_ref)
            dbg_ref[...] = jnp.zeros_like(dbg_ref)
            nxt_ref[...] = jnp.zeros_like(nxt_ref)

        inv_cnt = _pool_counts(i * tm, tm)
        zp, za, o = zp_ref[...].astype(F32), za_ref[...].astype(F32), o_ref[...].astype(F32)
        pooled, mixed, sp, sa, y_pool, y_attn, g0, g1, up_p, up_a, merged = _post_forward(
            u_ref[...], jnp.where(i == 0, 0.0, uh_ref[...]), (main_ref, edge_ref), inv_cnt, zp, o, za, gl_ref[...],
            bg_ref[...], pw_ref, sc_ref[...], wpu_ref, wau_ref)
        merged_ref[...] = merged.astype(BF16)
        yp_ref[...] = y_pool.astype(BF16)
        ya_ref[...] = y_attn.astype(BF16)
        pooled_ref[...] = pooled

        dmerged = _nt(dx_ref[...].astype(BF16), wout_ref[...])
        dup = (dmerged * g0).astype(BF16)
        dua = (dmerged * g1).astype(BF16)
        dup_ref[...] = dup
        dua_ref[...] = dua
        dgl0 = (dmerged * up_p) * (g0 * (1.0 - g0))
        dgl1 = (dmerged * up_a) * (g1 * (1.0 - g1))
        dzg_ref[:, ATTN_WIDTH:ATTN_WIDTH + D_MODEL] = dgl0.astype(BF16)
        dzg_ref[:, ATTN_WIDTH + D_MODEL:] = dgl1.astype(BF16)
        dbg_ref[:, :D_MODEL] += jnp.sum(dgl0, axis=0, keepdims=True)
        dbg_ref[:, D_MODEL:] += jnp.sum(dgl1, axis=0, keepdims=True)

        dy_attn = _nt(dua, wau_ref[...])
        do_ref[...] = (dy_attn * (za * sa)).astype(BF16)
        dzg_ref[:, :ATTN_WIDTH] = ((dy_attn * o) * (sa * (1.0 + za * (1.0 - sa)))).astype(BF16)

        dy_pool = _nt(dup, wpu_ref[...])
        ms = mixed * sc_ref[...]
        dms = dy_pool * (zp * sp)
        duz_ref[:, POOL_WIDTH:] = ((dy_pool * ms) * (sp * (1.0 + zp * (1.0 - sp)))).astype(BF16)
        dsc_ref[...] += jnp.sum(dms * mixed, axis=0, keepdims=True)
        dmixed = (dms * sc_ref[...]).astype(BF16)
        dmixed_ref[...] = dmixed
        dpooled = [_nt(dmixed[:, g * POOL_GROUP:(g + 1) * POOL_GROUP], pw_ref[g].astype(BF16)) for g in range(4)]
        scaled = jnp.concatenate([d * inv for d, inv in zip(dpooled, inv_cnt)], axis=1)
        for g, tot in enumerate(_window_sums(scaled, nxt_ref[...], back_main_ref, back_edge_ref, True)):
            duz_ref[:, g * POOL_GROUP:(g + 1) * POOL_GROUP] = (tot - dpooled[g]).astype(BF16)
        nxt_ref[...] = scaled[:POOL_HALO]

    sd = lambda n, dt: jax.ShapeDtypeStruct((S, n), dt)
    bands = [const((4, tm, tm)), const((4 * POOL_HALO, POOL_HALO))]
    return pl.pallas_call(
        body, name=f"post_bwd_l{layer}", grid=(n_tiles,),
        in_specs=[row(D_MODEL), row(512), halo, *bands, *bands, row(512), row(512), row(512), row(2048),
                  const((1, 2048)), const((4, POOL_GROUP, POOL_GROUP)), const((1, POOL_WIDTH)),
                  _layer_weight_spec(POOL_WIDTH, D_MODEL, layer), _layer_weight_spec(ATTN_WIDTH, D_MODEL, layer),
                  _layer_weight_spec(D_MODEL, D_MODEL, layer)],
        out_specs=[row(1024), row(512), row(2560), const((1, POOL_WIDTH)), const((1, 2048)),
                   row(D_MODEL), row(D_MODEL), row(D_MODEL), row(512), row(512), row(512), row(512)],
        out_shape=[sd(1024, BF16), sd(512, BF16), sd(2560, BF16),
                   jax.ShapeDtypeStruct((1, POOL_WIDTH), F32), jax.ShapeDtypeStruct((1, 2048), F32),
                   sd(D_MODEL, BF16), sd(D_MODEL, BF16), sd(D_MODEL, BF16), sd(512, BF16), sd(512, BF16),
                   sd(512, BF16), sd(512, BF16)],
        scratch_shapes=[pltpu.VMEM((POOL_HALO, POOL_WIDTH), F32)],
        compiler_params=_params(),
    )(dx, u, u, *_window_bands(tm, False), *_window_bands(tm, True), zp, o, za, gl, bg, pw, scale, wpu, wau, wout)


def _rms_backward(dh, xv, r, g):
    xhat = xv * r
    dxhat = dh * g
    return r * (dxhat - xhat * jnp.mean(dxhat * xhat, axis=-1, keepdims=True)), dh * xhat


def _inproj_bwd(pieces, w_in, x, g, dx_res, layer, exchange=()):
    S = x.shape[0]
    tm = min(PROJ_ROW_TILE, S)
    cols = [(c0, p.shape[1]) for p, c0 in pieces]

    def body(ins, outs):
        piece_refs = ins[:len(cols)]
        w_ref, x_ref, g_ref, res_ref = ins[len(cols):]
        dx_ref, dg_ref = outs

        @pl.when(pl.program_id(0) == 0)
        def _():
            dg_ref[...] = jnp.zeros_like(dg_ref)

        dh = jnp.zeros((tm, D_MODEL), F32)
        for p_ref, (c0, n) in zip(piece_refs, cols):
            for c in range(0, n, 512):
                dh = dh + _nt(p_ref[:, c:c + 512].astype(BF16), w_ref[:, c0 + c:c0 + c + 512])
        xv = x_ref[...]
        r = lax.rsqrt(jnp.mean(xv * xv, axis=-1, keepdims=True) + RMS_EPS)
        dx, dg_rows = _rms_backward(dh, xv, r, g_ref[...])
        dx_ref[...] = res_ref[...] + dx
        dg_ref[...] += jnp.sum(dg_rows, axis=0, keepdims=True)

    row = lambda n: pl.BlockSpec((tm, n), lambda i: (i, 0))
    vec = pl.BlockSpec((1, D_MODEL), lambda i: (0, 0))
    any_space = pl.BlockSpec(memory_space=pl.ANY)
    n_x = len(exchange)
    grid = (S // tm,)
    res = pl.pallas_call(
        _with_swap(body, grid, len(cols) + 4, 2, n_x, rider=_ChipExchange), name=f"inproj_bwd_l{layer}", grid=grid,
        in_specs=[row(n) for _, n in cols] + [_layer_weight_spec(D_MODEL, IN_WIDTH, layer), row(D_MODEL), vec,
                                              row(D_MODEL)] + [any_space] * n_x,
        out_specs=[row(D_MODEL), vec] + [any_space] * n_x,
        out_shape=[jax.ShapeDtypeStruct((S, D_MODEL), F32), jax.ShapeDtypeStruct((1, D_MODEL), F32)]
        + _ChipExchange.landing(exchange),
        scratch_shapes=_ChipExchange.semaphores(n_x) if n_x else [],
        compiler_params=_params(),
    )(*[p for p, _ in pieces], w_in, x, g, dx_res, *exchange)
    return res[0], res[1], res[2:]


class _SiblingSwap:
    def __init__(self, mine, theirs, send, recv):
        x, y, c, _ = _place()
        self.copies = [_remote(m.at[1 - c], t, send, recv, a, (x, y, 1 - c))
                       for a, (m, t) in enumerate(zip(mine, theirs))]

    def start(self):
        for cp in self.copies:
            cp.start()

    def wait(self):
        for cp in self.copies:
            cp.wait()


def _with_swap(body, grid, n_in, n_out, n_swap, rider=_SiblingSwap):
    if not n_swap:
        return lambda *refs: body(refs[:n_in], refs[n_in:])

    def riding(*refs):
        ins, mine = refs[:n_in], refs[n_in:n_in + n_swap]
        outs, theirs = refs[n_in + n_swap:n_in + n_swap + n_out], refs[n_in + n_swap + n_out:n_in + 2 * n_swap + n_out]
        swap = rider(mine, theirs, *refs[n_in + 2 * n_swap + n_out:])
        step = [pl.program_id(d) for d in range(len(grid))]
        first, last = step[0] == 0, step[0] == grid[0] - 1
        for d in range(1, len(grid)):
            first, last = jnp.logical_and(first, step[d] == 0), jnp.logical_and(last, step[d] == grid[d] - 1)
        pl.when(first)(swap.start)
        body(ins, outs)
        pl.when(last)(swap.wait)

    return riding


def _swap_specs(swap):
    any_space = pl.BlockSpec(memory_space=pl.ANY)
    shapes = [jax.ShapeDtypeStruct(d.shape[1:], d.dtype) for d in swap]
    sems = [pltpu.SemaphoreType.DMA((len(swap),)), pltpu.SemaphoreType.DMA((len(swap),))] if swap else []
    return [any_space] * len(swap), shapes, sems


def _wgrad(a, b, name, layer, into=None, col0=0, n_total=None, swap=()):
    S, M = a.shape
    N = b.shape[1]
    n_total = N if n_total is None else n_total
    tk = min(2048, S)
    tn = max(t for t in range(LANES, min(N, 1280) + 1, LANES) if N % t == 0 and col0 % t == 0)
    grid = (N // tn, S // tk)

    def body(ins, outs):
        prod = _tn(ins[0][...].astype(BF16), ins[1][...].astype(BF16))

        @pl.when(pl.program_id(1) == 0)
        def _():
            outs[0][...] = prod

        @pl.when(pl.program_id(1) > 0)
        def _():
            outs[0][...] += prod

    in_specs = [pl.BlockSpec((tk, M), lambda j, k: (k, 0)), pl.BlockSpec((tk, tn), lambda j, k: (k, j))]
    args = [a, b]
    aliases = {}
    if into is not None:
        in_specs.append(pl.BlockSpec(memory_space=pl.ANY))
        args.append(into)
        aliases = {2: 0}
    swap_specs, swap_shapes, swap_sems = _swap_specs(swap)
    res = pl.pallas_call(
        _with_swap(body, grid, len(args), 1, len(swap)), name=name, grid=grid,
        in_specs=in_specs + swap_specs,
        out_specs=[pl.BlockSpec((None, M, tn), lambda j, k: (layer, 0, col0 // tn + j))] + swap_specs,
        out_shape=[jax.ShapeDtypeStruct((2, M, n_total), F32)] + swap_shapes,
        input_output_aliases=aliases,
        scratch_shapes=swap_sems,
        compiler_params=_params(),
    )(*args, *swap)
    return (res[0], res[1:]) if swap else res[0]


def _pool_wgrad(pooled, dmixed, layer, swap=()):
    S = pooled.shape[0]
    tk = min(8192, S)
    grid = (4, S // tk)

    def body(ins, outs):
        prod = _tn(ins[0][...], ins[1][...])

        @pl.when(pl.program_id(1) == 0)
        def _():
            outs[0][...] = prod

        @pl.when(pl.program_id(1) > 0)
        def _():
            outs[0][...] += prod

    blk = pl.BlockSpec((tk, POOL_GROUP), lambda g, k: (k, g))
    swap_specs, swap_shapes, swap_sems = _swap_specs(swap)
    res = pl.pallas_call(
        _with_swap(body, grid, 2, 1, len(swap)), name=f"pool_wgrad_l{layer}", grid=grid,
        in_specs=[blk, blk] + swap_specs,
        out_specs=[pl.BlockSpec((None, POOL_GROUP, POOL_GROUP), lambda g, k: (g, 0, 0))] + swap_specs,
        out_shape=[jax.ShapeDtypeStruct((4, POOL_GROUP, POOL_GROUP), F32)] + swap_shapes,
        scratch_shapes=swap_sems,
        compiler_params=_params(),
    )(pooled, dmixed, *swap)
    return (res[0], res[1:]) if swap else res[0]


def _local_step(x, target, norm_g, b_gate, pool_w, pool_scale, final_g, weights, pos):
    n_layers = norm_g.shape[0]
    saved = []
    for l in range(n_layers):
        g = norm_g[l][None]
        bg = b_gate[l][None]
        sc = pool_scale[l][None]
        if l == 0:
            (u, zp, q, k, v, za, gl, h), (w_in, w_pu, w_au, w_out) = _rms_inproj(x, g, None, l, gather=weights)
        else:
            (u, zp, q, k, v, za, gl, h), _ = _rms_inproj(x, g, w_in, l)
        o, carry = _attn_fwd(q, k, v, l)
        saved.append((x, g, bg, sc, u, zp, q, k, v, za, gl, h, o, carry))
        if l < n_layers - 1:
            x = _post_fwd(x, u, zp, o, za, gl, bg, pool_w[l], sc, w_pu, w_au, w_out, l)
        else:
            loss, dx, d_final_g = _post_fwd(x, u, zp, o, za, gl, bg, pool_w[l], sc, w_pu, w_au, w_out, l,
                                            head=(final_g[None], target))

    small = [None] * n_layers
    dw_in = dw_out = dw_pu = dw_au = None
    for l in reversed(range(n_layers)):
        x_in, g, bg, sc, u, zp, q, k, v, za, gl, h, o, carry = saved[l]
        (duz, do, dzg, dsc, dbg, merged, dup, dua, y_pool, y_attn, pooled, dmixed) = _post_bwd(
            dx, u, zp, o, za, gl, bg, pool_w[l], sc, w_pu, w_au, w_out, l)
        dq, dk, dv = _attn_bwd(q, k, v, carry, do, l)
        pieces = [(duz, C_U), (dq, C_Q), (dk, C_K), (dv, C_V), (dzg, C_ZA)]
        for p, c0 in pieces:
            dw_in = _wgrad(h, p, f"wgrad_in_l{l}_c{c0}", l, into=dw_in, col0=c0, n_total=IN_WIDTH)
        if l > 0:
            dw_out = _wgrad(merged, dx, f"wgrad_out_l{l}", l, into=dw_out)
        else:
            dw_out, (other_in,) = _wgrad(merged, dx, f"wgrad_out_l{l}", l, into=dw_out, swap=(dw_in,))
        dw_pu = _wgrad(y_pool, dup, f"wgrad_pu_l{l}", l, into=dw_pu)
        dw_au = _wgrad(y_attn, dua, f"wgrad_au_l{l}", l, into=dw_au)
        if l > 0:
            dpw = _pool_wgrad(pooled, dmixed, l)
        else:
            dpw, (other_pu, other_au, other_out) = _pool_wgrad(pooled, dmixed, l, swap=(dw_pu, dw_au, dw_out))
        if l > 0:
            dx, dg, _ = _inproj_bwd(pieces, w_in, x_in, g, dx, l)
        else:
            pair = [_pair_sum(d, t, pos, f"grad_pair_sum_{n}") for d, t, n in
                    zip((dw_in, dw_pu, dw_au, dw_out), (other_in, other_pu, other_au, other_out), SHARDED_NAMES)]
            dx, dg, landed = _inproj_bwd(pieces, w_in, x_in, g, dx, l, exchange=[pb for _, pb in pair])
        small[l] = (dg[0], dbg[0], dpw, dsc[0])
    small = [jnp.stack([small[l][i] for l in range(n_layers)]) for i in range(4)]
    return loss[0, 0], dx, d_final_g[0], small, [p for p, _ in pair], landed


SHARDED = ((2, 1280), (2, 256), (2, 256), (1, 256))
SHARDED_NAMES = ("w_in", "w_pool_up", "w_attn_up", "w_out")
ANY = pl.BlockSpec(memory_space=pl.ANY)


def _part(ref, s, axis, width):
    sl = pl.ds(pl.multiple_of(s * width, width), width)
    return ref.at[:, sl] if axis == 2 else ref.at[sl, :]


def _place():
    x, y, c = lax.axis_index("x"), lax.axis_index("y"), lax.axis_index("c")
    return x, y, c, 2 * x + y


def _other_chip(x, y, m):
    px = 1 - x if m & 2 else x
    py = 1 - y if m & 1 else y
    return px, py, 2 * px + py


def _remote(src, dst, send, recv, k, to):
    return pltpu.make_async_remote_copy(src_ref=src, dst_ref=dst, send_sem=send.at[k], recv_sem=recv.at[k],
                                        device_id=to, device_id_type=MESH)


def _part_spec(tr, rows_s, cols_s, axis, width, lead):
    if axis == 2:
        return pl.BlockSpec((None, tr, width), lambda *a: (lead(a), a[-2], a[-1][1]))
    return pl.BlockSpec((None, tr, cols_s), lambda *a: (lead(a), a[-1][1] * (rows_s // tr) + a[-2], 0))


def _cast_into_place(w, pos, axis, width, name):
    L, Rs, Cs = w.shape
    tr = min(256, Rs)
    shape = [L, Rs, Cs]
    shape[axis] *= N_CHIPS

    def body(pos_ref, w_ref, o_ref):
        o_ref[...] = w_ref[...].astype(BF16)

    return pl.pallas_call(
        body, name=name,
        grid_spec=pltpu.PrefetchScalarGridSpec(
            num_scalar_prefetch=1, grid=(L, Rs // tr),
            in_specs=[pl.BlockSpec((None, tr, Cs), lambda l, i, pos: (l, i, 0))],
            out_specs=_part_spec(tr, Rs, Cs, axis, width, lambda a: a[0])),
        out_shape=jax.ShapeDtypeStruct(tuple(shape), BF16),
        compiler_params=_params(),
    )(pos, w)


def _w_in_half(layer):
    def piece(refs, who, shard):
        rows = pl.ds(pl.multiple_of(who * (D_MODEL // 2), D_MODEL // 2), D_MODEL // 2)
        return refs[0].at[layer, rows, pl.ds(pl.multiple_of(shard * SHARDED[0][1], SHARDED[0][1]), SHARDED[0][1])]
    return piece


def _whole_layer(a):
    def piece(refs, who, shard):
        return _part(refs[a].at[who], shard, *SHARDED[a])
    return piece


FIRST_PIECES = (_w_in_half(0),)
LATER_PIECES = (_w_in_half(1), _whole_layer(1), _whole_layer(2), _whole_layer(3))


class _Gather:
    def __init__(self, pieces, refs, send, recv):
        self.pieces, self.refs, self.send, self.recv = pieces, refs, send, recv
        self.x, self.y, self.c, s = _place()
        self.first = []
        for u, piece in enumerate(pieces):
            for m in (1, 2, 3):
                px, py, _ = _other_chip(self.x, self.y, m)
                own = piece(refs, self.c, s)
                self.first.append(_remote(own, own, send, recv, 3 * u + m - 1, (px, py, self.c)))

    def start(self):
        for cp in self.first:
            cp.start()

    def finish(self):
        x, y, c, n = self.x, self.y, self.c, len(self.pieces)

        def landed(u, m, who):
            _, _, sp = _other_chip(x, y, m)
            return self.pieces[u](self.refs, who, sp)

        passed = []
        for m in (1, 2, 3):
            for u in range(n):
                got = landed(u, m, c)
                _remote(got, got, self.send, self.recv, 3 * u + m - 1, (x, y, c)).wait_recv()
                passed.append(_remote(got, got, self.send, self.recv, 3 * n + 3 * u + m - 1, (x, y, 1 - c)))
                passed[-1].start()
        for m in (1, 2, 3):
            for u in range(n):
                got = landed(u, m, 1 - c)
                _remote(got, got, self.send, self.recv, 3 * n + 3 * u + m - 1, (x, y, c)).wait_recv()
        for cp in self.first + passed:
            cp.wait_send()

    @staticmethod
    def semaphores(pieces):
        return [pltpu.SemaphoreType.DMA((6 * len(pieces),)), pltpu.SemaphoreType.DMA((6 * len(pieces),))]


def _gather_first(fulls):
    n = len(fulls)

    def body(*refs):
        gather = _Gather(FIRST_PIECES, refs[n:2 * n], *refs[2 * n:])
        gather.start()
        gather.finish()

    return pl.pallas_call(
        body, name="gather_first",
        in_specs=[ANY] * n, out_specs=[ANY] * n,
        out_shape=[jax.ShapeDtypeStruct(f.shape, f.dtype) for f in fulls],
        input_output_aliases={a: a for a in range(n)},
        scratch_shapes=_Gather.semaphores(FIRST_PIECES),
    )(*fulls)


def _pair_sum(dw, other, pos, name):
    _, R, C = dw.shape
    tr = 128 if C > 1024 else 256

    def body(pos_ref, a_ref, b_ref, o_ref, ob_ref):
        tot = a_ref[...] + b_ref[...]
        o_ref[...] = tot
        ob_ref[...] = tot.astype(BF16)

    blk = pl.BlockSpec((tr, C), lambda i, pos: (i, 0))
    return pl.pallas_call(
        body, name=name,
        grid_spec=pltpu.PrefetchScalarGridSpec(
            num_scalar_prefetch=1, grid=(R // tr,),
            in_specs=[pl.BlockSpec((None, tr, C), lambda i, pos: (pos[0], i, 0)), blk],
            out_specs=[blk, blk]),
        out_shape=[jax.ShapeDtypeStruct((R, C), F32), jax.ShapeDtypeStruct((R, C), BF16)],
        compiler_params=_params(),
    )(pos, dw, other)


class _ChipExchange:
    def __init__(self, mine, theirs, send, recv):
        x, y, c, _ = _place()
        self.copies = []
        for a, (axis, width) in enumerate(SHARDED):
            for m in (1, 2, 3):
                px, py, sp = _other_chip(x, y, m)
                self.copies.append(_remote(_part(mine[a], sp, axis, width), theirs[a].at[m - 1], send, recv,
                                           3 * a + m - 1, (px, py, c)))

    def start(self):
        for cp in self.copies:
            cp.start()

    def wait(self):
        for cp in self.copies:
            cp.wait()

    @staticmethod
    def landing(ps):
        shapes = []
        for p, (axis, width) in zip(ps, SHARDED):
            shape = [3] + list(p.shape)
            shape[axis] = width
            shapes.append(jax.ShapeDtypeStruct(tuple(shape), p.dtype))
        return shapes

    @staticmethod
    def semaphores(n):
        return [pltpu.SemaphoreType.DMA((3 * n,)), pltpu.SemaphoreType.DMA((3 * n,))]


def _shard_sum(p, landed, pos, axis, width, name):
    _, Rs, Cs = landed.shape
    tr = min(256, Rs)
    p_spec = _part_spec(tr, Rs, Cs, axis, width, lambda a: 0)

    def body(pos_ref, p_ref, l_ref, o_ref):
        o_ref[...] = ((p_ref[...] + l_ref[0].astype(F32)) + l_ref[1].astype(F32)) + l_ref[2].astype(F32)

    return pl.pallas_call(
        body, name=name,
        grid_spec=pltpu.PrefetchScalarGridSpec(
            num_scalar_prefetch=1, grid=(Rs // tr,),
            in_specs=[p_spec, pl.BlockSpec((3, tr, Cs), lambda i, pos: (0, i, 0))],
            out_specs=pl.BlockSpec((None, tr, Cs), lambda i, pos: (pos[0], i, 0))),
        out_shape=jax.ShapeDtypeStruct((2, Rs, Cs), F32),
        compiler_params=_params(),
    )(pos, p[None], landed)


def _final_exchange(gs, packed):
    n = len(gs)

    def body(*refs):
        small_ref = refs[n]
        outs, total_ref = refs[n + 1:2 * n + 1], refs[2 * n + 1]
        all_ref, send, recv, small_send, small_recv = refs[2 * n + 2:]
        x, y, c, _ = _place()
        my_id = 4 * x + 2 * y + c
        all_ref[my_id] = small_ref[...]
        small = []
        for m in range(1, N_DEVICES):
            px = 1 - x if m & 4 else x
            py = 1 - y if m & 2 else y
            pc = 1 - c if m & 1 else c
            cp = _remote(small_ref, all_ref.at[my_id], small_send, small_recv, m - 1, (px, py, pc))
            cp.start()
            small.append((cp, 4 * px + 2 * py + pc))
        copies = [_remote(outs[a].at[c], outs[a].at[c], send, recv, a, (x, y, 1 - c)) for a in range(n)]
        for cp in copies:
            cp.start()
        for m, (cp, peer_id) in enumerate(small):
            _remote(small_ref, all_ref.at[peer_id], small_send, small_recv, m, (x, y, c)).wait_recv()
        total = all_ref[0]
        for d in range(1, N_DEVICES):
            total = total + all_ref[d]
        total_ref[...] = total
        for cp, _ in small:
            cp.wait_send()
        for a, cp in enumerate(copies):
            cp.wait_send()
            _remote(outs[a].at[1 - c], outs[a].at[1 - c], send, recv, a, (x, y, c)).wait_recv()

    vmem = pl.BlockSpec(memory_space=pltpu.VMEM)
    res = pl.pallas_call(
        body, name="final_exchange",
        in_specs=[ANY] * n + [vmem], out_specs=[ANY] * n + [vmem],
        out_shape=[jax.ShapeDtypeStruct(g.shape, g.dtype) for g in gs]
        + [jax.ShapeDtypeStruct(packed.shape, packed.dtype)],
        input_output_aliases={a: a for a in range(n)},
        scratch_shapes=[pltpu.VMEM((N_DEVICES,) + packed.shape, F32),
                        pltpu.SemaphoreType.DMA((n,)), pltpu.SemaphoreType.DMA((n,)),
                        pltpu.SemaphoreType.DMA((N_DEVICES - 1,)), pltpu.SemaphoreType.DMA((N_DEVICES - 1,))],
        compiler_params=_params(),
    )(*gs, packed)
    return res[:n], res[n]


def _adamw(w, g, m, v, name):
    shape = w.shape
    C = shape[-1]
    flat = [t.reshape(-1, C) for t in (w, g, m, v)]
    R = flat[0].shape[0]
    tr = max(t for t in range(8, R + 1, 8) if R % t == 0 and t * C <= 384 * 1024)

    def body(w_ref, g_ref, m_ref, v_ref, d_ref, nm_ref, nv_ref):
        gv = g_ref[...]
        nm = ADAM_B1 * m_ref[...] + (1.0 - ADAM_B1) * gv
        nv = ADAM_B2 * v_ref[...] + (1.0 - ADAM_B2) * (gv * gv)
        m_hat = nm / (1.0 - ADAM_B1 ** ADAM_STEP)
        v_hat = nv / (1.0 - ADAM_B2 ** ADAM_STEP)
        d_ref[...] = -ADAM_LR * (m_hat / (jnp.sqrt(v_hat) + ADAM_EPS) + ADAM_WD * w_ref[...])
        nm_ref[...] = nm
        nv_ref[...] = nv

    blk = pl.BlockSpec((tr, C), lambda i: (i, 0))
    out = jax.ShapeDtypeStruct((R, C), F32)
    res = pl.pallas_call(
        body, name=name, grid=(R // tr,),
        in_specs=[blk] * 4, out_specs=[blk] * 3, out_shape=[out] * 3,
        compiler_params=_params(),
    )(*flat)
    return [t.reshape(shape) for t in res]


SMALL_SHAPES = ((2, 1024), (2, 2048), (2, 4, 128, 128), (2, 512), (1024,))


def _pack_small(parts):
    return jnp.concatenate([p.reshape(-1, LANES) for p in parts], axis=0)


def _unpack_small(packed):
    out, row = [], 0
    for shape in SMALL_SHAPES:
        n = 1
        for d in shape:
            n *= d
        out.append(packed[row:row + n // LANES].reshape(shape))
        row += n // LANES
    return out


def kernel(x, norm_g, w_in, b_gate, pool_w, pool_scale, w_pool_up, w_attn_up, w_out, final_g, loss_target, m_norm_g, m_w_in, m_b_gate, m_pool_w, m_pool_scale, m_w_pool_up, m_w_attn_up, m_w_out, m_final_g, v_norm_g, v_w_in, v_b_gate, v_pool_w, v_pool_scale, v_w_pool_up, v_w_attn_up, v_w_out, v_final_g):
    _, _, c, s = _place()
    pos = jnp.stack([c, s]).astype(jnp.int32)
    names = SHARDED_NAMES

    weights = _gather_first([_cast_into_place(w, pos, axis, width, f"cast_{n}")
                             for w, (axis, width), n in zip((w_in, w_pool_up, w_attn_up, w_out), SHARDED, names)])
    loss_part, dx, d_final_g, small, pair, landed = _local_step(x[0], loss_target[0], norm_g, b_gate, pool_w,
                                                                pool_scale, final_g, weights, pos)
    mine = [_shard_sum(p, l, pos, axis, width, f"grad_shard_sum_{n}")
            for p, l, (axis, width), n in zip(pair, landed, SHARDED, names)]
    (g_in, g_pu, g_au, g_out), summed = _final_exchange(
        mine, _pack_small(small + [d_final_g, jnp.broadcast_to(loss_part, (8, LANES))]))
    g_small = _unpack_small(summed)
    loss = summed[-8, 0]
    upd_small = _adamw(_pack_small([norm_g, b_gate, pool_w, pool_scale, final_g]), _pack_small(g_small),
                       _pack_small([m_norm_g, m_b_gate, m_pool_w, m_pool_scale, m_final_g]),
                       _pack_small([v_norm_g, v_b_gate, v_pool_w, v_pool_scale, v_final_g]), "adamw_small")
    d_small, nm_small, nv_small = [_unpack_small(t) for t in upd_small]
    upd_in = _adamw(w_in, g_in, m_w_in, v_w_in, "adamw_w_in")
    upd_pu = _adamw(w_pool_up, g_pu, m_w_pool_up, v_w_pool_up, "adamw_w_pool_up")
    upd_au = _adamw(w_attn_up, g_au, m_w_attn_up, v_w_attn_up, "adamw_w_attn_up")
    upd_out = _adamw(w_out, g_out, m_w_out, v_w_out, "adamw_w_out")

    def ordered(sm, k):
        big = (upd_in[k], upd_pu[k], upd_au[k], upd_out[k]) if k is not None else (g_in, g_pu, g_au, g_out)
        return [sm[0], big[0], sm[1], sm[2], sm[3], big[1], big[2], big[3], sm[4]]

    return (loss, dx[None], *ordered(g_small, None), *ordered(d_small, 0), *ordered(nm_small, 1),
            *ordered(nv_small, 2))
```

```python
import jax
import jax.numpy as jnp
import numpy as np
from jax import lax
from jax.experimental import pallas as pl
from jax.experimental.pallas import tpu as pltpu

F32 = jnp.float32
BF16 = jnp.bfloat16
MESH = pl.DeviceIdType.MESH

D_MODEL = 1024
POOL_WIDTH = 512
POOL_WINDOWS = (2, 4, 8, 16)
POOL_GROUP = 128
POOL_HALO = 16
ATTN_WIDTH = 512
HEAD_DIM = 64
HEAD_PAIRS = 4
IN_WIDTH = 5120
N_CHIPS = 4
N_DEVICES = 8
RMS_EPS = 1e-6
C_U, C_ZP, C_Q, C_K, C_V, C_ZA, C_GL = 0, 512, 1024, 1536, 2048, 2560, 3072

ADAM_LR, ADAM_B1, ADAM_B2, ADAM_EPS, ADAM_WD, ADAM_STEP = 0.001, 0.9, 0.999, 1e-08, 0.01, 10

LANES = 128
ATTN_BLOCK = 256
QUERY_BLOCKS = 4
ROW_TILE = 256
PROJ_ROW_TILE = 512
VMEM_LIMIT = 56 * 1024 * 1024


def _params(**kw):
    return pltpu.CompilerParams(vmem_limit_bytes=VMEM_LIMIT, **kw)


def _nt(a, b):
    return lax.dot_general(a, b, (((1,), (1,)), ((), ())), preferred_element_type=F32)


def _tn(a, b):
    return lax.dot_general(a, b, (((0,), (0,)), ((), ())), preferred_element_type=F32)


def _nn(a, b):
    return jnp.dot(a, b, preferred_element_type=F32)


def _sigmoid(z):
    return 1.0 / (1.0 + jnp.exp(-z))


def _rms_inproj(x, g, w_in, layer, gather=()):
    S = x.shape[0]
    tm = min(PROJ_ROW_TILE, S)
    n_tiles = S // tm
    n_g = len(gather)

    def body(*refs):
        x_ref, g_ref = refs[:2]
        if n_g:
            (u_ref, zp_ref, q_ref, k_ref, v_ref, za_ref, gl_ref, h_ref) = refs[2 + n_g:10 + n_g]
            fulls = refs[10 + n_g:10 + 2 * n_g]
            w_ref, load_sem, send, recv = refs[10 + 2 * n_g:]
            later = _Gather(LATER_PIECES, fulls, send, recv)

            @pl.when(pl.program_id(0) == 0)
            def _():
                load = pltpu.make_async_copy(fulls[0].at[layer], w_ref, load_sem)
                load.start()
                later.start()
                load.wait()

            pl.when(pl.program_id(0) == (3 * n_tiles) // 4)(later.pass_on)
        else:
            w_ref, u_ref, zp_ref, q_ref, k_ref, v_ref, za_ref, gl_ref, h_ref = refs[2:]
        xv = x_ref[...]
        r = lax.rsqrt(jnp.mean(xv * xv, axis=-1, keepdims=True) + RMS_EPS)
        h = ((xv * r) * g_ref[...]).astype(BF16)
        h_ref[...] = h

        def mm(c0, n):
            return _nn(h, w_ref[:, c0:c0 + n])

        u_ref[...] = mm(C_U, 512)
        zp_ref[...] = mm(C_ZP, 512).astype(BF16)
        q_ref[...] = (mm(C_Q, 512) * 0.125).astype(BF16)
        k_ref[...] = mm(C_K, 512).astype(BF16)
        v_ref[...] = mm(C_V, 512).astype(BF16)
        za_ref[...] = mm(C_ZA, 512).astype(BF16)
        for c in range(4):
            gl_ref[:, c * 512:(c + 1) * 512] = mm(C_GL + c * 512, 512).astype(BF16)
        if n_g:
            pl.when(pl.program_id(0) == n_tiles - 1)(later.finish)

    row = lambda n: pl.BlockSpec((tm, n), lambda i: (i, 0))
    sd = lambda n, dt: jax.ShapeDtypeStruct((S, n), dt)
    any_space = pl.BlockSpec(memory_space=pl.ANY)
    weights = [any_space] * n_g if n_g else [_layer_weight_spec(D_MODEL, IN_WIDTH, layer)]
    res = pl.pallas_call(
        body, name=f"rms_inproj_l{layer}", grid=(n_tiles,),
        in_specs=[row(D_MODEL), pl.BlockSpec((1, D_MODEL), lambda i: (0, 0))] + weights,
        out_specs=[row(512), row(512), row(512), row(512), row(512), row(512), row(2048), row(D_MODEL)]
        + [any_space] * n_g,
        out_shape=[sd(512, F32), sd(512, BF16), sd(512, BF16), sd(512, BF16), sd(512, BF16), sd(512, BF16),
                   sd(2048, BF16), sd(D_MODEL, BF16)] + [jax.ShapeDtypeStruct(f.shape, f.dtype) for f in gather],
        input_output_aliases={2 + a: 8 + a for a in range(n_g)},
        scratch_shapes=([pltpu.VMEM((D_MODEL, IN_WIDTH), BF16), pltpu.SemaphoreType.DMA(())]
                        + _Gather.semaphores(LATER_PIECES)) if n_g else [],
        compiler_params=_params(),
    )(x, g, *(gather if n_g else (w_in,)))
    return res[:8], res[8:]


def _tri(n, strict_lower):
    r = lax.broadcasted_iota(jnp.int32, (n, n), 0)
    c = lax.broadcasted_iota(jnp.int32, (n, n), 1)
    return jnp.where(r > c if strict_lower else r < c, 1.0, 0.0).astype(BF16)


def _split_dot(x, m):
    hi = x.astype(BF16)
    lo = (x - hi.astype(F32)).astype(BF16)
    return _nn(hi, m) + _nn(lo, m)


def _log_terms(z):
    lg = jnp.log(1.0 + jnp.exp(-jnp.abs(z)))
    a = jnp.minimum(z, 0.0) - lg
    return a, a - z


EXHAUSTED = -104.0
UNREACHED = -1e30


class _HeadPair:
    def __init__(self, T):
        self.T = T
        self.first = lax.broadcasted_iota(jnp.int32, (T, LANES), 1) < HEAD_DIM
        self.lane = lax.broadcasted_iota(jnp.int32, (2 * T, LANES), 1)
        row = lax.broadcasted_iota(jnp.int32, (2 * T, T), 0)
        row = jnp.where(row >= T, row - T, row)
        self.causal = row > lax.broadcasted_iota(jnp.int32, (2 * T, T), 1)
        self.below = _tri(T, True)

    def stack(self, x2):
        return jnp.concatenate([jnp.where(self.first, x2, 0), jnp.where(self.first, 0, x2)], axis=0).astype(BF16)

    def unstack(self, x):
        return jnp.where(self.first, x[:self.T], x[self.T:])

    def keys(self, ref, blocks):
        T = self.T
        return jnp.concatenate([ref[pl.ds(pl.multiple_of(j * T, T), T), :] for j, _ in blocks], axis=0)

    def log_terms(self, z, blocks):
        T = self.T
        a_all, l_all = _log_terms(z)
        a = [a_all[:, b * T:(b + 1) * T] for b in range(len(blocks))]
        l1m = [l_all[:, b * T:(b + 1) * T] for b in range(len(blocks))]
        return a, [jnp.where(self.causal, l, 0.0) if diagonal else l for l, (_, diagonal) in zip(l1m, blocks)]

    def later_sums(self, l1m):
        later = _split_dot(jnp.concatenate(l1m, axis=0), self.below)
        return [later[2 * self.T * b:2 * self.T * (b + 1)] for b in range(len(l1m))]


def _sections(x, n):
    return x.reshape(n, x.shape[0] // n, x.shape[1])


def _attn_fwd(q, k, v, layer):
    S = q.shape[0]
    T = min(ATTN_BLOCK, S)
    nq = S // T
    jobs = min(QUERY_BLOCKS, nq)
    assert nq <= LANES and nq % jobs == 0
    per_job = nq // jobs

    def body(q_ref, k_ref, v_ref, o_ref, c_ref):
        i = pl.program_id(1)
        pair = _HeadPair(T)
        qs = [pair.stack(q_ref[n]) for n in range(jobs)]
        diag = [i + n * per_job for n in range(jobs)]

        def sweep(jobs):
            kv = [(pair.keys(k_ref, bl), pair.keys(v_ref, bl)) for _, bl, _ in jobs]
            zs = [_nt(qs[n], kcat) for (n, _, _), (kcat, _) in zip(jobs, kv)]
            terms = [pair.log_terms(z, bl) for (_, bl, _), z in zip(jobs, zs)]
            laters = [pair.later_sums(l1m) for _, l1m in terms]
            weights = []
            for (_, bl, (acc, run, saved)), (a, l1m), later in zip(jobs, terms, laters):
                ws = []
                for b, (j, diagonal) in enumerate(bl):
                    saved = jnp.where(pair.lane == j, run, saved)
                    w = jnp.exp(a[b] + later[b] + run)
                    ws.append(jnp.where(pair.causal, w, 0.0) if diagonal else w)
                    run = run + jnp.sum(l1m[b], axis=1, keepdims=True)
                weights.append((jnp.concatenate(ws, axis=1).astype(BF16), acc, run, saved))
            return [(acc + _nn(w, vcat), run, saved) for (w, acc, run, saved), (_, vcat) in zip(weights, kv)]

        def alive(carry):
            return (jnp.max(carry[1]) > EXHAUSTED).astype(jnp.int32)

        def older_blocks(n, carry):
            def older_block(state):
                j, _, c = state
                c = sweep([(n, [(j, False)], c)])[0]
                return j - 1, alive(c), c

            return lax.while_loop(lambda st: jnp.logical_and(st[0] >= 0, st[1] > 0), older_block,
                                  (diag[n] - 2, alive(carry), carry))[2]

        def run(first_blocks):
            init = (jnp.zeros((2 * T, LANES), F32), jnp.zeros((2 * T, 1), F32),
                    jnp.full((2 * T, LANES), UNREACHED, F32))
            carries = sweep([(n, first_blocks[n], init) for n in range(jobs)])
            for n in range(jobs):
                acc, _, saved = older_blocks(n, carries[n])
                o_ref[n] = pair.unstack(acc).astype(BF16)
                c_ref[n, :, :LANES] = saved[:T]
                c_ref[n, :, LANES:] = saved[T:]

        with_previous = lambda d: [(d, True), (d - 1, False)]

        @pl.when(i == 0)
        def _():
            run([[(diag[0], True)]] + [with_previous(d) for d in diag[1:]])

        @pl.when(i > 0)
        def _():
            run([with_previous(d) for d in diag])

    blk = lambda n: pl.BlockSpec((jobs, T, n), lambda p, i: (0, i, p))
    full = pl.BlockSpec((S, LANES), lambda p, i: (0, p))
    o, carry = pl.pallas_call(
        body, name=f"attn_fwd_l{layer}", grid=(HEAD_PAIRS, per_job),
        in_specs=[blk(LANES), full, full],
        out_specs=[blk(LANES), blk(2 * LANES)],
        out_shape=[jax.ShapeDtypeStruct((jobs, S // jobs, ATTN_WIDTH), BF16),
                   jax.ShapeDtypeStruct((jobs, S // jobs, 8 * LANES), F32)],
        compiler_params=_params(),
    )(_sections(q, jobs), k, v)
    return o.reshape(S, ATTN_WIDTH), carry.reshape(S, 8 * LANES)


def _attn_bwd(q, k, v, saved, do, layer):
    S = q.shape[0]
    T = min(ATTN_BLOCK, S)
    nq = S // T
    jobs = min(QUERY_BLOCKS, nq)
    per_job = nq // jobs

    def body(q_ref, k_ref, v_ref, c_ref, do_ref, dq_ref, dk_ref, dv_ref):
        i = pl.program_id(1)

        @pl.when(i == 0)
        def _():
            dk_ref[...] = jnp.zeros_like(dk_ref)
            dv_ref[...] = jnp.zeros_like(dv_ref)

        pair = _HeadPair(T)
        diag = [i + n * per_job for n in range(jobs)]
        qs = [pair.stack(q_ref[n]) for n in range(jobs)]
        dos = [pair.stack(do_ref[n].astype(BF16)) for n in range(jobs)]
        saved = [jnp.concatenate([c_ref[n, :, :LANES], c_ref[n, :, LANES:]], axis=0) for n in range(jobs)]
        before = _tri(T, False)

        def sweep(jobs):
            kv = [(pair.keys(k_ref, bl), pair.keys(v_ref, bl)) for _, bl, _ in jobs]
            zs = [_nt(qs[n], kcat) for (n, _, _), (kcat, _) in zip(jobs, kv)]
            gs = [_nt(dos[n], vcat) for (n, _, _), (_, vcat) in zip(jobs, kv)]
            terms = [pair.log_terms(z, bl) for (_, bl, _), z in zip(jobs, zs)]
            laters = [pair.later_sums(l1m) for _, l1m in terms]
            ws, es = [], []
            for (n, bl, _), (a, _), later, g in zip(jobs, terms, laters, gs):
                w_job, e_job = [], []
                for b, (j, diagonal) in enumerate(bl):
                    run = jnp.sum(jnp.where(pair.lane == j, saved[n], 0.0), axis=1, keepdims=True)
                    w = jnp.exp(a[b] + later[b] + run)
                    w_job.append(jnp.where(pair.causal, w, 0.0) if diagonal else w)
                    e_job.append(w_job[b] * g[:, b * T:(b + 1) * T])
                ws.append(w_job)
                es.append(e_job)
            prefixes = [_nn(jnp.concatenate(e_job, axis=0).astype(BF16), before) for e_job in es]
            dzs, olders = [], []
            for (_, bl, (_, older)), (a, _), e_job, prefix in zip(jobs, terms, es, prefixes):
                dz_job = []
                for b, (j, diagonal) in enumerate(bl):
                    dz = e_job[b] - jnp.exp(a[b]) * (e_job[b] + (prefix[2 * T * b:2 * T * (b + 1)] + older))
                    dz_job.append(jnp.where(pair.causal, dz, 0.0) if diagonal else dz)
                    older = older + jnp.sum(e_job[b], axis=1, keepdims=True)
                dzs.append(jnp.concatenate(dz_job, axis=1).astype(BF16))
                olders.append(older)
            out = []
            for (n, bl, (dq, _)), dz, w_job, older, (kcat, _) in zip(jobs, dzs, ws, olders, kv):
                dk = _tn(dz, qs[n])
                dv = _tn(jnp.concatenate(w_job, axis=1).astype(BF16), dos[n])
                for b, (j, _) in enumerate(bl):
                    rows = pl.ds(pl.multiple_of(j * T, T), T)
                    dk_ref[rows, :] += dk[b * T:(b + 1) * T]
                    dv_ref[rows, :] += dv[b * T:(b + 1) * T]
                out.append((dq + _nn(dz, kcat), older))
            return out

        def older_blocks(n):
            col_max = jnp.max(saved[n], axis=0, keepdims=True)
            lane_row = lax.broadcasted_iota(jnp.int32, (1, LANES), 1)
            reached = jnp.sum(jnp.where(jnp.logical_and(col_max > EXHAUSTED, lane_row < diag[n]), 1, 0))
            init = (jnp.zeros((2 * T, LANES), F32), jnp.zeros((2 * T, 1), F32))
            return lax.fori_loop(diag[n] - reached, diag[n] - 1, lambda j, c: sweep([(n, [(j, False)], c)])[0], init)

        def run(last_blocks):
            carries = sweep([(n, last_blocks[n], older_blocks(n)) for n in range(jobs)])
            for n in range(jobs):
                dq_ref[n] = (pair.unstack(carries[n][0]) * 0.125).astype(BF16)

        with_previous = lambda d: [(d - 1, False), (d, True)]

        @pl.when(i == 0)
        def _():
            run([[(diag[0], True)]] + [with_previous(d) for d in diag[1:]])

        @pl.when(i > 0)
        def _():
            run([with_previous(d) for d in diag])

    blk = lambda n: pl.BlockSpec((jobs, T, n), lambda p, i: (0, i, p))
    full = pl.BlockSpec((S, LANES), lambda p, i: (0, p))
    out = jax.ShapeDtypeStruct((S, ATTN_WIDTH), F32)
    dq, dk, dv = pl.pallas_call(
        body, name=f"attn_bwd_l{layer}", grid=(HEAD_PAIRS, per_job),
        in_specs=[blk(LANES), full, full, blk(2 * LANES), blk(LANES)],
        out_specs=[blk(LANES), full, full],
        out_shape=[jax.ShapeDtypeStruct((jobs, S // jobs, ATTN_WIDTH), BF16), out, out],
        compiler_params=_params(),
    )(_sections(q, jobs), k, v, _sections(saved, jobs), _sections(do, jobs))
    return dq.reshape(S, ATTN_WIDTH), dk, dv


def _pool_counts(row0, tm):
    pos = row0 + lax.broadcasted_iota(jnp.int32, (tm, 1), 0)
    return [1.0 / jnp.minimum(pos + 1, w).astype(F32) for w in POOL_WINDOWS]


def _window_bands(tm, backward):
    t = np.arange(tm)[:, None]
    c = np.arange(tm)[None, :]
    off = c - t if backward else t - c
    main = np.stack([(off >= 0) & (off < w) for w in POOL_WINDOWS])
    r = np.arange(POOL_HALO)[:, None]
    h = np.arange(POOL_HALO)[None, :]
    off = h - r + POOL_HALO if backward else r - h + POOL_HALO
    edge = np.concatenate([(off < w) for w in POOL_WINDOWS])
    return jnp.asarray(main, BF16), jnp.asarray(edge, BF16)


def _window_sums(tile, beside, main_ref, edge_ref, backward):
    tm = tile.shape[0]
    tb = tile.astype(BF16)
    edge = _nn(edge_ref[...], beside.astype(BF16))
    sums = []
    for g in range(len(POOL_WINDOWS)):
        cols = slice(g * POOL_GROUP, (g + 1) * POOL_GROUP)
        tot = _nn(main_ref[g], tb[:, cols])
        extra = edge[g * POOL_HALO:(g + 1) * POOL_HALO, cols]
        if backward:
            sums.append(jnp.concatenate([tot[:tm - POOL_HALO], tot[tm - POOL_HALO:] + extra], axis=0))
        else:
            sums.append(jnp.concatenate([tot[:POOL_HALO] + extra, tot[POOL_HALO:]], axis=0))
    return sums


def _post_forward(u, history, bands, inv_cnt, zp, o, za, gl, bg, pw_ref, scale, wpu_ref, wau_ref):
    pooled, mixed = [], []
    for g, tot in enumerate(_window_sums(u, history, *bands, False)):
        pg = (tot * inv_cnt[g] - u[:, g * POOL_GROUP:(g + 1) * POOL_GROUP]).astype(BF16)
        pooled.append(pg)
        mixed.append(_nn(pg, pw_ref[g].astype(BF16)))
    pooled = jnp.concatenate(pooled, axis=1)
    mixed = jnp.concatenate(mixed, axis=1)
    zp, za, o = zp.astype(F32), za.astype(F32), o.astype(F32)
    sp = _sigmoid(zp)
    sa = _sigmoid(za)
    y_pool = (mixed * scale) * (zp * sp)
    y_attn = o * (za * sa)
    gate = _sigmoid(gl + bg)
    g0, g1 = gate[:, :D_MODEL], gate[:, D_MODEL:]
    up_p = _nn(y_pool.astype(BF16), wpu_ref[...])
    up_a = _nn(y_attn.astype(BF16), wau_ref[...])
    merged = g0 * up_p + g1 * up_a
    return pooled, mixed, sp, sa, y_pool, y_attn, g0, g1, up_p, up_a, merged


def _row_specs(tm, rev, n_tiles):
    tile_of = (lambda i: n_tiles - 1 - i) if rev else (lambda i: i)
    row = lambda n: pl.BlockSpec((tm, n), lambda i: (tile_of(i), 0))
    halo = pl.BlockSpec((POOL_HALO, POOL_WIDTH),
                        lambda i: (jnp.maximum(tile_of(i) * (tm // POOL_HALO) - 1, 0), 0))
    const = lambda shape: pl.BlockSpec(shape, lambda i: (0,) * len(shape))
    return tile_of, row, halo, const


def _layer_weight_spec(rows, cols, layer):
    return pl.BlockSpec((None, rows, cols), lambda i: (layer, 0, 0), pipeline_mode=pl.Buffered(1))


def _post_fwd(x, u, zp, o, za, gl, bg, pw, scale, wpu, wau, wout, layer, head=()):
    S = x.shape[0]
    tm = min(ROW_TILE, S)
    n_tiles = S // tm
    tile_of, row, halo, const = _row_specs(tm, False, n_tiles)

    def body(x_ref, u_ref, uh_ref, main_ref, edge_ref, zp_ref, o_ref, za_ref, gl_ref, bg_ref, pw_ref, sc_ref, wpu_ref,
             wau_ref, wout_ref, *rest):
        i = pl.program_id(0)
        vals = _post_forward(u_ref[...], jnp.where(i == 0, 0.0, uh_ref[...]), (main_ref, edge_ref),
                             _pool_counts(i * tm, tm), zp_ref[...], o_ref[...], za_ref[...], gl_ref[...], bg_ref[...],
                             pw_ref, sc_ref[...], wpu_ref, wau_ref)
        xv = x_ref[...] + _nn(vals[-1].astype(BF16), wout_ref[...])
        if not head:
            rest[0][...] = xv
            return
        gf_ref, t_ref, loss_ref, dx_ref, dg_ref = rest

        @pl.when(i == 0)
        def _():
            loss_ref[...] = jnp.zeros_like(loss_ref)
            dg_ref[...] = jnp.zeros_like(dg_ref)

        r = lax.rsqrt(jnp.mean(xv * xv, axis=-1, keepdims=True) + RMS_EPS)
        diff = (xv * r) * gf_ref[...] - t_ref[...]
        per_row = jnp.mean(diff * diff, axis=-1, keepdims=True)
        loss_ref[...] += 0.5 * jnp.sum(per_row, axis=0, keepdims=True)
        dx, dg_rows = _rms_backward(diff * (1.0 / D_MODEL), xv, r, gf_ref[...])
        dx_ref[...] = dx
        dg_ref[...] += jnp.sum(dg_rows, axis=0, keepdims=True)

    out = jax.ShapeDtypeStruct((S, D_MODEL), F32)
    return pl.pallas_call(
        body, name=f"post_fwd_l{layer}", grid=(n_tiles,),
        in_specs=[row(D_MODEL), row(512), halo, const((4, tm, tm)), const((4 * POOL_HALO, POOL_HALO)), row(512),
                  row(512), row(512), row(2048), const((1, 2048)), const((4, POOL_GROUP, POOL_GROUP)),
                  const((1, POOL_WIDTH)),
                  _layer_weight_spec(POOL_WIDTH, D_MODEL, layer), _layer_weight_spec(ATTN_WIDTH, D_MODEL, layer),
                  _layer_weight_spec(D_MODEL, D_MODEL, layer)] + ([const((1, D_MODEL)), row(D_MODEL)] if head else []),
        out_specs=[const((1, LANES)), row(D_MODEL), const((1, D_MODEL))] if head else row(D_MODEL),
        out_shape=[jax.ShapeDtypeStruct((1, LANES), F32), out, jax.ShapeDtypeStruct((1, D_MODEL), F32)] if head else out,
        compiler_params=_params(),
    )(x, u, u, *_window_bands(tm, False), zp, o, za, gl, bg, pw, scale, wpu, wau, wout, *head)


def _post_bwd(dx, u, zp, o, za, gl, bg, pw, scale, wpu, wau, wout, layer):
    S = dx.shape[0]
    tm = min(ROW_TILE, S)
    n_tiles = S // tm
    tile_of, row, halo, const = _row_specs(tm, True, n_tiles)

    def body(dx_ref, u_ref, uh_ref, main_ref, edge_ref, back_main_ref, back_edge_ref, zp_ref, o_ref, za_ref, gl_ref,
             bg_ref, pw_ref, sc_ref, wpu_ref, wau_ref, wout_ref,
             duz_ref, do_ref, dzg_ref, dsc_ref, dbg_ref,
             merged_ref, dup_ref, dua_ref, yp_ref, ya_ref, pooled_ref, dmixed_ref, nxt_ref):
        step = pl.program_id(0)
        i = tile_of(step)

        @pl.when(step == 0)
        def _():
            dsc_ref[...] = jnp.zeros_like(dsc_ref)
            dbg_ref[...] = jnp.zeros_like(dbg_ref)
            nxt_ref[...] = jnp.zeros_like(nxt_ref)

        inv_cnt = _pool_counts(i * tm, tm)
        zp, za, o = zp_ref[...].astype(F32), za_ref[...].astype(F32), o_ref[...].astype(F32)
        pooled, mixed, sp, sa, y_pool, y_attn, g0, g1, up_p, up_a, merged = _post_forward(
            u_ref[...], jnp.where(i == 0, 0.0, uh_ref[...]), (main_ref, edge_ref), inv_cnt, zp, o, za, gl_ref[...],
            bg_ref[...], pw_ref, sc_ref[...], wpu_ref, wau_ref)
        merged_ref[...] = merged.astype(BF16)
        yp_ref[...] = y_pool.astype(BF16)
        ya_ref[...] = y_attn.astype(BF16)
        pooled_ref[...] = pooled

        dmerged = _nt(dx_ref[...].astype(BF16), wout_ref[...])
        dup = (dmerged * g0).astype(BF16)
        dua = (dmerged * g1).astype(BF16)
        dup_ref[...] = dup
        dua_ref[...] = dua
        dgl0 = (dmerged * up_p) * (g0 * (1.0 - g0))
        dgl1 = (dmerged * up_a) * (g1 * (1.0 - g1))
        dzg_ref[:, ATTN_WIDTH:ATTN_WIDTH + D_MODEL] = dgl0.astype(BF16)
        dzg_ref[:, ATTN_WIDTH + D_MODEL:] = dgl1.astype(BF16)
        dbg_ref[:, :D_MODEL] += jnp.sum(dgl0, axis=0, keepdims=True)
        dbg_ref[:, D_MODEL:] += jnp.sum(dgl1, axis=0, keepdims=True)

        dy_attn = _nt(dua, wau_ref[...])
        do_ref[...] = (dy_attn * (za * sa)).astype(BF16)
        dzg_ref[:, :ATTN_WIDTH] = ((dy_attn * o) * (sa * (1.0 + za * (1.0 - sa)))).astype(BF16)

        dy_pool = _nt(dup, wpu_ref[...])
        ms = mixed * sc_ref[...]
        dms = dy_pool * (zp * sp)
        duz_ref[:, POOL_WIDTH:] = ((dy_pool * ms) * (sp * (1.0 + zp * (1.0 - sp)))).astype(BF16)
        dsc_ref[...] += jnp.sum(dms * mixed, axis=0, keepdims=True)
        dmixed = (dms * sc_ref[...]).astype(BF16)
        dmixed_ref[...] = dmixed
        dpooled = [_nt(dmixed[:, g * POOL_GROUP:(g + 1) * POOL_GROUP], pw_ref[g].astype(BF16)) for g in range(4)]
        scaled = jnp.concatenate([d * inv for d, inv in zip(dpooled, inv_cnt)], axis=1)
        for g, tot in enumerate(_window_sums(scaled, nxt_ref[...], back_main_ref, back_edge_ref, True)):
            duz_ref[:, g * POOL_GROUP:(g + 1) * POOL_GROUP] = (tot - dpooled[g]).astype(BF16)
        nxt_ref[...] = scaled[:POOL_HALO]

    sd = lambda n, dt: jax.ShapeDtypeStruct((S, n), dt)
    bands = [const((4, tm, tm)), const((4 * POOL_HALO, POOL_HALO))]
    return pl.pallas_call(
        body, name=f"post_bwd_l{layer}", grid=(n_tiles,),
        in_specs=[row(D_MODEL), row(512), halo, *bands, *bands, row(512), row(512), row(512), row(2048),
                  const((1, 2048)), const((4, POOL_GROUP, POOL_GROUP)), const((1, POOL_WIDTH)),
                  _layer_weight_spec(POOL_WIDTH, D_MODEL, layer), _layer_weight_spec(ATTN_WIDTH, D_MODEL, layer),
                  _layer_weight_spec(D_MODEL, D_MODEL, layer)],
        out_specs=[row(1024), row(512), row(2560), const((1, POOL_WIDTH)), const((1, 2048)),
                   row(D_MODEL), row(D_MODEL), row(D_MODEL), row(512), row(512), row(512), row(512)],
        out_shape=[sd(1024, BF16), sd(512, BF16), sd(2560, BF16),
                   jax.ShapeDtypeStruct((1, POOL_WIDTH), F32), jax.ShapeDtypeStruct((1, 2048), F32),
                   sd(D_MODEL, BF16), sd(D_MODEL, BF16), sd(D_MODEL, BF16), sd(512, BF16), sd(512, BF16),
                   sd(512, BF16), sd(512, BF16)],
        scratch_shapes=[pltpu.VMEM((POOL_HALO, POOL_WIDTH), F32)],
        compiler_params=_params(),
    )(dx, u, u, *_window_bands(tm, False), *_window_bands(tm, True), zp, o, za, gl, bg, pw, scale, wpu, wau, wout)


def _rms_backward(dh, xv, r, g):
    xhat = xv * r
    dxhat = dh * g
    return r * (dxhat - xhat * jnp.mean(dxhat * xhat, axis=-1, keepdims=True)), dh * xhat


def _inproj_bwd(pieces, w_in, x, g, dx_res, layer, exchange=()):
    S = x.shape[0]
    tm = min(PROJ_ROW_TILE, S)
    cols = [(c0, p.shape[1]) for p, c0 in pieces]

    def body(ins, outs):
        piece_refs = ins[:len(cols)]
        w_ref, x_ref, g_ref, res_ref = ins[len(cols):]
        dx_ref, dg_ref = outs

        @pl.when(pl.program_id(0) == 0)
        def _():
            dg_ref[...] = jnp.zeros_like(dg_ref)

        dh = jnp.zeros((tm, D_MODEL), F32)
        for p_ref, (c0, n) in zip(piece_refs, cols):
            for c in range(0, n, 512):
                dh = dh + _nt(p_ref[:, c:c + 512].astype(BF16), w_ref[:, c0 + c:c0 + c + 512])
        xv = x_ref[...]
        r = lax.rsqrt(jnp.mean(xv * xv, axis=-1, keepdims=True) + RMS_EPS)
        dx, dg_rows = _rms_backward(dh, xv, r, g_ref[...])
        dx_ref[...] = res_ref[...] + dx
        dg_ref[...] += jnp.sum(dg_rows, axis=0, keepdims=True)

    row = lambda n: pl.BlockSpec((tm, n), lambda i: (i, 0))
    vec = pl.BlockSpec((1, D_MODEL), lambda i: (0, 0))
    any_space = pl.BlockSpec(memory_space=pl.ANY)
    n_x = len(exchange)
    grid = (S // tm,)
    res = pl.pallas_call(
        _with_swap(body, grid, len(cols) + 4, 2, n_x, rider=_ChipExchange), name=f"inproj_bwd_l{layer}", grid=grid,
        in_specs=[row(n) for _, n in cols] + [_layer_weight_spec(D_MODEL, IN_WIDTH, layer), row(D_MODEL), vec,
                                              row(D_MODEL)] + [any_space] * n_x,
        out_specs=[row(D_MODEL), vec] + [any_space] * n_x,
        out_shape=[jax.ShapeDtypeStruct((S, D_MODEL), F32), jax.ShapeDtypeStruct((1, D_MODEL), F32)]
        + _ChipExchange.landing(exchange),
        scratch_shapes=_ChipExchange.semaphores(n_x) if n_x else [],
        compiler_params=_params(),
    )(*[p for p, _ in pieces], w_in, x, g, dx_res, *exchange)
    return res[0], res[1], res[2:]


class _SiblingSwap:
    def __init__(self, mine, theirs, send, recv):
        x, y, c, _ = _place()
        self.copies = [_remote(m.at[1 - c], t, send, recv, a, (x, y, 1 - c))
                       for a, (m, t) in enumerate(zip(mine, theirs))]

    def start(self):
        for cp in self.copies:
            cp.start()

    def wait(self):
        for cp in self.copies:
            cp.wait()


def _with_swap(body, grid, n_in, n_out, n_swap, rider=_SiblingSwap):
    if not n_swap:
        return lambda *refs: body(refs[:n_in], refs[n_in:])

    def riding(*refs):
        ins, mine = refs[:n_in], refs[n_in:n_in + n_swap]
        outs, theirs = refs[n_in + n_swap:n_in + n_swap + n_out], refs[n_in + n_swap + n_out:n_in + 2 * n_swap + n_out]
        swap = rider(mine, theirs, *refs[n_in + 2 * n_swap + n_out:])
        step = [pl.program_id(d) for d in range(len(grid))]
        first, last = step[0] == 0, step[0] == grid[0] - 1
        for d in range(1, len(grid)):
            first, last = jnp.logical_and(first, step[d] == 0), jnp.logical_and(last, step[d] == grid[d] - 1)
        pl.when(first)(swap.start)
        body(ins, outs)
        pl.when(last)(swap.wait)

    return riding


def _swap_specs(swap):
    any_space = pl.BlockSpec(memory_space=pl.ANY)
    shapes = [jax.ShapeDtypeStruct(d.shape[1:], d.dtype) for d in swap]
    sems = [pltpu.SemaphoreType.DMA((len(swap),)), pltpu.SemaphoreType.DMA((len(swap),))] if swap else []
    return [any_space] * len(swap), shapes, sems


def _wgrad(a, b, name, layer, into=None, col0=0, n_total=None, swap=()):
    S, M = a.shape
    N = b.shape[1]
    n_total = N if n_total is None else n_total
    tk = min(2048, S)
    tn = max(t for t in range(LANES, min(N, 1280) + 1, LANES) if N % t == 0 and col0 % t == 0)
    grid = (N // tn, S // tk)

    def body(ins, outs):
        prod = _tn(ins[0][...].astype(BF16), ins[1][...].astype(BF16))

        @pl.when(pl.program_id(1) == 0)
        def _():
            outs[0][...] = prod

        @pl.when(pl.program_id(1) > 0)
        def _():
            outs[0][...] += prod

    in_specs = [pl.BlockSpec((tk, M), lambda j, k: (k, 0)), pl.BlockSpec((tk, tn), lambda j, k: (k, j))]
    args = [a, b]
    aliases = {}
    if into is not None:
        in_specs.append(pl.BlockSpec(memory_space=pl.ANY))
        args.append(into)
        aliases = {2: 0}
    swap_specs, swap_shapes, swap_sems = _swap_specs(swap)
    res = pl.pallas_call(
        _with_swap(body, grid, len(args), 1, len(swap)), name=name, grid=grid,
        in_specs=in_specs + swap_specs,
        out_specs=[pl.BlockSpec((None, M, tn), lambda j, k: (layer, 0, col0 // tn + j))] + swap_specs,
        out_shape=[jax.ShapeDtypeStruct((2, M, n_total), F32)] + swap_shapes,
        input_output_aliases=aliases,
        scratch_shapes=swap_sems,
        compiler_params=_params(),
    )(*args, *swap)
    return (res[0], res[1:]) if swap else res[0]


def _pool_wgrad(pooled, dmixed, layer, swap=()):
    S = pooled.shape[0]
    tk = min(8192, S)
    grid = (4, S // tk)

    def body(ins, outs):
        prod = _tn(ins[0][...], ins[1][...])

        @pl.when(pl.program_id(1) == 0)
        def _():
            outs[0][...] = prod

        @pl.when(pl.program_id(1) > 0)
        def _():
            outs[0][...] += prod

    blk = pl.BlockSpec((tk, POOL_GROUP), lambda g, k: (k, g))
    swap_specs, swap_shapes, swap_sems = _swap_specs(swap)
    res = pl.pallas_call(
        _with_swap(body, grid, 2, 1, len(swap)), name=f"pool_wgrad_l{layer}", grid=grid,
        in_specs=[blk, blk] + swap_specs,
        out_specs=[pl.BlockSpec((None, POOL_GROUP, POOL_GROUP), lambda g, k: (g, 0, 0))] + swap_specs,
        out_shape=[jax.ShapeDtypeStruct((4, POOL_GROUP, POOL_GROUP), F32)] + swap_shapes,
        scratch_shapes=swap_sems,
        compiler_params=_params(),
    )(pooled, dmixed, *swap)
    return (res[0], res[1:]) if swap else res[0]


def _local_step(x, target, norm_g, b_gate, pool_w, pool_scale, final_g, weights, pos):
    n_layers = norm_g.shape[0]
    saved = []
    for l in range(n_layers):
        g = norm_g[l][None]
        bg = b_gate[l][None]
        sc = pool_scale[l][None]
        if l == 0:
            (u, zp, q, k, v, za, gl, h), (w_in, w_pu, w_au, w_out) = _rms_inproj(x, g, None, l, gather=weights)
        else:
            (u, zp, q, k, v, za, gl, h), _ = _rms_inproj(x, g, w_in, l)
        o, carry = _attn_fwd(q, k, v, l)
        saved.append((x, g, bg, sc, u, zp, q, k, v, za, gl, h, o, carry))
        if l < n_layers - 1:
            x = _post_fwd(x, u, zp, o, za, gl, bg, pool_w[l], sc, w_pu, w_au, w_out, l)
        else:
            loss, dx, d_final_g = _post_fwd(x, u, zp, o, za, gl, bg, pool_w[l], sc, w_pu, w_au, w_out, l,
                                            head=(final_g[None], target))

    small = [None] * n_layers
    dw_in = dw_out = dw_pu = dw_au = None
    for l in reversed(range(n_layers)):
        x_in, g, bg, sc, u, zp, q, k, v, za, gl, h, o, carry = saved[l]
        (duz, do, dzg, dsc, dbg, merged, dup, dua, y_pool, y_attn, pooled, dmixed) = _post_bwd(
            dx, u, zp, o, za, gl, bg, pool_w[l], sc, w_pu, w_au, w_out, l)
        dq, dk, dv = _attn_bwd(q, k, v, carry, do, l)
        pieces = [(duz, C_U), (dq, C_Q), (dk, C_K), (dv, C_V), (dzg, C_ZA)]
        for p, c0 in pieces:
            dw_in = _wgrad(h, p, f"wgrad_in_l{l}_c{c0}", l, into=dw_in, col0=c0, n_total=IN_WIDTH)
        if l > 0:
            dw_out = _wgrad(merged, dx, f"wgrad_out_l{l}", l, into=dw_out)
        else:
            dw_out, (other_in,) = _wgrad(merged, dx, f"wgrad_out_l{l}", l, into=dw_out, swap=(dw_in,))
        dw_pu = _wgrad(y_pool, dup, f"wgrad_pu_l{l}", l, into=dw_pu)
        dw_au = _wgrad(y_attn, dua, f"wgrad_au_l{l}", l, into=dw_au)
        if l > 0:
            dpw = _pool_wgrad(pooled, dmixed, l)
        else:
            dpw, (other_pu, other_au, other_out) = _pool_wgrad(pooled, dmixed, l, swap=(dw_pu, dw_au, dw_out))
        if l > 0:
            dx, dg, _ = _inproj_bwd(pieces, w_in, x_in, g, dx, l)
        else:
            pair = [_pair_sum(d, t, pos, f"grad_pair_sum_{n}") for d, t, n in
                    zip((dw_in, dw_pu, dw_au, dw_out), (other_in, other_pu, other_au, other_out), SHARDED_NAMES)]
            dx, dg, landed = _inproj_bwd(pieces, w_in, x_in, g, dx, l, exchange=[pb for _, pb in pair])
        small[l] = (dg[0], dbg[0], dpw, dsc[0])
    small = [jnp.stack([small[l][i] for l in range(n_layers)]) for i in range(4)]
    return loss[0, 0], dx, d_final_g[0], small, [p for p, _ in pair], landed


SHARDED = ((2, 1280), (2, 256), (2, 256), (1, 256))
SHARDED_NAMES = ("w_in", "w_pool_up", "w_attn_up", "w_out")
ANY = pl.BlockSpec(memory_space=pl.ANY)


def _part(ref, s, axis, width):
    sl = pl.ds(pl.multiple_of(s * width, width), width)
    return ref.at[:, sl] if axis == 2 else ref.at[sl, :]


def _place():
    x, y, c = lax.axis_index("x"), lax.axis_index("y"), lax.axis_index("c")
    return x, y, c, 2 * x + y


def _other_chip(x, y, m):
    px = 1 - x if m & 2 else x
    py = 1 - y if m & 1 else y
    return px, py, 2 * px + py


def _remote(src, dst, send, recv, k, to):
    return pltpu.make_async_remote_copy(src_ref=src, dst_ref=dst, send_sem=send.at[k], recv_sem=recv.at[k],
                                        device_id=to, device_id_type=MESH)


def _part_spec(tr, rows_s, cols_s, axis, width, lead):
    if axis == 2:
        return pl.BlockSpec((None, tr, width), lambda *a: (lead(a), a[-2], a[-1][1]))
    return pl.BlockSpec((None, tr, cols_s), lambda *a: (lead(a), a[-1][1] * (rows_s // tr) + a[-2], 0))


def _cast_into_place(w, pos, axis, width, name):
    L, Rs, Cs = w.shape
    tr = min(256, Rs)
    shape = [L, Rs, Cs]
    shape[axis] *= N_CHIPS

    def body(pos_ref, w_ref, o_ref):
        o_ref[...] = w_ref[...].astype(BF16)

    return pl.pallas_call(
        body, name=name,
        grid_spec=pltpu.PrefetchScalarGridSpec(
            num_scalar_prefetch=1, grid=(L, Rs // tr),
            in_specs=[pl.BlockSpec((None, tr, Cs), lambda l, i, pos: (l, i, 0))],
            out_specs=_part_spec(tr, Rs, Cs, axis, width, lambda a: a[0])),
        out_shape=jax.ShapeDtypeStruct(tuple(shape), BF16),
        compiler_params=_params(),
    )(pos, w)


def _w_in_half(layer):
    def piece(refs, who, shard):
        rows = pl.ds(pl.multiple_of(who * (D_MODEL // 2), D_MODEL // 2), D_MODEL // 2)
        return refs[0].at[layer, rows, pl.ds(pl.multiple_of(shard * SHARDED[0][1], SHARDED[0][1]), SHARDED[0][1])]
    return piece


def _whole_layer(a):
    def piece(refs, who, shard):
        return _part(refs[a].at[who], shard, *SHARDED[a])
    return piece


FIRST_PIECES = (_w_in_half(0),)
LATER_PIECES = (_w_in_half(1), _whole_layer(1), _whole_layer(2), _whole_layer(3))


class _Gather:
    def __init__(self, pieces, refs, send, recv):
        self.pieces, self.refs, self.send, self.recv = pieces, refs, send, recv
        self.x, self.y, self.c, s = _place()
        self.first = []
        for u, piece in enumerate(pieces):
            for m in (1, 2, 3):
                px, py, _ = _other_chip(self.x, self.y, m)
                own = piece(refs, self.c, s)
                self.first.append(_remote(own, own, send, recv, 3 * u + m - 1, (px, py, self.c)))

    def start(self):
        for cp in self.first:
            cp.start()

    def _landed(self, u, m, who):
        _, _, sp = _other_chip(self.x, self.y, m)
        return self.pieces[u](self.refs, who, sp)

    def _passed(self):
        n = len(self.pieces)
        return [_remote(self._landed(u, m, self.c), self._landed(u, m, self.c), self.send, self.recv,
                        3 * n + 3 * u + m - 1, (self.x, self.y, 1 - self.c)) for m in (1, 2, 3) for u in range(n)]

    def pass_on(self):
        me = (self.x, self.y, self.c)
        passed = iter(self._passed())
        for m in (1, 2, 3):
            for u in range(len(self.pieces)):
                got = self._landed(u, m, self.c)
                _remote(got, got, self.send, self.recv, 3 * u + m - 1, me).wait_recv()
                next(passed).start()

    def finish(self):
        n = len(self.pieces)
        for m in (1, 2, 3):
            for u in range(n):
                got = self._landed(u, m, 1 - self.c)
                _remote(got, got, self.send, self.recv, 3 * n + 3 * u + m - 1, (self.x, self.y, self.c)).wait_recv()
        for cp in self.first + self._passed():
            cp.wait_send()

    @staticmethod
    def semaphores(pieces):
        return [pltpu.SemaphoreType.DMA((6 * len(pieces),)), pltpu.SemaphoreType.DMA((6 * len(pieces),))]


def _gather_first(fulls):
    n = len(fulls)

    def body(*refs):
        gather = _Gather(FIRST_PIECES, refs[n:2 * n], *refs[2 * n:])
        gather.start()
        gather.pass_on()
        gather.finish()

    return pl.pallas_call(
        body, name="gather_first",
        in_specs=[ANY] * n, out_specs=[ANY] * n,
        out_shape=[jax.ShapeDtypeStruct(f.shape, f.dtype) for f in fulls],
        input_output_aliases={a: a for a in range(n)},
        scratch_shapes=_Gather.semaphores(FIRST_PIECES),
    )(*fulls)


def _pair_sum(dw, other, pos, name):
    _, R, C = dw.shape
    tr = 128 if C > 1024 else 256

    def body(pos_ref, a_ref, b_ref, o_ref, ob_ref):
        tot = a_ref[...] + b_ref[...]
        o_ref[...] = tot
        ob_ref[...] = tot.astype(BF16)

    blk = pl.BlockSpec((tr, C), lambda i, pos: (i, 0))
    return pl.pallas_call(
        body, name=name,
        grid_spec=pltpu.PrefetchScalarGridSpec(
            num_scalar_prefetch=1, grid=(R // tr,),
            in_specs=[pl.BlockSpec((None, tr, C), lambda i, pos: (pos[0], i, 0)), blk],
            out_specs=[blk, blk]),
        out_shape=[jax.ShapeDtypeStruct((R, C), F32), jax.ShapeDtypeStruct((R, C), BF16)],
        compiler_params=_params(),
    )(pos, dw, other)


class _ChipExchange:
    def __init__(self, mine, theirs, send, recv):
        x, y, c, _ = _place()
        self.copies = []
        for a, (axis, width) in enumerate(SHARDED):
            for m in (1, 2, 3):
                px, py, sp = _other_chip(x, y, m)
                self.copies.append(_remote(_part(mine[a], sp, axis, width), theirs[a].at[m - 1], send, recv,
                                           3 * a + m - 1, (px, py, c)))

    def start(self):
        for cp in self.copies:
            cp.start()

    def wait(self):
        for cp in self.copies:
            cp.wait()

    @staticmethod
    def landing(ps):
        shapes = []
        for p, (axis, width) in zip(ps, SHARDED):
            shape = [3] + list(p.shape)
            shape[axis] = width
            shapes.append(jax.ShapeDtypeStruct(tuple(shape), p.dtype))
        return shapes

    @staticmethod
    def semaphores(n):
        return [pltpu.SemaphoreType.DMA((3 * n,)), pltpu.SemaphoreType.DMA((3 * n,))]


def _shard_sum(p, landed, pos, axis, width, name):
    _, Rs, Cs = landed.shape
    tr = min(256, Rs)
    p_spec = _part_spec(tr, Rs, Cs, axis, width, lambda a: 0)

    def body(pos_ref, p_ref, l_ref, o_ref):
        o_ref[...] = ((p_ref[...] + l_ref[0].astype(F32)) + l_ref[1].astype(F32)) + l_ref[2].astype(F32)

    return pl.pallas_call(
        body, name=name,
        grid_spec=pltpu.PrefetchScalarGridSpec(
            num_scalar_prefetch=1, grid=(Rs // tr,),
            in_specs=[p_spec, pl.BlockSpec((3, tr, Cs), lambda i, pos: (0, i, 0))],
            out_specs=pl.BlockSpec((None, tr, Cs), lambda i, pos: (pos[0], i, 0))),
        out_shape=jax.ShapeDtypeStruct((2, Rs, Cs), F32),
        compiler_params=_params(),
    )(pos, p[None], landed)


def _final_exchange(gs, packed):
    n = len(gs)

    def body(*refs):
        small_ref = refs[n]
        outs, total_ref = refs[n + 1:2 * n + 1], refs[2 * n + 1]
        all_ref, send, recv, small_send, small_recv = refs[2 * n + 2:]
        x, y, c, _ = _place()
        my_id = 4 * x + 2 * y + c
        all_ref[my_id] = small_ref[...]
        small = []
        for m in range(1, N_DEVICES):
            px = 1 - x if m & 4 else x
            py = 1 - y if m & 2 else y
            pc = 1 - c if m & 1 else c
            cp = _remote(small_ref, all_ref.at[my_id], small_send, small_recv, m - 1, (px, py, pc))
            cp.start()
            small.append((cp, 4 * px + 2 * py + pc))
        copies = [_remote(outs[a].at[c], outs[a].at[c], send, recv, a, (x, y, 1 - c)) for a in range(n)]
        for cp in copies:
            cp.start()
        for m, (cp, peer_id) in enumerate(small):
            _remote(small_ref, all_ref.at[peer_id], small_send, small_recv, m, (x, y, c)).wait_recv()
        total = all_ref[0]
        for d in range(1, N_DEVICES):
            total = total + all_ref[d]
        total_ref[...] = total
        for cp, _ in small:
            cp.wait_send()
        for a, cp in enumerate(copies):
            cp.wait_send()
            _remote(outs[a].at[1 - c], outs[a].at[1 - c], send, recv, a, (x, y, c)).wait_recv()

    vmem = pl.BlockSpec(memory_space=pltpu.VMEM)
    res = pl.pallas_call(
        body, name="final_exchange",
        in_specs=[ANY] * n + [vmem], out_specs=[ANY] * n + [vmem],
        out_shape=[jax.ShapeDtypeStruct(g.shape, g.dtype) for g in gs]
        + [jax.ShapeDtypeStruct(packed.shape, packed.dtype)],
        input_output_aliases={a: a for a in range(n)},
        scratch_shapes=[pltpu.VMEM((N_DEVICES,) + packed.shape, F32),
                        pltpu.SemaphoreType.DMA((n,)), pltpu.SemaphoreType.DMA((n,)),
                        pltpu.SemaphoreType.DMA((N_DEVICES - 1,)), pltpu.SemaphoreType.DMA((N_DEVICES - 1,))],
        compiler_params=_params(),
    )(*gs, packed)
    return res[:n], res[n]


def _adamw(w, g, m, v, name):
    shape = w.shape
    C = shape[-1]
    flat = [t.reshape(-1, C) for t in (w, g, m, v)]
    R = flat[0].shape[0]
    tr = max(t for t in range(8, R + 1, 8) if R % t == 0 and t * C <= 384 * 1024)

    def body(w_ref, g_ref, m_ref, v_ref, d_ref, nm_ref, nv_ref):
        _adamw_update(w_ref, g_ref, m_ref, v_ref, d_ref, nm_ref, nv_ref)

    blk = pl.BlockSpec((tr, C), lambda i: (i, 0))
    out = jax.ShapeDtypeStruct((R, C), F32)
    res = pl.pallas_call(
        body, name=name, grid=(R // tr,),
        in_specs=[blk] * 4, out_specs=[blk] * 3, out_shape=[out] * 3,
        compiler_params=_params(),
    )(*flat)
    return [t.reshape(shape) for t in res]


def _adamw_update(w_ref, g_ref, m_ref, v_ref, d_ref, nm_ref, nv_ref):
    gv = g_ref[...]
    nm = ADAM_B1 * m_ref[...] + (1.0 - ADAM_B1) * gv
    nv = ADAM_B2 * v_ref[...] + (1.0 - ADAM_B2) * (gv * gv)
    m_hat = nm / (1.0 - ADAM_B1 ** ADAM_STEP)
    v_hat = nv / (1.0 - ADAM_B2 ** ADAM_STEP)
    d_ref[...] = -ADAM_LR * (m_hat / (jnp.sqrt(v_hat) + ADAM_EPS) + ADAM_WD * w_ref[...])
    nm_ref[...] = nm
    nv_ref[...] = nv


def _adamw_small(ws, gs, ms, vs):
    n = len(ws)
    flat = lambda ts: [t.reshape(-1, t.shape[-1]) for t in ts]

    def body(*refs):
        for a in range(n):
            _adamw_update(*[refs[k * n + a] for k in range(7)])

    vmem = pl.BlockSpec(memory_space=pltpu.VMEM)
    shapes = [jax.ShapeDtypeStruct(w.shape, F32) for w in flat(ws)]
    res = pl.pallas_call(
        body, name="adamw_small",
        in_specs=[vmem] * (4 * n), out_specs=[vmem] * (3 * n), out_shape=shapes * 3,
        compiler_params=_params(),
    )(*flat(ws), *flat(gs), *flat(ms), *flat(vs))
    return [[r.reshape(w.shape) for r, w in zip(res[k * n:(k + 1) * n], ws)] for k in range(3)]


SMALL_SHAPES = ((2, 1024), (2, 2048), (2, 4, 128, 128), (2, 512), (1024,))


def _pack_small(parts):
    return jnp.concatenate([p.reshape(-1, LANES) for p in parts], axis=0)


def _unpack_small(packed):
    out, row = [], 0
    for shape in SMALL_SHAPES:
        n = 1
        for d in shape:
            n *= d
        out.append(packed[row:row + n // LANES].reshape(shape))
        row += n // LANES
    return out


def kernel(x, norm_g, w_in, b_gate, pool_w, pool_scale, w_pool_up, w_attn_up, w_out, final_g, loss_target, m_norm_g, m_w_in, m_b_gate, m_pool_w, m_pool_scale, m_w_pool_up, m_w_attn_up, m_w_out, m_final_g, v_norm_g, v_w_in, v_b_gate, v_pool_w, v_pool_scale, v_w_pool_up, v_w_attn_up, v_w_out, v_final_g):
    _, _, c, s = _place()
    pos = jnp.stack([c, s]).astype(jnp.int32)
    names = SHARDED_NAMES

    weights = _gather_first([_cast_into_place(w, pos, axis, width, f"cast_{n}")
                             for w, (axis, width), n in zip((w_in, w_pool_up, w_attn_up, w_out), SHARDED, names)])
    loss_part, dx, d_final_g, small, pair, landed = _local_step(x[0], loss_target[0], norm_g, b_gate, pool_w,
                                                                pool_scale, final_g, weights, pos)
    mine = [_shard_sum(p, l, pos, axis, width, f"grad_shard_sum_{n}")
            for p, l, (axis, width), n in zip(pair, landed, SHARDED, names)]
    (g_in, g_pu, g_au, g_out), summed = _final_exchange(
        mine, _pack_small(small + [d_final_g, jnp.broadcast_to(loss_part, (8, LANES))]))
    g_small = _unpack_small(summed)
    loss = summed[-8, 0]
    d_small, nm_small, nv_small = _adamw_small([norm_g, b_gate, pool_w, pool_scale, final_g], g_small,
                                               [m_norm_g, m_b_gate, m_pool_w, m_pool_scale, m_final_g],
                                               [v_norm_g, v_b_gate, v_pool_w, v_pool_scale, v_final_g])
    upd_in = _adamw(w_in, g_in, m_w_in, v_w_in, "adamw_w_in")
    upd_pu = _adamw(w_pool_up, g_pu, m_w_pool_up, v_w_pool_up, "adamw_w_pool_up")
    upd_au = _adamw(w_attn_up, g_au, m_w_attn_up, v_w_attn_up, "adamw_w_attn_up")
    upd_out = _adamw(w_out, g_out, m_w_out, v_w_out, "adamw_w_out")

    def ordered(sm, k):
        big = (upd_in[k], upd_pu[k], upd_au[k], upd_out[k]) if k is not None else (g_in, g_pu, g_au, g_out)
        return [sm[0], big[0], sm[1], sm[2], sm[3], big[1], big[2], big[3], sm[4]]

    return (loss, dx[None], *ordered(g_small, None), *ordered(d_small, 0), *ordered(nm_small, 1),
            *ordered(nv_small, 2))
```

```python
import jax
import jax.numpy as jnp
import numpy as np
from jax import lax
from jax.experimental import pallas as pl
from jax.experimental.pallas import tpu as pltpu

F32 = jnp.float32
BF16 = jnp.bfloat16
MESH = pl.DeviceIdType.MESH

D_MODEL = 1024
POOL_WIDTH = 512
POOL_WINDOWS = (2, 4, 8, 16)
POOL_GROUP = 128
POOL_HALO = 16
ATTN_WIDTH = 512
HEAD_DIM = 64
HEAD_PAIRS = 4
IN_WIDTH = 5120
N_CHIPS = 4
N_DEVICES = 8
RMS_EPS = 1e-6
C_U, C_ZP, C_Q, C_K, C_V, C_ZA, C_GL = 0, 512, 1024, 1536, 2048, 2560, 3072

ADAM_LR, ADAM_B1, ADAM_B2, ADAM_EPS, ADAM_WD, ADAM_STEP = 0.001, 0.9, 0.999, 1e-08, 0.01, 10

LANES = 128
ATTN_BLOCK = 256
QUERY_BLOCKS = 4
ROW_TILE = 256
PROJ_ROW_TILE = 512
VMEM_LIMIT = 56 * 1024 * 1024
WGRAD_BLOCK_BYTES = 8 * 1024 * 1024


def _params(**kw):
    return pltpu.CompilerParams(vmem_limit_bytes=VMEM_LIMIT, **kw)


def _nt(a, b):
    return lax.dot_general(a, b, (((1,), (1,)), ((), ())), preferred_element_type=F32)


def _tn(a, b):
    return lax.dot_general(a, b, (((0,), (0,)), ((), ())), preferred_element_type=F32)


def _nn(a, b):
    return jnp.dot(a, b, preferred_element_type=F32)


def _sigmoid(z):
    return 1.0 / (1.0 + jnp.exp(-z))


def _rms_inproj(x, g, w_in, layer, gather=()):
    S = x.shape[0]
    tm = min(PROJ_ROW_TILE, S)
    n_tiles = S // tm
    n_g = len(gather)

    def body(*refs):
        x_ref, g_ref = refs[:2]
        if n_g:
            (u_ref, zp_ref, q_ref, k_ref, v_ref, za_ref, gl_ref, h_ref) = refs[2 + n_g:10 + n_g]
            fulls = refs[10 + n_g:10 + 2 * n_g]
            w_ref, load_sem, send, recv = refs[10 + 2 * n_g:]
            later = _Gather(LATER_PIECES, fulls, send, recv)

            @pl.when(pl.program_id(0) == 0)
            def _():
                load = pltpu.make_async_copy(fulls[0].at[layer], w_ref, load_sem)
                load.start()
                later.start()
                load.wait()

            pl.when(pl.program_id(0) == n_tiles - 1)(later.pass_on)
        else:
            w_ref, u_ref, zp_ref, q_ref, k_ref, v_ref, za_ref, gl_ref, h_ref = refs[2:]
        xv = x_ref[...]
        r = lax.rsqrt(jnp.mean(xv * xv, axis=-1, keepdims=True) + RMS_EPS)
        h = ((xv * r) * g_ref[...]).astype(BF16)
        h_ref[...] = h

        def mm(c0, n):
            return _nn(h, w_ref[:, c0:c0 + n])

        u_ref[...] = mm(C_U, 512)
        zp_ref[...] = mm(C_ZP, 512).astype(BF16)
        q_ref[...] = (mm(C_Q, 512) * 0.125).astype(BF16)
        k_ref[...] = mm(C_K, 512).astype(BF16)
        v_ref[...] = mm(C_V, 512).astype(BF16)
        za_ref[...] = mm(C_ZA, 512).astype(BF16)
        for c in range(4):
            gl_ref[:, c * 512:(c + 1) * 512] = mm(C_GL + c * 512, 512).astype(BF16)
        if n_g:
            pl.when(pl.program_id(0) == n_tiles - 1)(later.finish)

    row = lambda n: pl.BlockSpec((tm, n), lambda i: (i, 0))
    sd = lambda n, dt: jax.ShapeDtypeStruct((S, n), dt)
    any_space = pl.BlockSpec(memory_space=pl.ANY)
    weights = [any_space] * n_g if n_g else [_layer_weight_spec(D_MODEL, IN_WIDTH, layer)]
    res = pl.pallas_call(
        body, name=f"rms_inproj_l{layer}", grid=(n_tiles,),
        in_specs=[row(D_MODEL), pl.BlockSpec((1, D_MODEL), lambda i: (0, 0))] + weights,
        out_specs=[row(512), row(512), row(512), row(512), row(512), row(512), row(2048), row(D_MODEL)]
        + [any_space] * n_g,
        out_shape=[sd(512, F32), sd(512, BF16), sd(512, BF16), sd(512, BF16), sd(512, BF16), sd(512, BF16),
                   sd(2048, BF16), sd(D_MODEL, BF16)] + [jax.ShapeDtypeStruct(f.shape, f.dtype) for f in gather],
        input_output_aliases={2 + a: 8 + a for a in range(n_g)},
        scratch_shapes=([pltpu.VMEM((D_MODEL, IN_WIDTH), BF16), pltpu.SemaphoreType.DMA(())]
                        + _Gather.semaphores(LATER_PIECES)) if n_g else [],
        compiler_params=_params(),
    )(x, g, *(gather if n_g else (w_in,)))
    return res[:8], res[8:]


def _tri(n, strict_lower):
    r = lax.broadcasted_iota(jnp.int32, (n, n), 0)
    c = lax.broadcasted_iota(jnp.int32, (n, n), 1)
    return jnp.where(r > c if strict_lower else r < c, 1.0, 0.0).astype(BF16)


def _split_dot(x, m):
    hi = x.astype(BF16)
    lo = (x - hi.astype(F32)).astype(BF16)
    return _nn(hi, m) + _nn(lo, m)


def _log_terms(z):
    lg = jnp.log(1.0 + jnp.exp(-jnp.abs(z)))
    a = jnp.minimum(z, 0.0) - lg
    return a, a - z


EXHAUSTED = -104.0
UNREACHED = -1e30


class _HeadPair:
    def __init__(self, T):
        self.T = T
        self.first = lax.broadcasted_iota(jnp.int32, (T, LANES), 1) < HEAD_DIM
        self.lane = lax.broadcasted_iota(jnp.int32, (2 * T, LANES), 1)
        row = lax.broadcasted_iota(jnp.int32, (2 * T, T), 0)
        row = jnp.where(row >= T, row - T, row)
        self.causal = row > lax.broadcasted_iota(jnp.int32, (2 * T, T), 1)
        self.below = _tri(T, True)

    def stack(self, x2):
        return jnp.concatenate([jnp.where(self.first, x2, 0), jnp.where(self.first, 0, x2)], axis=0).astype(BF16)

    def unstack(self, x):
        return jnp.where(self.first, x[:self.T], x[self.T:])

    def keys(self, ref, blocks):
        T = self.T
        return jnp.concatenate([ref[pl.ds(pl.multiple_of(j * T, T), T), :] for j, _ in blocks], axis=0)

    def log_terms(self, z, blocks):
        T = self.T
        a_all, l_all = _log_terms(z)
        a = [a_all[:, b * T:(b + 1) * T] for b in range(len(blocks))]
        l1m = [l_all[:, b * T:(b + 1) * T] for b in range(len(blocks))]
        return a, [jnp.where(self.causal, l, 0.0) if diagonal else l for l, (_, diagonal) in zip(l1m, blocks)]

    def later_sums(self, l1m):
        later = _split_dot(jnp.concatenate(l1m, axis=0), self.below)
        return [later[2 * self.T * b:2 * self.T * (b + 1)] for b in range(len(l1m))]


def _sections(x, n):
    return x.reshape(n, x.shape[0] // n, x.shape[1])


def _attn_fwd(q, k, v, layer):
    S = q.shape[0]
    T = min(ATTN_BLOCK, S)
    nq = S // T
    jobs = min(QUERY_BLOCKS, nq)
    assert nq <= LANES and nq % jobs == 0
    per_job = nq // jobs

    def body(q_ref, k_ref, v_ref, o_ref, c_ref):
        i = pl.program_id(1)
        pair = _HeadPair(T)
        qs = [pair.stack(q_ref[n]) for n in range(jobs)]
        diag = [i + n * per_job for n in range(jobs)]

        def sweep(jobs):
            kv = [(pair.keys(k_ref, bl), pair.keys(v_ref, bl)) for _, bl, _ in jobs]
            zs = [_nt(qs[n], kcat) for (n, _, _), (kcat, _) in zip(jobs, kv)]
            terms = [pair.log_terms(z, bl) for (_, bl, _), z in zip(jobs, zs)]
            laters = [pair.later_sums(l1m) for _, l1m in terms]
            weights = []
            for (_, bl, (acc, run, saved)), (a, l1m), later in zip(jobs, terms, laters):
                ws = []
                for b, (j, diagonal) in enumerate(bl):
                    saved = jnp.where(pair.lane == j, run, saved)
                    w = jnp.exp(a[b] + later[b] + run)
                    ws.append(jnp.where(pair.causal, w, 0.0) if diagonal else w)
                    run = run + jnp.sum(l1m[b], axis=1, keepdims=True)
                weights.append((jnp.concatenate(ws, axis=1).astype(BF16), acc, run, saved))
            return [(acc + _nn(w, vcat), run, saved) for (w, acc, run, saved), (_, vcat) in zip(weights, kv)]

        def alive(carry):
            return (jnp.max(carry[1]) > EXHAUSTED).astype(jnp.int32)

        def older_blocks(n, carry):
            def older_block(state):
                j, _, c = state
                c = sweep([(n, [(j, False)], c)])[0]
                return j - 1, alive(c), c

            return lax.while_loop(lambda st: jnp.logical_and(st[0] >= 0, st[1] > 0), older_block,
                                  (diag[n] - 2, alive(carry), carry))[2]

        def run(first_blocks):
            init = (jnp.zeros((2 * T, LANES), F32), jnp.zeros((2 * T, 1), F32),
                    jnp.full((2 * T, LANES), UNREACHED, F32))
            carries = sweep([(n, first_blocks[n], init) for n in range(jobs)])
            for n in range(jobs):
                acc, _, saved = older_blocks(n, carries[n])
                o_ref[n] = pair.unstack(acc).astype(BF16)
                c_ref[n, :, :LANES] = saved[:T]
                c_ref[n, :, LANES:] = saved[T:]

        with_previous = lambda d: [(d, True), (d - 1, False)]

        @pl.when(i == 0)
        def _():
            run([[(diag[0], True)]] + [with_previous(d) for d in diag[1:]])

        @pl.when(i > 0)
        def _():
            run([with_previous(d) for d in diag])

    blk = lambda n: pl.BlockSpec((jobs, T, n), lambda p, i: (0, i, p))
    full = pl.BlockSpec((S, LANES), lambda p, i: (0, p))
    o, carry = pl.pallas_call(
        body, name=f"attn_fwd_l{layer}", grid=(HEAD_PAIRS, per_job),
        in_specs=[blk(LANES), full, full],
        out_specs=[blk(LANES), blk(2 * LANES)],
        out_shape=[jax.ShapeDtypeStruct((jobs, S // jobs, ATTN_WIDTH), BF16),
                   jax.ShapeDtypeStruct((jobs, S // jobs, 8 * LANES), F32)],
        compiler_params=_params(),
    )(_sections(q, jobs), k, v)
    return o.reshape(S, ATTN_WIDTH), carry.reshape(S, 8 * LANES)


def _attn_bwd(q, k, v, saved, do, layer):
    S = q.shape[0]
    T = min(ATTN_BLOCK, S)
    nq = S // T
    jobs = min(QUERY_BLOCKS, nq)
    per_job = nq // jobs

    def body(q_ref, k_ref, v_ref, c_ref, do_ref, dq_ref, dk_ref, dv_ref):
        i = pl.program_id(1)

        @pl.when(i == 0)
        def _():
            dk_ref[...] = jnp.zeros_like(dk_ref)
            dv_ref[...] = jnp.zeros_like(dv_ref)

        pair = _HeadPair(T)
        diag = [i + n * per_job for n in range(jobs)]
        qs = [pair.stack(q_ref[n]) for n in range(jobs)]
        dos = [pair.stack(do_ref[n].astype(BF16)) for n in range(jobs)]
        saved = [jnp.concatenate([c_ref[n, :, :LANES], c_ref[n, :, LANES:]], axis=0) for n in range(jobs)]
        before = _tri(T, False)

        def sweep(jobs):
            kv = [(pair.keys(k_ref, bl), pair.keys(v_ref, bl)) for _, bl, _ in jobs]
            zs = [_nt(qs[n], kcat) for (n, _, _), (kcat, _) in zip(jobs, kv)]
            gs = [_nt(dos[n], vcat) for (n, _, _), (_, vcat) in zip(jobs, kv)]
            terms = [pair.log_terms(z, bl) for (_, bl, _), z in zip(jobs, zs)]
            laters = [pair.later_sums(l1m) for _, l1m in terms]
            ws, es = [], []
            for (n, bl, _), (a, _), later, g in zip(jobs, terms, laters, gs):
                w_job, e_job = [], []
                for b, (j, diagonal) in enumerate(bl):
                    run = jnp.sum(jnp.where(pair.lane == j, saved[n], 0.0), axis=1, keepdims=True)
                    w = jnp.exp(a[b] + later[b] + run)
                    w_job.append(jnp.where(pair.causal, w, 0.0) if diagonal else w)
                    e_job.append(w_job[b] * g[:, b * T:(b + 1) * T])
                ws.append(w_job)
                es.append(e_job)
            prefixes = [_nn(jnp.concatenate(e_job, axis=0).astype(BF16), before) for e_job in es]
            dzs, olders = [], []
            for (_, bl, (_, older)), (a, _), e_job, prefix in zip(jobs, terms, es, prefixes):
                dz_job = []
                for b, (j, diagonal) in enumerate(bl):
                    dz = e_job[b] - jnp.exp(a[b]) * (e_job[b] + (prefix[2 * T * b:2 * T * (b + 1)] + older))
                    dz_job.append(jnp.where(pair.causal, dz, 0.0) if diagonal else dz)
                    older = older + jnp.sum(e_job[b], axis=1, keepdims=True)
                dzs.append(jnp.concatenate(dz_job, axis=1).astype(BF16))
                olders.append(older)
            out = []
            for (n, bl, (dq, _)), dz, w_job, older, (kcat, _) in zip(jobs, dzs, ws, olders, kv):
                dk = _tn(dz, qs[n])
                dv = _tn(jnp.concatenate(w_job, axis=1).astype(BF16), dos[n])
                for b, (j, _) in enumerate(bl):
                    rows = pl.ds(pl.multiple_of(j * T, T), T)
                    dk_ref[rows, :] += dk[b * T:(b + 1) * T]
                    dv_ref[rows, :] += dv[b * T:(b + 1) * T]
                out.append((dq + _nn(dz, kcat), older))
            return out

        def older_blocks(n):
            col_max = jnp.max(saved[n], axis=0, keepdims=True)
            lane_row = lax.broadcasted_iota(jnp.int32, (1, LANES), 1)
            reached = jnp.sum(jnp.where(jnp.logical_and(col_max > EXHAUSTED, lane_row < diag[n]), 1, 0))
            init = (jnp.zeros((2 * T, LANES), F32), jnp.zeros((2 * T, 1), F32))
            return lax.fori_loop(diag[n] - reached, diag[n] - 1, lambda j, c: sweep([(n, [(j, False)], c)])[0], init)

        def run(last_blocks):
            carries = sweep([(n, last_blocks[n], older_blocks(n)) for n in range(jobs)])
            for n in range(jobs):
                dq_ref[n] = (pair.unstack(carries[n][0]) * 0.125).astype(BF16)

        with_previous = lambda d: [(d - 1, False), (d, True)]

        @pl.when(i == 0)
        def _():
            run([[(diag[0], True)]] + [with_previous(d) for d in diag[1:]])

        @pl.when(i > 0)
        def _():
            run([with_previous(d) for d in diag])

    blk = lambda n: pl.BlockSpec((jobs, T, n), lambda p, i: (0, i, p))
    full = pl.BlockSpec((S, LANES), lambda p, i: (0, p))
    out = jax.ShapeDtypeStruct((S, ATTN_WIDTH), F32)
    dq, dk, dv = pl.pallas_call(
        body, name=f"attn_bwd_l{layer}", grid=(HEAD_PAIRS, per_job),
        in_specs=[blk(LANES), full, full, blk(2 * LANES), blk(LANES)],
        out_specs=[blk(LANES), full, full],
        out_shape=[jax.ShapeDtypeStruct((jobs, S // jobs, ATTN_WIDTH), BF16), out, out],
        compiler_params=_params(),
    )(_sections(q, jobs), k, v, _sections(saved, jobs), _sections(do, jobs))
    return dq.reshape(S, ATTN_WIDTH), dk, dv


def _pool_counts(row0, tm):
    pos = row0 + lax.broadcasted_iota(jnp.int32, (tm, 1), 0)
    return [1.0 / jnp.minimum(pos + 1, w).astype(F32) for w in POOL_WINDOWS]


def _window_bands(tm, backward):
    t = np.arange(tm)[:, None]
    c = np.arange(tm)[None, :]
    off = c - t if backward else t - c
    main = np.stack([(off >= 0) & (off < w) for w in POOL_WINDOWS])
    r = np.arange(POOL_HALO)[:, None]
    h = np.arange(POOL_HALO)[None, :]
    off = h - r + POOL_HALO if backward else r - h + POOL_HALO
    edge = np.concatenate([(off < w) for w in POOL_WINDOWS])
    return jnp.asarray(main, BF16), jnp.asarray(edge, BF16)


def _window_sums(tile, beside, main_ref, edge_ref, backward):
    tm = tile.shape[0]
    tb = tile.astype(BF16)
    edge = _nn(edge_ref[...], beside.astype(BF16))
    sums = []
    for g in range(len(POOL_WINDOWS)):
        cols = slice(g * POOL_GROUP, (g + 1) * POOL_GROUP)
        tot = _nn(main_ref[g], tb[:, cols])
        extra = edge[g * POOL_HALO:(g + 1) * POOL_HALO, cols]
        if backward:
            sums.append(jnp.concatenate([tot[:tm - POOL_HALO], tot[tm - POOL_HALO:] + extra], axis=0))
        else:
            sums.append(jnp.concatenate([tot[:POOL_HALO] + extra, tot[POOL_HALO:]], axis=0))
    return sums


def _post_forward(u, history, bands, inv_cnt, zp, o, za, gl, bg, pw_ref, scale, wpu_ref, wau_ref):
    pooled, mixed = [], []
    for g, tot in enumerate(_window_sums(u, history, *bands, False)):
        pg = (tot * inv_cnt[g] - u[:, g * POOL_GROUP:(g + 1) * POOL_GROUP]).astype(BF16)
        pooled.append(pg)
        mixed.append(_nn(pg, pw_ref[g].astype(BF16)))
    pooled = jnp.concatenate(pooled, axis=1)
    mixed = jnp.concatenate(mixed, axis=1)
    zp, za, o = zp.astype(F32), za.astype(F32), o.astype(F32)
    sp = _sigmoid(zp)
    sa = _sigmoid(za)
    y_pool = (mixed * scale) * (zp * sp)
    y_attn = o * (za * sa)
    gate = _sigmoid(gl + bg)
    g0, g1 = gate[:, :D_MODEL], gate[:, D_MODEL:]
    up_p = _nn(y_pool.astype(BF16), wpu_ref[...])
    up_a = _nn(y_attn.astype(BF16), wau_ref[...])
    merged = g0 * up_p + g1 * up_a
    return pooled, mixed, sp, sa, y_pool, y_attn, g0, g1, up_p, up_a, merged


def _row_specs(tm, rev, n_tiles):
    tile_of = (lambda i: n_tiles - 1 - i) if rev else (lambda i: i)
    row = lambda n: pl.BlockSpec((tm, n), lambda i: (tile_of(i), 0))
    halo = pl.BlockSpec((POOL_HALO, POOL_WIDTH),
                        lambda i: (jnp.maximum(tile_of(i) * (tm // POOL_HALO) - 1, 0), 0))
    const = lambda shape: pl.BlockSpec(shape, lambda i: (0,) * len(shape))
    return tile_of, row, halo, const


def _layer_weight_spec(rows, cols, layer):
    return pl.BlockSpec((None, rows, cols), lambda i: (layer, 0, 0), pipeline_mode=pl.Buffered(1))


def _post_fwd(x, u, zp, o, za, gl, bg, pw, scale, wpu, wau, wout, layer, head=()):
    S = x.shape[0]
    tm = min(ROW_TILE, S)
    n_tiles = S // tm
    tile_of, row, halo, const = _row_specs(tm, False, n_tiles)

    def body(x_ref, u_ref, uh_ref, main_ref, edge_ref, zp_ref, o_ref, za_ref, gl_ref, bg_ref, pw_ref, sc_ref, wpu_ref,
             wau_ref, wout_ref, *rest):
        i = pl.program_id(0)
        vals = _post_forward(u_ref[...], jnp.where(i == 0, 0.0, uh_ref[...]), (main_ref, edge_ref),
                             _pool_counts(i * tm, tm), zp_ref[...], o_ref[...], za_ref[...], gl_ref[...], bg_ref[...],
                             pw_ref, sc_ref[...], wpu_ref, wau_ref)
        xv = x_ref[...] + _nn(vals[-1].astype(BF16), wout_ref[...])
        if not head:
            rest[0][...] = xv
            return
        gf_ref, t_ref, loss_ref, dx_ref, dg_ref = rest

        @pl.when(i == 0)
        def _():
            loss_ref[...] = jnp.zeros_like(loss_ref)
            dg_ref[...] = jnp.zeros_like(dg_ref)

        r = lax.rsqrt(jnp.mean(xv * xv, axis=-1, keepdims=True) + RMS_EPS)
        diff = (xv * r) * gf_ref[...] - t_ref[...]
        per_row = jnp.mean(diff * diff, axis=-1, keepdims=True)
        loss_ref[...] += 0.5 * jnp.sum(per_row, axis=0, keepdims=True)
        dx, dg_rows = _rms_backward(diff * (1.0 / D_MODEL), xv, r, gf_ref[...])
        dx_ref[...] = dx
        dg_ref[...] += jnp.sum(dg_rows, axis=0, keepdims=True)

    out = jax.ShapeDtypeStruct((S, D_MODEL), F32)
    return pl.pallas_call(
        body, name=f"post_fwd_l{layer}", grid=(n_tiles,),
        in_specs=[row(D_MODEL), row(512), halo, const((4, tm, tm)), const((4 * POOL_HALO, POOL_HALO)), row(512),
                  row(512), row(512), row(2048), const((1, 2048)), const((4, POOL_GROUP, POOL_GROUP)),
                  const((1, POOL_WIDTH)),
                  _layer_weight_spec(POOL_WIDTH, D_MODEL, layer), _layer_weight_spec(ATTN_WIDTH, D_MODEL, layer),
                  _layer_weight_spec(D_MODEL, D_MODEL, layer)] + ([const((1, D_MODEL)), row(D_MODEL)] if head else []),
        out_specs=[const((1, LANES)), row(D_MODEL), const((1, D_MODEL))] if head else row(D_MODEL),
        out_shape=[jax.ShapeDtypeStruct((1, LANES), F32), out, jax.ShapeDtypeStruct((1, D_MODEL), F32)] if head else out,
        compiler_params=_params(),
    )(x, u, u, *_window_bands(tm, False), zp, o, za, gl, bg, pw, scale, wpu, wau, wout, *head)


def _post_bwd(dx, u, zp, o, za, gl, bg, pw, scale, wpu, wau, wout, layer):
    S = dx.shape[0]
    tm = min(ROW_TILE, S)
    n_tiles = S // tm
    tile_of, row, halo, const = _row_specs(tm, True, n_tiles)

    def body(dx_ref, u_ref, uh_ref, main_ref, edge_ref, back_main_ref, back_edge_ref, zp_ref, o_ref, za_ref, gl_ref,
             bg_ref, pw_ref, sc_ref, wpu_ref, wau_ref, wout_ref,
             duz_ref, do_ref, dzg_ref, dsc_ref, dbg_ref,
             merged_ref, dup_ref, dua_ref, yp_ref, ya_ref, pooled_ref, dmixed_ref, nxt_ref):
        step = pl.program_id(0)
        i = tile_of(step)

        @pl.when(step == 0)
        def _():
            dsc_ref[...] = jnp.zeros_like(dsc_ref)
            dbg_ref[...] = jnp.zeros_like(dbg_ref)
            nxt_ref[...] = jnp.zeros_like(nxt_ref)

        inv_cnt = _pool_counts(i * tm, tm)
        zp, za, o = zp_ref[...].astype(F32), za_ref[...].astype(F32), o_ref[...].astype(F32)
        pooled, mixed, sp, sa, y_pool, y_attn, g0, g1, up_p, up_a, merged = _post_forward(
            u_ref[...], jnp.where(i == 0, 0.0, uh_ref[...]), (main_ref, edge_ref), inv_cnt, zp, o, za, gl_ref[...],
            bg_ref[...], pw_ref, sc_ref[...], wpu_ref, wau_ref)
        merged_ref[...] = merged.astype(BF16)
        yp_ref[...] = y_pool.astype(BF16)
        ya_ref[...] = y_attn.astype(BF16)
        pooled_ref[...] = pooled

        dmerged = _nt(dx_ref[...].astype(BF16), wout_ref[...])
        dup = (dmerged * g0).astype(BF16)
        dua = (dmerged * g1).astype(BF16)
        dup_ref[...] = dup
        dua_ref[...] = dua
        dgl0 = (dmerged * up_p) * (g0 * (1.0 - g0))
        dgl1 = (dmerged * up_a) * (g1 * (1.0 - g1))
        dzg_ref[:, ATTN_WIDTH:ATTN_WIDTH + D_MODEL] = dgl0.astype(BF16)
        dzg_ref[:, ATTN_WIDTH + D_MODEL:] = dgl1.astype(BF16)
        dbg_ref[:, :D_MODEL] += jnp.sum(dgl0, axis=0, keepdims=True)
        dbg_ref[:, D_MODEL:] += jnp.sum(dgl1, axis=0, keepdims=True)

        dy_attn = _nt(dua, wau_ref[...])
        do_ref[...] = (dy_attn * (za * sa)).astype(BF16)
        dzg_ref[:, :ATTN_WIDTH] = ((dy_attn * o) * (sa * (1.0 + za * (1.0 - sa)))).astype(BF16)

        dy_pool = _nt(dup, wpu_ref[...])
        ms = mixed * sc_ref[...]
        dms = dy_pool * (zp * sp)
        duz_ref[:, POOL_WIDTH:] = ((dy_pool * ms) * (sp * (1.0 + zp * (1.0 - sp)))).astype(BF16)
        dsc_ref[...] += jnp.sum(dms * mixed, axis=0, keepdims=True)
        dmixed = (dms * sc_ref[...]).astype(BF16)
        dmixed_ref[...] = dmixed
        dpooled = [_nt(dmixed[:, g * POOL_GROUP:(g + 1) * POOL_GROUP], pw_ref[g].astype(BF16)) for g in range(4)]
        scaled = jnp.concatenate([d * inv for d, inv in zip(dpooled, inv_cnt)], axis=1)
        for g, tot in enumerate(_window_sums(scaled, nxt_ref[...], back_main_ref, back_edge_ref, True)):
            duz_ref[:, g * POOL_GROUP:(g + 1) * POOL_GROUP] = (tot - dpooled[g]).astype(BF16)
        nxt_ref[...] = scaled[:POOL_HALO]

    sd = lambda n, dt: jax.ShapeDtypeStruct((S, n), dt)
    bands = [const((4, tm, tm)), const((4 * POOL_HALO, POOL_HALO))]
    return pl.pallas_call(
        body, name=f"post_bwd_l{layer}", grid=(n_tiles,),
        in_specs=[row(D_MODEL), row(512), halo, *bands, *bands, row(512), row(512), row(512), row(2048),
                  const((1, 2048)), const((4, POOL_GROUP, POOL_GROUP)), const((1, POOL_WIDTH)),
                  _layer_weight_spec(POOL_WIDTH, D_MODEL, layer), _layer_weight_spec(ATTN_WIDTH, D_MODEL, layer),
                  _layer_weight_spec(D_MODEL, D_MODEL, layer)],
        out_specs=[row(1024), row(512), row(2560), const((1, POOL_WIDTH)), const((1, 2048)),
                   row(D_MODEL), row(D_MODEL), row(D_MODEL), row(512), row(512), row(512), row(512)],
        out_shape=[sd(1024, BF16), sd(512, BF16), sd(2560, BF16),
                   jax.ShapeDtypeStruct((1, POOL_WIDTH), F32), jax.ShapeDtypeStruct((1, 2048), F32),
                   sd(D_MODEL, BF16), sd(D_MODEL, BF16), sd(D_MODEL, BF16), sd(512, BF16), sd(512, BF16),
                   sd(512, BF16), sd(512, BF16)],
        scratch_shapes=[pltpu.VMEM((POOL_HALO, POOL_WIDTH), F32)],
        compiler_params=_params(),
    )(dx, u, u, *_window_bands(tm, False), *_window_bands(tm, True), zp, o, za, gl, bg, pw, scale, wpu, wau, wout)


def _rms_backward(dh, xv, r, g):
    xhat = xv * r
    dxhat = dh * g
    return r * (dxhat - xhat * jnp.mean(dxhat * xhat, axis=-1, keepdims=True)), dh * xhat


def _inproj_bwd(pieces, w_in, x, g, dx_res, layer, exchange=()):
    S = x.shape[0]
    tm = min(PROJ_ROW_TILE, S)
    cols = [(c0, p.shape[1]) for p, c0 in pieces]

    def body(ins, outs):
        piece_refs = ins[:len(cols)]
        w_ref, x_ref, g_ref, res_ref = ins[len(cols):]
        dx_ref, dg_ref = outs

        @pl.when(pl.program_id(0) == 0)
        def _():
            dg_ref[...] = jnp.zeros_like(dg_ref)

        dh = jnp.zeros((tm, D_MODEL), F32)
        for p_ref, (c0, n) in zip(piece_refs, cols):
            for c in range(0, n, 512):
                dh = dh + _nt(p_ref[:, c:c + 512].astype(BF16), w_ref[:, c0 + c:c0 + c + 512])
        xv = x_ref[...]
        r = lax.rsqrt(jnp.mean(xv * xv, axis=-1, keepdims=True) + RMS_EPS)
        dx, dg_rows = _rms_backward(dh, xv, r, g_ref[...])
        dx_ref[...] = res_ref[...] + dx
        dg_ref[...] += jnp.sum(dg_rows, axis=0, keepdims=True)

    row = lambda n: pl.BlockSpec((tm, n), lambda i: (i, 0))
    vec = pl.BlockSpec((1, D_MODEL), lambda i: (0, 0))
    any_space = pl.BlockSpec(memory_space=pl.ANY)
    n_x = len(exchange)
    grid = (S // tm,)
    res = pl.pallas_call(
        _with_swap(body, grid, len(cols) + 4, 2, n_x, rider=_ChipExchange), name=f"inproj_bwd_l{layer}", grid=grid,
        in_specs=[row(n) for _, n in cols] + [_layer_weight_spec(D_MODEL, IN_WIDTH, layer), row(D_MODEL), vec,
                                              row(D_MODEL)] + [any_space] * n_x,
        out_specs=[row(D_MODEL), vec] + [any_space] * n_x,
        out_shape=[jax.ShapeDtypeStruct((S, D_MODEL), F32), jax.ShapeDtypeStruct((1, D_MODEL), F32)]
        + _ChipExchange.landing(exchange),
        scratch_shapes=_ChipExchange.semaphores(n_x) if n_x else [],
        compiler_params=_params(),
    )(*[p for p, _ in pieces], w_in, x, g, dx_res, *exchange)
    return res[0], res[1], res[2:]


class _SiblingSwap:
    def __init__(self, mine, theirs, send, recv):
        x, y, c, _ = _place()
        self.copies = [_remote(m.at[1 - c], t, send, recv, a, (x, y, 1 - c))
                       for a, (m, t) in enumerate(zip(mine, theirs))]

    def start(self):
        for cp in self.copies:
            cp.start()

    def wait(self):
        for cp in self.copies:
            cp.wait()


def _with_swap(body, grid, n_in, n_out, n_swap, rider=_SiblingSwap):
    if not n_swap:
        return lambda *refs: body(refs[:n_in], refs[n_in:])

    def riding(*refs):
        ins, mine = refs[:n_in], refs[n_in:n_in + n_swap]
        outs, theirs = refs[n_in + n_swap:n_in + n_swap + n_out], refs[n_in + n_swap + n_out:n_in + 2 * n_swap + n_out]
        swap = rider(mine, theirs, *refs[n_in + 2 * n_swap + n_out:])
        step = [pl.program_id(d) for d in range(len(grid))]
        first, last = step[0] == 0, step[0] == grid[0] - 1
        for d in range(1, len(grid)):
            first, last = jnp.logical_and(first, step[d] == 0), jnp.logical_and(last, step[d] == grid[d] - 1)
        pl.when(first)(swap.start)
        body(ins, outs)
        pl.when(last)(swap.wait)

    return riding


def _swap_specs(swap):
    any_space = pl.BlockSpec(memory_space=pl.ANY)
    shapes = [jax.ShapeDtypeStruct(d.shape[1:], d.dtype) for d in swap]
    sems = [pltpu.SemaphoreType.DMA((len(swap),)), pltpu.SemaphoreType.DMA((len(swap),))] if swap else []
    return [any_space] * len(swap), shapes, sems


def _wgrad(a, b, name, layer, into=None, col0=0, n_total=None, swap=()):
    S, M = a.shape
    N = b.shape[1]
    n_total = N if n_total is None else n_total
    tn = max(t for t in range(LANES, min(N, 1280) + 1, LANES) if N % t == 0 and col0 % t == 0)
    row_bytes = max(M * a.dtype.itemsize, tn * b.dtype.itemsize)
    tk = min(S, max(t for t in (512, 1024, 2048, 4096) if t * row_bytes <= WGRAD_BLOCK_BYTES))
    grid = (N // tn, S // tk)

    def body(ins, outs):
        prod = _tn(ins[0][...].astype(BF16), ins[1][...].astype(BF16))

        @pl.when(pl.program_id(1) == 0)
        def _():
            outs[0][...] = prod

        @pl.when(pl.program_id(1) > 0)
        def _():
            outs[0][...] += prod

    in_specs = [pl.BlockSpec((tk, M), lambda j, k: (k, 0)), pl.BlockSpec((tk, tn), lambda j, k: (k, j))]
    args = [a, b]
    aliases = {}
    if into is not None:
        in_specs.append(pl.BlockSpec(memory_space=pl.ANY))
        args.append(into)
        aliases = {2: 0}
    swap_specs, swap_shapes, swap_sems = _swap_specs(swap)
    res = pl.pallas_call(
        _with_swap(body, grid, len(args), 1, len(swap)), name=name, grid=grid,
        in_specs=in_specs + swap_specs,
        out_specs=[pl.BlockSpec((None, M, tn), lambda j, k: (layer, 0, col0 // tn + j))] + swap_specs,
        out_shape=[jax.ShapeDtypeStruct((2, M, n_total), F32)] + swap_shapes,
        input_output_aliases=aliases,
        scratch_shapes=swap_sems,
        compiler_params=_params(),
    )(*args, *swap)
    return (res[0], res[1:]) if swap else res[0]


def _pool_wgrad(pooled, dmixed, layer, swap=()):
    S = pooled.shape[0]
    tk = min(8192, S)
    grid = (4, S // tk)

    def body(ins, outs):
        prod = _tn(ins[0][...], ins[1][...])

        @pl.when(pl.program_id(1) == 0)
        def _():
            outs[0][...] = prod

        @pl.when(pl.program_id(1) > 0)
        def _():
            outs[0][...] += prod

    blk = pl.BlockSpec((tk, POOL_GROUP), lambda g, k: (k, g))
    swap_specs, swap_shapes, swap_sems = _swap_specs(swap)
    res = pl.pallas_call(
        _with_swap(body, grid, 2, 1, len(swap)), name=f"pool_wgrad_l{layer}", grid=grid,
        in_specs=[blk, blk] + swap_specs,
        out_specs=[pl.BlockSpec((None, POOL_GROUP, POOL_GROUP), lambda g, k: (g, 0, 0))] + swap_specs,
        out_shape=[jax.ShapeDtypeStruct((4, POOL_GROUP, POOL_GROUP), F32)] + swap_shapes,
        scratch_shapes=swap_sems,
        compiler_params=_params(),
    )(pooled, dmixed, *swap)
    return (res[0], res[1:]) if swap else res[0]


def _local_step(x, target, norm_g, b_gate, pool_w, pool_scale, final_g, weights, pos):
    n_layers = norm_g.shape[0]
    saved = []
    for l in range(n_layers):
        g = norm_g[l][None]
        bg = b_gate[l][None]
        sc = pool_scale[l][None]
        if l == 0:
            (u, zp, q, k, v, za, gl, h), (w_in, w_pu, w_au, w_out) = _rms_inproj(x, g, None, l, gather=weights)
        else:
            (u, zp, q, k, v, za, gl, h), _ = _rms_inproj(x, g, w_in, l)
        o, carry = _attn_fwd(q, k, v, l)
        saved.append((x, g, bg, sc, u, zp, q, k, v, za, gl, h, o, carry))
        if l < n_layers - 1:
            x = _post_fwd(x, u, zp, o, za, gl, bg, pool_w[l], sc, w_pu, w_au, w_out, l)
        else:
            loss, dx, d_final_g = _post_fwd(x, u, zp, o, za, gl, bg, pool_w[l], sc, w_pu, w_au, w_out, l,
                                            head=(final_g[None], target))

    small = [None] * n_layers
    dw_in = dw_out = dw_pu = dw_au = None
    for l in reversed(range(n_layers)):
        x_in, g, bg, sc, u, zp, q, k, v, za, gl, h, o, carry = saved[l]
        (duz, do, dzg, dsc, dbg, merged, dup, dua, y_pool, y_attn, pooled, dmixed) = _post_bwd(
            dx, u, zp, o, za, gl, bg, pool_w[l], sc, w_pu, w_au, w_out, l)
        dq, dk, dv = _attn_bwd(q, k, v, carry, do, l)
        pieces = [(duz, C_U), (dq, C_Q), (dk, C_K), (dv, C_V), (dzg, C_ZA)]
        for p, c0 in pieces:
            dw_in = _wgrad(h, p, f"wgrad_in_l{l}_c{c0}", l, into=dw_in, col0=c0, n_total=IN_WIDTH)
        if l > 0:
            dw_out = _wgrad(merged, dx, f"wgrad_out_l{l}", l, into=dw_out)
        else:
            dw_out, (other_in,) = _wgrad(merged, dx, f"wgrad_out_l{l}", l, into=dw_out, swap=(dw_in,))
        dw_pu = _wgrad(y_pool, dup, f"wgrad_pu_l{l}", l, into=dw_pu)
        dw_au = _wgrad(y_attn, dua, f"wgrad_au_l{l}", l, into=dw_au)
        if l > 0:
            dpw = _pool_wgrad(pooled, dmixed, l)
        else:
            dpw, (other_pu, other_au, other_out) = _pool_wgrad(pooled, dmixed, l, swap=(dw_pu, dw_au, dw_out))
        if l > 0:
            dx, dg, _ = _inproj_bwd(pieces, w_in, x_in, g, dx, l)
        else:
            pair = [_pair_sum(d, t, pos, f"grad_pair_sum_{n}") for d, t, n in
                    zip((dw_in, dw_pu, dw_au, dw_out), (other_in, other_pu, other_au, other_out), SHARDED_NAMES)]
            dx, dg, landed = _inproj_bwd(pieces, w_in, x_in, g, dx, l, exchange=[pb for _, pb in pair])
        small[l] = (dg[0], dbg[0], dpw, dsc[0])
    small = [jnp.stack([small[l][i] for l in range(n_layers)]) for i in range(4)]
    return loss[0, 0], dx, d_final_g[0], small, [p for p, _ in pair], landed


SHARDED = ((2, 1280), (2, 256), (2, 256), (1, 256))
SHARDED_NAMES = ("w_in", "w_pool_up", "w_attn_up", "w_out")
ANY = pl.BlockSpec(memory_space=pl.ANY)


def _part(ref, s, axis, width):
    sl = pl.ds(pl.multiple_of(s * width, width), width)
    return ref.at[:, sl] if axis == 2 else ref.at[sl, :]


def _place():
    x, y, c = lax.axis_index("x"), lax.axis_index("y"), lax.axis_index("c")
    return x, y, c, 2 * x + y


def _other_chip(x, y, m):
    px = 1 - x if m & 2 else x
    py = 1 - y if m & 1 else y
    return px, py, 2 * px + py


def _remote(src, dst, send, recv, k, to):
    return pltpu.make_async_remote_copy(src_ref=src, dst_ref=dst, send_sem=send.at[k], recv_sem=recv.at[k],
                                        device_id=to, device_id_type=MESH)


def _part_spec(tr, rows_s, cols_s, axis, width, lead):
    if axis == 2:
        return pl.BlockSpec((None, tr, width), lambda *a: (lead(a), a[-2], a[-1][1]))
    return pl.BlockSpec((None, tr, cols_s), lambda *a: (lead(a), a[-1][1] * (rows_s // tr) + a[-2], 0))


def _cast_into_place(w, pos, axis, width, name):
    L, Rs, Cs = w.shape
    tr = min(256, Rs)
    shape = [L, Rs, Cs]
    shape[axis] *= N_CHIPS

    def body(pos_ref, w_ref, o_ref):
        o_ref[...] = w_ref[...].astype(BF16)

    return pl.pallas_call(
        body, name=name,
        grid_spec=pltpu.PrefetchScalarGridSpec(
            num_scalar_prefetch=1, grid=(L, Rs // tr),
            in_specs=[pl.BlockSpec((None, tr, Cs), lambda l, i, pos: (l, i, 0))],
            out_specs=_part_spec(tr, Rs, Cs, axis, width, lambda a: a[0])),
        out_shape=jax.ShapeDtypeStruct(tuple(shape), BF16),
        compiler_params=_params(),
    )(pos, w)


def _w_in_half(layer):
    def piece(refs, who, shard):
        rows = pl.ds(pl.multiple_of(who * (D_MODEL // 2), D_MODEL // 2), D_MODEL // 2)
        return refs[0].at[layer, rows, pl.ds(pl.multiple_of(shard * SHARDED[0][1], SHARDED[0][1]), SHARDED[0][1])]
    return piece


def _whole_layer(a):
    def piece(refs, who, shard):
        return _part(refs[a].at[who], shard, *SHARDED[a])
    return piece


FIRST_PIECES = (_w_in_half(0),)
LATER_PIECES = (_w_in_half(1), _whole_layer(1), _whole_layer(2), _whole_layer(3))


class _Gather:
    def __init__(self, pieces, refs, send, recv):
        self.pieces, self.refs, self.send, self.recv = pieces, refs, send, recv
        self.x, self.y, self.c, s = _place()
        self.first = []
        for u, piece in enumerate(pieces):
            for m in (1, 2, 3):
                px, py, _ = _other_chip(self.x, self.y, m)
                own = piece(refs, self.c, s)
                self.first.append(_remote(own, own, send, recv, 3 * u + m - 1, (px, py, self.c)))

    def start(self):
        for cp in self.first:
            cp.start()

    def _landed(self, u, m, who):
        _, _, sp = _other_chip(self.x, self.y, m)
        return self.pieces[u](self.refs, who, sp)

    def _passed(self):
        n = len(self.pieces)
        return [_remote(self._landed(u, m, self.c), self._landed(u, m, self.c), self.send, self.recv,
                        3 * n + 3 * u + m - 1, (self.x, self.y, 1 - self.c)) for m in (1, 2, 3) for u in range(n)]

    def pass_on(self):
        me = (self.x, self.y, self.c)
        passed = iter(self._passed())
        for m in (1, 2, 3):
            for u in range(len(self.pieces)):
                got = self._landed(u, m, self.c)
                _remote(got, got, self.send, self.recv, 3 * u + m - 1, me).wait_recv()
                next(passed).start()

    def finish(self):
        n = len(self.pieces)
        for m in (1, 2, 3):
            for u in range(n):
                got = self._landed(u, m, 1 - self.c)
                _remote(got, got, self.send, self.recv, 3 * n + 3 * u + m - 1, (self.x, self.y, self.c)).wait_recv()
        for cp in self.first + self._passed():
            cp.wait_send()

    @staticmethod
    def semaphores(pieces):
        return [pltpu.SemaphoreType.DMA((6 * len(pieces),)), pltpu.SemaphoreType.DMA((6 * len(pieces),))]


def _gather_first(fulls):
    n = len(fulls)

    def body(*refs):
        gather = _Gather(FIRST_PIECES, refs[n:2 * n], *refs[2 * n:])
        gather.start()
        gather.pass_on()
        gather.finish()

    return pl.pallas_call(
        body, name="gather_first",
        in_specs=[ANY] * n, out_specs=[ANY] * n,
        out_shape=[jax.ShapeDtypeStruct(f.shape, f.dtype) for f in fulls],
        input_output_aliases={a: a for a in range(n)},
        scratch_shapes=_Gather.semaphores(FIRST_PIECES),
    )(*fulls)


def _pair_sum(dw, other, pos, name):
    _, R, C = dw.shape
    tr = 128 if C > 1024 else 256

    def body(pos_ref, a_ref, b_ref, o_ref, ob_ref):
        tot = a_ref[...] + b_ref[...]
        o_ref[...] = tot
        ob_ref[...] = tot.astype(BF16)

    blk = pl.BlockSpec((tr, C), lambda i, pos: (i, 0))
    return pl.pallas_call(
        body, name=name,
        grid_spec=pltpu.PrefetchScalarGridSpec(
            num_scalar_prefetch=1, grid=(R // tr,),
            in_specs=[pl.BlockSpec((None, tr, C), lambda i, pos: (pos[0], i, 0)), blk],
            out_specs=[blk, blk]),
        out_shape=[jax.ShapeDtypeStruct((R, C), F32), jax.ShapeDtypeStruct((R, C), BF16)],
        compiler_params=_params(),
    )(pos, dw, other)


class _ChipExchange:
    def __init__(self, mine, theirs, send, recv):
        x, y, c, _ = _place()
        self.copies = []
        for a, (axis, width) in enumerate(SHARDED):
            for m in (1, 2, 3):
                px, py, sp = _other_chip(x, y, m)
                self.copies.append(_remote(_part(mine[a], sp, axis, width), theirs[a].at[m - 1], send, recv,
                                           3 * a + m - 1, (px, py, c)))

    def start(self):
        for cp in self.copies:
            cp.start()

    def wait(self):
        for cp in self.copies:
            cp.wait()

    @staticmethod
    def landing(ps):
        shapes = []
        for p, (axis, width) in zip(ps, SHARDED):
            shape = [3] + list(p.shape)
            shape[axis] = width
            shapes.append(jax.ShapeDtypeStruct(tuple(shape), p.dtype))
        return shapes

    @staticmethod
    def semaphores(n):
        return [pltpu.SemaphoreType.DMA((3 * n,)), pltpu.SemaphoreType.DMA((3 * n,))]


def _shard_sum(p, landed, pos, axis, width, name):
    _, Rs, Cs = landed.shape
    tr = min(256, Rs)
    p_spec = _part_spec(tr, Rs, Cs, axis, width, lambda a: 0)

    def body(pos_ref, p_ref, l_ref, o_ref):
        o_ref[...] = ((p_ref[...] + l_ref[0].astype(F32)) + l_ref[1].astype(F32)) + l_ref[2].astype(F32)

    return pl.pallas_call(
        body, name=name,
        grid_spec=pltpu.PrefetchScalarGridSpec(
            num_scalar_prefetch=1, grid=(Rs // tr,),
            in_specs=[p_spec, pl.BlockSpec((3, tr, Cs), lambda i, pos: (0, i, 0))],
            out_specs=pl.BlockSpec((None, tr, Cs), lambda i, pos: (pos[0], i, 0))),
        out_shape=jax.ShapeDtypeStruct((2, Rs, Cs), F32),
        compiler_params=_params(),
    )(pos, p[None], landed)


def _final_exchange(gs, packed):
    n = len(gs)

    def body(*refs):
        small_ref = refs[n]
        outs, total_ref = refs[n + 1:2 * n + 1], refs[2 * n + 1]
        all_ref, send, recv, small_send, small_recv = refs[2 * n + 2:]
        x, y, c, _ = _place()
        my_id = 4 * x + 2 * y + c
        all_ref[my_id] = small_ref[...]
        small = []
        for m in range(1, N_DEVICES):
            px = 1 - x if m & 4 else x
            py = 1 - y if m & 2 else y
            pc = 1 - c if m & 1 else c
            cp = _remote(small_ref, all_ref.at[my_id], small_send, small_recv, m - 1, (px, py, pc))
            cp.start()
            small.append((cp, 4 * px + 2 * py + pc))
        copies = [_remote(outs[a].at[c], outs[a].at[c], send, recv, a, (x, y, 1 - c)) for a in range(n)]
        for cp in copies:
            cp.start()
        for m, (cp, peer_id) in enumerate(small):
            _remote(small_ref, all_ref.at[peer_id], small_send, small_recv, m, (x, y, c)).wait_recv()
        total = all_ref[0]
        for d in range(1, N_DEVICES):
            total = total + all_ref[d]
        total_ref[...] = total
        for cp, _ in small:
            cp.wait_send()
        for a, cp in enumerate(copies):
            cp.wait_send()
            _remote(outs[a].at[1 - c], outs[a].at[1 - c], send, recv, a, (x, y, c)).wait_recv()

    vmem = pl.BlockSpec(memory_space=pltpu.VMEM)
    res = pl.pallas_call(
        body, name="final_exchange",
        in_specs=[ANY] * n + [vmem], out_specs=[ANY] * n + [vmem],
        out_shape=[jax.ShapeDtypeStruct(g.shape, g.dtype) for g in gs]
        + [jax.ShapeDtypeStruct(packed.shape, packed.dtype)],
        input_output_aliases={a: a for a in range(n)},
        scratch_shapes=[pltpu.VMEM((N_DEVICES,) + packed.shape, F32),
                        pltpu.SemaphoreType.DMA((n,)), pltpu.SemaphoreType.DMA((n,)),
                        pltpu.SemaphoreType.DMA((N_DEVICES - 1,)), pltpu.SemaphoreType.DMA((N_DEVICES - 1,))],
        compiler_params=_params(),
    )(*gs, packed)
    return res[:n], res[n]


def _adamw(w, g, m, v, name):
    shape = w.shape
    C = shape[-1]
    flat = [t.reshape(-1, C) for t in (w, g, m, v)]
    R = flat[0].shape[0]
    tr = max(t for t in range(8, R + 1, 8) if R % t == 0 and t * C <= 384 * 1024)

    def body(w_ref, g_ref, m_ref, v_ref, d_ref, nm_ref, nv_ref):
        _adamw_update(w_ref, g_ref, m_ref, v_ref, d_ref, nm_ref, nv_ref)

    blk = pl.BlockSpec((tr, C), lambda i: (i, 0))
    out = jax.ShapeDtypeStruct((R, C), F32)
    res = pl.pallas_call(
        body, name=name, grid=(R // tr,),
        in_specs=[blk] * 4, out_specs=[blk] * 3, out_shape=[out] * 3,
        compiler_params=_params(),
    )(*flat)
    return [t.reshape(shape) for t in res]


def _adamw_update(w_ref, g_ref, m_ref, v_ref, d_ref, nm_ref, nv_ref):
    gv = g_ref[...]
    nm = ADAM_B1 * m_ref[...] + (1.0 - ADAM_B1) * gv
    nv = ADAM_B2 * v_ref[...] + (1.0 - ADAM_B2) * (gv * gv)
    m_hat = nm / (1.0 - ADAM_B1 ** ADAM_STEP)
    v_hat = nv / (1.0 - ADAM_B2 ** ADAM_STEP)
    d_ref[...] = -ADAM_LR * (m_hat / (jnp.sqrt(v_hat) + ADAM_EPS) + ADAM_WD * w_ref[...])
    nm_ref[...] = nm
    nv_ref[...] = nv


def _adamw_small(ws, gs, ms, vs):
    n = len(ws)
    flat = lambda ts: [t.reshape(-1, t.shape[-1]) for t in ts]

    def body(*refs):
        for a in range(n):
            _adamw_update(*[refs[k * n + a] for k in range(7)])

    vmem = pl.BlockSpec(memory_space=pltpu.VMEM)
    shapes = [jax.ShapeDtypeStruct(w.shape, F32) for w in flat(ws)]
    res = pl.pallas_call(
        body, name="adamw_small",
        in_specs=[vmem] * (4 * n), out_specs=[vmem] * (3 * n), out_shape=shapes * 3,
        compiler_params=_params(),
    )(*flat(ws), *flat(gs), *flat(ms), *flat(vs))
    return [[r.reshape(w.shape) for r, w in zip(res[k * n:(k + 1) * n], ws)] for k in range(3)]


SMALL_SHAPES = ((2, 1024), (2, 2048), (2, 4, 128, 128), (2, 512), (1024,))


def _pack_small(parts):
    return jnp.concatenate([p.reshape(-1, LANES) for p in parts], axis=0)


def _unpack_small(packed):
    out, row = [], 0
    for shape in SMALL_SHAPES:
        n = 1
        for d in shape:
            n *= d
        out.append(packed[row:row + n // LANES].reshape(shape))
        row += n // LANES
    return out


def kernel(x, norm_g, w_in, b_gate, pool_w, pool_scale, w_pool_up, w_attn_up, w_out, final_g, loss_target, m_norm_g, m_w_in, m_b_gate, m_pool_w, m_pool_scale, m_w_pool_up, m_w_attn_up, m_w_out, m_final_g, v_norm_g, v_w_in, v_b_gate, v_pool_w, v_pool_scale, v_w_pool_up, v_w_attn_up, v_w_out, v_final_g):
    _, _, c, s = _place()
    pos = jnp.stack([c, s]).astype(jnp.int32)
    names = SHARDED_NAMES

    weights = _gather_first([_cast_into_place(w, pos, axis, width, f"cast_{n}")
                             for w, (axis, width), n in zip((w_in, w_pool_up, w_attn_up, w_out), SHARDED, names)])
    loss_part, dx, d_final_g, small, pair, landed = _local_step(x[0], loss_target[0], norm_g, b_gate, pool_w,
                                                                pool_scale, final_g, weights, pos)
    mine = [_shard_sum(p, l, pos, axis, width, f"grad_shard_sum_{n}")
            for p, l, (axis, width), n in zip(pair, landed, SHARDED, names)]
    (g_in, g_pu, g_au, g_out), summed = _final_exchange(
        mine, _pack_small(small + [d_final_g, jnp.broadcast_to(loss_part, (8, LANES))]))
    g_small = _unpack_small(summed)
    loss = summed[-8, 0]
    d_small, nm_small, nv_small = _adamw_small([norm_g, b_gate, pool_w, pool_scale, final_g], g_small,
                                               [m_norm_g, m_b_gate, m_pool_w, m_pool_scale, m_final_g],
                                               [v_norm_g, v_b_gate, v_pool_w, v_pool_scale, v_final_g])
    upd_in = _adamw(w_in, g_in, m_w_in, v_w_in, "adamw_w_in")
    upd_pu = _adamw(w_pool_up, g_pu, m_w_pool_up, v_w_pool_up, "adamw_w_pool_up")
    upd_au = _adamw(w_attn_up, g_au, m_w_attn_up, v_w_attn_up, "adamw_w_attn_up")
    upd_out = _adamw(w_out, g_out, m_w_out, v_w_out, "adamw_w_out")

    def ordered(sm, k):
        big = (upd_in[k], upd_pu[k], upd_au[k], upd_out[k]) if k is not None else (g_in, g_pu, g_au, g_out)
        return [sm[0], big[0], sm[1], sm[2], sm[3], big[1], big[2], big[3], sm[4]]

    return (loss, dx[None], *ordered(g_small, None), *ordered(d_small, 0), *ordered(nm_small, 1),
            *ordered(nv_small, 2))
```

```python
import jax
import jax.numpy as jnp
import numpy as np
from jax import lax
from jax.experimental import pallas as pl
from jax.experimental.pallas import tpu as pltpu

F32 = jnp.float32
BF16 = jnp.bfloat16
MESH = pl.DeviceIdType.MESH

D_MODEL = 1024
POOL_WIDTH = 512
POOL_WINDOWS = (2, 4, 8, 16)
POOL_GROUP = 128
POOL_HALO = 16
ATTN_WIDTH = 512
HEAD_DIM = 64
HEAD_PAIRS = 4
IN_WIDTH = 5120
N_CHIPS = 4
N_DEVICES = 8
RMS_EPS = 1e-6
C_U, C_ZP, C_Q, C_K, C_V, C_ZA, C_GL = 0, 512, 1024, 1536, 2048, 2560, 3072

ADAM_LR, ADAM_B1, ADAM_B2, ADAM_EPS, ADAM_WD, ADAM_STEP = 0.001, 0.9, 0.999, 1e-08, 0.01, 10

LANES = 128
ATTN_BLOCK = 256
QUERY_BLOCKS = 4
ROW_TILE = 256
PROJ_ROW_TILE = 512
VMEM_LIMIT = 56 * 1024 * 1024


def _params(**kw):
    return pltpu.CompilerParams(vmem_limit_bytes=VMEM_LIMIT, **kw)


def _nt(a, b):
    return lax.dot_general(a, b, (((1,), (1,)), ((), ())), preferred_element_type=F32)


def _tn(a, b):
    return lax.dot_general(a, b, (((0,), (0,)), ((), ())), preferred_element_type=F32)


def _nn(a, b):
    return jnp.dot(a, b, preferred_element_type=F32)


def _sigmoid(z):
    return 1.0 / (1.0 + jnp.exp(-z))


def _rms_inproj(x, g, w_in, layer, gather=()):
    S = x.shape[0]
    tm = min(PROJ_ROW_TILE, S)
    n_tiles = S // tm
    n_g = len(gather)

    def body(*refs):
        x_ref, g_ref = refs[:2]
        if n_g:
            (u_ref, zp_ref, q_ref, k_ref, v_ref, za_ref, gl_ref, h_ref) = refs[2 + n_g:10 + n_g]
            fulls = refs[10 + n_g:10 + 2 * n_g]
            w_ref, load_sem, send, recv = refs[10 + 2 * n_g:]
            later = _Gather(LATER_PIECES, fulls, send, recv)

            @pl.when(pl.program_id(0) == 0)
            def _():
                load = pltpu.make_async_copy(fulls[0].at[layer], w_ref, load_sem)
                load.start()
                later.start()
                load.wait()

            pl.when(pl.program_id(0) == n_tiles - 1)(later.pass_on)
        else:
            w_ref, u_ref, zp_ref, q_ref, k_ref, v_ref, za_ref, gl_ref, h_ref = refs[2:]
        xv = x_ref[...]
        r = lax.rsqrt(jnp.mean(xv * xv, axis=-1, keepdims=True) + RMS_EPS)
        h = ((xv * r) * g_ref[...]).astype(BF16)
        h_ref[...] = h

        def mm(c0, n):
            return _nn(h, w_ref[:, c0:c0 + n])

        u_ref[...] = mm(C_U, 512)
        zp_ref[...] = mm(C_ZP, 512).astype(BF16)
        q_ref[...] = (mm(C_Q, 512) * 0.125).astype(BF16)
        k_ref[...] = mm(C_K, 512).astype(BF16)
        v_ref[...] = mm(C_V, 512).astype(BF16)
        za_ref[...] = mm(C_ZA, 512).astype(BF16)
        for c in range(4):
            gl_ref[:, c * 512:(c + 1) * 512] = mm(C_GL + c * 512, 512).astype(BF16)
        if n_g:
            pl.when(pl.program_id(0) == n_tiles - 1)(later.finish)

    row = lambda n: pl.BlockSpec((tm, n), lambda i: (i, 0))
    sd = lambda n, dt: jax.ShapeDtypeStruct((S, n), dt)
    any_space = pl.BlockSpec(memory_space=pl.ANY)
    weights = [any_space] * n_g if n_g else [_layer_weight_spec(D_MODEL, IN_WIDTH, layer)]
    res = pl.pallas_call(
        body, name=f"rms_inproj_l{layer}", grid=(n_tiles,),
        in_specs=[row(D_MODEL), pl.BlockSpec((1, D_MODEL), lambda i: (0, 0))] + weights,
        out_specs=[row(512), row(512), row(512), row(512), row(512), row(512), row(2048), row(D_MODEL)]
        + [any_space] * n_g,
        out_shape=[sd(512, F32), sd(512, BF16), sd(512, BF16), sd(512, BF16), sd(512, BF16), sd(512, BF16),
                   sd(2048, BF16), sd(D_MODEL, BF16)] + [jax.ShapeDtypeStruct(f.shape, f.dtype) for f in gather],
        input_output_aliases={2 + a: 8 + a for a in range(n_g)},
        scratch_shapes=([pltpu.VMEM((D_MODEL, IN_WIDTH), BF16), pltpu.SemaphoreType.DMA(())]
                        + _Gather.semaphores(LATER_PIECES)) if n_g else [],
        compiler_params=_params(),
    )(x, g, *(gather if n_g else (w_in,)))
    return res[:8], res[8:]


def _tri(n, strict_lower):
    r = lax.broadcasted_iota(jnp.int32, (n, n), 0)
    c = lax.broadcasted_iota(jnp.int32, (n, n), 1)
    return jnp.where(r > c if strict_lower else r < c, 1.0, 0.0).astype(BF16)


def _split_dot(x, m):
    hi = x.astype(BF16)
    lo = (x - hi.astype(F32)).astype(BF16)
    return _nn(hi, m) + _nn(lo, m)


def _log_terms(z):
    lg = jnp.log(1.0 + jnp.exp(-jnp.abs(z)))
    a = jnp.minimum(z, 0.0) - lg
    return a, a - z


EXHAUSTED = -104.0
UNREACHED = -1e30


class _HeadPair:
    def __init__(self, T):
        self.T = T
        self.first = lax.broadcasted_iota(jnp.int32, (T, LANES), 1) < HEAD_DIM
        self.lane = lax.broadcasted_iota(jnp.int32, (2 * T, LANES), 1)
        row = lax.broadcasted_iota(jnp.int32, (2 * T, T), 0)
        row = jnp.where(row >= T, row - T, row)
        self.causal = row > lax.broadcasted_iota(jnp.int32, (2 * T, T), 1)
        self.below = _tri(T, True)

    def stack(self, x2):
        return jnp.concatenate([jnp.where(self.first, x2, 0), jnp.where(self.first, 0, x2)], axis=0).astype(BF16)

    def unstack(self, x):
        return jnp.where(self.first, x[:self.T], x[self.T:])

    def keys(self, ref, blocks):
        T = self.T
        return jnp.concatenate([ref[pl.ds(pl.multiple_of(j * T, T), T), :] for j, _ in blocks], axis=0)

    def log_terms(self, z, blocks):
        T = self.T
        a_all, l_all = _log_terms(z)
        a = [a_all[:, b * T:(b + 1) * T] for b in range(len(blocks))]
        l1m = [l_all[:, b * T:(b + 1) * T] for b in range(len(blocks))]
        return a, [jnp.where(self.causal, l, 0.0) if diagonal else l for l, (_, diagonal) in zip(l1m, blocks)]

    def later_sums(self, l1m):
        later = _split_dot(jnp.concatenate(l1m, axis=0), self.below)
        return [later[2 * self.T * b:2 * self.T * (b + 1)] for b in range(len(l1m))]


def _sections(x, n):
    return x.reshape(n, x.shape[0] // n, x.shape[1])


def _attn_fwd(q, k, v, layer):
    S = q.shape[0]
    T = min(ATTN_BLOCK, S)
    nq = S // T
    jobs = min(QUERY_BLOCKS, nq)
    assert nq <= LANES and nq % jobs == 0
    per_job = nq // jobs

    def body(q_ref, k_ref, v_ref, o_ref, c_ref):
        i = pl.program_id(1)
        pair = _HeadPair(T)
        qs = [pair.stack(q_ref[n]) for n in range(jobs)]
        diag = [i + n * per_job for n in range(jobs)]

        def sweep(jobs):
            kv = [(pair.keys(k_ref, bl), pair.keys(v_ref, bl)) for _, bl, _ in jobs]
            zs = [_nt(qs[n], kcat) for (n, _, _), (kcat, _) in zip(jobs, kv)]
            terms = [pair.log_terms(z, bl) for (_, bl, _), z in zip(jobs, zs)]
            laters = [pair.later_sums(l1m) for _, l1m in terms]
            weights = []
            for (_, bl, (acc, run, saved)), (a, l1m), later in zip(jobs, terms, laters):
                ws = []
                for b, (j, diagonal) in enumerate(bl):
                    saved = jnp.where(pair.lane == j, run, saved)
                    w = jnp.exp(a[b] + later[b] + run)
                    ws.append(jnp.where(pair.causal, w, 0.0) if diagonal else w)
                    run = run + jnp.sum(l1m[b], axis=1, keepdims=True)
                weights.append((jnp.concatenate(ws, axis=1).astype(BF16), acc, run, saved))
            return [(acc + _nn(w, vcat), run, saved) for (w, acc, run, saved), (_, vcat) in zip(weights, kv)]

        def alive(carry):
            return (jnp.max(carry[1]) > EXHAUSTED).astype(jnp.int32)

        def older_blocks(n, carry):
            def older_block(state):
                j, _, c = state
                c = sweep([(n, [(j, False)], c)])[0]
                return j - 1, alive(c), c

            return lax.while_loop(lambda st: jnp.logical_and(st[0] >= 0, st[1] > 0), older_block,
                                  (diag[n] - 2, alive(carry), carry))[2]

        def run(first_blocks):
            init = (jnp.zeros((2 * T, LANES), F32), jnp.zeros((2 * T, 1), F32),
                    jnp.full((2 * T, LANES), UNREACHED, F32))
            carries = sweep([(n, first_blocks[n], init) for n in range(jobs)])
            for n in range(jobs):
                acc, _, saved = older_blocks(n, carries[n])
                o_ref[n] = pair.unstack(acc).astype(BF16)
                c_ref[n, :, :LANES] = saved[:T]
                c_ref[n, :, LANES:] = saved[T:]

        with_previous = lambda d: [(d, True), (d - 1, False)]

        @pl.when(i == 0)
        def _():
            run([[(diag[0], True)]] + [with_previous(d) for d in diag[1:]])

        @pl.when(i > 0)
        def _():
            run([with_previous(d) for d in diag])

    blk = lambda n: pl.BlockSpec((jobs, T, n), lambda p, i: (0, i, p))
    full = pl.BlockSpec((S, LANES), lambda p, i: (0, p))
    o, carry = pl.pallas_call(
        body, name=f"attn_fwd_l{layer}", grid=(HEAD_PAIRS, per_job),
        in_specs=[blk(LANES), full, full],
        out_specs=[blk(LANES), blk(2 * LANES)],
        out_shape=[jax.ShapeDtypeStruct((jobs, S // jobs, ATTN_WIDTH), BF16),
                   jax.ShapeDtypeStruct((jobs, S // jobs, 8 * LANES), F32)],
        compiler_params=_params(),
    )(_sections(q, jobs), k, v)
    return o.reshape(S, ATTN_WIDTH), carry.reshape(S, 8 * LANES)


def _attn_bwd(q, k, v, saved, do, layer):
    S = q.shape[0]
    T = min(ATTN_BLOCK, S)
    nq = S // T
    jobs = min(QUERY_BLOCKS, nq)
    per_job = nq // jobs

    def body(q_ref, k_ref, v_ref, c_ref, do_ref, dq_ref, dk_ref, dv_ref):
        i = pl.program_id(1)

        @pl.when(i == 0)
        def _():
            dk_ref[...] = jnp.zeros_like(dk_ref)
            dv_ref[...] = jnp.zeros_like(dv_ref)

        pair = _HeadPair(T)
        diag = [i + n * per_job for n in range(jobs)]
        qs = [pair.stack(q_ref[n]) for n in range(jobs)]
        dos = [pair.stack(do_ref[n].astype(BF16)) for n in range(jobs)]
        saved = [jnp.concatenate([c_ref[n, :, :LANES], c_ref[n, :, LANES:]], axis=0) for n in range(jobs)]
        before = _tri(T, False)

        def sweep(jobs):
            kv = [(pair.keys(k_ref, bl), pair.keys(v_ref, bl)) for _, bl, _ in jobs]
            zs = [_nt(qs[n], kcat) for (n, _, _), (kcat, _) in zip(jobs, kv)]
            gs = [_nt(dos[n], vcat) for (n, _, _), (_, vcat) in zip(jobs, kv)]
            terms = [pair.log_terms(z, bl) for (_, bl, _), z in zip(jobs, zs)]
            laters = [pair.later_sums(l1m) for _, l1m in terms]
            ws, es = [], []
            for (n, bl, _), (a, _), later, g in zip(jobs, terms, laters, gs):
                w_job, e_job = [], []
                for b, (j, diagonal) in enumerate(bl):
                    run = jnp.sum(jnp.where(pair.lane == j, saved[n], 0.0), axis=1, keepdims=True)
                    w = jnp.exp(a[b] + later[b] + run)
                    w_job.append(jnp.where(pair.causal, w, 0.0) if diagonal else w)
                    e_job.append(w_job[b] * g[:, b * T:(b + 1) * T])
                ws.append(w_job)
                es.append(e_job)
            prefixes = [_nn(jnp.concatenate(e_job, axis=0).astype(BF16), before) for e_job in es]
            dzs, olders = [], []
            for (_, bl, (_, older)), (a, _), e_job, prefix in zip(jobs, terms, es, prefixes):
                dz_job = []
                for b, (j, diagonal) in enumerate(bl):
                    dz = e_job[b] - jnp.exp(a[b]) * (e_job[b] + (prefix[2 * T * b:2 * T * (b + 1)] + older))
                    dz_job.append(jnp.where(pair.causal, dz, 0.0) if diagonal else dz)
                    older = older + jnp.sum(e_job[b], axis=1, keepdims=True)
                dzs.append(jnp.concatenate(dz_job, axis=1).astype(BF16))
                olders.append(older)
            out = []
            for (n, bl, (dq, _)), dz, w_job, older, (kcat, _) in zip(jobs, dzs, ws, olders, kv):
                dk = _tn(dz, qs[n])
                dv = _tn(jnp.concatenate(w_job, axis=1).astype(BF16), dos[n])
                for b, (j, _) in enumerate(bl):
                    rows = pl.ds(pl.multiple_of(j * T, T), T)
                    dk_ref[rows, :] += dk[b * T:(b + 1) * T]
                    dv_ref[rows, :] += dv[b * T:(b + 1) * T]
                out.append((dq + _nn(dz, kcat), older))
            return out

        def older_blocks(n):
            col_max = jnp.max(saved[n], axis=0, keepdims=True)
            lane_row = lax.broadcasted_iota(jnp.int32, (1, LANES), 1)
            reached = jnp.sum(jnp.where(jnp.logical_and(col_max > EXHAUSTED, lane_row < diag[n]), 1, 0))
            init = (jnp.zeros((2 * T, LANES), F32), jnp.zeros((2 * T, 1), F32))
            return lax.fori_loop(diag[n] - reached, diag[n] - 1, lambda j, c: sweep([(n, [(j, False)], c)])[0], init)

        def run(last_blocks):
            carries = sweep([(n, last_blocks[n], older_blocks(n)) for n in range(jobs)])
            for n in range(jobs):
                dq_ref[n] = (pair.unstack(carries[n][0]) * 0.125).astype(BF16)

        with_previous = lambda d: [(d - 1, False), (d, True)]

        @pl.when(i == 0)
        def _():
            run([[(diag[0], True)]] + [with_previous(d) for d in diag[1:]])

        @pl.when(i > 0)
        def _():
            run([with_previous(d) for d in diag])

    blk = lambda n: pl.BlockSpec((jobs, T, n), lambda p, i: (0, i, p))
    full = pl.BlockSpec((S, LANES), lambda p, i: (0, p))
    out = jax.ShapeDtypeStruct((S, ATTN_WIDTH), F32)
    dq, dk, dv = pl.pallas_call(
        body, name=f"attn_bwd_l{layer}", grid=(HEAD_PAIRS, per_job),
        in_specs=[blk(LANES), full, full, blk(2 * LANES), blk(LANES)],
        out_specs=[blk(LANES), full, full],
        out_shape=[jax.ShapeDtypeStruct((jobs, S // jobs, ATTN_WIDTH), BF16), out, out],
        compiler_params=_params(),
    )(_sections(q, jobs), k, v, _sections(saved, jobs), _sections(do, jobs))
    return dq.reshape(S, ATTN_WIDTH), dk, dv


def _pool_counts(row0, tm):
    pos = row0 + lax.broadcasted_iota(jnp.int32, (tm, 1), 0)
    return [1.0 / jnp.minimum(pos + 1, w).astype(F32) for w in POOL_WINDOWS]


def _window_bands(tm, backward):
    t = np.arange(tm)[:, None]
    c = np.arange(tm)[None, :]
    off = c - t if backward else t - c
    main = np.stack([(off >= 0) & (off < w) for w in POOL_WINDOWS])
    r = np.arange(POOL_HALO)[:, None]
    h = np.arange(POOL_HALO)[None, :]
    off = h - r + POOL_HALO if backward else r - h + POOL_HALO
    edge = np.concatenate([(off < w) for w in POOL_WINDOWS])
    return jnp.asarray(main, BF16), jnp.asarray(edge, BF16)


def _window_sums(tile, beside, main_ref, edge_ref, backward):
    tm = tile.shape[0]
    tb = tile.astype(BF16)
    edge = _nn(edge_ref[...], beside.astype(BF16))
    sums = []
    for g in range(len(POOL_WINDOWS)):
        cols = slice(g * POOL_GROUP, (g + 1) * POOL_GROUP)
        tot = _nn(main_ref[g], tb[:, cols])
        extra = edge[g * POOL_HALO:(g + 1) * POOL_HALO, cols]
        if backward:
            sums.append(jnp.concatenate([tot[:tm - POOL_HALO], tot[tm - POOL_HALO:] + extra], axis=0))
        else:
            sums.append(jnp.concatenate([tot[:POOL_HALO] + extra, tot[POOL_HALO:]], axis=0))
    return sums


def _post_forward(u, history, bands, inv_cnt, zp, o, za, gl, bg, pw_ref, scale, wpu_ref, wau_ref):
    pooled, mixed = [], []
    for g, tot in enumerate(_window_sums(u, history, *bands, False)):
        pg = (tot * inv_cnt[g] - u[:, g * POOL_GROUP:(g + 1) * POOL_GROUP]).astype(BF16)
        pooled.append(pg)
        mixed.append(_nn(pg, pw_ref[g].astype(BF16)))
    pooled = jnp.concatenate(pooled, axis=1)
    mixed = jnp.concatenate(mixed, axis=1)
    zp, za, o = zp.astype(F32), za.astype(F32), o.astype(F32)
    sp = _sigmoid(zp)
    sa = _sigmoid(za)
    y_pool = (mixed * scale) * (zp * sp)
    y_attn = o * (za * sa)
    gate = _sigmoid(gl + bg)
    g0, g1 = gate[:, :D_MODEL], gate[:, D_MODEL:]
    up_p = _nn(y_pool.astype(BF16), wpu_ref[...])
    up_a = _nn(y_attn.astype(BF16), wau_ref[...])
    merged = g0 * up_p + g1 * up_a
    return pooled, mixed, sp, sa, y_pool, y_attn, g0, g1, up_p, up_a, merged


def _row_specs(tm, rev, n_tiles):
    tile_of = (lambda i: n_tiles - 1 - i) if rev else (lambda i: i)
    row = lambda n: pl.BlockSpec((tm, n), lambda i: (tile_of(i), 0))
    halo = pl.BlockSpec((POOL_HALO, POOL_WIDTH),
                        lambda i: (jnp.maximum(tile_of(i) * (tm // POOL_HALO) - 1, 0), 0))
    const = lambda shape: pl.BlockSpec(shape, lambda i: (0,) * len(shape))
    return tile_of, row, halo, const


def _layer_weight_spec(rows, cols, layer):
    return pl.BlockSpec((None, rows, cols), lambda i: (layer, 0, 0), pipeline_mode=pl.Buffered(1))


def _post_fwd(x, u, zp, o, za, gl, bg, pw, scale, wpu, wau, wout, layer, head=()):
    S = x.shape[0]
    tm = min(ROW_TILE, S)
    n_tiles = S // tm
    tile_of, row, halo, const = _row_specs(tm, False, n_tiles)

    def body(x_ref, u_ref, uh_ref, main_ref, edge_ref, zp_ref, o_ref, za_ref, gl_ref, bg_ref, pw_ref, sc_ref, wpu_ref,
             wau_ref, wout_ref, *rest):
        i = pl.program_id(0)
        vals = _post_forward(u_ref[...], jnp.where(i == 0, 0.0, uh_ref[...]), (main_ref, edge_ref),
                             _pool_counts(i * tm, tm), zp_ref[...], o_ref[...], za_ref[...], gl_ref[...], bg_ref[...],
                             pw_ref, sc_ref[...], wpu_ref, wau_ref)
        xv = x_ref[...] + _nn(vals[-1].astype(BF16), wout_ref[...])
        if not head:
            rest[0][...] = xv
            return
        gf_ref, t_ref, loss_ref, dx_ref, dg_ref = rest

        @pl.when(i == 0)
        def _():
            loss_ref[...] = jnp.zeros_like(loss_ref)
            dg_ref[...] = jnp.zeros_like(dg_ref)

        r = lax.rsqrt(jnp.mean(xv * xv, axis=-1, keepdims=True) + RMS_EPS)
        diff = (xv * r) * gf_ref[...] - t_ref[...]
        per_row = jnp.mean(diff * diff, axis=-1, keepdims=True)
        loss_ref[...] += 0.5 * jnp.sum(per_row, axis=0, keepdims=True)
        dx, dg_rows = _rms_backward(diff * (1.0 / D_MODEL), xv, r, gf_ref[...])
        dx_ref[...] = dx
        dg_ref[...] += jnp.sum(dg_rows, axis=0, keepdims=True)

    out = jax.ShapeDtypeStruct((S, D_MODEL), F32)
    return pl.pallas_call(
        body, name=f"post_fwd_l{layer}", grid=(n_tiles,),
        in_specs=[row(D_MODEL), row(512), halo, const((4, tm, tm)), const((4 * POOL_HALO, POOL_HALO)), row(512),
                  row(512), row(512), row(2048), const((1, 2048)), const((4, POOL_GROUP, POOL_GROUP)),
                  const((1, POOL_WIDTH)),
                  _layer_weight_spec(POOL_WIDTH, D_MODEL, layer), _layer_weight_spec(ATTN_WIDTH, D_MODEL, layer),
                  _layer_weight_spec(D_MODEL, D_MODEL, layer)] + ([const((1, D_MODEL)), row(D_MODEL)] if head else []),
        out_specs=[const((1, LANES)), row(D_MODEL), const((1, D_MODEL))] if head else row(D_MODEL),
        out_shape=[jax.ShapeDtypeStruct((1, LANES), F32), out, jax.ShapeDtypeStruct((1, D_MODEL), F32)] if head else out,
        compiler_params=_params(),
    )(x, u, u, *_window_bands(tm, False), zp, o, za, gl, bg, pw, scale, wpu, wau, wout, *head)


def _post_bwd(dx, u, zp, o, za, gl, bg, pw, scale, wpu, wau, wout, layer):
    S = dx.shape[0]
    tm = min(ROW_TILE, S)
    n_tiles = S // tm
    tile_of, row, halo, const = _row_specs(tm, True, n_tiles)

    def body(dx_ref, u_ref, uh_ref, main_ref, edge_ref, back_main_ref, back_edge_ref, zp_ref, o_ref, za_ref, gl_ref,
             bg_ref, pw_ref, sc_ref, wpu_ref, wau_ref, wout_ref,
             duz_ref, do_ref, dzg_ref, dsc_ref, dbg_ref,
             merged_ref, dup_ref, dua_ref, yp_ref, ya_ref, pooled_ref, dmixed_ref, nxt_ref):
        step = pl.program_id(0)
        i = tile_of(step)

        @pl.when(step == 0)
        def _():
            dsc_ref[...] = jnp.zeros_like(dsc_ref)
            dbg_ref[...] = jnp.zeros_like(dbg_ref)
            nxt_ref[...] = jnp.zeros_like(nxt_ref)

        inv_cnt = _pool_counts(i * tm, tm)
        zp, za, o = zp_ref[...].astype(F32), za_ref[...].astype(F32), o_ref[...].astype(F32)
        pooled, mixed, sp, sa, y_pool, y_attn, g0, g1, up_p, up_a, merged = _post_forward(
            u_ref[...], jnp.where(i == 0, 0.0, uh_ref[...]), (main_ref, edge_ref), inv_cnt, zp, o, za, gl_ref[...],
            bg_ref[...], pw_ref, sc_ref[...], wpu_ref, wau_ref)
        merged_ref[...] = merged.astype(BF16)
        yp_ref[...] = y_pool.astype(BF16)
        ya_ref[...] = y_attn.astype(BF16)
        pooled_ref[...] = pooled

        dmerged = _nt(dx_ref[...].astype(BF16), wout_ref[...])
        dup = (dmerged * g0).astype(BF16)
        dua = (dmerged * g1).astype(BF16)
        dup_ref[...] = dup
        dua_ref[...] = dua
        dgl0 = (dmerged * up_p) * (g0 * (1.0 - g0))
        dgl1 = (dmerged * up_a) * (g1 * (1.0 - g1))
        dzg_ref[:, ATTN_WIDTH:ATTN_WIDTH + D_MODEL] = dgl0.astype(BF16)
        dzg_ref[:, ATTN_WIDTH + D_MODEL:] = dgl1.astype(BF16)
        dbg_ref[:, :D_MODEL] += jnp.sum(dgl0, axis=0, keepdims=True)
        dbg_ref[:, D_MODEL:] += jnp.sum(dgl1, axis=0, keepdims=True)

        dy_attn = _nt(dua, wau_ref[...])
        do_ref[...] = (dy_attn * (za * sa)).astype(BF16)
        dzg_ref[:, :ATTN_WIDTH] = ((dy_attn * o) * (sa * (1.0 + za * (1.0 - sa)))).astype(BF16)

        dy_pool = _nt(dup, wpu_ref[...])
        ms = mixed * sc_ref[...]
        dms = dy_pool * (zp * sp)
        duz_ref[:, POOL_WIDTH:] = ((dy_pool * ms) * (sp * (1.0 + zp * (1.0 - sp)))).astype(BF16)
        dsc_ref[...] += jnp.sum(dms * mixed, axis=0, keepdims=True)
        dmixed = (dms * sc_ref[...]).astype(BF16)
        dmixed_ref[...] = dmixed
        dpooled = [_nt(dmixed[:, g * POOL_GROUP:(g + 1) * POOL_GROUP], pw_ref[g].astype(BF16)) for g in range(4)]
        scaled = jnp.concatenate([d * inv for d, inv in zip(dpooled, inv_cnt)], axis=1)
        for g, tot in enumerate(_window_sums(scaled, nxt_ref[...], back_main_ref, back_edge_ref, True)):
            duz_ref[:, g * POOL_GROUP:(g + 1) * POOL_GROUP] = (tot - dpooled[g]).astype(BF16)
        nxt_ref[...] = scaled[:POOL_HALO]

    sd = lambda n, dt: jax.ShapeDtypeStruct((S, n), dt)
    bands = [const((4, tm, tm)), const((4 * POOL_HALO, POOL_HALO))]
    return pl.pallas_call(
        body, name=f"post_bwd_l{layer}", grid=(n_tiles,),
        in_specs=[row(D_MODEL), row(512), halo, *bands, *bands, row(512), row(512), row(512), row(2048),
                  const((1, 2048)), const((4, POOL_GROUP, POOL_GROUP)), const((1, POOL_WIDTH)),
                  _layer_weight_spec(POOL_WIDTH, D_MODEL, layer), _layer_weight_spec(ATTN_WIDTH, D_MODEL, layer),
                  _layer_weight_spec(D_MODEL, D_MODEL, layer)],
        out_specs=[row(1024), row(512), row(2560), const((1, POOL_WIDTH)), const((1, 2048)),
                   row(D_MODEL), row(D_MODEL), row(D_MODEL), row(512), row(512), row(512), row(512)],
        out_shape=[sd(1024, BF16), sd(512, BF16), sd(2560, BF16),
                   jax.ShapeDtypeStruct((1, POOL_WIDTH), F32), jax.ShapeDtypeStruct((1, 2048), F32),
                   sd(D_MODEL, BF16), sd(D_MODEL, BF16), sd(D_MODEL, BF16), sd(512, BF16), sd(512, BF16),
                   sd(512, BF16), sd(512, BF16)],
        scratch_shapes=[pltpu.VMEM((POOL_HALO, POOL_WIDTH), F32)],
        compiler_params=_params(),
    )(dx, u, u, *_window_bands(tm, False), *_window_bands(tm, True), zp, o, za, gl, bg, pw, scale, wpu, wau, wout)


def _rms_backward(dh, xv, r, g):
    xhat = xv * r
    dxhat = dh * g
    return r * (dxhat - xhat * jnp.mean(dxhat * xhat, axis=-1, keepdims=True)), dh * xhat


def _inproj_bwd(pieces, w_in, x, g, dx_res, layer, exchange=()):
    S = x.shape[0]
    tm = min(PROJ_ROW_TILE, S)
    cols = [(c0, p.shape[1]) for p, c0 in pieces]

    def body(ins, outs):
        piece_refs = ins[:len(cols)]
        w_ref, x_ref, g_ref, res_ref = ins[len(cols):]
        dx_ref, dg_ref = outs

        @pl.when(pl.program_id(0) == 0)
        def _():
            dg_ref[...] = jnp.zeros_like(dg_ref)

        dh = jnp.zeros((tm, D_MODEL), F32)
        for p_ref, (c0, n) in zip(piece_refs, cols):
            for c in range(0, n, 512):
                dh = dh + _nt(p_ref[:, c:c + 512].astype(BF16), w_ref[:, c0 + c:c0 + c + 512])
        xv = x_ref[...]
        r = lax.rsqrt(jnp.mean(xv * xv, axis=-1, keepdims=True) + RMS_EPS)
        dx, dg_rows = _rms_backward(dh, xv, r, g_ref[...])
        dx_ref[...] = res_ref[...] + dx
        dg_ref[...] += jnp.sum(dg_rows, axis=0, keepdims=True)

    row = lambda n: pl.BlockSpec((tm, n), lambda i: (i, 0))
    vec = pl.BlockSpec((1, D_MODEL), lambda i: (0, 0))
    any_space = pl.BlockSpec(memory_space=pl.ANY)
    n_x = len(exchange)
    grid = (S // tm,)
    res = pl.pallas_call(
        _with_swap(body, grid, len(cols) + 4, 2, n_x, rider=_ChipExchange), name=f"inproj_bwd_l{layer}", grid=grid,
        in_specs=[row(n) for _, n in cols] + [_layer_weight_spec(D_MODEL, IN_WIDTH, layer), row(D_MODEL), vec,
                                              row(D_MODEL)] + [any_space] * n_x,
        out_specs=[row(D_MODEL), vec] + [any_space] * n_x,
        out_shape=[jax.ShapeDtypeStruct((S, D_MODEL), F32), jax.ShapeDtypeStruct((1, D_MODEL), F32)]
        + _ChipExchange.landing(exchange),
        scratch_shapes=_ChipExchange.semaphores(n_x) if n_x else [],
        compiler_params=_params(),
    )(*[p for p, _ in pieces], w_in, x, g, dx_res, *exchange)
    return res[0], res[1], res[2:]


class _SiblingSwap:
    def __init__(self, mine, theirs, send, recv):
        x, y, c, _ = _place()
        self.copies = [_remote(m.at[1 - c], t, send, recv, a, (x, y, 1 - c))
                       for a, (m, t) in enumerate(zip(mine, theirs))]

    def start(self):
        for cp in self.copies:
            cp.start()

    def wait(self):
        for cp in self.copies:
            cp.wait()


def _with_swap(body, grid, n_in, n_out, n_swap, rider=_SiblingSwap):
    if not n_swap:
        return lambda *refs: body(refs[:n_in], refs[n_in:])

    def riding(*refs):
        ins, mine = refs[:n_in], refs[n_in:n_in + n_swap]
        outs, theirs = refs[n_in + n_swap:n_in + n_swap + n_out], refs[n_in + n_swap + n_out:n_in + 2 * n_swap + n_out]
        swap = rider(mine, theirs, *refs[n_in + 2 * n_swap + n_out:])
        step = [pl.program_id(d) for d in range(len(grid))]
        first, last = step[0] == 0, step[0] == grid[0] - 1
        for d in range(1, len(grid)):
            first, last = jnp.logical_and(first, step[d] == 0), jnp.logical_and(last, step[d] == grid[d] - 1)
        pl.when(first)(swap.start)
        body(ins, outs)
        pl.when(last)(swap.wait)

    return riding


def _swap_specs(swap):
    any_space = pl.BlockSpec(memory_space=pl.ANY)
    shapes = [jax.ShapeDtypeStruct(d.shape[1:], d.dtype) for d in swap]
    sems = [pltpu.SemaphoreType.DMA((len(swap),)), pltpu.SemaphoreType.DMA((len(swap),))] if swap else []
    return [any_space] * len(swap), shapes, sems


def _wgrad(a, b, name, layer, into=None, col0=0, n_total=None, swap=()):
    S, M = a.shape
    N = b.shape[1]
    n_total = N if n_total is None else n_total
    tk = min(2048, S)
    tn = max(t for t in range(LANES, min(N, 1280) + 1, LANES) if N % t == 0 and col0 % t == 0)
    grid = (N // tn, S // tk)

    def body(ins, outs):
        prod = _tn(ins[0][...].astype(BF16), ins[1][...].astype(BF16))

        @pl.when(pl.program_id(1) == 0)
        def _():
            outs[0][...] = prod

        @pl.when(pl.program_id(1) > 0)
        def _():
            outs[0][...] += prod

    in_specs = [pl.BlockSpec((tk, M), lambda j, k: (k, 0)), pl.BlockSpec((tk, tn), lambda j, k: (k, j))]
    args = [a, b]
    aliases = {}
    if into is not None:
        in_specs.append(pl.BlockSpec(memory_space=pl.ANY))
        args.append(into)
        aliases = {2: 0}
    swap_specs, swap_shapes, swap_sems = _swap_specs(swap)
    res = pl.pallas_call(
        _with_swap(body, grid, len(args), 1, len(swap)), name=name, grid=grid,
        in_specs=in_specs + swap_specs,
        out_specs=[pl.BlockSpec((None, M, tn), lambda j, k: (layer, 0, col0 // tn + j))] + swap_specs,
        out_shape=[jax.ShapeDtypeStruct((2, M, n_total), F32)] + swap_shapes,
        input_output_aliases=aliases,
        scratch_shapes=swap_sems,
        compiler_params=_params(),
    )(*args, *swap)
    return (res[0], res[1:]) if swap else res[0]


def _pool_wgrad(pooled, dmixed, layer, swap=()):
    S = pooled.shape[0]
    tk = min(8192, S)
    grid = (4, S // tk)

    def body(ins, outs):
        prod = _tn(ins[0][...], ins[1][...])

        @pl.when(pl.program_id(1) == 0)
        def _():
            outs[0][...] = prod

        @pl.when(pl.program_id(1) > 0)
        def _():
            outs[0][...] += prod

    blk = pl.BlockSpec((tk, POOL_GROUP), lambda g, k: (k, g))
    swap_specs, swap_shapes, swap_sems = _swap_specs(swap)
    res = pl.pallas_call(
        _with_swap(body, grid, 2, 1, len(swap)), name=f"pool_wgrad_l{layer}", grid=grid,
        in_specs=[blk, blk] + swap_specs,
        out_specs=[pl.BlockSpec((None, POOL_GROUP, POOL_GROUP), lambda g, k: (g, 0, 0))] + swap_specs,
        out_shape=[jax.ShapeDtypeStruct((4, POOL_GROUP, POOL_GROUP), F32)] + swap_shapes,
        scratch_shapes=swap_sems,
        compiler_params=_params(),
    )(pooled, dmixed, *swap)
    return (res[0], res[1:]) if swap else res[0]


def _local_step(x, target, norm_g, b_gate, pool_w, pool_scale, final_g, weights, pos):
    n_layers = norm_g.shape[0]
    saved = []
    for l in range(n_layers):
        g = norm_g[l][None]
        bg = b_gate[l][None]
        sc = pool_scale[l][None]
        if l == 0:
            (u, zp, q, k, v, za, gl, h), (w_in, w_pu, w_au, w_out) = _rms_inproj(x, g, None, l, gather=weights)
        else:
            (u, zp, q, k, v, za, gl, h), _ = _rms_inproj(x, g, w_in, l)
        o, carry = _attn_fwd(q, k, v, l)
        saved.append((x, g, bg, sc, u, zp, q, k, v, za, gl, h, o, carry))
        if l < n_layers - 1:
            x = _post_fwd(x, u, zp, o, za, gl, bg, pool_w[l], sc, w_pu, w_au, w_out, l)
        else:
            loss, dx, d_final_g = _post_fwd(x, u, zp, o, za, gl, bg, pool_w[l], sc, w_pu, w_au, w_out, l,
                                            head=(final_g[None], target))

    small = [None] * n_layers
    dw_in = dw_out = dw_pu = dw_au = None
    for l in reversed(range(n_layers)):
        x_in, g, bg, sc, u, zp, q, k, v, za, gl, h, o, carry = saved[l]
        (duz, do, dzg, dsc, dbg, merged, dup, dua, y_pool, y_attn, pooled, dmixed) = _post_bwd(
            dx, u, zp, o, za, gl, bg, pool_w[l], sc, w_pu, w_au, w_out, l)
        dq, dk, dv = _attn_bwd(q, k, v, carry, do, l)
        pieces = [(duz, C_U), (dq, C_Q), (dk, C_K), (dv, C_V), (dzg, C_ZA)]
        for p, c0 in pieces:
            dw_in = _wgrad(h, p, f"wgrad_in_l{l}_c{c0}", l, into=dw_in, col0=c0, n_total=IN_WIDTH)
        if l > 0:
            dw_out = _wgrad(merged, dx, f"wgrad_out_l{l}", l, into=dw_out)
        else:
            dw_out, (other_in,) = _wgrad(merged, dx, f"wgrad_out_l{l}", l, into=dw_out, swap=(dw_in,))
        dw_pu = _wgrad(y_pool, dup, f"wgrad_pu_l{l}", l, into=dw_pu)
        dw_au = _wgrad(y_attn, dua, f"wgrad_au_l{l}", l, into=dw_au)
        if l > 0:
            dpw = _pool_wgrad(pooled, dmixed, l)
        else:
            dpw, (other_pu, other_au, other_out) = _pool_wgrad(pooled, dmixed, l, swap=(dw_pu, dw_au, dw_out))
        if l > 0:
            dx, dg, _ = _inproj_bwd(pieces, w_in, x_in, g, dx, l)
        else:
            pair = [_pair_sum(d, t, pos, f"grad_pair_sum_{n}") for d, t, n in
                    zip((dw_in, dw_pu, dw_au, dw_out), (other_in, other_pu, other_au, other_out), SHARDED_NAMES)]
            dx, dg, landed = _inproj_bwd(pieces, w_in, x_in, g, dx, l, exchange=[pb for _, pb in pair])
        small[l] = (dg[0], dbg[0], dpw, dsc[0])
    small = [jnp.stack([small[l][i] for l in range(n_layers)]) for i in range(4)]
    return loss[0, 0], dx, d_final_g[0], small, [p for p, _ in pair], landed


SHARDED = ((2, 1280), (2, 256), (2, 256), (1, 256))
SHARDED_NAMES = ("w_in", "w_pool_up", "w_attn_up", "w_out")
ANY = pl.BlockSpec(memory_space=pl.ANY)


def _part(ref, s, axis, width):
    sl = pl.ds(pl.multiple_of(s * width, width), width)
    return ref.at[:, sl] if axis == 2 else ref.at[sl, :]


def _place():
    x, y, c = lax.axis_index("x"), lax.axis_index("y"), lax.axis_index("c")
    return x, y, c, 2 * x + y


def _other_chip(x, y, m):
    px = 1 - x if m & 2 else x
    py = 1 - y if m & 1 else y
    return px, py, 2 * px + py


def _remote(src, dst, send, recv, k, to):
    return pltpu.make_async_remote_copy(src_ref=src, dst_ref=dst, send_sem=send.at[k], recv_sem=recv.at[k],
                                        device_id=to, device_id_type=MESH)


def _part_spec(tr, rows_s, cols_s, axis, width, lead):
    if axis == 2:
        return pl.BlockSpec((None, tr, width), lambda *a: (lead(a), a[-2], a[-1][1]))
    return pl.BlockSpec((None, tr, cols_s), lambda *a: (lead(a), a[-1][1] * (rows_s // tr) + a[-2], 0))


def _cast_into_place(w, pos, axis, width, name):
    L, Rs, Cs = w.shape
    tr = min(256, Rs)
    shape = [L, Rs, Cs]
    shape[axis] *= N_CHIPS

    def body(pos_ref, w_ref, o_ref):
        o_ref[...] = w_ref[...].astype(BF16)

    return pl.pallas_call(
        body, name=name,
        grid_spec=pltpu.PrefetchScalarGridSpec(
            num_scalar_prefetch=1, grid=(L, Rs // tr),
            in_specs=[pl.BlockSpec((None, tr, Cs), lambda l, i, pos: (l, i, 0))],
            out_specs=_part_spec(tr, Rs, Cs, axis, width, lambda a: a[0])),
        out_shape=jax.ShapeDtypeStruct(tuple(shape), BF16),
        compiler_params=_params(),
    )(pos, w)


def _w_in_half(layer):
    def piece(refs, who, shard):
        rows = pl.ds(pl.multiple_of(who * (D_MODEL // 2), D_MODEL // 2), D_MODEL // 2)
        return refs[0].at[layer, rows, pl.ds(pl.multiple_of(shard * SHARDED[0][1], SHARDED[0][1]), SHARDED[0][1])]
    return piece


def _whole_layer(a):
    def piece(refs, who, shard):
        return _part(refs[a].at[who], shard, *SHARDED[a])
    return piece


FIRST_PIECES = (_w_in_half(0),)
LATER_PIECES = (_w_in_half(1), _whole_layer(1), _whole_layer(2), _whole_layer(3))


class _Gather:
    def __init__(self, pieces, refs, send, recv):
        self.pieces, self.refs, self.send, self.recv = pieces, refs, send, recv
        self.x, self.y, self.c, s = _place()
        self.first = []
        for u, piece in enumerate(pieces):
            for m in (1, 2, 3):
                px, py, _ = _other_chip(self.x, self.y, m)
                own = piece(refs, self.c, s)
                self.first.append(_remote(own, own, send, recv, 3 * u + m - 1, (px, py, self.c)))

    def start(self):
        for cp in self.first:
            cp.start()

    def _landed(self, u, m, who):
        _, _, sp = _other_chip(self.x, self.y, m)
        return self.pieces[u](self.refs, who, sp)

    def _passed(self):
        n = len(self.pieces)
        return [_remote(self._landed(u, m, self.c), self._landed(u, m, self.c), self.send, self.recv,
                        3 * n + 3 * u + m - 1, (self.x, self.y, 1 - self.c)) for m in (1, 2, 3) for u in range(n)]

    def pass_on(self):
        me = (self.x, self.y, self.c)
        passed = iter(self._passed())
        for m in (1, 2, 3):
            for u in range(len(self.pieces)):
                got = self._landed(u, m, self.c)
                _remote(got, got, self.send, self.recv, 3 * u + m - 1, me).wait_recv()
                next(passed).start()

    def finish(self):
        n = len(self.pieces)
        for m in (1, 2, 3):
            for u in range(n):
                got = self._landed(u, m, 1 - self.c)
                _remote(got, got, self.send, self.recv, 3 * n + 3 * u + m - 1, (self.x, self.y, self.c)).wait_recv()
        for cp in self.first + self._passed():
            cp.wait_send()

    @staticmethod
    def semaphores(pieces):
        return [pltpu.SemaphoreType.DMA((6 * len(pieces),)), pltpu.SemaphoreType.DMA((6 * len(pieces),))]


def _gather_first(fulls):
    n = len(fulls)

    def body(*refs):
        gather = _Gather(FIRST_PIECES, refs[n:2 * n], *refs[2 * n:])
        gather.start()
        gather.pass_on()
        gather.finish()

    return pl.pallas_call(
        body, name="gather_first",
        in_specs=[ANY] * n, out_specs=[ANY] * n,
        out_shape=[jax.ShapeDtypeStruct(f.shape, f.dtype) for f in fulls],
        input_output_aliases={a: a for a in range(n)},
        scratch_shapes=_Gather.semaphores(FIRST_PIECES),
    )(*fulls)


def _pair_sum(dw, other, pos, name):
    _, R, C = dw.shape
    tr = 128 if C > 1024 else 256

    def body(pos_ref, a_ref, b_ref, o_ref, ob_ref):
        tot = a_ref[...] + b_ref[...]
        o_ref[...] = tot
        ob_ref[...] = tot.astype(BF16)

    blk = pl.BlockSpec((tr, C), lambda i, pos: (i, 0))
    return pl.pallas_call(
        body, name=name,
        grid_spec=pltpu.PrefetchScalarGridSpec(
            num_scalar_prefetch=1, grid=(R // tr,),
            in_specs=[pl.BlockSpec((None, tr, C), lambda i, pos: (pos[0], i, 0)), blk],
            out_specs=[blk, blk]),
        out_shape=[jax.ShapeDtypeStruct((R, C), F32), jax.ShapeDtypeStruct((R, C), BF16)],
        compiler_params=_params(),
    )(pos, dw, other)


class _ChipExchange:
    def __init__(self, mine, theirs, send, recv):
        x, y, c, _ = _place()
        self.copies = []
        for a, (axis, width) in enumerate(SHARDED):
            for m in (1, 2, 3):
                px, py, sp = _other_chip(x, y, m)
                self.copies.append(_remote(_part(mine[a], sp, axis, width), theirs[a].at[m - 1], send, recv,
                                           3 * a + m - 1, (px, py, c)))

    def start(self):
        for cp in self.copies:
            cp.start()

    def wait(self):
        for cp in self.copies:
            cp.wait()

    @staticmethod
    def landing(ps):
        shapes = []
        for p, (axis, width) in zip(ps, SHARDED):
            shape = [3] + list(p.shape)
            shape[axis] = width
            shapes.append(jax.ShapeDtypeStruct(tuple(shape), p.dtype))
        return shapes

    @staticmethod
    def semaphores(n):
        return [pltpu.SemaphoreType.DMA((3 * n,)), pltpu.SemaphoreType.DMA((3 * n,))]


def _shard_sum(p, landed, pos, axis, width, name):
    _, Rs, Cs = landed.shape
    tr = min(256, Rs)
    p_spec = _part_spec(tr, Rs, Cs, axis, width, lambda a: 0)

    def body(pos_ref, p_ref, l_ref, o_ref):
        o_ref[...] = ((p_ref[...] + l_ref[0].astype(F32)) + l_ref[1].astype(F32)) + l_ref[2].astype(F32)

    return pl.pallas_call(
        body, name=name,
        grid_spec=pltpu.PrefetchScalarGridSpec(
            num_scalar_prefetch=1, grid=(Rs // tr,),
            in_specs=[p_spec, pl.BlockSpec((3, tr, Cs), lambda i, pos: (0, i, 0))],
            out_specs=pl.BlockSpec((None, tr, Cs), lambda i, pos: (pos[0], i, 0))),
        out_shape=jax.ShapeDtypeStruct((2, Rs, Cs), F32),
        compiler_params=_params(),
    )(pos, p[None], landed)


def _final_exchange(gs, packed):
    n = len(gs)
    rows = packed.shape[0]
    half = rows // 2
    assert half % 8 == 0

    def body(*refs):
        small_ref = refs[n]
        outs, total_ref = refs[n + 1:2 * n + 1], refs[2 * n + 1]
        sib_ref, chips_ref, done_ref, send, recv, small_send, small_recv = refs[2 * n + 2:]
        x, y, c, s = _place()
        me, sibling = (x, y, c), (x, y, 1 - c)
        copies = [_remote(outs[a].at[c], outs[a].at[c], send, recv, a, sibling) for a in range(n)]
        for cp in copies:
            cp.start()

        def small(src, dst, k, to):
            return _remote(src, dst, small_send, small_recv, k, to)

        mine = small_ref.at[pl.ds(pl.multiple_of(c * half, 8), half)]
        theirs = small_ref.at[pl.ds(pl.multiple_of((1 - c) * half, 8), half)]
        to_sibling = small(theirs, sib_ref, 0, sibling)
        to_sibling.start()
        to_sibling.wait()
        chips_ref[s] = mine[...] + sib_ref[...]
        to_chips = []
        for m in (1, 2, 3):
            px, py, _ = _other_chip(x, y, m)
            to_chips.append(small(chips_ref.at[s], chips_ref.at[s], m, (px, py, c)))
            to_chips[-1].start()
        for m in (1, 2, 3):
            _, _, sp = _other_chip(x, y, m)
            small(chips_ref.at[sp], chips_ref.at[sp], m, me).wait_recv()
        done_ref[c] = ((chips_ref[0] + chips_ref[1]) + chips_ref[2]) + chips_ref[3]
        finished = small(done_ref.at[c], done_ref.at[c], 4, sibling)
        finished.start()
        small(done_ref.at[1 - c], done_ref.at[1 - c], 4, me).wait_recv()
        total_ref[:half] = done_ref[0]
        total_ref[half:] = done_ref[1]
        for cp in to_chips + [finished]:
            cp.wait_send()
        for a, cp in enumerate(copies):
            cp.wait_send()
            _remote(outs[a].at[1 - c], outs[a].at[1 - c], send, recv, a, me).wait_recv()

    vmem = pl.BlockSpec(memory_space=pltpu.VMEM)
    res = pl.pallas_call(
        body, name="final_exchange",
        in_specs=[ANY] * n + [vmem], out_specs=[ANY] * n + [vmem],
        out_shape=[jax.ShapeDtypeStruct(g.shape, g.dtype) for g in gs]
        + [jax.ShapeDtypeStruct(packed.shape, packed.dtype)],
        input_output_aliases={a: a for a in range(n)},
        scratch_shapes=[pltpu.VMEM((half, LANES), F32), pltpu.VMEM((N_CHIPS, half, LANES), F32),
                        pltpu.VMEM((2, half, LANES), F32),
                        pltpu.SemaphoreType.DMA((n,)), pltpu.SemaphoreType.DMA((n,)),
                        pltpu.SemaphoreType.DMA((5,)), pltpu.SemaphoreType.DMA((5,))],
        compiler_params=_params(),
    )(*gs, packed)
    return res[:n], res[n]


def _adamw(w, g, m, v, name):
    shape = w.shape
    C = shape[-1]
    flat = [t.reshape(-1, C) for t in (w, g, m, v)]
    R = flat[0].shape[0]
    tr = max(t for t in range(8, R + 1, 8) if R % t == 0 and t * C <= 384 * 1024)

    def body(w_ref, g_ref, m_ref, v_ref, d_ref, nm_ref, nv_ref):
        _adamw_update(w_ref, g_ref, m_ref, v_ref, d_ref, nm_ref, nv_ref)

    blk = pl.BlockSpec((tr, C), lambda i: (i, 0))
    out = jax.ShapeDtypeStruct((R, C), F32)
    res = pl.pallas_call(
        body, name=name, grid=(R // tr,),
        in_specs=[blk] * 4, out_specs=[blk] * 3, out_shape=[out] * 3,
        compiler_params=_params(),
    )(*flat)
    return [t.reshape(shape) for t in res]


def _adamw_update(w_ref, g_ref, m_ref, v_ref, d_ref, nm_ref, nv_ref):
    gv = g_ref[...]
    nm = ADAM_B1 * m_ref[...] + (1.0 - ADAM_B1) * gv
    nv = ADAM_B2 * v_ref[...] + (1.0 - ADAM_B2) * (gv * gv)
    m_hat = nm / (1.0 - ADAM_B1 ** ADAM_STEP)
    v_hat = nv / (1.0 - ADAM_B2 ** ADAM_STEP)
    d_ref[...] = -ADAM_LR * (m_hat / (jnp.sqrt(v_hat) + ADAM_EPS) + ADAM_WD * w_ref[...])
    nm_ref[...] = nm
    nv_ref[...] = nv


def _adamw_small(ws, gs, ms, vs):
    n = len(ws)
    flat = lambda ts: [t.reshape(-1, t.shape[-1]) for t in ts]

    def body(*refs):
        for a in range(n):
            _adamw_update(*[refs[k * n + a] for k in range(7)])

    vmem = pl.BlockSpec(memory_space=pltpu.VMEM)
    shapes = [jax.ShapeDtypeStruct(w.shape, F32) for w in flat(ws)]
    res = pl.pallas_call(
        body, name="adamw_small",
        in_specs=[vmem] * (4 * n), out_specs=[vmem] * (3 * n), out_shape=shapes * 3,
        compiler_params=_params(),
    )(*flat(ws), *flat(gs), *flat(ms), *flat(vs))
    return [[r.reshape(w.shape) for r, w in zip(res[k * n:(k + 1) * n], ws)] for k in range(3)]


SMALL_SHAPES = ((2, 1024), (2, 2048), (2, 4, 128, 128), (2, 512), (1024,))


def _pack_small(parts):
    return jnp.concatenate([p.reshape(-1, LANES) for p in parts], axis=0)


def _unpack_small(packed):
    out, row = [], 0
    for shape in SMALL_SHAPES:
        n = 1
        for d in shape:
            n *= d
        out.append(packed[row:row + n // LANES].reshape(shape))
        row += n // LANES
    return out


def kernel(x, norm_g, w_in, b_gate, pool_w, pool_scale, w_pool_up, w_attn_up, w_out, final_g, loss_target, m_norm_g, m_w_in, m_b_gate, m_pool_w, m_pool_scale, m_w_pool_up, m_w_attn_up, m_w_out, m_final_g, v_norm_g, v_w_in, v_b_gate, v_pool_w, v_pool_scale, v_w_pool_up, v_w_attn_up, v_w_out, v_final_g):
    _, _, c, s = _place()
    pos = jnp.stack([c, s]).astype(jnp.int32)
    names = SHARDED_NAMES

    weights = _gather_first([_cast_into_place(w, pos, axis, width, f"cast_{n}")
                             for w, (axis, width), n in zip((w_in, w_pool_up, w_attn_up, w_out), SHARDED, names)])
    loss_part, dx, d_final_g, small, pair, landed = _local_step(x[0], loss_target[0], norm_g, b_gate, pool_w,
                                                                pool_scale, final_g, weights, pos)
    mine = [_shard_sum(p, l, pos, axis, width, f"grad_shard_sum_{n}")
            for p, l, (axis, width), n in zip(pair, landed, SHARDED, names)]
    (g_in, g_pu, g_au, g_out), summed = _final_exchange(
        mine, _pack_small(small + [d_final_g, jnp.broadcast_to(loss_part, (16, LANES))]))
    g_small = _unpack_small(summed)
    loss = summed[-16, 0]
    d_small, nm_small, nv_small = _adamw_small([norm_g, b_gate, pool_w, pool_scale, final_g], g_small,
                                               [m_norm_g, m_b_gate, m_pool_w, m_pool_scale, m_final_g],
                                               [v_norm_g, v_b_gate, v_pool_w, v_pool_scale, v_final_g])
    upd_in = _adamw(w_in, g_in, m_w_in, v_w_in, "adamw_w_in")
    upd_pu = _adamw(w_pool_up, g_pu, m_w_pool_up, v_w_pool_up, "adamw_w_pool_up")
    upd_au = _adamw(w_attn_up, g_au, m_w_attn_up, v_w_attn_up, "adamw_w_attn_up")
    upd_out = _adamw(w_out, g_out, m_w_out, v_w_out, "adamw_w_out")

    def ordered(sm, k):
        big = (upd_in[k], upd_pu[k], upd_au[k], upd_out[k]) if k is not None else (g_in, g_pu, g_au, g_out)
        return [sm[0], big[0], sm[1], sm[2], sm[3], big[1], big[2], big[3], sm[4]]

    return (loss, dx[None], *ordered(g_small, None), *ordered(d_small, 0), *ordered(nm_small, 1),
            *ordered(nv_small, 2))
```

```python
import jax
import jax.numpy as jnp
import numpy as np
from jax import lax
from jax.experimental import pallas as pl
from jax.experimental.pallas import tpu as pltpu

F32 = jnp.float32
BF16 = jnp.bfloat16
MESH = pl.DeviceIdType.MESH

D_MODEL = 1024
POOL_WIDTH = 512
POOL_WINDOWS = (2, 4, 8, 16)
POOL_GROUP = 128
POOL_HALO = 16
ATTN_WIDTH = 512
HEAD_DIM = 64
HEAD_PAIRS = 4
IN_WIDTH = 5120
N_CHIPS = 4
RMS_EPS = 1e-6
C_U, C_ZP, C_Q, C_K, C_V, C_ZA, C_GL = 0, 512, 1024, 1536, 2048, 2560, 3072

ADAM_LR, ADAM_B1, ADAM_B2, ADAM_EPS, ADAM_WD, ADAM_STEP = 0.001, 0.9, 0.999, 1e-08, 0.01, 10

LANES = 128
ATTN_BLOCK = 256
QUERY_BLOCKS = 4
ROW_TILE = 256
PROJ_ROW_TILE = 512
VMEM_LIMIT = 56 * 1024 * 1024


def _params(**kw):
    return pltpu.CompilerParams(vmem_limit_bytes=VMEM_LIMIT, **kw)


def _nt(a, b):
    return lax.dot_general(a, b, (((1,), (1,)), ((), ())), preferred_element_type=F32)


def _tn(a, b):
    return lax.dot_general(a, b, (((0,), (0,)), ((), ())), preferred_element_type=F32)


def _nn(a, b):
    return jnp.dot(a, b, preferred_element_type=F32)


def _sigmoid(z):
    return 1.0 / (1.0 + jnp.exp(-z))


def _rms_inproj(x, g, w_in, layer, gather=()):
    S = x.shape[0]
    tm = min(PROJ_ROW_TILE, S)
    n_tiles = S // tm
    n_g = len(gather)

    def body(*refs):
        x_ref, g_ref = refs[:2]
        if n_g:
            (u_ref, zp_ref, q_ref, k_ref, v_ref, za_ref, gl_ref, h_ref) = refs[2 + n_g:10 + n_g]
            fulls = refs[10 + n_g:10 + 2 * n_g]
            w_ref, load_sem, send, recv = refs[10 + 2 * n_g:]
            later = _Gather(LATER_PIECES, fulls, send, recv)

            @pl.when(pl.program_id(0) == 0)
            def _():
                load = pltpu.make_async_copy(fulls[0].at[layer], w_ref, load_sem)
                load.start()
                later.start()
                load.wait()

            pl.when(pl.program_id(0) == n_tiles - 1)(later.pass_on)
        else:
            w_ref, u_ref, zp_ref, q_ref, k_ref, v_ref, za_ref, gl_ref, h_ref = refs[2:]
        xv = x_ref[...]
        r = lax.rsqrt(jnp.mean(xv * xv, axis=-1, keepdims=True) + RMS_EPS)
        h = ((xv * r) * g_ref[...]).astype(BF16)
        h_ref[...] = h

        def mm(c0, n):
            return _nn(h, w_ref[:, c0:c0 + n])

        u_ref[...] = mm(C_U, 512)
        zp_ref[...] = mm(C_ZP, 512).astype(BF16)
        q_ref[...] = (mm(C_Q, 512) * 0.125).astype(BF16)
        k_ref[...] = mm(C_K, 512).astype(BF16)
        v_ref[...] = mm(C_V, 512).astype(BF16)
        za_ref[...] = mm(C_ZA, 512).astype(BF16)
        for c in range(4):
            gl_ref[:, c * 512:(c + 1) * 512] = mm(C_GL + c * 512, 512).astype(BF16)
        if n_g:
            pl.when(pl.program_id(0) == n_tiles - 1)(later.finish)

    row = lambda n: pl.BlockSpec((tm, n), lambda i: (i, 0))
    sd = lambda n, dt: jax.ShapeDtypeStruct((S, n), dt)
    any_space = pl.BlockSpec(memory_space=pl.ANY)
    weights = [any_space] * n_g if n_g else [_layer_weight_spec(D_MODEL, IN_WIDTH, layer)]
    res = pl.pallas_call(
        body, name=f"rms_inproj_l{layer}", grid=(n_tiles,),
        in_specs=[row(D_MODEL), pl.BlockSpec((1, D_MODEL), lambda i: (0, 0))] + weights,
        out_specs=[row(512), row(512), row(512), row(512), row(512), row(512), row(2048), row(D_MODEL)]
        + [any_space] * n_g,
        out_shape=[sd(512, F32), sd(512, BF16), sd(512, BF16), sd(512, BF16), sd(512, BF16), sd(512, BF16),
                   sd(2048, BF16), sd(D_MODEL, BF16)] + [jax.ShapeDtypeStruct(f.shape, f.dtype) for f in gather],
        input_output_aliases={2 + a: 8 + a for a in range(n_g)},
        scratch_shapes=([pltpu.VMEM((D_MODEL, IN_WIDTH), BF16), pltpu.SemaphoreType.DMA(())]
                        + _Gather.semaphores(LATER_PIECES)) if n_g else [],
        compiler_params=_params(),
    )(x, g, *(gather if n_g else (w_in,)))
    return res[:8], res[8:]


def _tri(n, strict_lower):
    r = lax.broadcasted_iota(jnp.int32, (n, n), 0)
    c = lax.broadcasted_iota(jnp.int32, (n, n), 1)
    return jnp.where(r > c if strict_lower else r < c, 1.0, 0.0).astype(BF16)


def _split_dot(x, m):
    hi = x.astype(BF16)
    lo = (x - hi.astype(F32)).astype(BF16)
    return _nn(hi, m) + _nn(lo, m)


def _log_terms(z):
    lg = jnp.log(1.0 + jnp.exp(-jnp.abs(z)))
    a = jnp.minimum(z, 0.0) - lg
    return a, a - z


EXHAUSTED = -104.0
UNREACHED = -1e30


class _HeadPair:
    def __init__(self, T):
        self.T = T
        self.first = lax.broadcasted_iota(jnp.int32, (T, LANES), 1) < HEAD_DIM
        self.lane = lax.broadcasted_iota(jnp.int32, (2 * T, LANES), 1)
        row = lax.broadcasted_iota(jnp.int32, (2 * T, T), 0)
        row = jnp.where(row >= T, row - T, row)
        self.causal = row > lax.broadcasted_iota(jnp.int32, (2 * T, T), 1)
        self.below = _tri(T, True)

    def stack(self, x2):
        return jnp.concatenate([jnp.where(self.first, x2, 0), jnp.where(self.first, 0, x2)], axis=0).astype(BF16)

    def unstack(self, x):
        return jnp.where(self.first, x[:self.T], x[self.T:])

    def keys(self, ref, blocks):
        T = self.T
        return jnp.concatenate([ref[pl.ds(pl.multiple_of(j * T, T), T), :] for j, _ in blocks], axis=0)

    def log_terms(self, z, blocks):
        T = self.T
        a_all, l_all = _log_terms(z)
        a = [a_all[:, b * T:(b + 1) * T] for b in range(len(blocks))]
        l1m = [l_all[:, b * T:(b + 1) * T] for b in range(len(blocks))]
        return a, [jnp.where(self.causal, l, 0.0) if diagonal else l for l, (_, diagonal) in zip(l1m, blocks)]

    def later_sums(self, l1m):
        later = _split_dot(jnp.concatenate(l1m, axis=0), self.below)
        return [later[2 * self.T * b:2 * self.T * (b + 1)] for b in range(len(l1m))]


def _sections(x, n):
    return x.reshape(n, x.shape[0] // n, x.shape[1])


def _attn_fwd(q, k, v, layer, gather=()):
    S = q.shape[0]
    T = min(ATTN_BLOCK, S)
    nq = S // T
    jobs = min(QUERY_BLOCKS, nq)
    assert nq <= LANES and nq % jobs == 0
    per_job = nq // jobs

    n_g = len(gather)

    def body(q_ref, k_ref, v_ref, *rest):
        o_ref, c_ref = rest[n_g:n_g + 2]
        i = pl.program_id(1)
        if n_g:
            last = _Gather(LAST_PIECES, rest[n_g + 2:2 * n_g + 2], *rest[2 * n_g + 2:])
            pl.when(jnp.logical_and(pl.program_id(0) == 0, i == 0))(last.start)
        pair = _HeadPair(T)
        qs = [pair.stack(q_ref[n]) for n in range(jobs)]
        diag = [i + n * per_job for n in range(jobs)]

        def sweep(jobs):
            kv = [(pair.keys(k_ref, bl), pair.keys(v_ref, bl)) for _, bl, _ in jobs]
            zs = [_nt(qs[n], kcat) for (n, _, _), (kcat, _) in zip(jobs, kv)]
            terms = [pair.log_terms(z, bl) for (_, bl, _), z in zip(jobs, zs)]
            laters = [pair.later_sums(l1m) for _, l1m in terms]
            weights = []
            for (_, bl, (acc, run, saved)), (a, l1m), later in zip(jobs, terms, laters):
                ws = []
                for b, (j, diagonal) in enumerate(bl):
                    saved = jnp.where(pair.lane == j, run, saved)
                    w = jnp.exp(a[b] + later[b] + run)
                    ws.append(jnp.where(pair.causal, w, 0.0) if diagonal else w)
                    run = run + jnp.sum(l1m[b], axis=1, keepdims=True)
                weights.append((jnp.concatenate(ws, axis=1).astype(BF16), acc, run, saved))
            return [(acc + _nn(w, vcat), run, saved) for (w, acc, run, saved), (_, vcat) in zip(weights, kv)]

        def alive(carry):
            return (jnp.max(carry[1]) > EXHAUSTED).astype(jnp.int32)

        def older_blocks(n, carry):
            def older_block(state):
                j, _, c = state
                c = sweep([(n, [(j, False)], c)])[0]
                return j - 1, alive(c), c

            return lax.while_loop(lambda st: jnp.logical_and(st[0] >= 0, st[1] > 0), older_block,
                                  (diag[n] - 2, alive(carry), carry))[2]

        def run(first_blocks):
            init = (jnp.zeros((2 * T, LANES), F32), jnp.zeros((2 * T, 1), F32),
                    jnp.full((2 * T, LANES), UNREACHED, F32))
            carries = sweep([(n, first_blocks[n], init) for n in range(jobs)])
            for n in range(jobs):
                acc, _, saved = older_blocks(n, carries[n])
                o_ref[n] = pair.unstack(acc).astype(BF16)
                c_ref[n, :, :LANES] = saved[:T]
                c_ref[n, :, LANES:] = saved[T:]

        with_previous = lambda d: [(d, True), (d - 1, False)]

        @pl.when(i == 0)
        def _():
            run([[(diag[0], True)]] + [with_previous(d) for d in diag[1:]])

        @pl.when(i > 0)
        def _():
            run([with_previous(d) for d in diag])

        if n_g:
            @pl.when(jnp.logical_and(pl.program_id(0) == HEAD_PAIRS - 1, i == per_job - 1))
            def _():
                last.pass_on()
                last.finish()

    blk = lambda n: pl.BlockSpec((jobs, T, n), lambda p, i: (0, i, p))
    full = pl.BlockSpec((S, LANES), lambda p, i: (0, p))
    any_space = pl.BlockSpec(memory_space=pl.ANY)
    res = pl.pallas_call(
        body, name=f"attn_fwd_l{layer}", grid=(HEAD_PAIRS, per_job),
        in_specs=[blk(LANES), full, full] + [any_space] * n_g,
        out_specs=[blk(LANES), blk(2 * LANES)] + [any_space] * n_g,
        out_shape=[jax.ShapeDtypeStruct((jobs, S // jobs, ATTN_WIDTH), BF16),
                   jax.ShapeDtypeStruct((jobs, S // jobs, 8 * LANES), F32)]
        + [jax.ShapeDtypeStruct(f.shape, f.dtype) for f in gather],
        input_output_aliases={3 + a: 2 + a for a in range(n_g)},
        scratch_shapes=_Gather.semaphores(LAST_PIECES) if n_g else [],
        compiler_params=_params(),
    )(_sections(q, jobs), k, v, *gather)
    return res[0].reshape(S, ATTN_WIDTH), res[1].reshape(S, 8 * LANES), res[2:]


def _attn_bwd(q, k, v, saved, do, layer):
    S = q.shape[0]
    T = min(ATTN_BLOCK, S)
    nq = S // T
    jobs = min(QUERY_BLOCKS, nq)
    per_job = nq // jobs

    def body(q_ref, k_ref, v_ref, c_ref, do_ref, dq_ref, dk_ref, dv_ref):
        i = pl.program_id(1)

        @pl.when(i == 0)
        def _():
            dk_ref[...] = jnp.zeros_like(dk_ref)
            dv_ref[...] = jnp.zeros_like(dv_ref)

        pair = _HeadPair(T)
        diag = [i + n * per_job for n in range(jobs)]
        qs = [pair.stack(q_ref[n]) for n in range(jobs)]
        dos = [pair.stack(do_ref[n].astype(BF16)) for n in range(jobs)]
        saved = [jnp.concatenate([c_ref[n, :, :LANES], c_ref[n, :, LANES:]], axis=0) for n in range(jobs)]
        before = _tri(T, False)

        def sweep(jobs):
            kv = [(pair.keys(k_ref, bl), pair.keys(v_ref, bl)) for _, bl, _ in jobs]
            zs = [_nt(qs[n], kcat) for (n, _, _), (kcat, _) in zip(jobs, kv)]
            gs = [_nt(dos[n], vcat) for (n, _, _), (_, vcat) in zip(jobs, kv)]
            terms = [pair.log_terms(z, bl) for (_, bl, _), z in zip(jobs, zs)]
            laters = [pair.later_sums(l1m) for _, l1m in terms]
            ws, es = [], []
            for (n, bl, _), (a, _), later, g in zip(jobs, terms, laters, gs):
                w_job, e_job = [], []
                for b, (j, diagonal) in enumerate(bl):
                    run = jnp.sum(jnp.where(pair.lane == j, saved[n], 0.0), axis=1, keepdims=True)
                    w = jnp.exp(a[b] + later[b] + run)
                    w_job.append(jnp.where(pair.causal, w, 0.0) if diagonal else w)
                    e_job.append(w_job[b] * g[:, b * T:(b + 1) * T])
                ws.append(w_job)
                es.append(e_job)
            prefixes = [_nn(jnp.concatenate(e_job, axis=0).astype(BF16), before) for e_job in es]
            dzs, olders = [], []
            for (_, bl, (_, older)), (a, _), e_job, prefix in zip(jobs, terms, es, prefixes):
                dz_job = []
                for b, (j, diagonal) in enumerate(bl):
                    dz = e_job[b] - jnp.exp(a[b]) * (e_job[b] + (prefix[2 * T * b:2 * T * (b + 1)] + older))
                    dz_job.append(jnp.where(pair.causal, dz, 0.0) if diagonal else dz)
                    older = older + jnp.sum(e_job[b], axis=1, keepdims=True)
                dzs.append(jnp.concatenate(dz_job, axis=1).astype(BF16))
                olders.append(older)
            out = []
            for (n, bl, (dq, _)), dz, w_job, older, (kcat, _) in zip(jobs, dzs, ws, olders, kv):
                dk = _tn(dz, qs[n])
                dv = _tn(jnp.concatenate(w_job, axis=1).astype(BF16), dos[n])
                for b, (j, _) in enumerate(bl):
                    rows = pl.ds(pl.multiple_of(j * T, T), T)
                    dk_ref[rows, :] += dk[b * T:(b + 1) * T]
                    dv_ref[rows, :] += dv[b * T:(b + 1) * T]
                out.append((dq + _nn(dz, kcat), older))
            return out

        def older_blocks(n):
            col_max = jnp.max(saved[n], axis=0, keepdims=True)
            lane_row = lax.broadcasted_iota(jnp.int32, (1, LANES), 1)
            reached = jnp.sum(jnp.where(jnp.logical_and(col_max > EXHAUSTED, lane_row < diag[n]), 1, 0))
            init = (jnp.zeros((2 * T, LANES), F32), jnp.zeros((2 * T, 1), F32))
            return lax.fori_loop(diag[n] - reached, diag[n] - 1, lambda j, c: sweep([(n, [(j, False)], c)])[0], init)

        def run(last_blocks):
            carries = sweep([(n, last_blocks[n], older_blocks(n)) for n in range(jobs)])
            for n in range(jobs):
                dq_ref[n] = (pair.unstack(carries[n][0]) * 0.125).astype(BF16)

        with_previous = lambda d: [(d - 1, False), (d, True)]

        @pl.when(i == 0)
        def _():
            run([[(diag[0], True)]] + [with_previous(d) for d in diag[1:]])

        @pl.when(i > 0)
        def _():
            run([with_previous(d) for d in diag])

    blk = lambda n: pl.BlockSpec((jobs, T, n), lambda p, i: (0, i, p))
    full = pl.BlockSpec((S, LANES), lambda p, i: (0, p))
    out = jax.ShapeDtypeStruct((S, ATTN_WIDTH), F32)
    dq, dk, dv = pl.pallas_call(
        body, name=f"attn_bwd_l{layer}", grid=(HEAD_PAIRS, per_job),
        in_specs=[blk(LANES), full, full, blk(2 * LANES), blk(LANES)],
        out_specs=[blk(LANES), full, full],
        out_shape=[jax.ShapeDtypeStruct((jobs, S // jobs, ATTN_WIDTH), BF16), out, out],
        compiler_params=_params(),
    )(_sections(q, jobs), k, v, _sections(saved, jobs), _sections(do, jobs))
    return dq.reshape(S, ATTN_WIDTH), dk, dv


def _pool_counts(row0, tm):
    pos = row0 + lax.broadcasted_iota(jnp.int32, (tm, 1), 0)
    return [1.0 / jnp.minimum(pos + 1, w).astype(F32) for w in POOL_WINDOWS]


def _window_bands(tm, backward):
    t = np.arange(tm)[:, None]
    c = np.arange(tm)[None, :]
    off = c - t if backward else t - c
    main = np.stack([(off >= 0) & (off < w) for w in POOL_WINDOWS])
    r = np.arange(POOL_HALO)[:, None]
    h = np.arange(POOL_HALO)[None, :]
    off = h - r + POOL_HALO if backward else r - h + POOL_HALO
    edge = np.concatenate([(off < w) for w in POOL_WINDOWS])
    return jnp.asarray(main, BF16), jnp.asarray(edge, BF16)


def _window_sums(tile, beside, main_ref, edge_ref, backward):
    tm = tile.shape[0]
    tb = tile.astype(BF16)
    edge = _nn(edge_ref[...], beside.astype(BF16))
    sums = []
    for g in range(len(POOL_WINDOWS)):
        cols = slice(g * POOL_GROUP, (g + 1) * POOL_GROUP)
        tot = _nn(main_ref[g], tb[:, cols])
        extra = edge[g * POOL_HALO:(g + 1) * POOL_HALO, cols]
        if backward:
            sums.append(jnp.concatenate([tot[:tm - POOL_HALO], tot[tm - POOL_HALO:] + extra], axis=0))
        else:
            sums.append(jnp.concatenate([tot[:POOL_HALO] + extra, tot[POOL_HALO:]], axis=0))
    return sums


def _post_forward(u, history, bands, inv_cnt, zp, o, za, gl, bg, pw_ref, scale, wpu_ref, wau_ref):
    pooled, mixed = [], []
    for g, tot in enumerate(_window_sums(u, history, *bands, False)):
        pg = (tot * inv_cnt[g] - u[:, g * POOL_GROUP:(g + 1) * POOL_GROUP]).astype(BF16)
        pooled.append(pg)
        mixed.append(_nn(pg, pw_ref[g].astype(BF16)))
    pooled = jnp.concatenate(pooled, axis=1)
    mixed = jnp.concatenate(mixed, axis=1)
    zp, za, o = zp.astype(F32), za.astype(F32), o.astype(F32)
    sp = _sigmoid(zp)
    sa = _sigmoid(za)
    y_pool = (mixed * scale) * (zp * sp)
    y_attn = o * (za * sa)
    gate = _sigmoid(gl + bg)
    g0, g1 = gate[:, :D_MODEL], gate[:, D_MODEL:]
    up_p = _nn(y_pool.astype(BF16), wpu_ref[...])
    up_a = _nn(y_attn.astype(BF16), wau_ref[...])
    merged = g0 * up_p + g1 * up_a
    return pooled, mixed, sp, sa, y_pool, y_attn, g0, g1, up_p, up_a, merged


def _row_specs(tm, rev, n_tiles):
    tile_of = (lambda i: n_tiles - 1 - i) if rev else (lambda i: i)
    row = lambda n: pl.BlockSpec((tm, n), lambda i: (tile_of(i), 0))
    halo = pl.BlockSpec((POOL_HALO, POOL_WIDTH),
                        lambda i: (jnp.maximum(tile_of(i) * (tm // POOL_HALO) - 1, 0), 0))
    const = lambda shape: pl.BlockSpec(shape, lambda i: (0,) * len(shape))
    return tile_of, row, halo, const


def _layer_weight_spec(rows, cols, layer):
    return pl.BlockSpec((None, rows, cols), lambda i: (layer, 0, 0), pipeline_mode=pl.Buffered(1))


def _post_fwd(x, u, zp, o, za, gl, bg, pw, scale, wpu, wau, wout, layer, head=()):
    S = x.shape[0]
    tm = min(ROW_TILE, S)
    n_tiles = S // tm
    tile_of, row, halo, const = _row_specs(tm, False, n_tiles)

    def body(x_ref, u_ref, uh_ref, main_ref, edge_ref, zp_ref, o_ref, za_ref, gl_ref, bg_ref, pw_ref, sc_ref, wpu_ref,
             wau_ref, wout_ref, *rest):
        i = pl.program_id(0)
        vals = _post_forward(u_ref[...], jnp.where(i == 0, 0.0, uh_ref[...]), (main_ref, edge_ref),
                             _pool_counts(i * tm, tm), zp_ref[...], o_ref[...], za_ref[...], gl_ref[...], bg_ref[...],
                             pw_ref, sc_ref[...], wpu_ref, wau_ref)
        xv = x_ref[...] + _nn(vals[-1].astype(BF16), wout_ref[...])
        if not head:
            rest[0][...] = xv
            return
        gf_ref, t_ref, loss_ref, dx_ref, dg_ref = rest

        @pl.when(i == 0)
        def _():
            loss_ref[...] = jnp.zeros_like(loss_ref)
            dg_ref[...] = jnp.zeros_like(dg_ref)

        r = lax.rsqrt(jnp.mean(xv * xv, axis=-1, keepdims=True) + RMS_EPS)
        diff = (xv * r) * gf_ref[...] - t_ref[...]
        per_row = jnp.mean(diff * diff, axis=-1, keepdims=True)
        loss_ref[...] += 0.5 * jnp.sum(per_row, axis=0, keepdims=True)
        dx, dg_rows = _rms_backward(diff * (1.0 / D_MODEL), xv, r, gf_ref[...])
        dx_ref[...] = dx
        dg_ref[...] += jnp.sum(dg_rows, axis=0, keepdims=True)

    out = jax.ShapeDtypeStruct((S, D_MODEL), F32)
    return pl.pallas_call(
        body, name=f"post_fwd_l{layer}", grid=(n_tiles,),
        in_specs=[row(D_MODEL), row(512), halo, const((4, tm, tm)), const((4 * POOL_HALO, POOL_HALO)), row(512),
                  row(512), row(512), row(2048), const((1, 2048)), const((4, POOL_GROUP, POOL_GROUP)),
                  const((1, POOL_WIDTH)),
                  _layer_weight_spec(POOL_WIDTH, D_MODEL, layer), _layer_weight_spec(ATTN_WIDTH, D_MODEL, layer),
                  _layer_weight_spec(D_MODEL, D_MODEL, layer)] + ([const((1, D_MODEL)), row(D_MODEL)] if head else []),
        out_specs=[const((1, LANES)), row(D_MODEL), const((1, D_MODEL))] if head else row(D_MODEL),
        out_shape=[jax.ShapeDtypeStruct((1, LANES), F32), out, jax.ShapeDtypeStruct((1, D_MODEL), F32)] if head else out,
        compiler_params=_params(),
    )(x, u, u, *_window_bands(tm, False), zp, o, za, gl, bg, pw, scale, wpu, wau, wout, *head)


def _post_bwd(dx, u, zp, o, za, gl, bg, pw, scale, wpu, wau, wout, layer):
    S = dx.shape[0]
    tm = min(ROW_TILE, S)
    n_tiles = S // tm
    tile_of, row, halo, const = _row_specs(tm, True, n_tiles)

    def body(dx_ref, u_ref, uh_ref, main_ref, edge_ref, back_main_ref, back_edge_ref, zp_ref, o_ref, za_ref, gl_ref,
             bg_ref, pw_ref, sc_ref, wpu_ref, wau_ref, wout_ref,
             duz_ref, do_ref, dzg_ref, dsc_ref, dbg_ref,
             merged_ref, dup_ref, dua_ref, yp_ref, ya_ref, pooled_ref, dmixed_ref, nxt_ref):
        step = pl.program_id(0)
        i = tile_of(step)

        @pl.when(step == 0)
        def _():
            dsc_ref[...] = jnp.zeros_like(dsc_ref)
            dbg_ref[...] = jnp.zeros_like(dbg_ref)
            nxt_ref[...] = jnp.zeros_like(nxt_ref)

        inv_cnt = _pool_counts(i * tm, tm)
        zp, za, o = zp_ref[...].astype(F32), za_ref[...].astype(F32), o_ref[...].astype(F32)
        pooled, mixed, sp, sa, y_pool, y_attn, g0, g1, up_p, up_a, merged = _post_forward(
            u_ref[...], jnp.where(i == 0, 0.0, uh_ref[...]), (main_ref, edge_ref), inv_cnt, zp, o, za, gl_ref[...],
            bg_ref[...], pw_ref, sc_ref[...], wpu_ref, wau_ref)
        merged_ref[...] = merged.astype(BF16)
        yp_ref[...] = y_pool.astype(BF16)
        ya_ref[...] = y_attn.astype(BF16)
        pooled_ref[...] = pooled

        dmerged = _nt(dx_ref[...].astype(BF16), wout_ref[...])
        dup = (dmerged * g0).astype(BF16)
        dua = (dmerged * g1).astype(BF16)
        dup_ref[...] = dup
        dua_ref[...] = dua
        dgl0 = (dmerged * up_p) * (g0 * (1.0 - g0))
        dgl1 = (dmerged * up_a) * (g1 * (1.0 - g1))
        dzg_ref[:, ATTN_WIDTH:ATTN_WIDTH + D_MODEL] = dgl0.astype(BF16)
        dzg_ref[:, ATTN_WIDTH + D_MODEL:] = dgl1.astype(BF16)
        dbg_ref[:, :D_MODEL] += jnp.sum(dgl0, axis=0, keepdims=True)
        dbg_ref[:, D_MODEL:] += jnp.sum(dgl1, axis=0, keepdims=True)

        dy_attn = _nt(dua, wau_ref[...])
        do_ref[...] = (dy_attn * (za * sa)).astype(BF16)
        dzg_ref[:, :ATTN_WIDTH] = ((dy_attn * o) * (sa * (1.0 + za * (1.0 - sa)))).astype(BF16)

        dy_pool = _nt(dup, wpu_ref[...])
        ms = mixed * sc_ref[...]
        dms = dy_pool * (zp * sp)
        duz_ref[:, POOL_WIDTH:] = ((dy_pool * ms) * (sp * (1.0 + zp * (1.0 - sp)))).astype(BF16)
        dsc_ref[...] += jnp.sum(dms * mixed, axis=0, keepdims=True)
        dmixed = (dms * sc_ref[...]).astype(BF16)
        dmixed_ref[...] = dmixed
        dpooled = [_nt(dmixed[:, g * POOL_GROUP:(g + 1) * POOL_GROUP], pw_ref[g].astype(BF16)) for g in range(4)]
        scaled = jnp.concatenate([d * inv for d, inv in zip(dpooled, inv_cnt)], axis=1)
        for g, tot in enumerate(_window_sums(scaled, nxt_ref[...], back_main_ref, back_edge_ref, True)):
            duz_ref[:, g * POOL_GROUP:(g + 1) * POOL_GROUP] = (tot - dpooled[g]).astype(BF16)
        nxt_ref[...] = scaled[:POOL_HALO]

    sd = lambda n, dt: jax.ShapeDtypeStruct((S, n), dt)
    bands = [const((4, tm, tm)), const((4 * POOL_HALO, POOL_HALO))]
    return pl.pallas_call(
        body, name=f"post_bwd_l{layer}", grid=(n_tiles,),
        in_specs=[row(D_MODEL), row(512), halo, *bands, *bands, row(512), row(512), row(512), row(2048),
                  const((1, 2048)), const((4, POOL_GROUP, POOL_GROUP)), const((1, POOL_WIDTH)),
                  _layer_weight_spec(POOL_WIDTH, D_MODEL, layer), _layer_weight_spec(ATTN_WIDTH, D_MODEL, layer),
                  _layer_weight_spec(D_MODEL, D_MODEL, layer)],
        out_specs=[row(1024), row(512), row(2560), const((1, POOL_WIDTH)), const((1, 2048)),
                   row(D_MODEL), row(D_MODEL), row(D_MODEL), row(512), row(512), row(512), row(512)],
        out_shape=[sd(1024, BF16), sd(512, BF16), sd(2560, BF16),
                   jax.ShapeDtypeStruct((1, POOL_WIDTH), F32), jax.ShapeDtypeStruct((1, 2048), F32),
                   sd(D_MODEL, BF16), sd(D_MODEL, BF16), sd(D_MODEL, BF16), sd(512, BF16), sd(512, BF16),
                   sd(512, BF16), sd(512, BF16)],
        scratch_shapes=[pltpu.VMEM((POOL_HALO, POOL_WIDTH), F32)],
        compiler_params=_params(),
    )(dx, u, u, *_window_bands(tm, False), *_window_bands(tm, True), zp, o, za, gl, bg, pw, scale, wpu, wau, wout)


def _rms_backward(dh, xv, r, g):
    xhat = xv * r
    dxhat = dh * g
    return r * (dxhat - xhat * jnp.mean(dxhat * xhat, axis=-1, keepdims=True)), dh * xhat


def _inproj_bwd(pieces, w_in, x, g, dx_res, layer, exchange=()):
    S = x.shape[0]
    tm = min(PROJ_ROW_TILE, S)
    cols = [(c0, p.shape[1]) for p, c0 in pieces]

    def body(ins, outs):
        piece_refs = ins[:len(cols)]
        w_ref, x_ref, g_ref, res_ref = ins[len(cols):]
        dx_ref, dg_ref = outs

        @pl.when(pl.program_id(0) == 0)
        def _():
            dg_ref[...] = jnp.zeros_like(dg_ref)

        dh = jnp.zeros((tm, D_MODEL), F32)
        for p_ref, (c0, n) in zip(piece_refs, cols):
            for c in range(0, n, 512):
                dh = dh + _nt(p_ref[:, c:c + 512].astype(BF16), w_ref[:, c0 + c:c0 + c + 512])
        xv = x_ref[...]
        r = lax.rsqrt(jnp.mean(xv * xv, axis=-1, keepdims=True) + RMS_EPS)
        dx, dg_rows = _rms_backward(dh, xv, r, g_ref[...])
        dx_ref[...] = res_ref[...] + dx
        dg_ref[...] += jnp.sum(dg_rows, axis=0, keepdims=True)

    row = lambda n: pl.BlockSpec((tm, n), lambda i: (i, 0))
    vec = pl.BlockSpec((1, D_MODEL), lambda i: (0, 0))
    any_space = pl.BlockSpec(memory_space=pl.ANY)
    n_x = len(exchange)
    grid = (S // tm,)
    res = pl.pallas_call(
        _with_swap(body, grid, len(cols) + 4, 2, n_x, rider=_ChipExchange), name=f"inproj_bwd_l{layer}", grid=grid,
        in_specs=[row(n) for _, n in cols] + [_layer_weight_spec(D_MODEL, IN_WIDTH, layer), row(D_MODEL), vec,
                                              row(D_MODEL)] + [any_space] * n_x,
        out_specs=[row(D_MODEL), vec] + [any_space] * n_x,
        out_shape=[jax.ShapeDtypeStruct((S, D_MODEL), F32), jax.ShapeDtypeStruct((1, D_MODEL), F32)]
        + _ChipExchange.landing(exchange),
        scratch_shapes=_ChipExchange.semaphores(n_x) if n_x else [],
        compiler_params=_params(),
    )(*[p for p, _ in pieces], w_in, x, g, dx_res, *exchange)
    return res[0], res[1], res[2:]


class _SiblingSwap:
    def __init__(self, mine, theirs, send, recv):
        x, y, c, _ = _place()
        self.copies = [_remote(m.at[1 - c], t, send, recv, a, (x, y, 1 - c))
                       for a, (m, t) in enumerate(zip(mine, theirs))]

    def start(self):
        for cp in self.copies:
            cp.start()

    def wait(self):
        for cp in self.copies:
            cp.wait()


def _with_swap(body, grid, n_in, n_out, n_swap, rider=_SiblingSwap):
    if not n_swap:
        return lambda *refs: body(refs[:n_in], refs[n_in:])

    def riding(*refs):
        ins, mine = refs[:n_in], refs[n_in:n_in + n_swap]
        outs, theirs = refs[n_in + n_swap:n_in + n_swap + n_out], refs[n_in + n_swap + n_out:n_in + 2 * n_swap + n_out]
        swap = rider(mine, theirs, *refs[n_in + 2 * n_swap + n_out:])
        step = [pl.program_id(d) for d in range(len(grid))]
        first, last = step[0] == 0, step[0] == grid[0] - 1
        for d in range(1, len(grid)):
            first, last = jnp.logical_and(first, step[d] == 0), jnp.logical_and(last, step[d] == grid[d] - 1)
        pl.when(first)(swap.start)
        body(ins, outs)
        pl.when(last)(swap.wait)

    return riding


def _swap_specs(swap):
    any_space = pl.BlockSpec(memory_space=pl.ANY)
    shapes = [jax.ShapeDtypeStruct(d.shape[1:], d.dtype) for d in swap]
    sems = [pltpu.SemaphoreType.DMA((len(swap),)), pltpu.SemaphoreType.DMA((len(swap),))] if swap else []
    return [any_space] * len(swap), shapes, sems


def _wgrad(a, b, name, layer, into=None, col0=0, n_total=None, swap=()):
    S, M = a.shape
    N = b.shape[1]
    n_total = N if n_total is None else n_total
    tk = min(2048, S)
    tn = max(t for t in range(LANES, min(N, 1280) + 1, LANES) if N % t == 0 and col0 % t == 0)
    grid = (N // tn, S // tk)

    def body(ins, outs):
        prod = _tn(ins[0][...].astype(BF16), ins[1][...].astype(BF16))

        @pl.when(pl.program_id(1) == 0)
        def _():
            outs[0][...] = prod

        @pl.when(pl.program_id(1) > 0)
        def _():
            outs[0][...] += prod

    in_specs = [pl.BlockSpec((tk, M), lambda j, k: (k, 0)), pl.BlockSpec((tk, tn), lambda j, k: (k, j))]
    args = [a, b]
    aliases = {}
    if into is not None:
        in_specs.append(pl.BlockSpec(memory_space=pl.ANY))
        args.append(into)
        aliases = {2: 0}
    swap_specs, swap_shapes, swap_sems = _swap_specs(swap)
    res = pl.pallas_call(
        _with_swap(body, grid, len(args), 1, len(swap)), name=name, grid=grid,
        in_specs=in_specs + swap_specs,
        out_specs=[pl.BlockSpec((None, M, tn), lambda j, k: (layer, 0, col0 // tn + j))] + swap_specs,
        out_shape=[jax.ShapeDtypeStruct((2, M, n_total), F32)] + swap_shapes,
        input_output_aliases=aliases,
        scratch_shapes=swap_sems,
        compiler_params=_params(),
    )(*args, *swap)
    return (res[0], res[1:]) if swap else res[0]


def _pool_wgrad(pooled, dmixed, layer, swap=()):
    S = pooled.shape[0]
    tk = min(8192, S)
    grid = (4, S // tk)

    def body(ins, outs):
        prod = _tn(ins[0][...], ins[1][...])

        @pl.when(pl.program_id(1) == 0)
        def _():
            outs[0][...] = prod

        @pl.when(pl.program_id(1) > 0)
        def _():
            outs[0][...] += prod

    blk = pl.BlockSpec((tk, POOL_GROUP), lambda g, k: (k, g))
    swap_specs, swap_shapes, swap_sems = _swap_specs(swap)
    res = pl.pallas_call(
        _with_swap(body, grid, 2, 1, len(swap)), name=f"pool_wgrad_l{layer}", grid=grid,
        in_specs=[blk, blk] + swap_specs,
        out_specs=[pl.BlockSpec((None, POOL_GROUP, POOL_GROUP), lambda g, k: (g, 0, 0))] + swap_specs,
        out_shape=[jax.ShapeDtypeStruct((4, POOL_GROUP, POOL_GROUP), F32)] + swap_shapes,
        scratch_shapes=swap_sems,
        compiler_params=_params(),
    )(pooled, dmixed, *swap)
    return (res[0], res[1:]) if swap else res[0]


def _local_step(x, target, norm_g, b_gate, pool_w, pool_scale, final_g, weights, pos):
    n_layers = norm_g.shape[0]
    saved = []
    for l in range(n_layers):
        g = norm_g[l][None]
        bg = b_gate[l][None]
        sc = pool_scale[l][None]
        if l == 0:
            (u, zp, q, k, v, za, gl, h), (w_in, w_pu, w_au, w_out) = _rms_inproj(x, g, None, l, gather=weights)
        else:
            (u, zp, q, k, v, za, gl, h), _ = _rms_inproj(x, g, w_in, l)
        if l == 0:
            o, carry, (w_in,) = _attn_fwd(q, k, v, l, gather=(w_in,))
        else:
            o, carry, _ = _attn_fwd(q, k, v, l)
        saved.append((x, g, bg, sc, u, zp, q, k, v, za, gl, h, o, carry))
        if l < n_layers - 1:
            x = _post_fwd(x, u, zp, o, za, gl, bg, pool_w[l], sc, w_pu, w_au, w_out, l)
        else:
            loss, dx, d_final_g = _post_fwd(x, u, zp, o, za, gl, bg, pool_w[l], sc, w_pu, w_au, w_out, l,
                                            head=(final_g[None], target))

    small = [None] * n_layers
    dw_in = dw_out = dw_pu = dw_au = None
    for l in reversed(range(n_layers)):
        x_in, g, bg, sc, u, zp, q, k, v, za, gl, h, o, carry = saved[l]
        (duz, do, dzg, dsc, dbg, merged, dup, dua, y_pool, y_attn, pooled, dmixed) = _post_bwd(
            dx, u, zp, o, za, gl, bg, pool_w[l], sc, w_pu, w_au, w_out, l)
        dq, dk, dv = _attn_bwd(q, k, v, carry, do, l)
        pieces = [(duz, C_U), (dq, C_Q), (dk, C_K), (dv, C_V), (dzg, C_ZA)]
        for p, c0 in pieces:
            dw_in = _wgrad(h, p, f"wgrad_in_l{l}_c{c0}", l, into=dw_in, col0=c0, n_total=IN_WIDTH)
        if l > 0:
            dw_out = _wgrad(merged, dx, f"wgrad_out_l{l}", l, into=dw_out)
        else:
            dw_out, (other_in,) = _wgrad(merged, dx, f"wgrad_out_l{l}", l, into=dw_out, swap=(dw_in,))
        dw_pu = _wgrad(y_pool, dup, f"wgrad_pu_l{l}", l, into=dw_pu)
        dw_au = _wgrad(y_attn, dua, f"wgrad_au_l{l}", l, into=dw_au)
        if l > 0:
            dpw = _pool_wgrad(pooled, dmixed, l)
        else:
            dpw, (other_pu, other_au, other_out) = _pool_wgrad(pooled, dmixed, l, swap=(dw_pu, dw_au, dw_out))
        if l > 0:
            dx, dg, _ = _inproj_bwd(pieces, w_in, x_in, g, dx, l)
        else:
            pair = [_pair_sum(d, t, pos, f"grad_pair_sum_{n}") for d, t, n in
                    zip((dw_in, dw_pu, dw_au, dw_out), (other_in, other_pu, other_au, other_out), SHARDED_NAMES)]
            dx, dg, landed = _inproj_bwd(pieces, w_in, x_in, g, dx, l, exchange=[pb for _, pb in pair])
        small[l] = (dg[0], dbg[0], dpw, dsc[0])
    small = [jnp.stack([small[l][i] for l in range(n_layers)]) for i in range(4)]
    return loss[0, 0], dx, d_final_g[0], small, [p for p, _ in pair], landed


SHARDED = ((2, 1280), (2, 256), (2, 256), (1, 256))
SHARDED_NAMES = ("w_in", "w_pool_up", "w_attn_up", "w_out")
ANY = pl.BlockSpec(memory_space=pl.ANY)


def _part(ref, s, axis, width):
    sl = pl.ds(pl.multiple_of(s * width, width), width)
    return ref.at[:, sl] if axis == 2 else ref.at[sl, :]


def _place():
    x, y, c = lax.axis_index("x"), lax.axis_index("y"), lax.axis_index("c")
    return x, y, c, 2 * x + y


def _other_chip(x, y, m):
    px = 1 - x if m & 2 else x
    py = 1 - y if m & 1 else y
    return px, py, 2 * px + py


def _remote(src, dst, send, recv, k, to):
    return pltpu.make_async_remote_copy(src_ref=src, dst_ref=dst, send_sem=send.at[k], recv_sem=recv.at[k],
                                        device_id=to, device_id_type=MESH)


def _part_spec(tr, rows_s, cols_s, axis, width, lead):
    if axis == 2:
        return pl.BlockSpec((None, tr, width), lambda *a: (lead(a), a[-2], a[-1][1]))
    return pl.BlockSpec((None, tr, cols_s), lambda *a: (lead(a), a[-1][1] * (rows_s // tr) + a[-2], 0))


def _cast_into_place(w, pos, axis, width, name):
    L, Rs, Cs = w.shape
    tr = min(256, Rs)
    shape = [L, Rs, Cs]
    shape[axis] *= N_CHIPS

    def body(pos_ref, w_ref, o_ref):
        o_ref[...] = w_ref[...].astype(BF16)

    return pl.pallas_call(
        body, name=name,
        grid_spec=pltpu.PrefetchScalarGridSpec(
            num_scalar_prefetch=1, grid=(L, Rs // tr),
            in_specs=[pl.BlockSpec((None, tr, Cs), lambda l, i, pos: (l, i, 0))],
            out_specs=_part_spec(tr, Rs, Cs, axis, width, lambda a: a[0])),
        out_shape=jax.ShapeDtypeStruct(tuple(shape), BF16),
        compiler_params=_params(),
    )(pos, w)


def _w_in_half(layer):
    def piece(refs, who, shard):
        rows = pl.ds(pl.multiple_of(who * (D_MODEL // 2), D_MODEL // 2), D_MODEL // 2)
        return refs[0].at[layer, rows, pl.ds(pl.multiple_of(shard * SHARDED[0][1], SHARDED[0][1]), SHARDED[0][1])]
    return piece


def _whole_layer(a):
    def piece(refs, who, shard):
        return _part(refs[a].at[who], shard, *SHARDED[a])
    return piece


FIRST_PIECES = (_w_in_half(0),)
LATER_PIECES = (_whole_layer(1), _whole_layer(2), _whole_layer(3))
LAST_PIECES = (_w_in_half(1),)


class _Gather:
    def __init__(self, pieces, refs, send, recv):
        self.pieces, self.refs, self.send, self.recv = pieces, refs, send, recv
        self.x, self.y, self.c, s = _place()
        self.first = []
        for u, piece in enumerate(pieces):
            for m in (1, 2, 3):
                px, py, _ = _other_chip(self.x, self.y, m)
                own = piece(refs, self.c, s)
                self.first.append(_remote(own, own, send, recv, 3 * u + m - 1, (px, py, self.c)))

    def start(self):
        for cp in self.first:
            cp.start()

    def _landed(self, u, m, who):
        _, _, sp = _other_chip(self.x, self.y, m)
        return self.pieces[u](self.refs, who, sp)

    def _passed(self):
        n = len(self.pieces)
        return [_remote(self._landed(u, m, self.c), self._landed(u, m, self.c), self.send, self.recv,
                        3 * n + 3 * u + m - 1, (self.x, self.y, 1 - self.c)) for m in (1, 2, 3) for u in range(n)]

    def pass_on(self):
        me = (self.x, self.y, self.c)
        passed = iter(self._passed())
        for m in (1, 2, 3):
            for u in range(len(self.pieces)):
                got = self._landed(u, m, self.c)
                _remote(got, got, self.send, self.recv, 3 * u + m - 1, me).wait_recv()
                next(passed).start()

    def finish(self):
        n = len(self.pieces)
        for m in (1, 2, 3):
            for u in range(n):
                got = self._landed(u, m, 1 - self.c)
                _remote(got, got, self.send, self.recv, 3 * n + 3 * u + m - 1, (self.x, self.y, self.c)).wait_recv()
        for cp in self.first + self._passed():
            cp.wait_send()

    @staticmethod
    def semaphores(pieces):
        return [pltpu.SemaphoreType.DMA((6 * len(pieces),)), pltpu.SemaphoreType.DMA((6 * len(pieces),))]


def _gather_first(fulls):
    n = len(fulls)

    def body(*refs):
        gather = _Gather(FIRST_PIECES, refs[n:2 * n], *refs[2 * n:])
        gather.start()
        gather.pass_on()
        gather.finish()

    return pl.pallas_call(
        body, name="gather_first",
        in_specs=[ANY] * n, out_specs=[ANY] * n,
        out_shape=[jax.ShapeDtypeStruct(f.shape, f.dtype) for f in fulls],
        input_output_aliases={a: a for a in range(n)},
        scratch_shapes=_Gather.semaphores(FIRST_PIECES),
    )(*fulls)


def _pair_sum(dw, other, pos, name):
    _, R, C = dw.shape
    tr = 128 if C > 1024 else 256

    def body(pos_ref, a_ref, b_ref, o_ref, ob_ref):
        tot = a_ref[...] + b_ref[...]
        o_ref[...] = tot
        ob_ref[...] = tot.astype(BF16)

    blk = pl.BlockSpec((tr, C), lambda i, pos: (i, 0))
    return pl.pallas_call(
        body, name=name,
        grid_spec=pltpu.PrefetchScalarGridSpec(
            num_scalar_prefetch=1, grid=(R // tr,),
            in_specs=[pl.BlockSpec((None, tr, C), lambda i, pos: (pos[0], i, 0)), blk],
            out_specs=[blk, blk]),
        out_shape=[jax.ShapeDtypeStruct((R, C), F32), jax.ShapeDtypeStruct((R, C), BF16)],
        compiler_params=_params(),
    )(pos, dw, other)


class _ChipExchange:
    def __init__(self, mine, theirs, send, recv):
        x, y, c, _ = _place()
        self.copies = []
        for a, (axis, width) in enumerate(SHARDED):
            for m in (1, 2, 3):
                px, py, sp = _other_chip(x, y, m)
                self.copies.append(_remote(_part(mine[a], sp, axis, width), theirs[a].at[m - 1], send, recv,
                                           3 * a + m - 1, (px, py, c)))

    def start(self):
        for cp in self.copies:
            cp.start()

    def wait(self):
        for cp in self.copies:
            cp.wait()

    @staticmethod
    def landing(ps):
        shapes = []
        for p, (axis, width) in zip(ps, SHARDED):
            shape = [3] + list(p.shape)
            shape[axis] = width
            shapes.append(jax.ShapeDtypeStruct(tuple(shape), p.dtype))
        return shapes

    @staticmethod
    def semaphores(n):
        return [pltpu.SemaphoreType.DMA((3 * n,)), pltpu.SemaphoreType.DMA((3 * n,))]


def _shard_sum(p, landed, pos, axis, width, name):
    _, Rs, Cs = landed.shape
    tr = min(256, Rs)
    p_spec = _part_spec(tr, Rs, Cs, axis, width, lambda a: 0)

    def body(pos_ref, p_ref, l_ref, o_ref):
        o_ref[...] = ((p_ref[...] + l_ref[0].astype(F32)) + l_ref[1].astype(F32)) + l_ref[2].astype(F32)

    return pl.pallas_call(
        body, name=name,
        grid_spec=pltpu.PrefetchScalarGridSpec(
            num_scalar_prefetch=1, grid=(Rs // tr,),
            in_specs=[p_spec, pl.BlockSpec((3, tr, Cs), lambda i, pos: (0, i, 0))],
            out_specs=pl.BlockSpec((None, tr, Cs), lambda i, pos: (pos[0], i, 0))),
        out_shape=jax.ShapeDtypeStruct((2, Rs, Cs), F32),
        compiler_params=_params(),
    )(pos, p[None], landed)


def _final_exchange(gs, packed):
    n = len(gs)
    rows = packed.shape[0]
    half = rows // 2
    assert half % 8 == 0

    def body(*refs):
        small_ref = refs[n]
        outs, total_ref = refs[n + 1:2 * n + 1], refs[2 * n + 1]
        sib_ref, chips_ref, done_ref, send, recv, small_send, small_recv = refs[2 * n + 2:]
        x, y, c, s = _place()
        me, sibling = (x, y, c), (x, y, 1 - c)
        copies = [_remote(outs[a].at[c], outs[a].at[c], send, recv, a, sibling) for a in range(n)]
        for cp in copies:
            cp.start()

        def small(src, dst, k, to):
            return _remote(src, dst, small_send, small_recv, k, to)

        mine = small_ref.at[pl.ds(pl.multiple_of(c * half, 8), half)]
        theirs = small_ref.at[pl.ds(pl.multiple_of((1 - c) * half, 8), half)]
        to_sibling = small(theirs, sib_ref, 0, sibling)
        to_sibling.start()
        to_sibling.wait()
        chips_ref[s] = mine[...] + sib_ref[...]
        to_chips = []
        for m in (1, 2, 3):
            px, py, _ = _other_chip(x, y, m)
            to_chips.append(small(chips_ref.at[s], chips_ref.at[s], m, (px, py, c)))
            to_chips[-1].start()
        for m in (1, 2, 3):
            _, _, sp = _other_chip(x, y, m)
            small(chips_ref.at[sp], chips_ref.at[sp], m, me).wait_recv()
        done_ref[c] = ((chips_ref[0] + chips_ref[1]) + chips_ref[2]) + chips_ref[3]
        finished = small(done_ref.at[c], done_ref.at[c], 4, sibling)
        finished.start()
        small(done_ref.at[1 - c], done_ref.at[1 - c], 4, me).wait_recv()
        total_ref[:half] = done_ref[0]
        total_ref[half:] = done_ref[1]
        for cp in to_chips + [finished]:
            cp.wait_send()
        for a, cp in enumerate(copies):
            cp.wait_send()
            _remote(outs[a].at[1 - c], outs[a].at[1 - c], send, recv, a, me).wait_recv()

    vmem = pl.BlockSpec(memory_space=pltpu.VMEM)
    res = pl.pallas_call(
        body, name="final_exchange",
        in_specs=[ANY] * n + [vmem], out_specs=[ANY] * n + [vmem],
        out_shape=[jax.ShapeDtypeStruct(g.shape, g.dtype) for g in gs]
        + [jax.ShapeDtypeStruct(packed.shape, packed.dtype)],
        input_output_aliases={a: a for a in range(n)},
        scratch_shapes=[pltpu.VMEM((half, LANES), F32), pltpu.VMEM((N_CHIPS, half, LANES), F32),
                        pltpu.VMEM((2, half, LANES), F32),
                        pltpu.SemaphoreType.DMA((n,)), pltpu.SemaphoreType.DMA((n,)),
                        pltpu.SemaphoreType.DMA((5,)), pltpu.SemaphoreType.DMA((5,))],
        compiler_params=_params(),
    )(*gs, packed)
    return res[:n], res[n]


def _adamw(w, g, m, v, name):
    shape = w.shape
    C = shape[-1]
    flat = [t.reshape(-1, C) for t in (w, g, m, v)]
    R = flat[0].shape[0]
    tr = max(t for t in range(8, R + 1, 8) if R % t == 0 and t * C <= 384 * 1024)

    def body(w_ref, g_ref, m_ref, v_ref, d_ref, nm_ref, nv_ref):
        _adamw_update(w_ref, g_ref, m_ref, v_ref, d_ref, nm_ref, nv_ref)

    blk = pl.BlockSpec((tr, C), lambda i: (i, 0))
    out = jax.ShapeDtypeStruct((R, C), F32)
    res = pl.pallas_call(
        body, name=name, grid=(R // tr,),
        in_specs=[blk] * 4, out_specs=[blk] * 3, out_shape=[out] * 3,
        compiler_params=_params(),
    )(*flat)
    return [t.reshape(shape) for t in res]


def _adamw_update(w_ref, g_ref, m_ref, v_ref, d_ref, nm_ref, nv_ref):
    gv = g_ref[...]
    nm = ADAM_B1 * m_ref[...] + (1.0 - ADAM_B1) * gv
    nv = ADAM_B2 * v_ref[...] + (1.0 - ADAM_B2) * (gv * gv)
    m_hat = nm / (1.0 - ADAM_B1 ** ADAM_STEP)
    v_hat = nv / (1.0 - ADAM_B2 ** ADAM_STEP)
    d_ref[...] = -ADAM_LR * (m_hat / (jnp.sqrt(v_hat) + ADAM_EPS) + ADAM_WD * w_ref[...])
    nm_ref[...] = nm
    nv_ref[...] = nv


def _adamw_small(ws, gs, ms, vs):
    n = len(ws)
    flat = lambda ts: [t.reshape(-1, t.shape[-1]) for t in ts]

    def body(*refs):
        for a in range(n):
            _adamw_update(*[refs[k * n + a] for k in range(7)])

    vmem = pl.BlockSpec(memory_space=pltpu.VMEM)
    shapes = [jax.ShapeDtypeStruct(w.shape, F32) for w in flat(ws)]
    res = pl.pallas_call(
        body, name="adamw_small",
        in_specs=[vmem] * (4 * n), out_specs=[vmem] * (3 * n), out_shape=shapes * 3,
        compiler_params=_params(),
    )(*flat(ws), *flat(gs), *flat(ms), *flat(vs))
    return [[r.reshape(w.shape) for r, w in zip(res[k * n:(k + 1) * n], ws)] for k in range(3)]


SMALL_SHAPES = ((2, 1024), (2, 2048), (2, 4, 128, 128), (2, 512), (1024,))


def _pack_small(parts):
    return jnp.concatenate([p.reshape(-1, LANES) for p in parts], axis=0)


def _unpack_small(packed):
    out, row = [], 0
    for shape in SMALL_SHAPES:
        n = 1
        for d in shape:
            n *= d
        out.append(packed[row:row + n // LANES].reshape(shape))
        row += n // LANES
    return out


def kernel(x, norm_g, w_in, b_gate, pool_w, pool_scale, w_pool_up, w_attn_up, w_out, final_g, loss_target, m_norm_g, m_w_in, m_b_gate, m_pool_w, m_pool_scale, m_w_pool_up, m_w_attn_up, m_w_out, m_final_g, v_norm_g, v_w_in, v_b_gate, v_pool_w, v_pool_scale, v_w_pool_up, v_w_attn_up, v_w_out, v_final_g):
    _, _, c, s = _place()
    pos = jnp.stack([c, s]).astype(jnp.int32)
    names = SHARDED_NAMES

    weights = _gather_first([_cast_into_place(w, pos, axis, width, f"cast_{n}")
                             for w, (axis, width), n in zip((w_in, w_pool_up, w_attn_up, w_out), SHARDED, names)])
    loss_part, dx, d_final_g, small, pair, landed = _local_step(x[0], loss_target[0], norm_g, b_gate, pool_w,
                                                                pool_scale, final_g, weights, pos)
    mine = [_shard_sum(p, l, pos, axis, width, f"grad_shard_sum_{n}")
            for p, l, (axis, width), n in zip(pair, landed, SHARDED, names)]
    (g_in, g_pu, g_au, g_out), summed = _final_exchange(
        mine, _pack_small(small + [d_final_g, jnp.broadcast_to(loss_part, (16, LANES))]))
    g_small = _unpack_small(summed)
    loss = summed[-16, 0]
    d_small, nm_small, nv_small = _adamw_small([norm_g, b_gate, pool_w, pool_scale, final_g], g_small,
                                               [m_norm_g, m_b_gate, m_pool_w, m_pool_scale, m_final_g],
                                               [v_norm_g, v_b_gate, v_pool_w, v_pool_scale, v_final_g])
    upd_in = _adamw(w_in, g_in, m_w_in, v_w_in, "adamw_w_in")
    upd_pu = _adamw(w_pool_up, g_pu, m_w_pool_up, v_w_pool_up, "adamw_w_pool_up")
    upd_au = _adamw(w_attn_up, g_au, m_w_attn_up, v_w_attn_up, "adamw_w_attn_up")
    upd_out = _adamw(w_out, g_out, m_w_out, v_w_out, "adamw_w_out")

    def ordered(sm, k):
        big = (upd_in[k], upd_pu[k], upd_au[k], upd_out[k]) if k is not None else (g_in, g_pu, g_au, g_out)
        return [sm[0], big[0], sm[1], sm[2], sm[3], big[1], big[2], big[3], sm[4]]

    return (loss, dx[None], *ordered(g_small, None), *ordered(d_small, 0), *ordered(nm_small, 1),
            *ordered(nv_small, 2))
```

```python
import jax
import jax.numpy as jnp
import numpy as np
from jax import lax
from jax.experimental import pallas as pl
from jax.experimental.pallas import tpu as pltpu

F32 = jnp.float32
BF16 = jnp.bfloat16
MESH = pl.DeviceIdType.MESH

D_MODEL = 1024
POOL_WIDTH = 512
POOL_WINDOWS = (2, 4, 8, 16)
POOL_GROUP = 128
POOL_HALO = 16
ATTN_WIDTH = 512
HEAD_DIM = 64
HEAD_PAIRS = 4
IN_WIDTH = 5120
N_CHIPS = 4
RMS_EPS = 1e-6
C_U, C_ZP, C_Q, C_K, C_V, C_ZA, C_GL = 0, 512, 1024, 1536, 2048, 2560, 3072

ADAM_LR, ADAM_B1, ADAM_B2, ADAM_EPS, ADAM_WD, ADAM_STEP = 0.001, 0.9, 0.999, 1e-08, 0.01, 10

LANES = 128
ATTN_BLOCK = 256
QUERY_BLOCKS = 4
ROW_TILE = 256
PROJ_ROW_TILE = 512
VMEM_LIMIT = 56 * 1024 * 1024


def _params(**kw):
    return pltpu.CompilerParams(vmem_limit_bytes=VMEM_LIMIT, **kw)


def _nt(a, b):
    return lax.dot_general(a, b, (((1,), (1,)), ((), ())), preferred_element_type=F32)


def _tn(a, b):
    return lax.dot_general(a, b, (((0,), (0,)), ((), ())), preferred_element_type=F32)


def _nn(a, b):
    return jnp.dot(a, b, preferred_element_type=F32)


def _sigmoid(z):
    return 1.0 / (1.0 + jnp.exp(-z))


def _rms_inproj(x, g, w_in, layer, gather=()):
    S = x.shape[0]
    tm = min(PROJ_ROW_TILE, S)
    n_tiles = S // tm
    n_g = len(gather)

    def body(*refs):
        x_ref, g_ref = refs[:2]
        if n_g:
            (u_ref, zp_ref, q_ref, k_ref, v_ref, za_ref, gl_ref, h_ref) = refs[2 + n_g:10 + n_g]
            fulls = refs[10 + n_g:10 + 2 * n_g]
            w_ref, load_sem, send, recv = refs[10 + 2 * n_g:]
            later = _Gather(LATER_PIECES, fulls, send, recv)

            @pl.when(pl.program_id(0) == 0)
            def _():
                load = pltpu.make_async_copy(fulls[0].at[layer], w_ref, load_sem)
                load.start()
                later.start()
                load.wait()

            pl.when(pl.program_id(0) == n_tiles - 1)(later.pass_on)
        else:
            w_ref, u_ref, zp_ref, q_ref, k_ref, v_ref, za_ref, gl_ref, h_ref = refs[2:]
        xv = x_ref[...]
        r = lax.rsqrt(jnp.mean(xv * xv, axis=-1, keepdims=True) + RMS_EPS)
        h = ((xv * r) * g_ref[...]).astype(BF16)
        h_ref[...] = h

        def mm(c0, n):
            return _nn(h, w_ref[:, c0:c0 + n])

        u_ref[...] = mm(C_U, 512)
        zp_ref[...] = mm(C_ZP, 512).astype(BF16)
        q_ref[...] = (mm(C_Q, 512) * 0.125).astype(BF16)
        k_ref[...] = mm(C_K, 512).astype(BF16)
        v_ref[...] = mm(C_V, 512).astype(BF16)
        za_ref[...] = mm(C_ZA, 512).astype(BF16)
        for c in range(4):
            gl_ref[:, c * 512:(c + 1) * 512] = mm(C_GL + c * 512, 512).astype(BF16)
        if n_g:
            pl.when(pl.program_id(0) == n_tiles - 1)(later.finish)

    row = lambda n: pl.BlockSpec((tm, n), lambda i: (i, 0))
    sd = lambda n, dt: jax.ShapeDtypeStruct((S, n), dt)
    any_space = pl.BlockSpec(memory_space=pl.ANY)
    weights = [any_space] * n_g if n_g else [_layer_weight_spec(D_MODEL, IN_WIDTH, layer)]
    res = pl.pallas_call(
        body, name=f"rms_inproj_l{layer}", grid=(n_tiles,),
        in_specs=[row(D_MODEL), pl.BlockSpec((1, D_MODEL), lambda i: (0, 0))] + weights,
        out_specs=[row(512), row(512), row(512), row(512), row(512), row(512), row(2048), row(D_MODEL)]
        + [any_space] * n_g,
        out_shape=[sd(512, F32), sd(512, BF16), sd(512, BF16), sd(512, BF16), sd(512, BF16), sd(512, BF16),
                   sd(2048, BF16), sd(D_MODEL, BF16)] + [jax.ShapeDtypeStruct(f.shape, f.dtype) for f in gather],
        input_output_aliases={2 + a: 8 + a for a in range(n_g)},
        scratch_shapes=([pltpu.VMEM((D_MODEL, IN_WIDTH), BF16), pltpu.SemaphoreType.DMA(())]
                        + _Gather.semaphores(LATER_PIECES)) if n_g else [],
        compiler_params=_params(),
    )(x, g, *(gather if n_g else (w_in,)))
    return res[:8], res[8:]


def _tri(n, strict_lower):
    r = lax.broadcasted_iota(jnp.int32, (n, n), 0)
    c = lax.broadcasted_iota(jnp.int32, (n, n), 1)
    return jnp.where(r > c if strict_lower else r < c, 1.0, 0.0).astype(BF16)


def _split_dot(x, m):
    hi = x.astype(BF16)
    lo = (x - hi.astype(F32)).astype(BF16)
    return _nn(hi, m) + _nn(lo, m)


def _log_terms(z):
    lg = jnp.log(1.0 + jnp.exp(-jnp.abs(z)))
    a = jnp.minimum(z, 0.0) - lg
    return a, a - z


EXHAUSTED = -104.0
UNREACHED = -1e30


class _HeadPair:
    def __init__(self, T):
        self.T = T
        self.first = lax.broadcasted_iota(jnp.int32, (T, LANES), 1) < HEAD_DIM
        self.lane = lax.broadcasted_iota(jnp.int32, (2 * T, LANES), 1)
        row = lax.broadcasted_iota(jnp.int32, (2 * T, T), 0)
        row = jnp.where(row >= T, row - T, row)
        self.causal = row > lax.broadcasted_iota(jnp.int32, (2 * T, T), 1)
        self.below = _tri(T, True)

    def stack(self, x2):
        return jnp.concatenate([jnp.where(self.first, x2, 0), jnp.where(self.first, 0, x2)], axis=0).astype(BF16)

    def unstack(self, x):
        return jnp.where(self.first, x[:self.T], x[self.T:])

    def keys(self, ref, blocks):
        T = self.T
        return jnp.concatenate([ref[pl.ds(pl.multiple_of(j * T, T), T), :] for j, _ in blocks], axis=0)

    def log_terms(self, z, blocks):
        T = self.T
        a_all, l_all = _log_terms(z)
        a = [a_all[:, b * T:(b + 1) * T] for b in range(len(blocks))]
        l1m = [l_all[:, b * T:(b + 1) * T] for b in range(len(blocks))]
        return a, [jnp.where(self.causal, l, 0.0) if diagonal else l for l, (_, diagonal) in zip(l1m, blocks)]

    def later_sums(self, l1m):
        later = _split_dot(jnp.concatenate(l1m, axis=0), self.below)
        return [later[2 * self.T * b:2 * self.T * (b + 1)] for b in range(len(l1m))]


def _sections(x, n):
    return x.reshape(n, x.shape[0] // n, x.shape[1])


def _attn_fwd(q, k, v, layer, gather=()):
    S = q.shape[0]
    T = min(ATTN_BLOCK, S)
    nq = S // T
    jobs = min(QUERY_BLOCKS, nq)
    assert nq <= LANES and nq % jobs == 0
    per_job = nq // jobs

    n_g = len(gather)

    def body(q_ref, k_ref, v_ref, *rest):
        o_ref, c_ref = rest[n_g:n_g + 2]
        i = pl.program_id(1)
        if n_g:
            last = _Gather(LAST_PIECES, rest[n_g + 2:2 * n_g + 2], *rest[2 * n_g + 2:])
            pl.when(jnp.logical_and(pl.program_id(0) == 0, i == 0))(last.start)
        pair = _HeadPair(T)
        qs = [pair.stack(q_ref[n]) for n in range(jobs)]
        diag = [i + n * per_job for n in range(jobs)]

        def sweep(jobs):
            kv = [(pair.keys(k_ref, bl), pair.keys(v_ref, bl)) for _, bl, _ in jobs]
            zs = [_nt(qs[n], kcat) for (n, _, _), (kcat, _) in zip(jobs, kv)]
            terms = [pair.log_terms(z, bl) for (_, bl, _), z in zip(jobs, zs)]
            laters = [pair.later_sums(l1m) for _, l1m in terms]
            weights = []
            for (_, bl, (acc, run, saved)), (a, l1m), later in zip(jobs, terms, laters):
                ws = []
                for b, (j, diagonal) in enumerate(bl):
                    saved = jnp.where(pair.lane == j, run, saved)
                    w = jnp.exp(a[b] + later[b] + run)
                    ws.append(jnp.where(pair.causal, w, 0.0) if diagonal else w)
                    run = run + jnp.sum(l1m[b], axis=1, keepdims=True)
                weights.append((jnp.concatenate(ws, axis=1).astype(BF16), acc, run, saved))
            return [(acc + _nn(w, vcat), run, saved) for (w, acc, run, saved), (_, vcat) in zip(weights, kv)]

        def alive(carry):
            return (jnp.max(carry[1]) > EXHAUSTED).astype(jnp.int32)

        def older_blocks(n, carry):
            def older_block(state):
                j, _, c = state
                c = sweep([(n, [(j, False)], c)])[0]
                return j - 1, alive(c), c

            return lax.while_loop(lambda st: jnp.logical_and(st[0] >= 0, st[1] > 0), older_block,
                                  (diag[n] - 2, alive(carry), carry))[2]

        def run(first_blocks):
            init = (jnp.zeros((2 * T, LANES), F32), jnp.zeros((2 * T, 1), F32),
                    jnp.full((2 * T, LANES), UNREACHED, F32))
            carries = sweep([(n, first_blocks[n], init) for n in range(jobs)])
            for n in range(jobs):
                acc, _, saved = older_blocks(n, carries[n])
                o_ref[n] = pair.unstack(acc).astype(BF16)
                c_ref[n, :, :LANES] = saved[:T]
                c_ref[n, :, LANES:] = saved[T:]

        with_previous = lambda d: [(d, True), (d - 1, False)]

        @pl.when(i == 0)
        def _():
            run([[(diag[0], True)]] + [with_previous(d) for d in diag[1:]])

        @pl.when(i > 0)
        def _():
            run([with_previous(d) for d in diag])

        if n_g:
            pl.when(jnp.logical_and(pl.program_id(0) == HEAD_PAIRS - 1, i == 0))(last.pass_on)
            pl.when(jnp.logical_and(pl.program_id(0) == HEAD_PAIRS - 1, i == per_job - 1))(last.finish)

    blk = lambda n: pl.BlockSpec((jobs, T, n), lambda p, i: (0, i, p))
    full = pl.BlockSpec((S, LANES), lambda p, i: (0, p))
    any_space = pl.BlockSpec(memory_space=pl.ANY)
    res = pl.pallas_call(
        body, name=f"attn_fwd_l{layer}", grid=(HEAD_PAIRS, per_job),
        in_specs=[blk(LANES), full, full] + [any_space] * n_g,
        out_specs=[blk(LANES), blk(2 * LANES)] + [any_space] * n_g,
        out_shape=[jax.ShapeDtypeStruct((jobs, S // jobs, ATTN_WIDTH), BF16),
                   jax.ShapeDtypeStruct((jobs, S // jobs, 8 * LANES), F32)]
        + [jax.ShapeDtypeStruct(f.shape, f.dtype) for f in gather],
        input_output_aliases={3 + a: 2 + a for a in range(n_g)},
        scratch_shapes=_Gather.semaphores(LAST_PIECES) if n_g else [],
        compiler_params=_params(),
    )(_sections(q, jobs), k, v, *gather)
    return res[0].reshape(S, ATTN_WIDTH), res[1].reshape(S, 8 * LANES), res[2:]


def _attn_bwd(q, k, v, saved, do, layer):
    S = q.shape[0]
    T = min(ATTN_BLOCK, S)
    nq = S // T
    jobs = min(QUERY_BLOCKS, nq)
    per_job = nq // jobs

    def body(q_ref, k_ref, v_ref, c_ref, do_ref, dq_ref, dk_ref, dv_ref):
        i = pl.program_id(1)

        @pl.when(i == 0)
        def _():
            dk_ref[...] = jnp.zeros_like(dk_ref)
            dv_ref[...] = jnp.zeros_like(dv_ref)

        pair = _HeadPair(T)
        diag = [i + n * per_job for n in range(jobs)]
        qs = [pair.stack(q_ref[n]) for n in range(jobs)]
        dos = [pair.stack(do_ref[n].astype(BF16)) for n in range(jobs)]
        saved = [jnp.concatenate([c_ref[n, :, :LANES], c_ref[n, :, LANES:]], axis=0) for n in range(jobs)]
        before = _tri(T, False)

        def sweep(jobs):
            kv = [(pair.keys(k_ref, bl), pair.keys(v_ref, bl)) for _, bl, _ in jobs]
            zs = [_nt(qs[n], kcat) for (n, _, _), (kcat, _) in zip(jobs, kv)]
            gs = [_nt(dos[n], vcat) for (n, _, _), (_, vcat) in zip(jobs, kv)]
            terms = [pair.log_terms(z, bl) for (_, bl, _), z in zip(jobs, zs)]
            laters = [pair.later_sums(l1m) for _, l1m in terms]
            ws, es = [], []
            for (n, bl, _), (a, _), later, g in zip(jobs, terms, laters, gs):
                w_job, e_job = [], []
                for b, (j, diagonal) in enumerate(bl):
                    run = jnp.sum(jnp.where(pair.lane == j, saved[n], 0.0), axis=1, keepdims=True)
                    w = jnp.exp(a[b] + later[b] + run)
                    w_job.append(jnp.where(pair.causal, w, 0.0) if diagonal else w)
                    e_job.append(w_job[b] * g[:, b * T:(b + 1) * T])
                ws.append(w_job)
                es.append(e_job)
            prefixes = [_nn(jnp.concatenate(e_job, axis=0).astype(BF16), before) for e_job in es]
            dzs, olders = [], []
            for (_, bl, (_, older)), (a, _), e_job, prefix in zip(jobs, terms, es, prefixes):
                dz_job = []
                for b, (j, diagonal) in enumerate(bl):
                    dz = e_job[b] - jnp.exp(a[b]) * (e_job[b] + (prefix[2 * T * b:2 * T * (b + 1)] + older))
                    dz_job.append(jnp.where(pair.causal, dz, 0.0) if diagonal else dz)
                    older = older + jnp.sum(e_job[b], axis=1, keepdims=True)
                dzs.append(jnp.concatenate(dz_job, axis=1).astype(BF16))
                olders.append(older)
            out = []
            for (n, bl, (dq, _)), dz, w_job, older, (kcat, _) in zip(jobs, dzs, ws, olders, kv):
                dk = _tn(dz, qs[n])
                dv = _tn(jnp.concatenate(w_job, axis=1).astype(BF16), dos[n])
                for b, (j, _) in enumerate(bl):
                    rows = pl.ds(pl.multiple_of(j * T, T), T)
                    dk_ref[rows, :] += dk[b * T:(b + 1) * T]
                    dv_ref[rows, :] += dv[b * T:(b + 1) * T]
                out.append((dq + _nn(dz, kcat), older))
            return out

        def older_blocks(n):
            col_max = jnp.max(saved[n], axis=0, keepdims=True)
            lane_row = lax.broadcasted_iota(jnp.int32, (1, LANES), 1)
            reached = jnp.sum(jnp.where(jnp.logical_and(col_max > EXHAUSTED, lane_row < diag[n]), 1, 0))
            init = (jnp.zeros((2 * T, LANES), F32), jnp.zeros((2 * T, 1), F32))
            return lax.fori_loop(diag[n] - reached, diag[n] - 1, lambda j, c: sweep([(n, [(j, False)], c)])[0], init)

        def run(last_blocks):
            carries = sweep([(n, last_blocks[n], older_blocks(n)) for n in range(jobs)])
            for n in range(jobs):
                dq_ref[n] = (pair.unstack(carries[n][0]) * 0.125).astype(BF16)

        with_previous = lambda d: [(d - 1, False), (d, True)]

        @pl.when(i == 0)
        def _():
            run([[(diag[0], True)]] + [with_previous(d) for d in diag[1:]])

        @pl.when(i > 0)
        def _():
            run([with_previous(d) for d in diag])

    blk = lambda n: pl.BlockSpec((jobs, T, n), lambda p, i: (0, i, p))
    full = pl.BlockSpec((S, LANES), lambda p, i: (0, p))
    out = jax.ShapeDtypeStruct((S, ATTN_WIDTH), F32)
    dq, dk, dv = pl.pallas_call(
        body, name=f"attn_bwd_l{layer}", grid=(HEAD_PAIRS, per_job),
        in_specs=[blk(LANES), full, full, blk(2 * LANES), blk(LANES)],
        out_specs=[blk(LANES), full, full],
        out_shape=[jax.ShapeDtypeStruct((jobs, S // jobs, ATTN_WIDTH), BF16), out, out],
        compiler_params=_params(),
    )(_sections(q, jobs), k, v, _sections(saved, jobs), _sections(do, jobs))
    return dq.reshape(S, ATTN_WIDTH), dk, dv


def _pool_counts(row0, tm):
    pos = row0 + lax.broadcasted_iota(jnp.int32, (tm, 1), 0)
    return [1.0 / jnp.minimum(pos + 1, w).astype(F32) for w in POOL_WINDOWS]


def _window_bands(tm, backward):
    t = np.arange(tm)[:, None]
    c = np.arange(tm)[None, :]
    off = c - t if backward else t - c
    main = np.stack([(off >= 0) & (off < w) for w in POOL_WINDOWS])
    r = np.arange(POOL_HALO)[:, None]
    h = np.arange(POOL_HALO)[None, :]
    off = h - r + POOL_HALO if backward else r - h + POOL_HALO
    edge = np.concatenate([(off < w) for w in POOL_WINDOWS])
    return jnp.asarray(main, BF16), jnp.asarray(edge, BF16)


def _window_sums(tile, beside, main_ref, edge_ref, backward):
    tm = tile.shape[0]
    tb = tile.astype(BF16)
    edge = _nn(edge_ref[...], beside.astype(BF16))
    sums = []
    for g in range(len(POOL_WINDOWS)):
        cols = slice(g * POOL_GROUP, (g + 1) * POOL_GROUP)
        tot = _nn(main_ref[g], tb[:, cols])
        extra = edge[g * POOL_HALO:(g + 1) * POOL_HALO, cols]
        if backward:
            sums.append(jnp.concatenate([tot[:tm - POOL_HALO], tot[tm - POOL_HALO:] + extra], axis=0))
        else:
            sums.append(jnp.concatenate([tot[:POOL_HALO] + extra, tot[POOL_HALO:]], axis=0))
    return sums


def _post_forward(u, history, bands, inv_cnt, zp, o, za, gl, bg, pw_ref, scale, wpu_ref, wau_ref):
    pooled, mixed = [], []
    for g, tot in enumerate(_window_sums(u, history, *bands, False)):
        pg = (tot * inv_cnt[g] - u[:, g * POOL_GROUP:(g + 1) * POOL_GROUP]).astype(BF16)
        pooled.append(pg)
        mixed.append(_nn(pg, pw_ref[g].astype(BF16)))
    pooled = jnp.concatenate(pooled, axis=1)
    mixed = jnp.concatenate(mixed, axis=1)
    zp, za, o = zp.astype(F32), za.astype(F32), o.astype(F32)
    sp = _sigmoid(zp)
    sa = _sigmoid(za)
    y_pool = (mixed * scale) * (zp * sp)
    y_attn = o * (za * sa)
    gate = _sigmoid(gl + bg)
    g0, g1 = gate[:, :D_MODEL], gate[:, D_MODEL:]
    up_p = _nn(y_pool.astype(BF16), wpu_ref[...])
    up_a = _nn(y_attn.astype(BF16), wau_ref[...])
    merged = g0 * up_p + g1 * up_a
    return pooled, mixed, sp, sa, y_pool, y_attn, g0, g1, up_p, up_a, merged


def _row_specs(tm, rev, n_tiles):
    tile_of = (lambda i: n_tiles - 1 - i) if rev else (lambda i: i)
    row = lambda n: pl.BlockSpec((tm, n), lambda i: (tile_of(i), 0))
    halo = pl.BlockSpec((POOL_HALO, POOL_WIDTH),
                        lambda i: (jnp.maximum(tile_of(i) * (tm // POOL_HALO) - 1, 0), 0))
    const = lambda shape: pl.BlockSpec(shape, lambda i: (0,) * len(shape))
    return tile_of, row, halo, const


def _layer_weight_spec(rows, cols, layer):
    return pl.BlockSpec((None, rows, cols), lambda i: (layer, 0, 0), pipeline_mode=pl.Buffered(1))


def _post_fwd(x, u, zp, o, za, gl, bg, pw, scale, wpu, wau, wout, layer, head=()):
    S = x.shape[0]
    tm = min(ROW_TILE, S)
    n_tiles = S // tm
    tile_of, row, halo, const = _row_specs(tm, False, n_tiles)

    def body(x_ref, u_ref, uh_ref, main_ref, edge_ref, zp_ref, o_ref, za_ref, gl_ref, bg_ref, pw_ref, sc_ref, wpu_ref,
             wau_ref, wout_ref, *rest):
        i = pl.program_id(0)
        vals = _post_forward(u_ref[...], jnp.where(i == 0, 0.0, uh_ref[...]), (main_ref, edge_ref),
                             _pool_counts(i * tm, tm), zp_ref[...], o_ref[...], za_ref[...], gl_ref[...], bg_ref[...],
                             pw_ref, sc_ref[...], wpu_ref, wau_ref)
        xv = x_ref[...] + _nn(vals[-1].astype(BF16), wout_ref[...])
        if not head:
            rest[0][...] = xv
            return
        gf_ref, t_ref, loss_ref, dx_ref, dg_ref = rest

        @pl.when(i == 0)
        def _():
            loss_ref[...] = jnp.zeros_like(loss_ref)
            dg_ref[...] = jnp.zeros_like(dg_ref)

        r = lax.rsqrt(jnp.mean(xv * xv, axis=-1, keepdims=True) + RMS_EPS)
        diff = (xv * r) * gf_ref[...] - t_ref[...]
        per_row = jnp.mean(diff * diff, axis=-1, keepdims=True)
        loss_ref[...] += 0.5 * jnp.sum(per_row, axis=0, keepdims=True)
        dx, dg_rows = _rms_backward(diff * (1.0 / D_MODEL), xv, r, gf_ref[...])
        dx_ref[...] = dx
        dg_ref[...] += jnp.sum(dg_rows, axis=0, keepdims=True)

    out = jax.ShapeDtypeStruct((S, D_MODEL), F32)
    return pl.pallas_call(
        body, name=f"post_fwd_l{layer}", grid=(n_tiles,),
        in_specs=[row(D_MODEL), row(512), halo, const((4, tm, tm)), const((4 * POOL_HALO, POOL_HALO)), row(512),
                  row(512), row(512), row(2048), const((1, 2048)), const((4, POOL_GROUP, POOL_GROUP)),
                  const((1, POOL_WIDTH)),
                  _layer_weight_spec(POOL_WIDTH, D_MODEL, layer), _layer_weight_spec(ATTN_WIDTH, D_MODEL, layer),
                  _layer_weight_spec(D_MODEL, D_MODEL, layer)] + ([const((1, D_MODEL)), row(D_MODEL)] if head else []),
        out_specs=[const((1, LANES)), row(D_MODEL), const((1, D_MODEL))] if head else row(D_MODEL),
        out_shape=[jax.ShapeDtypeStruct((1, LANES), F32), out, jax.ShapeDtypeStruct((1, D_MODEL), F32)] if head else out,
        compiler_params=_params(),
    )(x, u, u, *_window_bands(tm, False), zp, o, za, gl, bg, pw, scale, wpu, wau, wout, *head)


def _post_bwd(dx, u, zp, o, za, gl, bg, pw, scale, wpu, wau, wout, layer):
    S = dx.shape[0]
    tm = min(ROW_TILE, S)
    n_tiles = S // tm
    tile_of, row, halo, const = _row_specs(tm, True, n_tiles)

    def body(dx_ref, u_ref, uh_ref, main_ref, edge_ref, back_main_ref, back_edge_ref, zp_ref, o_ref, za_ref, gl_ref,
             bg_ref, pw_ref, sc_ref, wpu_ref, wau_ref, wout_ref,
             duz_ref, do_ref, dzg_ref, dsc_ref, dbg_ref,
             merged_ref, dup_ref, dua_ref, yp_ref, ya_ref, pooled_ref, dmixed_ref, nxt_ref):
        step = pl.program_id(0)
        i = tile_of(step)

        @pl.when(step == 0)
        def _():
            dsc_ref[...] = jnp.zeros_like(dsc_ref)
            dbg_ref[...] = jnp.zeros_like(dbg_ref)
            nxt_ref[...] = jnp.zeros_like(nxt_ref)

        inv_cnt = _pool_counts(i * tm, tm)
        zp, za, o = zp_ref[...].astype(F32), za_ref[...].astype(F32), o_ref[...].astype(F32)
        pooled, mixed, sp, sa, y_pool, y_attn, g0, g1, up_p, up_a, merged = _post_forward(
            u_ref[...], jnp.where(i == 0, 0.0, uh_ref[...]), (main_ref, edge_ref), inv_cnt, zp, o, za, gl_ref[...],
            bg_ref[...], pw_ref, sc_ref[...], wpu_ref, wau_ref)
        merged_ref[...] = merged.astype(BF16)
        yp_ref[...] = y_pool.astype(BF16)
        ya_ref[...] = y_attn.astype(BF16)
        pooled_ref[...] = pooled

        dmerged = _nt(dx_ref[...].astype(BF16), wout_ref[...])
        dup = (dmerged * g0).astype(BF16)
        dua = (dmerged * g1).astype(BF16)
        dup_ref[...] = dup
        dua_ref[...] = dua
        dgl0 = (dmerged * up_p) * (g0 * (1.0 - g0))
        dgl1 = (dmerged * up_a) * (g1 * (1.0 - g1))
        dzg_ref[:, ATTN_WIDTH:ATTN_WIDTH + D_MODEL] = dgl0.astype(BF16)
        dzg_ref[:, ATTN_WIDTH + D_MODEL:] = dgl1.astype(BF16)
        dbg_ref[:, :D_MODEL] += jnp.sum(dgl0, axis=0, keepdims=True)
        dbg_ref[:, D_MODEL:] += jnp.sum(dgl1, axis=0, keepdims=True)

        dy_attn = _nt(dua, wau_ref[...])
        do_ref[...] = (dy_attn * (za * sa)).astype(BF16)
        dzg_ref[:, :ATTN_WIDTH] = ((dy_attn * o) * (sa * (1.0 + za * (1.0 - sa)))).astype(BF16)

        dy_pool = _nt(dup, wpu_ref[...])
        ms = mixed * sc_ref[...]
        dms = dy_pool * (zp * sp)
        duz_ref[:, POOL_WIDTH:] = ((dy_pool * ms) * (sp * (1.0 + zp * (1.0 - sp)))).astype(BF16)
        dsc_ref[...] += jnp.sum(dms * mixed, axis=0, keepdims=True)
        dmixed = (dms * sc_ref[...]).astype(BF16)
        dmixed_ref[...] = dmixed
        dpooled = [_nt(dmixed[:, g * POOL_GROUP:(g + 1) * POOL_GROUP], pw_ref[g].astype(BF16)) for g in range(4)]
        scaled = jnp.concatenate([d * inv for d, inv in zip(dpooled, inv_cnt)], axis=1)
        for g, tot in enumerate(_window_sums(scaled, nxt_ref[...], back_main_ref, back_edge_ref, True)):
            duz_ref[:, g * POOL_GROUP:(g + 1) * POOL_GROUP] = (tot - dpooled[g]).astype(BF16)
        nxt_ref[...] = scaled[:POOL_HALO]

    sd = lambda n, dt: jax.ShapeDtypeStruct((S, n), dt)
    bands = [const((4, tm, tm)), const((4 * POOL_HALO, POOL_HALO))]
    return pl.pallas_call(
        body, name=f"post_bwd_l{layer}", grid=(n_tiles,),
        in_specs=[row(D_MODEL), row(512), halo, *bands, *bands, row(512), row(512), row(512), row(2048),
                  const((1, 2048)), const((4, POOL_GROUP, POOL_GROUP)), const((1, POOL_WIDTH)),
                  _layer_weight_spec(POOL_WIDTH, D_MODEL, layer), _layer_weight_spec(ATTN_WIDTH, D_MODEL, layer),
                  _layer_weight_spec(D_MODEL, D_MODEL, layer)],
        out_specs=[row(1024), row(512), row(2560), const((1, POOL_WIDTH)), const((1, 2048)),
                   row(D_MODEL), row(D_MODEL), row(D_MODEL), row(512), row(512), row(512), row(512)],
        out_shape=[sd(1024, BF16), sd(512, BF16), sd(2560, BF16),
                   jax.ShapeDtypeStruct((1, POOL_WIDTH), F32), jax.ShapeDtypeStruct((1, 2048), F32),
                   sd(D_MODEL, BF16), sd(D_MODEL, BF16), sd(D_MODEL, BF16), sd(512, BF16), sd(512, BF16),
                   sd(512, BF16), sd(512, BF16)],
        scratch_shapes=[pltpu.VMEM((POOL_HALO, POOL_WIDTH), F32)],
        compiler_params=_params(),
    )(dx, u, u, *_window_bands(tm, False), *_window_bands(tm, True), zp, o, za, gl, bg, pw, scale, wpu, wau, wout)


def _rms_backward(dh, xv, r, g):
    xhat = xv * r
    dxhat = dh * g
    return r * (dxhat - xhat * jnp.mean(dxhat * xhat, axis=-1, keepdims=True)), dh * xhat


def _inproj_bwd(pieces, w_in, x, g, dx_res, layer, exchange=()):
    S = x.shape[0]
    tm = min(PROJ_ROW_TILE, S)
    cols = [(c0, p.shape[1]) for p, c0 in pieces]

    def body(ins, outs):
        piece_refs = ins[:len(cols)]
        w_ref, x_ref, g_ref, res_ref = ins[len(cols):]
        dx_ref, dg_ref = outs

        @pl.when(pl.program_id(0) == 0)
        def _():
            dg_ref[...] = jnp.zeros_like(dg_ref)

        dh = jnp.zeros((tm, D_MODEL), F32)
        for p_ref, (c0, n) in zip(piece_refs, cols):
            for c in range(0, n, 512):
                dh = dh + _nt(p_ref[:, c:c + 512].astype(BF16), w_ref[:, c0 + c:c0 + c + 512])
        xv = x_ref[...]
        r = lax.rsqrt(jnp.mean(xv * xv, axis=-1, keepdims=True) + RMS_EPS)
        dx, dg_rows = _rms_backward(dh, xv, r, g_ref[...])
        dx_ref[...] = res_ref[...] + dx
        dg_ref[...] += jnp.sum(dg_rows, axis=0, keepdims=True)

    row = lambda n: pl.BlockSpec((tm, n), lambda i: (i, 0))
    vec = pl.BlockSpec((1, D_MODEL), lambda i: (0, 0))
    any_space = pl.BlockSpec(memory_space=pl.ANY)
    n_x = len(exchange)
    grid = (S // tm,)
    res = pl.pallas_call(
        _with_swap(body, grid, len(cols) + 4, 2, n_x, rider=_ChipExchange), name=f"inproj_bwd_l{layer}", grid=grid,
        in_specs=[row(n) for _, n in cols] + [_layer_weight_spec(D_MODEL, IN_WIDTH, layer), row(D_MODEL), vec,
                                              row(D_MODEL)] + [any_space] * n_x,
        out_specs=[row(D_MODEL), vec] + [any_space] * n_x,
        out_shape=[jax.ShapeDtypeStruct((S, D_MODEL), F32), jax.ShapeDtypeStruct((1, D_MODEL), F32)]
        + _ChipExchange.landing(exchange),
        scratch_shapes=_ChipExchange.semaphores(n_x) if n_x else [],
        compiler_params=_params(),
    )(*[p for p, _ in pieces], w_in, x, g, dx_res, *exchange)
    return res[0], res[1], res[2:]


class _SiblingSwap:
    def __init__(self, mine, theirs, send, recv):
        x, y, c, _ = _place()
        self.copies = [_remote(m.at[1 - c], t, send, recv, a, (x, y, 1 - c))
                       for a, (m, t) in enumerate(zip(mine, theirs))]

    def start(self):
        for cp in self.copies:
            cp.start()

    def wait(self):
        for cp in self.copies:
            cp.wait()


def _with_swap(body, grid, n_in, n_out, n_swap, rider=_SiblingSwap):
    if not n_swap:
        return lambda *refs: body(refs[:n_in], refs[n_in:])

    def riding(*refs):
        ins, mine = refs[:n_in], refs[n_in:n_in + n_swap]
        outs, theirs = refs[n_in + n_swap:n_in + n_swap + n_out], refs[n_in + n_swap + n_out:n_in + 2 * n_swap + n_out]
        swap = rider(mine, theirs, *refs[n_in + 2 * n_swap + n_out:])
        step = [pl.program_id(d) for d in range(len(grid))]
        first, last = step[0] == 0, step[0] == grid[0] - 1
        for d in range(1, len(grid)):
            first, last = jnp.logical_and(first, step[d] == 0), jnp.logical_and(last, step[d] == grid[d] - 1)
        pl.when(first)(swap.start)
        body(ins, outs)
        pl.when(last)(swap.wait)

    return riding


def _swap_specs(swap):
    any_space = pl.BlockSpec(memory_space=pl.ANY)
    shapes = [jax.ShapeDtypeStruct(d.shape[1:], d.dtype) for d in swap]
    sems = [pltpu.SemaphoreType.DMA((len(swap),)), pltpu.SemaphoreType.DMA((len(swap),))] if swap else []
    return [any_space] * len(swap), shapes, sems


def _wgrad(a, b, name, layer, into=None, col0=0, n_total=None, swap=()):
    S, M = a.shape
    N = b.shape[1]
    n_total = N if n_total is None else n_total
    tk = min(2048, S)
    tn = max(t for t in range(LANES, min(N, 1280) + 1, LANES) if N % t == 0 and col0 % t == 0)
    grid = (N // tn, S // tk)

    def body(ins, outs):
        prod = _tn(ins[0][...].astype(BF16), ins[1][...].astype(BF16))

        @pl.when(pl.program_id(1) == 0)
        def _():
            outs[0][...] = prod

        @pl.when(pl.program_id(1) > 0)
        def _():
            outs[0][...] += prod

    in_specs = [pl.BlockSpec((tk, M), lambda j, k: (k, 0)), pl.BlockSpec((tk, tn), lambda j, k: (k, j))]
    args = [a, b]
    aliases = {}
    if into is not None:
        in_specs.append(pl.BlockSpec(memory_space=pl.ANY))
        args.append(into)
        aliases = {2: 0}
    swap_specs, swap_shapes, swap_sems = _swap_specs(swap)
    res = pl.pallas_call(
        _with_swap(body, grid, len(args), 1, len(swap)), name=name, grid=grid,
        in_specs=in_specs + swap_specs,
        out_specs=[pl.BlockSpec((None, M, tn), lambda j, k: (layer, 0, col0 // tn + j))] + swap_specs,
        out_shape=[jax.ShapeDtypeStruct((2, M, n_total), F32)] + swap_shapes,
        input_output_aliases=aliases,
        scratch_shapes=swap_sems,
        compiler_params=_params(),
    )(*args, *swap)
    return (res[0], res[1:]) if swap else res[0]


def _pool_wgrad(pooled, dmixed, layer, swap=()):
    S = pooled.shape[0]
    tk = min(8192, S)
    grid = (4, S // tk)

    def body(ins, outs):
        prod = _tn(ins[0][...], ins[1][...])

        @pl.when(pl.program_id(1) == 0)
        def _():
            outs[0][...] = prod

        @pl.when(pl.program_id(1) > 0)
        def _():
            outs[0][...] += prod

    blk = pl.BlockSpec((tk, POOL_GROUP), lambda g, k: (k, g))
    swap_specs, swap_shapes, swap_sems = _swap_specs(swap)
    res = pl.pallas_call(
        _with_swap(body, grid, 2, 1, len(swap)), name=f"pool_wgrad_l{layer}", grid=grid,
        in_specs=[blk, blk] + swap_specs,
        out_specs=[pl.BlockSpec((None, POOL_GROUP, POOL_GROUP), lambda g, k: (g, 0, 0))] + swap_specs,
        out_shape=[jax.ShapeDtypeStruct((4, POOL_GROUP, POOL_GROUP), F32)] + swap_shapes,
        scratch_shapes=swap_sems,
        compiler_params=_params(),
    )(pooled, dmixed, *swap)
    return (res[0], res[1:]) if swap else res[0]


def _local_step(x, target, norm_g, b_gate, pool_w, pool_scale, final_g, weights, pos):
    n_layers = norm_g.shape[0]
    saved = []
    for l in range(n_layers):
        g = norm_g[l][None]
        bg = b_gate[l][None]
        sc = pool_scale[l][None]
        if l == 0:
            (u, zp, q, k, v, za, gl, h), (w_in, w_pu, w_au, w_out) = _rms_inproj(x, g, None, l, gather=weights)
        else:
            (u, zp, q, k, v, za, gl, h), _ = _rms_inproj(x, g, w_in, l)
        if l == 0:
            o, carry, (w_in,) = _attn_fwd(q, k, v, l, gather=(w_in,))
        else:
            o, carry, _ = _attn_fwd(q, k, v, l)
        saved.append((x, g, bg, sc, u, zp, q, k, v, za, gl, h, o, carry))
        if l < n_layers - 1:
            x = _post_fwd(x, u, zp, o, za, gl, bg, pool_w[l], sc, w_pu, w_au, w_out, l)
        else:
            loss, dx, d_final_g = _post_fwd(x, u, zp, o, za, gl, bg, pool_w[l], sc, w_pu, w_au, w_out, l,
                                            head=(final_g[None], target))

    small = [None] * n_layers
    dw_in = dw_out = dw_pu = dw_au = None
    for l in reversed(range(n_layers)):
        x_in, g, bg, sc, u, zp, q, k, v, za, gl, h, o, carry = saved[l]
        (duz, do, dzg, dsc, dbg, merged, dup, dua, y_pool, y_attn, pooled, dmixed) = _post_bwd(
            dx, u, zp, o, za, gl, bg, pool_w[l], sc, w_pu, w_au, w_out, l)
        dq, dk, dv = _attn_bwd(q, k, v, carry, do, l)
        pieces = [(duz, C_U), (dq, C_Q), (dk, C_K), (dv, C_V), (dzg, C_ZA)]
        for p, c0 in pieces:
            dw_in = _wgrad(h, p, f"wgrad_in_l{l}_c{c0}", l, into=dw_in, col0=c0, n_total=IN_WIDTH)
        if l > 0:
            dw_out = _wgrad(merged, dx, f"wgrad_out_l{l}", l, into=dw_out)
        else:
            dw_out, (other_in,) = _wgrad(merged, dx, f"wgrad_out_l{l}", l, into=dw_out, swap=(dw_in,))
        dw_pu = _wgrad(y_pool, dup, f"wgrad_pu_l{l}", l, into=dw_pu)
        dw_au = _wgrad(y_attn, dua, f"wgrad_au_l{l}", l, into=dw_au)
        if l > 0:
            dpw = _pool_wgrad(pooled, dmixed, l)
        else:
            dpw, (other_pu, other_au, other_out) = _pool_wgrad(pooled, dmixed, l, swap=(dw_pu, dw_au, dw_out))
        if l > 0:
            dx, dg, _ = _inproj_bwd(pieces, w_in, x_in, g, dx, l)
        else:
            pair = [_pair_sum(d, t, pos, f"grad_pair_sum_{n}") for d, t, n in
                    zip((dw_in, dw_pu, dw_au, dw_out), (other_in, other_pu, other_au, other_out), SHARDED_NAMES)]
            dx, dg, landed = _inproj_bwd(pieces, w_in, x_in, g, dx, l, exchange=[pb for _, pb in pair])
        small[l] = (dg[0], dbg[0], dpw, dsc[0])
    small = [jnp.stack([small[l][i] for l in range(n_layers)]) for i in range(4)]
    return loss[0, 0], dx, d_final_g[0], small, [p for p, _ in pair], landed


SHARDED = ((2, 1280), (2, 256), (2, 256), (1, 256))
SHARDED_NAMES = ("w_in", "w_pool_up", "w_attn_up", "w_out")
ANY = pl.BlockSpec(memory_space=pl.ANY)


def _part(ref, s, axis, width):
    sl = pl.ds(pl.multiple_of(s * width, width), width)
    return ref.at[:, sl] if axis == 2 else ref.at[sl, :]


def _place():
    x, y, c = lax.axis_index("x"), lax.axis_index("y"), lax.axis_index("c")
    return x, y, c, 2 * x + y


def _other_chip(x, y, m):
    px = 1 - x if m & 2 else x
    py = 1 - y if m & 1 else y
    return px, py, 2 * px + py


def _remote(src, dst, send, recv, k, to):
    return pltpu.make_async_remote_copy(src_ref=src, dst_ref=dst, send_sem=send.at[k], recv_sem=recv.at[k],
                                        device_id=to, device_id_type=MESH)


def _part_spec(tr, rows_s, cols_s, axis, width, lead):
    if axis == 2:
        return pl.BlockSpec((None, tr, width), lambda *a: (lead(a), a[-2], a[-1][1]))
    return pl.BlockSpec((None, tr, cols_s), lambda *a: (lead(a), a[-1][1] * (rows_s // tr) + a[-2], 0))


def _cast_into_place(w, pos, axis, width, name):
    L, Rs, Cs = w.shape
    tr = min(256, Rs)
    shape = [L, Rs, Cs]
    shape[axis] *= N_CHIPS

    def body(pos_ref, w_ref, o_ref):
        o_ref[...] = w_ref[...].astype(BF16)

    return pl.pallas_call(
        body, name=name,
        grid_spec=pltpu.PrefetchScalarGridSpec(
            num_scalar_prefetch=1, grid=(L, Rs // tr),
            in_specs=[pl.BlockSpec((None, tr, Cs), lambda l, i, pos: (l, i, 0))],
            out_specs=_part_spec(tr, Rs, Cs, axis, width, lambda a: a[0])),
        out_shape=jax.ShapeDtypeStruct(tuple(shape), BF16),
        compiler_params=_params(),
    )(pos, w)


def _w_in_half(layer):
    def piece(refs, who, shard):
        rows = pl.ds(pl.multiple_of(who * (D_MODEL // 2), D_MODEL // 2), D_MODEL // 2)
        return refs[0].at[layer, rows, pl.ds(pl.multiple_of(shard * SHARDED[0][1], SHARDED[0][1]), SHARDED[0][1])]
    return piece


def _whole_layer(a):
    def piece(refs, who, shard):
        return _part(refs[a].at[who], shard, *SHARDED[a])
    return piece


FIRST_PIECES = (_w_in_half(0),)
LATER_PIECES = (_whole_layer(1), _whole_layer(2), _whole_layer(3))
LAST_PIECES = (_w_in_half(1),)


class _Gather:
    def __init__(self, pieces, refs, send, recv):
        self.pieces, self.refs, self.send, self.recv = pieces, refs, send, recv
        self.x, self.y, self.c, s = _place()
        self.first = []
        for u, piece in enumerate(pieces):
            for m in (1, 2, 3):
                px, py, _ = _other_chip(self.x, self.y, m)
                own = piece(refs, self.c, s)
                self.first.append(_remote(own, own, send, recv, 3 * u + m - 1, (px, py, self.c)))

    def start(self):
        for cp in self.first:
            cp.start()

    def _landed(self, u, m, who):
        _, _, sp = _other_chip(self.x, self.y, m)
        return self.pieces[u](self.refs, who, sp)

    def _passed(self):
        n = len(self.pieces)
        return [_remote(self._landed(u, m, self.c), self._landed(u, m, self.c), self.send, self.recv,
                        3 * n + 3 * u + m - 1, (self.x, self.y, 1 - self.c)) for m in (1, 2, 3) for u in range(n)]

    def pass_on(self):
        me = (self.x, self.y, self.c)
        passed = iter(self._passed())
        for m in (1, 2, 3):
            for u in range(len(self.pieces)):
                got = self._landed(u, m, self.c)
                _remote(got, got, self.send, self.recv, 3 * u + m - 1, me).wait_recv()
                next(passed).start()

    def finish(self):
        n = len(self.pieces)
        for m in (1, 2, 3):
            for u in range(n):
                got = self._landed(u, m, 1 - self.c)
                _remote(got, got, self.send, self.recv, 3 * n + 3 * u + m - 1, (self.x, self.y, self.c)).wait_recv()
        for cp in self.first + self._passed():
            cp.wait_send()

    @staticmethod
    def semaphores(pieces):
        return [pltpu.SemaphoreType.DMA((6 * len(pieces),)), pltpu.SemaphoreType.DMA((6 * len(pieces),))]


def _gather_first(fulls):
    n = len(fulls)

    def body(*refs):
        gather = _Gather(FIRST_PIECES, refs[n:2 * n], *refs[2 * n:])
        gather.start()
        gather.pass_on()
        gather.finish()

    return pl.pallas_call(
        body, name="gather_first",
        in_specs=[ANY] * n, out_specs=[ANY] * n,
        out_shape=[jax.ShapeDtypeStruct(f.shape, f.dtype) for f in fulls],
        input_output_aliases={a: a for a in range(n)},
        scratch_shapes=_Gather.semaphores(FIRST_PIECES),
    )(*fulls)


def _pair_sum(dw, other, pos, name):
    _, R, C = dw.shape
    tr = 128 if C > 1024 else 256

    def body(pos_ref, a_ref, b_ref, o_ref, ob_ref):
        tot = a_ref[...] + b_ref[...]
        o_ref[...] = tot
        ob_ref[...] = tot.astype(BF16)

    blk = pl.BlockSpec((tr, C), lambda i, pos: (i, 0))
    return pl.pallas_call(
        body, name=name,
        grid_spec=pltpu.PrefetchScalarGridSpec(
            num_scalar_prefetch=1, grid=(R // tr,),
            in_specs=[pl.BlockSpec((None, tr, C), lambda i, pos: (pos[0], i, 0)), blk],
            out_specs=[blk, blk]),
        out_shape=[jax.ShapeDtypeStruct((R, C), F32), jax.ShapeDtypeStruct((R, C), BF16)],
        compiler_params=_params(),
    )(pos, dw, other)


class _ChipExchange:
    def __init__(self, mine, theirs, send, recv):
        x, y, c, _ = _place()
        self.copies = []
        for a, (axis, width) in enumerate(SHARDED):
            for m in (1, 2, 3):
                px, py, sp = _other_chip(x, y, m)
                self.copies.append(_remote(_part(mine[a], sp, axis, width), theirs[a].at[m - 1], send, recv,
                                           3 * a + m - 1, (px, py, c)))

    def start(self):
        for cp in self.copies:
            cp.start()

    def wait(self):
        for cp in self.copies:
            cp.wait()

    @staticmethod
    def landing(ps):
        shapes = []
        for p, (axis, width) in zip(ps, SHARDED):
            shape = [3] + list(p.shape)
            shape[axis] = width
            shapes.append(jax.ShapeDtypeStruct(tuple(shape), p.dtype))
        return shapes

    @staticmethod
    def semaphores(n):
        return [pltpu.SemaphoreType.DMA((3 * n,)), pltpu.SemaphoreType.DMA((3 * n,))]


def _shard_sum(p, landed, pos, axis, width, name):
    _, Rs, Cs = landed.shape
    tr = min(256, Rs)
    p_spec = _part_spec(tr, Rs, Cs, axis, width, lambda a: 0)

    def body(pos_ref, p_ref, l_ref, o_ref):
        o_ref[...] = ((p_ref[...] + l_ref[0].astype(F32)) + l_ref[1].astype(F32)) + l_ref[2].astype(F32)

    return pl.pallas_call(
        body, name=name,
        grid_spec=pltpu.PrefetchScalarGridSpec(
            num_scalar_prefetch=1, grid=(Rs // tr,),
            in_specs=[p_spec, pl.BlockSpec((3, tr, Cs), lambda i, pos: (0, i, 0))],
            out_specs=pl.BlockSpec((None, tr, Cs), lambda i, pos: (pos[0], i, 0))),
        out_shape=jax.ShapeDtypeStruct((2, Rs, Cs), F32),
        compiler_params=_params(),
    )(pos, p[None], landed)


def _final_exchange(gs, packed):
    n = len(gs)
    rows = packed.shape[0]
    half = rows // 2
    assert half % 8 == 0

    def body(*refs):
        small_ref = refs[n]
        outs, total_ref = refs[n + 1:2 * n + 1], refs[2 * n + 1]
        sib_ref, chips_ref, done_ref, send, recv, small_send, small_recv = refs[2 * n + 2:]
        x, y, c, s = _place()
        me, sibling = (x, y, c), (x, y, 1 - c)
        copies = [_remote(outs[a].at[c], outs[a].at[c], send, recv, a, sibling) for a in range(n)]
        for cp in copies:
            cp.start()

        def small(src, dst, k, to):
            return _remote(src, dst, small_send, small_recv, k, to)

        mine = small_ref.at[pl.ds(pl.multiple_of(c * half, 8), half)]
        theirs = small_ref.at[pl.ds(pl.multiple_of((1 - c) * half, 8), half)]
        to_sibling = small(theirs, sib_ref, 0, sibling)
        to_sibling.start()
        to_sibling.wait()
        chips_ref[s] = mine[...] + sib_ref[...]
        to_chips = []
        for m in (1, 2, 3):
            px, py, _ = _other_chip(x, y, m)
            to_chips.append(small(chips_ref.at[s], chips_ref.at[s], m, (px, py, c)))
            to_chips[-1].start()
        for m in (1, 2, 3):
            _, _, sp = _other_chip(x, y, m)
            small(chips_ref.at[sp], chips_ref.at[sp], m, me).wait_recv()
        done_ref[c] = ((chips_ref[0] + chips_ref[1]) + chips_ref[2]) + chips_ref[3]
        finished = small(done_ref.at[c], done_ref.at[c], 4, sibling)
        finished.start()
        small(done_ref.at[1 - c], done_ref.at[1 - c], 4, me).wait_recv()
        total_ref[:half] = done_ref[0]
        total_ref[half:] = done_ref[1]
        for cp in to_chips + [finished]:
            cp.wait_send()
        for a, cp in enumerate(copies):
            cp.wait_send()
            _remote(outs[a].at[1 - c], outs[a].at[1 - c], send, recv, a, me).wait_recv()

    vmem = pl.BlockSpec(memory_space=pltpu.VMEM)
    res = pl.pallas_call(
        body, name="final_exchange",
        in_specs=[ANY] * n + [vmem], out_specs=[ANY] * n + [vmem],
        out_shape=[jax.ShapeDtypeStruct(g.shape, g.dtype) for g in gs]
        + [jax.ShapeDtypeStruct(packed.shape, packed.dtype)],
        input_output_aliases={a: a for a in range(n)},
        scratch_shapes=[pltpu.VMEM((half, LANES), F32), pltpu.VMEM((N_CHIPS, half, LANES), F32),
                        pltpu.VMEM((2, half, LANES), F32),
                        pltpu.SemaphoreType.DMA((n,)), pltpu.SemaphoreType.DMA((n,)),
                        pltpu.SemaphoreType.DMA((5,)), pltpu.SemaphoreType.DMA((5,))],
        compiler_params=_params(),
    )(*gs, packed)
    return res[:n], res[n]


def _adamw(w, g, m, v, name):
    shape = w.shape
    C = shape[-1]
    flat = [t.reshape(-1, C) for t in (w, g, m, v)]
    R = flat[0].shape[0]
    tr = max(t for t in range(8, R + 1, 8) if R % t == 0 and t * C <= 384 * 1024)

    def body(w_ref, g_ref, m_ref, v_ref, d_ref, nm_ref, nv_ref):
        _adamw_update(w_ref, g_ref, m_ref, v_ref, d_ref, nm_ref, nv_ref)

    blk = pl.BlockSpec((tr, C), lambda i: (i, 0))
    out = jax.ShapeDtypeStruct((R, C), F32)
    res = pl.pallas_call(
        body, name=name, grid=(R // tr,),
        in_specs=[blk] * 4, out_specs=[blk] * 3, out_shape=[out] * 3,
        compiler_params=_params(),
    )(*flat)
    return [t.reshape(shape) for t in res]


def _adamw_update(w_ref, g_ref, m_ref, v_ref, d_ref, nm_ref, nv_ref):
    gv = g_ref[...]
    nm = ADAM_B1 * m_ref[...] + (1.0 - ADAM_B1) * gv
    nv = ADAM_B2 * v_ref[...] + (1.0 - ADAM_B2) * (gv * gv)
    m_hat = nm / (1.0 - ADAM_B1 ** ADAM_STEP)
    v_hat = nv / (1.0 - ADAM_B2 ** ADAM_STEP)
    d_ref[...] = -ADAM_LR * (m_hat / (jnp.sqrt(v_hat) + ADAM_EPS) + ADAM_WD * w_ref[...])
    nm_ref[...] = nm
    nv_ref[...] = nv


def _adamw_small(ws, gs, ms, vs):
    n = len(ws)
    flat = lambda ts: [t.reshape(-1, t.shape[-1]) for t in ts]

    def body(*refs):
        for a in range(n):
            _adamw_update(*[refs[k * n + a] for k in range(7)])

    vmem = pl.BlockSpec(memory_space=pltpu.VMEM)
    shapes = [jax.ShapeDtypeStruct(w.shape, F32) for w in flat(ws)]
    res = pl.pallas_call(
        body, name="adamw_small",
        in_specs=[vmem] * (4 * n), out_specs=[vmem] * (3 * n), out_shape=shapes * 3,
        compiler_params=_params(),
    )(*flat(ws), *flat(gs), *flat(ms), *flat(vs))
    return [[r.reshape(w.shape) for r, w in zip(res[k * n:(k + 1) * n], ws)] for k in range(3)]


SMALL_SHAPES = ((2, 1024), (2, 2048), (2, 4, 128, 128), (2, 512), (1024,))


def _pack_small(parts):
    return jnp.concatenate([p.reshape(-1, LANES) for p in parts], axis=0)


def _unpack_small(packed):
    out, row = [], 0
    for shape in SMALL_SHAPES:
        n = 1
        for d in shape:
            n *= d
        out.append(packed[row:row + n // LANES].reshape(shape))
        row += n // LANES
    return out


def kernel(x, norm_g, w_in, b_gate, pool_w, pool_scale, w_pool_up, w_attn_up, w_out, final_g, loss_target, m_norm_g, m_w_in, m_b_gate, m_pool_w, m_pool_scale, m_w_pool_up, m_w_attn_up, m_w_out, m_final_g, v_norm_g, v_w_in, v_b_gate, v_pool_w, v_pool_scale, v_w_pool_up, v_w_attn_up, v_w_out, v_final_g):
    _, _, c, s = _place()
    pos = jnp.stack([c, s]).astype(jnp.int32)
    names = SHARDED_NAMES

    weights = _gather_first([_cast_into_place(w, pos, axis, width, f"cast_{n}")
                             for w, (axis, width), n in zip((w_in, w_pool_up, w_attn_up, w_out), SHARDED, names)])
    loss_part, dx, d_final_g, small, pair, landed = _local_step(x[0], loss_target[0], norm_g, b_gate, pool_w,
                                                                pool_scale, final_g, weights, pos)
    mine = [_shard_sum(p, l, pos, axis, width, f"grad_shard_sum_{n}")
            for p, l, (axis, width), n in zip(pair, landed, SHARDED, names)]
    (g_in, g_pu, g_au, g_out), summed = _final_exchange(
        mine, _pack_small(small + [d_final_g, jnp.broadcast_to(loss_part, (16, LANES))]))
    g_small = _unpack_small(summed)
    loss = summed[-16, 0]
    d_small, nm_small, nv_small = _adamw_small([norm_g, b_gate, pool_w, pool_scale, final_g], g_small,
                                               [m_norm_g, m_b_gate, m_pool_w, m_pool_scale, m_final_g],
                                               [v_norm_g, v_b_gate, v_pool_w, v_pool_scale, v_final_g])
    upd_in = _adamw(w_in, g_in, m_w_in, v_w_in, "adamw_w_in")
    upd_pu = _adamw(w_pool_up, g_pu, m_w_pool_up, v_w_pool_up, "adamw_w_pool_up")
    upd_au = _adamw(w_attn_up, g_au, m_w_attn_up, v_w_attn_up, "adamw_w_attn_up")
    upd_out = _adamw(w_out, g_out, m_w_out, v_w_out, "adamw_w_out")

    def ordered(sm, k):
        big = (upd_in[k], upd_pu[k], upd_au[k], upd_out[k]) if k is not None else (g_in, g_pu, g_au, g_out)
        return [sm[0], big[0], sm[1], sm[2], sm[3], big[1], big[2], big[3], sm[4]]

    return (loss, dx[None], *ordered(g_small, None), *ordered(d_small, 0), *ordered(nm_small, 1),
            *ordered(nv_small, 2))
```

```python
import jax
import jax.numpy as jnp
import numpy as np
from jax import lax
from jax.experimental import pallas as pl
from jax.experimental.pallas import tpu as pltpu

F32 = jnp.float32
BF16 = jnp.bfloat16
MESH = pl.DeviceIdType.MESH

D_MODEL = 1024
POOL_WIDTH = 512
POOL_WINDOWS = (2, 4, 8, 16)
POOL_GROUP = 128
POOL_HALO = 16
ATTN_WIDTH = 512
HEAD_DIM = 64
HEAD_PAIRS = 4
IN_WIDTH = 5120
N_CHIPS = 4
RMS_EPS = 1e-6
C_U, C_ZP, C_Q, C_K, C_V, C_ZA, C_GL = 0, 512, 1024, 1536, 2048, 2560, 3072

ADAM_LR, ADAM_B1, ADAM_B2, ADAM_EPS, ADAM_WD, ADAM_STEP = 0.001, 0.9, 0.999, 1e-08, 0.01, 10

LANES = 128
ATTN_BLOCK = 256
QUERY_BLOCKS = 4
ROW_TILE = 256
PROJ_ROW_TILE = 512
VMEM_LIMIT = 56 * 1024 * 1024


def _params(**kw):
    return pltpu.CompilerParams(vmem_limit_bytes=VMEM_LIMIT, **kw)


def _nt(a, b):
    return lax.dot_general(a, b, (((1,), (1,)), ((), ())), preferred_element_type=F32)


def _tn(a, b):
    return lax.dot_general(a, b, (((0,), (0,)), ((), ())), preferred_element_type=F32)


def _nn(a, b):
    return jnp.dot(a, b, preferred_element_type=F32)


def _sigmoid(z):
    return 1.0 / (1.0 + jnp.exp(-z))


def _rms_inproj(x, g, w_in, layer, gather=()):
    S = x.shape[0]
    tm = min(PROJ_ROW_TILE, S)
    n_tiles = S // tm
    n_g = len(gather)

    def body(*refs):
        x_ref, g_ref = refs[:2]
        if n_g:
            (u_ref, zp_ref, q_ref, k_ref, v_ref, za_ref, gl_ref, h_ref) = refs[2 + n_g:10 + n_g]
            fulls = refs[10 + n_g:10 + 2 * n_g]
            w_ref, load_sem, send, recv = refs[10 + 2 * n_g:]
            later = _Gather(LATER_PIECES, fulls, send, recv)

            @pl.when(pl.program_id(0) == 0)
            def _():
                load = pltpu.make_async_copy(fulls[0].at[layer], w_ref, load_sem)
                load.start()
                later.start()
                load.wait()

            pl.when(pl.program_id(0) == (3 * n_tiles) // 4)(later.pass_on)
        else:
            w_ref, u_ref, zp_ref, q_ref, k_ref, v_ref, za_ref, gl_ref, h_ref = refs[2:]
        xv = x_ref[...]
        r = lax.rsqrt(jnp.mean(xv * xv, axis=-1, keepdims=True) + RMS_EPS)
        h = ((xv * r) * g_ref[...]).astype(BF16)
        h_ref[...] = h

        def mm(c0, n):
            return _nn(h, w_ref[:, c0:c0 + n])

        u_ref[...] = mm(C_U, 512)
        zp_ref[...] = mm(C_ZP, 512).astype(BF16)
        q_ref[...] = (mm(C_Q, 512) * 0.125).astype(BF16)
        k_ref[...] = mm(C_K, 512).astype(BF16)
        v_ref[...] = mm(C_V, 512).astype(BF16)
        za_ref[...] = mm(C_ZA, 512).astype(BF16)
        for c in range(4):
            gl_ref[:, c * 512:(c + 1) * 512] = mm(C_GL + c * 512, 512).astype(BF16)
        if n_g:
            pl.when(pl.program_id(0) == n_tiles - 1)(later.finish)

    row = lambda n: pl.BlockSpec((tm, n), lambda i: (i, 0))
    sd = lambda n, dt: jax.ShapeDtypeStruct((S, n), dt)
    any_space = pl.BlockSpec(memory_space=pl.ANY)
    weights = [any_space] * n_g if n_g else [_layer_weight_spec(D_MODEL, IN_WIDTH, layer)]
    res = pl.pallas_call(
        body, name=f"rms_inproj_l{layer}", grid=(n_tiles,),
        in_specs=[row(D_MODEL), pl.BlockSpec((1, D_MODEL), lambda i: (0, 0))] + weights,
        out_specs=[row(512), row(512), row(512), row(512), row(512), row(512), row(2048), row(D_MODEL)]
        + [any_space] * n_g,
        out_shape=[sd(512, F32), sd(512, BF16), sd(512, BF16), sd(512, BF16), sd(512, BF16), sd(512, BF16),
                   sd(2048, BF16), sd(D_MODEL, BF16)] + [jax.ShapeDtypeStruct(f.shape, f.dtype) for f in gather],
        input_output_aliases={2 + a: 8 + a for a in range(n_g)},
        scratch_shapes=([pltpu.VMEM((D_MODEL, IN_WIDTH), BF16), pltpu.SemaphoreType.DMA(())]
                        + _Gather.semaphores(LATER_PIECES)) if n_g else [],
        compiler_params=_params(),
    )(x, g, *(gather if n_g else (w_in,)))
    return res[:8], res[8:]


def _tri(n, strict_lower):
    r = lax.broadcasted_iota(jnp.int32, (n, n), 0)
    c = lax.broadcasted_iota(jnp.int32, (n, n), 1)
    return jnp.where(r > c if strict_lower else r < c, 1.0, 0.0).astype(BF16)


def _split_dot(x, m):
    hi = x.astype(BF16)
    lo = (x - hi.astype(F32)).astype(BF16)
    return _nn(hi, m) + _nn(lo, m)


def _log_terms(z):
    lg = jnp.log(1.0 + jnp.exp(-jnp.abs(z)))
    a = jnp.minimum(z, 0.0) - lg
    return a, a - z


EXHAUSTED = -104.0
UNREACHED = -1e30


class _HeadPair:
    def __init__(self, T):
        self.T = T
        self.first = lax.broadcasted_iota(jnp.int32, (T, LANES), 1) < HEAD_DIM
        self.lane = lax.broadcasted_iota(jnp.int32, (2 * T, LANES), 1)
        row = lax.broadcasted_iota(jnp.int32, (2 * T, T), 0)
        row = jnp.where(row >= T, row - T, row)
        self.causal = row > lax.broadcasted_iota(jnp.int32, (2 * T, T), 1)
        self.below = _tri(T, True)

    def stack(self, x2):
        return jnp.concatenate([jnp.where(self.first, x2, 0), jnp.where(self.first, 0, x2)], axis=0).astype(BF16)

    def unstack(self, x):
        return jnp.where(self.first, x[:self.T], x[self.T:])

    def keys(self, ref, blocks):
        T = self.T
        return jnp.concatenate([ref[pl.ds(pl.multiple_of(j * T, T), T), :] for j, _ in blocks], axis=0)

    def log_terms(self, z, blocks):
        T = self.T
        a_all, l_all = _log_terms(z)
        a = [a_all[:, b * T:(b + 1) * T] for b in range(len(blocks))]
        l1m = [l_all[:, b * T:(b + 1) * T] for b in range(len(blocks))]
        return a, [jnp.where(self.causal, l, 0.0) if diagonal else l for l, (_, diagonal) in zip(l1m, blocks)]

    def later_sums(self, l1m):
        later = _split_dot(jnp.concatenate(l1m, axis=0), self.below)
        return [later[2 * self.T * b:2 * self.T * (b + 1)] for b in range(len(l1m))]


def _sections(x, n):
    return x.reshape(n, x.shape[0] // n, x.shape[1])


def _attn_fwd(q, k, v, layer, gather=()):
    S = q.shape[0]
    T = min(ATTN_BLOCK, S)
    nq = S // T
    jobs = min(QUERY_BLOCKS, nq)
    assert nq <= LANES and nq % jobs == 0
    per_job = nq // jobs

    n_g = len(gather)

    def body(q_ref, k_ref, v_ref, *rest):
        o_ref, c_ref = rest[n_g:n_g + 2]
        i = pl.program_id(1)
        if n_g:
            last = _Gather(LAST_PIECES, rest[n_g + 2:2 * n_g + 2], *rest[2 * n_g + 2:])
            pl.when(jnp.logical_and(pl.program_id(0) == 0, i == 0))(last.start)
        pair = _HeadPair(T)
        qs = [pair.stack(q_ref[n]) for n in range(jobs)]
        diag = [i + n * per_job for n in range(jobs)]

        def sweep(jobs):
            kv = [(pair.keys(k_ref, bl), pair.keys(v_ref, bl)) for _, bl, _ in jobs]
            zs = [_nt(qs[n], kcat) for (n, _, _), (kcat, _) in zip(jobs, kv)]
            terms = [pair.log_terms(z, bl) for (_, bl, _), z in zip(jobs, zs)]
            laters = [pair.later_sums(l1m) for _, l1m in terms]
            weights = []
            for (_, bl, (acc, run, saved)), (a, l1m), later in zip(jobs, terms, laters):
                ws = []
                for b, (j, diagonal) in enumerate(bl):
                    saved = jnp.where(pair.lane == j, run, saved)
                    w = jnp.exp(a[b] + later[b] + run)
                    ws.append(jnp.where(pair.causal, w, 0.0) if diagonal else w)
                    run = run + jnp.sum(l1m[b], axis=1, keepdims=True)
                weights.append((jnp.concatenate(ws, axis=1).astype(BF16), acc, run, saved))
            return [(acc + _nn(w, vcat), run, saved) for (w, acc, run, saved), (_, vcat) in zip(weights, kv)]

        def alive(carry):
            return (jnp.max(carry[1]) > EXHAUSTED).astype(jnp.int32)

        def older_blocks(n, carry):
            def older_block(state):
                j, _, c = state
                c = sweep([(n, [(j, False)], c)])[0]
                return j - 1, alive(c), c

            return lax.while_loop(lambda st: jnp.logical_and(st[0] >= 0, st[1] > 0), older_block,
                                  (diag[n] - 2, alive(carry), carry))[2]

        def run(first_blocks):
            init = (jnp.zeros((2 * T, LANES), F32), jnp.zeros((2 * T, 1), F32),
                    jnp.full((2 * T, LANES), UNREACHED, F32))
            carries = sweep([(n, first_blocks[n], init) for n in range(jobs)])
            for n in range(jobs):
                acc, _, saved = older_blocks(n, carries[n])
                o_ref[n] = pair.unstack(acc).astype(BF16)
                c_ref[n, :, :LANES] = saved[:T]
                c_ref[n, :, LANES:] = saved[T:]

        with_previous = lambda d: [(d, True), (d - 1, False)]

        @pl.when(i == 0)
        def _():
            run([[(diag[0], True)]] + [with_previous(d) for d in diag[1:]])

        @pl.when(i > 0)
        def _():
            run([with_previous(d) for d in diag])

        if n_g:
            pl.when(jnp.logical_and(pl.program_id(0) == HEAD_PAIRS // 2, i == 0))(last.pass_on)
            pl.when(jnp.logical_and(pl.program_id(0) == HEAD_PAIRS - 1, i == per_job - 1))(last.finish)

    blk = lambda n: pl.BlockSpec((jobs, T, n), lambda p, i: (0, i, p))
    full = pl.BlockSpec((S, LANES), lambda p, i: (0, p))
    any_space = pl.BlockSpec(memory_space=pl.ANY)
    res = pl.pallas_call(
        body, name=f"attn_fwd_l{layer}", grid=(HEAD_PAIRS, per_job),
        in_specs=[blk(LANES), full, full] + [any_space] * n_g,
        out_specs=[blk(LANES), blk(2 * LANES)] + [any_space] * n_g,
        out_shape=[jax.ShapeDtypeStruct((jobs, S // jobs, ATTN_WIDTH), BF16),
                   jax.ShapeDtypeStruct((jobs, S // jobs, 8 * LANES), F32)]
        + [jax.ShapeDtypeStruct(f.shape, f.dtype) for f in gather],
        input_output_aliases={3 + a: 2 + a for a in range(n_g)},
        scratch_shapes=_Gather.semaphores(LAST_PIECES) if n_g else [],
        compiler_params=_params(),
    )(_sections(q, jobs), k, v, *gather)
    return res[0].reshape(S, ATTN_WIDTH), res[1].reshape(S, 8 * LANES), res[2:]


def _attn_bwd(q, k, v, saved, do, layer):
    S = q.shape[0]
    T = min(ATTN_BLOCK, S)
    nq = S // T
    jobs = min(QUERY_BLOCKS, nq)
    per_job = nq // jobs

    def body(q_ref, k_ref, v_ref, c_ref, do_ref, dq_ref, dk_ref, dv_ref):
        i = pl.program_id(1)

        @pl.when(i == 0)
        def _():
            dk_ref[...] = jnp.zeros_like(dk_ref)
            dv_ref[...] = jnp.zeros_like(dv_ref)

        pair = _HeadPair(T)
        diag = [i + n * per_job for n in range(jobs)]
        qs = [pair.stack(q_ref[n]) for n in range(jobs)]
        dos = [pair.stack(do_ref[n].astype(BF16)) for n in range(jobs)]
        saved = [jnp.concatenate([c_ref[n, :, :LANES], c_ref[n, :, LANES:]], axis=0) for n in range(jobs)]
        before = _tri(T, False)

        def sweep(jobs):
            kv = [(pair.keys(k_ref, bl), pair.keys(v_ref, bl)) for _, bl, _ in jobs]
            zs = [_nt(qs[n], kcat) for (n, _, _), (kcat, _) in zip(jobs, kv)]
            gs = [_nt(dos[n], vcat) for (n, _, _), (_, vcat) in zip(jobs, kv)]
            terms = [pair.log_terms(z, bl) for (_, bl, _), z in zip(jobs, zs)]
            laters = [pair.later_sums(l1m) for _, l1m in terms]
            ws, es = [], []
            for (n, bl, _), (a, _), later, g in zip(jobs, terms, laters, gs):
                w_job, e_job = [], []
                for b, (j, diagonal) in enumerate(bl):
                    run = jnp.sum(jnp.where(pair.lane == j, saved[n], 0.0), axis=1, keepdims=True)
                    w = jnp.exp(a[b] + later[b] + run)
                    w_job.append(jnp.where(pair.causal, w, 0.0) if diagonal else w)
                    e_job.append(w_job[b] * g[:, b * T:(b + 1) * T])
                ws.append(w_job)
                es.append(e_job)
            prefixes = [_nn(jnp.concatenate(e_job, axis=0).astype(BF16), before) for e_job in es]
            dzs, olders = [], []
            for (_, bl, (_, older)), (a, _), e_job, prefix in zip(jobs, terms, es, prefixes):
                dz_job = []
                for b, (j, diagonal) in enumerate(bl):
                    dz = e_job[b] - jnp.exp(a[b]) * (e_job[b] + (prefix[2 * T * b:2 * T * (b + 1)] + older))
                    dz_job.append(jnp.where(pair.causal, dz, 0.0) if diagonal else dz)
                    older = older + jnp.sum(e_job[b], axis=1, keepdims=True)
                dzs.append(jnp.concatenate(dz_job, axis=1).astype(BF16))
                olders.append(older)
            out = []
            for (n, bl, (dq, _)), dz, w_job, older, (kcat, _) in zip(jobs, dzs, ws, olders, kv):
                dk = _tn(dz, qs[n])
                dv = _tn(jnp.concatenate(w_job, axis=1).astype(BF16), dos[n])
                for b, (j, _) in enumerate(bl):
                    rows = pl.ds(pl.multiple_of(j * T, T), T)
                    dk_ref[rows, :] += dk[b * T:(b + 1) * T]
                    dv_ref[rows, :] += dv[b * T:(b + 1) * T]
                out.append((dq + _nn(dz, kcat), older))
            return out

        def older_blocks(n):
            col_max = jnp.max(saved[n], axis=0, keepdims=True)
            lane_row = lax.broadcasted_iota(jnp.int32, (1, LANES), 1)
            reached = jnp.sum(jnp.where(jnp.logical_and(col_max > EXHAUSTED, lane_row < diag[n]), 1, 0))
            init = (jnp.zeros((2 * T, LANES), F32), jnp.zeros((2 * T, 1), F32))
            return lax.fori_loop(diag[n] - reached, diag[n] - 1, lambda j, c: sweep([(n, [(j, False)], c)])[0], init)

        def run(last_blocks):
            carries = sweep([(n, last_blocks[n], older_blocks(n)) for n in range(jobs)])
            for n in range(jobs):
                dq_ref[n] = (pair.unstack(carries[n][0]) * 0.125).astype(BF16)

        with_previous = lambda d: [(d - 1, False), (d, True)]

        @pl.when(i == 0)
        def _():
            run([[(diag[0], True)]] + [with_previous(d) for d in diag[1:]])

        @pl.when(i > 0)
        def _():
            run([with_previous(d) for d in diag])

    blk = lambda n: pl.BlockSpec((jobs, T, n), lambda p, i: (0, i, p))
    full = pl.BlockSpec((S, LANES), lambda p, i: (0, p))
    out = jax.ShapeDtypeStruct((S, ATTN_WIDTH), F32)
    dq, dk, dv = pl.pallas_call(
        body, name=f"attn_bwd_l{layer}", grid=(HEAD_PAIRS, per_job),
        in_specs=[blk(LANES), full, full, blk(2 * LANES), blk(LANES)],
        out_specs=[blk(LANES), full, full],
        out_shape=[jax.ShapeDtypeStruct((jobs, S // jobs, ATTN_WIDTH), BF16), out, out],
        compiler_params=_params(),
    )(_sections(q, jobs), k, v, _sections(saved, jobs), _sections(do, jobs))
    return dq.reshape(S, ATTN_WIDTH), dk, dv


def _pool_counts(row0, tm):
    pos = row0 + lax.broadcasted_iota(jnp.int32, (tm, 1), 0)
    return [1.0 / jnp.minimum(pos + 1, w).astype(F32) for w in POOL_WINDOWS]


def _window_bands(tm, backward):
    t = np.arange(tm)[:, None]
    c = np.arange(tm)[None, :]
    off = c - t if backward else t - c
    main = np.stack([(off >= 0) & (off < w) for w in POOL_WINDOWS])
    r = np.arange(POOL_HALO)[:, None]
    h = np.arange(POOL_HALO)[None, :]
    off = h - r + POOL_HALO if backward else r - h + POOL_HALO
    edge = np.concatenate([(off < w) for w in POOL_WINDOWS])
    return jnp.asarray(main, BF16), jnp.asarray(edge, BF16)


def _window_sums(tile, beside, main_ref, edge_ref, backward):
    tm = tile.shape[0]
    tb = tile.astype(BF16)
    edge = _nn(edge_ref[...], beside.astype(BF16))
    sums = []
    for g in range(len(POOL_WINDOWS)):
        cols = slice(g * POOL_GROUP, (g + 1) * POOL_GROUP)
        tot = _nn(main_ref[g], tb[:, cols])
        extra = edge[g * POOL_HALO:(g + 1) * POOL_HALO, cols]
        if backward:
            sums.append(jnp.concatenate([tot[:tm - POOL_HALO], tot[tm - POOL_HALO:] + extra], axis=0))
        else:
            sums.append(jnp.concatenate([tot[:POOL_HALO] + extra, tot[POOL_HALO:]], axis=0))
    return sums


def _post_forward(u, history, bands, inv_cnt, zp, o, za, gl, bg, pw_ref, scale, wpu_ref, wau_ref):
    pooled, mixed = [], []
    for g, tot in enumerate(_window_sums(u, history, *bands, False)):
        pg = (tot * inv_cnt[g] - u[:, g * POOL_GROUP:(g + 1) * POOL_GROUP]).astype(BF16)
        pooled.append(pg)
        mixed.append(_nn(pg, pw_ref[g].astype(BF16)))
    pooled = jnp.concatenate(pooled, axis=1)
    mixed = jnp.concatenate(mixed, axis=1)
    zp, za, o = zp.astype(F32), za.astype(F32), o.astype(F32)
    sp = _sigmoid(zp)
    sa = _sigmoid(za)
    y_pool = (mixed * scale) * (zp * sp)
    y_attn = o * (za * sa)
    gate = _sigmoid(gl + bg)
    g0, g1 = gate[:, :D_MODEL], gate[:, D_MODEL:]
    up_p = _nn(y_pool.astype(BF16), wpu_ref[...])
    up_a = _nn(y_attn.astype(BF16), wau_ref[...])
    merged = g0 * up_p + g1 * up_a
    return pooled, mixed, sp, sa, y_pool, y_attn, g0, g1, up_p, up_a, merged


def _row_specs(tm, rev, n_tiles):
    tile_of = (lambda i: n_tiles - 1 - i) if rev else (lambda i: i)
    row = lambda n: pl.BlockSpec((tm, n), lambda i: (tile_of(i), 0))
    halo = pl.BlockSpec((POOL_HALO, POOL_WIDTH),
                        lambda i: (jnp.maximum(tile_of(i) * (tm // POOL_HALO) - 1, 0), 0))
    const = lambda shape: pl.BlockSpec(shape, lambda i: (0,) * len(shape))
    return tile_of, row, halo, const


def _layer_weight_spec(rows, cols, layer):
    return pl.BlockSpec((None, rows, cols), lambda i: (layer, 0, 0), pipeline_mode=pl.Buffered(1))


def _post_fwd(x, u, zp, o, za, gl, bg, pw, scale, wpu, wau, wout, layer, head=()):
    S = x.shape[0]
    tm = min(ROW_TILE, S)
    n_tiles = S // tm
    tile_of, row, halo, const = _row_specs(tm, False, n_tiles)

    def body(x_ref, u_ref, uh_ref, main_ref, edge_ref, zp_ref, o_ref, za_ref, gl_ref, bg_ref, pw_ref, sc_ref, wpu_ref,
             wau_ref, wout_ref, *rest):
        i = pl.program_id(0)
        vals = _post_forward(u_ref[...], jnp.where(i == 0, 0.0, uh_ref[...]), (main_ref, edge_ref),
                             _pool_counts(i * tm, tm), zp_ref[...], o_ref[...], za_ref[...], gl_ref[...], bg_ref[...],
                             pw_ref, sc_ref[...], wpu_ref, wau_ref)
        xv = x_ref[...] + _nn(vals[-1].astype(BF16), wout_ref[...])
        if not head:
            rest[0][...] = xv
            return
        gf_ref, t_ref, loss_ref, dx_ref, dg_ref = rest

        @pl.when(i == 0)
        def _():
            loss_ref[...] = jnp.zeros_like(loss_ref)
            dg_ref[...] = jnp.zeros_like(dg_ref)

        r = lax.rsqrt(jnp.mean(xv * xv, axis=-1, keepdims=True) + RMS_EPS)
        diff = (xv * r) * gf_ref[...] - t_ref[...]
        per_row = jnp.mean(diff * diff, axis=-1, keepdims=True)
        loss_ref[...] += 0.5 * jnp.sum(per_row, axis=0, keepdims=True)
        dx, dg_rows = _rms_backward(diff * (1.0 / D_MODEL), xv, r, gf_ref[...])
        dx_ref[...] = dx
        dg_ref[...] += jnp.sum(dg_rows, axis=0, keepdims=True)

    out = jax.ShapeDtypeStruct((S, D_MODEL), F32)
    return pl.pallas_call(
        body, name=f"post_fwd_l{layer}", grid=(n_tiles,),
        in_specs=[row(D_MODEL), row(512), halo, const((4, tm, tm)), const((4 * POOL_HALO, POOL_HALO)), row(512),
                  row(512), row(512), row(2048), const((1, 2048)), const((4, POOL_GROUP, POOL_GROUP)),
                  const((1, POOL_WIDTH)),
                  _layer_weight_spec(POOL_WIDTH, D_MODEL, layer), _layer_weight_spec(ATTN_WIDTH, D_MODEL, layer),
                  _layer_weight_spec(D_MODEL, D_MODEL, layer)] + ([const((1, D_MODEL)), row(D_MODEL)] if head else []),
        out_specs=[const((1, LANES)), row(D_MODEL), const((1, D_MODEL))] if head else row(D_MODEL),
        out_shape=[jax.ShapeDtypeStruct((1, LANES), F32), out, jax.ShapeDtypeStruct((1, D_MODEL), F32)] if head else out,
        compiler_params=_params(),
    )(x, u, u, *_window_bands(tm, False), zp, o, za, gl, bg, pw, scale, wpu, wau, wout, *head)


def _post_bwd(dx, u, zp, o, za, gl, bg, pw, scale, wpu, wau, wout, layer):
    S = dx.shape[0]
    tm = min(ROW_TILE, S)
    n_tiles = S // tm
    tile_of, row, halo, const = _row_specs(tm, True, n_tiles)

    def body(dx_ref, u_ref, uh_ref, main_ref, edge_ref, back_main_ref, back_edge_ref, zp_ref, o_ref, za_ref, gl_ref,
             bg_ref, pw_ref, sc_ref, wpu_ref, wau_ref, wout_ref,
             duz_ref, do_ref, dzg_ref, dsc_ref, dbg_ref,
             merged_ref, dup_ref, dua_ref, yp_ref, ya_ref, pooled_ref, dmixed_ref, nxt_ref):
        step = pl.program_id(0)
        i = tile_of(step)

        @pl.when(step == 0)
        def _():
            dsc_ref[...] = jnp.zeros_like(dsc_ref)
            dbg_ref[...] = jnp.zeros_like(dbg_ref)
            nxt_ref[...] = jnp.zeros_like(nxt_ref)

        inv_cnt = _pool_counts(i * tm, tm)
        zp, za, o = zp_ref[...].astype(F32), za_ref[...].astype(F32), o_ref[...].astype(F32)
        pooled, mixed, sp, sa, y_pool, y_attn, g0, g1, up_p, up_a, merged = _post_forward(
            u_ref[...], jnp.where(i == 0, 0.0, uh_ref[...]), (main_ref, edge_ref), inv_cnt, zp, o, za, gl_ref[...],
            bg_ref[...], pw_ref, sc_ref[...], wpu_ref, wau_ref)
        merged_ref[...] = merged.astype(BF16)
        yp_ref[...] = y_pool.astype(BF16)
        ya_ref[...] = y_attn.astype(BF16)
        pooled_ref[...] = pooled

        dmerged = _nt(dx_ref[...].astype(BF16), wout_ref[...])
        dup = (dmerged * g0).astype(BF16)
        dua = (dmerged * g1).astype(BF16)
        dup_ref[...] = dup
        dua_ref[...] = dua
        dgl0 = (dmerged * up_p) * (g0 * (1.0 - g0))
        dgl1 = (dmerged * up_a) * (g1 * (1.0 - g1))
        dzg_ref[:, ATTN_WIDTH:ATTN_WIDTH + D_MODEL] = dgl0.astype(BF16)
        dzg_ref[:, ATTN_WIDTH + D_MODEL:] = dgl1.astype(BF16)
        dbg_ref[:, :D_MODEL] += jnp.sum(dgl0, axis=0, keepdims=True)
        dbg_ref[:, D_MODEL:] += jnp.sum(dgl1, axis=0, keepdims=True)

        dy_attn = _nt(dua, wau_ref[...])
        do_ref[...] = (dy_attn * (za * sa)).astype(BF16)
        dzg_ref[:, :ATTN_WIDTH] = ((dy_attn * o) * (sa * (1.0 + za * (1.0 - sa)))).astype(BF16)

        dy_pool = _nt(dup, wpu_ref[...])
        ms = mixed * sc_ref[...]
        dms = dy_pool * (zp * sp)
        duz_ref[:, POOL_WIDTH:] = ((dy_pool * ms) * (sp * (1.0 + zp * (1.0 - sp)))).astype(BF16)
        dsc_ref[...] += jnp.sum(dms * mixed, axis=0, keepdims=True)
        dmixed = (dms * sc_ref[...]).astype(BF16)
        dmixed_ref[...] = dmixed
        dpooled = [_nt(dmixed[:, g * POOL_GROUP:(g + 1) * POOL_GROUP], pw_ref[g].astype(BF16)) for g in range(4)]
        scaled = jnp.concatenate([d * inv for d, inv in zip(dpooled, inv_cnt)], axis=1)
        for g, tot in enumerate(_window_sums(scaled, nxt_ref[...], back_main_ref, back_edge_ref, True)):
            duz_ref[:, g * POOL_GROUP:(g + 1) * POOL_GROUP] = (tot - dpooled[g]).astype(BF16)
        nxt_ref[...] = scaled[:POOL_HALO]

    sd = lambda n, dt: jax.ShapeDtypeStruct((S, n), dt)
    bands = [const((4, tm, tm)), const((4 * POOL_HALO, POOL_HALO))]
    return pl.pallas_call(
        body, name=f"post_bwd_l{layer}", grid=(n_tiles,),
        in_specs=[row(D_MODEL), row(512), halo, *bands, *bands, row(512), row(512), row(512), row(2048),
                  const((1, 2048)), const((4, POOL_GROUP, POOL_GROUP)), const((1, POOL_WIDTH)),
                  _layer_weight_spec(POOL_WIDTH, D_MODEL, layer), _layer_weight_spec(ATTN_WIDTH, D_MODEL, layer),
                  _layer_weight_spec(D_MODEL, D_MODEL, layer)],
        out_specs=[row(1024), row(512), row(2560), const((1, POOL_WIDTH)), const((1, 2048)),
                   row(D_MODEL), row(D_MODEL), row(D_MODEL), row(512), row(512), row(512), row(512)],
        out_shape=[sd(1024, BF16), sd(512, BF16), sd(2560, BF16),
                   jax.ShapeDtypeStruct((1, POOL_WIDTH), F32), jax.ShapeDtypeStruct((1, 2048), F32),
                   sd(D_MODEL, BF16), sd(D_MODEL, BF16), sd(D_MODEL, BF16), sd(512, BF16), sd(512, BF16),
                   sd(512, BF16), sd(512, BF16)],
        scratch_shapes=[pltpu.VMEM((POOL_HALO, POOL_WIDTH), F32)],
        compiler_params=_params(),
    )(dx, u, u, *_window_bands(tm, False), *_window_bands(tm, True), zp, o, za, gl, bg, pw, scale, wpu, wau, wout)


def _rms_backward(dh, xv, r, g):
    xhat = xv * r
    dxhat = dh * g
    return r * (dxhat - xhat * jnp.mean(dxhat * xhat, axis=-1, keepdims=True)), dh * xhat


def _inproj_bwd(pieces, w_in, x, g, dx_res, layer, exchange=()):
    S = x.shape[0]
    tm = min(PROJ_ROW_TILE, S)
    cols = [(c0, p.shape[1]) for p, c0 in pieces]

    def body(ins, outs):
        piece_refs = ins[:len(cols)]
        w_ref, x_ref, g_ref, res_ref = ins[len(cols):]
        dx_ref, dg_ref = outs

        @pl.when(pl.program_id(0) == 0)
        def _():
            dg_ref[...] = jnp.zeros_like(dg_ref)

        dh = jnp.zeros((tm, D_MODEL), F32)
        for p_ref, (c0, n) in zip(piece_refs, cols):
            for c in range(0, n, 512):
                dh = dh + _nt(p_ref[:, c:c + 512].astype(BF16), w_ref[:, c0 + c:c0 + c + 512])
        xv = x_ref[...]
        r = lax.rsqrt(jnp.mean(xv * xv, axis=-1, keepdims=True) + RMS_EPS)
        dx, dg_rows = _rms_backward(dh, xv, r, g_ref[...])
        dx_ref[...] = res_ref[...] + dx
        dg_ref[...] += jnp.sum(dg_rows, axis=0, keepdims=True)

    row = lambda n: pl.BlockSpec((tm, n), lambda i: (i, 0))
    vec = pl.BlockSpec((1, D_MODEL), lambda i: (0, 0))
    any_space = pl.BlockSpec(memory_space=pl.ANY)
    n_x = len(exchange)
    grid = (S // tm,)
    res = pl.pallas_call(
        _with_swap(body, grid, len(cols) + 4, 2, n_x, rider=_ChipExchange), name=f"inproj_bwd_l{layer}", grid=grid,
        in_specs=[row(n) for _, n in cols] + [_layer_weight_spec(D_MODEL, IN_WIDTH, layer), row(D_MODEL), vec,
                                              row(D_MODEL)] + [any_space] * n_x,
        out_specs=[row(D_MODEL), vec] + [any_space] * n_x,
        out_shape=[jax.ShapeDtypeStruct((S, D_MODEL), F32), jax.ShapeDtypeStruct((1, D_MODEL), F32)]
        + _ChipExchange.landing(exchange),
        scratch_shapes=_ChipExchange.semaphores(n_x) if n_x else [],
        compiler_params=_params(),
    )(*[p for p, _ in pieces], w_in, x, g, dx_res, *exchange)
    return res[0], res[1], res[2:]


class _SiblingSwap:
    def __init__(self, mine, theirs, send, recv):
        x, y, c, _ = _place()
        self.copies = [_remote(m.at[1 - c], t, send, recv, a, (x, y, 1 - c))
                       for a, (m, t) in enumerate(zip(mine, theirs))]

    def start(self):
        for cp in self.copies:
            cp.start()

    def wait(self):
        for cp in self.copies:
            cp.wait()


def _with_swap(body, grid, n_in, n_out, n_swap, rider=_SiblingSwap):
    if not n_swap:
        return lambda *refs: body(refs[:n_in], refs[n_in:])

    def riding(*refs):
        ins, mine = refs[:n_in], refs[n_in:n_in + n_swap]
        outs, theirs = refs[n_in + n_swap:n_in + n_swap + n_out], refs[n_in + n_swap + n_out:n_in + 2 * n_swap + n_out]
        swap = rider(mine, theirs, *refs[n_in + 2 * n_swap + n_out:])
        step = [pl.program_id(d) for d in range(len(grid))]
        first, last = step[0] == 0, step[0] == grid[0] - 1
        for d in range(1, len(grid)):
            first, last = jnp.logical_and(first, step[d] == 0), jnp.logical_and(last, step[d] == grid[d] - 1)
        pl.when(first)(swap.start)
        body(ins, outs)
        pl.when(last)(swap.wait)

    return riding


def _swap_specs(swap):
    any_space = pl.BlockSpec(memory_space=pl.ANY)
    shapes = [jax.ShapeDtypeStruct(d.shape[1:], d.dtype) for d in swap]
    sems = [pltpu.SemaphoreType.DMA((len(swap),)), pltpu.SemaphoreType.DMA((len(swap),))] if swap else []
    return [any_space] * len(swap), shapes, sems


def _wgrad(a, b, name, layer, into=None, col0=0, n_total=None, swap=()):
    S, M = a.shape
    N = b.shape[1]
    n_total = N if n_total is None else n_total
    tk = min(2048, S)
    tn = max(t for t in range(LANES, min(N, 1280) + 1, LANES) if N % t == 0 and col0 % t == 0)
    grid = (N // tn, S // tk)

    def body(ins, outs):
        prod = _tn(ins[0][...].astype(BF16), ins[1][...].astype(BF16))

        @pl.when(pl.program_id(1) == 0)
        def _():
            outs[0][...] = prod

        @pl.when(pl.program_id(1) > 0)
        def _():
            outs[0][...] += prod

    in_specs = [pl.BlockSpec((tk, M), lambda j, k: (k, 0)), pl.BlockSpec((tk, tn), lambda j, k: (k, j))]
    args = [a, b]
    aliases = {}
    if into is not None:
        in_specs.append(pl.BlockSpec(memory_space=pl.ANY))
        args.append(into)
        aliases = {2: 0}
    swap_specs, swap_shapes, swap_sems = _swap_specs(swap)
    res = pl.pallas_call(
        _with_swap(body, grid, len(args), 1, len(swap)), name=name, grid=grid,
        in_specs=in_specs + swap_specs,
        out_specs=[pl.BlockSpec((None, M, tn), lambda j, k: (layer, 0, col0 // tn + j))] + swap_specs,
        out_shape=[jax.ShapeDtypeStruct((2, M, n_total), F32)] + swap_shapes,
        input_output_aliases=aliases,
        scratch_shapes=swap_sems,
        compiler_params=_params(),
    )(*args, *swap)
    return (res[0], res[1:]) if swap else res[0]


def _pool_wgrad(pooled, dmixed, layer, swap=()):
    S = pooled.shape[0]
    tk = min(8192, S)
    grid = (4, S // tk)

    def body(ins, outs):
        prod = _tn(ins[0][...], ins[1][...])

        @pl.when(pl.program_id(1) == 0)
        def _():
            outs[0][...] = prod

        @pl.when(pl.program_id(1) > 0)
        def _():
            outs[0][...] += prod

    blk = pl.BlockSpec((tk, POOL_GROUP), lambda g, k: (k, g))
    swap_specs, swap_shapes, swap_sems = _swap_specs(swap)
    res = pl.pallas_call(
        _with_swap(body, grid, 2, 1, len(swap)), name=f"pool_wgrad_l{layer}", grid=grid,
        in_specs=[blk, blk] + swap_specs,
        out_specs=[pl.BlockSpec((None, POOL_GROUP, POOL_GROUP), lambda g, k: (g, 0, 0))] + swap_specs,
        out_shape=[jax.ShapeDtypeStruct((4, POOL_GROUP, POOL_GROUP), F32)] + swap_shapes,
        scratch_shapes=swap_sems,
        compiler_params=_params(),
    )(pooled, dmixed, *swap)
    return (res[0], res[1:]) if swap else res[0]


def _local_step(x, target, norm_g, b_gate, pool_w, pool_scale, final_g, weights, pos):
    n_layers = norm_g.shape[0]
    saved = []
    for l in range(n_layers):
        g = norm_g[l][None]
        bg = b_gate[l][None]
        sc = pool_scale[l][None]
        if l == 0:
            (u, zp, q, k, v, za, gl, h), (w_in, w_pu, w_au, w_out) = _rms_inproj(x, g, None, l, gather=weights)
        else:
            (u, zp, q, k, v, za, gl, h), _ = _rms_inproj(x, g, w_in, l)
        if l == 0:
            o, carry, (w_in,) = _attn_fwd(q, k, v, l, gather=(w_in,))
        else:
            o, carry, _ = _attn_fwd(q, k, v, l)
        saved.append((x, g, bg, sc, u, zp, q, k, v, za, gl, h, o, carry))
        if l < n_layers - 1:
            x = _post_fwd(x, u, zp, o, za, gl, bg, pool_w[l], sc, w_pu, w_au, w_out, l)
        else:
            loss, dx, d_final_g = _post_fwd(x, u, zp, o, za, gl, bg, pool_w[l], sc, w_pu, w_au, w_out, l,
                                            head=(final_g[None], target))

    small = [None] * n_layers
    dw_in = dw_out = dw_pu = dw_au = None
    for l in reversed(range(n_layers)):
        x_in, g, bg, sc, u, zp, q, k, v, za, gl, h, o, carry = saved[l]
        (duz, do, dzg, dsc, dbg, merged, dup, dua, y_pool, y_attn, pooled, dmixed) = _post_bwd(
            dx, u, zp, o, za, gl, bg, pool_w[l], sc, w_pu, w_au, w_out, l)
        dq, dk, dv = _attn_bwd(q, k, v, carry, do, l)
        pieces = [(duz, C_U), (dq, C_Q), (dk, C_K), (dv, C_V), (dzg, C_ZA)]
        for p, c0 in pieces:
            dw_in = _wgrad(h, p, f"wgrad_in_l{l}_c{c0}", l, into=dw_in, col0=c0, n_total=IN_WIDTH)
        if l > 0:
            dw_out = _wgrad(merged, dx, f"wgrad_out_l{l}", l, into=dw_out)
        else:
            dw_out, (other_in,) = _wgrad(merged, dx, f"wgrad_out_l{l}", l, into=dw_out, swap=(dw_in,))
        dw_pu = _wgrad(y_pool, dup, f"wgrad_pu_l{l}", l, into=dw_pu)
        dw_au = _wgrad(y_attn, dua, f"wgrad_au_l{l}", l, into=dw_au)
        if l > 0:
            dpw = _pool_wgrad(pooled, dmixed, l)
        else:
            dpw, (other_pu, other_au, other_out) = _pool_wgrad(pooled, dmixed, l, swap=(dw_pu, dw_au, dw_out))
        if l > 0:
            dx, dg, _ = _inproj_bwd(pieces, w_in, x_in, g, dx, l)
        else:
            pair = [_pair_sum(d, t, pos, f"grad_pair_sum_{n}") for d, t, n in
                    zip((dw_in, dw_pu, dw_au, dw_out), (other_in, other_pu, other_au, other_out), SHARDED_NAMES)]
            dx, dg, landed = _inproj_bwd(pieces, w_in, x_in, g, dx, l, exchange=[pb for _, pb in pair])
        small[l] = (dg[0], dbg[0], dpw, dsc[0])
    small = [jnp.stack([small[l][i] for l in range(n_layers)]) for i in range(4)]
    return loss[0, 0], dx, d_final_g[0], small, [p for p, _ in pair], landed


SHARDED = ((2, 1280), (2, 256), (2, 256), (1, 256))
SHARDED_NAMES = ("w_in", "w_pool_up", "w_attn_up", "w_out")
ANY = pl.BlockSpec(memory_space=pl.ANY)


def _part(ref, s, axis, width):
    sl = pl.ds(pl.multiple_of(s * width, width), width)
    return ref.at[:, sl] if axis == 2 else ref.at[sl, :]


def _place():
    x, y, c = lax.axis_index("x"), lax.axis_index("y"), lax.axis_index("c")
    return x, y, c, 2 * x + y


def _other_chip(x, y, m):
    px = 1 - x if m & 2 else x
    py = 1 - y if m & 1 else y
    return px, py, 2 * px + py


def _remote(src, dst, send, recv, k, to):
    return pltpu.make_async_remote_copy(src_ref=src, dst_ref=dst, send_sem=send.at[k], recv_sem=recv.at[k],
                                        device_id=to, device_id_type=MESH)


def _part_spec(tr, rows_s, cols_s, axis, width, lead):
    if axis == 2:
        return pl.BlockSpec((None, tr, width), lambda *a: (lead(a), a[-2], a[-1][1]))
    return pl.BlockSpec((None, tr, cols_s), lambda *a: (lead(a), a[-1][1] * (rows_s // tr) + a[-2], 0))


def _cast_into_place(w, pos, axis, width, name):
    L, Rs, Cs = w.shape
    tr = min(256, Rs)
    shape = [L, Rs, Cs]
    shape[axis] *= N_CHIPS

    def body(pos_ref, w_ref, o_ref):
        o_ref[...] = w_ref[...].astype(BF16)

    return pl.pallas_call(
        body, name=name,
        grid_spec=pltpu.PrefetchScalarGridSpec(
            num_scalar_prefetch=1, grid=(L, Rs // tr),
            in_specs=[pl.BlockSpec((None, tr, Cs), lambda l, i, pos: (l, i, 0))],
            out_specs=_part_spec(tr, Rs, Cs, axis, width, lambda a: a[0])),
        out_shape=jax.ShapeDtypeStruct(tuple(shape), BF16),
        compiler_params=_params(),
    )(pos, w)


def _w_in_half(layer):
    def piece(refs, who, shard):
        rows = pl.ds(pl.multiple_of(who * (D_MODEL // 2), D_MODEL // 2), D_MODEL // 2)
        return refs[0].at[layer, rows, pl.ds(pl.multiple_of(shard * SHARDED[0][1], SHARDED[0][1]), SHARDED[0][1])]
    return piece


def _whole_layer(a):
    def piece(refs, who, shard):
        return _part(refs[a].at[who], shard, *SHARDED[a])
    return piece


FIRST_PIECES = (_w_in_half(0),)
LATER_PIECES = (_whole_layer(1), _whole_layer(2), _whole_layer(3))
LAST_PIECES = (_w_in_half(1),)


class _Gather:
    def __init__(self, pieces, refs, send, recv):
        self.pieces, self.refs, self.send, self.recv = pieces, refs, send, recv
        self.x, self.y, self.c, s = _place()
        self.first = []
        for u, piece in enumerate(pieces):
            for m in (1, 2, 3):
                px, py, _ = _other_chip(self.x, self.y, m)
                own = piece(refs, self.c, s)
                self.first.append(_remote(own, own, send, recv, 3 * u + m - 1, (px, py, self.c)))

    def start(self):
        for cp in self.first:
            cp.start()

    def _landed(self, u, m, who):
        _, _, sp = _other_chip(self.x, self.y, m)
        return self.pieces[u](self.refs, who, sp)

    def _passed(self):
        n = len(self.pieces)
        return [_remote(self._landed(u, m, self.c), self._landed(u, m, self.c), self.send, self.recv,
                        3 * n + 3 * u + m - 1, (self.x, self.y, 1 - self.c)) for m in (1, 2, 3) for u in range(n)]

    def pass_on(self):
        me = (self.x, self.y, self.c)
        passed = iter(self._passed())
        for m in (1, 2, 3):
            for u in range(len(self.pieces)):
                got = self._landed(u, m, self.c)
                _remote(got, got, self.send, self.recv, 3 * u + m - 1, me).wait_recv()
                next(passed).start()

    def finish(self):
        n = len(self.pieces)
        for m in (1, 2, 3):
            for u in range(n):
                got = self._landed(u, m, 1 - self.c)
                _remote(got, got, self.send, self.recv, 3 * n + 3 * u + m - 1, (self.x, self.y, self.c)).wait_recv()
        for cp in self.first + self._passed():
            cp.wait_send()

    @staticmethod
    def semaphores(pieces):
        return [pltpu.SemaphoreType.DMA((6 * len(pieces),)), pltpu.SemaphoreType.DMA((6 * len(pieces),))]


def _gather_first(fulls):
    n = len(fulls)

    def body(*refs):
        gather = _Gather(FIRST_PIECES, refs[n:2 * n], *refs[2 * n:])
        gather.start()
        gather.pass_on()
        gather.finish()

    return pl.pallas_call(
        body, name="gather_first",
        in_specs=[ANY] * n, out_specs=[ANY] * n,
        out_shape=[jax.ShapeDtypeStruct(f.shape, f.dtype) for f in fulls],
        input_output_aliases={a: a for a in range(n)},
        scratch_shapes=_Gather.semaphores(FIRST_PIECES),
    )(*fulls)


def _pair_sum(dw, other, pos, name):
    _, R, C = dw.shape
    tr = 128 if C > 1024 else 256

    def body(pos_ref, a_ref, b_ref, o_ref, ob_ref):
        tot = a_ref[...] + b_ref[...]
        o_ref[...] = tot
        ob_ref[...] = tot.astype(BF16)

    blk = pl.BlockSpec((tr, C), lambda i, pos: (i, 0))
    return pl.pallas_call(
        body, name=name,
        grid_spec=pltpu.PrefetchScalarGridSpec(
            num_scalar_prefetch=1, grid=(R // tr,),
            in_specs=[pl.BlockSpec((None, tr, C), lambda i, pos: (pos[0], i, 0)), blk],
            out_specs=[blk, blk]),
        out_shape=[jax.ShapeDtypeStruct((R, C), F32), jax.ShapeDtypeStruct((R, C), BF16)],
        compiler_params=_params(),
    )(pos, dw, other)


class _ChipExchange:
    def __init__(self, mine, theirs, send, recv):
        x, y, c, _ = _place()
        self.copies = []
        for a, (axis, width) in enumerate(SHARDED):
            for m in (1, 2, 3):
                px, py, sp = _other_chip(x, y, m)
                self.copies.append(_remote(_part(mine[a], sp, axis, width), theirs[a].at[m - 1], send, recv,
                                           3 * a + m - 1, (px, py, c)))

    def start(self):
        for cp in self.copies:
            cp.start()

    def wait(self):
        for cp in self.copies:
            cp.wait()

    @staticmethod
    def landing(ps):
        shapes = []
        for p, (axis, width) in zip(ps, SHARDED):
            shape = [3] + list(p.shape)
            shape[axis] = width
            shapes.append(jax.ShapeDtypeStruct(tuple(shape), p.dtype))
        return shapes

    @staticmethod
    def semaphores(n):
        return [pltpu.SemaphoreType.DMA((3 * n,)), pltpu.SemaphoreType.DMA((3 * n,))]


def _shard_sum(p, landed, pos, axis, width, name):
    _, Rs, Cs = landed.shape
    tr = min(256, Rs)
    p_spec = _part_spec(tr, Rs, Cs, axis, width, lambda a: 0)

    def body(pos_ref, p_ref, l_ref, o_ref):
        o_ref[...] = ((p_ref[...] + l_ref[0].astype(F32)) + l_ref[1].astype(F32)) + l_ref[2].astype(F32)

    return pl.pallas_call(
        body, name=name,
        grid_spec=pltpu.PrefetchScalarGridSpec(
            num_scalar_prefetch=1, grid=(Rs // tr,),
            in_specs=[p_spec, pl.BlockSpec((3, tr, Cs), lambda i, pos: (0, i, 0))],
            out_specs=pl.BlockSpec((None, tr, Cs), lambda i, pos: (pos[0], i, 0))),
        out_shape=jax.ShapeDtypeStruct((2, Rs, Cs), F32),
        compiler_params=_params(),
    )(pos, p[None], landed)


def _final_exchange(gs, packed):
    n = len(gs)
    rows = packed.shape[0]
    half = rows // 2
    assert half % 8 == 0

    def body(*refs):
        small_ref = refs[n]
        outs, total_ref = refs[n + 1:2 * n + 1], refs[2 * n + 1]
        sib_ref, chips_ref, done_ref, send, recv, small_send, small_recv = refs[2 * n + 2:]
        x, y, c, s = _place()
        me, sibling = (x, y, c), (x, y, 1 - c)
        copies = [_remote(outs[a].at[c], outs[a].at[c], send, recv, a, sibling) for a in range(n)]
        for cp in copies:
            cp.start()

        def small(src, dst, k, to):
            return _remote(src, dst, small_send, small_recv, k, to)

        mine = small_ref.at[pl.ds(pl.multiple_of(c * half, 8), half)]
        theirs = small_ref.at[pl.ds(pl.multiple_of((1 - c) * half, 8), half)]
        to_sibling = small(theirs, sib_ref, 0, sibling)
        to_sibling.start()
        to_sibling.wait()
        chips_ref[s] = mine[...] + sib_ref[...]
        to_chips = []
        for m in (1, 2, 3):
            px, py, _ = _other_chip(x, y, m)
            to_chips.append(small(chips_ref.at[s], chips_ref.at[s], m, (px, py, c)))
            to_chips[-1].start()
        for m in (1, 2, 3):
            _, _, sp = _other_chip(x, y, m)
            small(chips_ref.at[sp], chips_ref.at[sp], m, me).wait_recv()
        done_ref[c] = ((chips_ref[0] + chips_ref[1]) + chips_ref[2]) + chips_ref[3]
        finished = small(done_ref.at[c], done_ref.at[c], 4, sibling)
        finished.start()
        small(done_ref.at[1 - c], done_ref.at[1 - c], 4, me).wait_recv()
        total_ref[:half] = done_ref[0]
        total_ref[half:] = done_ref[1]
        for cp in to_chips + [finished]:
            cp.wait_send()
        for a, cp in enumerate(copies):
            cp.wait_send()
            _remote(outs[a].at[1 - c], outs[a].at[1 - c], send, recv, a, me).wait_recv()

    vmem = pl.BlockSpec(memory_space=pltpu.VMEM)
    res = pl.pallas_call(
        body, name="final_exchange",
        in_specs=[ANY] * n + [vmem], out_specs=[ANY] * n + [vmem],
        out_shape=[jax.ShapeDtypeStruct(g.shape, g.dtype) for g in gs]
        + [jax.ShapeDtypeStruct(packed.shape, packed.dtype)],
        input_output_aliases={a: a for a in range(n)},
        scratch_shapes=[pltpu.VMEM((half, LANES), F32), pltpu.VMEM((N_CHIPS, half, LANES), F32),
                        pltpu.VMEM((2, half, LANES), F32),
                        pltpu.SemaphoreType.DMA((n,)), pltpu.SemaphoreType.DMA((n,)),
                        pltpu.SemaphoreType.DMA((5,)), pltpu.SemaphoreType.DMA((5,))],
        compiler_params=_params(),
    )(*gs, packed)
    return res[:n], res[n]


def _adamw(w, g, m, v, name):
    shape = w.shape
    C = shape[-1]
    flat = [t.reshape(-1, C) for t in (w, g, m, v)]
    R = flat[0].shape[0]
    tr = max(t for t in range(8, R + 1, 8) if R % t == 0 and t * C <= 384 * 1024)

    def body(w_ref, g_ref, m_ref, v_ref, d_ref, nm_ref, nv_ref):
        _adamw_update(w_ref, g_ref, m_ref, v_ref, d_ref, nm_ref, nv_ref)

    blk = pl.BlockSpec((tr, C), lambda i: (i, 0))
    out = jax.ShapeDtypeStruct((R, C), F32)
    res = pl.pallas_call(
        body, name=name, grid=(R // tr,),
        in_specs=[blk] * 4, out_specs=[blk] * 3, out_shape=[out] * 3,
        compiler_params=_params(),
    )(*flat)
    return [t.reshape(shape) for t in res]


def _adamw_update(w_ref, g_ref, m_ref, v_ref, d_ref, nm_ref, nv_ref):
    gv = g_ref[...]
    nm = ADAM_B1 * m_ref[...] + (1.0 - ADAM_B1) * gv
    nv = ADAM_B2 * v_ref[...] + (1.0 - ADAM_B2) * (gv * gv)
    m_hat = nm / (1.0 - ADAM_B1 ** ADAM_STEP)
    v_hat = nv / (1.0 - ADAM_B2 ** ADAM_STEP)
    d_ref[...] = -ADAM_LR * (m_hat / (jnp.sqrt(v_hat) + ADAM_EPS) + ADAM_WD * w_ref[...])
    nm_ref[...] = nm
    nv_ref[...] = nv


def _adamw_small(ws, gs, ms, vs):
    n = len(ws)
    flat = lambda ts: [t.reshape(-1, t.shape[-1]) for t in ts]

    def body(*refs):
        for a in range(n):
            _adamw_update(*[refs[k * n + a] for k in range(7)])

    vmem = pl.BlockSpec(memory_space=pltpu.VMEM)
    shapes = [jax.ShapeDtypeStruct(w.shape, F32) for w in flat(ws)]
    res = pl.pallas_call(
        body, name="adamw_small",
        in_specs=[vmem] * (4 * n), out_specs=[vmem] * (3 * n), out_shape=shapes * 3,
        compiler_params=_params(),
    )(*flat(ws), *flat(gs), *flat(ms), *flat(vs))
    return [[r.reshape(w.shape) for r, w in zip(res[k * n:(k + 1) * n], ws)] for k in range(3)]


SMALL_SHAPES = ((2, 1024), (2, 2048), (2, 4, 128, 128), (2, 512), (1024,))


def _pack_small(parts):
    return jnp.concatenate([p.reshape(-1, LANES) for p in parts], axis=0)


def _unpack_small(packed):
    out, row = [], 0
    for shape in SMALL_SHAPES:
        n = 1
        for d in shape:
            n *= d
        out.append(packed[row:row + n // LANES].reshape(shape))
        row += n // LANES
    return out


def kernel(x, norm_g, w_in, b_gate, pool_w, pool_scale, w_pool_up, w_attn_up, w_out, final_g, loss_target, m_norm_g, m_w_in, m_b_gate, m_pool_w, m_pool_scale, m_w_pool_up, m_w_attn_up, m_w_out, m_final_g, v_norm_g, v_w_in, v_b_gate, v_pool_w, v_pool_scale, v_w_pool_up, v_w_attn_up, v_w_out, v_final_g):
    _, _, c, s = _place()
    pos = jnp.stack([c, s]).astype(jnp.int32)
    names = SHARDED_NAMES

    weights = _gather_first([_cast_into_place(w, pos, axis, width, f"cast_{n}")
                             for w, (axis, width), n in zip((w_in, w_pool_up, w_attn_up, w_out), SHARDED, names)])
    loss_part, dx, d_final_g, small, pair, landed = _local_step(x[0], loss_target[0], norm_g, b_gate, pool_w,
                                                                pool_scale, final_g, weights, pos)
    mine = [_shard_sum(p, l, pos, axis, width, f"grad_shard_sum_{n}")
            for p, l, (axis, width), n in zip(pair, landed, SHARDED, names)]
    (g_in, g_pu, g_au, g_out), summed = _final_exchange(
        mine, _pack_small(small + [d_final_g, jnp.broadcast_to(loss_part, (16, LANES))]))
    g_small = _unpack_small(summed)
    loss = summed[-16, 0]
    d_small, nm_small, nv_small = _adamw_small([norm_g, b_gate, pool_w, pool_scale, final_g], g_small,
                                               [m_norm_g, m_b_gate, m_pool_w, m_pool_scale, m_final_g],
                                               [v_norm_g, v_b_gate, v_pool_w, v_pool_scale, v_final_g])
    upd_in = _adamw(w_in, g_in, m_w_in, v_w_in, "adamw_w_in")
    upd_pu = _adamw(w_pool_up, g_pu, m_w_pool_up, v_w_pool_up, "adamw_w_pool_up")
    upd_au = _adamw(w_attn_up, g_au, m_w_attn_up, v_w_attn_up, "adamw_w_attn_up")
    upd_out = _adamw(w_out, g_out, m_w_out, v_w_out, "adamw_w_out")

    def ordered(sm, k):
        big = (upd_in[k], upd_pu[k], upd_au[k], upd_out[k]) if k is not None else (g_in, g_pu, g_au, g_out)
        return [sm[0], big[0], sm[1], sm[2], sm[3], big[1], big[2], big[3], sm[4]]

    return (loss, dx[None], *ordered(g_small, None), *ordered(d_small, 0), *ordered(nm_small, 1),
            *ordered(nv_small, 2))
```

```python
import jax
import jax.numpy as jnp
import numpy as np
from jax import lax
from jax.experimental import pallas as pl
from jax.experimental.pallas import tpu as pltpu

F32 = jnp.float32
BF16 = jnp.bfloat16
MESH = pl.DeviceIdType.MESH

D_MODEL = 1024
POOL_WIDTH = 512
POOL_WINDOWS = (2, 4, 8, 16)
POOL_GROUP = 128
POOL_HALO = 16
ATTN_WIDTH = 512
HEAD_DIM = 64
HEAD_PAIRS = 4
IN_WIDTH = 5120
N_CHIPS = 4
RMS_EPS = 1e-6
C_U, C_ZP, C_Q, C_K, C_V, C_ZA, C_GL = 0, 512, 1024, 1536, 2048, 2560, 3072

ADAM_LR, ADAM_B1, ADAM_B2, ADAM_EPS, ADAM_WD, ADAM_STEP = 0.001, 0.9, 0.999, 1e-08, 0.01, 10

LANES = 128
ATTN_BLOCK = 256
QUERY_BLOCKS = 4
ROW_TILE = 256
PROJ_ROW_TILE = 512
VMEM_LIMIT = 56 * 1024 * 1024


def _params(**kw):
    return pltpu.CompilerParams(vmem_limit_bytes=VMEM_LIMIT, **kw)


def _nt(a, b):
    return lax.dot_general(a, b, (((1,), (1,)), ((), ())), preferred_element_type=F32)


def _tn(a, b):
    return lax.dot_general(a, b, (((0,), (0,)), ((), ())), preferred_element_type=F32)


def _nn(a, b):
    return jnp.dot(a, b, preferred_element_type=F32)


def _sigmoid(z):
    return 1.0 / (1.0 + jnp.exp(-z))


def _rms_inproj(x, g, w_in, layer, gather=()):
    S = x.shape[0]
    tm = min(PROJ_ROW_TILE, S)
    n_tiles = S // tm
    n_g = len(gather)

    def body(*refs):
        x_ref, g_ref = refs[:2]
        if n_g:
            (u_ref, zp_ref, q_ref, k_ref, v_ref, za_ref, gl_ref, h_ref) = refs[2 + n_g:10 + n_g]
            fulls = refs[10 + n_g:10 + 2 * n_g]
            w_ref, load_sem, send, recv = refs[10 + 2 * n_g:]
            later = _Gather(LATER_PIECES, fulls, send, recv)

            @pl.when(pl.program_id(0) == 0)
            def _():
                load = pltpu.make_async_copy(fulls[0].at[layer], w_ref, load_sem)
                load.start()
                later.start()
                load.wait()

            pl.when(pl.program_id(0) == n_tiles - 1)(later.pass_on)
        else:
            w_ref, u_ref, zp_ref, q_ref, k_ref, v_ref, za_ref, gl_ref, h_ref = refs[2:]
        xv = x_ref[...]
        r = lax.rsqrt(jnp.mean(xv * xv, axis=-1, keepdims=True) + RMS_EPS)
        h = ((xv * r) * g_ref[...]).astype(BF16)
        h_ref[...] = h

        def mm(c0, n):
            return _nn(h, w_ref[:, c0:c0 + n])

        u_ref[...] = mm(C_U, 512)
        zp_ref[...] = mm(C_ZP, 512).astype(BF16)
        q_ref[...] = (mm(C_Q, 512) * 0.125).astype(BF16)
        k_ref[...] = mm(C_K, 512).astype(BF16)
        v_ref[...] = mm(C_V, 512).astype(BF16)
        za_ref[...] = mm(C_ZA, 512).astype(BF16)
        for c in range(4):
            gl_ref[:, c * 512:(c + 1) * 512] = mm(C_GL + c * 512, 512).astype(BF16)
        if n_g:
            pl.when(pl.program_id(0) == n_tiles - 1)(later.finish)

    row = lambda n: pl.BlockSpec((tm, n), lambda i: (i, 0))
    sd = lambda n, dt: jax.ShapeDtypeStruct((S, n), dt)
    any_space = pl.BlockSpec(memory_space=pl.ANY)
    weights = [any_space] * n_g if n_g else [_layer_weight_spec(D_MODEL, IN_WIDTH, layer)]
    res = pl.pallas_call(
        body, name=f"rms_inproj_l{layer}", grid=(n_tiles,),
        in_specs=[row(D_MODEL), pl.BlockSpec((1, D_MODEL), lambda i: (0, 0))] + weights,
        out_specs=[row(512), row(512), row(512), row(512), row(512), row(512), row(2048), row(D_MODEL)]
        + [any_space] * n_g,
        out_shape=[sd(512, F32), sd(512, BF16), sd(512, BF16), sd(512, BF16), sd(512, BF16), sd(512, BF16),
                   sd(2048, BF16), sd(D_MODEL, BF16)] + [jax.ShapeDtypeStruct(f.shape, f.dtype) for f in gather],
        input_output_aliases={2 + a: 8 + a for a in range(n_g)},
        scratch_shapes=([pltpu.VMEM((D_MODEL, IN_WIDTH), BF16), pltpu.SemaphoreType.DMA(())]
                        + _Gather.semaphores(LATER_PIECES)) if n_g else [],
        compiler_params=_params(),
    )(x, g, *(gather if n_g else (w_in,)))
    return res[:8], res[8:]


def _tri(n, strict_lower):
    r = lax.broadcasted_iota(jnp.int32, (n, n), 0)
    c = lax.broadcasted_iota(jnp.int32, (n, n), 1)
    return jnp.where(r > c if strict_lower else r < c, 1.0, 0.0).astype(BF16)


def _split_dot(x, m):
    hi = x.astype(BF16)
    lo = (x - hi.astype(F32)).astype(BF16)
    return _nn(hi, m) + _nn(lo, m)


def _log_terms(z):
    lg = jnp.log(1.0 + jnp.exp(-jnp.abs(z)))
    a = jnp.minimum(z, 0.0) - lg
    return a, a - z


EXHAUSTED = -104.0
UNREACHED = -1e30


class _HeadPair:
    def __init__(self, T):
        self.T = T
        self.first = lax.broadcasted_iota(jnp.int32, (T, LANES), 1) < HEAD_DIM
        self.lane = lax.broadcasted_iota(jnp.int32, (2 * T, LANES), 1)
        row = lax.broadcasted_iota(jnp.int32, (2 * T, T), 0)
        row = jnp.where(row >= T, row - T, row)
        self.causal = row > lax.broadcasted_iota(jnp.int32, (2 * T, T), 1)
        self.below = _tri(T, True)

    def stack(self, x2):
        return jnp.concatenate([jnp.where(self.first, x2, 0), jnp.where(self.first, 0, x2)], axis=0).astype(BF16)

    def unstack(self, x):
        return jnp.where(self.first, x[:self.T], x[self.T:])

    def keys(self, ref, blocks):
        T = self.T
        return jnp.concatenate([ref[pl.ds(pl.multiple_of(j * T, T), T), :] for j, _ in blocks], axis=0)

    def log_terms(self, z, blocks):
        T = self.T
        a_all, l_all = _log_terms(z)
        a = [a_all[:, b * T:(b + 1) * T] for b in range(len(blocks))]
        l1m = [l_all[:, b * T:(b + 1) * T] for b in range(len(blocks))]
        return a, [jnp.where(self.causal, l, 0.0) if diagonal else l for l, (_, diagonal) in zip(l1m, blocks)]

    def later_sums(self, l1m):
        later = _split_dot(jnp.concatenate(l1m, axis=0), self.below)
        return [later[2 * self.T * b:2 * self.T * (b + 1)] for b in range(len(l1m))]


def _sections(x, n):
    return x.reshape(n, x.shape[0] // n, x.shape[1])


def _attn_fwd(q, k, v, layer, gather=()):
    S = q.shape[0]
    T = min(ATTN_BLOCK, S)
    nq = S // T
    jobs = min(QUERY_BLOCKS, nq)
    assert nq <= LANES and nq % jobs == 0
    per_job = nq // jobs

    n_g = len(gather)

    def body(q_ref, k_ref, v_ref, *rest):
        o_ref, c_ref = rest[n_g:n_g + 2]
        i = pl.program_id(1)
        if n_g:
            last = _Gather(LAST_PIECES, rest[n_g + 2:2 * n_g + 2], *rest[2 * n_g + 2:])
            pl.when(jnp.logical_and(pl.program_id(0) == 0, i == 0))(last.start)
        pair = _HeadPair(T)
        qs = [pair.stack(q_ref[n]) for n in range(jobs)]
        diag = [i + n * per_job for n in range(jobs)]

        def sweep(jobs):
            kv = [(pair.keys(k_ref, bl), pair.keys(v_ref, bl)) for _, bl, _ in jobs]
            zs = [_nt(qs[n], kcat) for (n, _, _), (kcat, _) in zip(jobs, kv)]
            terms = [pair.log_terms(z, bl) for (_, bl, _), z in zip(jobs, zs)]
            laters = [pair.later_sums(l1m) for _, l1m in terms]
            weights = []
            for (_, bl, (acc, run, saved)), (a, l1m), later in zip(jobs, terms, laters):
                ws = []
                for b, (j, diagonal) in enumerate(bl):
                    saved = jnp.where(pair.lane == j, run, saved)
                    w = jnp.exp(a[b] + later[b] + run)
                    ws.append(jnp.where(pair.causal, w, 0.0) if diagonal else w)
                    run = run + jnp.sum(l1m[b], axis=1, keepdims=True)
                weights.append((jnp.concatenate(ws, axis=1).astype(BF16), acc, run, saved))
            return [(acc + _nn(w, vcat), run, saved) for (w, acc, run, saved), (_, vcat) in zip(weights, kv)]

        def alive(carry):
            return (jnp.max(carry[1]) > EXHAUSTED).astype(jnp.int32)

        def older_blocks(n, carry):
            def older_block(state):
                j, _, c = state
                c = sweep([(n, [(j, False)], c)])[0]
                return j - 1, alive(c), c

            return lax.while_loop(lambda st: jnp.logical_and(st[0] >= 0, st[1] > 0), older_block,
                                  (diag[n] - 2, alive(carry), carry))[2]

        def run(first_blocks):
            init = (jnp.zeros((2 * T, LANES), F32), jnp.zeros((2 * T, 1), F32),
                    jnp.full((2 * T, LANES), UNREACHED, F32))
            carries = sweep([(n, first_blocks[n], init) for n in range(jobs)])
            for n in range(jobs):
                acc, _, saved = older_blocks(n, carries[n])
                o_ref[n] = pair.unstack(acc).astype(BF16)
                c_ref[n, :, :LANES] = saved[:T]
                c_ref[n, :, LANES:] = saved[T:]

        with_previous = lambda d: [(d, True), (d - 1, False)]

        @pl.when(i == 0)
        def _():
            run([[(diag[0], True)]] + [with_previous(d) for d in diag[1:]])

        @pl.when(i > 0)
        def _():
            run([with_previous(d) for d in diag])

        if n_g:
            pl.when(jnp.logical_and(pl.program_id(0) == HEAD_PAIRS - 1, i == 0))(last.pass_on)
            pl.when(jnp.logical_and(pl.program_id(0) == HEAD_PAIRS - 1, i == per_job - 1))(last.finish)

    blk = lambda n: pl.BlockSpec((jobs, T, n), lambda p, i: (0, i, p))
    full = pl.BlockSpec((S, LANES), lambda p, i: (0, p))
    any_space = pl.BlockSpec(memory_space=pl.ANY)
    res = pl.pallas_call(
        body, name=f"attn_fwd_l{layer}", grid=(HEAD_PAIRS, per_job),
        in_specs=[blk(LANES), full, full] + [any_space] * n_g,
        out_specs=[blk(LANES), blk(2 * LANES)] + [any_space] * n_g,
        out_shape=[jax.ShapeDtypeStruct((jobs, S // jobs, ATTN_WIDTH), BF16),
                   jax.ShapeDtypeStruct((jobs, S // jobs, 8 * LANES), F32)]
        + [jax.ShapeDtypeStruct(f.shape, f.dtype) for f in gather],
        input_output_aliases={3 + a: 2 + a for a in range(n_g)},
        scratch_shapes=_Gather.semaphores(LAST_PIECES) if n_g else [],
        compiler_params=_params(),
    )(_sections(q, jobs), k, v, *gather)
    return res[0].reshape(S, ATTN_WIDTH), res[1].reshape(S, 8 * LANES), res[2:]


def _attn_bwd(q, k, v, saved, do, layer):
    S = q.shape[0]
    T = min(ATTN_BLOCK, S)
    nq = S // T
    jobs = min(QUERY_BLOCKS, nq)
    per_job = nq // jobs

    def body(q_ref, k_ref, v_ref, c_ref, do_ref, dq_ref, dk_ref, dv_ref):
        i = pl.program_id(1)

        @pl.when(i == 0)
        def _():
            dk_ref[...] = jnp.zeros_like(dk_ref)
            dv_ref[...] = jnp.zeros_like(dv_ref)

        pair = _HeadPair(T)
        diag = [i + n * per_job for n in range(jobs)]
        qs = [pair.stack(q_ref[n]) for n in range(jobs)]
        dos = [pair.stack(do_ref[n].astype(BF16)) for n in range(jobs)]
        saved = [jnp.concatenate([c_ref[n, :, :LANES], c_ref[n, :, LANES:]], axis=0) for n in range(jobs)]
        before = _tri(T, False)

        def sweep(jobs):
            kv = [(pair.keys(k_ref, bl), pair.keys(v_ref, bl)) for _, bl, _ in jobs]
            zs = [_nt(qs[n], kcat) for (n, _, _), (kcat, _) in zip(jobs, kv)]
            gs = [_nt(dos[n], vcat) for (n, _, _), (_, vcat) in zip(jobs, kv)]
            terms = [pair.log_terms(z, bl) for (_, bl, _), z in zip(jobs, zs)]
            laters = [pair.later_sums(l1m) for _, l1m in terms]
            ws, es = [], []
            for (n, bl, _), (a, _), later, g in zip(jobs, terms, laters, gs):
                w_job, e_job = [], []
                for b, (j, diagonal) in enumerate(bl):
                    run = jnp.sum(jnp.where(pair.lane == j, saved[n], 0.0), axis=1, keepdims=True)
                    w = jnp.exp(a[b] + later[b] + run)
                    w_job.append(jnp.where(pair.causal, w, 0.0) if diagonal else w)
                    e_job.append(w_job[b] * g[:, b * T:(b + 1) * T])
                ws.append(w_job)
                es.append(e_job)
            prefixes = [_nn(jnp.concatenate(e_job, axis=0).astype(BF16), before) for e_job in es]
            dzs, olders = [], []
            for (_, bl, (_, older)), (a, _), e_job, prefix in zip(jobs, terms, es, prefixes):
                dz_job = []
                for b, (j, diagonal) in enumerate(bl):
                    dz = e_job[b] - jnp.exp(a[b]) * (e_job[b] + (prefix[2 * T * b:2 * T * (b + 1)] + older))
                    dz_job.append(jnp.where(pair.causal, dz, 0.0) if diagonal else dz)
                    older = older + jnp.sum(e_job[b], axis=1, keepdims=True)
                dzs.append(jnp.concatenate(dz_job, axis=1).astype(BF16))
                olders.append(older)
            out = []
            for (n, bl, (dq, _)), dz, w_job, older, (kcat, _) in zip(jobs, dzs, ws, olders, kv):
                dk = _tn(dz, qs[n])
                dv = _tn(jnp.concatenate(w_job, axis=1).astype(BF16), dos[n])
                for b, (j, _) in enumerate(bl):
                    rows = pl.ds(pl.multiple_of(j * T, T), T)
                    dk_ref[rows, :] += dk[b * T:(b + 1) * T]
                    dv_ref[rows, :] += dv[b * T:(b + 1) * T]
                out.append((dq + _nn(dz, kcat), older))
            return out

        def older_blocks(n):
            col_max = jnp.max(saved[n], axis=0, keepdims=True)
            lane_row = lax.broadcasted_iota(jnp.int32, (1, LANES), 1)
            reached = jnp.sum(jnp.where(jnp.logical_and(col_max > EXHAUSTED, lane_row < diag[n]), 1, 0))
            init = (jnp.zeros((2 * T, LANES), F32), jnp.zeros((2 * T, 1), F32))
            return lax.fori_loop(diag[n] - reached, diag[n] - 1, lambda j, c: sweep([(n, [(j, False)], c)])[0], init)

        def run(last_blocks):
            carries = sweep([(n, last_blocks[n], older_blocks(n)) for n in range(jobs)])
            for n in range(jobs):
                dq_ref[n] = (pair.unstack(carries[n][0]) * 0.125).astype(BF16)

        with_previous = lambda d: [(d - 1, False), (d, True)]

        @pl.when(i == 0)
        def _():
            run([[(diag[0], True)]] + [with_previous(d) for d in diag[1:]])

        @pl.when(i > 0)
        def _():
            run([with_previous(d) for d in diag])

    blk = lambda n: pl.BlockSpec((jobs, T, n), lambda p, i: (0, i, p))
    full = pl.BlockSpec((S, LANES), lambda p, i: (0, p))
    out = jax.ShapeDtypeStruct((S, ATTN_WIDTH), F32)
    dq, dk, dv = pl.pallas_call(
        body, name=f"attn_bwd_l{layer}", grid=(HEAD_PAIRS, per_job),
        in_specs=[blk(LANES), full, full, blk(2 * LANES), blk(LANES)],
        out_specs=[blk(LANES), full, full],
        out_shape=[jax.ShapeDtypeStruct((jobs, S // jobs, ATTN_WIDTH), BF16), out, out],
        compiler_params=_params(),
    )(_sections(q, jobs), k, v, _sections(saved, jobs), _sections(do, jobs))
    return dq.reshape(S, ATTN_WIDTH), dk, dv


def _pool_counts(row0, tm):
    pos = row0 + lax.broadcasted_iota(jnp.int32, (tm, 1), 0)
    return [1.0 / jnp.minimum(pos + 1, w).astype(F32) for w in POOL_WINDOWS]


def _window_bands(tm, backward):
    t = np.arange(tm)[:, None]
    c = np.arange(tm)[None, :]
    off = c - t if backward else t - c
    main = np.stack([(off >= 0) & (off < w) for w in POOL_WINDOWS])
    r = np.arange(POOL_HALO)[:, None]
    h = np.arange(POOL_HALO)[None, :]
    off = h - r + POOL_HALO if backward else r - h + POOL_HALO
    edge = np.concatenate([(off < w) for w in POOL_WINDOWS])
    return jnp.asarray(main, BF16), jnp.asarray(edge, BF16)


def _window_sums(tile, beside, main_ref, edge_ref, backward):
    tm = tile.shape[0]
    tb = tile.astype(BF16)
    edge = _nn(edge_ref[...], beside.astype(BF16))
    sums = []
    for g in range(len(POOL_WINDOWS)):
        cols = slice(g * POOL_GROUP, (g + 1) * POOL_GROUP)
        tot = _nn(main_ref[g], tb[:, cols])
        extra = edge[g * POOL_HALO:(g + 1) * POOL_HALO, cols]
        if backward:
            sums.append(jnp.concatenate([tot[:tm - POOL_HALO], tot[tm - POOL_HALO:] + extra], axis=0))
        else:
            sums.append(jnp.concatenate([tot[:POOL_HALO] + extra, tot[POOL_HALO:]], axis=0))
    return sums


def _post_forward(u, history, bands, inv_cnt, zp, o, za, gl, bg, pw_ref, scale, wpu_ref, wau_ref):
    pooled, mixed = [], []
    for g, tot in enumerate(_window_sums(u, history, *bands, False)):
        pg = (tot * inv_cnt[g] - u[:, g * POOL_GROUP:(g + 1) * POOL_GROUP]).astype(BF16)
        pooled.append(pg)
        mixed.append(_nn(pg, pw_ref[g].astype(BF16)))
    pooled = jnp.concatenate(pooled, axis=1)
    mixed = jnp.concatenate(mixed, axis=1)
    zp, za, o = zp.astype(F32), za.astype(F32), o.astype(F32)
    sp = _sigmoid(zp)
    sa = _sigmoid(za)
    y_pool = (mixed * scale) * (zp * sp)
    y_attn = o * (za * sa)
    gate = _sigmoid(gl + bg)
    g0, g1 = gate[:, :D_MODEL], gate[:, D_MODEL:]
    up_p = _nn(y_pool.astype(BF16), wpu_ref[...])
    up_a = _nn(y_attn.astype(BF16), wau_ref[...])
    merged = g0 * up_p + g1 * up_a
    return pooled, mixed, sp, sa, y_pool, y_attn, g0, g1, up_p, up_a, merged


def _row_specs(tm, rev, n_tiles):
    tile_of = (lambda i: n_tiles - 1 - i) if rev else (lambda i: i)
    row = lambda n: pl.BlockSpec((tm, n), lambda i: (tile_of(i), 0))
    halo = pl.BlockSpec((POOL_HALO, POOL_WIDTH),
                        lambda i: (jnp.maximum(tile_of(i) * (tm // POOL_HALO) - 1, 0), 0))
    const = lambda shape: pl.BlockSpec(shape, lambda i: (0,) * len(shape))
    return tile_of, row, halo, const


def _layer_weight_spec(rows, cols, layer):
    return pl.BlockSpec((None, rows, cols), lambda i: (layer, 0, 0), pipeline_mode=pl.Buffered(1))


def _post_fwd(x, u, zp, o, za, gl, bg, pw, scale, wpu, wau, wout, layer, head=()):
    S = x.shape[0]
    tm = min(ROW_TILE, S)
    n_tiles = S // tm
    tile_of, row, halo, const = _row_specs(tm, False, n_tiles)

    def body(x_ref, u_ref, uh_ref, main_ref, edge_ref, zp_ref, o_ref, za_ref, gl_ref, bg_ref, pw_ref, sc_ref, wpu_ref,
             wau_ref, wout_ref, *rest):
        i = pl.program_id(0)
        vals = _post_forward(u_ref[...], jnp.where(i == 0, 0.0, uh_ref[...]), (main_ref, edge_ref),
                             _pool_counts(i * tm, tm), zp_ref[...], o_ref[...], za_ref[...], gl_ref[...], bg_ref[...],
                             pw_ref, sc_ref[...], wpu_ref, wau_ref)
        xv = x_ref[...] + _nn(vals[-1].astype(BF16), wout_ref[...])
        if not head:
            rest[0][...] = xv
            return
        gf_ref, t_ref, loss_ref, dx_ref, dg_ref = rest

        @pl.when(i == 0)
        def _():
            loss_ref[...] = jnp.zeros_like(loss_ref)
            dg_ref[...] = jnp.zeros_like(dg_ref)

        r = lax.rsqrt(jnp.mean(xv * xv, axis=-1, keepdims=True) + RMS_EPS)
        diff = (xv * r) * gf_ref[...] - t_ref[...]
        per_row = jnp.mean(diff * diff, axis=-1, keepdims=True)
        loss_ref[...] += 0.5 * jnp.sum(per_row, axis=0, keepdims=True)
        dx, dg_rows = _rms_backward(diff * (1.0 / D_MODEL), xv, r, gf_ref[...])
        dx_ref[...] = dx
        dg_ref[...] += jnp.sum(dg_rows, axis=0, keepdims=True)

    out = jax.ShapeDtypeStruct((S, D_MODEL), F32)
    return pl.pallas_call(
        body, name=f"post_fwd_l{layer}", grid=(n_tiles,),
        in_specs=[row(D_MODEL), row(512), halo, const((4, tm, tm)), const((4 * POOL_HALO, POOL_HALO)), row(512),
                  row(512), row(512), row(2048), const((1, 2048)), const((4, POOL_GROUP, POOL_GROUP)),
                  const((1, POOL_WIDTH)),
                  _layer_weight_spec(POOL_WIDTH, D_MODEL, layer), _layer_weight_spec(ATTN_WIDTH, D_MODEL, layer),
                  _layer_weight_spec(D_MODEL, D_MODEL, layer)] + ([const((1, D_MODEL)), row(D_MODEL)] if head else []),
        out_specs=[const((1, LANES)), row(D_MODEL), const((1, D_MODEL))] if head else row(D_MODEL),
        out_shape=[jax.ShapeDtypeStruct((1, LANES), F32), out, jax.ShapeDtypeStruct((1, D_MODEL), F32)] if head else out,
        compiler_params=_params(),
    )(x, u, u, *_window_bands(tm, False), zp, o, za, gl, bg, pw, scale, wpu, wau, wout, *head)


def _post_bwd(dx, u, zp, o, za, gl, bg, pw, scale, wpu, wau, wout, layer):
    S = dx.shape[0]
    tm = min(ROW_TILE, S)
    n_tiles = S // tm
    tile_of, row, halo, const = _row_specs(tm, True, n_tiles)

    def body(dx_ref, u_ref, uh_ref, main_ref, edge_ref, back_main_ref, back_edge_ref, zp_ref, o_ref, za_ref, gl_ref,
             bg_ref, pw_ref, sc_ref, wpu_ref, wau_ref, wout_ref,
             duz_ref, do_ref, dzg_ref, dsc_ref, dbg_ref,
             merged_ref, dup_ref, dua_ref, yp_ref, ya_ref, pooled_ref, dmixed_ref, nxt_ref):
        step = pl.program_id(0)
        i = tile_of(step)

        @pl.when(step == 0)
        def _():
            dsc_ref[...] = jnp.zeros_like(dsc_ref)
            dbg_ref[...] = jnp.zeros_like(dbg_ref)
            nxt_ref[...] = jnp.zeros_like(nxt_ref)

        inv_cnt = _pool_counts(i * tm, tm)
        zp, za, o = zp_ref[...].astype(F32), za_ref[...].astype(F32), o_ref[...].astype(F32)
        pooled, mixed, sp, sa, y_pool, y_attn, g0, g1, up_p, up_a, merged = _post_forward(
            u_ref[...], jnp.where(i == 0, 0.0, uh_ref[...]), (main_ref, edge_ref), inv_cnt, zp, o, za, gl_ref[...],
            bg_ref[...], pw_ref, sc_ref[...], wpu_ref, wau_ref)
        merged_ref[...] = merged.astype(BF16)
        yp_ref[...] = y_pool.astype(BF16)
        ya_ref[...] = y_attn.astype(BF16)
        pooled_ref[...] = pooled

        dmerged = _nt(dx_ref[...].astype(BF16), wout_ref[...])
        dup = (dmerged * g0).astype(BF16)
        dua = (dmerged * g1).astype(BF16)
        dup_ref[...] = dup
        dua_ref[...] = dua
        dgl0 = (dmerged * up_p) * (g0 * (1.0 - g0))
        dgl1 = (dmerged * up_a) * (g1 * (1.0 - g1))
        dzg_ref[:, ATTN_WIDTH:ATTN_WIDTH + D_MODEL] = dgl0.astype(BF16)
        dzg_ref[:, ATTN_WIDTH + D_MODEL:] = dgl1.astype(BF16)
        dbg_ref[:, :D_MODEL] += jnp.sum(dgl0, axis=0, keepdims=True)
        dbg_ref[:, D_MODEL:] += jnp.sum(dgl1, axis=0, keepdims=True)

        dy_attn = _nt(dua, wau_ref[...])
        do_ref[...] = (dy_attn * (za * sa)).astype(BF16)
        dzg_ref[:, :ATTN_WIDTH] = ((dy_attn * o) * (sa * (1.0 + za * (1.0 - sa)))).astype(BF16)

        dy_pool = _nt(dup, wpu_ref[...])
        ms = mixed * sc_ref[...]
        dms = dy_pool * (zp * sp)
        duz_ref[:, POOL_WIDTH:] = ((dy_pool * ms) * (sp * (1.0 + zp * (1.0 - sp)))).astype(BF16)
        dsc_ref[...] += jnp.sum(dms * mixed, axis=0, keepdims=True)
        dmixed = (dms * sc_ref[...]).astype(BF16)
        dmixed_ref[...] = dmixed
        dpooled = [_nt(dmixed[:, g * POOL_GROUP:(g + 1) * POOL_GROUP], pw_ref[g].astype(BF16)) for g in range(4)]
        scaled = jnp.concatenate([d * inv for d, inv in zip(dpooled, inv_cnt)], axis=1)
        for g, tot in enumerate(_window_sums(scaled, nxt_ref[...], back_main_ref, back_edge_ref, True)):
            duz_ref[:, g * POOL_GROUP:(g + 1) * POOL_GROUP] = (tot - dpooled[g]).astype(BF16)
        nxt_ref[...] = scaled[:POOL_HALO]

    sd = lambda n, dt: jax.ShapeDtypeStruct((S, n), dt)
    bands = [const((4, tm, tm)), const((4 * POOL_HALO, POOL_HALO))]
    return pl.pallas_call(
        body, name=f"post_bwd_l{layer}", grid=(n_tiles,),
        in_specs=[row(D_MODEL), row(512), halo, *bands, *bands, row(512), row(512), row(512), row(2048),
                  const((1, 2048)), const((4, POOL_GROUP, POOL_GROUP)), const((1, POOL_WIDTH)),
                  _layer_weight_spec(POOL_WIDTH, D_MODEL, layer), _layer_weight_spec(ATTN_WIDTH, D_MODEL, layer),
                  _layer_weight_spec(D_MODEL, D_MODEL, layer)],
        out_specs=[row(1024), row(512), row(2560), const((1, POOL_WIDTH)), const((1, 2048)),
                   row(D_MODEL), row(D_MODEL), row(D_MODEL), row(512), row(512), row(512), row(512)],
        out_shape=[sd(1024, BF16), sd(512, BF16), sd(2560, BF16),
                   jax.ShapeDtypeStruct((1, POOL_WIDTH), F32), jax.ShapeDtypeStruct((1, 2048), F32),
                   sd(D_MODEL, BF16), sd(D_MODEL, BF16), sd(D_MODEL, BF16), sd(512, BF16), sd(512, BF16),
                   sd(512, BF16), sd(512, BF16)],
        scratch_shapes=[pltpu.VMEM((POOL_HALO, POOL_WIDTH), F32)],
        compiler_params=_params(),
    )(dx, u, u, *_window_bands(tm, False), *_window_bands(tm, True), zp, o, za, gl, bg, pw, scale, wpu, wau, wout)


def _rms_backward(dh, xv, r, g):
    xhat = xv * r
    dxhat = dh * g
    return r * (dxhat - xhat * jnp.mean(dxhat * xhat, axis=-1, keepdims=True)), dh * xhat


def _inproj_bwd(pieces, w_in, x, g, dx_res, layer, exchange=()):
    S = x.shape[0]
    tm = min(PROJ_ROW_TILE, S)
    cols = [(c0, p.shape[1]) for p, c0 in pieces]

    def body(ins, outs):
        piece_refs = ins[:len(cols)]
        w_ref, x_ref, g_ref, res_ref = ins[len(cols):]
        dx_ref, dg_ref = outs

        @pl.when(pl.program_id(0) == 0)
        def _():
            dg_ref[...] = jnp.zeros_like(dg_ref)

        dh = jnp.zeros((tm, D_MODEL), F32)
        for p_ref, (c0, n) in zip(piece_refs, cols):
            for c in range(0, n, 512):
                dh = dh + _nt(p_ref[:, c:c + 512].astype(BF16), w_ref[:, c0 + c:c0 + c + 512])
        xv = x_ref[...]
        r = lax.rsqrt(jnp.mean(xv * xv, axis=-1, keepdims=True) + RMS_EPS)
        dx, dg_rows = _rms_backward(dh, xv, r, g_ref[...])
        dx_ref[...] = res_ref[...] + dx
        dg_ref[...] += jnp.sum(dg_rows, axis=0, keepdims=True)

    row = lambda n: pl.BlockSpec((tm, n), lambda i: (i, 0))
    vec = pl.BlockSpec((1, D_MODEL), lambda i: (0, 0))
    any_space = pl.BlockSpec(memory_space=pl.ANY)
    n_x = len(exchange)
    grid = (S // tm,)
    res = pl.pallas_call(
        _with_swap(body, grid, len(cols) + 4, 2, n_x, rider=_ChipExchange), name=f"inproj_bwd_l{layer}", grid=grid,
        in_specs=[row(n) for _, n in cols] + [_layer_weight_spec(D_MODEL, IN_WIDTH, layer), row(D_MODEL), vec,
                                              row(D_MODEL)] + [any_space] * n_x,
        out_specs=[row(D_MODEL), vec] + [any_space] * n_x,
        out_shape=[jax.ShapeDtypeStruct((S, D_MODEL), F32), jax.ShapeDtypeStruct((1, D_MODEL), F32)]
        + _ChipExchange.landing(exchange),
        scratch_shapes=_ChipExchange.semaphores(n_x) if n_x else [],
        compiler_params=_params(),
    )(*[p for p, _ in pieces], w_in, x, g, dx_res, *exchange)
    return res[0], res[1], res[2:]


class _SiblingSwap:
    def __init__(self, mine, theirs, send, recv):
        x, y, c, _ = _place()
        self.copies = [_remote(m.at[1 - c], t, send, recv, a, (x, y, 1 - c))
                       for a, (m, t) in enumerate(zip(mine, theirs))]

    def start(self):
        for cp in self.copies:
            cp.start()

    def wait(self):
        for cp in self.copies:
            cp.wait()


def _with_swap(body, grid, n_in, n_out, n_swap, rider=_SiblingSwap):
    if not n_swap:
        return lambda *refs: body(refs[:n_in], refs[n_in:])

    def riding(*refs):
        ins, mine = refs[:n_in], refs[n_in:n_in + n_swap]
        outs, theirs = refs[n_in + n_swap:n_in + n_swap + n_out], refs[n_in + n_swap + n_out:n_in + 2 * n_swap + n_out]
        swap = rider(mine, theirs, *refs[n_in + 2 * n_swap + n_out:])
        step = [pl.program_id(d) for d in range(len(grid))]
        first, last = step[0] == 0, step[0] == grid[0] - 1
        for d in range(1, len(grid)):
            first, last = jnp.logical_and(first, step[d] == 0), jnp.logical_and(last, step[d] == grid[d] - 1)
        pl.when(first)(swap.start)
        body(ins, outs)
        pl.when(last)(swap.wait)

    return riding


def _swap_specs(swap):
    any_space = pl.BlockSpec(memory_space=pl.ANY)
    shapes = [jax.ShapeDtypeStruct(d.shape[1:], d.dtype) for d in swap]
    sems = [pltpu.SemaphoreType.DMA((len(swap),)), pltpu.SemaphoreType.DMA((len(swap),))] if swap else []
    return [any_space] * len(swap), shapes, sems


def _wgrad(a, b, name, layer, into=None, col0=0, n_total=None, swap=()):
    S, M = a.shape
    N = b.shape[1]
    n_total = N if n_total is None else n_total
    tk = min(2048, S)
    tn = max(t for t in range(LANES, min(N, 1280) + 1, LANES) if N % t == 0 and col0 % t == 0)
    grid = (N // tn, S // tk)

    def body(ins, outs):
        prod = _tn(ins[0][...].astype(BF16), ins[1][...].astype(BF16))

        @pl.when(pl.program_id(1) == 0)
        def _():
            outs[0][...] = prod

        @pl.when(pl.program_id(1) > 0)
        def _():
            outs[0][...] += prod

        @pl.when(pl.program_id(1) == grid[1] - 1)
        def _():
            outs[1][...] = outs[0][...].astype(BF16)

    in_specs = [pl.BlockSpec((tk, M), lambda j, k: (k, 0)), pl.BlockSpec((tk, tn), lambda j, k: (k, j))]
    args = [a, b]
    aliases = {}
    if into is not None:
        in_specs += [pl.BlockSpec(memory_space=pl.ANY)] * 2
        args += list(into)
        aliases = {2: 0, 3: 1}
    swap_specs, swap_shapes, swap_sems = _swap_specs(swap)
    out_spec = pl.BlockSpec((None, M, tn), lambda j, k: (layer, 0, col0 // tn + j))
    res = pl.pallas_call(
        _with_swap(body, grid, len(args), 2, len(swap)), name=name, grid=grid,
        in_specs=in_specs + swap_specs,
        out_specs=[out_spec, out_spec] + swap_specs,
        out_shape=[jax.ShapeDtypeStruct((2, M, n_total), F32), jax.ShapeDtypeStruct((2, M, n_total), BF16)]
        + swap_shapes,
        input_output_aliases=aliases,
        scratch_shapes=swap_sems,
        compiler_params=_params(),
    )(*args, *swap)
    return ((res[0], res[1]), res[2:]) if swap else (res[0], res[1])


def _pool_wgrad(pooled, dmixed, layer, swap=()):
    S = pooled.shape[0]
    tk = min(8192, S)
    grid = (4, S // tk)

    def body(ins, outs):
        prod = _tn(ins[0][...], ins[1][...])

        @pl.when(pl.program_id(1) == 0)
        def _():
            outs[0][...] = prod

        @pl.when(pl.program_id(1) > 0)
        def _():
            outs[0][...] += prod

    blk = pl.BlockSpec((tk, POOL_GROUP), lambda g, k: (k, g))
    swap_specs, swap_shapes, swap_sems = _swap_specs(swap)
    res = pl.pallas_call(
        _with_swap(body, grid, 2, 1, len(swap)), name=f"pool_wgrad_l{layer}", grid=grid,
        in_specs=[blk, blk] + swap_specs,
        out_specs=[pl.BlockSpec((None, POOL_GROUP, POOL_GROUP), lambda g, k: (g, 0, 0))] + swap_specs,
        out_shape=[jax.ShapeDtypeStruct((4, POOL_GROUP, POOL_GROUP), F32)] + swap_shapes,
        scratch_shapes=swap_sems,
        compiler_params=_params(),
    )(pooled, dmixed, *swap)
    return (res[0], res[1:]) if swap else res[0]


def _local_step(x, target, norm_g, b_gate, pool_w, pool_scale, final_g, weights, pos):
    n_layers = norm_g.shape[0]
    saved = []
    for l in range(n_layers):
        g = norm_g[l][None]
        bg = b_gate[l][None]
        sc = pool_scale[l][None]
        if l == 0:
            (u, zp, q, k, v, za, gl, h), (w_in, w_pu, w_au, w_out) = _rms_inproj(x, g, None, l, gather=weights)
        else:
            (u, zp, q, k, v, za, gl, h), _ = _rms_inproj(x, g, w_in, l)
        if l == 0:
            o, carry, (w_in,) = _attn_fwd(q, k, v, l, gather=(w_in,))
        else:
            o, carry, _ = _attn_fwd(q, k, v, l)
        saved.append((x, g, bg, sc, u, zp, q, k, v, za, gl, h, o, carry))
        if l < n_layers - 1:
            x = _post_fwd(x, u, zp, o, za, gl, bg, pool_w[l], sc, w_pu, w_au, w_out, l)
        else:
            loss, dx, d_final_g = _post_fwd(x, u, zp, o, za, gl, bg, pool_w[l], sc, w_pu, w_au, w_out, l,
                                            head=(final_g[None], target))

    small = [None] * n_layers
    dw_in = dw_out = dw_pu = dw_au = None
    for l in reversed(range(n_layers)):
        x_in, g, bg, sc, u, zp, q, k, v, za, gl, h, o, carry = saved[l]
        (duz, do, dzg, dsc, dbg, merged, dup, dua, y_pool, y_attn, pooled, dmixed) = _post_bwd(
            dx, u, zp, o, za, gl, bg, pool_w[l], sc, w_pu, w_au, w_out, l)
        dq, dk, dv = _attn_bwd(q, k, v, carry, do, l)
        pieces = [(duz, C_U), (dq, C_Q), (dk, C_K), (dv, C_V), (dzg, C_ZA)]
        for p, c0 in pieces:
            dw_in = _wgrad(h, p, f"wgrad_in_l{l}_c{c0}", l, into=dw_in, col0=c0, n_total=IN_WIDTH)
        if l > 0:
            dw_out = _wgrad(merged, dx, f"wgrad_out_l{l}", l, into=dw_out)
        else:
            dw_out, (other_in,) = _wgrad(merged, dx, f"wgrad_out_l{l}", l, into=dw_out, swap=(dw_in[1],))
        dw_pu = _wgrad(y_pool, dup, f"wgrad_pu_l{l}", l, into=dw_pu)
        dw_au = _wgrad(y_attn, dua, f"wgrad_au_l{l}", l, into=dw_au)
        if l > 0:
            dpw = _pool_wgrad(pooled, dmixed, l)
        else:
            dpw, (other_pu, other_au, other_out) = _pool_wgrad(pooled, dmixed, l,
                                                               swap=(dw_pu[1], dw_au[1], dw_out[1]))
        if l > 0:
            dx, dg, _ = _inproj_bwd(pieces, w_in, x_in, g, dx, l)
        else:
            pair = [_pair_sum(d, t, pos, f"grad_pair_sum_{n}") for d, t, n in
                    zip((dw_in[0], dw_pu[0], dw_au[0], dw_out[0]), (other_in, other_pu, other_au, other_out),
                        SHARDED_NAMES)]
            dx, dg, landed = _inproj_bwd(pieces, w_in, x_in, g, dx, l, exchange=[pb for _, pb in pair])
        small[l] = (dg[0], dbg[0], dpw, dsc[0])
    small = [jnp.stack([small[l][i] for l in range(n_layers)]) for i in range(4)]
    return loss[0, 0], dx, d_final_g[0], small, [p for p, _ in pair], landed


SHARDED = ((2, 1280), (2, 256), (2, 256), (1, 256))
SHARDED_NAMES = ("w_in", "w_pool_up", "w_attn_up", "w_out")
ANY = pl.BlockSpec(memory_space=pl.ANY)


def _part(ref, s, axis, width):
    sl = pl.ds(pl.multiple_of(s * width, width), width)
    return ref.at[:, sl] if axis == 2 else ref.at[sl, :]


def _place():
    x, y, c = lax.axis_index("x"), lax.axis_index("y"), lax.axis_index("c")
    return x, y, c, 2 * x + y


def _other_chip(x, y, m):
    px = 1 - x if m & 2 else x
    py = 1 - y if m & 1 else y
    return px, py, 2 * px + py


def _remote(src, dst, send, recv, k, to):
    return pltpu.make_async_remote_copy(src_ref=src, dst_ref=dst, send_sem=send.at[k], recv_sem=recv.at[k],
                                        device_id=to, device_id_type=MESH)


def _part_spec(tr, rows_s, cols_s, axis, width, lead):
    if axis == 2:
        return pl.BlockSpec((None, tr, width), lambda *a: (lead(a), a[-2], a[-1][1]))
    return pl.BlockSpec((None, tr, cols_s), lambda *a: (lead(a), a[-1][1] * (rows_s // tr) + a[-2], 0))


def _cast_into_place(w, pos, axis, width, name):
    L, Rs, Cs = w.shape
    tr = min(256, Rs)
    shape = [L, Rs, Cs]
    shape[axis] *= N_CHIPS

    def body(pos_ref, w_ref, o_ref):
        o_ref[...] = w_ref[...].astype(BF16)

    return pl.pallas_call(
        body, name=name,
        grid_spec=pltpu.PrefetchScalarGridSpec(
            num_scalar_prefetch=1, grid=(L, Rs // tr),
            in_specs=[pl.BlockSpec((None, tr, Cs), lambda l, i, pos: (l, i, 0))],
            out_specs=_part_spec(tr, Rs, Cs, axis, width, lambda a: a[0])),
        out_shape=jax.ShapeDtypeStruct(tuple(shape), BF16),
        compiler_params=_params(),
    )(pos, w)


def _w_in_half(layer):
    def piece(refs, who, shard):
        rows = pl.ds(pl.multiple_of(who * (D_MODEL // 2), D_MODEL // 2), D_MODEL // 2)
        return refs[0].at[layer, rows, pl.ds(pl.multiple_of(shard * SHARDED[0][1], SHARDED[0][1]), SHARDED[0][1])]
    return piece


def _whole_layer(a):
    def piece(refs, who, shard):
        return _part(refs[a].at[who], shard, *SHARDED[a])
    return piece


FIRST_PIECES = (_w_in_half(0),)
LATER_PIECES = (_whole_layer(1), _whole_layer(2), _whole_layer(3))
LAST_PIECES = (_w_in_half(1),)


class _Gather:
    def __init__(self, pieces, refs, send, recv):
        self.pieces, self.refs, self.send, self.recv = pieces, refs, send, recv
        self.x, self.y, self.c, s = _place()
        self.first = []
        for u, piece in enumerate(pieces):
            for m in (1, 2, 3):
                px, py, _ = _other_chip(self.x, self.y, m)
                own = piece(refs, self.c, s)
                self.first.append(_remote(own, own, send, recv, 3 * u + m - 1, (px, py, self.c)))

    def start(self):
        for cp in self.first:
            cp.start()

    def _landed(self, u, m, who):
        _, _, sp = _other_chip(self.x, self.y, m)
        return self.pieces[u](self.refs, who, sp)

    def _passed(self):
        n = len(self.pieces)
        return [_remote(self._landed(u, m, self.c), self._landed(u, m, self.c), self.send, self.recv,
                        3 * n + 3 * u + m - 1, (self.x, self.y, 1 - self.c)) for m in (1, 2, 3) for u in range(n)]

    def pass_on(self):
        me = (self.x, self.y, self.c)
        passed = iter(self._passed())
        for m in (1, 2, 3):
            for u in range(len(self.pieces)):
                got = self._landed(u, m, self.c)
                _remote(got, got, self.send, self.recv, 3 * u + m - 1, me).wait_recv()
                next(passed).start()

    def finish(self):
        n = len(self.pieces)
        for m in (1, 2, 3):
            for u in range(n):
                got = self._landed(u, m, 1 - self.c)
                _remote(got, got, self.send, self.recv, 3 * n + 3 * u + m - 1, (self.x, self.y, self.c)).wait_recv()
        for cp in self.first + self._passed():
            cp.wait_send()

    @staticmethod
    def semaphores(pieces):
        return [pltpu.SemaphoreType.DMA((6 * len(pieces),)), pltpu.SemaphoreType.DMA((6 * len(pieces),))]


def _gather_first(fulls):
    n = len(fulls)

    def body(*refs):
        gather = _Gather(FIRST_PIECES, refs[n:2 * n], *refs[2 * n:])
        gather.start()
        gather.pass_on()
        gather.finish()

    return pl.pallas_call(
        body, name="gather_first",
        in_specs=[ANY] * n, out_specs=[ANY] * n,
        out_shape=[jax.ShapeDtypeStruct(f.shape, f.dtype) for f in fulls],
        input_output_aliases={a: a for a in range(n)},
        scratch_shapes=_Gather.semaphores(FIRST_PIECES),
    )(*fulls)


def _pair_sum(dw, other, pos, name):
    _, R, C = dw.shape
    tr = 128 if C > 1024 else 256

    def body(pos_ref, a_ref, b_ref, o_ref, ob_ref):
        tot = a_ref[...] + b_ref[...].astype(F32)
        o_ref[...] = tot
        ob_ref[...] = tot.astype(BF16)

    blk = pl.BlockSpec((tr, C), lambda i, pos: (i, 0))
    return pl.pallas_call(
        body, name=name,
        grid_spec=pltpu.PrefetchScalarGridSpec(
            num_scalar_prefetch=1, grid=(R // tr,),
            in_specs=[pl.BlockSpec((None, tr, C), lambda i, pos: (pos[0], i, 0)), blk],
            out_specs=[blk, blk]),
        out_shape=[jax.ShapeDtypeStruct((R, C), F32), jax.ShapeDtypeStruct((R, C), BF16)],
        compiler_params=_params(),
    )(pos, dw, other)


class _ChipExchange:
    def __init__(self, mine, theirs, send, recv):
        x, y, c, _ = _place()
        self.copies = []
        for a, (axis, width) in enumerate(SHARDED):
            for m in (1, 2, 3):
                px, py, sp = _other_chip(x, y, m)
                self.copies.append(_remote(_part(mine[a], sp, axis, width), theirs[a].at[m - 1], send, recv,
                                           3 * a + m - 1, (px, py, c)))

    def start(self):
        for cp in self.copies:
            cp.start()

    def wait(self):
        for cp in self.copies:
            cp.wait()

    @staticmethod
    def landing(ps):
        shapes = []
        for p, (axis, width) in zip(ps, SHARDED):
            shape = [3] + list(p.shape)
            shape[axis] = width
            shapes.append(jax.ShapeDtypeStruct(tuple(shape), p.dtype))
        return shapes

    @staticmethod
    def semaphores(n):
        return [pltpu.SemaphoreType.DMA((3 * n,)), pltpu.SemaphoreType.DMA((3 * n,))]


def _shard_sum(p, landed, pos, axis, width, name):
    _, Rs, Cs = landed.shape
    tr = min(256, Rs)
    p_spec = _part_spec(tr, Rs, Cs, axis, width, lambda a: 0)

    def body(pos_ref, p_ref, l_ref, o_ref):
        o_ref[...] = ((p_ref[...] + l_ref[0].astype(F32)) + l_ref[1].astype(F32)) + l_ref[2].astype(F32)

    return pl.pallas_call(
        body, name=name,
        grid_spec=pltpu.PrefetchScalarGridSpec(
            num_scalar_prefetch=1, grid=(Rs // tr,),
            in_specs=[p_spec, pl.BlockSpec((3, tr, Cs), lambda i, pos: (0, i, 0))],
            out_specs=pl.BlockSpec((None, tr, Cs), lambda i, pos: (pos[0], i, 0))),
        out_shape=jax.ShapeDtypeStruct((2, Rs, Cs), F32),
        compiler_params=_params(),
    )(pos, p[None], landed)


def _final_exchange(gs, packed):
    n = len(gs)
    rows = packed.shape[0]
    half = rows // 2
    assert half % 8 == 0

    def body(*refs):
        small_ref = refs[n]
        outs, total_ref = refs[n + 1:2 * n + 1], refs[2 * n + 1]
        sib_ref, chips_ref, done_ref, send, recv, small_send, small_recv = refs[2 * n + 2:]
        x, y, c, s = _place()
        me, sibling = (x, y, c), (x, y, 1 - c)
        copies = [_remote(outs[a].at[c], outs[a].at[c], send, recv, a, sibling) for a in range(n)]
        for cp in copies:
            cp.start()

        def small(src, dst, k, to):
            return _remote(src, dst, small_send, small_recv, k, to)

        mine = small_ref.at[pl.ds(pl.multiple_of(c * half, 8), half)]
        theirs = small_ref.at[pl.ds(pl.multiple_of((1 - c) * half, 8), half)]
        to_sibling = small(theirs, sib_ref, 0, sibling)
        to_sibling.start()
        to_sibling.wait()
        chips_ref[s] = mine[...] + sib_ref[...]
        to_chips = []
        for m in (1, 2, 3):
            px, py, _ = _other_chip(x, y, m)
            to_chips.append(small(chips_ref.at[s], chips_ref.at[s], m, (px, py, c)))
            to_chips[-1].start()
        for m in (1, 2, 3):
            _, _, sp = _other_chip(x, y, m)
            small(chips_ref.at[sp], chips_ref.at[sp], m, me).wait_recv()
        done_ref[c] = ((chips_ref[0] + chips_ref[1]) + chips_ref[2]) + chips_ref[3]
        finished = small(done_ref.at[c], done_ref.at[c], 4, sibling)
        finished.start()
        small(done_ref.at[1 - c], done_ref.at[1 - c], 4, me).wait_recv()
        total_ref[:half] = done_ref[0]
        total_ref[half:] = done_ref[1]
        for cp in to_chips + [finished]:
            cp.wait_send()
        for a, cp in enumerate(copies):
            cp.wait_send()
            _remote(outs[a].at[1 - c], outs[a].at[1 - c], send, recv, a, me).wait_recv()

    vmem = pl.BlockSpec(memory_space=pltpu.VMEM)
    res = pl.pallas_call(
        body, name="final_exchange",
        in_specs=[ANY] * n + [vmem], out_specs=[ANY] * n + [vmem],
        out_shape=[jax.ShapeDtypeStruct(g.shape, g.dtype) for g in gs]
        + [jax.ShapeDtypeStruct(packed.shape, packed.dtype)],
        input_output_aliases={a: a for a in range(n)},
        scratch_shapes=[pltpu.VMEM((half, LANES), F32), pltpu.VMEM((N_CHIPS, half, LANES), F32),
                        pltpu.VMEM((2, half, LANES), F32),
                        pltpu.SemaphoreType.DMA((n,)), pltpu.SemaphoreType.DMA((n,)),
                        pltpu.SemaphoreType.DMA((5,)), pltpu.SemaphoreType.DMA((5,))],
        compiler_params=_params(),
    )(*gs, packed)
    return res[:n], res[n]


def _adamw(w, g, m, v, name):
    shape = w.shape
    C = shape[-1]
    flat = [t.reshape(-1, C) for t in (w, g, m, v)]
    R = flat[0].shape[0]
    tr = max(t for t in range(8, R + 1, 8) if R % t == 0 and t * C <= 384 * 1024)

    def body(w_ref, g_ref, m_ref, v_ref, d_ref, nm_ref, nv_ref):
        _adamw_update(w_ref, g_ref, m_ref, v_ref, d_ref, nm_ref, nv_ref)

    blk = pl.BlockSpec((tr, C), lambda i: (i, 0))
    out = jax.ShapeDtypeStruct((R, C), F32)
    res = pl.pallas_call(
        body, name=name, grid=(R // tr,),
        in_specs=[blk] * 4, out_specs=[blk] * 3, out_shape=[out] * 3,
        compiler_params=_params(),
    )(*flat)
    return [t.reshape(shape) for t in res]


def _adamw_update(w_ref, g_ref, m_ref, v_ref, d_ref, nm_ref, nv_ref):
    gv = g_ref[...]
    nm = ADAM_B1 * m_ref[...] + (1.0 - ADAM_B1) * gv
    nv = ADAM_B2 * v_ref[...] + (1.0 - ADAM_B2) * (gv * gv)
    m_hat = nm / (1.0 - ADAM_B1 ** ADAM_STEP)
    v_hat = nv / (1.0 - ADAM_B2 ** ADAM_STEP)
    d_ref[...] = -ADAM_LR * (m_hat / (jnp.sqrt(v_hat) + ADAM_EPS) + ADAM_WD * w_ref[...])
    nm_ref[...] = nm
    nv_ref[...] = nv


def _adamw_small(ws, gs, ms, vs):
    n = len(ws)
    flat = lambda ts: [t.reshape(-1, t.shape[-1]) for t in ts]

    def body(*refs):
        for a in range(n):
            _adamw_update(*[refs[k * n + a] for k in range(7)])

    vmem = pl.BlockSpec(memory_space=pltpu.VMEM)
    shapes = [jax.ShapeDtypeStruct(w.shape, F32) for w in flat(ws)]
    res = pl.pallas_call(
        body, name="adamw_small",
        in_specs=[vmem] * (4 * n), out_specs=[vmem] * (3 * n), out_shape=shapes * 3,
        compiler_params=_params(),
    )(*flat(ws), *flat(gs), *flat(ms), *flat(vs))
    return [[r.reshape(w.shape) for r, w in zip(res[k * n:(k + 1) * n], ws)] for k in range(3)]


SMALL_SHAPES = ((2, 1024), (2, 2048), (2, 4, 128, 128), (2, 512), (1024,))


def _pack_small(parts):
    return jnp.concatenate([p.reshape(-1, LANES) for p in parts], axis=0)


def _unpack_small(packed):
    out, row = [], 0
    for shape in SMALL_SHAPES:
        n = 1
        for d in shape:
            n *= d
        out.append(packed[row:row + n // LANES].reshape(shape))
        row += n // LANES
    return out


def kernel(x, norm_g, w_in, b_gate, pool_w, pool_scale, w_pool_up, w_attn_up, w_out, final_g, loss_target, m_norm_g, m_w_in, m_b_gate, m_pool_w, m_pool_scale, m_w_pool_up, m_w_attn_up, m_w_out, m_final_g, v_norm_g, v_w_in, v_b_gate, v_pool_w, v_pool_scale, v_w_pool_up, v_w_attn_up, v_w_out, v_final_g):
    _, _, c, s = _place()
    pos = jnp.stack([c, s]).astype(jnp.int32)
    names = SHARDED_NAMES

    weights = _gather_first([_cast_into_place(w, pos, axis, width, f"cast_{n}")
                             for w, (axis, width), n in zip((w_in, w_pool_up, w_attn_up, w_out), SHARDED, names)])
    loss_part, dx, d_final_g, small, pair, landed = _local_step(x[0], loss_target[0], norm_g, b_gate, pool_w,
                                                                pool_scale, final_g, weights, pos)
    mine = [_shard_sum(p, l, pos, axis, width, f"grad_shard_sum_{n}")
            for p, l, (axis, width), n in zip(pair, landed, SHARDED, names)]
    (g_in, g_pu, g_au, g_out), summed = _final_exchange(
        mine, _pack_small(small + [d_final_g, jnp.broadcast_to(loss_part, (16, LANES))]))
    g_small = _unpack_small(summed)
    loss = summed[-16, 0]
    d_small, nm_small, nv_small = _adamw_small([norm_g, b_gate, pool_w, pool_scale, final_g], g_small,
                                               [m_norm_g, m_b_gate, m_pool_w, m_pool_scale, m_final_g],
                                               [v_norm_g, v_b_gate, v_pool_w, v_pool_scale, v_final_g])
    upd_in = _adamw(w_in, g_in, m_w_in, v_w_in, "adamw_w_in")
    upd_pu = _adamw(w_pool_up, g_pu, m_w_pool_up, v_w_pool_up, "adamw_w_pool_up")
    upd_au = _adamw(w_attn_up, g_au, m_w_attn_up, v_w_attn_up, "adamw_w_attn_up")
    upd_out = _adamw(w_out, g_out, m_w_out, v_w_out, "adamw_w_out")

    def ordered(sm, k):
        big = (upd_in[k], upd_pu[k], upd_au[k], upd_out[k]) if k is not None else (g_in, g_pu, g_au, g_out)
        return [sm[0], big[0], sm[1], sm[2], sm[3], big[1], big[2], big[3], sm[4]]

    return (loss, dx[None], *ordered(g_small, None), *ordered(d_small, 0), *ordered(nm_small, 1),
            *ordered(nv_small, 2))
```

```python
import jax
import jax.numpy as jnp
import numpy as np
from jax import lax
from jax.experimental import pallas as pl
from jax.experimental.pallas import tpu as pltpu

F32 = jnp.float32
BF16 = jnp.bfloat16
MESH = pl.DeviceIdType.MESH

D_MODEL = 1024
POOL_WIDTH = 512
POOL_WINDOWS = (2, 4, 8, 16)
POOL_GROUP = 128
POOL_HALO = 16
ATTN_WIDTH = 512
HEAD_DIM = 64
HEAD_PAIRS = 4
IN_WIDTH = 5120
N_CHIPS = 4
RMS_EPS = 1e-6
C_U, C_ZP, C_Q, C_K, C_V, C_ZA, C_GL = 0, 512, 1024, 1536, 2048, 2560, 3072

ADAM_LR, ADAM_B1, ADAM_B2, ADAM_EPS, ADAM_WD, ADAM_STEP = 0.001, 0.9, 0.999, 1e-08, 0.01, 10

LANES = 128
ATTN_BLOCK = 256
QUERY_BLOCKS = 4
ROW_TILE = 256
PROJ_ROW_TILE = 512
VMEM_LIMIT = 56 * 1024 * 1024


def _params(**kw):
    return pltpu.CompilerParams(vmem_limit_bytes=VMEM_LIMIT, **kw)


def _nt(a, b):
    return lax.dot_general(a, b, (((1,), (1,)), ((), ())), preferred_element_type=F32)


def _tn(a, b):
    return lax.dot_general(a, b, (((0,), (0,)), ((), ())), preferred_element_type=F32)


def _nn(a, b):
    return jnp.dot(a, b, preferred_element_type=F32)


def _sigmoid(z):
    return 1.0 / (1.0 + jnp.exp(-z))


def _rms_inproj(x, g, w_in, layer, gather=()):
    S = x.shape[0]
    tm = min(PROJ_ROW_TILE, S)
    n_tiles = S // tm
    n_g = len(gather)

    def body(*refs):
        x_ref, g_ref = refs[:2]
        if n_g:
            (u_ref, zp_ref, q_ref, k_ref, v_ref, za_ref, gl_ref, h_ref) = refs[2 + n_g:10 + n_g]
            fulls = refs[10 + n_g:10 + 2 * n_g]
            w_ref, load_sem, send, recv = refs[10 + 2 * n_g:]
            later = _Gather(LATER_PIECES, fulls, send, recv)

            @pl.when(pl.program_id(0) == 0)
            def _():
                load = pltpu.make_async_copy(fulls[0].at[layer], w_ref, load_sem)
                load.start()
                later.start()
                load.wait()

            pl.when(pl.program_id(0) == n_tiles - 1)(later.pass_on)
        else:
            w_ref, u_ref, zp_ref, q_ref, k_ref, v_ref, za_ref, gl_ref, h_ref = refs[2:]
        xv = x_ref[...]
        r = lax.rsqrt(jnp.mean(xv * xv, axis=-1, keepdims=True) + RMS_EPS)
        h = ((xv * r) * g_ref[...]).astype(BF16)
        h_ref[...] = h

        def mm(c0, n):
            return _nn(h, w_ref[:, c0:c0 + n])

        u_ref[...] = mm(C_U, 512)
        zp_ref[...] = mm(C_ZP, 512).astype(BF16)
        q_ref[...] = (mm(C_Q, 512) * 0.125).astype(BF16)
        k_ref[...] = mm(C_K, 512).astype(BF16)
        v_ref[...] = mm(C_V, 512).astype(BF16)
        za_ref[...] = mm(C_ZA, 512).astype(BF16)
        for c in range(4):
            gl_ref[:, c * 512:(c + 1) * 512] = mm(C_GL + c * 512, 512).astype(BF16)
        if n_g:
            pl.when(pl.program_id(0) == n_tiles - 1)(later.finish)

    row = lambda n: pl.BlockSpec((tm, n), lambda i: (i, 0))
    sd = lambda n, dt: jax.ShapeDtypeStruct((S, n), dt)
    any_space = pl.BlockSpec(memory_space=pl.ANY)
    weights = [any_space] * n_g if n_g else [_layer_weight_spec(D_MODEL, IN_WIDTH, layer)]
    res = pl.pallas_call(
        body, name=f"rms_inproj_l{layer}", grid=(n_tiles,),
        in_specs=[row(D_MODEL), pl.BlockSpec((1, D_MODEL), lambda i: (0, 0))] + weights,
        out_specs=[row(512), row(512), row(512), row(512), row(512), row(512), row(2048), row(D_MODEL)]
        + [any_space] * n_g,
        out_shape=[sd(512, F32), sd(512, BF16), sd(512, BF16), sd(512, BF16), sd(512, BF16), sd(512, BF16),
                   sd(2048, BF16), sd(D_MODEL, BF16)] + [jax.ShapeDtypeStruct(f.shape, f.dtype) for f in gather],
        input_output_aliases={2 + a: 8 + a for a in range(n_g)},
        scratch_shapes=([pltpu.VMEM((D_MODEL, IN_WIDTH), BF16), pltpu.SemaphoreType.DMA(())]
                        + _Gather.semaphores(LATER_PIECES)) if n_g else [],
        compiler_params=_params(),
    )(x, g, *(gather if n_g else (w_in,)))
    return res[:8], res[8:]


def _tri(n, strict_lower):
    r = lax.broadcasted_iota(jnp.int32, (n, n), 0)
    c = lax.broadcasted_iota(jnp.int32, (n, n), 1)
    return jnp.where(r > c if strict_lower else r < c, 1.0, 0.0).astype(BF16)


def _split_dot(x, m):
    hi = x.astype(BF16)
    lo = (x - hi.astype(F32)).astype(BF16)
    return _nn(hi, m) + _nn(lo, m)


def _log_terms(z):
    lg = jnp.log(1.0 + jnp.exp(-jnp.abs(z)))
    a = jnp.minimum(z, 0.0) - lg
    return a, a - z


EXHAUSTED = -104.0
UNREACHED = -1e30


class _HeadPair:
    def __init__(self, T):
        self.T = T
        self.first = lax.broadcasted_iota(jnp.int32, (T, LANES), 1) < HEAD_DIM
        self.lane = lax.broadcasted_iota(jnp.int32, (2 * T, LANES), 1)
        row = lax.broadcasted_iota(jnp.int32, (2 * T, T), 0)
        row = jnp.where(row >= T, row - T, row)
        self.causal = row > lax.broadcasted_iota(jnp.int32, (2 * T, T), 1)
        self.below = _tri(T, True)

    def stack(self, x2):
        return jnp.concatenate([jnp.where(self.first, x2, 0), jnp.where(self.first, 0, x2)], axis=0).astype(BF16)

    def unstack(self, x):
        return jnp.where(self.first, x[:self.T], x[self.T:])

    def keys(self, ref, blocks):
        T = self.T
        return jnp.concatenate([ref[pl.ds(pl.multiple_of(j * T, T), T), :] for j, _ in blocks], axis=0)

    def log_terms(self, z, blocks):
        T = self.T
        a_all, l_all = _log_terms(z)
        a = [a_all[:, b * T:(b + 1) * T] for b in range(len(blocks))]
        l1m = [l_all[:, b * T:(b + 1) * T] for b in range(len(blocks))]
        return a, [jnp.where(self.causal, l, 0.0) if diagonal else l for l, (_, diagonal) in zip(l1m, blocks)]

    def later_sums(self, l1m):
        later = _split_dot(jnp.concatenate(l1m, axis=0), self.below)
        return [later[2 * self.T * b:2 * self.T * (b + 1)] for b in range(len(l1m))]


def _sections(x, n):
    return x.reshape(n, x.shape[0] // n, x.shape[1])


def _attn_fwd(q, k, v, layer, gather=()):
    S = q.shape[0]
    T = min(ATTN_BLOCK, S)
    nq = S // T
    jobs = min(QUERY_BLOCKS, nq)
    assert nq <= LANES and nq % jobs == 0
    per_job = nq // jobs

    n_g = len(gather)

    def body(q_ref, k_ref, v_ref, *rest):
        o_ref, c_ref = rest[n_g:n_g + 2]
        i = pl.program_id(1)
        if n_g:
            last = _Gather(LAST_PIECES, rest[n_g + 2:2 * n_g + 2], *rest[2 * n_g + 2:])
            pl.when(jnp.logical_and(pl.program_id(0) == 0, i == 0))(last.start)
        pair = _HeadPair(T)
        qs = [pair.stack(q_ref[n]) for n in range(jobs)]
        diag = [i + n * per_job for n in range(jobs)]

        def sweep(jobs):
            kv = [(pair.keys(k_ref, bl), pair.keys(v_ref, bl)) for _, bl, _ in jobs]
            zs = [_nt(qs[n], kcat) for (n, _, _), (kcat, _) in zip(jobs, kv)]
            terms = [pair.log_terms(z, bl) for (_, bl, _), z in zip(jobs, zs)]
            laters = [pair.later_sums(l1m) for _, l1m in terms]
            weights = []
            for (_, bl, (acc, run, saved)), (a, l1m), later in zip(jobs, terms, laters):
                ws = []
                for b, (j, diagonal) in enumerate(bl):
                    saved = jnp.where(pair.lane == j, run, saved)
                    w = jnp.exp(a[b] + later[b] + run)
                    ws.append(jnp.where(pair.causal, w, 0.0) if diagonal else w)
                    run = run + jnp.sum(l1m[b], axis=1, keepdims=True)
                weights.append((jnp.concatenate(ws, axis=1).astype(BF16), acc, run, saved))
            return [(acc + _nn(w, vcat), run, saved) for (w, acc, run, saved), (_, vcat) in zip(weights, kv)]

        def alive(carry):
            return (jnp.max(carry[1]) > EXHAUSTED).astype(jnp.int32)

        def older_blocks(n, carry):
            def older_block(state):
                j, _, c = state
                c = sweep([(n, [(j, False)], c)])[0]
                return j - 1, alive(c), c

            return lax.while_loop(lambda st: jnp.logical_and(st[0] >= 0, st[1] > 0), older_block,
                                  (diag[n] - 2, alive(carry), carry))[2]

        def run(first_blocks):
            init = (jnp.zeros((2 * T, LANES), F32), jnp.zeros((2 * T, 1), F32),
                    jnp.full((2 * T, LANES), UNREACHED, F32))
            carries = sweep([(n, first_blocks[n], init) for n in range(jobs)])
            for n in range(jobs):
                acc, _, saved = older_blocks(n, carries[n])
                o_ref[n] = pair.unstack(acc).astype(BF16)
                c_ref[n, :, :LANES] = saved[:T]
                c_ref[n, :, LANES:] = saved[T:]

        with_previous = lambda d: [(d, True), (d - 1, False)]

        @pl.when(i == 0)
        def _():
            run([[(diag[0], True)]] + [with_previous(d) for d in diag[1:]])

        @pl.when(i > 0)
        def _():
            run([with_previous(d) for d in diag])

        if n_g:
            pl.when(jnp.logical_and(pl.program_id(0) == HEAD_PAIRS - 1, i == 0))(last.pass_on)
            pl.when(jnp.logical_and(pl.program_id(0) == HEAD_PAIRS - 1, i == per_job - 1))(last.finish)

    blk = lambda n: pl.BlockSpec((jobs, T, n), lambda p, i: (0, i, p))
    full = pl.BlockSpec((S, LANES), lambda p, i: (0, p))
    any_space = pl.BlockSpec(memory_space=pl.ANY)
    res = pl.pallas_call(
        body, name=f"attn_fwd_l{layer}", grid=(HEAD_PAIRS, per_job),
        in_specs=[blk(LANES), full, full] + [any_space] * n_g,
        out_specs=[blk(LANES), blk(2 * LANES)] + [any_space] * n_g,
        out_shape=[jax.ShapeDtypeStruct((jobs, S // jobs, ATTN_WIDTH), BF16),
                   jax.ShapeDtypeStruct((jobs, S // jobs, 8 * LANES), F32)]
        + [jax.ShapeDtypeStruct(f.shape, f.dtype) for f in gather],
        input_output_aliases={3 + a: 2 + a for a in range(n_g)},
        scratch_shapes=_Gather.semaphores(LAST_PIECES) if n_g else [],
        compiler_params=_params(),
    )(_sections(q, jobs), k, v, *gather)
    return res[0].reshape(S, ATTN_WIDTH), res[1].reshape(S, 8 * LANES), res[2:]


def _attn_bwd(q, k, v, saved, do, layer):
    S = q.shape[0]
    T = min(ATTN_BLOCK, S)
    nq = S // T
    jobs = min(QUERY_BLOCKS, nq)
    per_job = nq // jobs

    def body(q_ref, k_ref, v_ref, c_ref, do_ref, dq_ref, dk_ref, dv_ref):
        i = pl.program_id(1)

        @pl.when(i == 0)
        def _():
            dk_ref[...] = jnp.zeros_like(dk_ref)
            dv_ref[...] = jnp.zeros_like(dv_ref)

        pair = _HeadPair(T)
        diag = [i + n * per_job for n in range(jobs)]
        qs = [pair.stack(q_ref[n]) for n in range(jobs)]
        dos = [pair.stack(do_ref[n].astype(BF16)) for n in range(jobs)]
        saved = [jnp.concatenate([c_ref[n, :, :LANES], c_ref[n, :, LANES:]], axis=0) for n in range(jobs)]
        before = _tri(T, False)

        def sweep(jobs):
            kv = [(pair.keys(k_ref, bl), pair.keys(v_ref, bl)) for _, bl, _ in jobs]
            zs = [_nt(qs[n], kcat) for (n, _, _), (kcat, _) in zip(jobs, kv)]
            gs = [_nt(dos[n], vcat) for (n, _, _), (_, vcat) in zip(jobs, kv)]
            terms = [pair.log_terms(z, bl) for (_, bl, _), z in zip(jobs, zs)]
            laters = [pair.later_sums(l1m) for _, l1m in terms]
            ws, es = [], []
            for (n, bl, _), (a, _), later, g in zip(jobs, terms, laters, gs):
                w_job, e_job = [], []
                for b, (j, diagonal) in enumerate(bl):
                    run = jnp.sum(jnp.where(pair.lane == j, saved[n], 0.0), axis=1, keepdims=True)
                    w = jnp.exp(a[b] + later[b] + run)
                    w_job.append(jnp.where(pair.causal, w, 0.0) if diagonal else w)
                    e_job.append(w_job[b] * g[:, b * T:(b + 1) * T])
                ws.append(w_job)
                es.append(e_job)
            prefixes = [_nn(jnp.concatenate(e_job, axis=0).astype(BF16), before) for e_job in es]
            dzs, olders = [], []
            for (_, bl, (_, older)), (a, _), e_job, prefix in zip(jobs, terms, es, prefixes):
                dz_job = []
                for b, (j, diagonal) in enumerate(bl):
                    dz = e_job[b] - jnp.exp(a[b]) * (e_job[b] + (prefix[2 * T * b:2 * T * (b + 1)] + older))
                    dz_job.append(jnp.where(pair.causal, dz, 0.0) if diagonal else dz)
                    older = older + jnp.sum(e_job[b], axis=1, keepdims=True)
                dzs.append(jnp.concatenate(dz_job, axis=1).astype(BF16))
                olders.append(older)
            out = []
            for (n, bl, (dq, _)), dz, w_job, older, (kcat, _) in zip(jobs, dzs, ws, olders, kv):
                dk = _tn(dz, qs[n])
                dv = _tn(jnp.concatenate(w_job, axis=1).astype(BF16), dos[n])
                for b, (j, _) in enumerate(bl):
                    rows = pl.ds(pl.multiple_of(j * T, T), T)
                    dk_ref[rows, :] += dk[b * T:(b + 1) * T]
                    dv_ref[rows, :] += dv[b * T:(b + 1) * T]
                out.append((dq + _nn(dz, kcat), older))
            return out

        def older_blocks(n):
            col_max = jnp.max(saved[n], axis=0, keepdims=True)
            lane_row = lax.broadcasted_iota(jnp.int32, (1, LANES), 1)
            reached = jnp.sum(jnp.where(jnp.logical_and(col_max > EXHAUSTED, lane_row < diag[n]), 1, 0))
            init = (jnp.zeros((2 * T, LANES), F32), jnp.zeros((2 * T, 1), F32))
            return lax.fori_loop(diag[n] - reached, diag[n] - 1, lambda j, c: sweep([(n, [(j, False)], c)])[0], init)

        def run(last_blocks):
            carries = sweep([(n, last_blocks[n], older_blocks(n)) for n in range(jobs)])
            for n in range(jobs):
                dq_ref[n] = (pair.unstack(carries[n][0]) * 0.125).astype(BF16)

        with_previous = lambda d: [(d - 1, False), (d, True)]

        @pl.when(i == 0)
        def _():
            run([[(diag[0], True)]] + [with_previous(d) for d in diag[1:]])

        @pl.when(i > 0)
        def _():
            run([with_previous(d) for d in diag])

    blk = lambda n: pl.BlockSpec((jobs, T, n), lambda p, i: (0, i, p))
    full = pl.BlockSpec((S, LANES), lambda p, i: (0, p))
    out = jax.ShapeDtypeStruct((S, ATTN_WIDTH), F32)
    dq, dk, dv = pl.pallas_call(
        body, name=f"attn_bwd_l{layer}", grid=(HEAD_PAIRS, per_job),
        in_specs=[blk(LANES), full, full, blk(2 * LANES), blk(LANES)],
        out_specs=[blk(LANES), full, full],
        out_shape=[jax.ShapeDtypeStruct((jobs, S // jobs, ATTN_WIDTH), BF16), out, out],
        compiler_params=_params(),
    )(_sections(q, jobs), k, v, _sections(saved, jobs), _sections(do, jobs))
    return dq.reshape(S, ATTN_WIDTH), dk, dv


def _pool_counts(row0, tm):
    pos = row0 + lax.broadcasted_iota(jnp.int32, (tm, 1), 0)
    return [1.0 / jnp.minimum(pos + 1, w).astype(F32) for w in POOL_WINDOWS]


def _window_bands(tm, backward):
    t = np.arange(tm)[:, None]
    c = np.arange(tm)[None, :]
    off = c - t if backward else t - c
    main = np.stack([(off >= 0) & (off < w) for w in POOL_WINDOWS])
    r = np.arange(POOL_HALO)[:, None]
    h = np.arange(POOL_HALO)[None, :]
    off = h - r + POOL_HALO if backward else r - h + POOL_HALO
    edge = np.concatenate([(off < w) for w in POOL_WINDOWS])
    return jnp.asarray(main, BF16), jnp.asarray(edge, BF16)


def _window_sums(tile, beside, main_ref, edge_ref, backward):
    tm = tile.shape[0]
    tb = tile.astype(BF16)
    edge = _nn(edge_ref[...], beside.astype(BF16))
    sums = []
    for g in range(len(POOL_WINDOWS)):
        cols = slice(g * POOL_GROUP, (g + 1) * POOL_GROUP)
        tot = _nn(main_ref[g], tb[:, cols])
        extra = edge[g * POOL_HALO:(g + 1) * POOL_HALO, cols]
        if backward:
            sums.append(jnp.concatenate([tot[:tm - POOL_HALO], tot[tm - POOL_HALO:] + extra], axis=0))
        else:
            sums.append(jnp.concatenate([tot[:POOL_HALO] + extra, tot[POOL_HALO:]], axis=0))
    return sums


def _post_forward(u, history, bands, inv_cnt, zp, o, za, gl, bg, pw_ref, scale, wpu_ref, wau_ref):
    pooled, mixed = [], []
    for g, tot in enumerate(_window_sums(u, history, *bands, False)):
        pg = (tot * inv_cnt[g] - u[:, g * POOL_GROUP:(g + 1) * POOL_GROUP]).astype(BF16)
        pooled.append(pg)
        mixed.append(_nn(pg, pw_ref[g].astype(BF16)))
    pooled = jnp.concatenate(pooled, axis=1)
    mixed = jnp.concatenate(mixed, axis=1)
    zp, za, o = zp.astype(F32), za.astype(F32), o.astype(F32)
    sp = _sigmoid(zp)
    sa = _sigmoid(za)
    y_pool = (mixed * scale) * (zp * sp)
    y_attn = o * (za * sa)
    gate = _sigmoid(gl + bg)
    g0, g1 = gate[:, :D_MODEL], gate[:, D_MODEL:]
    up_p = _nn(y_pool.astype(BF16), wpu_ref[...])
    up_a = _nn(y_attn.astype(BF16), wau_ref[...])
    merged = g0 * up_p + g1 * up_a
    return pooled, mixed, sp, sa, y_pool, y_attn, g0, g1, up_p, up_a, merged


def _row_specs(tm, rev, n_tiles):
    tile_of = (lambda i: n_tiles - 1 - i) if rev else (lambda i: i)
    row = lambda n: pl.BlockSpec((tm, n), lambda i: (tile_of(i), 0))
    halo = pl.BlockSpec((POOL_HALO, POOL_WIDTH),
                        lambda i: (jnp.maximum(tile_of(i) * (tm // POOL_HALO) - 1, 0), 0))
    const = lambda shape: pl.BlockSpec(shape, lambda i: (0,) * len(shape))
    return tile_of, row, halo, const


def _layer_weight_spec(rows, cols, layer):
    return pl.BlockSpec((None, rows, cols), lambda i: (layer, 0, 0), pipeline_mode=pl.Buffered(1))


def _post_fwd(x, u, zp, o, za, gl, bg, pw, scale, wpu, wau, wout, layer, head=()):
    S = x.shape[0]
    tm = min(ROW_TILE, S)
    n_tiles = S // tm
    tile_of, row, halo, const = _row_specs(tm, False, n_tiles)

    def body(x_ref, u_ref, uh_ref, main_ref, edge_ref, zp_ref, o_ref, za_ref, gl_ref, bg_ref, pw_ref, sc_ref, wpu_ref,
             wau_ref, wout_ref, *rest):
        i = pl.program_id(0)
        vals = _post_forward(u_ref[...], jnp.where(i == 0, 0.0, uh_ref[...]), (main_ref, edge_ref),
                             _pool_counts(i * tm, tm), zp_ref[...], o_ref[...], za_ref[...], gl_ref[...], bg_ref[...],
                             pw_ref, sc_ref[...], wpu_ref, wau_ref)
        xv = x_ref[...] + _nn(vals[-1].astype(BF16), wout_ref[...])
        if not head:
            rest[0][...] = xv
            return
        gf_ref, t_ref, loss_ref, dx_ref, dg_ref = rest

        @pl.when(i == 0)
        def _():
            loss_ref[...] = jnp.zeros_like(loss_ref)
            dg_ref[...] = jnp.zeros_like(dg_ref)

        r = lax.rsqrt(jnp.mean(xv * xv, axis=-1, keepdims=True) + RMS_EPS)
        diff = (xv * r) * gf_ref[...] - t_ref[...]
        per_row = jnp.mean(diff * diff, axis=-1, keepdims=True)
        loss_ref[...] += 0.5 * jnp.sum(per_row, axis=0, keepdims=True)
        dx, dg_rows = _rms_backward(diff * (1.0 / D_MODEL), xv, r, gf_ref[...])
        dx_ref[...] = dx
        dg_ref[...] += jnp.sum(dg_rows, axis=0, keepdims=True)

    out = jax.ShapeDtypeStruct((S, D_MODEL), F32)
    return pl.pallas_call(
        body, name=f"post_fwd_l{layer}", grid=(n_tiles,),
        in_specs=[row(D_MODEL), row(512), halo, const((4, tm, tm)), const((4 * POOL_HALO, POOL_HALO)), row(512),
                  row(512), row(512), row(2048), const((1, 2048)), const((4, POOL_GROUP, POOL_GROUP)),
                  const((1, POOL_WIDTH)),
                  _layer_weight_spec(POOL_WIDTH, D_MODEL, layer), _layer_weight_spec(ATTN_WIDTH, D_MODEL, layer),
                  _layer_weight_spec(D_MODEL, D_MODEL, layer)] + ([const((1, D_MODEL)), row(D_MODEL)] if head else []),
        out_specs=[const((1, LANES)), row(D_MODEL), const((1, D_MODEL))] if head else row(D_MODEL),
        out_shape=[jax.ShapeDtypeStruct((1, LANES), F32), out, jax.ShapeDtypeStruct((1, D_MODEL), F32)] if head else out,
        compiler_params=_params(),
    )(x, u, u, *_window_bands(tm, False), zp, o, za, gl, bg, pw, scale, wpu, wau, wout, *head)


def _post_bwd(dx, u, zp, o, za, gl, bg, pw, scale, wpu, wau, wout, layer):
    S = dx.shape[0]
    tm = min(ROW_TILE, S)
    n_tiles = S // tm
    tile_of, row, halo, const = _row_specs(tm, True, n_tiles)

    def body(dx_ref, u_ref, uh_ref, main_ref, edge_ref, back_main_ref, back_edge_ref, zp_ref, o_ref, za_ref, gl_ref,
             bg_ref, pw_ref, sc_ref, wpu_ref, wau_ref, wout_ref,
             duz_ref, do_ref, dzg_ref, dsc_ref, dbg_ref,
             merged_ref, dup_ref, dua_ref, yp_ref, ya_ref, pooled_ref, dmixed_ref, nxt_ref):
        step = pl.program_id(0)
        i = tile_of(step)

        @pl.when(step == 0)
        def _():
            dsc_ref[...] = jnp.zeros_like(dsc_ref)
            dbg_ref[...] = jnp.zeros_like(dbg_ref)
            nxt_ref[...] = jnp.zeros_like(nxt_ref)

        inv_cnt = _pool_counts(i * tm, tm)
        zp, za, o = zp_ref[...].astype(F32), za_ref[...].astype(F32), o_ref[...].astype(F32)
        pooled, mixed, sp, sa, y_pool, y_attn, g0, g1, up_p, up_a, merged = _post_forward(
            u_ref[...], jnp.where(i == 0, 0.0, uh_ref[...]), (main_ref, edge_ref), inv_cnt, zp, o, za, gl_ref[...],
            bg_ref[...], pw_ref, sc_ref[...], wpu_ref, wau_ref)
        merged_ref[...] = merged.astype(BF16)
        yp_ref[...] = y_pool.astype(BF16)
        ya_ref[...] = y_attn.astype(BF16)
        pooled_ref[...] = pooled

        dmerged = _nt(dx_ref[...].astype(BF16), wout_ref[...])
        dup = (dmerged * g0).astype(BF16)
        dua = (dmerged * g1).astype(BF16)
        dup_ref[...] = dup
        dua_ref[...] = dua
        dgl0 = (dmerged * up_p) * (g0 * (1.0 - g0))
        dgl1 = (dmerged * up_a) * (g1 * (1.0 - g1))
        dzg_ref[:, ATTN_WIDTH:ATTN_WIDTH + D_MODEL] = dgl0.astype(BF16)
        dzg_ref[:, ATTN_WIDTH + D_MODEL:] = dgl1.astype(BF16)
        dbg_ref[:, :D_MODEL] += jnp.sum(dgl0, axis=0, keepdims=True)
        dbg_ref[:, D_MODEL:] += jnp.sum(dgl1, axis=0, keepdims=True)

        dy_attn = _nt(dua, wau_ref[...])
        do_ref[...] = (dy_attn * (za * sa)).astype(BF16)
        dzg_ref[:, :ATTN_WIDTH] = ((dy_attn * o) * (sa * (1.0 + za * (1.0 - sa)))).astype(BF16)

        dy_pool = _nt(dup, wpu_ref[...])
        ms = mixed * sc_ref[...]
        dms = dy_pool * (zp * sp)
        duz_ref[:, POOL_WIDTH:] = ((dy_pool * ms) * (sp * (1.0 + zp * (1.0 - sp)))).astype(BF16)
        dsc_ref[...] += jnp.sum(dms * mixed, axis=0, keepdims=True)
        dmixed = (dms * sc_ref[...]).astype(BF16)
        dmixed_ref[...] = dmixed
        dpooled = [_nt(dmixed[:, g * POOL_GROUP:(g + 1) * POOL_GROUP], pw_ref[g].astype(BF16)) for g in range(4)]
        scaled = jnp.concatenate([d * inv for d, inv in zip(dpooled, inv_cnt)], axis=1)
        for g, tot in enumerate(_window_sums(scaled, nxt_ref[...], back_main_ref, back_edge_ref, True)):
            duz_ref[:, g * POOL_GROUP:(g + 1) * POOL_GROUP] = (tot - dpooled[g]).astype(BF16)
        nxt_ref[...] = scaled[:POOL_HALO]

    sd = lambda n, dt: jax.ShapeDtypeStruct((S, n), dt)
    bands = [const((4, tm, tm)), const((4 * POOL_HALO, POOL_HALO))]
    return pl.pallas_call(
        body, name=f"post_bwd_l{layer}", grid=(n_tiles,),
        in_specs=[row(D_MODEL), row(512), halo, *bands, *bands, row(512), row(512), row(512), row(2048),
                  const((1, 2048)), const((4, POOL_GROUP, POOL_GROUP)), const((1, POOL_WIDTH)),
                  _layer_weight_spec(POOL_WIDTH, D_MODEL, layer), _layer_weight_spec(ATTN_WIDTH, D_MODEL, layer),
                  _layer_weight_spec(D_MODEL, D_MODEL, layer)],
        out_specs=[row(1024), row(512), row(2560), const((1, POOL_WIDTH)), const((1, 2048)),
                   row(D_MODEL), row(D_MODEL), row(D_MODEL), row(512), row(512), row(512), row(512)],
        out_shape=[sd(1024, BF16), sd(512, BF16), sd(2560, BF16),
                   jax.ShapeDtypeStruct((1, POOL_WIDTH), F32), jax.ShapeDtypeStruct((1, 2048), F32),
                   sd(D_MODEL, BF16), sd(D_MODEL, BF16), sd(D_MODEL, BF16), sd(512, BF16), sd(512, BF16),
                   sd(512, BF16), sd(512, BF16)],
        scratch_shapes=[pltpu.VMEM((POOL_HALO, POOL_WIDTH), F32)],
        compiler_params=_params(),
    )(dx, u, u, *_window_bands(tm, False), *_window_bands(tm, True), zp, o, za, gl, bg, pw, scale, wpu, wau, wout)


def _rms_backward(dh, xv, r, g):
    xhat = xv * r
    dxhat = dh * g
    return r * (dxhat - xhat * jnp.mean(dxhat * xhat, axis=-1, keepdims=True)), dh * xhat


def _inproj_bwd(pieces, w_in, x, g, dx_res, layer, exchange=()):
    S = x.shape[0]
    tm = min(PROJ_ROW_TILE, S)
    cols = [(c0, p.shape[1]) for p, c0 in pieces]

    def body(ins, outs):
        piece_refs = ins[:len(cols)]
        w_ref, x_ref, g_ref, res_ref = ins[len(cols):]
        dx_ref, dg_ref = outs

        @pl.when(pl.program_id(0) == 0)
        def _():
            dg_ref[...] = jnp.zeros_like(dg_ref)

        dh = jnp.zeros((tm, D_MODEL), F32)
        for p_ref, (c0, n) in zip(piece_refs, cols):
            for c in range(0, n, 512):
                dh = dh + _nt(p_ref[:, c:c + 512].astype(BF16), w_ref[:, c0 + c:c0 + c + 512])
        xv = x_ref[...]
        r = lax.rsqrt(jnp.mean(xv * xv, axis=-1, keepdims=True) + RMS_EPS)
        dx, dg_rows = _rms_backward(dh, xv, r, g_ref[...])
        dx_ref[...] = res_ref[...] + dx
        dg_ref[...] += jnp.sum(dg_rows, axis=0, keepdims=True)

    row = lambda n: pl.BlockSpec((tm, n), lambda i: (i, 0))
    vec = pl.BlockSpec((1, D_MODEL), lambda i: (0, 0))
    any_space = pl.BlockSpec(memory_space=pl.ANY)
    n_x = len(exchange)
    grid = (S // tm,)
    res = pl.pallas_call(
        _with_swap(body, grid, len(cols) + 4, 2, n_x, rider=_ChipExchange), name=f"inproj_bwd_l{layer}", grid=grid,
        in_specs=[row(n) for _, n in cols] + [_layer_weight_spec(D_MODEL, IN_WIDTH, layer), row(D_MODEL), vec,
                                              row(D_MODEL)] + [any_space] * n_x,
        out_specs=[row(D_MODEL), vec] + [any_space] * n_x,
        out_shape=[jax.ShapeDtypeStruct((S, D_MODEL), F32), jax.ShapeDtypeStruct((1, D_MODEL), F32)]
        + _ChipExchange.landing(exchange),
        scratch_shapes=_ChipExchange.semaphores(n_x) if n_x else [],
        compiler_params=_params(),
    )(*[p for p, _ in pieces], w_in, x, g, dx_res, *exchange)
    return res[0], res[1], res[2:]


class _SiblingSwap:
    def __init__(self, mine, theirs, send, recv):
        x, y, c, _ = _place()
        self.copies = [_remote(m.at[1 - c], t, send, recv, a, (x, y, 1 - c))
                       for a, (m, t) in enumerate(zip(mine, theirs))]

    def start(self):
        for cp in self.copies:
            cp.start()

    def wait(self):
        for cp in self.copies:
            cp.wait()


def _with_swap(body, grid, n_in, n_out, n_swap, rider=_SiblingSwap):
    if not n_swap:
        return lambda *refs: body(refs[:n_in], refs[n_in:])

    def riding(*refs):
        ins, mine = refs[:n_in], refs[n_in:n_in + n_swap]
        outs, theirs = refs[n_in + n_swap:n_in + n_swap + n_out], refs[n_in + n_swap + n_out:n_in + 2 * n_swap + n_out]
        swap = rider(mine, theirs, *refs[n_in + 2 * n_swap + n_out:])
        step = [pl.program_id(d) for d in range(len(grid))]
        first, last = step[0] == 0, step[0] == grid[0] - 1
        for d in range(1, len(grid)):
            first, last = jnp.logical_and(first, step[d] == 0), jnp.logical_and(last, step[d] == grid[d] - 1)
        pl.when(first)(swap.start)
        body(ins, outs)
        pl.when(last)(swap.wait)

    return riding


def _swap_specs(swap):
    any_space = pl.BlockSpec(memory_space=pl.ANY)
    shapes = [jax.ShapeDtypeStruct(d.shape[1:], d.dtype) for d in swap]
    sems = [pltpu.SemaphoreType.DMA((len(swap),)), pltpu.SemaphoreType.DMA((len(swap),))] if swap else []
    return [any_space] * len(swap), shapes, sems


def _wgrad(a, b, name, layer, into=None, col0=0, n_total=None, swap=()):
    S, M = a.shape
    N = b.shape[1]
    n_total = N if n_total is None else n_total
    tk = min(2048, S)
    tn = max(t for t in range(LANES, min(N, 1280) + 1, LANES) if N % t == 0 and col0 % t == 0)
    grid = (N // tn, S // tk)

    def body(ins, outs):
        prod = _tn(ins[0][...].astype(BF16), ins[1][...].astype(BF16))

        @pl.when(pl.program_id(1) == 0)
        def _():
            outs[0][...] = prod

        @pl.when(pl.program_id(1) > 0)
        def _():
            outs[0][...] += prod

        @pl.when(pl.program_id(1) == grid[1] - 1)
        def _():
            outs[1][...] = outs[0][...].astype(BF16)

    in_specs = [pl.BlockSpec((tk, M), lambda j, k: (k, 0)), pl.BlockSpec((tk, tn), lambda j, k: (k, j))]
    args = [a, b]
    aliases = {}
    if into is not None:
        in_specs += [pl.BlockSpec(memory_space=pl.ANY)] * 2
        args += list(into)
        aliases = {2: 0, 3: 1}
    swap_specs, swap_shapes, swap_sems = _swap_specs(swap)
    out_spec = pl.BlockSpec((None, M, tn), lambda j, k: (layer, 0, col0 // tn + j))
    res = pl.pallas_call(
        _with_swap(body, grid, len(args), 2, len(swap)), name=name, grid=grid,
        in_specs=in_specs + swap_specs,
        out_specs=[out_spec, out_spec] + swap_specs,
        out_shape=[jax.ShapeDtypeStruct((2, M, n_total), F32), jax.ShapeDtypeStruct((2, M, n_total), BF16)]
        + swap_shapes,
        input_output_aliases=aliases,
        scratch_shapes=swap_sems,
        compiler_params=_params(),
    )(*args, *swap)
    return ((res[0], res[1]), res[2:]) if swap else (res[0], res[1])


def _pool_wgrad(pooled, dmixed, layer, swap=()):
    S = pooled.shape[0]
    tk = min(8192, S)
    grid = (4, S // tk)

    def body(ins, outs):
        prod = _tn(ins[0][...], ins[1][...])

        @pl.when(pl.program_id(1) == 0)
        def _():
            outs[0][...] = prod

        @pl.when(pl.program_id(1) > 0)
        def _():
            outs[0][...] += prod

    blk = pl.BlockSpec((tk, POOL_GROUP), lambda g, k: (k, g))
    swap_specs, swap_shapes, swap_sems = _swap_specs(swap)
    res = pl.pallas_call(
        _with_swap(body, grid, 2, 1, len(swap)), name=f"pool_wgrad_l{layer}", grid=grid,
        in_specs=[blk, blk] + swap_specs,
        out_specs=[pl.BlockSpec((None, POOL_GROUP, POOL_GROUP), lambda g, k: (g, 0, 0))] + swap_specs,
        out_shape=[jax.ShapeDtypeStruct((4, POOL_GROUP, POOL_GROUP), F32)] + swap_shapes,
        scratch_shapes=swap_sems,
        compiler_params=_params(),
    )(pooled, dmixed, *swap)
    return (res[0], res[1:]) if swap else res[0]


def _local_step(x, target, norm_g, b_gate, pool_w, pool_scale, final_g, weights, pos):
    n_layers = norm_g.shape[0]
    saved = []
    for l in range(n_layers):
        g = norm_g[l][None]
        bg = b_gate[l][None]
        sc = pool_scale[l][None]
        if l == 0:
            (u, zp, q, k, v, za, gl, h), (w_in, w_pu, w_au, w_out) = _rms_inproj(x, g, None, l, gather=weights)
        else:
            (u, zp, q, k, v, za, gl, h), _ = _rms_inproj(x, g, w_in, l)
        if l == 0:
            o, carry, (w_in,) = _attn_fwd(q, k, v, l, gather=(w_in,))
        else:
            o, carry, _ = _attn_fwd(q, k, v, l)
        saved.append((x, g, bg, sc, u, zp, q, k, v, za, gl, h, o, carry))
        if l < n_layers - 1:
            x = _post_fwd(x, u, zp, o, za, gl, bg, pool_w[l], sc, w_pu, w_au, w_out, l)
        else:
            loss, dx, d_final_g = _post_fwd(x, u, zp, o, za, gl, bg, pool_w[l], sc, w_pu, w_au, w_out, l,
                                            head=(final_g[None], target))

    small = [None] * n_layers
    dw_in = dw_out = dw_pu = dw_au = None
    for l in reversed(range(n_layers)):
        x_in, g, bg, sc, u, zp, q, k, v, za, gl, h, o, carry = saved[l]
        (duz, do, dzg, dsc, dbg, merged, dup, dua, y_pool, y_attn, pooled, dmixed) = _post_bwd(
            dx, u, zp, o, za, gl, bg, pool_w[l], sc, w_pu, w_au, w_out, l)
        dq, dk, dv = _attn_bwd(q, k, v, carry, do, l)
        pieces = [(duz, C_U), (dq, C_Q), (dk, C_K), (dv, C_V), (dzg, C_ZA)]
        for p, c0 in pieces:
            dw_in = _wgrad(h, p, f"wgrad_in_l{l}_c{c0}", l, into=dw_in, col0=c0, n_total=IN_WIDTH)
        if l > 0:
            dw_out = _wgrad(merged, dx, f"wgrad_out_l{l}", l, into=dw_out)
        else:
            dw_out, (other_in,) = _wgrad(merged, dx, f"wgrad_out_l{l}", l, into=dw_out, swap=(dw_in[1],))
        dw_pu = _wgrad(y_pool, dup, f"wgrad_pu_l{l}", l, into=dw_pu)
        dw_au = _wgrad(y_attn, dua, f"wgrad_au_l{l}", l, into=dw_au)
        if l > 0:
            dpw = _pool_wgrad(pooled, dmixed, l)
        else:
            dpw, (other_pu, other_au, other_out) = _pool_wgrad(pooled, dmixed, l,
                                                               swap=(dw_pu[1], dw_au[1], dw_out[1]))
        if l > 0:
            dx, dg, _ = _inproj_bwd(pieces, w_in, x_in, g, dx, l)
        else:
            dws = (dw_in[0], dw_pu[0], dw_au[0], dw_out[0])
            others = (other_in, other_pu, other_au, other_out)
            pair = [_pair_sum(d, t, pos, f"grad_pair_sum_{n}") for d, t, n in zip(dws, others, SHARDED_NAMES)]
            dx, dg, landed = _inproj_bwd(pieces, w_in, x_in, g, dx, l, exchange=pair)
        small[l] = (dg[0], dbg[0], dpw, dsc[0])
    small = [jnp.stack([small[l][i] for l in range(n_layers)]) for i in range(4)]
    return loss[0, 0], dx, d_final_g[0], small, list(zip(dws, others)), landed


SHARDED = ((2, 1280), (2, 256), (2, 256), (1, 256))
SHARDED_NAMES = ("w_in", "w_pool_up", "w_attn_up", "w_out")
ANY = pl.BlockSpec(memory_space=pl.ANY)


def _part(ref, s, axis, width):
    sl = pl.ds(pl.multiple_of(s * width, width), width)
    return ref.at[:, sl] if axis == 2 else ref.at[sl, :]


def _place():
    x, y, c = lax.axis_index("x"), lax.axis_index("y"), lax.axis_index("c")
    return x, y, c, 2 * x + y


def _other_chip(x, y, m):
    px = 1 - x if m & 2 else x
    py = 1 - y if m & 1 else y
    return px, py, 2 * px + py


def _remote(src, dst, send, recv, k, to):
    return pltpu.make_async_remote_copy(src_ref=src, dst_ref=dst, send_sem=send.at[k], recv_sem=recv.at[k],
                                        device_id=to, device_id_type=MESH)


def _part_spec(tr, rows_s, cols_s, axis, width, lead):
    if axis == 2:
        return pl.BlockSpec((None, tr, width), lambda *a: (lead(a), a[-2], a[-1][1]))
    return pl.BlockSpec((None, tr, cols_s), lambda *a: (lead(a), a[-1][1] * (rows_s // tr) + a[-2], 0))


def _cast_into_place(w, pos, axis, width, name):
    L, Rs, Cs = w.shape
    tr = min(256, Rs)
    shape = [L, Rs, Cs]
    shape[axis] *= N_CHIPS

    def body(pos_ref, w_ref, o_ref):
        o_ref[...] = w_ref[...].astype(BF16)

    return pl.pallas_call(
        body, name=name,
        grid_spec=pltpu.PrefetchScalarGridSpec(
            num_scalar_prefetch=1, grid=(L, Rs // tr),
            in_specs=[pl.BlockSpec((None, tr, Cs), lambda l, i, pos: (l, i, 0))],
            out_specs=_part_spec(tr, Rs, Cs, axis, width, lambda a: a[0])),
        out_shape=jax.ShapeDtypeStruct(tuple(shape), BF16),
        compiler_params=_params(),
    )(pos, w)


def _w_in_half(layer):
    def piece(refs, who, shard):
        rows = pl.ds(pl.multiple_of(who * (D_MODEL // 2), D_MODEL // 2), D_MODEL // 2)
        return refs[0].at[layer, rows, pl.ds(pl.multiple_of(shard * SHARDED[0][1], SHARDED[0][1]), SHARDED[0][1])]
    return piece


def _whole_layer(a):
    def piece(refs, who, shard):
        return _part(refs[a].at[who], shard, *SHARDED[a])
    return piece


FIRST_PIECES = (_w_in_half(0),)
LATER_PIECES = (_whole_layer(1), _whole_layer(2), _whole_layer(3))
LAST_PIECES = (_w_in_half(1),)


class _Gather:
    def __init__(self, pieces, refs, send, recv):
        self.pieces, self.refs, self.send, self.recv = pieces, refs, send, recv
        self.x, self.y, self.c, s = _place()
        self.first = []
        for u, piece in enumerate(pieces):
            for m in (1, 2, 3):
                px, py, _ = _other_chip(self.x, self.y, m)
                own = piece(refs, self.c, s)
                self.first.append(_remote(own, own, send, recv, 3 * u + m - 1, (px, py, self.c)))

    def start(self):
        for cp in self.first:
            cp.start()

    def _landed(self, u, m, who):
        _, _, sp = _other_chip(self.x, self.y, m)
        return self.pieces[u](self.refs, who, sp)

    def _passed(self):
        n = len(self.pieces)
        return [_remote(self._landed(u, m, self.c), self._landed(u, m, self.c), self.send, self.recv,
                        3 * n + 3 * u + m - 1, (self.x, self.y, 1 - self.c)) for m in (1, 2, 3) for u in range(n)]

    def pass_on(self):
        me = (self.x, self.y, self.c)
        passed = iter(self._passed())
        for m in (1, 2, 3):
            for u in range(len(self.pieces)):
                got = self._landed(u, m, self.c)
                _remote(got, got, self.send, self.recv, 3 * u + m - 1, me).wait_recv()
                next(passed).start()

    def finish(self):
        n = len(self.pieces)
        for m in (1, 2, 3):
            for u in range(n):
                got = self._landed(u, m, 1 - self.c)
                _remote(got, got, self.send, self.recv, 3 * n + 3 * u + m - 1, (self.x, self.y, self.c)).wait_recv()
        for cp in self.first + self._passed():
            cp.wait_send()

    @staticmethod
    def semaphores(pieces):
        return [pltpu.SemaphoreType.DMA((6 * len(pieces),)), pltpu.SemaphoreType.DMA((6 * len(pieces),))]


def _gather_first(fulls):
    n = len(fulls)

    def body(*refs):
        gather = _Gather(FIRST_PIECES, refs[n:2 * n], *refs[2 * n:])
        gather.start()
        gather.pass_on()
        gather.finish()

    return pl.pallas_call(
        body, name="gather_first",
        in_specs=[ANY] * n, out_specs=[ANY] * n,
        out_shape=[jax.ShapeDtypeStruct(f.shape, f.dtype) for f in fulls],
        input_output_aliases={a: a for a in range(n)},
        scratch_shapes=_Gather.semaphores(FIRST_PIECES),
    )(*fulls)


def _pair_sum(dw, other, pos, name):
    _, R, C = dw.shape
    tr = 128 if C > 1024 else 256

    def body(pos_ref, a_ref, b_ref, ob_ref):
        ob_ref[...] = (a_ref[...] + b_ref[...].astype(F32)).astype(BF16)

    blk = pl.BlockSpec((tr, C), lambda i, pos: (i, 0))
    return pl.pallas_call(
        body, name=name,
        grid_spec=pltpu.PrefetchScalarGridSpec(
            num_scalar_prefetch=1, grid=(R // tr,),
            in_specs=[pl.BlockSpec((None, tr, C), lambda i, pos: (pos[0], i, 0)), blk],
            out_specs=blk),
        out_shape=jax.ShapeDtypeStruct((R, C), BF16),
        compiler_params=_params(),
    )(pos, dw, other)


class _ChipExchange:
    def __init__(self, mine, theirs, send, recv):
        x, y, c, _ = _place()
        self.copies = []
        for a, (axis, width) in enumerate(SHARDED):
            for m in (1, 2, 3):
                px, py, sp = _other_chip(x, y, m)
                self.copies.append(_remote(_part(mine[a], sp, axis, width), theirs[a].at[m - 1], send, recv,
                                           3 * a + m - 1, (px, py, c)))

    def start(self):
        for cp in self.copies:
            cp.start()

    def wait(self):
        for cp in self.copies:
            cp.wait()

    @staticmethod
    def landing(ps):
        shapes = []
        for p, (axis, width) in zip(ps, SHARDED):
            shape = [3] + list(p.shape)
            shape[axis] = width
            shapes.append(jax.ShapeDtypeStruct(tuple(shape), p.dtype))
        return shapes

    @staticmethod
    def semaphores(n):
        return [pltpu.SemaphoreType.DMA((3 * n,)), pltpu.SemaphoreType.DMA((3 * n,))]


def _shard_sum(dw, other, landed, pos, axis, width, name):
    _, Rs, Cs = landed.shape
    tr = min(256, Rs)
    own = _part_spec(tr, Rs, Cs, axis, width, lambda a: a[-1][0])
    sibling = _part_spec(tr, Rs, Cs, axis, width, lambda a: 0)

    def body(pos_ref, d_ref, t_ref, l_ref, o_ref):
        pair = d_ref[...] + t_ref[...].astype(F32)
        o_ref[...] = ((pair + l_ref[0].astype(F32)) + l_ref[1].astype(F32)) + l_ref[2].astype(F32)

    return pl.pallas_call(
        body, name=name,
        grid_spec=pltpu.PrefetchScalarGridSpec(
            num_scalar_prefetch=1, grid=(Rs // tr,),
            in_specs=[own, sibling, pl.BlockSpec((3, tr, Cs), lambda i, pos: (0, i, 0))],
            out_specs=pl.BlockSpec((None, tr, Cs), lambda i, pos: (pos[0], i, 0))),
        out_shape=jax.ShapeDtypeStruct((2, Rs, Cs), F32),
        compiler_params=_params(),
    )(pos, dw, other[None], landed)


def _final_exchange(gs, packed):
    n = len(gs)
    rows = packed.shape[0]
    half = rows // 2
    assert half % 8 == 0

    def body(*refs):
        small_ref = refs[n]
        outs, total_ref = refs[n + 1:2 * n + 1], refs[2 * n + 1]
        sib_ref, chips_ref, done_ref, send, recv, small_send, small_recv = refs[2 * n + 2:]
        x, y, c, s = _place()
        me, sibling = (x, y, c), (x, y, 1 - c)
        copies = [_remote(outs[a].at[c], outs[a].at[c], send, recv, a, sibling) for a in range(n)]
        for cp in copies:
            cp.start()

        def small(src, dst, k, to):
            return _remote(src, dst, small_send, small_recv, k, to)

        mine = small_ref.at[pl.ds(pl.multiple_of(c * half, 8), half)]
        theirs = small_ref.at[pl.ds(pl.multiple_of((1 - c) * half, 8), half)]
        to_sibling = small(theirs, sib_ref, 0, sibling)
        to_sibling.start()
        to_sibling.wait()
        chips_ref[s] = mine[...] + sib_ref[...]
        to_chips = []
        for m in (1, 2, 3):
            px, py, _ = _other_chip(x, y, m)
            to_chips.append(small(chips_ref.at[s], chips_ref.at[s], m, (px, py, c)))
            to_chips[-1].start()
        for m in (1, 2, 3):
            _, _, sp = _other_chip(x, y, m)
            small(chips_ref.at[sp], chips_ref.at[sp], m, me).wait_recv()
        done_ref[c] = ((chips_ref[0] + chips_ref[1]) + chips_ref[2]) + chips_ref[3]
        finished = small(done_ref.at[c], done_ref.at[c], 4, sibling)
        finished.start()
        small(done_ref.at[1 - c], done_ref.at[1 - c], 4, me).wait_recv()
        total_ref[:half] = done_ref[0]
        total_ref[half:] = done_ref[1]
        for cp in to_chips + [finished]:
            cp.wait_send()
        for a, cp in enumerate(copies):
            cp.wait_send()
            _remote(outs[a].at[1 - c], outs[a].at[1 - c], send, recv, a, me).wait_recv()

    vmem = pl.BlockSpec(memory_space=pltpu.VMEM)
    res = pl.pallas_call(
        body, name="final_exchange",
        in_specs=[ANY] * n + [vmem], out_specs=[ANY] * n + [vmem],
        out_shape=[jax.ShapeDtypeStruct(g.shape, g.dtype) for g in gs]
        + [jax.ShapeDtypeStruct(packed.shape, packed.dtype)],
        input_output_aliases={a: a for a in range(n)},
        scratch_shapes=[pltpu.VMEM((half, LANES), F32), pltpu.VMEM((N_CHIPS, half, LANES), F32),
                        pltpu.VMEM((2, half, LANES), F32),
                        pltpu.SemaphoreType.DMA((n,)), pltpu.SemaphoreType.DMA((n,)),
                        pltpu.SemaphoreType.DMA((5,)), pltpu.SemaphoreType.DMA((5,))],
        compiler_params=_params(),
    )(*gs, packed)
    return res[:n], res[n]


def _adamw(w, g, m, v, name):
    shape = w.shape
    C = shape[-1]
    flat = [t.reshape(-1, C) for t in (w, g, m, v)]
    R = flat[0].shape[0]
    tr = max(t for t in range(8, R + 1, 8) if R % t == 0 and t * C <= 384 * 1024)

    def body(w_ref, g_ref, m_ref, v_ref, d_ref, nm_ref, nv_ref):
        _adamw_update(w_ref, g_ref, m_ref, v_ref, d_ref, nm_ref, nv_ref)

    blk = pl.BlockSpec((tr, C), lambda i: (i, 0))
    out = jax.ShapeDtypeStruct((R, C), F32)
    res = pl.pallas_call(
        body, name=name, grid=(R // tr,),
        in_specs=[blk] * 4, out_specs=[blk] * 3, out_shape=[out] * 3,
        compiler_params=_params(),
    )(*flat)
    return [t.reshape(shape) for t in res]


def _adamw_update(w_ref, g_ref, m_ref, v_ref, d_ref, nm_ref, nv_ref):
    gv = g_ref[...]
    nm = ADAM_B1 * m_ref[...] + (1.0 - ADAM_B1) * gv
    nv = ADAM_B2 * v_ref[...] + (1.0 - ADAM_B2) * (gv * gv)
    m_hat = nm / (1.0 - ADAM_B1 ** ADAM_STEP)
    v_hat = nv / (1.0 - ADAM_B2 ** ADAM_STEP)
    d_ref[...] = -ADAM_LR * (m_hat / (jnp.sqrt(v_hat) + ADAM_EPS) + ADAM_WD * w_ref[...])
    nm_ref[...] = nm
    nv_ref[...] = nv


def _adamw_small(ws, gs, ms, vs):
    n = len(ws)
    flat = lambda ts: [t.reshape(-1, t.shape[-1]) for t in ts]

    def body(*refs):
        for a in range(n):
            _adamw_update(*[refs[k * n + a] for k in range(7)])

    vmem = pl.BlockSpec(memory_space=pltpu.VMEM)
    shapes = [jax.ShapeDtypeStruct(w.shape, F32) for w in flat(ws)]
    res = pl.pallas_call(
        body, name="adamw_small",
        in_specs=[vmem] * (4 * n), out_specs=[vmem] * (3 * n), out_shape=shapes * 3,
        compiler_params=_params(),
    )(*flat(ws), *flat(gs), *flat(ms), *flat(vs))
    return [[r.reshape(w.shape) for r, w in zip(res[k * n:(k + 1) * n], ws)] for k in range(3)]


SMALL_SHAPES = ((2, 1024), (2, 2048), (2, 4, 128, 128), (2, 512), (1024,))


def _pack_small(parts):
    return jnp.concatenate([p.reshape(-1, LANES) for p in parts], axis=0)


def _unpack_small(packed):
    out, row = [], 0
    for shape in SMALL_SHAPES:
        n = 1
        for d in shape:
            n *= d
        out.append(packed[row:row + n // LANES].reshape(shape))
        row += n // LANES
    return out


def kernel(x, norm_g, w_in, b_gate, pool_w, pool_scale, w_pool_up, w_attn_up, w_out, final_g, loss_target, m_norm_g, m_w_in, m_b_gate, m_pool_w, m_pool_scale, m_w_pool_up, m_w_attn_up, m_w_out, m_final_g, v_norm_g, v_w_in, v_b_gate, v_pool_w, v_pool_scale, v_w_pool_up, v_w_attn_up, v_w_out, v_final_g):
    _, _, c, s = _place()
    pos = jnp.stack([c, s]).astype(jnp.int32)
    names = SHARDED_NAMES

    weights = _gather_first([_cast_into_place(w, pos, axis, width, f"cast_{n}")
                             for w, (axis, width), n in zip((w_in, w_pool_up, w_attn_up, w_out), SHARDED, names)])
    loss_part, dx, d_final_g, small, pair, landed = _local_step(x[0], loss_target[0], norm_g, b_gate, pool_w,
                                                                pool_scale, final_g, weights, pos)
    mine = [_shard_sum(d, t, l, pos, axis, width, f"grad_shard_sum_{n}")
            for (d, t), l, (axis, width), n in zip(pair, landed, SHARDED, names)]
    (g_in, g_pu, g_au, g_out), summed = _final_exchange(
        mine, _pack_small(small + [d_final_g, jnp.broadcast_to(loss_part, (16, LANES))]))
    g_small = _unpack_small(summed)
    loss = summed[-16, 0]
    d_small, nm_small, nv_small = _adamw_small([norm_g, b_gate, pool_w, pool_scale, final_g], g_small,
                                               [m_norm_g, m_b_gate, m_pool_w, m_pool_scale, m_final_g],
                                               [v_norm_g, v_b_gate, v_pool_w, v_pool_scale, v_final_g])
    upd_in = _adamw(w_in, g_in, m_w_in, v_w_in, "adamw_w_in")
    upd_pu = _adamw(w_pool_up, g_pu, m_w_pool_up, v_w_pool_up, "adamw_w_pool_up")
    upd_au = _adamw(w_attn_up, g_au, m_w_attn_up, v_w_attn_up, "adamw_w_attn_up")
    upd_out = _adamw(w_out, g_out, m_w_out, v_w_out, "adamw_w_out")

    def ordered(sm, k):
        big = (upd_in[k], upd_pu[k], upd_au[k], upd_out[k]) if k is not None else (g_in, g_pu, g_au, g_out)
        return [sm[0], big[0], sm[1], sm[2], sm[3], big[1], big[2], big[3], sm[4]]

    return (loss, dx[None], *ordered(g_small, None), *ordered(d_small, 0), *ordered(nm_small, 1),
            *ordered(nv_small, 2))
```
